```python
import math
import jax
import jax.numpy as jnp
from jax import lax
import numpy as np

D_MODEL = 1024
BATCH = 8
SEQ = 4096
DEPTH = 1

GRID_W = 64
CTX_LEN = 256
MIX_WIDTH = D_MODEL
ATTN_WIDTH = D_MODEL // 2
HEAD_DIM = 64
N_Q_HEADS = ATTN_WIDTH // HEAD_DIM
N_KV_HEADS = 2
GQA_GROUP = N_Q_HEADS // N_KV_HEADS
WINDOW = 128
BLOCK = 128
CONV_WIDTH_CH = MIX_WIDTH - ATTN_WIDTH
CONV_KERNEL = 31
FFN_HIDDEN = 2816
FFN_CONV = 3
ROPE_BASE = 10000.0
ROPE_AXIS_DIM = HEAD_DIM // 2
NORM_EPS = 1e-6
NEG_INF = -1e30

Q_COLS = N_Q_HEADS * HEAD_DIM
KV_COLS = N_KV_HEADS * HEAD_DIM
GLU_COLS = 2 * CONV_WIDTH_CH
IN_COLS = Q_COLS + 2 * KV_COLS + GLU_COLS

kernel_name = "hybrid_window_gqa_conformer_convffn_dit_layer"


def rms_norm(x, w):
    xf = x.astype(jnp.float32)
    y = xf * lax.rsqrt(jnp.mean(xf * xf, axis=-1, keepdims=True) + NORM_EPS)
    return (y * w.astype(jnp.float32)).astype(x.dtype)


def layer_norm(x, w, b):
    xf = x.astype(jnp.float32)
    mu = jnp.mean(xf, axis=-1, keepdims=True)
    var = jnp.mean(jnp.square(xf - mu), axis=-1, keepdims=True)
    y = (xf - mu) * lax.rsqrt(var + NORM_EPS)
    return (y * w.astype(jnp.float32) + b.astype(jnp.float32)).astype(x.dtype)


def modulate(h, shift, scale):
    return h * (1 + scale) + shift


def depthwise_conv(u, w, b):
    width = w.shape[0]
    pad = (width - 1) // 2
    y = lax.conv_general_dilated(
        u, w[:, None, :].astype(u.dtype), window_strides=(1,),
        padding=[(pad, pad)], dimension_numbers=("NWC", "WIO", "NWC"),
        feature_group_count=u.shape[-1])
    return y + b.astype(u.dtype)


def axial_rope_tables(row, col):
    inv = ROPE_BASE ** (-jnp.arange(0, ROPE_AXIS_DIM, 2, dtype=jnp.float32) / ROPE_AXIS_DIM)
    ang_r = row.astype(jnp.float32)[:, None] * inv
    ang_c = col.astype(jnp.float32)[:, None] * inv
    ang = jnp.concatenate([ang_r, ang_r, ang_c, ang_c], axis=-1)
    return jnp.cos(ang), jnp.sin(ang)


def rotate_half(u):
    u1, u2 = jnp.split(u, 2, axis=-1)
    return jnp.concatenate([-u2, u1], axis=-1)


def apply_axial_rope(x, cos, sin):
    s, hd = cos.shape
    shape = (s,) + (1,) * (x.ndim - 3) + (hd,)
    cos = cos.reshape(shape).astype(x.dtype)
    sin = sin.reshape(shape).astype(x.dtype)
    xr, xc = jnp.split(x, 2, axis=-1)
    rotated = jnp.concatenate([rotate_half(xr), rotate_half(xc)], axis=-1)
    return x * cos + rotated * sin


def windowed_gqa_with_context(q, k, v, k_ctx, v_ctx, sink):
    b, s = q.shape[0], q.shape[1]
    nb = s // BLOCK
    n_ctx = k_ctx.shape[1]
    scale = 1.0 / math.sqrt(HEAD_DIM)
    k_pad = jnp.pad(k, ((0, 0), (BLOCK, BLOCK), (0, 0), (0, 0)))
    v_pad = jnp.pad(v, ((0, 0), (BLOCK, BLOCK), (0, 0), (0, 0)))
    q_blocks = jnp.moveaxis(q.reshape(b, nb, BLOCK, N_KV_HEADS, GQA_GROUP, HEAD_DIM), 1, 0)
    offs_q = jnp.arange(BLOCK)
    offs_k = jnp.arange(3 * BLOCK)
    sink_col = jnp.broadcast_to(sink.astype(jnp.float32)[None, :, :, None, None],
                                (b, N_KV_HEADS, GQA_GROUP, BLOCK, 1))

    def one_block(args):
        i, qb = args
        kb = lax.dynamic_slice_in_dim(k_pad, i * BLOCK, 3 * BLOCK, axis=1)
        vb = lax.dynamic_slice_in_dim(v_pad, i * BLOCK, 3 * BLOCK, axis=1)
        qpos = i * BLOCK + offs_q
        kpos = (i - 1) * BLOCK + offs_k
        valid = ((jnp.abs(qpos[:, None] - kpos[None, :]) <= WINDOW)
                 & (kpos[None, :] >= 0) & (kpos[None, :] < s))
        s_loc = jnp.einsum("bqhgd,bkhd->bhgqk", qb, kb).astype(jnp.float32) * scale
        s_loc = jnp.where(valid, s_loc, NEG_INF)
        s_ctx = jnp.einsum("bqhgd,bkhd->bhgqk", qb, k_ctx).astype(jnp.float32) * scale
        p = jax.nn.softmax(jnp.concatenate([sink_col, s_ctx, s_loc], axis=-1), axis=-1)
        p = p.astype(v.dtype)
        o = (jnp.einsum("bhgqk,bkhd->bqhgd", p[..., 1:1 + n_ctx], v_ctx)
             + jnp.einsum("bhgqk,bkhd->bqhgd", p[..., 1 + n_ctx:], vb))
        return o

    out = lax.map(one_block, (jnp.arange(nb), q_blocks))
    return jnp.moveaxis(out, 0, 1).reshape(b, s, N_KV_HEADS * GQA_GROUP * HEAD_DIM)


def _fwd_setup_inputs(seed: int = 0) -> dict:
    key = jax.random.key(seed)
    ks = jax.random.split(key, 24)
    f32 = jnp.float32

    def nrm(k, shape, scale):
        return jax.random.normal(k, shape, f32) * scale

    L = DEPTH
    return {
        "x": nrm(ks[0], (BATCH, SEQ, D_MODEL), 1.0),
        "c": nrm(ks[1], (BATCH, D_MODEL), 1.0),
        "ctx": nrm(ks[2], (BATCH, CTX_LEN, D_MODEL), 1.0),
        "c_ctx": nrm(ks[3], (D_MODEL,), 1.0),
        "w_mod": nrm(ks[4], (L, D_MODEL, 6 * D_MODEL), 0.5 * D_MODEL ** -0.5),
        "b_mod": nrm(ks[5], (L, 6 * D_MODEL), 0.02),
        "norm_mix_w": 1.0 + nrm(ks[6], (L, D_MODEL), 0.02),
        "w_in": nrm(ks[7], (L, D_MODEL, IN_COLS), D_MODEL ** -0.5),
        "q_norm_w": 1.0 + nrm(ks[8], (L, HEAD_DIM), 0.02),
        "k_norm_w": 1.0 + nrm(ks[9], (L, HEAD_DIM), 0.02),
        "sink_logit": nrm(ks[10], (L, N_Q_HEADS), 0.5),
        "conv_w": nrm(ks[11], (L, CONV_KERNEL, CONV_WIDTH_CH), CONV_KERNEL ** -0.5),
        "conv_b": nrm(ks[12], (L, CONV_WIDTH_CH), 0.02),
        "conv_norm_w": 1.0 + nrm(ks[13], (L, CONV_WIDTH_CH), 0.02),
        "conv_norm_b": nrm(ks[14], (L, CONV_WIDTH_CH), 0.02),
        "w_out": nrm(ks[15], (L, MIX_WIDTH, D_MODEL), MIX_WIDTH ** -0.5),
        "norm_ffn_w": 1.0 + nrm(ks[16], (L, D_MODEL), 0.02),
        "w_up": nrm(ks[17], (L, D_MODEL, 2 * FFN_HIDDEN), D_MODEL ** -0.5),
        "ffn_conv_w": nrm(ks[18], (L, FFN_CONV, 2 * FFN_HIDDEN), FFN_CONV ** -0.5),
        "ffn_conv_b": nrm(ks[19], (L, 2 * FFN_HIDDEN), 0.02),
        "w_down": nrm(ks[20], (L, FFN_HIDDEN, D_MODEL), FFN_HIDDEN ** -0.5),
    }


def _fwd_reference(x, c, ctx, c_ctx, w_mod, b_mod, norm_mix_w, w_in, q_norm_w, k_norm_w,
              sink_logit, conv_w, conv_b, conv_norm_w, conv_norm_b, w_out,
              norm_ffn_w, w_up, ffn_conv_w, ffn_conv_b, w_down):
    b, s, d = x.shape
    rows = s // GRID_W
    row = jnp.broadcast_to(jnp.arange(rows)[:, None], (rows, GRID_W)).reshape(s)
    col = jnp.broadcast_to(jnp.arange(GRID_W)[None, :], (rows, GRID_W)).reshape(s)
    cos, sin = axial_rope_tables(row, col)

    for l in range(DEPTH):
        mod = jax.nn.silu(c) @ w_mod[l] + b_mod[l]
        sh1, sc1, g1, sh2, sc2, g2 = [m[:, None, :] for m in jnp.split(mod, 6, axis=-1)]
        mod_ctx = jax.nn.silu(c_ctx) @ w_mod[l] + b_mod[l]
        sh1c, sc1c = mod_ctx[:d], mod_ctx[d:2 * d]

        h = modulate(rms_norm(x, norm_mix_w[l]), sh1, sc1)
        hc = modulate(rms_norm(ctx, norm_mix_w[l]), sh1c, sc1c)
        proj = h @ w_in[l]
        q = proj[..., :Q_COLS].reshape(b, s, N_KV_HEADS, GQA_GROUP, HEAD_DIM)
        k = proj[..., Q_COLS:Q_COLS + KV_COLS].reshape(b, s, N_KV_HEADS, HEAD_DIM)
        v = proj[..., Q_COLS + KV_COLS:Q_COLS + 2 * KV_COLS].reshape(b, s, N_KV_HEADS, HEAD_DIM)
        glu_in = proj[..., Q_COLS + 2 * KV_COLS:]

        kv_ctx = hc @ w_in[l][:, Q_COLS:Q_COLS + 2 * KV_COLS]
        n_ctx = ctx.shape[1]
        k_ctx = rms_norm(kv_ctx[..., :KV_COLS].reshape(b, n_ctx, N_KV_HEADS, HEAD_DIM), k_norm_w[l])
        v_ctx = kv_ctx[..., KV_COLS:].reshape(b, n_ctx, N_KV_HEADS, HEAD_DIM)

        q = apply_axial_rope(rms_norm(q, q_norm_w[l]), cos, sin)
        k = apply_axial_rope(rms_norm(k, k_norm_w[l]), cos, sin)
        sink = sink_logit[l].reshape(N_KV_HEADS, GQA_GROUP)
        attn_out = windowed_gqa_with_context(q, k, v, k_ctx, v_ctx, sink)

        ga, gb = jnp.split(glu_in, 2, axis=-1)
        u = ga * jax.nn.sigmoid(gb)
        u = depthwise_conv(u, conv_w[l], conv_b[l])
        u = jax.nn.silu(layer_norm(u, conv_norm_w[l], conv_norm_b[l]))

        mix = jnp.concatenate([attn_out, u], axis=-1) @ w_out[l]
        x = x + g1 * mix

        h2 = modulate(rms_norm(x, norm_ffn_w[l]), sh2, sc2)
        up = depthwise_conv(h2 @ w_up[l], ffn_conv_w[l], ffn_conv_b[l])
        gate, val = jnp.split(up, 2, axis=-1)
        x = x + g2 * ((jax.nn.silu(gate) * val) @ w_down[l])
    return x


import jax as _jax
import jax.numpy as _jnp

TWIN_FORMAT = 'train_step'
FWD_PARAMS = ['x', 'c', 'ctx', 'c_ctx', 'w_mod', 'b_mod', 'norm_mix_w', 'w_in', 'q_norm_w', 'k_norm_w', 'sink_logit', 'conv_w', 'conv_b', 'conv_norm_w', 'conv_norm_b', 'w_out', 'norm_ffn_w', 'w_up', 'ffn_conv_w', 'ffn_conv_b', 'w_down']
TWIN_WEIGHTS = ['c_ctx', 'w_mod', 'b_mod', 'norm_mix_w', 'w_in', 'q_norm_w', 'k_norm_w', 'sink_logit', 'conv_w', 'conv_b', 'conv_norm_w', 'conv_norm_b', 'w_out', 'norm_ffn_w', 'w_up', 'ffn_conv_w', 'ffn_conv_b', 'w_down']
TWIN_DIFF_INPUT = 'x'
TWIN_INPUTS = ['x', 'c', 'ctx', 'c_ctx', 'w_mod', 'b_mod', 'norm_mix_w', 'w_in', 'q_norm_w', 'k_norm_w', 'sink_logit', 'conv_w', 'conv_b', 'conv_norm_w', 'conv_norm_b', 'w_out', 'norm_ffn_w', 'w_up', 'ffn_conv_w', 'ffn_conv_b', 'w_down', 'loss_target', 'm_c_ctx', 'm_w_mod', 'm_b_mod', 'm_norm_mix_w', 'm_w_in', 'm_q_norm_w', 'm_k_norm_w', 'm_sink_logit', 'm_conv_w', 'm_conv_b', 'm_conv_norm_w', 'm_conv_norm_b', 'm_w_out', 'm_norm_ffn_w', 'm_w_up', 'm_ffn_conv_w', 'm_ffn_conv_b', 'm_w_down', 'v_c_ctx', 'v_w_mod', 'v_b_mod', 'v_norm_mix_w', 'v_w_in', 'v_q_norm_w', 'v_k_norm_w', 'v_sink_logit', 'v_conv_w', 'v_conv_b', 'v_conv_norm_w', 'v_conv_norm_b', 'v_w_out', 'v_norm_ffn_w', 'v_w_up', 'v_ffn_conv_w', 'v_ffn_conv_b', 'v_w_down']
TWIN_OUTPUTS = ['loss', 'grad_x', 'grad_c_ctx', 'grad_w_mod', 'grad_b_mod', 'grad_norm_mix_w', 'grad_w_in', 'grad_q_norm_w', 'grad_k_norm_w', 'grad_sink_logit', 'grad_conv_w', 'grad_conv_b', 'grad_conv_norm_w', 'grad_conv_norm_b', 'grad_w_out', 'grad_norm_ffn_w', 'grad_w_up', 'grad_ffn_conv_w', 'grad_ffn_conv_b', 'grad_w_down', 'delta_c_ctx', 'delta_w_mod', 'delta_b_mod', 'delta_norm_mix_w', 'delta_w_in', 'delta_q_norm_w', 'delta_k_norm_w', 'delta_sink_logit', 'delta_conv_w', 'delta_conv_b', 'delta_conv_norm_w', 'delta_conv_norm_b', 'delta_w_out', 'delta_norm_ffn_w', 'delta_w_up', 'delta_ffn_conv_w', 'delta_ffn_conv_b', 'delta_w_down', 'new_m_c_ctx', 'new_m_w_mod', 'new_m_b_mod', 'new_m_norm_mix_w', 'new_m_w_in', 'new_m_q_norm_w', 'new_m_k_norm_w', 'new_m_sink_logit', 'new_m_conv_w', 'new_m_conv_b', 'new_m_conv_norm_w', 'new_m_conv_norm_b', 'new_m_w_out', 'new_m_norm_ffn_w', 'new_m_w_up', 'new_m_ffn_conv_w', 'new_m_ffn_conv_b', 'new_m_w_down', 'new_v_c_ctx', 'new_v_w_mod', 'new_v_b_mod', 'new_v_norm_mix_w', 'new_v_w_in', 'new_v_q_norm_w', 'new_v_k_norm_w', 'new_v_sink_logit', 'new_v_conv_w', 'new_v_conv_b', 'new_v_conv_norm_w', 'new_v_conv_norm_b', 'new_v_w_out', 'new_v_norm_ffn_w', 'new_v_w_up', 'new_v_ffn_conv_w', 'new_v_ffn_conv_b', 'new_v_w_down']
TWIN_LEAF_KINDS = {'loss': 'loss', 'grad_x': 'grad_x', 'grad_c_ctx': 'grad_w', 'grad_w_mod': 'grad_w', 'grad_b_mod': 'grad_w', 'grad_norm_mix_w': 'grad_w', 'grad_w_in': 'grad_w', 'grad_q_norm_w': 'grad_w', 'grad_k_norm_w': 'grad_w', 'grad_sink_logit': 'grad_w', 'grad_conv_w': 'grad_w', 'grad_conv_b': 'grad_w', 'grad_conv_norm_w': 'grad_w', 'grad_conv_norm_b': 'grad_w', 'grad_w_out': 'grad_w', 'grad_norm_ffn_w': 'grad_w', 'grad_w_up': 'grad_w', 'grad_ffn_conv_w': 'grad_w', 'grad_ffn_conv_b': 'grad_w', 'grad_w_down': 'grad_w', 'delta_c_ctx': 'delta_w', 'delta_w_mod': 'delta_w', 'delta_b_mod': 'delta_w', 'delta_norm_mix_w': 'delta_w', 'delta_w_in': 'delta_w', 'delta_q_norm_w': 'delta_w', 'delta_k_norm_w': 'delta_w', 'delta_sink_logit': 'delta_w', 'delta_conv_w': 'delta_w', 'delta_conv_b': 'delta_w', 'delta_conv_norm_w': 'delta_w', 'delta_conv_norm_b': 'delta_w', 'delta_w_out': 'delta_w', 'delta_norm_ffn_w': 'delta_w', 'delta_w_up': 'delta_w', 'delta_ffn_conv_w': 'delta_w', 'delta_ffn_conv_b': 'delta_w', 'delta_w_down': 'delta_w', 'new_m_c_ctx': 'new_m', 'new_m_w_mod': 'new_m', 'new_m_b_mod': 'new_m', 'new_m_norm_mix_w': 'new_m', 'new_m_w_in': 'new_m', 'new_m_q_norm_w': 'new_m', 'new_m_k_norm_w': 'new_m', 'new_m_sink_logit': 'new_m', 'new_m_conv_w': 'new_m', 'new_m_conv_b': 'new_m', 'new_m_conv_norm_w': 'new_m', 'new_m_conv_norm_b': 'new_m', 'new_m_w_out': 'new_m', 'new_m_norm_ffn_w': 'new_m', 'new_m_w_up': 'new_m', 'new_m_ffn_conv_w': 'new_m', 'new_m_ffn_conv_b': 'new_m', 'new_m_w_down': 'new_m', 'new_v_c_ctx': 'new_v', 'new_v_w_mod': 'new_v', 'new_v_b_mod': 'new_v', 'new_v_norm_mix_w': 'new_v', 'new_v_w_in': 'new_v', 'new_v_q_norm_w': 'new_v', 'new_v_k_norm_w': 'new_v', 'new_v_sink_logit': 'new_v', 'new_v_conv_w': 'new_v', 'new_v_conv_b': 'new_v', 'new_v_conv_norm_w': 'new_v', 'new_v_conv_norm_b': 'new_v', 'new_v_w_out': 'new_v', 'new_v_norm_ffn_w': 'new_v', 'new_v_w_up': 'new_v', 'new_v_ffn_conv_w': 'new_v', 'new_v_ffn_conv_b': 'new_v', 'new_v_w_down': 'new_v'}


def _forward(args):
    return _fwd_reference(*[args[k] for k in FWD_PARAMS])


def _output_shape():
    out = _jax.eval_shape(lambda: _forward(_fwd_setup_inputs(0)))
    return out.shape, out.dtype

N_MICROBATCH = 1
ADAM_LR = 0.001
ADAM_B1 = 0.9
ADAM_B2 = 0.999
ADAM_EPS = 1e-08
ADAM_WD = 0.01
ADAM_STEP = 10
PER_EXAMPLE_BATCH_AXIS = {'x': 0, 'c': 0, 'ctx': 0, 'loss_target': 0}
SHARED_INPUTS = []
_WEIGHT_DTYPES = {'c_ctx': _jnp.float32, 'w_mod': _jnp.float32, 'b_mod': _jnp.float32, 'norm_mix_w': _jnp.float32, 'w_in': _jnp.float32, 'q_norm_w': _jnp.float32, 'k_norm_w': _jnp.float32, 'sink_logit': _jnp.float32, 'conv_w': _jnp.float32, 'conv_b': _jnp.float32, 'conv_norm_w': _jnp.float32, 'conv_norm_b': _jnp.float32, 'w_out': _jnp.float32, 'norm_ffn_w': _jnp.float32, 'w_up': _jnp.float32, 'ffn_conv_w': _jnp.float32, 'ffn_conv_b': _jnp.float32, 'w_down': _jnp.float32}
MOMENT_SCALE = {'c_ctx': 1.440197e-01, 'w_mod': 8.727709e-01, 'b_mod': 1.931587e+00, 'norm_mix_w': 4.555458e-02, 'w_in': 1.237728e-01, 'q_norm_w': 1.162980e-01, 'k_norm_w': 1.144289e-01, 'sink_logit': 1.635729e-02, 'conv_w': 1.032524e-01, 'conv_b': 6.952699e-01, 'conv_norm_w': 1.405866e+00, 'conv_norm_b': 9.369959e-01, 'w_out': 1.520979e-01, 'norm_ffn_w': 3.431085e+00, 'w_up': 9.799252e-02, 'ffn_conv_w': 5.029533e-01, 'ffn_conv_b': 4.111150e-01, 'w_down': 7.811159e-02}


def _to_microbatches(a, axis):
    t = _jnp.moveaxis(a, axis, 0)
    t = t.reshape((N_MICROBATCH, t.shape[0] // N_MICROBATCH) + t.shape[1:])
    return _jnp.moveaxis(t, 1, axis + 1)


def setup_inputs(seed: int = 0) -> dict:
    inp = _fwd_setup_inputs(seed)
    key = _jax.random.fold_in(_jax.random.key(seed), 7919)
    shape, _ = _output_shape()
    out = dict(inp)
    out["loss_target"] = _jax.random.normal(_jax.random.fold_in(key, 0), shape, _jnp.float32)
    for i, name in enumerate(TWIN_WEIGHTS):
        w = inp[name].astype(_jnp.float32)
        if MOMENT_SCALE is None:
            s = _jnp.sqrt(_jnp.mean(_jnp.square(w)) + 1e-30)
        else:
            s = MOMENT_SCALE[name]
        km, kv = _jax.random.split(_jax.random.fold_in(key, i + 1))
        out[name] = w
        out["m_" + name] = s * _jax.random.normal(km, w.shape, _jnp.float32)
        out["v_" + name] = (s * s) * _jax.random.uniform(kv, w.shape, _jnp.float32, 0.5, 1.5)
    if N_MICROBATCH > 1:
        for name, axis in PER_EXAMPLE_BATCH_AXIS.items():
            out[name] = _to_microbatches(out[name], axis)
    return {'x': out['x'], 'c': out['c'], 'ctx': out['ctx'], 'c_ctx': out['c_ctx'], 'w_mod': out['w_mod'], 'b_mod': out['b_mod'], 'norm_mix_w': out['norm_mix_w'], 'w_in': out['w_in'], 'q_norm_w': out['q_norm_w'], 'k_norm_w': out['k_norm_w'], 'sink_logit': out['sink_logit'], 'conv_w': out['conv_w'], 'conv_b': out['conv_b'], 'conv_norm_w': out['conv_norm_w'], 'conv_norm_b': out['conv_norm_b'], 'w_out': out['w_out'], 'norm_ffn_w': out['norm_ffn_w'], 'w_up': out['w_up'], 'ffn_conv_w': out['ffn_conv_w'], 'ffn_conv_b': out['ffn_conv_b'], 'w_down': out['w_down'], 'loss_target': out['loss_target'], 'm_c_ctx': out['m_c_ctx'], 'm_w_mod': out['m_w_mod'], 'm_b_mod': out['m_b_mod'], 'm_norm_mix_w': out['m_norm_mix_w'], 'm_w_in': out['m_w_in'], 'm_q_norm_w': out['m_q_norm_w'], 'm_k_norm_w': out['m_k_norm_w'], 'm_sink_logit': out['m_sink_logit'], 'm_conv_w': out['m_conv_w'], 'm_conv_b': out['m_conv_b'], 'm_conv_norm_w': out['m_conv_norm_w'], 'm_conv_norm_b': out['m_conv_norm_b'], 'm_w_out': out['m_w_out'], 'm_norm_ffn_w': out['m_norm_ffn_w'], 'm_w_up': out['m_w_up'], 'm_ffn_conv_w': out['m_ffn_conv_w'], 'm_ffn_conv_b': out['m_ffn_conv_b'], 'm_w_down': out['m_w_down'], 'v_c_ctx': out['v_c_ctx'], 'v_w_mod': out['v_w_mod'], 'v_b_mod': out['v_b_mod'], 'v_norm_mix_w': out['v_norm_mix_w'], 'v_w_in': out['v_w_in'], 'v_q_norm_w': out['v_q_norm_w'], 'v_k_norm_w': out['v_k_norm_w'], 'v_sink_logit': out['v_sink_logit'], 'v_conv_w': out['v_conv_w'], 'v_conv_b': out['v_conv_b'], 'v_conv_norm_w': out['v_conv_norm_w'], 'v_conv_norm_b': out['v_conv_norm_b'], 'v_w_out': out['v_w_out'], 'v_norm_ffn_w': out['v_norm_ffn_w'], 'v_w_up': out['v_w_up'], 'v_ffn_conv_w': out['v_ffn_conv_w'], 'v_ffn_conv_b': out['v_ffn_conv_b'], 'v_w_down': out['v_w_down']}


def _loss(weights, diff, rest, loss_target):
    with _jax.named_scope("forward"):
        args = {**rest, TWIN_DIFF_INPUT: diff, **{k: w.astype(_WEIGHT_DTYPES[k]) for k, w in weights.items()}}
        y = _forward(args)
    with _jax.named_scope("loss_head"):
        err = _jnp.square(y.astype(_jnp.float32) - loss_target)
        return 0.5 * _jnp.sum(_jnp.mean(err, axis=-1)) if err.ndim else 0.5 * err


def _adamw(w, g, m, v):
    m = ADAM_B1 * m + (1.0 - ADAM_B1) * g
    v = ADAM_B2 * v + (1.0 - ADAM_B2) * _jnp.square(g)
    m_hat = m / (1.0 - ADAM_B1 ** ADAM_STEP)
    v_hat = v / (1.0 - ADAM_B2 ** ADAM_STEP)
    delta = -ADAM_LR * (m_hat / (_jnp.sqrt(v_hat) + ADAM_EPS) + ADAM_WD * w)
    return delta, m, v


def reference(x, c, ctx, c_ctx, w_mod, b_mod, norm_mix_w, w_in, q_norm_w, k_norm_w, sink_logit, conv_w, conv_b, conv_norm_w, conv_norm_b, w_out, norm_ffn_w, w_up, ffn_conv_w, ffn_conv_b, w_down, loss_target, m_c_ctx, m_w_mod, m_b_mod, m_norm_mix_w, m_w_in, m_q_norm_w, m_k_norm_w, m_sink_logit, m_conv_w, m_conv_b, m_conv_norm_w, m_conv_norm_b, m_w_out, m_norm_ffn_w, m_w_up, m_ffn_conv_w, m_ffn_conv_b, m_w_down, v_c_ctx, v_w_mod, v_b_mod, v_norm_mix_w, v_w_in, v_q_norm_w, v_k_norm_w, v_sink_logit, v_conv_w, v_conv_b, v_conv_norm_w, v_conv_norm_b, v_w_out, v_norm_ffn_w, v_w_up, v_ffn_conv_w, v_ffn_conv_b, v_w_down):
    given = dict(x=x, c=c, ctx=ctx, c_ctx=c_ctx, w_mod=w_mod, b_mod=b_mod, norm_mix_w=norm_mix_w, w_in=w_in, q_norm_w=q_norm_w, k_norm_w=k_norm_w, sink_logit=sink_logit, conv_w=conv_w, conv_b=conv_b, conv_norm_w=conv_norm_w, conv_norm_b=conv_norm_b, w_out=w_out, norm_ffn_w=norm_ffn_w, w_up=w_up, ffn_conv_w=ffn_conv_w, ffn_conv_b=ffn_conv_b, w_down=w_down, loss_target=loss_target, m_c_ctx=m_c_ctx, m_w_mod=m_w_mod, m_b_mod=m_b_mod, m_norm_mix_w=m_norm_mix_w, m_w_in=m_w_in, m_q_norm_w=m_q_norm_w, m_k_norm_w=m_k_norm_w, m_sink_logit=m_sink_logit, m_conv_w=m_conv_w, m_conv_b=m_conv_b, m_conv_norm_w=m_conv_norm_w, m_conv_norm_b=m_conv_norm_b, m_w_out=m_w_out, m_norm_ffn_w=m_norm_ffn_w, m_w_up=m_w_up, m_ffn_conv_w=m_ffn_conv_w, m_ffn_conv_b=m_ffn_conv_b, m_w_down=m_w_down, v_c_ctx=v_c_ctx, v_w_mod=v_w_mod, v_b_mod=v_b_mod, v_norm_mix_w=v_norm_mix_w, v_w_in=v_w_in, v_q_norm_w=v_q_norm_w, v_k_norm_w=v_k_norm_w, v_sink_logit=v_sink_logit, v_conv_w=v_conv_w, v_conv_b=v_conv_b, v_conv_norm_w=v_conv_norm_w, v_conv_norm_b=v_conv_norm_b, v_w_out=v_w_out, v_norm_ffn_w=v_norm_ffn_w, v_w_up=v_w_up, v_ffn_conv_w=v_ffn_conv_w, v_ffn_conv_b=v_ffn_conv_b, v_w_down=v_w_down)
    weights = {n: given[n] for n in TWIN_WEIGHTS}
    shared = {n: given[n] for n in SHARED_INPUTS}
    per_example = {n: given[n] for n in ['x', 'c', 'ctx']}
    grad_fn = _jax.value_and_grad(_loss, argnums=(0, 1))

    def one_microbatch(ex, loss_target):
        ex = dict(ex)
        diff = ex.pop(TWIN_DIFF_INPUT)
        return grad_fn(weights, diff, {**shared, **ex}, loss_target)

    if N_MICROBATCH == 1:
        loss, (grad_w, grad_x) = one_microbatch(per_example, given["loss_target"])
    else:
        def body(carry, xs):
            loss_sum, grad_sum = carry
            l_k, (gw_k, gx_k) = one_microbatch(xs[0], xs[1])
            with _jax.named_scope("update"):
                return (loss_sum + l_k, _jax.tree.map(_jnp.add, grad_sum, gw_k)), gx_k

        init = (_jnp.zeros((), _jnp.float32), _jax.tree.map(_jnp.zeros_like, weights))
        (loss, grad_w), grad_x = _jax.lax.scan(body, init, (per_example, given["loss_target"]))
    with _jax.named_scope("update"):
        delta_w, new_m, new_v = {}, {}, {}
        for n in TWIN_WEIGHTS:
            delta_w[n], new_m[n], new_v[n] = _adamw(weights[n], grad_w[n], given["m_" + n], given["v_" + n])
    return (loss, grad_x, *[grad_w[n] for n in TWIN_WEIGHTS], *[delta_w[n] for n in TWIN_WEIGHTS],
            *[new_m[n] for n in TWIN_WEIGHTS], *[new_v[n] for n in TWIN_WEIGHTS])
```

```python
import functools
import math

import jax
import jax.numpy as jnp
from jax import lax
from jax.experimental import pallas as pl
from jax.experimental.pallas import tpu as pltpu

F32 = jnp.float32
_MX = jnp.bfloat16
EPS = 1e-6
NEG = -1e30
D = 1024
AW = 512
CW = 512
INW = 1792
FH = 2816
LC = 256
CK = 31
GRID_W = 64
TM = 512
TQ = 128
NCH = FH // 128
NUP = FH // 256
ROWS_WIN, ROWS_WOUT, ROWS_WUP, ROWS_WDN = 3584, 2048, 11264, 5632
HROWS = (ROWS_WIN + ROWS_WOUT + ROWS_WUP + ROWS_WDN) // 2
VMEM_LIMIT = 56 * 1024 * 1024
MESH = pl.DeviceIdType.MESH

ADAM_LR, ADAM_B1, ADAM_B2, ADAM_EPS, ADAM_WD, ADAM_STEP = 0.001, 0.9, 0.999, 1e-08, 0.01, 10


def _cp(*sem):
    return pltpu.CompilerParams(dimension_semantics=sem, vmem_limit_bytes=VMEM_LIMIT)


def _vspec():
    return pl.BlockSpec(memory_space=pltpu.VMEM)


def _sig(z):
    return 1.0 / (1.0 + jnp.exp(-z))


def _dot(a, b):
    return jnp.dot(a, b, preferred_element_type=F32)


def _dot_nt(a, b):
    return lax.dot_general(a, b, (((1,), (1,)), ((), ())), preferred_element_type=F32)


def _dot_tn(a, b):
    return lax.dot_general(a, b, (((0,), (0,)), ((), ())), preferred_element_type=F32)


def _gsum(v, g):
    hi = v.astype(_MX)
    lo = (v - hi.astype(F32)).astype(_MX)
    return _dot(hi, g) + _dot(lo, g)


def _headnorm(ch, w, g):
    r = lax.rsqrt(_gsum(ch * ch, g) * (1.0 / 64.0) + EPS)
    return ch * r * w


def _headnorm_bwd(ch, dy, w, g):
    r = lax.rsqrt(_gsum(ch * ch, g) * (1.0 / 64.0) + EPS)
    hat = ch * r
    dhat = dy * w
    dch = r * (dhat - hat * (_gsum(dhat * hat, g) * (1.0 / 64.0)))
    return dch, dy * hat


def _rope(u, cos, sa, sb):
    return u * cos + pltpu.roll(u, 112, 1) * sa + pltpu.roll(u, 16, 1) * sb


def _rope_bwd(d, cos, sa, sb):
    return d * cos - pltpu.roll(d, 112, 1) * sa - pltpu.roll(d, 16, 1) * sb


def _rowsum(v):
    return jnp.sum(v, axis=0, keepdims=True)


def _mix_in(x, nw, sc, sh, win, qnw2, knw2, cos, sa, sb, gmat):
    s = x.shape[0]

    def body(x_ref, nw_ref, sc_ref, sh_ref, win_ref, qnw_ref, knw_ref, cos_ref, sa_ref, sb_ref, g_ref,
             proj_ref, h_ref, q0_ref, q1_ref, k_ref, ksw_ref, v_ref, vsw_ref, u0_ref):
        xv = x_ref[...]
        r = lax.rsqrt(jnp.mean(xv * xv, axis=-1, keepdims=True) + EPS)
        h = xv * r * (nw_ref[...] * (1.0 + sc_ref[...])) + sh_ref[...]
        hb = h.astype(_MX)
        h_ref[...] = hb
        proj = _dot(hb, win_ref[...])
        proj_ref[...] = proj
        cs, sav, sbv, g = cos_ref[...], sa_ref[...], sb_ref[...], g_ref[...]
        lo = lax.broadcasted_iota(jnp.int32, (1, 128), 1) < 64
        for j in range(4):
            q = _rope(_headnorm(proj[:, 128 * j:128 * (j + 1)], qnw_ref[...], g), cs, sav, sbv) * 0.125
            q0_ref[:, 128 * j:128 * (j + 1)] = jnp.where(lo, q, 0.0).astype(_MX)
            q1_ref[:, 128 * j:128 * (j + 1)] = jnp.where(lo, 0.0, q).astype(_MX)
        k = _rope(_headnorm(proj[:, 512:640], knw_ref[...], g), cs, sav, sbv)
        k_ref[...] = k.astype(_MX)
        ksw_ref[...] = pltpu.roll(k, 64, 1).astype(_MX)
        v = proj[:, 640:768]
        v_ref[...] = v.astype(_MX)
        vsw_ref[...] = pltpu.roll(v, 64, 1).astype(_MX)
        u0_ref[...] = proj[:, 768:1280] * _sig(proj[:, 1280:1792])

    row = lambda n: pl.BlockSpec((1, n), lambda i: (0, 0))
    tile = lambda n: pl.BlockSpec((TM, n), lambda i: (i, 0))
    return pl.pallas_call(
        body, name="mix_in", grid=(s // TM,),
        in_specs=[tile(D), row(D), row(D), row(D), pl.BlockSpec((D, INW), lambda i: (0, 0)), row(128), row(128),
                  tile(128), tile(128), tile(128), pl.BlockSpec((128, 128), lambda i: (0, 0))],
        out_specs=[tile(INW), tile(D), tile(AW), tile(AW), tile(128), tile(128), tile(128), tile(128), tile(CW)],
        out_shape=[jax.ShapeDtypeStruct((s, INW), F32), jax.ShapeDtypeStruct((s, D), _MX),
                   jax.ShapeDtypeStruct((s, AW), _MX), jax.ShapeDtypeStruct((s, AW), _MX),
                   jax.ShapeDtypeStruct((s, 128), _MX), jax.ShapeDtypeStruct((s, 128), _MX),
                   jax.ShapeDtypeStruct((s, 128), _MX), jax.ShapeDtypeStruct((s, 128), _MX),
                   jax.ShapeDtypeStruct((s, CW), F32)],
        compiler_params=_cp("arbitrary"),
    )(x, nw, sc, sh, win, qnw2, knw2, cos, sa, sb, gmat)


def _ctx_kv(ctx, nw, scc, shc, winkv, knw2, gmat):
    def body(ctx_ref, nw_ref, sc_ref, sh_ref, w_ref, knw_ref, g_ref,
             kvc_ref, hc_ref, kc_ref, kcsw_ref, vc_ref, vcsw_ref):
        cv = ctx_ref[...]
        r = lax.rsqrt(jnp.mean(cv * cv, axis=-1, keepdims=True) + EPS)
        hc = (cv * r * (nw_ref[...] * (1.0 + sc_ref[...])) + sh_ref[...]).astype(_MX)
        hc_ref[...] = hc
        kvc = _dot(hc, w_ref[...])
        kvc_ref[...] = kvc
        kc = _headnorm(kvc[:, :128], knw_ref[...], g_ref[...])
        kc_ref[...] = kc.astype(_MX)
        kcsw_ref[...] = pltpu.roll(kc, 64, 1).astype(_MX)
        vc = kvc[:, 128:]
        vc_ref[...] = vc.astype(_MX)
        vcsw_ref[...] = pltpu.roll(vc, 64, 1).astype(_MX)

    return pl.pallas_call(
        body, name="ctx_kv",
        in_specs=[_vspec()] * 7, out_specs=[_vspec()] * 6,
        out_shape=[jax.ShapeDtypeStruct((LC, 256), F32), jax.ShapeDtypeStruct((LC, D), _MX)]
        + [jax.ShapeDtypeStruct((LC, 128), _MX)] * 4,
        compiler_params=pltpu.CompilerParams(vmem_limit_bytes=VMEM_LIMIT),
    )(ctx, nw, scc, shc, winkv, knw2, gmat)


def _attn_mask(i, s):
    r = lax.broadcasted_iota(jnp.int32, (2 * TQ, LC + 3 * TQ), 0) % TQ
    cidx = lax.broadcasted_iota(jnp.int32, (2 * TQ, LC + 3 * TQ), 1)
    qpos = i * TQ + r
    kpos = (i - 1) * TQ + (cidx - LC)
    near = (jnp.abs(qpos - kpos) <= 128) & (kpos >= 0) & (kpos < s)
    return (cidx < LC) | near


def _attn_probs(qm, kge, mask, sinkv):
    sc = _dot_nt(qm, kge)
    sc = jnp.where(mask, sc, NEG)
    m = jnp.maximum(jnp.max(sc, axis=-1, keepdims=True), sinkv)
    ex = jnp.exp(sc - m)
    es = jnp.exp(sinkv - m)
    inv = 1.0 / (jnp.sum(ex, axis=-1, keepdims=True) + es)
    return ex * inv, es * inv


def _sink_rows(sink_ref, g, e):
    return jnp.concatenate([jnp.full((TQ, 1), sink_ref[4 * g + e], F32),
                            jnp.full((TQ, 1), sink_ref[4 * g + 2 + e], F32)], axis=0)


def _attn_fwd(q0, q1, kp, kswp, vp, vswp, kc, kcsw, vc, vcsw, sink):
    s = q0.shape[0]

    def body(q0_ref, q1_ref, kp_ref, kswp_ref, vp_ref, vswp_ref, kc_ref, kcsw_ref, vc_ref, vcsw_ref, sink_ref,
             o_ref):
        i = pl.program_id(0)
        st = pl.multiple_of(i * TQ, TQ)
        kall = (jnp.concatenate([kc_ref[...], kp_ref[pl.ds(st, 3 * TQ), :]], axis=0),
                jnp.concatenate([kcsw_ref[...], kswp_ref[pl.ds(st, 3 * TQ), :]], axis=0))
        vall = (jnp.concatenate([vc_ref[...], vp_ref[pl.ds(st, 3 * TQ), :]], axis=0),
                jnp.concatenate([vcsw_ref[...], vswp_ref[pl.ds(st, 3 * TQ), :]], axis=0))
        mask = _attn_mask(i, s)
        lo = lax.broadcasted_iota(jnp.int32, (1, 128), 1) < 64
        qrefs = (q0_ref, q1_ref)
        out = [jnp.zeros((TQ, 128), F32) for _ in range(4)]
        for g in range(2):
            for e in range(2):
                me = lo if e == 0 else jnp.logical_not(lo)
                qm = jnp.concatenate([qrefs[e][:, 256 * g:256 * g + 128],
                                      qrefs[e][:, 256 * g + 128:256 * g + 256]], axis=0)
                sw = 0 if e == g else 1
                p, _ = _attn_probs(qm, kall[sw], mask, _sink_rows(sink_ref, g, e))
                o2 = _dot(p.astype(_MX), vall[sw])
                out[2 * g] = out[2 * g] + jnp.where(me, o2[:TQ], 0.0)
                out[2 * g + 1] = out[2 * g + 1] + jnp.where(me, o2[TQ:], 0.0)
        for j in range(4):
            o_ref[:, 128 * j:128 * (j + 1)] = out[j]

    full = lambda a: pl.BlockSpec(a.shape, lambda i: (0, 0))
    qs = pl.BlockSpec((TQ, AW), lambda i: (i, 0))
    return pl.pallas_call(
        body, name="attn_fwd", grid=(s // TQ,),
        in_specs=[qs, qs, full(kp), full(kp), full(kp), full(kp), full(kc), full(kc), full(kc), full(kc),
                  pl.BlockSpec(memory_space=pltpu.SMEM)],
        out_specs=qs, out_shape=jax.ShapeDtypeStruct((s, AW), F32),
        compiler_params=_cp("arbitrary"),
    )(q0, q1, kp, kswp, vp, vswp, kc, kcsw, vc, vcsw, sink)


def _conv31(u0, cw32, cb):
    s = u0.shape[0]
    rch = 256

    def body(u_ref, w_ref, b_ref, o_ref, pad_ref):
        pad_ref[0:16, :] = jnp.zeros((16, 128), F32)
        pad_ref[s + 16:s + 32, :] = jnp.zeros((16, 128), F32)
        pad_ref[16:s + 16, :] = u_ref[...]
        for cidx in range(s // rch):
            base = cidx * rch
            acc = jnp.zeros((rch, 128), F32) + b_ref[...]
            for j in range(CK):
                acc = acc + w_ref[j:j + 1, :] * pad_ref[base + j + 1:base + j + 1 + rch, :]
            o_ref[base:base + rch, :] = acc

    return pl.pallas_call(
        body, name="conv31", grid=(CW // 128,),
        in_specs=[pl.BlockSpec((s, 128), lambda c: (0, c)), pl.BlockSpec((32, 128), lambda c: (0, c)),
                  pl.BlockSpec((1, 128), lambda c: (0, c))],
        out_specs=pl.BlockSpec((s, 128), lambda c: (0, c)),
        out_shape=jax.ShapeDtypeStruct((s, CW), F32),
        scratch_shapes=[pltpu.VMEM((s + 32, 128), F32)],
        compiler_params=_cp("arbitrary"),
    )(u0, cw32, cb)


def _ln_stats(u1):
    mu = jnp.mean(u1, axis=-1, keepdims=True)
    xc = u1 - mu
    rstd = lax.rsqrt(jnp.mean(xc * xc, axis=-1, keepdims=True) + EPS)
    return xc * rstd, rstd


def _mix_out(o, u1, cnw, cnb, wout, x, g1):
    s = x.shape[0]

    def body(o_ref, u1_ref, cnw_ref, cnb_ref, w_ref, x_ref, g1_ref, x1_ref, mix_ref, cat_ref):
        u2n, _ = _ln_stats(u1_ref[...])
        u2 = u2n * cnw_ref[...] + cnb_ref[...]
        u3 = u2 * _sig(u2)
        cat = jnp.concatenate([o_ref[...], u3], axis=1).astype(_MX)
        cat_ref[...] = cat
        mix = _dot(cat, w_ref[...])
        mix_ref[...] = mix
        x1_ref[...] = x_ref[...] + g1_ref[...] * mix

    row = lambda n: pl.BlockSpec((1, n), lambda i: (0, 0))
    tile = lambda n: pl.BlockSpec((TM, n), lambda i: (i, 0))
    return pl.pallas_call(
        body, name="mix_out", grid=(s // TM,),
        in_specs=[tile(AW), tile(CW), row(CW), row(CW), pl.BlockSpec((D, D), lambda i: (0, 0)), tile(D), row(D)],
        out_specs=[tile(D), tile(D), tile(D)],
        out_shape=[jax.ShapeDtypeStruct((s, D), F32), jax.ShapeDtypeStruct((s, D), F32),
                   jax.ShapeDtypeStruct((s, D), _MX)],
        compiler_params=_cp("arbitrary"),
    )(o, u1, cnw, cnb, wout, x, g1)


def _ffn_up(x1, nw, sc, sh, wup):
    s = x1.shape[0]

    def body(x_ref, nw_ref, sc_ref, sh_ref, w_ref, h2_ref, up_ref, h2s):
        @pl.when(pl.program_id(1) == 0)
        def _():
            xv = x_ref[...]
            r = lax.rsqrt(jnp.mean(xv * xv, axis=-1, keepdims=True) + EPS)
            h2 = (xv * r * (nw_ref[...] * (1.0 + sc_ref[...])) + sh_ref[...]).astype(_MX)
            h2s[...] = h2
            h2_ref[...] = h2
        up_ref[...] = _dot(h2s[...], w_ref[...])

    row = pl.BlockSpec((1, D), lambda i, n: (0, 0))
    return pl.pallas_call(
        body, name="ffn_up", grid=(s // TM, 2 * NUP),
        in_specs=[pl.BlockSpec((TM, D), lambda i, n: (i, 0)), row, row, row,
                  pl.BlockSpec((D, 256), lambda i, n: (0, n))],
        out_specs=[pl.BlockSpec((TM, D), lambda i, n: (i, 0)),
                   pl.BlockSpec((None, TM, 256), lambda i, n: (n // NUP, i, n % NUP))],
        out_shape=[jax.ShapeDtypeStruct((s, D), _MX), jax.ShapeDtypeStruct((2, s, FH), F32)],
        scratch_shapes=[pltpu.VMEM((TM, D), _MX)],
        compiler_params=_cp("arbitrary", "arbitrary"),
    )(x1, nw, sc, sh, wup)


def _fill_pad8(pad_ref, val, s):
    pad_ref[0:8, :] = jnp.zeros((8, 128), F32)
    pad_ref[s + 8:s + 16, :] = jnp.zeros((8, 128), F32)
    pad_ref[8:s + 8, :] = val


def _conv3_at(pad_ref, w_ref, half, base, rch):
    return (w_ref[half, 0:1, :] * pad_ref[base + 7:base + 7 + rch, :]
            + w_ref[half, 1:2, :] * pad_ref[base + 8:base + 8 + rch, :]
            + w_ref[half, 2:3, :] * pad_ref[base + 9:base + 9 + rch, :])


def _ffn_conv_act(up0, fcw, fcb):
    s = up0.shape[1]
    rch = 256

    def body(up_ref, w_ref, b_ref, act_ref, padg, padv):
        _fill_pad8(padg, up_ref[0], s)
        _fill_pad8(padv, up_ref[1], s)
        for cidx in range(s // rch):
            base = cidx * rch
            gate = _conv3_at(padg, w_ref, 0, base, rch) + b_ref[0]
            val = _conv3_at(padv, w_ref, 1, base, rch) + b_ref[1]
            act_ref[base:base + rch, :] = (gate * _sig(gate) * val).astype(_MX)

    return pl.pallas_call(
        body, name="ffn_conv_act", grid=(NCH,),
        in_specs=[pl.BlockSpec((2, s, 128), lambda c: (0, 0, c)), pl.BlockSpec((2, 8, 128), lambda c: (0, 0, c)),
                  pl.BlockSpec((2, 1, 128), lambda c: (0, 0, c))],
        out_specs=pl.BlockSpec((s, 128), lambda c: (0, c)),
        out_shape=jax.ShapeDtypeStruct((s, FH), _MX),
        scratch_shapes=[pltpu.VMEM((s + 16, 128), F32)] * 2,
        compiler_params=_cp("arbitrary"),
    )(up0, fcw, fcb)


def _ffn_down(act, wdown, x1, g2, tgt):
    s = x1.shape[0]

    def body(act_ref, w_ref, x1_ref, g2_ref, tgt_ref, dy_ref, dact_ref, ddn_ref, vec_ref):
        @pl.when(pl.program_id(0) == 0)
        def _():
            vec_ref[...] = jnp.zeros((8, D), F32)
        dn = _dot(act_ref[...], w_ref[...])
        diff = x1_ref[...] + g2_ref[...] * dn - tgt_ref[...]
        dy = diff * (1.0 / D)
        dy_ref[...] = dy
        ddn = (dy * g2_ref[...]).astype(_MX)
        ddn_ref[...] = ddn
        dact_ref[...] = _dot_nt(ddn, w_ref[...])
        vec_ref[0:1, :] += _rowsum(dy * dn)
        vec_ref[1:2, :] += _rowsum(diff * diff)

    tile = lambda n: pl.BlockSpec((TM, n), lambda i: (i, 0))
    return pl.pallas_call(
        body, name="ffn_down", grid=(s // TM,),
        in_specs=[tile(FH), pl.BlockSpec((FH, D), lambda i: (0, 0)), tile(D), pl.BlockSpec((1, D), lambda i: (0, 0)),
                  tile(D)],
        out_specs=[tile(D), tile(FH), tile(D), pl.BlockSpec((8, D), lambda i: (0, 0))],
        out_shape=[jax.ShapeDtypeStruct((s, D), F32), jax.ShapeDtypeStruct((s, FH), F32),
                   jax.ShapeDtypeStruct((s, D), _MX), jax.ShapeDtypeStruct((8, D), F32)],
        compiler_params=_cp("arbitrary"),
    )(act, wdown, x1, g2, tgt)


def _ffn_conv_bwd(up0, dact, fcw, fcb):
    s = up0.shape[1]
    rch = 256

    def body(up_ref, da_ref, w_ref, b_ref, dup_ref, gw_ref, padg, padv, dpg, dpv):
        _fill_pad8(padg, up_ref[0], s)
        _fill_pad8(padv, up_ref[1], s)
        for p in (dpg, dpv):
            p[0:8, :] = jnp.zeros((8, 128), F32)
            p[s + 8:s + 16, :] = jnp.zeros((8, 128), F32)
        acc = [[jnp.zeros((1, 128), F32) for _ in range(4)] for _ in range(2)]
        for cidx in range(s // rch):
            base = cidx * rch
            gate = _conv3_at(padg, w_ref, 0, base, rch) + b_ref[0]
            val = _conv3_at(padv, w_ref, 1, base, rch) + b_ref[1]
            da = da_ref[base:base + rch, :]
            sg = _sig(gate)
            dgate = da * val * sg * (1.0 + gate * (1.0 - sg))
            dval = da * gate * sg
            dpg[base + 8:base + 8 + rch, :] = dgate
            dpv[base + 8:base + 8 + rch, :] = dval
            for half, (dd, pad) in enumerate(((dgate, padg), (dval, padv))):
                for j in range(3):
                    acc[half][j] = acc[half][j] + _rowsum(dd * pad[base + 7 + j:base + 7 + j + rch, :])
                acc[half][3] = acc[half][3] + _rowsum(dd)
        for half in range(2):
            gw_ref[half] = jnp.zeros((8, 128), F32)
            for j in range(4):
                gw_ref[half, j:j + 1, :] = acc[half][j]
        for cidx in range(s // rch):
            base = cidx * rch
            for half, dp in enumerate((dpg, dpv)):
                dup_ref[half, base:base + rch, :] = (
                    w_ref[half, 0:1, :] * dp[base + 9:base + 9 + rch, :]
                    + w_ref[half, 1:2, :] * dp[base + 8:base + 8 + rch, :]
                    + w_ref[half, 2:3, :] * dp[base + 7:base + 7 + rch, :]).astype(_MX)

    return pl.pallas_call(
        body, name="ffn_conv_bwd", grid=(NCH,),
        in_specs=[pl.BlockSpec((2, s, 128), lambda c: (0, 0, c)), pl.BlockSpec((s, 128), lambda c: (0, c)),
                  pl.BlockSpec((2, 8, 128), lambda c: (0, 0, c)), pl.BlockSpec((2, 1, 128), lambda c: (0, 0, c))],
        out_specs=[pl.BlockSpec((2, s, 128), lambda c: (0, 0, c)), pl.BlockSpec((2, 8, 128), lambda c: (0, 0, c))],
        out_shape=[jax.ShapeDtypeStruct((2, s, FH), _MX), jax.ShapeDtypeStruct((2, 8, FH), F32)],
        scratch_shapes=[pltpu.VMEM((s + 16, 128), F32)] * 4,
        compiler_params=_cp("arbitrary"),
    )(up0, dact, fcw, fcb)


def _tn_matmul(a, b, tm, tn, ts, name, b_split=False, init=None):
    s, m = a.shape
    n = 2 * b.shape[2] if b_split else b.shape[1]
    nsteps = s // ts
    npb = (n // 2) // tn if b_split else None

    def body(*refs):
        if init is None:
            a_ref, b_ref, o_ref, acc = refs
        else:
            a_ref, b_ref, i_ref, o_ref, acc = refs
        k = pl.program_id(2)

        @pl.when(k == 0)
        def _():
            acc[...] = jnp.zeros((tm, tn), F32) if init is None else i_ref[...]
        acc[...] += _dot_tn(a_ref[...], b_ref[...])

        @pl.when(k == nsteps - 1)
        def _():
            o_ref[...] = acc[...]

    if b_split:
        bspec = pl.BlockSpec((None, ts, tn), lambda i, j, k: (j // npb, k, j % npb))
    else:
        bspec = pl.BlockSpec((ts, tn), lambda i, j, k: (k, j))
    in_specs = [pl.BlockSpec((ts, tm), lambda i, j, k: (k, i)), bspec]
    args = [a, b]
    if init is not None:
        in_specs.append(pl.BlockSpec((tm, tn), lambda i, j, k: (i, j)))
        args.append(init)
    return pl.pallas_call(
        body, name=name, grid=(m // tm, n // tn, nsteps),
        in_specs=in_specs, out_specs=pl.BlockSpec((tm, tn), lambda i, j, k: (i, j)),
        out_shape=jax.ShapeDtypeStruct((m, n), F32),
        scratch_shapes=[pltpu.VMEM((tm, tn), F32)],
        compiler_params=_cp("arbitrary", "arbitrary", "arbitrary"),
    )(*args)


def _ffn_up_bwd(dup, wup, x1, dy, mix, nw, sc, g1):
    s = x1.shape[0]
    nk = 2 * NUP

    def body(dup_ref, w_ref, x1_ref, dy_ref, mix_ref, nw_ref, sc_ref, g1_ref, dx1_ref, dmix_ref, vec_ref, acc):
        i, k = pl.program_id(0), pl.program_id(1)

        @pl.when(k == 0)
        def _():
            acc[...] = jnp.zeros((TM, D), F32)

        @pl.when((k == 0) & (i == 0))
        def _():
            vec_ref[...] = jnp.zeros((8, D), F32)
        acc[...] += _dot_nt(dup_ref[...], w_ref[...])

        @pl.when(k == nk - 1)
        def _():
            dh = acc[...]
            xv = x1_ref[...]
            r = lax.rsqrt(jnp.mean(xv * xv, axis=-1, keepdims=True) + EPS)
            xn = xv * r
            nwv, scv = nw_ref[...], sc_ref[...]
            vec_ref[0:1, :] += _rowsum(dh)
            vec_ref[1:2, :] += _rowsum(dh * xn) * nwv
            vec_ref[2:3, :] += _rowsum(dh * xn) * (1.0 + scv)
            dxn = dh * (nwv * (1.0 + scv))
            dx1 = dy_ref[...] + r * (dxn - xn * jnp.mean(dxn * xn, axis=-1, keepdims=True))
            dx1_ref[...] = dx1
            vec_ref[3:4, :] += _rowsum(dx1 * mix_ref[...])
            dmix_ref[...] = (dx1 * g1_ref[...]).astype(_MX)

    tile = pl.BlockSpec((TM, D), lambda i, k: (i, 0))
    row = pl.BlockSpec((1, D), lambda i, k: (0, 0))
    return pl.pallas_call(
        body, name="ffn_up_bwd", grid=(s // TM, nk),
        in_specs=[pl.BlockSpec((None, TM, 256), lambda i, k: (k // NUP, i, k % NUP)),
                  pl.BlockSpec((D, 256), lambda i, k: (0, k)), tile, tile, tile, row, row, row],
        out_specs=[tile, tile, pl.BlockSpec((8, D), lambda i, k: (0, 0))],
        out_shape=[jax.ShapeDtypeStruct((s, D), F32), jax.ShapeDtypeStruct((s, D), _MX),
                   jax.ShapeDtypeStruct((8, D), F32)],
        scratch_shapes=[pltpu.VMEM((TM, D), F32)],
        compiler_params=_cp("arbitrary", "arbitrary"),
    )(dup, wup, x1, dy, mix, nw, sc, g1)


def _mix_out_bwd(dmix, wout, u1, cnw, cnb):
    s = u1.shape[0]

    def body(dm_ref, w_ref, u1_ref, cnw_ref, cnb_ref, do_ref, du1_ref, vec_ref):
        @pl.when(pl.program_id(0) == 0)
        def _():
            vec_ref[...] = jnp.zeros((8, CW), F32)
        dcat = _dot_nt(dm_ref[...], w_ref[...])
        do_ref[...] = dcat[:, :AW]
        du3 = dcat[:, AW:]
        u2n, rstd = _ln_stats(u1_ref[...])
        u2 = u2n * cnw_ref[...] + cnb_ref[...]
        sg = _sig(u2)
        du2 = du3 * sg * (1.0 + u2 * (1.0 - sg))
        vec_ref[0:1, :] += _rowsum(du2)
        vec_ref[1:2, :] += _rowsum(du2 * u2n)
        d2n = du2 * cnw_ref[...]
        du1_ref[...] = rstd * (d2n - jnp.mean(d2n, axis=-1, keepdims=True)
                               - u2n * jnp.mean(d2n * u2n, axis=-1, keepdims=True))

    row = lambda n: pl.BlockSpec((1, n), lambda i: (0, 0))
    tile = lambda n: pl.BlockSpec((TM, n), lambda i: (i, 0))
    return pl.pallas_call(
        body, name="mix_out_bwd", grid=(s // TM,),
        in_specs=[tile(D), pl.BlockSpec((D, D), lambda i: (0, 0)), tile(CW), row(CW), row(CW)],
        out_specs=[tile(AW), tile(CW), pl.BlockSpec((8, CW), lambda i: (0, 0))],
        out_shape=[jax.ShapeDtypeStruct((s, AW), F32), jax.ShapeDtypeStruct((s, CW), F32),
                   jax.ShapeDtypeStruct((8, CW), F32)],
        compiler_params=_cp("arbitrary"),
    )(dmix, wout, u1, cnw, cnb)


def _conv31_bwd(du1, u0, cw32, proj):
    s = u0.shape[0]
    rch = 256

    def body(d_ref, u_ref, w_ref, ga_ref, gb_ref, dga_ref, dgb_ref, gw_ref, padu, padd):
        for p, src in ((padu, u_ref), (padd, d_ref)):
            p[0:16, :] = jnp.zeros((16, 128), F32)
            p[s + 16:s + 32, :] = jnp.zeros((16, 128), F32)
            p[16:s + 16, :] = src[...]
        for j in range(CK):
            acc = jnp.zeros((1, 128), F32)
            for cidx in range(s // rch):
                base = cidx * rch
                acc = acc + _rowsum(d_ref[base:base + rch, :] * padu[base + j + 1:base + j + 1 + rch, :])
            gw_ref[j:j + 1, :] = acc
        gw_ref[CK:CK + 1, :] = _rowsum(d_ref[...])
        for cidx in range(s // rch):
            base = cidx * rch
            du0 = jnp.zeros((rch, 128), F32)
            for j in range(CK):
                du0 = du0 + w_ref[j:j + 1, :] * padd[base + 31 - j:base + 31 - j + rch, :]
            sg = _sig(gb_ref[base:base + rch, :])
            ga = ga_ref[base:base + rch, :]
            dga_ref[base:base + rch, :] = (du0 * sg).astype(_MX)
            dgb_ref[base:base + rch, :] = (du0 * ga * sg * (1.0 - sg)).astype(_MX)

    blk = lambda off: pl.BlockSpec((s, 128), lambda c: (0, c + off))
    return pl.pallas_call(
        body, name="conv31_bwd", grid=(CW // 128,),
        in_specs=[blk(0), blk(0), pl.BlockSpec((32, 128), lambda c: (0, c)), blk(6), blk(10)],
        out_specs=[blk(0), blk(0), pl.BlockSpec((32, 128), lambda c: (0, c))],
        out_shape=[jax.ShapeDtypeStruct((s, CW), _MX), jax.ShapeDtypeStruct((s, CW), _MX),
                   jax.ShapeDtypeStruct((32, CW), F32)],
        scratch_shapes=[pltpu.VMEM((s + 32, 128), F32)] * 2,
        compiler_params=_cp("arbitrary"),
    )(du1, u0, cw32, proj, proj)


def _attn_bwd(q0, q1, kp, kswp, vp, vswp, kc, kcsw, vc, vcsw, sink, o, do):
    s = q0.shape[0]
    nb = s // TQ

    def body(q0_ref, q1_ref, kp_ref, kswp_ref, vp_ref, vswp_ref, kc_ref, kcsw_ref, vc_ref, vcsw_ref, sink_ref,
             o_ref, do_ref, dq_ref, dk_ref, dv_ref, dkc_ref, dvc_ref, dsink_ref):
        i = pl.program_id(0)

        @pl.when(i == 0)
        def _():
            dk_ref[...] = jnp.zeros((s + 2 * TQ, 128), F32)
            dv_ref[...] = jnp.zeros((s + 2 * TQ, 128), F32)
            dkc_ref[...] = jnp.zeros((LC, 128), F32)
            dvc_ref[...] = jnp.zeros((LC, 128), F32)
            dsink_ref[...] = jnp.zeros((8, 128), F32)
        st = pl.multiple_of(i * TQ, TQ)
        kall = (jnp.concatenate([kc_ref[...], kp_ref[pl.ds(st, 3 * TQ), :]], axis=0),
                jnp.concatenate([kcsw_ref[...], kswp_ref[pl.ds(st, 3 * TQ), :]], axis=0))
        vall = (jnp.concatenate([vc_ref[...], vp_ref[pl.ds(st, 3 * TQ), :]], axis=0),
                jnp.concatenate([vcsw_ref[...], vswp_ref[pl.ds(st, 3 * TQ), :]], axis=0))
        mask = _attn_mask(i, s)
        lo = lax.broadcasted_iota(jnp.int32, (1, 128), 1) < 64
        qrefs = (q0_ref, q1_ref)
        dq = [jnp.zeros((TQ, 128), F32) for _ in range(4)]
        dkt = jnp.zeros((LC + 3 * TQ, 128), F32)
        dvt = jnp.zeros((LC + 3 * TQ, 128), F32)
        for g in range(2):
            for e in range(2):
                me = lo if e == 0 else jnp.logical_not(lo)
                c0, c1 = 256 * g, 256 * g + 128
                qm = jnp.concatenate([qrefs[e][:, c0:c0 + 128], qrefs[e][:, c1:c1 + 128]], axis=0)
                sw = 0 if e == g else 1
                p, ps = _attn_probs(qm, kall[sw], mask, _sink_rows(sink_ref, g, e))
                dom = jnp.where(me, jnp.concatenate([do_ref[:, c0:c0 + 128], do_ref[:, c1:c1 + 128]], axis=0), 0.0)
                o2 = jnp.concatenate([o_ref[:, c0:c0 + 128], o_ref[:, c1:c1 + 128]], axis=0)
                dd = jnp.sum(dom * o2, axis=-1, keepdims=True)
                domx = dom.astype(_MX)
                dp = _dot_nt(domx, vall[sw])
                ds = (p * (dp - dd)).astype(_MX)
                sd = ps * dd
                dsink_ref[4 * g + e:4 * g + e + 1, :] -= jnp.sum(sd[:TQ], axis=0, keepdims=True)
                dsink_ref[4 * g + 2 + e:4 * g + 3 + e, :] -= jnp.sum(sd[TQ:], axis=0, keepdims=True)
                dq2 = _dot(ds, kall[sw])
                dq[2 * g] = dq[2 * g] + jnp.where(me, dq2[:TQ], 0.0)
                dq[2 * g + 1] = dq[2 * g + 1] + jnp.where(me, dq2[TQ:], 0.0)
                dk2 = _dot_tn(ds, qm)
                dv2 = _dot_tn(p.astype(_MX), domx)
                if sw:
                    dk2 = pltpu.roll(dk2, 64, 1)
                    dv2 = pltpu.roll(dv2, 64, 1)
                dkt = dkt + dk2
                dvt = dvt + dv2
        for j in range(4):
            dq_ref[:, 128 * j:128 * (j + 1)] = dq[j] * 0.125
        dkc_ref[...] += dkt[:LC]
        dvc_ref[...] += dvt[:LC]
        dk_ref[pl.ds(st, 3 * TQ), :] += dkt[LC:]
        dv_ref[pl.ds(st, 3 * TQ), :] += dvt[LC:]

    full = lambda a: pl.BlockSpec(a.shape, lambda i: (0, 0))
    fs = lambda r: pl.BlockSpec((r, 128), lambda i: (0, 0))
    qs = pl.BlockSpec((TQ, AW), lambda i: (i, 0))
    return pl.pallas_call(
        body, name="attn_bwd", grid=(nb,),
        in_specs=[qs, qs, full(kp), full(kp), full(kp), full(kp), full(kc), full(kc), full(kc), full(kc),
                  pl.BlockSpec(memory_space=pltpu.SMEM), qs, qs],
        out_specs=[qs, fs(s + 2 * TQ), fs(s + 2 * TQ), fs(LC), fs(LC), fs(8)],
        out_shape=[jax.ShapeDtypeStruct((s, AW), F32), jax.ShapeDtypeStruct((s + 2 * TQ, 128), F32),
                   jax.ShapeDtypeStruct((s + 2 * TQ, 128), F32), jax.ShapeDtypeStruct((LC, 128), F32),
                   jax.ShapeDtypeStruct((LC, 128), F32), jax.ShapeDtypeStruct((8, 128), F32)],
        compiler_params=_cp("arbitrary"),
    )(q0, q1, kp, kswp, vp, vswp, kc, kcsw, vc, vcsw, sink, o, do)


def _mix_in_bwd(dq, dk, dv, dga, dgb, proj, x, dx1, win, nw, sc, qnw2, knw2, cos, sa, sb, gmat):
    s = x.shape[0]

    def body(dq_ref, dk_ref, dv_ref, dga_ref, dgb_ref, proj_ref, x_ref, dx1_ref, win_ref, nw_ref, sc_ref,
             qnw_ref, knw_ref, cos_ref, sa_ref, sb_ref, g_ref, gx_ref, dproj_ref, vec_ref):
        @pl.when(pl.program_id(0) == 0)
        def _():
            vec_ref[...] = jnp.zeros((8, D), F32)
        cs, sav, sbv, g = cos_ref[...], sa_ref[...], sb_ref[...], g_ref[...]
        gq = jnp.zeros((1, 128), F32)
        for j in range(4):
            dqn = _rope_bwd(dq_ref[:, 128 * j:128 * (j + 1)], cs, sav, sbv)
            dch, gw = _headnorm_bwd(proj_ref[:, 128 * j:128 * (j + 1)], dqn, qnw_ref[...], g)
            dproj_ref[:, 128 * j:128 * (j + 1)] = dch.astype(_MX)
            gq = gq + _rowsum(gw)
        dkn = _rope_bwd(dk_ref[...], cs, sav, sbv)
        dch, gw = _headnorm_bwd(proj_ref[:, 512:640], dkn, knw_ref[...], g)
        dproj_ref[:, 512:640] = dch.astype(_MX)
        dproj_ref[:, 640:768] = dv_ref[...].astype(_MX)
        dproj_ref[:, 768:1280] = dga_ref[...]
        dproj_ref[:, 1280:1792] = dgb_ref[...]
        vec_ref[3:4, 0:128] += gq
        vec_ref[3:4, 128:256] += _rowsum(gw)
        dh = _dot_nt(dproj_ref[...], win_ref[...])
        xv = x_ref[...]
        r = lax.rsqrt(jnp.mean(xv * xv, axis=-1, keepdims=True) + EPS)
        xn = xv * r
        nwv, scv = nw_ref[...], sc_ref[...]
        vec_ref[0:1, :] += _rowsum(dh)
        vec_ref[1:2, :] += _rowsum(dh * xn) * nwv
        vec_ref[2:3, :] += _rowsum(dh * xn) * (1.0 + scv)
        dxn = dh * (nwv * (1.0 + scv))
        gx_ref[...] = dx1_ref[...] + r * (dxn - xn * jnp.mean(dxn * xn, axis=-1, keepdims=True))

    row = lambda n: pl.BlockSpec((1, n), lambda i: (0, 0))
    tile = lambda n: pl.BlockSpec((TM, n), lambda i: (i, 0))
    return pl.pallas_call(
        body, name="mix_in_bwd", grid=(s // TM,),
        in_specs=[tile(AW), tile(128), tile(128), tile(CW), tile(CW), tile(INW), tile(D), tile(D),
                  pl.BlockSpec((D, INW), lambda i: (0, 0)), row(D), row(D), row(128), row(128),
                  tile(128), tile(128), tile(128), pl.BlockSpec((128, 128), lambda i: (0, 0))],
        out_specs=[tile(D), tile(INW), pl.BlockSpec((8, D), lambda i: (0, 0))],
        out_shape=[jax.ShapeDtypeStruct((s, D), F32), jax.ShapeDtypeStruct((s, INW), _MX),
                   jax.ShapeDtypeStruct((8, D), F32)],
        compiler_params=_cp("arbitrary"),
    )(dq, dk, dv, dga, dgb, proj, x, dx1, win, nw, sc, qnw2, knw2, cos, sa, sb, gmat)


def _ctx_bwd(ctx, nw, scc, winkv, kvc, hc, dkc, dvc, knw2, gmat):
    def body(ctx_ref, nw_ref, sc_ref, w_ref, kvc_ref, hc_ref, dkc_ref, dvc_ref, knw_ref, g_ref, gw_ref, vec_ref):
        dkr, gk = _headnorm_bwd(kvc_ref[:, 0:128], dkc_ref[...], knw_ref[...], g_ref[...])
        dkv = jnp.concatenate([dkr, dvc_ref[...]], axis=1).astype(_MX)
        gw_ref[...] = _dot_tn(hc_ref[...], dkv)
        dh = _dot_nt(dkv, w_ref[...])
        cv = ctx_ref[...]
        r = lax.rsqrt(jnp.mean(cv * cv, axis=-1, keepdims=True) + EPS)
        cn = cv * r
        vec_ref[...] = jnp.zeros((8, D), F32)
        vec_ref[0:1, :] = _rowsum(dh)
        vec_ref[1:2, :] = _rowsum(dh * cn) * nw_ref[...]
        vec_ref[2:3, :] = _rowsum(dh * cn) * (1.0 + sc_ref[...])
        vec_ref[3:4, 0:128] = _rowsum(gk)

    return pl.pallas_call(
        body, name="ctx_bwd", in_specs=[_vspec()] * 10, out_specs=[_vspec()] * 2,
        out_shape=[jax.ShapeDtypeStruct((D, 256), F32), jax.ShapeDtypeStruct((8, D), F32)],
        compiler_params=pltpu.CompilerParams(vmem_limit_bytes=VMEM_LIMIT),
    )(ctx, nw, scc, winkv, kvc, hc, dkc, dvc, knw2, gmat)


def _mod_fwd(cs, wmod, bloc):
    def body(c_ref, w_ref, b_ref, o_ref):
        cv = c_ref[...]
        o_ref[...] = _dot((cv * _sig(cv)).astype(_MX), w_ref[...].astype(_MX)) + b_ref[...]

    return pl.pallas_call(
        body, name="mod_fwd", in_specs=[_vspec()] * 3, out_specs=_vspec(),
        out_shape=jax.ShapeDtypeStruct((16, wmod.shape[1]), F32),
        compiler_params=pltpu.CompilerParams(vmem_limit_bytes=VMEM_LIMIT),
    )(cs, wmod, bloc)


def _mod_bwd(cs, dmod, wmod):
    def body(c_ref, d_ref, w_ref, gw_ref, part_ref):
        cv = c_ref[...]
        sl = (cv * _sig(cv)).astype(_MX)
        dm = d_ref[...].astype(_MX)
        gw_ref[...] = _dot_tn(sl, dm)
        part_ref[...] = _dot_nt(dm, w_ref[...].astype(_MX))

    return pl.pallas_call(
        body, name="mod_bwd", in_specs=[_vspec()] * 3, out_specs=[_vspec()] * 2,
        out_shape=[jax.ShapeDtypeStruct(wmod.shape, F32), jax.ShapeDtypeStruct((16, D), F32)],
        compiler_params=pltpu.CompilerParams(vmem_limit_bytes=VMEM_LIMIT),
    )(cs, dmod, wmod)


def _sum_leading(a, name):
    n = a.shape[0]

    def body(a_ref, o_ref):
        acc = a_ref[0]
        for k in range(1, n):
            acc = acc + a_ref[k]
        o_ref[...] = acc

    return pl.pallas_call(
        body, name=name, in_specs=[_vspec()], out_specs=_vspec(),
        out_shape=jax.ShapeDtypeStruct(a.shape[1:], F32),
        compiler_params=pltpu.CompilerParams(vmem_limit_bytes=VMEM_LIMIT),
    )(a)


def _cctx_grad(parts, cc):
    def body(p_ref, c_ref, o_ref):
        acc = p_ref[0, 8:9, :]
        for k in range(1, 4):
            acc = acc + p_ref[k, 8:9, :]
        cv = c_ref[...]
        sg = _sig(cv)
        o_ref[...] = acc * (sg * (1.0 + cv * (1.0 - sg)))

    return pl.pallas_call(
        body, name="cctx_grad", in_specs=[_vspec()] * 2, out_specs=_vspec(),
        out_shape=jax.ShapeDtypeStruct((1, D), F32),
    )(parts, cc)


def _adam_math(w, g, m, v):
    mn = ADAM_B1 * m + (1.0 - ADAM_B1) * g
    vn = ADAM_B2 * v + (1.0 - ADAM_B2) * (g * g)
    mh = mn / (1.0 - ADAM_B1 ** ADAM_STEP)
    vh = vn / (1.0 - ADAM_B2 ** ADAM_STEP)
    delta = -ADAM_LR * (mh / (jnp.sqrt(vh) + ADAM_EPS) + ADAM_WD * w)
    return delta, mn, vn


def _adam_big(w, g, m, v, name):
    r, n = w.shape
    tr = 256 if r % 256 == 0 else 64

    def body(w_ref, g_ref, m_ref, v_ref, d_ref, mo_ref, vo_ref):
        d, mn, vn = _adam_math(w_ref[...], g_ref[...], m_ref[...], v_ref[...])
        d_ref[...] = d
        mo_ref[...] = mn
        vo_ref[...] = vn

    spec = pl.BlockSpec((tr, n), lambda i: (i, 0))
    return pl.pallas_call(
        body, name=name, grid=(r // tr,), in_specs=[spec] * 4, out_specs=[spec] * 3,
        out_shape=[jax.ShapeDtypeStruct((r, n), F32)] * 3,
        compiler_params=_cp("arbitrary"),
    )(w, g, m, v)


def _adam_small(ws, gs, ms, vs):
    n = len(ws)

    def body(*refs):
        ins, outs = refs[:4 * n], refs[4 * n:]
        for k in range(n):
            d, mn, vn = _adam_math(ins[k][...], ins[n + k][...], ins[2 * n + k][...], ins[3 * n + k][...])
            outs[k][...] = d
            outs[n + k][...] = mn
            outs[2 * n + k][...] = vn

    shapes = [jax.ShapeDtypeStruct(w.shape, F32) for w in ws]
    res = pl.pallas_call(
        body, name="adam_small", in_specs=[_vspec()] * (4 * n), out_specs=[_vspec()] * (3 * n),
        out_shape=shapes * 3,
    )(*ws, *gs, *ms, *vs)
    return res[:n], res[n:2 * n], res[2 * n:]


def _pair_sum_bf16(a, b, name):
    def body(a_ref, b_ref, o_ref):
        o_ref[...] = (a_ref[...].astype(F32) + b_ref[...].astype(F32)).astype(_MX)

    spec = pl.BlockSpec((None, HROWS // 4, 128), lambda j, i: (j, i, 0))
    return pl.pallas_call(
        body, name=name, grid=(4, 4), in_specs=[spec] * 2, out_specs=spec,
        out_shape=jax.ShapeDtypeStruct(a.shape, _MX), compiler_params=_cp("arbitrary", "arbitrary"),
    )(a, b)


def _chip_sum(p3, name):
    def body(p_ref, o_ref):
        acc = p_ref[0].astype(F32)
        for k in range(1, 4):
            acc = acc + p_ref[k].astype(F32)
        o_ref[...] = acc

    return pl.pallas_call(
        body, name=name, grid=(4,),
        in_specs=[pl.BlockSpec((4, HROWS // 4, 128), lambda i: (0, i, 0))],
        out_specs=pl.BlockSpec((HROWS // 4, 128), lambda i: (i, 0)),
        out_shape=jax.ShapeDtypeStruct((HROWS, 128), F32), compiler_params=_cp("arbitrary"),
    )(p3)


def _position():
    x, y, c = lax.axis_index("x"), lax.axis_index("y"), lax.axis_index("c")
    return x, y, c


def _small_allgather(blk, name):
    m_per, n = blk.shape

    def body(x_ref, out_ref, send_sems, recv_sems, local_sem):
        x, y, c = _position()
        me, sibling = (x, y, c), (x, y, 1 - c)
        chips = [(1 - x, y), (x, 1 - y), (1 - x, 1 - y)]

        def rows(px, py, pc):
            return out_ref.at[pl.ds((4 * px + 2 * py + pc) * m_per, m_per), :]

        def copy(k, block, to, src=None):
            return pltpu.make_async_remote_copy(
                src_ref=rows(*block) if src is None else src, dst_ref=rows(*block),
                send_sem=send_sems.at[k], recv_sem=recv_sems.at[k], device_id=to, device_id_type=MESH)

        mine = pltpu.make_async_copy(x_ref, rows(*me), local_sem)
        mine.start()
        first = [copy(0, me, sibling, src=x_ref)]
        first += [copy(1 + j, me, (*chip, c), src=x_ref) for j, chip in enumerate(chips)]
        for cp in first:
            cp.start()
        passed = [copy(4 + j, (*chip, c), sibling) for j, chip in enumerate(chips)]
        for j, chip in enumerate(chips):
            copy(1 + j, (*chip, c), me).wait_recv()
            passed[j].start()
        copy(0, sibling, me).wait_recv()
        for j, chip in enumerate(chips):
            copy(4 + j, (*chip, 1 - c), me).wait_recv()
        for cp in first + passed:
            cp.wait_send()
        mine.wait()

    return pl.pallas_call(
        body, name=name, out_shape=jax.ShapeDtypeStruct((8 * m_per, n), blk.dtype),
        in_specs=[_vspec()], out_specs=_vspec(),
        scratch_shapes=[pltpu.SemaphoreType.DMA((7,)), pltpu.SemaphoreType.DMA((7,)), pltpu.SemaphoreType.DMA],
    )(blk)


def _weight_allgather(shard2):
    def body(s_ref, out_ref, send_sems, recv_sems, local_sems):
        x, y, c = _position()
        chip = 2 * x + y
        sibling = (x, y, 1 - c)
        chips = [(1 - x, y), (x, 1 - y), (1 - x, 1 - y)]

        def slot(px, py, half):
            return out_ref.at[2 * (2 * px + py) + half]

        def copy(k, src, dst, to):
            return pltpu.make_async_remote_copy(src_ref=src, dst_ref=dst, send_sem=send_sems.at[k],
                                                recv_sem=recv_sems.at[k], device_id=to, device_id_type=MESH)

        mine = [pltpu.make_async_copy(s_ref.at[h], out_ref.at[2 * chip + h], local_sems.at[h]) for h in range(2)]
        for cp in mine:
            cp.start()
        first = [copy(j, s_ref.at[c], slot(x, y, c), (*ch, c)) for j, ch in enumerate(chips)]
        for cp in first:
            cp.start()
        passed = [copy(3 + j, slot(*ch, c), slot(*ch, c), sibling) for j, ch in enumerate(chips)]
        for j, ch in enumerate(chips):
            copy(j, s_ref.at[c], slot(*ch, c), (*ch, c)).wait_recv()
            passed[j].start()
        for j, ch in enumerate(chips):
            copy(3 + j, slot(*ch, 1 - c), slot(*ch, 1 - c), sibling).wait_recv()
        for cp in first + passed:
            cp.wait_send()
        for cp in mine:
            cp.wait()

    return pl.pallas_call(
        body, name="weight_allgather", out_shape=jax.ShapeDtypeStruct((8, HROWS, 128), shard2.dtype),
        in_specs=[pl.BlockSpec(memory_space=pl.ANY)], out_specs=pl.BlockSpec(memory_space=pl.ANY),
        scratch_shapes=[pltpu.SemaphoreType.DMA((6,)), pltpu.SemaphoreType.DMA((6,)), pltpu.SemaphoreType.DMA((2,))],
    )(shard2)


def _sibling_swap(send, name):
    def body(s_ref, out_ref, send_sem, recv_sem):
        x, y, c = _position()
        cp = pltpu.make_async_remote_copy(src_ref=s_ref, dst_ref=out_ref, send_sem=send_sem, recv_sem=recv_sem,
                                          device_id=(x, y, 1 - c), device_id_type=MESH)
        cp.start()
        cp.wait()

    return pl.pallas_call(
        body, name=name, out_shape=jax.ShapeDtypeStruct(send.shape, send.dtype),
        in_specs=[pl.BlockSpec(memory_space=pl.ANY)], out_specs=pl.BlockSpec(memory_space=pl.ANY),
        scratch_shapes=[pltpu.SemaphoreType.DMA, pltpu.SemaphoreType.DMA],
    )(send)


def _chip_exchange(q):
    def body(q_ref, out_ref, send_sems, recv_sems, local_sem):
        x, y, c = _position()
        chip = 2 * x + y
        chips = [(1 - x, y), (x, 1 - y), (1 - x, 1 - y)]
        mine = pltpu.make_async_copy(q_ref.at[chip], out_ref.at[chip], local_sem)
        mine.start()

        def copy(k, ch):
            return pltpu.make_async_remote_copy(
                src_ref=q_ref.at[2 * ch[0] + ch[1]], dst_ref=out_ref.at[chip], send_sem=send_sems.at[k],
                recv_sem=recv_sems.at[k], device_id=(*ch, c), device_id_type=MESH)

        sends = [copy(k, ch) for k, ch in enumerate(chips)]
        for cp in sends:
            cp.start()
        for k, ch in enumerate(chips):
            pltpu.make_async_remote_copy(
                src_ref=q_ref.at[chip], dst_ref=out_ref.at[2 * ch[0] + ch[1]], send_sem=send_sems.at[k],
                recv_sem=recv_sems.at[k], device_id=(*ch, c), device_id_type=MESH).wait_recv()
        for cp in sends:
            cp.wait_send()
        mine.wait()

    return pl.pallas_call(
        body, name="chip_exchange", out_shape=jax.ShapeDtypeStruct(q.shape, q.dtype),
        in_specs=[pl.BlockSpec(memory_space=pl.ANY)], out_specs=pl.BlockSpec(memory_space=pl.ANY),
        scratch_shapes=[pltpu.SemaphoreType.DMA((3,)), pltpu.SemaphoreType.DMA((3,)), pltpu.SemaphoreType.DMA],
    )(q)


def _rope_tables(s):
    t = jnp.arange(s)
    inv = 10000.0 ** (-jnp.arange(0, 32, 2, dtype=F32) / 32.0)
    ang_r = (t // GRID_W).astype(F32)[:, None] * inv
    ang_c = (t % GRID_W).astype(F32)[:, None] * inv
    ang = jnp.concatenate([ang_r, ang_r, ang_c, ang_c] * 2, axis=-1)
    first = (jnp.arange(128) % 32) < 16
    sin = jnp.sin(ang)
    return jnp.cos(ang), jnp.where(first, -sin, 0.0), jnp.where(first, 0.0, sin)


def _local_step(x, ctx, tgt, mod, modc, nw1, win, qnw, knw, sink, cw, cb, cnw, cnb, wout, nw2, wup, fcw, fcb, wdown):
    s = x.shape[0]
    sh1, sc1, g1, sh2, sc2, g2 = [mod[:, D * k:D * (k + 1)] for k in range(6)]
    shc, scc = modc[:, :D], modc[:, D:2 * D]
    cos, sa, sb = _rope_tables(s)
    gi = jnp.arange(128) // 64
    gmat = (gi[:, None] == gi[None, :]).astype(_MX)
    qnw2, knw2 = jnp.tile(qnw, (1, 2)), jnp.tile(knw, (1, 2))
    cw32 = jnp.pad(cw, ((0, 1), (0, 0)))
    fcw2 = jnp.pad(fcw, ((0, 5), (0, 0))).reshape(8, 2, FH).transpose(1, 0, 2)
    fcb2 = fcb.reshape(2, 1, FH)
    winkv = win[:, 512:768]
    sinkv = sink.reshape(8)

    proj, h, q0, q1, k, ksw, v, vsw, u0 = _mix_in(x, nw1, sc1, sh1, win, qnw2, knw2, cos, sa, sb, gmat)
    kvc, hc, kc, kcsw, vc, vcsw = _ctx_kv(ctx, nw1, scc, shc, winkv, knw2, gmat)
    padr = lambda a: jnp.pad(a, ((TQ, TQ), (0, 0)))
    kp, kswp, vp, vswp = padr(k), padr(ksw), padr(v), padr(vsw)
    o = _attn_fwd(q0, q1, kp, kswp, vp, vswp, kc, kcsw, vc, vcsw, sinkv)
    u1 = _conv31(u0, cw32, cb)
    x1, mix, cat = _mix_out(o, u1, cnw, cnb, wout, x, g1)
    h2, up0 = _ffn_up(x1, nw2, sc2, sh2, wup)
    act = _ffn_conv_act(up0, fcw2, fcb2)
    dy, dact, ddn, vec_dn = _ffn_down(act, wdown, x1, g2, tgt)
    loss = (0.5 / D) * jnp.sum(vec_dn[1])

    dup, gfc = _ffn_conv_bwd(up0, dact, fcw2, fcb2)
    g_wdown = _tn_matmul(act, ddn, 256, 512, 1024 if s % 1024 == 0 else 512, "gw_down")
    g_wup = _tn_matmul(h2, dup, 512, 256, 1024 if s % 1024 == 0 else 512, "gw_up", b_split=True)
    dx1, dmix, vec_up = _ffn_up_bwd(dup, wup, x1, dy, mix, nw2, sc2, g1)
    g_wout = _tn_matmul(cat, dmix, 512, 512, 1024 if s % 1024 == 0 else 512, "gw_out")
    do, du1, vec_ln = _mix_out_bwd(dmix, wout, u1, cnw, cnb)
    dga, dgb, gcw = _conv31_bwd(du1, u0, cw32, proj)
    dq, dkp, dvp, dkc, dvc, dsink = _attn_bwd(q0, q1, kp, kswp, vp, vswp, kc, kcsw, vc, vcsw, sinkv, o, do)
    gwinkv, vec_ctx = _ctx_bwd(ctx, nw1, scc, winkv, kvc, hc, dkc, dvc, knw2, gmat)
    gx, dproj, vec_in = _mix_in_bwd(dq, dkp[TQ:TQ + s], dvp[TQ:TQ + s], dga, dgb, proj, x, dx1, win, nw1, sc1,
                                    qnw2, knw2, cos, sa, sb, gmat)
    g_win = _tn_matmul(h, dproj, 512, 256, 1024 if s % 1024 == 0 else 512, "gw_in",
                       init=jnp.pad(gwinkv, ((0, 0), (512, INW - 768))))

    g_qn = vec_in[3:4, 0:128]
    g_kn = vec_in[3:4, 128:256] + vec_ctx[3:4, 0:128]
    grads = dict(
        norm_mix_w=vec_in[2:3] + vec_ctx[2:3], w_in=g_win,
        q_norm_w=g_qn[:, :64] + g_qn[:, 64:], k_norm_w=g_kn[:, :64] + g_kn[:, 64:],
        sink_logit=dsink[:, 0].reshape(1, 8), conv_w=gcw[:CK], conv_b=gcw[CK:CK + 1],
        conv_norm_w=vec_ln[1:2], conv_norm_b=vec_ln[0:1], w_out=g_wout, norm_ffn_w=vec_up[2:3], w_up=g_wup,
        ffn_conv_w=gfc[:, 0:3].transpose(1, 0, 2).reshape(3, 2 * FH), ffn_conv_b=gfc[:, 3].reshape(1, 2 * FH),
        w_down=g_wdown)
    dmod = jnp.concatenate([vec_in[0:1], vec_in[1:2], vec_up[3:4], vec_up[0:1], vec_up[1:2], vec_dn[0:1]], axis=1)
    dmodc = jnp.concatenate([vec_ctx[0:1], vec_ctx[1:2]], axis=1)
    return loss, gx, grads, dmod, dmodc


_SMALL = ["c_ctx", "b_mod", "norm_mix_w", "q_norm_w", "k_norm_w", "sink_logit", "conv_w", "conv_b", "conv_norm_w",
          "conv_norm_b", "norm_ffn_w", "ffn_conv_w", "ffn_conv_b"]
_BIG = ["w_mod", "w_in", "w_out", "w_up", "w_down"]
_ORDER = ["c_ctx", "w_mod", "b_mod", "norm_mix_w", "w_in", "q_norm_w", "k_norm_w", "sink_logit", "conv_w", "conv_b",
          "conv_norm_w", "conv_norm_b", "w_out", "norm_ffn_w", "w_up", "ffn_conv_w", "ffn_conv_b", "w_down"]
_PACK = [("norm_mix_w", 1), ("norm_ffn_w", 1), ("q_norm_w", 1), ("k_norm_w", 1), ("sink_logit", 1), ("conv_b", 1),
         ("conv_norm_w", 1), ("conv_norm_b", 1), ("ffn_conv_b", 6), ("conv_w", 16), ("ffn_conv_w", 17)]
_PACK_ROWS = 56


def _pack_rows(a, nrows):
    flat = a.reshape(-1)
    return jnp.pad(flat, (0, nrows * D - flat.shape[0])).reshape(nrows, D)


def _flat128(a):
    return a.reshape(-1, 128)


def kernel(x, c, ctx, c_ctx, w_mod, b_mod, norm_mix_w, w_in, q_norm_w, k_norm_w, sink_logit, conv_w, conv_b, conv_norm_w, conv_norm_b, w_out, norm_ffn_w, w_up, ffn_conv_w, ffn_conv_b, w_down, loss_target, m_c_ctx, m_w_mod, m_b_mod, m_norm_mix_w, m_w_in, m_q_norm_w, m_k_norm_w, m_sink_logit, m_conv_w, m_conv_b, m_conv_norm_w, m_conv_norm_b, m_w_out, m_norm_ffn_w, m_w_up, m_ffn_conv_w, m_ffn_conv_b, m_w_down, v_c_ctx, v_w_mod, v_b_mod, v_norm_mix_w, v_w_in, v_q_norm_w, v_k_norm_w, v_sink_logit, v_conv_w, v_conv_b, v_conv_norm_w, v_conv_norm_b, v_w_out, v_norm_ffn_w, v_w_up, v_ffn_conv_w, v_ffn_conv_b, v_w_down):
    w = dict(c_ctx=c_ctx.reshape(1, D), w_mod=w_mod[0], b_mod=b_mod, norm_mix_w=norm_mix_w, w_in=w_in[0],
             q_norm_w=q_norm_w, k_norm_w=k_norm_w, sink_logit=sink_logit, conv_w=conv_w[0], conv_b=conv_b,
             conv_norm_w=conv_norm_w, conv_norm_b=conv_norm_b, w_out=w_out[0], norm_ffn_w=norm_ffn_w, w_up=w_up[0],
             ffn_conv_w=ffn_conv_w[0], ffn_conv_b=ffn_conv_b, w_down=w_down[0])
    m = dict(c_ctx=m_c_ctx.reshape(1, D), w_mod=m_w_mod[0], b_mod=m_b_mod, norm_mix_w=m_norm_mix_w, w_in=m_w_in[0],
             q_norm_w=m_q_norm_w, k_norm_w=m_k_norm_w, sink_logit=m_sink_logit, conv_w=m_conv_w[0], conv_b=m_conv_b,
             conv_norm_w=m_conv_norm_w, conv_norm_b=m_conv_norm_b, w_out=m_w_out[0], norm_ffn_w=m_norm_ffn_w,
             w_up=m_w_up[0], ffn_conv_w=m_ffn_conv_w[0], ffn_conv_b=m_ffn_conv_b, w_down=m_w_down[0])
    v = dict(c_ctx=v_c_ctx.reshape(1, D), w_mod=v_w_mod[0], b_mod=v_b_mod, norm_mix_w=v_norm_mix_w, w_in=v_w_in[0],
             q_norm_w=v_q_norm_w, k_norm_w=v_k_norm_w, sink_logit=v_sink_logit, conv_w=v_conv_w[0], conv_b=v_conv_b,
             conv_norm_w=v_conv_norm_w, conv_norm_b=v_conv_norm_b, w_out=v_w_out[0], norm_ffn_w=v_norm_ffn_w,
             w_up=v_w_up[0], ffn_conv_w=v_ffn_conv_w[0], ffn_conv_b=v_ffn_conv_b, w_down=v_w_down[0])
    xi, yi, ci = _position()
    chip = 2 * xi + yi
    dev = 2 * chip + ci
    s = x.shape[1]
    ncol = w["w_mod"].shape[1]

    blk0 = jnp.concatenate([_pack_rows(w["conv_w"], 4), _pack_rows(w["ffn_conv_w"], 5), c,
                            jnp.zeros((6, D), F32)], axis=0)
    g0 = _small_allgather(blk0, "gather_c_convw").reshape(8, 16, D)
    c_all = g0[:, 9, :]
    cs = jnp.concatenate([c_all, w["c_ctx"], jnp.zeros((7, D), F32)], axis=0)
    cw_full = jnp.concatenate([g0[2 * j, 0:4].reshape(-1)[:CK * 128].reshape(CK, 128) for j in range(4)], axis=1)
    fcw_full = jnp.concatenate([g0[2 * j, 4:9].reshape(-1)[:3 * 1408].reshape(3, 1408) for j in range(4)], axis=1)

    b_loc = lax.dynamic_slice(w["b_mod"], (0, chip * ncol), (1, ncol))
    modp = _mod_fwd(cs, w["w_mod"], b_loc)
    gm = _small_allgather(modp, "gather_mod").reshape(8, 16, ncol)
    mod_all = jnp.concatenate([gm[2 * j] for j in range(4)], axis=1)
    mod = lax.dynamic_slice(mod_all, (dev, 0), (1, 6 * D))
    modc = mod_all[8:9]

    payload = jnp.concatenate([_flat128(w["w_in"]), _flat128(w["w_out"]), _flat128(w["w_up"]), _flat128(w["w_down"])],
                              axis=0).astype(_MX).reshape(2, HROWS, 128)
    wg = _weight_allgather(payload).reshape(4, 2 * HROWS, 128)
    r0, r1, r2 = ROWS_WIN, ROWS_WIN + ROWS_WOUT, ROWS_WIN + ROWS_WOUT + ROWS_WUP
    win = jnp.concatenate([wg[j, :r0].reshape(D, INW // 4) for j in range(4)], axis=1)
    wout = jnp.concatenate([wg[j, r0:r1].reshape(D // 4, D) for j in range(4)], axis=0)
    wup = jnp.concatenate([wg[j, r1:r2].reshape(D, 2 * FH // 4) for j in range(4)], axis=1)
    wdown = jnp.concatenate([wg[j, r2:].reshape(FH // 4, D) for j in range(4)], axis=0)

    loss_loc, gx, gl, dmod, dmodc = _local_step(
        x[0], ctx[0], loss_target[0], mod, modc, w["norm_mix_w"], win, w["q_norm_w"], w["k_norm_w"], w["sink_logit"],
        cw_full, w["conv_b"], w["conv_norm_w"], w["conv_norm_b"], wout, w["norm_ffn_w"], wup, fcw_full,
        w["ffn_conv_b"], wdown)
    loss = lax.psum(loss_loc, ("x", "y", "c"))

    pack = jnp.concatenate([dmod.reshape(6, D), dmodc.reshape(2, D)]
                           + [_pack_rows(gl[name], nr) for name, nr in _PACK], axis=0)
    pack = jnp.pad(pack, ((0, _PACK_ROWS - pack.shape[0]), (0, 0)))
    gp = _small_allgather(pack, "gather_small_grads").reshape(8, _PACK_ROWS, D)
    tot = _sum_leading(gp, "sum_small_grads")
    g = {}
    r = 8
    for name, nr in _PACK:
        shape = gl[name].shape
        g[name] = tot[r:r + nr].reshape(-1)[:math.prod(shape)].reshape(shape)
        r += nr
    dmod_all = jnp.concatenate([gp[:, 0:6].reshape(8, 6 * D),
                                jnp.pad(tot[6:8].reshape(1, 2 * D), ((0, 0), (0, 4 * D))),
                                jnp.zeros((7, 6 * D), F32)], axis=0)
    g["b_mod"] = _sum_leading(dmod_all.reshape(16, 1, 6 * D), "grad_b_mod")
    g["w_mod"], part = _mod_bwd(cs, lax.dynamic_slice(dmod_all, (0, chip * ncol), (16, ncol)), w["w_mod"])
    parts = _small_allgather(part, "gather_cctx").reshape(8, 16, D)
    g["c_ctx"] = _cctx_grad(parts[0::2], w["c_ctx"])
    g["conv_w"] = lax.dynamic_slice(g["conv_w"], (0, chip * 128), (CK, 128))
    g["ffn_conv_w"] = lax.dynamic_slice(g["ffn_conv_w"], (0, chip * 1408), (3, 1408))

    def shard_rows(j):
        return jnp.concatenate([
            _flat128(gl["w_in"][:, 448 * j:448 * (j + 1)]), _flat128(gl["w_out"][256 * j:256 * (j + 1)]),
            _flat128(gl["w_up"][:, 1408 * j:1408 * (j + 1)]), _flat128(gl["w_down"][704 * j:704 * (j + 1)])], axis=0)

    gsend = jnp.stack([shard_rows(j) for j in range(4)]).astype(_MX).reshape(4, 2, HROWS, 128).transpose(1, 0, 2, 3)
    mine = lax.dynamic_index_in_dim(gsend, ci, 0, keepdims=False)
    other = lax.dynamic_index_in_dim(gsend, 1 - ci, 0, keepdims=False)
    from_sib = _sibling_swap(other, "grad_swap_halves")
    chip_part = _pair_sum_bf16(mine, from_sib, "grad_pair_sum")
    arrived = _chip_exchange(chip_part)
    ghalf = _chip_sum(arrived, "grad_chip_sum")
    sib_half = _sibling_swap(ghalf, "grad_share_half")
    both = jnp.stack([ghalf, sib_half])
    gflat = jnp.where(ci == 0, both, both[::-1]).reshape(2 * HROWS, 128)
    g["w_in"] = gflat[:r0].reshape(D, INW // 4)
    g["w_out"] = gflat[r0:r1].reshape(D // 4, D)
    g["w_up"] = gflat[r1:r2].reshape(D, 2 * FH // 4)
    g["w_down"] = gflat[r2:].reshape(FH // 4, D)

    delta, new_m, new_v = {}, {}, {}
    for name in _BIG:
        delta[name], new_m[name], new_v[name] = _adam_big(w[name], g[name], m[name], v[name], "adam_" + name)
    ds, ms, vs = _adam_small([w[n] for n in _SMALL], [g[n] for n in _SMALL], [m[n] for n in _SMALL],
                             [v[n] for n in _SMALL])
    for k, name in enumerate(_SMALL):
        delta[name], new_m[name], new_v[name] = ds[k], ms[k], vs[k]

    def shaped(d, name):
        a = d[name]
        if name == "c_ctx":
            return a.reshape(D)
        if name in ("w_mod", "w_in", "w_out", "w_up", "w_down", "conv_w", "ffn_conv_w"):
            return a[None]
        return a

    outs = [loss, gx[None]]
    for d in (g, delta, new_m, new_v):
        outs += [shaped(d, name) for name in _ORDER]
    return tuple(outs)
```

```python
import functools
import math

import jax
import jax.numpy as jnp
from jax import lax
from jax.experimental import pallas as pl
from jax.experimental.pallas import tpu as pltpu

F32 = jnp.float32
_MX = jnp.bfloat16
EPS = 1e-6
NEG = -1e30
D = 1024
AW = 512
CW = 512
INW = 1792
FH = 2816
LC = 256
CK = 31
GRID_W = 64
TM = 512
TQ = 128
NCH = FH // 128
FQ = 2 * FH // 4
NBIG = 4
VMEM_LIMIT = 56 * 1024 * 1024
MESH = pl.DeviceIdType.MESH

ADAM_LR, ADAM_B1, ADAM_B2, ADAM_EPS, ADAM_WD, ADAM_STEP = 0.001, 0.9, 0.999, 1e-08, 0.01, 10


def _cp(*sem):
    return pltpu.CompilerParams(dimension_semantics=sem, vmem_limit_bytes=VMEM_LIMIT)


def _vspec():
    return pl.BlockSpec(memory_space=pltpu.VMEM)


def _sig(z):
    return 1.0 / (1.0 + jnp.exp(-z))


def _dot(a, b):
    return jnp.dot(a, b, preferred_element_type=F32)


def _dot_nt(a, b):
    return lax.dot_general(a, b, (((1,), (1,)), ((), ())), preferred_element_type=F32)


def _dot_tn(a, b):
    return lax.dot_general(a, b, (((0,), (0,)), ((), ())), preferred_element_type=F32)


def _gsum(v, g):
    hi = v.astype(_MX)
    lo = (v - hi.astype(F32)).astype(_MX)
    return _dot(hi, g) + _dot(lo, g)


def _headnorm(ch, w, g):
    r = lax.rsqrt(_gsum(ch * ch, g) * (1.0 / 64.0) + EPS)
    return ch * r * w


def _headnorm_bwd(ch, dy, w, g):
    r = lax.rsqrt(_gsum(ch * ch, g) * (1.0 / 64.0) + EPS)
    hat = ch * r
    dhat = dy * w
    dch = r * (dhat - hat * (_gsum(dhat * hat, g) * (1.0 / 64.0)))
    return dch, dy * hat


def _rope(u, cos, sa, sb):
    return u * cos + pltpu.roll(u, 112, 1) * sa + pltpu.roll(u, 16, 1) * sb


def _rope_bwd(d, cos, sa, sb):
    return d * cos - pltpu.roll(d, 112, 1) * sa - pltpu.roll(d, 16, 1) * sb


def _rowsum(v):
    return jnp.sum(v, axis=0, keepdims=True)


def _mix_in(x, nw, sc, sh, win, qnw2, knw2, cos, sa, sb, gmat):
    s = x.shape[0]

    def body(x_ref, nw_ref, sc_ref, sh_ref, win_ref, qnw_ref, knw_ref, cos_ref, sa_ref, sb_ref, g_ref,
             proj_ref, h_ref, q0_ref, q1_ref, k_ref, ksw_ref, v_ref, vsw_ref, u0_ref):
        xv = x_ref[...]
        r = lax.rsqrt(jnp.mean(xv * xv, axis=-1, keepdims=True) + EPS)
        h = xv * r * (nw_ref[...] * (1.0 + sc_ref[...])) + sh_ref[...]
        hb = h.astype(_MX)
        h_ref[...] = hb
        proj = _dot(hb, win_ref[...])
        proj_ref[...] = proj
        cs, sav, sbv, g = cos_ref[...], sa_ref[...], sb_ref[...], g_ref[...]
        lo = lax.broadcasted_iota(jnp.int32, (1, 128), 1) < 64
        for j in range(4):
            q = _rope(_headnorm(proj[:, 128 * j:128 * (j + 1)], qnw_ref[...], g), cs, sav, sbv) * 0.125
            q0_ref[:, 128 * j:128 * (j + 1)] = jnp.where(lo, q, 0.0).astype(_MX)
            q1_ref[:, 128 * j:128 * (j + 1)] = jnp.where(lo, 0.0, q).astype(_MX)
        k = _rope(_headnorm(proj[:, 512:640], knw_ref[...], g), cs, sav, sbv)
        k_ref[...] = k.astype(_MX)
        ksw_ref[...] = pltpu.roll(k, 64, 1).astype(_MX)
        v = proj[:, 640:768]
        v_ref[...] = v.astype(_MX)
        vsw_ref[...] = pltpu.roll(v, 64, 1).astype(_MX)
        u0_ref[...] = proj[:, 768:1280] * _sig(proj[:, 1280:1792])

    row = lambda n: pl.BlockSpec((1, n), lambda i: (0, 0))
    tile = lambda n: pl.BlockSpec((TM, n), lambda i: (i, 0))
    return pl.pallas_call(
        body, name="mix_in", grid=(s // TM,),
        in_specs=[tile(D), row(D), row(D), row(D), pl.BlockSpec((D, INW), lambda i: (0, 0)), row(128), row(128),
                  tile(128), tile(128), tile(128), pl.BlockSpec((128, 128), lambda i: (0, 0))],
        out_specs=[tile(INW), tile(D), tile(AW), tile(AW), tile(128), tile(128), tile(128), tile(128), tile(CW)],
        out_shape=[jax.ShapeDtypeStruct((s, INW), F32), jax.ShapeDtypeStruct((s, D), _MX),
                   jax.ShapeDtypeStruct((s, AW), _MX), jax.ShapeDtypeStruct((s, AW), _MX),
                   jax.ShapeDtypeStruct((s, 128), _MX), jax.ShapeDtypeStruct((s, 128), _MX),
                   jax.ShapeDtypeStruct((s, 128), _MX), jax.ShapeDtypeStruct((s, 128), _MX),
                   jax.ShapeDtypeStruct((s, CW), F32)],
        compiler_params=_cp("arbitrary"),
    )(x, nw, sc, sh, win, qnw2, knw2, cos, sa, sb, gmat)


def _ctx_kv(ctx, nw, scc, shc, winkv, knw2, gmat):
    def body(ctx_ref, nw_ref, sc_ref, sh_ref, w_ref, knw_ref, g_ref,
             kvc_ref, hc_ref, kc_ref, kcsw_ref, vc_ref, vcsw_ref):
        cv = ctx_ref[...]
        r = lax.rsqrt(jnp.mean(cv * cv, axis=-1, keepdims=True) + EPS)
        hc = (cv * r * (nw_ref[...] * (1.0 + sc_ref[...])) + sh_ref[...]).astype(_MX)
        hc_ref[...] = hc
        kvc = _dot(hc, w_ref[...])
        kvc_ref[...] = kvc
        kc = _headnorm(kvc[:, :128], knw_ref[...], g_ref[...])
        kc_ref[...] = kc.astype(_MX)
        kcsw_ref[...] = pltpu.roll(kc, 64, 1).astype(_MX)
        vc = kvc[:, 128:]
        vc_ref[...] = vc.astype(_MX)
        vcsw_ref[...] = pltpu.roll(vc, 64, 1).astype(_MX)

    return pl.pallas_call(
        body, name="ctx_kv",
        in_specs=[_vspec()] * 7, out_specs=[_vspec()] * 6,
        out_shape=[jax.ShapeDtypeStruct((LC, 256), F32), jax.ShapeDtypeStruct((LC, D), _MX)]
        + [jax.ShapeDtypeStruct((LC, 128), _MX)] * 4,
        compiler_params=pltpu.CompilerParams(vmem_limit_bytes=VMEM_LIMIT),
    )(ctx, nw, scc, shc, winkv, knw2, gmat)


def _attn_mask(i, s):
    r = lax.broadcasted_iota(jnp.int32, (2 * TQ, LC + 3 * TQ), 0) % TQ
    cidx = lax.broadcasted_iota(jnp.int32, (2 * TQ, LC + 3 * TQ), 1)
    qpos = i * TQ + r
    kpos = (i - 1) * TQ + (cidx - LC)
    near = (jnp.abs(qpos - kpos) <= 128) & (kpos >= 0) & (kpos < s)
    return (cidx < LC) | near


def _attn_probs(qm, kge, mask, sinkv):
    sc = _dot_nt(qm, kge)
    sc = jnp.where(mask, sc, NEG)
    m = jnp.maximum(jnp.max(sc, axis=-1, keepdims=True), sinkv)
    ex = jnp.exp(sc - m)
    es = jnp.exp(sinkv - m)
    inv = 1.0 / (jnp.sum(ex, axis=-1, keepdims=True) + es)
    return ex * inv, es * inv


def _sink_rows(sink_ref, g, e):
    return jnp.concatenate([jnp.full((TQ, 1), sink_ref[4 * g + e], F32),
                            jnp.full((TQ, 1), sink_ref[4 * g + 2 + e], F32)], axis=0)


def _attn_fwd(q0, q1, kp, kswp, vp, vswp, kc, kcsw, vc, vcsw, sink):
    s = q0.shape[0]

    def body(q0_ref, q1_ref, kp_ref, kswp_ref, vp_ref, vswp_ref, kc_ref, kcsw_ref, vc_ref, vcsw_ref, sink_ref,
             o_ref):
        i = pl.program_id(0)
        st = pl.multiple_of(i * TQ, TQ)
        kall = (jnp.concatenate([kc_ref[...], kp_ref[pl.ds(st, 3 * TQ), :]], axis=0),
                jnp.concatenate([kcsw_ref[...], kswp_ref[pl.ds(st, 3 * TQ), :]], axis=0))
        vall = (jnp.concatenate([vc_ref[...], vp_ref[pl.ds(st, 3 * TQ), :]], axis=0),
                jnp.concatenate([vcsw_ref[...], vswp_ref[pl.ds(st, 3 * TQ), :]], axis=0))
        mask = _attn_mask(i, s)
        lo = lax.broadcasted_iota(jnp.int32, (1, 128), 1) < 64
        qrefs = (q0_ref, q1_ref)
        out = [jnp.zeros((TQ, 128), F32) for _ in range(4)]
        for g in range(2):
            for e in range(2):
                me = lo if e == 0 else jnp.logical_not(lo)
                qm = jnp.concatenate([qrefs[e][:, 256 * g:256 * g + 128],
                                      qrefs[e][:, 256 * g + 128:256 * g + 256]], axis=0)
                sw = 0 if e == g else 1
                p, _ = _attn_probs(qm, kall[sw], mask, _sink_rows(sink_ref, g, e))
                o2 = _dot(p.astype(_MX), vall[sw])
                out[2 * g] = out[2 * g] + jnp.where(me, o2[:TQ], 0.0)
                out[2 * g + 1] = out[2 * g + 1] + jnp.where(me, o2[TQ:], 0.0)
        for j in range(4):
            o_ref[:, 128 * j:128 * (j + 1)] = out[j]

    full = lambda a: pl.BlockSpec(a.shape, lambda i: (0, 0))
    qs = pl.BlockSpec((TQ, AW), lambda i: (i, 0))
    return pl.pallas_call(
        body, name="attn_fwd", grid=(s // TQ,),
        in_specs=[qs, qs, full(kp), full(kp), full(kp), full(kp), full(kc), full(kc), full(kc), full(kc),
                  pl.BlockSpec(memory_space=pltpu.SMEM)],
        out_specs=qs, out_shape=jax.ShapeDtypeStruct((s, AW), F32),
        compiler_params=_cp("arbitrary"),
    )(q0, q1, kp, kswp, vp, vswp, kc, kcsw, vc, vcsw, sink)


def _conv31(u0, cw32, cb):
    s = u0.shape[0]
    rch = 256

    def body(u_ref, w_ref, b_ref, o_ref, pad_ref):
        pad_ref[0:16, :] = jnp.zeros((16, 128), F32)
        pad_ref[s + 16:s + 32, :] = jnp.zeros((16, 128), F32)
        pad_ref[16:s + 16, :] = u_ref[...]
        for cidx in range(s // rch):
            base = cidx * rch
            acc = jnp.zeros((rch, 128), F32) + b_ref[...]
            for j in range(CK):
                acc = acc + w_ref[j:j + 1, :] * pad_ref[base + j + 1:base + j + 1 + rch, :]
            o_ref[base:base + rch, :] = acc

    return pl.pallas_call(
        body, name="conv31", grid=(CW // 128,),
        in_specs=[pl.BlockSpec((s, 128), lambda c: (0, c)), pl.BlockSpec((32, 128), lambda c: (0, c)),
                  pl.BlockSpec((1, 128), lambda c: (0, c))],
        out_specs=pl.BlockSpec((s, 128), lambda c: (0, c)),
        out_shape=jax.ShapeDtypeStruct((s, CW), F32),
        scratch_shapes=[pltpu.VMEM((s + 32, 128), F32)],
        compiler_params=_cp("arbitrary"),
    )(u0, cw32, cb)


def _ln_stats(u1):
    mu = jnp.mean(u1, axis=-1, keepdims=True)
    xc = u1 - mu
    rstd = lax.rsqrt(jnp.mean(xc * xc, axis=-1, keepdims=True) + EPS)
    return xc * rstd, rstd


def _mix_out(o, u1, cnw, cnb, wout, x, g1):
    s = x.shape[0]

    def body(o_ref, u1_ref, cnw_ref, cnb_ref, w_ref, x_ref, g1_ref, x1_ref, mix_ref, cat_ref):
        u2n, _ = _ln_stats(u1_ref[...])
        u2 = u2n * cnw_ref[...] + cnb_ref[...]
        u3 = u2 * _sig(u2)
        cat = jnp.concatenate([o_ref[...], u3], axis=1).astype(_MX)
        cat_ref[...] = cat
        mix = _dot(cat, w_ref[...])
        mix_ref[...] = mix
        x1_ref[...] = x_ref[...] + g1_ref[...] * mix

    row = lambda n: pl.BlockSpec((1, n), lambda i: (0, 0))
    tile = lambda n: pl.BlockSpec((TM, n), lambda i: (i, 0))
    return pl.pallas_call(
        body, name="mix_out", grid=(s // TM,),
        in_specs=[tile(AW), tile(CW), row(CW), row(CW), pl.BlockSpec((D, D), lambda i: (0, 0)), tile(D), row(D)],
        out_specs=[tile(D), tile(D), tile(D)],
        out_shape=[jax.ShapeDtypeStruct((s, D), F32), jax.ShapeDtypeStruct((s, D), F32),
                   jax.ShapeDtypeStruct((s, D), _MX)],
        compiler_params=_cp("arbitrary"),
    )(o, u1, cnw, cnb, wout, x, g1)


def _ffn_up(x1, nw, sc, sh, wup4):
    s = x1.shape[0]
    tm = min(1024, s)

    def body(x_ref, nw_ref, sc_ref, sh_ref, w_ref, h2_ref, up_ref, h2s):
        @pl.when(pl.program_id(1) == 0)
        def _():
            xv = x_ref[...]
            r = lax.rsqrt(jnp.mean(xv * xv, axis=-1, keepdims=True) + EPS)
            h2 = (xv * r * (nw_ref[...] * (1.0 + sc_ref[...])) + sh_ref[...]).astype(_MX)
            h2s[...] = h2
            h2_ref[...] = h2
        up_ref[...] = _dot(h2s[...], w_ref[...])

    row = pl.BlockSpec((1, D), lambda i, j: (0, 0))
    return pl.pallas_call(
        body, name="ffn_up", grid=(s // tm, 4),
        in_specs=[pl.BlockSpec((tm, D), lambda i, j: (i, 0)), row, row, row,
                  pl.BlockSpec((None, D, FQ), lambda i, j: (j, 0, 0))],
        out_specs=[pl.BlockSpec((tm, D), lambda i, j: (i, 0)),
                   pl.BlockSpec((None, tm, FQ), lambda i, j: (j // 2, i, j % 2))],
        out_shape=[jax.ShapeDtypeStruct((s, D), _MX), jax.ShapeDtypeStruct((2, s, FH), F32)],
        scratch_shapes=[pltpu.VMEM((tm, D), _MX)],
        compiler_params=_cp("arbitrary", "arbitrary"),
    )(x1, nw, sc, sh, wup4)


def _fill_pad8(pad_ref, val, s):
    pad_ref[0:8, :] = jnp.zeros((8, 128), F32)
    pad_ref[s + 8:s + 16, :] = jnp.zeros((8, 128), F32)
    pad_ref[8:s + 8, :] = val


def _conv3_at(pad_ref, w_ref, half, base, rch):
    return (w_ref[half, 0:1, :] * pad_ref[base + 7:base + 7 + rch, :]
            + w_ref[half, 1:2, :] * pad_ref[base + 8:base + 8 + rch, :]
            + w_ref[half, 2:3, :] * pad_ref[base + 9:base + 9 + rch, :])


def _ffn_conv_act(up0, fcw, fcb):
    s = up0.shape[1]
    rch = 256

    def body(up_ref, w_ref, b_ref, act_ref, padg, padv):
        _fill_pad8(padg, up_ref[0], s)
        _fill_pad8(padv, up_ref[1], s)
        for cidx in range(s // rch):
            base = cidx * rch
            gate = _conv3_at(padg, w_ref, 0, base, rch) + b_ref[0]
            val = _conv3_at(padv, w_ref, 1, base, rch) + b_ref[1]
            act_ref[base:base + rch, :] = (gate * _sig(gate) * val).astype(_MX)

    return pl.pallas_call(
        body, name="ffn_conv_act", grid=(NCH,),
        in_specs=[pl.BlockSpec((2, s, 128), lambda c: (0, 0, c)), pl.BlockSpec((2, 8, 128), lambda c: (0, 0, c)),
                  pl.BlockSpec((2, 1, 128), lambda c: (0, 0, c))],
        out_specs=pl.BlockSpec((s, 128), lambda c: (0, c)),
        out_shape=jax.ShapeDtypeStruct((s, FH), _MX),
        scratch_shapes=[pltpu.VMEM((s + 16, 128), F32)] * 2,
        compiler_params=_cp("arbitrary"),
    )(up0, fcw, fcb)


def _ffn_down(act, wdown, x1, g2, tgt):
    s = x1.shape[0]

    def body(act_ref, w_ref, x1_ref, g2_ref, tgt_ref, dy_ref, dact_ref, ddn_ref, vec_ref):
        @pl.when(pl.program_id(0) == 0)
        def _():
            vec_ref[...] = jnp.zeros((8, D), F32)
        dn = _dot(act_ref[...], w_ref[...])
        diff = x1_ref[...] + g2_ref[...] * dn - tgt_ref[...]
        dy = diff * (1.0 / D)
        dy_ref[...] = dy
        ddn = (dy * g2_ref[...]).astype(_MX)
        ddn_ref[...] = ddn
        dact_ref[...] = _dot_nt(ddn, w_ref[...])
        vec_ref[0:1, :] += _rowsum(dy * dn)
        vec_ref[1:2, :] += _rowsum(diff * diff)

    tile = lambda n: pl.BlockSpec((TM, n), lambda i: (i, 0))
    return pl.pallas_call(
        body, name="ffn_down", grid=(s // TM,),
        in_specs=[tile(FH), pl.BlockSpec((FH, D), lambda i: (0, 0)), tile(D), pl.BlockSpec((1, D), lambda i: (0, 0)),
                  tile(D)],
        out_specs=[tile(D), tile(FH), tile(D), pl.BlockSpec((8, D), lambda i: (0, 0))],
        out_shape=[jax.ShapeDtypeStruct((s, D), F32), jax.ShapeDtypeStruct((s, FH), F32),
                   jax.ShapeDtypeStruct((s, D), _MX), jax.ShapeDtypeStruct((8, D), F32)],
        compiler_params=_cp("arbitrary"),
    )(act, wdown, x1, g2, tgt)


def _ffn_conv_bwd(up0, dact, fcw, fcb):
    s = up0.shape[1]
    rch = 256

    def body(up_ref, da_ref, w_ref, b_ref, dup_ref, gw_ref, padg, padv, dpg, dpv):
        _fill_pad8(padg, up_ref[0], s)
        _fill_pad8(padv, up_ref[1], s)
        for p in (dpg, dpv):
            p[0:8, :] = jnp.zeros((8, 128), F32)
            p[s + 8:s + 16, :] = jnp.zeros((8, 128), F32)
        acc = [[jnp.zeros((1, 128), F32) for _ in range(4)] for _ in range(2)]
        for cidx in range(s // rch):
            base = cidx * rch
            gate = _conv3_at(padg, w_ref, 0, base, rch) + b_ref[0]
            val = _conv3_at(padv, w_ref, 1, base, rch) + b_ref[1]
            da = da_ref[base:base + rch, :]
            sg = _sig(gate)
            dgate = da * val * sg * (1.0 + gate * (1.0 - sg))
            dval = da * gate * sg
            dpg[base + 8:base + 8 + rch, :] = dgate
            dpv[base + 8:base + 8 + rch, :] = dval
            for half, (dd, pad) in enumerate(((dgate, padg), (dval, padv))):
                for j in range(3):
                    acc[half][j] = acc[half][j] + _rowsum(dd * pad[base + 7 + j:base + 7 + j + rch, :])
                acc[half][3] = acc[half][3] + _rowsum(dd)
        for half in range(2):
            gw_ref[half] = jnp.zeros((8, 128), F32)
            for j in range(4):
                gw_ref[half, j:j + 1, :] = acc[half][j]
        for cidx in range(s // rch):
            base = cidx * rch
            for half, dp in enumerate((dpg, dpv)):
                dup_ref[half, base:base + rch, :] = (
                    w_ref[half, 0:1, :] * dp[base + 9:base + 9 + rch, :]
                    + w_ref[half, 1:2, :] * dp[base + 8:base + 8 + rch, :]
                    + w_ref[half, 2:3, :] * dp[base + 7:base + 7 + rch, :]).astype(_MX)

    return pl.pallas_call(
        body, name="ffn_conv_bwd", grid=(NCH,),
        in_specs=[pl.BlockSpec((2, s, 128), lambda c: (0, 0, c)), pl.BlockSpec((s, 128), lambda c: (0, c)),
                  pl.BlockSpec((2, 8, 128), lambda c: (0, 0, c)), pl.BlockSpec((2, 1, 128), lambda c: (0, 0, c))],
        out_specs=[pl.BlockSpec((2, s, 128), lambda c: (0, 0, c)), pl.BlockSpec((2, 8, 128), lambda c: (0, 0, c))],
        out_shape=[jax.ShapeDtypeStruct((2, s, FH), _MX), jax.ShapeDtypeStruct((2, 8, FH), F32)],
        scratch_shapes=[pltpu.VMEM((s + 16, 128), F32)] * 4,
        compiler_params=_cp("arbitrary"),
    )(up0, dact, fcw, fcb)


def _tn_matmul(a, b, tm, tn, name, b_split=False, by_chip=False, init=None):
    s, m = a.shape
    n = 2 * b.shape[2] if b_split else b.shape[1]
    ts = min(1024, s)
    nsteps = s // ts
    npb = (n // 2) // tn if b_split else None

    def body(*refs):
        if init is None:
            a_ref, b_ref, o_ref, acc = refs
        else:
            a_ref, b_ref, i_ref, o_ref, acc = refs
        k = pl.program_id(2)

        @pl.when(k == 0)
        def _():
            acc[...] = jnp.zeros((tm, tn), F32)
            if init is not None:
                acc[:, 512:768] = i_ref[...]
        acc[...] += _dot_tn(a_ref[...], b_ref[...])

        @pl.when(k == nsteps - 1)
        def _():
            o_ref[...] = acc[...].astype(_MX)

    if b_split:
        bspec = pl.BlockSpec((None, ts, tn), lambda i, j, k: (j // npb, k, j % npb))
    else:
        bspec = pl.BlockSpec((ts, tn), lambda i, j, k: (k, j))
    in_specs = [pl.BlockSpec((ts, tm), lambda i, j, k: (k, i)), bspec]
    args = [a, b]
    if init is not None:
        in_specs.append(pl.BlockSpec((tm, 256), lambda i, j, k: (i, 0)))
        args.append(init)
    if by_chip:
        out_spec = pl.BlockSpec((None, tm, tn), lambda i, j, k: (j, i, 0))
        out_shape = jax.ShapeDtypeStruct((n // tn, m, tn), _MX)
    else:
        out_spec = pl.BlockSpec((tm, tn), lambda i, j, k: (i, j))
        out_shape = jax.ShapeDtypeStruct((m, n), _MX)
    return pl.pallas_call(
        body, name=name, grid=(m // tm, n // tn, nsteps),
        in_specs=in_specs, out_specs=out_spec, out_shape=out_shape,
        scratch_shapes=[pltpu.VMEM((tm, tn), F32)],
        compiler_params=_cp("arbitrary", "arbitrary", "arbitrary"),
    )(*args)


def _ffn_up_bwd(dup, wup4, x1, dy, mix, nw, sc, g1):
    s = x1.shape[0]
    nk = 4

    def body(dup_ref, w_ref, x1_ref, dy_ref, mix_ref, nw_ref, sc_ref, g1_ref, dx1_ref, dmix_ref, vec_ref, acc):
        i, k = pl.program_id(0), pl.program_id(1)

        @pl.when(k == 0)
        def _():
            acc[...] = jnp.zeros((TM, D), F32)

        @pl.when((k == 0) & (i == 0))
        def _():
            vec_ref[...] = jnp.zeros((8, D), F32)
        acc[...] += _dot_nt(dup_ref[...], w_ref[...])

        @pl.when(k == nk - 1)
        def _():
            dh = acc[...]
            xv = x1_ref[...]
            r = lax.rsqrt(jnp.mean(xv * xv, axis=-1, keepdims=True) + EPS)
            xn = xv * r
            nwv, scv = nw_ref[...], sc_ref[...]
            vec_ref[0:1, :] += _rowsum(dh)
            vec_ref[1:2, :] += _rowsum(dh * xn) * nwv
            vec_ref[2:3, :] += _rowsum(dh * xn) * (1.0 + scv)
            dxn = dh * (nwv * (1.0 + scv))
            dx1 = dy_ref[...] + r * (dxn - xn * jnp.mean(dxn * xn, axis=-1, keepdims=True))
            dx1_ref[...] = dx1
            vec_ref[3:4, :] += _rowsum(dx1 * mix_ref[...])
            dmix_ref[...] = (dx1 * g1_ref[...]).astype(_MX)

    tile = pl.BlockSpec((TM, D), lambda i, k: (i, 0))
    row = pl.BlockSpec((1, D), lambda i, k: (0, 0))
    return pl.pallas_call(
        body, name="ffn_up_bwd", grid=(s // TM, nk),
        in_specs=[pl.BlockSpec((None, TM, FQ), lambda i, k: (k // 2, i, k % 2)),
                  pl.BlockSpec((None, D, FQ), lambda i, k: (k, 0, 0)), tile, tile, tile, row, row, row],
        out_specs=[tile, tile, pl.BlockSpec((8, D), lambda i, k: (0, 0))],
        out_shape=[jax.ShapeDtypeStruct((s, D), F32), jax.ShapeDtypeStruct((s, D), _MX),
                   jax.ShapeDtypeStruct((8, D), F32)],
        scratch_shapes=[pltpu.VMEM((TM, D), F32)],
        compiler_params=_cp("arbitrary", "arbitrary"),
    )(dup, wup4, x1, dy, mix, nw, sc, g1)


def _mix_out_bwd(dmix, wout, u1, cnw, cnb):
    s = u1.shape[0]

    def body(dm_ref, w_ref, u1_ref, cnw_ref, cnb_ref, do_ref, du1_ref, vec_ref):
        @pl.when(pl.program_id(0) == 0)
        def _():
            vec_ref[...] = jnp.zeros((8, CW), F32)
        dcat = _dot_nt(dm_ref[...], w_ref[...])
        do_ref[...] = dcat[:, :AW]
        du3 = dcat[:, AW:]
        u2n, rstd = _ln_stats(u1_ref[...])
        u2 = u2n * cnw_ref[...] + cnb_ref[...]
        sg = _sig(u2)
        du2 = du3 * sg * (1.0 + u2 * (1.0 - sg))
        vec_ref[0:1, :] += _rowsum(du2)
        vec_ref[1:2, :] += _rowsum(du2 * u2n)
        d2n = du2 * cnw_ref[...]
        du1_ref[...] = rstd * (d2n - jnp.mean(d2n, axis=-1, keepdims=True)
                               - u2n * jnp.mean(d2n * u2n, axis=-1, keepdims=True))

    row = lambda n: pl.BlockSpec((1, n), lambda i: (0, 0))
    tile = lambda n: pl.BlockSpec((TM, n), lambda i: (i, 0))
    return pl.pallas_call(
        body, name="mix_out_bwd", grid=(s // TM,),
        in_specs=[tile(D), pl.BlockSpec((D, D), lambda i: (0, 0)), tile(CW), row(CW), row(CW)],
        out_specs=[tile(AW), tile(CW), pl.BlockSpec((8, CW), lambda i: (0, 0))],
        out_shape=[jax.ShapeDtypeStruct((s, AW), F32), jax.ShapeDtypeStruct((s, CW), F32),
                   jax.ShapeDtypeStruct((8, CW), F32)],
        compiler_params=_cp("arbitrary"),
    )(dmix, wout, u1, cnw, cnb)


def _conv31_bwd(du1, u0, cw32, proj):
    s = u0.shape[0]
    rch = 256

    def body(d_ref, u_ref, w_ref, ga_ref, gb_ref, dga_ref, dgb_ref, gw_ref, padu, padd):
        for p, src in ((padu, u_ref), (padd, d_ref)):
            p[0:16, :] = jnp.zeros((16, 128), F32)
            p[s + 16:s + 32, :] = jnp.zeros((16, 128), F32)
            p[16:s + 16, :] = src[...]
        for j in range(CK):
            acc = jnp.zeros((1, 128), F32)
            for cidx in range(s // rch):
                base = cidx * rch
                acc = acc + _rowsum(d_ref[base:base + rch, :] * padu[base + j + 1:base + j + 1 + rch, :])
            gw_ref[j:j + 1, :] = acc
        gw_ref[CK:CK + 1, :] = _rowsum(d_ref[...])
        for cidx in range(s // rch):
            base = cidx * rch
            du0 = jnp.zeros((rch, 128), F32)
            for j in range(CK):
                du0 = du0 + w_ref[j:j + 1, :] * padd[base + 31 - j:base + 31 - j + rch, :]
            sg = _sig(gb_ref[base:base + rch, :])
            ga = ga_ref[base:base + rch, :]
            dga_ref[base:base + rch, :] = (du0 * sg).astype(_MX)
            dgb_ref[base:base + rch, :] = (du0 * ga * sg * (1.0 - sg)).astype(_MX)

    blk = lambda off: pl.BlockSpec((s, 128), lambda c: (0, c + off))
    return pl.pallas_call(
        body, name="conv31_bwd", grid=(CW // 128,),
        in_specs=[blk(0), blk(0), pl.BlockSpec((32, 128), lambda c: (0, c)), blk(6), blk(10)],
        out_specs=[blk(0), blk(0), pl.BlockSpec((32, 128), lambda c: (0, c))],
        out_shape=[jax.ShapeDtypeStruct((s, CW), _MX), jax.ShapeDtypeStruct((s, CW), _MX),
                   jax.ShapeDtypeStruct((32, CW), F32)],
        scratch_shapes=[pltpu.VMEM((s + 32, 128), F32)] * 2,
        compiler_params=_cp("arbitrary"),
    )(du1, u0, cw32, proj, proj)


def _attn_bwd(q0, q1, kp, kswp, vp, vswp, kc, kcsw, vc, vcsw, sink, o, do):
    s = q0.shape[0]
    nb = s // TQ

    def body(q0_ref, q1_ref, kp_ref, kswp_ref, vp_ref, vswp_ref, kc_ref, kcsw_ref, vc_ref, vcsw_ref, sink_ref,
             o_ref, do_ref, dq_ref, dk_ref, dv_ref, dkc_ref, dvc_ref, dsink_ref):
        i = pl.program_id(0)

        @pl.when(i == 0)
        def _():
            dk_ref[...] = jnp.zeros((s + 2 * TQ, 128), F32)
            dv_ref[...] = jnp.zeros((s + 2 * TQ, 128), F32)
            dkc_ref[...] = jnp.zeros((LC, 128), F32)
            dvc_ref[...] = jnp.zeros((LC, 128), F32)
            dsink_ref[...] = jnp.zeros((8, 128), F32)
        st = pl.multiple_of(i * TQ, TQ)
        kall = (jnp.concatenate([kc_ref[...], kp_ref[pl.ds(st, 3 * TQ), :]], axis=0),
                jnp.concatenate([kcsw_ref[...], kswp_ref[pl.ds(st, 3 * TQ), :]], axis=0))
        vall = (jnp.concatenate([vc_ref[...], vp_ref[pl.ds(st, 3 * TQ), :]], axis=0),
                jnp.concatenate([vcsw_ref[...], vswp_ref[pl.ds(st, 3 * TQ), :]], axis=0))
        mask = _attn_mask(i, s)
        lo = lax.broadcasted_iota(jnp.int32, (1, 128), 1) < 64
        qrefs = (q0_ref, q1_ref)
        dq = [jnp.zeros((TQ, 128), F32) for _ in range(4)]
        dkt = jnp.zeros((LC + 3 * TQ, 128), F32)
        dvt = jnp.zeros((LC + 3 * TQ, 128), F32)
        for g in range(2):
            for e in range(2):
                me = lo if e == 0 else jnp.logical_not(lo)
                c0, c1 = 256 * g, 256 * g + 128
                qm = jnp.concatenate([qrefs[e][:, c0:c0 + 128], qrefs[e][:, c1:c1 + 128]], axis=0)
                sw = 0 if e == g else 1
                p, ps = _attn_probs(qm, kall[sw], mask, _sink_rows(sink_ref, g, e))
                dom = jnp.where(me, jnp.concatenate([do_ref[:, c0:c0 + 128], do_ref[:, c1:c1 + 128]], axis=0), 0.0)
                o2 = jnp.concatenate([o_ref[:, c0:c0 + 128], o_ref[:, c1:c1 + 128]], axis=0)
                dd = jnp.sum(dom * o2, axis=-1, keepdims=True)
                domx = dom.astype(_MX)
                dp = _dot_nt(domx, vall[sw])
                ds = (p * (dp - dd)).astype(_MX)
                sd = ps * dd
                dsink_ref[4 * g + e:4 * g + e + 1, :] -= jnp.sum(sd[:TQ], axis=0, keepdims=True)
                dsink_ref[4 * g + 2 + e:4 * g + 3 + e, :] -= jnp.sum(sd[TQ:], axis=0, keepdims=True)
                dq2 = _dot(ds, kall[sw])
                dq[2 * g] = dq[2 * g] + jnp.where(me, dq2[:TQ], 0.0)
                dq[2 * g + 1] = dq[2 * g + 1] + jnp.where(me, dq2[TQ:], 0.0)
                dk2 = _dot_tn(ds, qm)
                dv2 = _dot_tn(p.astype(_MX), domx)
                if sw:
                    dk2 = pltpu.roll(dk2, 64, 1)
                    dv2 = pltpu.roll(dv2, 64, 1)
                dkt = dkt + dk2
                dvt = dvt + dv2
        for j in range(4):
            dq_ref[:, 128 * j:128 * (j + 1)] = dq[j] * 0.125
        dkc_ref[...] += dkt[:LC]
        dvc_ref[...] += dvt[:LC]
        dk_ref[pl.ds(st, 3 * TQ), :] += dkt[LC:]
        dv_ref[pl.ds(st, 3 * TQ), :] += dvt[LC:]

    full = lambda a: pl.BlockSpec(a.shape, lambda i: (0, 0))
    fs = lambda r: pl.BlockSpec((r, 128), lambda i: (0, 0))
    qs = pl.BlockSpec((TQ, AW), lambda i: (i, 0))
    return pl.pallas_call(
        body, name="attn_bwd", grid=(nb,),
        in_specs=[qs, qs, full(kp), full(kp), full(kp), full(kp), full(kc), full(kc), full(kc), full(kc),
                  pl.BlockSpec(memory_space=pltpu.SMEM), qs, qs],
        out_specs=[qs, fs(s + 2 * TQ), fs(s + 2 * TQ), fs(LC), fs(LC), fs(8)],
        out_shape=[jax.ShapeDtypeStruct((s, AW), F32), jax.ShapeDtypeStruct((s + 2 * TQ, 128), F32),
                   jax.ShapeDtypeStruct((s + 2 * TQ, 128), F32), jax.ShapeDtypeStruct((LC, 128), F32),
                   jax.ShapeDtypeStruct((LC, 128), F32), jax.ShapeDtypeStruct((8, 128), F32)],
        compiler_params=_cp("arbitrary"),
    )(q0, q1, kp, kswp, vp, vswp, kc, kcsw, vc, vcsw, sink, o, do)


def _mix_in_bwd(dq, dk, dv, dga, dgb, proj, x, dx1, win, nw, sc, qnw2, knw2, cos, sa, sb, gmat):
    s = x.shape[0]

    def body(dq_ref, dk_ref, dv_ref, dga_ref, dgb_ref, proj_ref, x_ref, dx1_ref, win_ref, nw_ref, sc_ref,
             qnw_ref, knw_ref, cos_ref, sa_ref, sb_ref, g_ref, gx_ref, dproj_ref, vec_ref):
        @pl.when(pl.program_id(0) == 0)
        def _():
            vec_ref[...] = jnp.zeros((8, D), F32)
        cs, sav, sbv, g = cos_ref[...], sa_ref[...], sb_ref[...], g_ref[...]
        gq = jnp.zeros((1, 128), F32)
        for j in range(4):
            dqn = _rope_bwd(dq_ref[:, 128 * j:128 * (j + 1)], cs, sav, sbv)
            dch, gw = _headnorm_bwd(proj_ref[:, 128 * j:128 * (j + 1)], dqn, qnw_ref[...], g)
            dproj_ref[:, 128 * j:128 * (j + 1)] = dch.astype(_MX)
            gq = gq + _rowsum(gw)
        dkn = _rope_bwd(dk_ref[...], cs, sav, sbv)
        dch, gw = _headnorm_bwd(proj_ref[:, 512:640], dkn, knw_ref[...], g)
        dproj_ref[:, 512:640] = dch.astype(_MX)
        dproj_ref[:, 640:768] = dv_ref[...].astype(_MX)
        dproj_ref[:, 768:1280] = dga_ref[...]
        dproj_ref[:, 1280:1792] = dgb_ref[...]
        vec_ref[3:4, 0:128] += gq
        vec_ref[3:4, 128:256] += _rowsum(gw)
        dh = _dot_nt(dproj_ref[...], win_ref[...])
        xv = x_ref[...]
        r = lax.rsqrt(jnp.mean(xv * xv, axis=-1, keepdims=True) + EPS)
        xn = xv * r
        nwv, scv = nw_ref[...], sc_ref[...]
        vec_ref[0:1, :] += _rowsum(dh)
        vec_ref[1:2, :] += _rowsum(dh * xn) * nwv
        vec_ref[2:3, :] += _rowsum(dh * xn) * (1.0 + scv)
        dxn = dh * (nwv * (1.0 + scv))
        gx_ref[...] = dx1_ref[...] + r * (dxn - xn * jnp.mean(dxn * xn, axis=-1, keepdims=True))

    row = lambda n: pl.BlockSpec((1, n), lambda i: (0, 0))
    tile = lambda n: pl.BlockSpec((TM, n), lambda i: (i, 0))
    return pl.pallas_call(
        body, name="mix_in_bwd", grid=(s // TM,),
        in_specs=[tile(AW), tile(128), tile(128), tile(CW), tile(CW), tile(INW), tile(D), tile(D),
                  pl.BlockSpec((D, INW), lambda i: (0, 0)), row(D), row(D), row(128), row(128),
                  tile(128), tile(128), tile(128), pl.BlockSpec((128, 128), lambda i: (0, 0))],
        out_specs=[tile(D), tile(INW), pl.BlockSpec((8, D), lambda i: (0, 0))],
        out_shape=[jax.ShapeDtypeStruct((s, D), F32), jax.ShapeDtypeStruct((s, INW), _MX),
                   jax.ShapeDtypeStruct((8, D), F32)],
        compiler_params=_cp("arbitrary"),
    )(dq, dk, dv, dga, dgb, proj, x, dx1, win, nw, sc, qnw2, knw2, cos, sa, sb, gmat)


def _ctx_bwd(ctx, nw, scc, winkv, kvc, hc, dkc, dvc, knw2, gmat):
    def body(ctx_ref, nw_ref, sc_ref, w_ref, kvc_ref, hc_ref, dkc_ref, dvc_ref, knw_ref, g_ref, gw_ref, vec_ref):
        dkr, gk = _headnorm_bwd(kvc_ref[:, 0:128], dkc_ref[...], knw_ref[...], g_ref[...])
        dkv = jnp.concatenate([dkr, dvc_ref[...]], axis=1).astype(_MX)
        gw_ref[...] = _dot_tn(hc_ref[...], dkv)
        dh = _dot_nt(dkv, w_ref[...])
        cv = ctx_ref[...]
        r = lax.rsqrt(jnp.mean(cv * cv, axis=-1, keepdims=True) + EPS)
        cn = cv * r
        vec_ref[...] = jnp.zeros((8, D), F32)
        vec_ref[0:1, :] = _rowsum(dh)
        vec_ref[1:2, :] = _rowsum(dh * cn) * nw_ref[...]
        vec_ref[2:3, :] = _rowsum(dh * cn) * (1.0 + sc_ref[...])
        vec_ref[3:4, 0:128] = _rowsum(gk)

    return pl.pallas_call(
        body, name="ctx_bwd", in_specs=[_vspec()] * 10, out_specs=[_vspec()] * 2,
        out_shape=[jax.ShapeDtypeStruct((D, 256), F32), jax.ShapeDtypeStruct((8, D), F32)],
        compiler_params=pltpu.CompilerParams(vmem_limit_bytes=VMEM_LIMIT),
    )(ctx, nw, scc, winkv, kvc, hc, dkc, dvc, knw2, gmat)


def _mod_fwd(cs, wmod, bloc):
    def body(c_ref, w_ref, b_ref, o_ref):
        cv = c_ref[...]
        o_ref[...] = _dot((cv * _sig(cv)).astype(_MX), w_ref[...].astype(_MX)) + b_ref[...]

    return pl.pallas_call(
        body, name="mod_fwd", in_specs=[_vspec()] * 3, out_specs=_vspec(),
        out_shape=jax.ShapeDtypeStruct((16, wmod.shape[1]), F32),
        compiler_params=pltpu.CompilerParams(vmem_limit_bytes=VMEM_LIMIT),
    )(cs, wmod, bloc)


def _mod_bwd(cs, dmod, wmod):
    def body(c_ref, d_ref, w_ref, gw_ref, part_ref):
        cv = c_ref[...]
        sl = (cv * _sig(cv)).astype(_MX)
        dm = d_ref[...].astype(_MX)
        gw_ref[...] = _dot_tn(sl, dm)
        part_ref[...] = _dot_nt(dm, w_ref[...].astype(_MX))

    return pl.pallas_call(
        body, name="mod_bwd", in_specs=[_vspec()] * 3, out_specs=[_vspec()] * 2,
        out_shape=[jax.ShapeDtypeStruct(wmod.shape, F32), jax.ShapeDtypeStruct((16, D), F32)],
        compiler_params=pltpu.CompilerParams(vmem_limit_bytes=VMEM_LIMIT),
    )(cs, dmod, wmod)


def _sum_leading(a, name):
    n = a.shape[0]

    def body(a_ref, o_ref):
        acc = a_ref[0]
        for k in range(1, n):
            acc = acc + a_ref[k]
        o_ref[...] = acc

    return pl.pallas_call(
        body, name=name, in_specs=[_vspec()], out_specs=_vspec(),
        out_shape=jax.ShapeDtypeStruct(a.shape[1:], F32),
        compiler_params=pltpu.CompilerParams(vmem_limit_bytes=VMEM_LIMIT),
    )(a)


def _cctx_grad(parts, cc):
    def body(p_ref, c_ref, o_ref):
        acc = p_ref[0, 8:9, :]
        for k in range(1, 4):
            acc = acc + p_ref[k, 8:9, :]
        cv = c_ref[...]
        sg = _sig(cv)
        o_ref[...] = acc * (sg * (1.0 + cv * (1.0 - sg)))

    return pl.pallas_call(
        body, name="cctx_grad", in_specs=[_vspec()] * 2, out_specs=_vspec(),
        out_shape=jax.ShapeDtypeStruct((1, D), F32),
    )(parts, cc)


def _adam_math(w, g, m, v):
    mn = ADAM_B1 * m + (1.0 - ADAM_B1) * g
    vn = ADAM_B2 * v + (1.0 - ADAM_B2) * (g * g)
    mh = mn / (1.0 - ADAM_B1 ** ADAM_STEP)
    vh = vn / (1.0 - ADAM_B2 ** ADAM_STEP)
    delta = -ADAM_LR * (mh / (jnp.sqrt(vh) + ADAM_EPS) + ADAM_WD * w)
    return delta, mn, vn


def _adam_big(w, g, m, v, name):
    r, n = w.shape
    tr = 256 if r % 256 == 0 else 64

    def body(w_ref, g_ref, m_ref, v_ref, d_ref, mo_ref, vo_ref):
        d, mn, vn = _adam_math(w_ref[...], g_ref[...], m_ref[...], v_ref[...])
        d_ref[...] = d
        mo_ref[...] = mn
        vo_ref[...] = vn

    spec = pl.BlockSpec((tr, n), lambda i: (i, 0))
    return pl.pallas_call(
        body, name=name, grid=(r // tr,), in_specs=[spec] * 4, out_specs=[spec] * 3,
        out_shape=[jax.ShapeDtypeStruct((r, n), F32)] * 3,
        compiler_params=_cp("arbitrary"),
    )(w, g, m, v)


def _adam_small(ws, gs, ms, vs):
    n = len(ws)

    def body(*refs):
        ins, outs = refs[:4 * n], refs[4 * n:]
        for k in range(n):
            d, mn, vn = _adam_math(ins[k][...], ins[n + k][...], ins[2 * n + k][...], ins[3 * n + k][...])
            outs[k][...] = d
            outs[n + k][...] = mn
            outs[2 * n + k][...] = vn

    shapes = [jax.ShapeDtypeStruct(w.shape, F32) for w in ws]
    res = pl.pallas_call(
        body, name="adam_small", in_specs=[_vspec()] * (4 * n), out_specs=[_vspec()] * (3 * n),
        out_shape=shapes * 3,
    )(*ws, *gs, *ms, *vs)
    return res[:n], res[n:2 * n], res[2 * n:]


def _pair_sum(grads, from_sib, core):
    def body(c_ref, *refs):
        for w in range(NBIG):
            a_ref, b_ref, o_ref = refs[w], refs[NBIG + w], refs[2 * NBIG + w]
            o_ref[...] = (a_ref[...].astype(F32) + b_ref[...].astype(F32)).astype(_MX)

    halves = [(None, g.shape[1] // 2, g.shape[2]) for g in grads]
    return pl.pallas_call(
        body, name="grad_pair_sum",
        grid_spec=pltpu.PrefetchScalarGridSpec(
            num_scalar_prefetch=1, grid=(4,),
            in_specs=[pl.BlockSpec(h, lambda j, c: (j, c[0], 0)) for h in halves]
            + [pl.BlockSpec(h, lambda j, c: (j, 0, 0)) for h in halves],
            out_specs=[pl.BlockSpec(h, lambda j, c: (j, 0, 0)) for h in halves]),
        out_shape=[jax.ShapeDtypeStruct(p.shape, _MX) for p in from_sib],
        compiler_params=_cp("arbitrary"),
    )(core.reshape(1), *grads, *from_sib)


def _chip_sum(parts):
    def body(*refs):
        for w in range(NBIG):
            p_ref, o_ref = refs[w], refs[NBIG + w]
            acc = p_ref[0].astype(F32)
            for k in range(1, 4):
                acc = acc + p_ref[k].astype(F32)
            o_ref[...] = acc

    return pl.pallas_call(
        body, name="grad_chip_sum", grid=(2,),
        in_specs=[pl.BlockSpec((4, p.shape[1] // 2, p.shape[2]), lambda i: (0, i, 0)) for p in parts],
        out_specs=[pl.BlockSpec((p.shape[1] // 2, p.shape[2]), lambda i: (i, 0)) for p in parts],
        out_shape=[jax.ShapeDtypeStruct(p.shape[1:], F32) for p in parts],
        compiler_params=_cp("arbitrary"),
    )(*parts)


def _position():
    x, y, c = lax.axis_index("x"), lax.axis_index("y"), lax.axis_index("c")
    return x, y, c


def _small_allgather(blk, name):
    m_per, n = blk.shape

    def body(x_ref, out_ref, send_sems, recv_sems, local_sem):
        x, y, c = _position()
        me, sibling = (x, y, c), (x, y, 1 - c)
        chips = [(1 - x, y), (x, 1 - y), (1 - x, 1 - y)]

        def rows(px, py, pc):
            return out_ref.at[pl.ds((4 * px + 2 * py + pc) * m_per, m_per), :]

        def copy(k, block, to, src=None):
            return pltpu.make_async_remote_copy(
                src_ref=rows(*block) if src is None else src, dst_ref=rows(*block),
                send_sem=send_sems.at[k], recv_sem=recv_sems.at[k], device_id=to, device_id_type=MESH)

        mine = pltpu.make_async_copy(x_ref, rows(*me), local_sem)
        mine.start()
        first = [copy(0, me, sibling, src=x_ref)]
        first += [copy(1 + j, me, (*chip, c), src=x_ref) for j, chip in enumerate(chips)]
        for cp in first:
            cp.start()
        passed = [copy(4 + j, (*chip, c), sibling) for j, chip in enumerate(chips)]
        for j, chip in enumerate(chips):
            copy(1 + j, (*chip, c), me).wait_recv()
            passed[j].start()
        copy(0, sibling, me).wait_recv()
        for j, chip in enumerate(chips):
            copy(4 + j, (*chip, 1 - c), me).wait_recv()
        for cp in first + passed:
            cp.wait_send()
        mine.wait()

    return pl.pallas_call(
        body, name=name, out_shape=jax.ShapeDtypeStruct((8 * m_per, n), blk.dtype),
        in_specs=[_vspec()], out_specs=_vspec(),
        scratch_shapes=[pltpu.SemaphoreType.DMA((7,)), pltpu.SemaphoreType.DMA((7,)), pltpu.SemaphoreType.DMA],
    )(blk)


def _any_specs(n):
    return [pl.BlockSpec(memory_space=pl.ANY)] * n


def _rows(ref, half, nrows):
    return ref.at[pl.ds(half * (nrows // 2), nrows // 2), :]


def _weight_allgather(shards):
    def body(*refs):
        s_refs, o_refs = refs[:NBIG], refs[NBIG:2 * NBIG]
        send_sems, recv_sems, local_sems = refs[2 * NBIG:]
        x, y, c = _position()
        chip = 2 * x + y
        sibling = (x, y, 1 - c)
        chips = [(1 - x, y), (x, 1 - y), (1 - x, 1 - y)]

        def copy(k, src, dst, to):
            return pltpu.make_async_remote_copy(src_ref=src, dst_ref=dst, send_sem=send_sems.at[k],
                                                recv_sem=recv_sems.at[k], device_id=to, device_id_type=MESH)

        mine = [pltpu.make_async_copy(s_refs[w], o_refs[w].at[chip], local_sems.at[w]) for w in range(NBIG)]
        for cp in mine:
            cp.start()
        first, passed = [], []
        for w in range(NBIG):
            r = shards[w].shape[0]
            for j, ch in enumerate(chips):
                first.append(copy(6 * w + j, _rows(s_refs[w], c, r), _rows(o_refs[w].at[chip], c, r), (*ch, c)))
                theirs = _rows(o_refs[w].at[2 * ch[0] + ch[1]], c, r)
                passed.append(copy(6 * w + 3 + j, theirs, theirs, sibling))
        for cp in first:
            cp.start()
        for w in range(NBIG):
            r = shards[w].shape[0]
            for j, ch in enumerate(chips):
                theirs = _rows(o_refs[w].at[2 * ch[0] + ch[1]], c, r)
                copy(6 * w + j, theirs, theirs, sibling).wait_recv()
                passed[3 * w + j].start()
        for w in range(NBIG):
            r = shards[w].shape[0]
            for j, ch in enumerate(chips):
                other = _rows(o_refs[w].at[2 * ch[0] + ch[1]], 1 - c, r)
                copy(6 * w + 3 + j, other, other, sibling).wait_recv()
        for cp in first + passed:
            cp.wait_send()
        for cp in mine:
            cp.wait()

    return pl.pallas_call(
        body, name="weight_allgather",
        out_shape=[jax.ShapeDtypeStruct((4,) + a.shape, a.dtype) for a in shards],
        in_specs=_any_specs(NBIG), out_specs=_any_specs(NBIG),
        scratch_shapes=[pltpu.SemaphoreType.DMA((6 * NBIG,)), pltpu.SemaphoreType.DMA((6 * NBIG,)),
                        pltpu.SemaphoreType.DMA((NBIG,))],
    )(*shards)


def _grad_swap(grads):
    def body(*refs):
        g_refs, o_refs = refs[:NBIG], refs[NBIG:2 * NBIG]
        send_sems, recv_sems = refs[2 * NBIG:]
        x, y, c = _position()
        cps = []
        for w in range(NBIG):
            h = grads[w].shape[1] // 2
            cps.append(pltpu.make_async_remote_copy(
                src_ref=g_refs[w].at[:, pl.ds((1 - c) * h, h), :], dst_ref=o_refs[w], send_sem=send_sems.at[w],
                recv_sem=recv_sems.at[w], device_id=(x, y, 1 - c), device_id_type=MESH))
        for cp in cps:
            cp.start()
        for cp in cps:
            cp.wait()

    return pl.pallas_call(
        body, name="grad_swap_halves",
        out_shape=[jax.ShapeDtypeStruct((4, g.shape[1] // 2, g.shape[2]), g.dtype) for g in grads],
        in_specs=_any_specs(NBIG), out_specs=_any_specs(NBIG),
        scratch_shapes=[pltpu.SemaphoreType.DMA((NBIG,)), pltpu.SemaphoreType.DMA((NBIG,))],
    )(*grads)


def _chip_exchange(parts):
    def body(*refs):
        q_refs, o_refs = refs[:NBIG], refs[NBIG:2 * NBIG]
        send_sems, recv_sems, local_sems = refs[2 * NBIG:]
        x, y, c = _position()
        chip = 2 * x + y
        chips = [(1 - x, y), (x, 1 - y), (1 - x, 1 - y)]
        mine = [pltpu.make_async_copy(q_refs[w].at[chip], o_refs[w].at[chip], local_sems.at[w]) for w in range(NBIG)]
        for cp in mine:
            cp.start()

        def copy(w, k, src_slot, dst_slot, ch):
            return pltpu.make_async_remote_copy(
                src_ref=q_refs[w].at[src_slot], dst_ref=o_refs[w].at[dst_slot], send_sem=send_sems.at[3 * w + k],
                recv_sem=recv_sems.at[3 * w + k], device_id=(*ch, c), device_id_type=MESH)

        sends = [copy(w, k, 2 * ch[0] + ch[1], chip, ch) for w in range(NBIG) for k, ch in enumerate(chips)]
        for cp in sends:
            cp.start()
        for w in range(NBIG):
            for k, ch in enumerate(chips):
                copy(w, k, chip, 2 * ch[0] + ch[1], ch).wait_recv()
        for cp in sends:
            cp.wait_send()
        for cp in mine:
            cp.wait()

    return pl.pallas_call(
        body, name="chip_exchange", out_shape=[jax.ShapeDtypeStruct(p.shape, p.dtype) for p in parts],
        in_specs=_any_specs(NBIG), out_specs=_any_specs(NBIG),
        scratch_shapes=[pltpu.SemaphoreType.DMA((3 * NBIG,)), pltpu.SemaphoreType.DMA((3 * NBIG,)),
                        pltpu.SemaphoreType.DMA((NBIG,))],
    )(*parts)


def _grad_share(halves):
    def body(*refs):
        h_refs, o_refs = refs[:NBIG], refs[NBIG:2 * NBIG]
        send_sems, recv_sems, local_sems = refs[2 * NBIG:]
        x, y, c = _position()
        mine, cps = [], []
        for w in range(NBIG):
            r = 2 * halves[w].shape[0]
            mine.append(pltpu.make_async_copy(h_refs[w], _rows(o_refs[w], c, r), local_sems.at[w]))
            cps.append(pltpu.make_async_remote_copy(
                src_ref=h_refs[w], dst_ref=_rows(o_refs[w], c, r), send_sem=send_sems.at[w],
                recv_sem=recv_sems.at[w], device_id=(x, y, 1 - c), device_id_type=MESH))
        for cp in mine + cps:
            cp.start()
        for w in range(NBIG):
            r = 2 * halves[w].shape[0]
            other = _rows(o_refs[w], 1 - c, r)
            pltpu.make_async_remote_copy(src_ref=other, dst_ref=other, send_sem=send_sems.at[w],
                                         recv_sem=recv_sems.at[w], device_id=(x, y, 1 - c),
                                         device_id_type=MESH).wait_recv()
        for cp in cps:
            cp.wait_send()
        for cp in mine:
            cp.wait()

    return pl.pallas_call(
        body, name="grad_share_half",
        out_shape=[jax.ShapeDtypeStruct((2 * h.shape[0], h.shape[1]), h.dtype) for h in halves],
        in_specs=_any_specs(NBIG), out_specs=_any_specs(NBIG),
        scratch_shapes=[pltpu.SemaphoreType.DMA((NBIG,)), pltpu.SemaphoreType.DMA((NBIG,)),
                        pltpu.SemaphoreType.DMA((NBIG,))],
    )(*halves)


def _rope_tables(s):
    t = jnp.arange(s)
    inv = 10000.0 ** (-jnp.arange(0, 32, 2, dtype=F32) / 32.0)
    ang_r = (t // GRID_W).astype(F32)[:, None] * inv
    ang_c = (t % GRID_W).astype(F32)[:, None] * inv
    ang = jnp.concatenate([ang_r, ang_r, ang_c, ang_c] * 2, axis=-1)
    first = (jnp.arange(128) % 32) < 16
    sin = jnp.sin(ang)
    return jnp.cos(ang), jnp.where(first, -sin, 0.0), jnp.where(first, 0.0, sin)


def _local_step(x, ctx, tgt, mod, modc, nw1, win, qnw, knw, sink, cw, cb, cnw, cnb, wout, nw2, wup, fcw, fcb, wdown):
    s = x.shape[0]
    sh1, sc1, g1, sh2, sc2, g2 = [mod[:, D * k:D * (k + 1)] for k in range(6)]
    shc, scc = modc[:, :D], modc[:, D:2 * D]
    cos, sa, sb = _rope_tables(s)
    gi = jnp.arange(128) // 64
    gmat = (gi[:, None] == gi[None, :]).astype(_MX)
    qnw2, knw2 = jnp.tile(qnw, (1, 2)), jnp.tile(knw, (1, 2))
    cw32 = jnp.pad(cw, ((0, 1), (0, 0)))
    fcw2 = jnp.pad(fcw, ((0, 5), (0, 0))).reshape(8, 2, FH).transpose(1, 0, 2)
    fcb2 = fcb.reshape(2, 1, FH)
    winkv = win[:, 512:768]
    sinkv = sink.reshape(8)

    proj, h, q0, q1, k, ksw, v, vsw, u0 = _mix_in(x, nw1, sc1, sh1, win, qnw2, knw2, cos, sa, sb, gmat)
    kvc, hc, kc, kcsw, vc, vcsw = _ctx_kv(ctx, nw1, scc, shc, winkv, knw2, gmat)
    padr = lambda a: jnp.pad(a, ((TQ, TQ), (0, 0)))
    kp, kswp, vp, vswp = padr(k), padr(ksw), padr(v), padr(vsw)
    o = _attn_fwd(q0, q1, kp, kswp, vp, vswp, kc, kcsw, vc, vcsw, sinkv)
    u1 = _conv31(u0, cw32, cb)
    x1, mix, cat = _mix_out(o, u1, cnw, cnb, wout, x, g1)
    h2, up0 = _ffn_up(x1, nw2, sc2, sh2, wup)
    act = _ffn_conv_act(up0, fcw2, fcb2)
    dy, dact, ddn, vec_dn = _ffn_down(act, wdown, x1, g2, tgt)
    loss = (0.5 / D) * jnp.sum(vec_dn[1])

    dup, gfc = _ffn_conv_bwd(up0, dact, fcw2, fcb2)
    g_wdown = _tn_matmul(act, ddn, FQ, D, "gw_down")
    g_wup = _tn_matmul(h2, dup, D, FQ, "gw_up", b_split=True, by_chip=True)
    dx1, dmix, vec_up = _ffn_up_bwd(dup, wup, x1, dy, mix, nw2, sc2, g1)
    g_wout = _tn_matmul(cat, dmix, D, D, "gw_out")
    do, du1, vec_ln = _mix_out_bwd(dmix, wout, u1, cnw, cnb)
    dga, dgb, gcw = _conv31_bwd(du1, u0, cw32, proj)
    dq, dkp, dvp, dkc, dvc, dsink = _attn_bwd(q0, q1, kp, kswp, vp, vswp, kc, kcsw, vc, vcsw, sinkv, o, do)
    gwinkv, vec_ctx = _ctx_bwd(ctx, nw1, scc, winkv, kvc, hc, dkc, dvc, knw2, gmat)
    gx, dproj, vec_in = _mix_in_bwd(dq, dkp[TQ:TQ + s], dvp[TQ:TQ + s], dga, dgb, proj, x, dx1, win, nw1, sc1,
                                    qnw2, knw2, cos, sa, sb, gmat)
    g_win = _tn_matmul(h, dproj, D, INW, "gw_in", init=gwinkv)

    g_qn = vec_in[3:4, 0:128]
    g_kn = vec_in[3:4, 128:256] + vec_ctx[3:4, 0:128]
    grads = dict(
        norm_mix_w=vec_in[2:3] + vec_ctx[2:3], w_in=g_win,
        q_norm_w=g_qn[:, :64] + g_qn[:, 64:], k_norm_w=g_kn[:, :64] + g_kn[:, 64:],
        sink_logit=dsink[:, 0].reshape(1, 8), conv_w=gcw[:CK], conv_b=gcw[CK:CK + 1],
        conv_norm_w=vec_ln[1:2], conv_norm_b=vec_ln[0:1], w_out=g_wout, norm_ffn_w=vec_up[2:3], w_up=g_wup,
        ffn_conv_w=gfc[:, 0:3].transpose(1, 0, 2).reshape(3, 2 * FH), ffn_conv_b=gfc[:, 3].reshape(1, 2 * FH),
        w_down=g_wdown)
    dmod = jnp.concatenate([vec_in[0:1], vec_in[1:2], vec_up[3:4], vec_up[0:1], vec_up[1:2], vec_dn[0:1]], axis=1)
    dmodc = jnp.concatenate([vec_ctx[0:1], vec_ctx[1:2]], axis=1)
    return loss, gx, grads, dmod, dmodc


_SMALL = ["c_ctx", "b_mod", "norm_mix_w", "q_norm_w", "k_norm_w", "sink_logit", "conv_w", "conv_b", "conv_norm_w",
          "conv_norm_b", "norm_ffn_w", "ffn_conv_w", "ffn_conv_b"]
_GATHERED = ["w_in", "w_out", "w_up", "w_down"]
_BIG = ["w_mod"] + _GATHERED
_ORDER = ["c_ctx", "w_mod", "b_mod", "norm_mix_w", "w_in", "q_norm_w", "k_norm_w", "sink_logit", "conv_w", "conv_b",
          "conv_norm_w", "conv_norm_b", "w_out", "norm_ffn_w", "w_up", "ffn_conv_w", "ffn_conv_b", "w_down"]
_PACK = [("norm_mix_w", 1), ("norm_ffn_w", 1), ("q_norm_w", 1), ("k_norm_w", 1), ("sink_logit", 1), ("conv_b", 1),
         ("conv_norm_w", 1), ("conv_norm_b", 1), ("ffn_conv_b", 6), ("conv_w", 16), ("ffn_conv_w", 17)]
_PACK_ROWS = 56


def _pack_rows(a, nrows):
    flat = a.reshape(-1)
    return jnp.pad(flat, (0, nrows * D - flat.shape[0])).reshape(nrows, D)


def kernel(x, c, ctx, c_ctx, w_mod, b_mod, norm_mix_w, w_in, q_norm_w, k_norm_w, sink_logit, conv_w, conv_b, conv_norm_w, conv_norm_b, w_out, norm_ffn_w, w_up, ffn_conv_w, ffn_conv_b, w_down, loss_target, m_c_ctx, m_w_mod, m_b_mod, m_norm_mix_w, m_w_in, m_q_norm_w, m_k_norm_w, m_sink_logit, m_conv_w, m_conv_b, m_conv_norm_w, m_conv_norm_b, m_w_out, m_norm_ffn_w, m_w_up, m_ffn_conv_w, m_ffn_conv_b, m_w_down, v_c_ctx, v_w_mod, v_b_mod, v_norm_mix_w, v_w_in, v_q_norm_w, v_k_norm_w, v_sink_logit, v_conv_w, v_conv_b, v_conv_norm_w, v_conv_norm_b, v_w_out, v_norm_ffn_w, v_w_up, v_ffn_conv_w, v_ffn_conv_b, v_w_down):
    w = dict(c_ctx=c_ctx.reshape(1, D), w_mod=w_mod[0], b_mod=b_mod, norm_mix_w=norm_mix_w, w_in=w_in[0],
             q_norm_w=q_norm_w, k_norm_w=k_norm_w, sink_logit=sink_logit, conv_w=conv_w[0], conv_b=conv_b,
             conv_norm_w=conv_norm_w, conv_norm_b=conv_norm_b, w_out=w_out[0], norm_ffn_w=norm_ffn_w, w_up=w_up[0],
             ffn_conv_w=ffn_conv_w[0], ffn_conv_b=ffn_conv_b, w_down=w_down[0])
    m = dict(c_ctx=m_c_ctx.reshape(1, D), w_mod=m_w_mod[0], b_mod=m_b_mod, norm_mix_w=m_norm_mix_w, w_in=m_w_in[0],
             q_norm_w=m_q_norm_w, k_norm_w=m_k_norm_w, sink_logit=m_sink_logit, conv_w=m_conv_w[0], conv_b=m_conv_b,
             conv_norm_w=m_conv_norm_w, conv_norm_b=m_conv_norm_b, w_out=m_w_out[0], norm_ffn_w=m_norm_ffn_w,
             w_up=m_w_up[0], ffn_conv_w=m_ffn_conv_w[0], ffn_conv_b=m_ffn_conv_b, w_down=m_w_down[0])
    v = dict(c_ctx=v_c_ctx.reshape(1, D), w_mod=v_w_mod[0], b_mod=v_b_mod, norm_mix_w=v_norm_mix_w, w_in=v_w_in[0],
             q_norm_w=v_q_norm_w, k_norm_w=v_k_norm_w, sink_logit=v_sink_logit, conv_w=v_conv_w[0], conv_b=v_conv_b,
             conv_norm_w=v_conv_norm_w, conv_norm_b=v_conv_norm_b, w_out=v_w_out[0], norm_ffn_w=v_norm_ffn_w,
             w_up=v_w_up[0], ffn_conv_w=v_ffn_conv_w[0], ffn_conv_b=v_ffn_conv_b, w_down=v_w_down[0])
    xi, yi, ci = _position()
    chip = 2 * xi + yi
    dev = 2 * chip + ci
    s = x.shape[1]
    ncol = w["w_mod"].shape[1]

    blk0 = jnp.concatenate([_pack_rows(w["conv_w"], 4), _pack_rows(w["ffn_conv_w"], 5), c,
                            jnp.zeros((6, D), F32)], axis=0)
    g0 = _small_allgather(blk0, "gather_c_convw").reshape(8, 16, D)
    c_all = g0[:, 9, :]
    cs = jnp.concatenate([c_all, w["c_ctx"], jnp.zeros((7, D), F32)], axis=0)
    cw_full = jnp.concatenate([g0[2 * j, 0:4].reshape(-1)[:CK * 128].reshape(CK, 128) for j in range(4)], axis=1)
    fcw_full = jnp.concatenate([g0[2 * j, 4:9].reshape(-1)[:3 * 1408].reshape(3, 1408) for j in range(4)], axis=1)

    b_loc = lax.dynamic_slice(w["b_mod"], (0, chip * ncol), (1, ncol))
    modp = _mod_fwd(cs, w["w_mod"], b_loc)
    gm = _small_allgather(modp, "gather_mod").reshape(8, 16, ncol)
    mod_all = jnp.concatenate([gm[2 * j] for j in range(4)], axis=1)
    mod = lax.dynamic_slice(mod_all, (dev, 0), (1, 6 * D))
    modc = mod_all[8:9]

    win4, wout4, wup4, wdown4 = _weight_allgather([w[n].astype(_MX) for n in _GATHERED])
    win = win4.transpose(1, 0, 2).reshape(D, INW)
    wout = wout4.reshape(D, D)
    wdown = wdown4.reshape(FH, D)

    loss_loc, gx, gl, dmod, dmodc = _local_step(
        x[0], ctx[0], loss_target[0], mod, modc, w["norm_mix_w"], win, w["q_norm_w"], w["k_norm_w"], w["sink_logit"],
        cw_full, w["conv_b"], w["conv_norm_w"], w["conv_norm_b"], wout, w["norm_ffn_w"], wup4, fcw_full,
        w["ffn_conv_b"], wdown)
    loss = lax.psum(loss_loc, ("x", "y", "c"))

    pack = jnp.concatenate([dmod.reshape(6, D), dmodc.reshape(2, D)]
                           + [_pack_rows(gl[name], nr) for name, nr in _PACK], axis=0)
    pack = jnp.pad(pack, ((0, _PACK_ROWS - pack.shape[0]), (0, 0)))
    gp = _small_allgather(pack, "gather_small_grads").reshape(8, _PACK_ROWS, D)
    tot = _sum_leading(gp, "sum_small_grads")
    g = {}
    r = 8
    for name, nr in _PACK:
        shape = gl[name].shape
        g[name] = tot[r:r + nr].reshape(-1)[:math.prod(shape)].reshape(shape)
        r += nr
    dmod_all = jnp.concatenate([gp[:, 0:6].reshape(8, 6 * D),
                                jnp.pad(tot[6:8].reshape(1, 2 * D), ((0, 0), (0, 4 * D))),
                                jnp.zeros((7, 6 * D), F32)], axis=0)
    g["b_mod"] = _sum_leading(dmod_all.reshape(16, 1, 6 * D), "grad_b_mod")
    g["w_mod"], part = _mod_bwd(cs, lax.dynamic_slice(dmod_all, (0, chip * ncol), (16, ncol)), w["w_mod"])
    parts = _small_allgather(part, "gather_cctx").reshape(8, 16, D)
    g["c_ctx"] = _cctx_grad(parts[0::2], w["c_ctx"])
    g["conv_w"] = lax.dynamic_slice(g["conv_w"], (0, chip * 128), (CK, 128))
    g["ffn_conv_w"] = lax.dynamic_slice(g["ffn_conv_w"], (0, chip * 1408), (3, 1408))

    gsend = [gl["w_in"].reshape(D, 4, INW // 4).transpose(1, 0, 2), gl["w_out"].reshape(4, D // 4, D), gl["w_up"],
             gl["w_down"].reshape(4, FH // 4, D)]
    from_sib = _grad_swap(gsend)
    chip_part = _pair_sum(gsend, from_sib, ci)
    arrived = _chip_exchange(chip_part)
    ghalf = _chip_sum(arrived)
    g["w_in"], g["w_out"], g["w_up"], g["w_down"] = _grad_share(ghalf)

    delta, new_m, new_v = {}, {}, {}
    for name in _BIG:
        delta[name], new_m[name], new_v[name] = _adam_big(w[name], g[name], m[name], v[name], "adam_" + name)
    ds, ms, vs = _adam_small([w[n] for n in _SMALL], [g[n] for n in _SMALL], [m[n] for n in _SMALL],
                             [v[n] for n in _SMALL])
    for k, name in enumerate(_SMALL):
        delta[name], new_m[name], new_v[name] = ds[k], ms[k], vs[k]

    def shaped(d, name):
        a = d[name]
        if name == "c_ctx":
            return a.reshape(D)
        if name in ("w_mod", "w_in", "w_out", "w_up", "w_down", "conv_w", "ffn_conv_w"):
            return a[None]
        return a

    outs = [loss, gx[None]]
    for d in (g, delta, new_m, new_v):
        outs += [shaped(d, name) for name in _ORDER]
    return tuple(outs)
```

```python
import functools
import math

import jax
import jax.numpy as jnp
from jax import lax
from jax.experimental import pallas as pl
from jax.experimental.pallas import tpu as pltpu

F32 = jnp.float32
_MX = jnp.bfloat16
EPS = 1e-6
NEG = -1e30
D = 1024
AW = 512
CW = 512
INW = 1792
FH = 2816
LC = 256
CK = 31
GRID_W = 64
TM = 512
TQ = 128
NCH = FH // 128
FQ = 2 * FH // 4
NBIG = 4
VMEM_LIMIT = 56 * 1024 * 1024
MESH = pl.DeviceIdType.MESH

ADAM_LR, ADAM_B1, ADAM_B2, ADAM_EPS, ADAM_WD, ADAM_STEP = 0.001, 0.9, 0.999, 1e-08, 0.01, 10


def _cp(*sem):
    return pltpu.CompilerParams(dimension_semantics=sem, vmem_limit_bytes=VMEM_LIMIT)


def _vspec():
    return pl.BlockSpec(memory_space=pltpu.VMEM)


def _sig(z):
    return 1.0 / (1.0 + jnp.exp(-z))


def _dot(a, b):
    return jnp.dot(a, b, preferred_element_type=F32)


def _dot_nt(a, b):
    return lax.dot_general(a, b, (((1,), (1,)), ((), ())), preferred_element_type=F32)


def _dot_tn(a, b):
    return lax.dot_general(a, b, (((0,), (0,)), ((), ())), preferred_element_type=F32)


def _gsum(v, g):
    hi = v.astype(_MX)
    lo = (v - hi.astype(F32)).astype(_MX)
    return _dot(hi, g) + _dot(lo, g)


def _headnorm(ch, w, g):
    r = lax.rsqrt(_gsum(ch * ch, g) * (1.0 / 64.0) + EPS)
    return ch * r * w


def _headnorm_bwd(ch, dy, w, g):
    r = lax.rsqrt(_gsum(ch * ch, g) * (1.0 / 64.0) + EPS)
    hat = ch * r
    dhat = dy * w
    dch = r * (dhat - hat * (_gsum(dhat * hat, g) * (1.0 / 64.0)))
    return dch, dy * hat


def _rope(u, cos, sa, sb):
    return u * cos + pltpu.roll(u, 112, 1) * sa + pltpu.roll(u, 16, 1) * sb


def _rope_bwd(d, cos, sa, sb):
    return d * cos - pltpu.roll(d, 112, 1) * sa - pltpu.roll(d, 16, 1) * sb


def _rowsum(v):
    return jnp.sum(v, axis=0, keepdims=True)


def _mix_in(x, nw, sc, sh, win, qnw2, knw2, cos, sa, sb, gmat):
    s = x.shape[0]

    def body(x_ref, nw_ref, sc_ref, sh_ref, win_ref, qnw_ref, knw_ref, cos_ref, sa_ref, sb_ref, g_ref,
             proj_ref, h_ref, q0_ref, q1_ref, k_ref, ksw_ref, v_ref, vsw_ref, u0_ref):
        xv = x_ref[...]
        r = lax.rsqrt(jnp.mean(xv * xv, axis=-1, keepdims=True) + EPS)
        h = xv * r * (nw_ref[...] * (1.0 + sc_ref[...])) + sh_ref[...]
        hb = h.astype(_MX)
        h_ref[...] = hb
        proj = _dot(hb, win_ref[...])
        proj_ref[...] = proj
        cs, sav, sbv, g = cos_ref[...], sa_ref[...], sb_ref[...], g_ref[...]
        lo = lax.broadcasted_iota(jnp.int32, (1, 128), 1) < 64
        for j in range(4):
            q = _rope(_headnorm(proj[:, 128 * j:128 * (j + 1)], qnw_ref[...], g), cs, sav, sbv) * 0.125
            q0_ref[:, 128 * j:128 * (j + 1)] = jnp.where(lo, q, 0.0).astype(_MX)
            q1_ref[:, 128 * j:128 * (j + 1)] = jnp.where(lo, 0.0, q).astype(_MX)
        k = _rope(_headnorm(proj[:, 512:640], knw_ref[...], g), cs, sav, sbv)
        k_ref[...] = k.astype(_MX)
        ksw_ref[...] = pltpu.roll(k, 64, 1).astype(_MX)
        v = proj[:, 640:768]
        v_ref[...] = v.astype(_MX)
        vsw_ref[...] = pltpu.roll(v, 64, 1).astype(_MX)
        u0_ref[...] = proj[:, 768:1280] * _sig(proj[:, 1280:1792])

    row = lambda n: pl.BlockSpec((1, n), lambda i: (0, 0))
    tile = lambda n: pl.BlockSpec((TM, n), lambda i: (i, 0))
    return pl.pallas_call(
        body, name="mix_in", grid=(s // TM,),
        in_specs=[tile(D), row(D), row(D), row(D), pl.BlockSpec((D, INW), lambda i: (0, 0)), row(128), row(128),
                  tile(128), tile(128), tile(128), pl.BlockSpec((128, 128), lambda i: (0, 0))],
        out_specs=[tile(INW), tile(D), tile(AW), tile(AW), tile(128), tile(128), tile(128), tile(128), tile(CW)],
        out_shape=[jax.ShapeDtypeStruct((s, INW), F32), jax.ShapeDtypeStruct((s, D), _MX),
                   jax.ShapeDtypeStruct((s, AW), _MX), jax.ShapeDtypeStruct((s, AW), _MX),
                   jax.ShapeDtypeStruct((s, 128), _MX), jax.ShapeDtypeStruct((s, 128), _MX),
                   jax.ShapeDtypeStruct((s, 128), _MX), jax.ShapeDtypeStruct((s, 128), _MX),
                   jax.ShapeDtypeStruct((s, CW), F32)],
        compiler_params=_cp("arbitrary"),
    )(x, nw, sc, sh, win, qnw2, knw2, cos, sa, sb, gmat)


def _ctx_kv(ctx, nw, scc, shc, winkv, knw2, gmat):
    def body(ctx_ref, nw_ref, sc_ref, sh_ref, w_ref, knw_ref, g_ref,
             kvc_ref, hc_ref, kc_ref, kcsw_ref, vc_ref, vcsw_ref):
        cv = ctx_ref[...]
        r = lax.rsqrt(jnp.mean(cv * cv, axis=-1, keepdims=True) + EPS)
        hc = (cv * r * (nw_ref[...] * (1.0 + sc_ref[...])) + sh_ref[...]).astype(_MX)
        hc_ref[...] = hc
        kvc = _dot(hc, w_ref[...])
        kvc_ref[...] = kvc
        kc = _headnorm(kvc[:, :128], knw_ref[...], g_ref[...])
        kc_ref[...] = kc.astype(_MX)
        kcsw_ref[...] = pltpu.roll(kc, 64, 1).astype(_MX)
        vc = kvc[:, 128:]
        vc_ref[...] = vc.astype(_MX)
        vcsw_ref[...] = pltpu.roll(vc, 64, 1).astype(_MX)

    return pl.pallas_call(
        body, name="ctx_kv",
        in_specs=[_vspec()] * 7, out_specs=[_vspec()] * 6,
        out_shape=[jax.ShapeDtypeStruct((LC, 256), F32), jax.ShapeDtypeStruct((LC, D), _MX)]
        + [jax.ShapeDtypeStruct((LC, 128), _MX)] * 4,
        compiler_params=pltpu.CompilerParams(vmem_limit_bytes=VMEM_LIMIT),
    )(ctx, nw, scc, shc, winkv, knw2, gmat)


def _attn_mask(i, s):
    r = lax.broadcasted_iota(jnp.int32, (2 * TQ, LC + 3 * TQ), 0) % TQ
    cidx = lax.broadcasted_iota(jnp.int32, (2 * TQ, LC + 3 * TQ), 1)
    qpos = i * TQ + r
    kpos = (i - 1) * TQ + (cidx - LC)
    near = (jnp.abs(qpos - kpos) <= 128) & (kpos >= 0) & (kpos < s)
    return (cidx < LC) | near


def _attn_probs(qm, kge, mask, sinkv):
    sc = _dot_nt(qm, kge)
    sc = jnp.where(mask, sc, NEG)
    m = jnp.maximum(jnp.max(sc, axis=-1, keepdims=True), sinkv)
    ex = jnp.exp(sc - m)
    es = jnp.exp(sinkv - m)
    inv = 1.0 / (jnp.sum(ex, axis=-1, keepdims=True) + es)
    return ex * inv, es * inv


def _sink_rows(sink_ref, g, e):
    return jnp.concatenate([jnp.full((TQ, 1), sink_ref[4 * g + e], F32),
                            jnp.full((TQ, 1), sink_ref[4 * g + 2 + e], F32)], axis=0)


def _attn_fwd(q0, q1, kp, kswp, vp, vswp, kc, kcsw, vc, vcsw, sink):
    s = q0.shape[0]

    def body(q0_ref, q1_ref, kp_ref, kswp_ref, vp_ref, vswp_ref, kc_ref, kcsw_ref, vc_ref, vcsw_ref, sink_ref,
             o_ref):
        i = pl.program_id(0)
        st = pl.multiple_of(i * TQ, TQ)
        kall = (jnp.concatenate([kc_ref[...], kp_ref[pl.ds(st, 3 * TQ), :]], axis=0),
                jnp.concatenate([kcsw_ref[...], kswp_ref[pl.ds(st, 3 * TQ), :]], axis=0))
        vall = (jnp.concatenate([vc_ref[...], vp_ref[pl.ds(st, 3 * TQ), :]], axis=0),
                jnp.concatenate([vcsw_ref[...], vswp_ref[pl.ds(st, 3 * TQ), :]], axis=0))
        mask = _attn_mask(i, s)
        lo = lax.broadcasted_iota(jnp.int32, (1, 128), 1) < 64
        qrefs = (q0_ref, q1_ref)
        out = [jnp.zeros((TQ, 128), F32) for _ in range(4)]
        for g in range(2):
            for e in range(2):
                me = lo if e == 0 else jnp.logical_not(lo)
                qm = jnp.concatenate([qrefs[e][:, 256 * g:256 * g + 128],
                                      qrefs[e][:, 256 * g + 128:256 * g + 256]], axis=0)
                sw = 0 if e == g else 1
                p, _ = _attn_probs(qm, kall[sw], mask, _sink_rows(sink_ref, g, e))
                o2 = _dot(p.astype(_MX), vall[sw])
                out[2 * g] = out[2 * g] + jnp.where(me, o2[:TQ], 0.0)
                out[2 * g + 1] = out[2 * g + 1] + jnp.where(me, o2[TQ:], 0.0)
        for j in range(4):
            o_ref[:, 128 * j:128 * (j + 1)] = out[j]

    full = lambda a: pl.BlockSpec(a.shape, lambda i: (0, 0))
    qs = pl.BlockSpec((TQ, AW), lambda i: (i, 0))
    return pl.pallas_call(
        body, name="attn_fwd", grid=(s // TQ,),
        in_specs=[qs, qs, full(kp), full(kp), full(kp), full(kp), full(kc), full(kc), full(kc), full(kc),
                  pl.BlockSpec(memory_space=pltpu.SMEM)],
        out_specs=qs, out_shape=jax.ShapeDtypeStruct((s, AW), F32),
        compiler_params=_cp("arbitrary"),
    )(q0, q1, kp, kswp, vp, vswp, kc, kcsw, vc, vcsw, sink)


def _conv31(u0, cw32, cb):
    s = u0.shape[0]
    rch = 256

    def body(u_ref, w_ref, b_ref, o_ref, pad_ref):
        pad_ref[0:16, :] = jnp.zeros((16, 128), F32)
        pad_ref[s + 16:s + 32, :] = jnp.zeros((16, 128), F32)
        pad_ref[16:s + 16, :] = u_ref[...]
        for cidx in range(s // rch):
            base = cidx * rch
            acc = jnp.zeros((rch, 128), F32) + b_ref[...]
            for j in range(CK):
                acc = acc + w_ref[j:j + 1, :] * pad_ref[base + j + 1:base + j + 1 + rch, :]
            o_ref[base:base + rch, :] = acc

    return pl.pallas_call(
        body, name="conv31", grid=(CW // 128,),
        in_specs=[pl.BlockSpec((s, 128), lambda c: (0, c)), pl.BlockSpec((32, 128), lambda c: (0, c)),
                  pl.BlockSpec((1, 128), lambda c: (0, c))],
        out_specs=pl.BlockSpec((s, 128), lambda c: (0, c)),
        out_shape=jax.ShapeDtypeStruct((s, CW), F32),
        scratch_shapes=[pltpu.VMEM((s + 32, 128), F32)],
        compiler_params=_cp("arbitrary"),
    )(u0, cw32, cb)


def _ln_stats(u1):
    mu = jnp.mean(u1, axis=-1, keepdims=True)
    xc = u1 - mu
    rstd = lax.rsqrt(jnp.mean(xc * xc, axis=-1, keepdims=True) + EPS)
    return xc * rstd, rstd


def _mix_out(o, u1, cnw, cnb, wout, x, g1):
    s = x.shape[0]

    def body(o_ref, u1_ref, cnw_ref, cnb_ref, w_ref, x_ref, g1_ref, x1_ref, mix_ref, cat_ref):
        u2n, _ = _ln_stats(u1_ref[...])
        u2 = u2n * cnw_ref[...] + cnb_ref[...]
        u3 = u2 * _sig(u2)
        cat = jnp.concatenate([o_ref[...], u3], axis=1).astype(_MX)
        cat_ref[...] = cat
        mix = _dot(cat, w_ref[...])
        mix_ref[...] = mix
        x1_ref[...] = x_ref[...] + g1_ref[...] * mix

    row = lambda n: pl.BlockSpec((1, n), lambda i: (0, 0))
    tile = lambda n: pl.BlockSpec((TM, n), lambda i: (i, 0))
    return pl.pallas_call(
        body, name="mix_out", grid=(s // TM,),
        in_specs=[tile(AW), tile(CW), row(CW), row(CW), pl.BlockSpec((D, D), lambda i: (0, 0)), tile(D), row(D)],
        out_specs=[tile(D), tile(D), tile(D)],
        out_shape=[jax.ShapeDtypeStruct((s, D), F32), jax.ShapeDtypeStruct((s, D), F32),
                   jax.ShapeDtypeStruct((s, D), _MX)],
        compiler_params=_cp("arbitrary"),
    )(o, u1, cnw, cnb, wout, x, g1)


def _ffn_up(x1, nw, sc, sh, wup4):
    s = x1.shape[0]
    tm = min(1024, s)

    def body(x_ref, nw_ref, sc_ref, sh_ref, w_ref, h2_ref, up_ref, h2s):
        @pl.when(pl.program_id(1) == 0)
        def _():
            xv = x_ref[...]
            r = lax.rsqrt(jnp.mean(xv * xv, axis=-1, keepdims=True) + EPS)
            h2 = (xv * r * (nw_ref[...] * (1.0 + sc_ref[...])) + sh_ref[...]).astype(_MX)
            h2s[...] = h2
            h2_ref[...] = h2
        up_ref[...] = _dot(h2s[...], w_ref[...])

    row = pl.BlockSpec((1, D), lambda i, j: (0, 0))
    return pl.pallas_call(
        body, name="ffn_up", grid=(s // tm, 4),
        in_specs=[pl.BlockSpec((tm, D), lambda i, j: (i, 0)), row, row, row,
                  pl.BlockSpec((None, D, FQ), lambda i, j: (j, 0, 0))],
        out_specs=[pl.BlockSpec((tm, D), lambda i, j: (i, 0)),
                   pl.BlockSpec((None, tm, FQ), lambda i, j: (j // 2, i, j % 2))],
        out_shape=[jax.ShapeDtypeStruct((s, D), _MX), jax.ShapeDtypeStruct((2, s, FH), F32)],
        scratch_shapes=[pltpu.VMEM((tm, D), _MX)],
        compiler_params=_cp("arbitrary", "arbitrary"),
    )(x1, nw, sc, sh, wup4)


def _fill_pad8(pad_ref, val, s):
    pad_ref[0:8, :] = jnp.zeros((8, 128), F32)
    pad_ref[s + 8:s + 16, :] = jnp.zeros((8, 128), F32)
    pad_ref[8:s + 8, :] = val


def _conv3_at(pad_ref, w_ref, half, base, rch):
    return (w_ref[half, 0:1, :] * pad_ref[base + 7:base + 7 + rch, :]
            + w_ref[half, 1:2, :] * pad_ref[base + 8:base + 8 + rch, :]
            + w_ref[half, 2:3, :] * pad_ref[base + 9:base + 9 + rch, :])


def _ffn_conv_act(up0, fcw, fcb):
    s = up0.shape[1]
    rch = 256

    def body(up_ref, w_ref, b_ref, act_ref, padg, padv):
        _fill_pad8(padg, up_ref[0], s)
        _fill_pad8(padv, up_ref[1], s)
        for cidx in range(s // rch):
            base = cidx * rch
            gate = _conv3_at(padg, w_ref, 0, base, rch) + b_ref[0]
            val = _conv3_at(padv, w_ref, 1, base, rch) + b_ref[1]
            act_ref[base:base + rch, :] = (gate * _sig(gate) * val).astype(_MX)

    return pl.pallas_call(
        body, name="ffn_conv_act", grid=(NCH,),
        in_specs=[pl.BlockSpec((2, s, 128), lambda c: (0, 0, c)), pl.BlockSpec((2, 8, 128), lambda c: (0, 0, c)),
                  pl.BlockSpec((2, 1, 128), lambda c: (0, 0, c))],
        out_specs=pl.BlockSpec((s, 128), lambda c: (0, c)),
        out_shape=jax.ShapeDtypeStruct((s, FH), _MX),
        scratch_shapes=[pltpu.VMEM((s + 16, 128), F32)] * 2,
        compiler_params=_cp("arbitrary"),
    )(up0, fcw, fcb)


def _ffn_down(act, wdown, x1, g2, tgt):
    s = x1.shape[0]

    def body(act_ref, w_ref, x1_ref, g2_ref, tgt_ref, dy_ref, dact_ref, ddn_ref, vec_ref):
        @pl.when(pl.program_id(0) == 0)
        def _():
            vec_ref[...] = jnp.zeros((8, D), F32)
        dn = _dot(act_ref[...], w_ref[...])
        diff = x1_ref[...] + g2_ref[...] * dn - tgt_ref[...]
        dy = diff * (1.0 / D)
        dy_ref[...] = dy
        ddn = (dy * g2_ref[...]).astype(_MX)
        ddn_ref[...] = ddn
        dact_ref[...] = _dot_nt(ddn, w_ref[...])
        vec_ref[0:1, :] += _rowsum(dy * dn)
        vec_ref[1:2, :] += _rowsum(diff * diff)

    tile = lambda n: pl.BlockSpec((TM, n), lambda i: (i, 0))
    return pl.pallas_call(
        body, name="ffn_down", grid=(s // TM,),
        in_specs=[tile(FH), pl.BlockSpec((FH, D), lambda i: (0, 0)), tile(D), pl.BlockSpec((1, D), lambda i: (0, 0)),
                  tile(D)],
        out_specs=[tile(D), tile(FH), tile(D), pl.BlockSpec((8, D), lambda i: (0, 0))],
        out_shape=[jax.ShapeDtypeStruct((s, D), F32), jax.ShapeDtypeStruct((s, FH), F32),
                   jax.ShapeDtypeStruct((s, D), _MX), jax.ShapeDtypeStruct((8, D), F32)],
        compiler_params=_cp("arbitrary"),
    )(act, wdown, x1, g2, tgt)


def _ffn_conv_bwd(up0, dact, fcw, fcb):
    s = up0.shape[1]
    rch = 256

    def body(up_ref, da_ref, w_ref, b_ref, dup_ref, gw_ref, padg, padv, dpg, dpv):
        _fill_pad8(padg, up_ref[0], s)
        _fill_pad8(padv, up_ref[1], s)
        for p in (dpg, dpv):
            p[0:8, :] = jnp.zeros((8, 128), F32)
            p[s + 8:s + 16, :] = jnp.zeros((8, 128), F32)
        acc = [[jnp.zeros((1, 128), F32) for _ in range(4)] for _ in range(2)]
        for cidx in range(s // rch):
            base = cidx * rch
            gate = _conv3_at(padg, w_ref, 0, base, rch) + b_ref[0]
            val = _conv3_at(padv, w_ref, 1, base, rch) + b_ref[1]
            da = da_ref[base:base + rch, :]
            sg = _sig(gate)
            dgate = da * val * sg * (1.0 + gate * (1.0 - sg))
            dval = da * gate * sg
            dpg[base + 8:base + 8 + rch, :] = dgate
            dpv[base + 8:base + 8 + rch, :] = dval
            for half, (dd, pad) in enumerate(((dgate, padg), (dval, padv))):
                for j in range(3):
                    acc[half][j] = acc[half][j] + _rowsum(dd * pad[base + 7 + j:base + 7 + j + rch, :])
                acc[half][3] = acc[half][3] + _rowsum(dd)
        for half in range(2):
            gw_ref[half] = jnp.zeros((8, 128), F32)
            for j in range(4):
                gw_ref[half, j:j + 1, :] = acc[half][j]
        for cidx in range(s // rch):
            base = cidx * rch
            for half, dp in enumerate((dpg, dpv)):
                dup_ref[half, base:base + rch, :] = (
                    w_ref[half, 0:1, :] * dp[base + 9:base + 9 + rch, :]
                    + w_ref[half, 1:2, :] * dp[base + 8:base + 8 + rch, :]
                    + w_ref[half, 2:3, :] * dp[base + 7:base + 7 + rch, :]).astype(_MX)

    return pl.pallas_call(
        body, name="ffn_conv_bwd", grid=(NCH,),
        in_specs=[pl.BlockSpec((2, s, 128), lambda c: (0, 0, c)), pl.BlockSpec((s, 128), lambda c: (0, c)),
                  pl.BlockSpec((2, 8, 128), lambda c: (0, 0, c)), pl.BlockSpec((2, 1, 128), lambda c: (0, 0, c))],
        out_specs=[pl.BlockSpec((2, s, 128), lambda c: (0, 0, c)), pl.BlockSpec((2, 8, 128), lambda c: (0, 0, c))],
        out_shape=[jax.ShapeDtypeStruct((2, s, FH), _MX), jax.ShapeDtypeStruct((2, 8, FH), F32)],
        scratch_shapes=[pltpu.VMEM((s + 16, 128), F32)] * 4,
        compiler_params=_cp("arbitrary"),
    )(up0, dact, fcw, fcb)


def _tn_matmul(a, b, tm, tn, name, b_split=False, by_chip=False, init=None):
    s, m = a.shape
    n = 2 * b.shape[2] if b_split else b.shape[1]
    ts = min(1024, s)
    nsteps = s // ts
    npb = (n // 2) // tn if b_split else None

    def body(*refs):
        if init is None:
            a_ref, b_ref, o_ref, acc = refs
        else:
            a_ref, b_ref, i_ref, o_ref, acc = refs
        k = pl.program_id(2)

        @pl.when(k == 0)
        def _():
            acc[...] = jnp.zeros((tm, tn), F32)
            if init is not None:
                acc[:, 512:768] = i_ref[...]
        acc[...] += _dot_tn(a_ref[...], b_ref[...])

        @pl.when(k == nsteps - 1)
        def _():
            o_ref[...] = acc[...].astype(_MX)

    if b_split:
        bspec = pl.BlockSpec((None, ts, tn), lambda i, j, k: (j // npb, k, j % npb))
    else:
        bspec = pl.BlockSpec((ts, tn), lambda i, j, k: (k, j))
    in_specs = [pl.BlockSpec((ts, tm), lambda i, j, k: (k, i)), bspec]
    args = [a, b]
    if init is not None:
        in_specs.append(pl.BlockSpec((tm, 256), lambda i, j, k: (i, 0)))
        args.append(init)
    if by_chip:
        out_spec = pl.BlockSpec((None, tm, tn), lambda i, j, k: (j, i, 0))
        out_shape = jax.ShapeDtypeStruct((n // tn, m, tn), _MX)
    else:
        out_spec = pl.BlockSpec((tm, tn), lambda i, j, k: (i, j))
        out_shape = jax.ShapeDtypeStruct((m, n), _MX)
    return pl.pallas_call(
        body, name=name, grid=(m // tm, n // tn, nsteps),
        in_specs=in_specs, out_specs=out_spec, out_shape=out_shape,
        scratch_shapes=[pltpu.VMEM((tm, tn), F32)],
        compiler_params=_cp("arbitrary", "arbitrary", "arbitrary"),
    )(*args)


def _ffn_up_bwd(dup, wup4, x1, dy, mix, nw, sc, g1):
    s = x1.shape[0]
    nk = 4

    def body(dup_ref, w_ref, x1_ref, dy_ref, mix_ref, nw_ref, sc_ref, g1_ref, dx1_ref, dmix_ref, vec_ref, acc):
        i, k = pl.program_id(0), pl.program_id(1)

        @pl.when(k == 0)
        def _():
            acc[...] = jnp.zeros((TM, D), F32)

        @pl.when((k == 0) & (i == 0))
        def _():
            vec_ref[...] = jnp.zeros((8, D), F32)
        acc[...] += _dot_nt(dup_ref[...], w_ref[...])

        @pl.when(k == nk - 1)
        def _():
            dh = acc[...]
            xv = x1_ref[...]
            r = lax.rsqrt(jnp.mean(xv * xv, axis=-1, keepdims=True) + EPS)
            xn = xv * r
            nwv, scv = nw_ref[...], sc_ref[...]
            vec_ref[0:1, :] += _rowsum(dh)
            vec_ref[1:2, :] += _rowsum(dh * xn) * nwv
            vec_ref[2:3, :] += _rowsum(dh * xn) * (1.0 + scv)
            dxn = dh * (nwv * (1.0 + scv))
            dx1 = dy_ref[...] + r * (dxn - xn * jnp.mean(dxn * xn, axis=-1, keepdims=True))
            dx1_ref[...] = dx1
            vec_ref[3:4, :] += _rowsum(dx1 * mix_ref[...])
            dmix_ref[...] = (dx1 * g1_ref[...]).astype(_MX)

    tile = pl.BlockSpec((TM, D), lambda i, k: (i, 0))
    row = pl.BlockSpec((1, D), lambda i, k: (0, 0))
    return pl.pallas_call(
        body, name="ffn_up_bwd", grid=(s // TM, nk),
        in_specs=[pl.BlockSpec((None, TM, FQ), lambda i, k: (k // 2, i, k % 2)),
                  pl.BlockSpec((None, D, FQ), lambda i, k: (k, 0, 0)), tile, tile, tile, row, row, row],
        out_specs=[tile, tile, pl.BlockSpec((8, D), lambda i, k: (0, 0))],
        out_shape=[jax.ShapeDtypeStruct((s, D), F32), jax.ShapeDtypeStruct((s, D), _MX),
                   jax.ShapeDtypeStruct((8, D), F32)],
        scratch_shapes=[pltpu.VMEM((TM, D), F32)],
        compiler_params=_cp("arbitrary", "arbitrary"),
    )(dup, wup4, x1, dy, mix, nw, sc, g1)


def _mix_out_bwd(dmix, wout, u1, cnw, cnb):
    s = u1.shape[0]

    def body(dm_ref, w_ref, u1_ref, cnw_ref, cnb_ref, do_ref, du1_ref, vec_ref):
        @pl.when(pl.program_id(0) == 0)
        def _():
            vec_ref[...] = jnp.zeros((8, CW), F32)
        dcat = _dot_nt(dm_ref[...], w_ref[...])
        do_ref[...] = dcat[:, :AW]
        du3 = dcat[:, AW:]
        u2n, rstd = _ln_stats(u1_ref[...])
        u2 = u2n * cnw_ref[...] + cnb_ref[...]
        sg = _sig(u2)
        du2 = du3 * sg * (1.0 + u2 * (1.0 - sg))
        vec_ref[0:1, :] += _rowsum(du2)
        vec_ref[1:2, :] += _rowsum(du2 * u2n)
        d2n = du2 * cnw_ref[...]
        du1_ref[...] = rstd * (d2n - jnp.mean(d2n, axis=-1, keepdims=True)
                               - u2n * jnp.mean(d2n * u2n, axis=-1, keepdims=True))

    row = lambda n: pl.BlockSpec((1, n), lambda i: (0, 0))
    tile = lambda n: pl.BlockSpec((TM, n), lambda i: (i, 0))
    return pl.pallas_call(
        body, name="mix_out_bwd", grid=(s // TM,),
        in_specs=[tile(D), pl.BlockSpec((D, D), lambda i: (0, 0)), tile(CW), row(CW), row(CW)],
        out_specs=[tile(AW), tile(CW), pl.BlockSpec((8, CW), lambda i: (0, 0))],
        out_shape=[jax.ShapeDtypeStruct((s, AW), F32), jax.ShapeDtypeStruct((s, CW), F32),
                   jax.ShapeDtypeStruct((8, CW), F32)],
        compiler_params=_cp("arbitrary"),
    )(dmix, wout, u1, cnw, cnb)


def _conv31_bwd(du1, u0, cw32, proj):
    s = u0.shape[0]
    rch = 256

    def body(d_ref, u_ref, w_ref, ga_ref, gb_ref, dga_ref, dgb_ref, gw_ref, padu, padd):
        for p, src in ((padu, u_ref), (padd, d_ref)):
            p[0:16, :] = jnp.zeros((16, 128), F32)
            p[s + 16:s + 32, :] = jnp.zeros((16, 128), F32)
            p[16:s + 16, :] = src[...]
        for j in range(CK):
            acc = jnp.zeros((1, 128), F32)
            for cidx in range(s // rch):
                base = cidx * rch
                acc = acc + _rowsum(d_ref[base:base + rch, :] * padu[base + j + 1:base + j + 1 + rch, :])
            gw_ref[j:j + 1, :] = acc
        gw_ref[CK:CK + 1, :] = _rowsum(d_ref[...])
        for cidx in range(s // rch):
            base = cidx * rch
            du0 = jnp.zeros((rch, 128), F32)
            for j in range(CK):
                du0 = du0 + w_ref[j:j + 1, :] * padd[base + 31 - j:base + 31 - j + rch, :]
            sg = _sig(gb_ref[base:base + rch, :])
            ga = ga_ref[base:base + rch, :]
            dga_ref[base:base + rch, :] = (du0 * sg).astype(_MX)
            dgb_ref[base:base + rch, :] = (du0 * ga * sg * (1.0 - sg)).astype(_MX)

    blk = lambda off: pl.BlockSpec((s, 128), lambda c: (0, c + off))
    return pl.pallas_call(
        body, name="conv31_bwd", grid=(CW // 128,),
        in_specs=[blk(0), blk(0), pl.BlockSpec((32, 128), lambda c: (0, c)), blk(6), blk(10)],
        out_specs=[blk(0), blk(0), pl.BlockSpec((32, 128), lambda c: (0, c))],
        out_shape=[jax.ShapeDtypeStruct((s, CW), _MX), jax.ShapeDtypeStruct((s, CW), _MX),
                   jax.ShapeDtypeStruct((32, CW), F32)],
        scratch_shapes=[pltpu.VMEM((s + 32, 128), F32)] * 2,
        compiler_params=_cp("arbitrary"),
    )(du1, u0, cw32, proj, proj)


def _attn_bwd(q0, q1, kp, kswp, vp, vswp, kc, kcsw, vc, vcsw, sink, o, do):
    s = q0.shape[0]
    nb = s // TQ

    def body(q0_ref, q1_ref, kp_ref, kswp_ref, vp_ref, vswp_ref, kc_ref, kcsw_ref, vc_ref, vcsw_ref, sink_ref,
             o_ref, do_ref, dq_ref, dk_ref, dv_ref, dkc_ref, dvc_ref, dsink_ref):
        i = pl.program_id(0)

        @pl.when(i == 0)
        def _():
            dk_ref[...] = jnp.zeros((s + 2 * TQ, 128), F32)
            dv_ref[...] = jnp.zeros((s + 2 * TQ, 128), F32)
            dkc_ref[...] = jnp.zeros((LC, 128), F32)
            dvc_ref[...] = jnp.zeros((LC, 128), F32)
            dsink_ref[...] = jnp.zeros((8, 128), F32)
        st = pl.multiple_of(i * TQ, TQ)
        kall = (jnp.concatenate([kc_ref[...], kp_ref[pl.ds(st, 3 * TQ), :]], axis=0),
                jnp.concatenate([kcsw_ref[...], kswp_ref[pl.ds(st, 3 * TQ), :]], axis=0))
        vall = (jnp.concatenate([vc_ref[...], vp_ref[pl.ds(st, 3 * TQ), :]], axis=0),
                jnp.concatenate([vcsw_ref[...], vswp_ref[pl.ds(st, 3 * TQ), :]], axis=0))
        mask = _attn_mask(i, s)
        lo = lax.broadcasted_iota(jnp.int32, (1, 128), 1) < 64
        qrefs = (q0_ref, q1_ref)
        dq = [jnp.zeros((TQ, 128), F32) for _ in range(4)]
        dkt = jnp.zeros((LC + 3 * TQ, 128), F32)
        dvt = jnp.zeros((LC + 3 * TQ, 128), F32)
        for g in range(2):
            for e in range(2):
                me = lo if e == 0 else jnp.logical_not(lo)
                c0, c1 = 256 * g, 256 * g + 128
                qm = jnp.concatenate([qrefs[e][:, c0:c0 + 128], qrefs[e][:, c1:c1 + 128]], axis=0)
                sw = 0 if e == g else 1
                p, ps = _attn_probs(qm, kall[sw], mask, _sink_rows(sink_ref, g, e))
                dom = jnp.where(me, jnp.concatenate([do_ref[:, c0:c0 + 128], do_ref[:, c1:c1 + 128]], axis=0), 0.0)
                o2 = jnp.concatenate([o_ref[:, c0:c0 + 128], o_ref[:, c1:c1 + 128]], axis=0)
                dd = jnp.sum(dom * o2, axis=-1, keepdims=True)
                domx = dom.astype(_MX)
                dp = _dot_nt(domx, vall[sw])
                ds = (p * (dp - dd)).astype(_MX)
                sd = ps * dd
                dsink_ref[4 * g + e:4 * g + e + 1, :] -= jnp.sum(sd[:TQ], axis=0, keepdims=True)
                dsink_ref[4 * g + 2 + e:4 * g + 3 + e, :] -= jnp.sum(sd[TQ:], axis=0, keepdims=True)
                dq2 = _dot(ds, kall[sw])
                dq[2 * g] = dq[2 * g] + jnp.where(me, dq2[:TQ], 0.0)
                dq[2 * g + 1] = dq[2 * g + 1] + jnp.where(me, dq2[TQ:], 0.0)
                dk2 = _dot_tn(ds, qm)
                dv2 = _dot_tn(p.astype(_MX), domx)
                if sw:
                    dk2 = pltpu.roll(dk2, 64, 1)
                    dv2 = pltpu.roll(dv2, 64, 1)
                dkt = dkt + dk2
                dvt = dvt + dv2
        for j in range(4):
            dq_ref[:, 128 * j:128 * (j + 1)] = dq[j] * 0.125
        dkc_ref[...] += dkt[:LC]
        dvc_ref[...] += dvt[:LC]
        dk_ref[pl.ds(st, 3 * TQ), :] += dkt[LC:]
        dv_ref[pl.ds(st, 3 * TQ), :] += dvt[LC:]

    full = lambda a: pl.BlockSpec(a.shape, lambda i: (0, 0))
    fs = lambda r: pl.BlockSpec((r, 128), lambda i: (0, 0))
    qs = pl.BlockSpec((TQ, AW), lambda i: (i, 0))
    return pl.pallas_call(
        body, name="attn_bwd", grid=(nb,),
        in_specs=[qs, qs, full(kp), full(kp), full(kp), full(kp), full(kc), full(kc), full(kc), full(kc),
                  pl.BlockSpec(memory_space=pltpu.SMEM), qs, qs],
        out_specs=[qs, fs(s + 2 * TQ), fs(s + 2 * TQ), fs(LC), fs(LC), fs(8)],
        out_shape=[jax.ShapeDtypeStruct((s, AW), F32), jax.ShapeDtypeStruct((s + 2 * TQ, 128), F32),
                   jax.ShapeDtypeStruct((s + 2 * TQ, 128), F32), jax.ShapeDtypeStruct((LC, 128), F32),
                   jax.ShapeDtypeStruct((LC, 128), F32), jax.ShapeDtypeStruct((8, 128), F32)],
        compiler_params=_cp("arbitrary"),
    )(q0, q1, kp, kswp, vp, vswp, kc, kcsw, vc, vcsw, sink, o, do)


def _mix_in_bwd(dq, dk, dv, dga, dgb, proj, x, dx1, win, nw, sc, qnw2, knw2, cos, sa, sb, gmat):
    s = x.shape[0]

    def body(dq_ref, dk_ref, dv_ref, dga_ref, dgb_ref, proj_ref, x_ref, dx1_ref, win_ref, nw_ref, sc_ref,
             qnw_ref, knw_ref, cos_ref, sa_ref, sb_ref, g_ref, gx_ref, dproj_ref, vec_ref):
        @pl.when(pl.program_id(0) == 0)
        def _():
            vec_ref[...] = jnp.zeros((8, D), F32)
        cs, sav, sbv, g = cos_ref[...], sa_ref[...], sb_ref[...], g_ref[...]
        gq = jnp.zeros((1, 128), F32)
        for j in range(4):
            dqn = _rope_bwd(dq_ref[:, 128 * j:128 * (j + 1)], cs, sav, sbv)
            dch, gw = _headnorm_bwd(proj_ref[:, 128 * j:128 * (j + 1)], dqn, qnw_ref[...], g)
            dproj_ref[:, 128 * j:128 * (j + 1)] = dch.astype(_MX)
            gq = gq + _rowsum(gw)
        dkn = _rope_bwd(dk_ref[...], cs, sav, sbv)
        dch, gw = _headnorm_bwd(proj_ref[:, 512:640], dkn, knw_ref[...], g)
        dproj_ref[:, 512:640] = dch.astype(_MX)
        dproj_ref[:, 640:768] = dv_ref[...].astype(_MX)
        dproj_ref[:, 768:1280] = dga_ref[...]
        dproj_ref[:, 1280:1792] = dgb_ref[...]
        vec_ref[3:4, 0:128] += gq
        vec_ref[3:4, 128:256] += _rowsum(gw)
        dh = _dot_nt(dproj_ref[...], win_ref[...])
        xv = x_ref[...]
        r = lax.rsqrt(jnp.mean(xv * xv, axis=-1, keepdims=True) + EPS)
        xn = xv * r
        nwv, scv = nw_ref[...], sc_ref[...]
        vec_ref[0:1, :] += _rowsum(dh)
        vec_ref[1:2, :] += _rowsum(dh * xn) * nwv
        vec_ref[2:3, :] += _rowsum(dh * xn) * (1.0 + scv)
        dxn = dh * (nwv * (1.0 + scv))
        gx_ref[...] = dx1_ref[...] + r * (dxn - xn * jnp.mean(dxn * xn, axis=-1, keepdims=True))

    row = lambda n: pl.BlockSpec((1, n), lambda i: (0, 0))
    tile = lambda n: pl.BlockSpec((TM, n), lambda i: (i, 0))
    return pl.pallas_call(
        body, name="mix_in_bwd", grid=(s // TM,),
        in_specs=[tile(AW), tile(128), tile(128), tile(CW), tile(CW), tile(INW), tile(D), tile(D),
                  pl.BlockSpec((D, INW), lambda i: (0, 0)), row(D), row(D), row(128), row(128),
                  tile(128), tile(128), tile(128), pl.BlockSpec((128, 128), lambda i: (0, 0))],
        out_specs=[tile(D), tile(INW), pl.BlockSpec((8, D), lambda i: (0, 0))],
        out_shape=[jax.ShapeDtypeStruct((s, D), F32), jax.ShapeDtypeStruct((s, INW), _MX),
                   jax.ShapeDtypeStruct((8, D), F32)],
        compiler_params=_cp("arbitrary"),
    )(dq, dk, dv, dga, dgb, proj, x, dx1, win, nw, sc, qnw2, knw2, cos, sa, sb, gmat)


def _ctx_bwd(ctx, nw, scc, winkv, kvc, hc, dkc, dvc, knw2, gmat):
    def body(ctx_ref, nw_ref, sc_ref, w_ref, kvc_ref, hc_ref, dkc_ref, dvc_ref, knw_ref, g_ref, gw_ref, vec_ref):
        dkr, gk = _headnorm_bwd(kvc_ref[:, 0:128], dkc_ref[...], knw_ref[...], g_ref[...])
        dkv = jnp.concatenate([dkr, dvc_ref[...]], axis=1).astype(_MX)
        gw_ref[...] = _dot_tn(hc_ref[...], dkv)
        dh = _dot_nt(dkv, w_ref[...])
        cv = ctx_ref[...]
        r = lax.rsqrt(jnp.mean(cv * cv, axis=-1, keepdims=True) + EPS)
        cn = cv * r
        vec_ref[...] = jnp.zeros((8, D), F32)
        vec_ref[0:1, :] = _rowsum(dh)
        vec_ref[1:2, :] = _rowsum(dh * cn) * nw_ref[...]
        vec_ref[2:3, :] = _rowsum(dh * cn) * (1.0 + sc_ref[...])
        vec_ref[3:4, 0:128] = _rowsum(gk)

    return pl.pallas_call(
        body, name="ctx_bwd", in_specs=[_vspec()] * 10, out_specs=[_vspec()] * 2,
        out_shape=[jax.ShapeDtypeStruct((D, 256), F32), jax.ShapeDtypeStruct((8, D), F32)],
        compiler_params=pltpu.CompilerParams(vmem_limit_bytes=VMEM_LIMIT),
    )(ctx, nw, scc, winkv, kvc, hc, dkc, dvc, knw2, gmat)


def _mod_fwd(cs, wmod, bloc):
    def body(c_ref, w_ref, b_ref, o_ref):
        cv = c_ref[...]
        o_ref[...] = _dot((cv * _sig(cv)).astype(_MX), w_ref[...].astype(_MX)) + b_ref[...]

    return pl.pallas_call(
        body, name="mod_fwd", in_specs=[_vspec()] * 3, out_specs=_vspec(),
        out_shape=jax.ShapeDtypeStruct((16, wmod.shape[1]), F32),
        compiler_params=pltpu.CompilerParams(vmem_limit_bytes=VMEM_LIMIT),
    )(cs, wmod, bloc)


def _mod_bwd(cs, dmod, wmod):
    def body(c_ref, d_ref, w_ref, gw_ref, part_ref):
        cv = c_ref[...]
        sl = (cv * _sig(cv)).astype(_MX)
        dm = d_ref[...].astype(_MX)
        gw_ref[...] = _dot_tn(sl, dm)
        part_ref[...] = _dot_nt(dm, w_ref[...].astype(_MX))

    return pl.pallas_call(
        body, name="mod_bwd", in_specs=[_vspec()] * 3, out_specs=[_vspec()] * 2,
        out_shape=[jax.ShapeDtypeStruct(wmod.shape, F32), jax.ShapeDtypeStruct((16, D), F32)],
        compiler_params=pltpu.CompilerParams(vmem_limit_bytes=VMEM_LIMIT),
    )(cs, dmod, wmod)


def _sum_leading(a, name):
    n = a.shape[0]

    def body(a_ref, o_ref):
        acc = a_ref[0]
        for k in range(1, n):
            acc = acc + a_ref[k]
        o_ref[...] = acc

    return pl.pallas_call(
        body, name=name, in_specs=[_vspec()], out_specs=_vspec(),
        out_shape=jax.ShapeDtypeStruct(a.shape[1:], F32),
        compiler_params=pltpu.CompilerParams(vmem_limit_bytes=VMEM_LIMIT),
    )(a)


def _cctx_grad(parts, cc):
    def body(p_ref, c_ref, o_ref):
        acc = p_ref[0, 8:9, :]
        for k in range(1, 4):
            acc = acc + p_ref[k, 8:9, :]
        cv = c_ref[...]
        sg = _sig(cv)
        o_ref[...] = acc * (sg * (1.0 + cv * (1.0 - sg)))

    return pl.pallas_call(
        body, name="cctx_grad", in_specs=[_vspec()] * 2, out_specs=_vspec(),
        out_shape=jax.ShapeDtypeStruct((1, D), F32),
    )(parts, cc)


def _adam_math(w, g, m, v):
    mn = ADAM_B1 * m + (1.0 - ADAM_B1) * g
    vn = ADAM_B2 * v + (1.0 - ADAM_B2) * (g * g)
    mh = mn / (1.0 - ADAM_B1 ** ADAM_STEP)
    vh = vn / (1.0 - ADAM_B2 ** ADAM_STEP)
    delta = -ADAM_LR * (mh / (jnp.sqrt(vh) + ADAM_EPS) + ADAM_WD * w)
    return delta, mn, vn


def _adam_big(w, g, m, v, name):
    r, n = w.shape
    tr = 256 if r % 256 == 0 else 64

    def body(w_ref, g_ref, m_ref, v_ref, d_ref, mo_ref, vo_ref):
        d, mn, vn = _adam_math(w_ref[...], g_ref[...], m_ref[...], v_ref[...])
        d_ref[...] = d
        mo_ref[...] = mn
        vo_ref[...] = vn

    spec = pl.BlockSpec((tr, n), lambda i: (i, 0))
    return pl.pallas_call(
        body, name=name, grid=(r // tr,), in_specs=[spec] * 4, out_specs=[spec] * 3,
        out_shape=[jax.ShapeDtypeStruct((r, n), F32)] * 3,
        compiler_params=_cp("arbitrary"),
    )(w, g, m, v)


def _adam_small(ws, gs, ms, vs):
    n = len(ws)

    def body(*refs):
        ins, outs = refs[:4 * n], refs[4 * n:]
        for k in range(n):
            d, mn, vn = _adam_math(ins[k][...], ins[n + k][...], ins[2 * n + k][...], ins[3 * n + k][...])
            outs[k][...] = d
            outs[n + k][...] = mn
            outs[2 * n + k][...] = vn

    shapes = [jax.ShapeDtypeStruct(w.shape, F32) for w in ws]
    res = pl.pallas_call(
        body, name="adam_small", in_specs=[_vspec()] * (4 * n), out_specs=[_vspec()] * (3 * n),
        out_shape=shapes * 3,
    )(*ws, *gs, *ms, *vs)
    return res[:n], res[n:2 * n], res[2 * n:]


def _pair_sum(grads, from_sib, core):
    def body(c_ref, *refs):
        for w in range(NBIG):
            a_ref, b_ref, o_ref = refs[w], refs[NBIG + w], refs[2 * NBIG + w]
            o_ref[...] = (a_ref[...].astype(F32) + b_ref[...].astype(F32)).astype(_MX)

    halves = [(None, g.shape[1] // 2, g.shape[2]) for g in grads]
    return pl.pallas_call(
        body, name="grad_pair_sum",
        grid_spec=pltpu.PrefetchScalarGridSpec(
            num_scalar_prefetch=1, grid=(4,),
            in_specs=[pl.BlockSpec(h, lambda j, c: (j, c[0], 0)) for h in halves]
            + [pl.BlockSpec(h, lambda j, c: (j, 0, 0)) for h in halves],
            out_specs=[pl.BlockSpec(h, lambda j, c: (j, 0, 0)) for h in halves]),
        out_shape=[jax.ShapeDtypeStruct(p.shape, _MX) for p in from_sib],
        compiler_params=_cp("arbitrary"),
    )(core.reshape(1), *grads, *from_sib)


def _chip_sum(parts, arrived, chip, core):
    def body(s_ref, *refs):
        for w in range(NBIG):
            own_ref, p_ref, o_ref = refs[w], refs[NBIG + w], refs[2 * NBIG + w]
            acc = own_ref[...].astype(F32)
            for k in range(3):
                acc = acc + p_ref[k].astype(F32)
            o_ref[...] = acc

    blk = [(p.shape[1] // 2, p.shape[2]) for p in parts]
    return pl.pallas_call(
        body, name="grad_chip_sum",
        grid_spec=pltpu.PrefetchScalarGridSpec(
            num_scalar_prefetch=1, grid=(2,),
            in_specs=[pl.BlockSpec((None,) + b, lambda i, s: (s[0], i, 0)) for b in blk]
            + [pl.BlockSpec((3,) + b, lambda i, s: (0, i, 0)) for b in blk],
            out_specs=[pl.BlockSpec(b, lambda i, s: (2 * s[1] + i, 0)) for b in blk]),
        out_shape=[jax.ShapeDtypeStruct((2 * p.shape[1], p.shape[2]), F32) for p in parts],
        compiler_params=_cp("arbitrary"),
    )(jnp.stack([chip, core]), *parts, *arrived)


def _cast_shards(ws, chip):
    def body(s_ref, *refs):
        for w in range(NBIG):
            refs[NBIG + w][...] = refs[w][...].astype(_MX)

    blk = [(a.shape[0] // 2, a.shape[1]) for a in ws]
    return pl.pallas_call(
        body, name="cast_shards",
        grid_spec=pltpu.PrefetchScalarGridSpec(
            num_scalar_prefetch=1, grid=(2,),
            in_specs=[pl.BlockSpec(b, lambda i, s: (i, 0)) for b in blk],
            out_specs=[pl.BlockSpec((None,) + b, lambda i, s: (s[0], i, 0)) for b in blk]),
        out_shape=[jax.ShapeDtypeStruct((4,) + a.shape, _MX) for a in ws],
        compiler_params=_cp("arbitrary"),
    )(chip.reshape(1), *ws)


def _position():
    x, y, c = lax.axis_index("x"), lax.axis_index("y"), lax.axis_index("c")
    return x, y, c


def _small_allgather(blk, name):
    m_per, n = blk.shape

    def body(x_ref, out_ref, send_sems, recv_sems, local_sem):
        x, y, c = _position()
        me, sibling = (x, y, c), (x, y, 1 - c)
        chips = [(1 - x, y), (x, 1 - y), (1 - x, 1 - y)]

        def rows(px, py, pc):
            return out_ref.at[pl.ds((4 * px + 2 * py + pc) * m_per, m_per), :]

        def copy(k, block, to, src=None):
            return pltpu.make_async_remote_copy(
                src_ref=rows(*block) if src is None else src, dst_ref=rows(*block),
                send_sem=send_sems.at[k], recv_sem=recv_sems.at[k], device_id=to, device_id_type=MESH)

        mine = pltpu.make_async_copy(x_ref, rows(*me), local_sem)
        mine.start()
        first = [copy(0, me, sibling, src=x_ref)]
        first += [copy(1 + j, me, (*chip, c), src=x_ref) for j, chip in enumerate(chips)]
        for cp in first:
            cp.start()
        passed = [copy(4 + j, (*chip, c), sibling) for j, chip in enumerate(chips)]
        for j, chip in enumerate(chips):
            copy(1 + j, (*chip, c), me).wait_recv()
            passed[j].start()
        copy(0, sibling, me).wait_recv()
        for j, chip in enumerate(chips):
            copy(4 + j, (*chip, 1 - c), me).wait_recv()
        for cp in first + passed:
            cp.wait_send()
        mine.wait()

    return pl.pallas_call(
        body, name=name, out_shape=jax.ShapeDtypeStruct((8 * m_per, n), blk.dtype),
        in_specs=[_vspec()], out_specs=_vspec(),
        scratch_shapes=[pltpu.SemaphoreType.DMA((7,)), pltpu.SemaphoreType.DMA((7,)), pltpu.SemaphoreType.DMA],
    )(blk)


def _any_specs(n):
    return [pl.BlockSpec(memory_space=pl.ANY)] * n


def _rows(ref, half, nrows):
    return ref.at[pl.ds(half * (nrows // 2), nrows // 2), :]


def _weight_allgather(bufs):
    def body(*refs):
        s_refs, o_refs = refs[:NBIG], refs[NBIG:2 * NBIG]
        send_sems, recv_sems = refs[2 * NBIG:]
        x, y, c = _position()
        chip = 2 * x + y
        sibling = (x, y, 1 - c)
        chips = [(1 - x, y), (x, 1 - y), (1 - x, 1 - y)]

        def copy(k, src, dst, to):
            return pltpu.make_async_remote_copy(src_ref=src, dst_ref=dst, send_sem=send_sems.at[k],
                                                recv_sem=recv_sems.at[k], device_id=to, device_id_type=MESH)

        first, passed = [], []
        for w in range(NBIG):
            r = bufs[w].shape[1]
            for j, ch in enumerate(chips):
                first.append(copy(6 * w + j, _rows(s_refs[w].at[chip], c, r), _rows(o_refs[w].at[chip], c, r),
                                  (*ch, c)))
                theirs = _rows(o_refs[w].at[2 * ch[0] + ch[1]], c, r)
                passed.append(copy(6 * w + 3 + j, theirs, theirs, sibling))
        for cp in first:
            cp.start()
        for w in range(NBIG):
            r = bufs[w].shape[1]
            for j, ch in enumerate(chips):
                theirs = _rows(o_refs[w].at[2 * ch[0] + ch[1]], c, r)
                copy(6 * w + j, theirs, theirs, sibling).wait_recv()
                passed[3 * w + j].start()
        for w in range(NBIG):
            r = bufs[w].shape[1]
            for j, ch in enumerate(chips):
                other = _rows(o_refs[w].at[2 * ch[0] + ch[1]], 1 - c, r)
                copy(6 * w + 3 + j, other, other, sibling).wait_recv()
        for cp in first + passed:
            cp.wait_send()

    return pl.pallas_call(
        body, name="weight_allgather",
        out_shape=[jax.ShapeDtypeStruct(a.shape, a.dtype) for a in bufs],
        in_specs=_any_specs(NBIG), out_specs=_any_specs(NBIG),
        input_output_aliases={w: w for w in range(NBIG)},
        scratch_shapes=[pltpu.SemaphoreType.DMA((6 * NBIG,)), pltpu.SemaphoreType.DMA((6 * NBIG,))],
    )(*bufs)


def _grad_swap(grads):
    def body(*refs):
        g_refs, o_refs = refs[:NBIG], refs[NBIG:2 * NBIG]
        send_sems, recv_sems = refs[2 * NBIG:]
        x, y, c = _position()
        cps = []
        for w in range(NBIG):
            h = grads[w].shape[1] // 2
            cps.append(pltpu.make_async_remote_copy(
                src_ref=g_refs[w].at[:, pl.ds((1 - c) * h, h), :], dst_ref=o_refs[w], send_sem=send_sems.at[w],
                recv_sem=recv_sems.at[w], device_id=(x, y, 1 - c), device_id_type=MESH))
        for cp in cps:
            cp.start()
        for cp in cps:
            cp.wait()

    return pl.pallas_call(
        body, name="grad_swap_halves",
        out_shape=[jax.ShapeDtypeStruct((4, g.shape[1] // 2, g.shape[2]), g.dtype) for g in grads],
        in_specs=_any_specs(NBIG), out_specs=_any_specs(NBIG),
        scratch_shapes=[pltpu.SemaphoreType.DMA((NBIG,)), pltpu.SemaphoreType.DMA((NBIG,))],
    )(*grads)


def _chip_exchange(parts):
    def body(*refs):
        q_refs, o_refs = refs[:NBIG], refs[NBIG:2 * NBIG]
        send_sems, recv_sems = refs[2 * NBIG:]
        x, y, c = _position()
        chips = [(1 - x, y), (x, 1 - y), (1 - x, 1 - y)]

        def copy(w, k, ch):
            return pltpu.make_async_remote_copy(
                src_ref=q_refs[w].at[2 * ch[0] + ch[1]], dst_ref=o_refs[w].at[k], send_sem=send_sems.at[3 * w + k],
                recv_sem=recv_sems.at[3 * w + k], device_id=(*ch, c), device_id_type=MESH)

        cps = [copy(w, k, ch) for w in range(NBIG) for k, ch in enumerate(chips)]
        for cp in cps:
            cp.start()
        for cp in cps:
            cp.wait()

    return pl.pallas_call(
        body, name="chip_exchange", out_shape=[jax.ShapeDtypeStruct((3,) + p.shape[1:], p.dtype) for p in parts],
        in_specs=_any_specs(NBIG), out_specs=_any_specs(NBIG),
        scratch_shapes=[pltpu.SemaphoreType.DMA((3 * NBIG,)), pltpu.SemaphoreType.DMA((3 * NBIG,))],
    )(*parts)


def _grad_share(bufs):
    def body(*refs):
        s_refs, o_refs = refs[:NBIG], refs[NBIG:2 * NBIG]
        send_sems, recv_sems = refs[2 * NBIG:]
        x, y, c = _position()
        cps = []
        for w in range(NBIG):
            r = bufs[w].shape[0]
            cps.append(pltpu.make_async_remote_copy(
                src_ref=_rows(s_refs[w], c, r), dst_ref=_rows(o_refs[w], c, r), send_sem=send_sems.at[w],
                recv_sem=recv_sems.at[w], device_id=(x, y, 1 - c), device_id_type=MESH))
        for cp in cps:
            cp.start()
        for cp in cps:
            cp.wait()

    return pl.pallas_call(
        body, name="grad_share_half", out_shape=[jax.ShapeDtypeStruct(a.shape, a.dtype) for a in bufs],
        in_specs=_any_specs(NBIG), out_specs=_any_specs(NBIG),
        input_output_aliases={w: w for w in range(NBIG)},
        scratch_shapes=[pltpu.SemaphoreType.DMA((NBIG,)), pltpu.SemaphoreType.DMA((NBIG,))],
    )(*bufs)


def _rope_tables(s):
    t = jnp.arange(s)
    inv = 10000.0 ** (-jnp.arange(0, 32, 2, dtype=F32) / 32.0)
    ang_r = (t // GRID_W).astype(F32)[:, None] * inv
    ang_c = (t % GRID_W).astype(F32)[:, None] * inv
    ang = jnp.concatenate([ang_r, ang_r, ang_c, ang_c] * 2, axis=-1)
    first = (jnp.arange(128) % 32) < 16
    sin = jnp.sin(ang)
    return jnp.cos(ang), jnp.where(first, -sin, 0.0), jnp.where(first, 0.0, sin)


def _local_step(x, ctx, tgt, mod, modc, nw1, win, qnw, knw, sink, cw, cb, cnw, cnb, wout, nw2, wup, fcw, fcb, wdown):
    s = x.shape[0]
    sh1, sc1, g1, sh2, sc2, g2 = [mod[:, D * k:D * (k + 1)] for k in range(6)]
    shc, scc = modc[:, :D], modc[:, D:2 * D]
    cos, sa, sb = _rope_tables(s)
    gi = jnp.arange(128) // 64
    gmat = (gi[:, None] == gi[None, :]).astype(_MX)
    qnw2, knw2 = jnp.tile(qnw, (1, 2)), jnp.tile(knw, (1, 2))
    cw32 = jnp.pad(cw, ((0, 1), (0, 0)))
    fcw2 = jnp.pad(fcw, ((0, 5), (0, 0))).reshape(8, 2, FH).transpose(1, 0, 2)
    fcb2 = fcb.reshape(2, 1, FH)
    winkv = win[:, 512:768]
    sinkv = sink.reshape(8)

    proj, h, q0, q1, k, ksw, v, vsw, u0 = _mix_in(x, nw1, sc1, sh1, win, qnw2, knw2, cos, sa, sb, gmat)
    kvc, hc, kc, kcsw, vc, vcsw = _ctx_kv(ctx, nw1, scc, shc, winkv, knw2, gmat)
    padr = lambda a: jnp.pad(a, ((TQ, TQ), (0, 0)))
    kp, kswp, vp, vswp = padr(k), padr(ksw), padr(v), padr(vsw)
    o = _attn_fwd(q0, q1, kp, kswp, vp, vswp, kc, kcsw, vc, vcsw, sinkv)
    u1 = _conv31(u0, cw32, cb)
    x1, mix, cat = _mix_out(o, u1, cnw, cnb, wout, x, g1)
    h2, up0 = _ffn_up(x1, nw2, sc2, sh2, wup)
    act = _ffn_conv_act(up0, fcw2, fcb2)
    dy, dact, ddn, vec_dn = _ffn_down(act, wdown, x1, g2, tgt)
    loss = (0.5 / D) * jnp.sum(vec_dn[1])

    dup, gfc = _ffn_conv_bwd(up0, dact, fcw2, fcb2)
    g_wdown = _tn_matmul(act, ddn, FQ, D, "gw_down")
    g_wup = _tn_matmul(h2, dup, D, FQ, "gw_up", b_split=True, by_chip=True)
    dx1, dmix, vec_up = _ffn_up_bwd(dup, wup, x1, dy, mix, nw2, sc2, g1)
    g_wout = _tn_matmul(cat, dmix, D, D, "gw_out")
    do, du1, vec_ln = _mix_out_bwd(dmix, wout, u1, cnw, cnb)
    dga, dgb, gcw = _conv31_bwd(du1, u0, cw32, proj)
    dq, dkp, dvp, dkc, dvc, dsink = _attn_bwd(q0, q1, kp, kswp, vp, vswp, kc, kcsw, vc, vcsw, sinkv, o, do)
    gwinkv, vec_ctx = _ctx_bwd(ctx, nw1, scc, winkv, kvc, hc, dkc, dvc, knw2, gmat)
    gx, dproj, vec_in = _mix_in_bwd(dq, dkp[TQ:TQ + s], dvp[TQ:TQ + s], dga, dgb, proj, x, dx1, win, nw1, sc1,
                                    qnw2, knw2, cos, sa, sb, gmat)
    g_win = _tn_matmul(h, dproj, D, INW, "gw_in", init=gwinkv)

    g_qn = vec_in[3:4, 0:128]
    g_kn = vec_in[3:4, 128:256] + vec_ctx[3:4, 0:128]
    grads = dict(
        norm_mix_w=vec_in[2:3] + vec_ctx[2:3], w_in=g_win,
        q_norm_w=g_qn[:, :64] + g_qn[:, 64:], k_norm_w=g_kn[:, :64] + g_kn[:, 64:],
        sink_logit=dsink[:, 0].reshape(1, 8), conv_w=gcw[:CK], conv_b=gcw[CK:CK + 1],
        conv_norm_w=vec_ln[1:2], conv_norm_b=vec_ln[0:1], w_out=g_wout, norm_ffn_w=vec_up[2:3], w_up=g_wup,
        ffn_conv_w=gfc[:, 0:3].transpose(1, 0, 2).reshape(3, 2 * FH), ffn_conv_b=gfc[:, 3].reshape(1, 2 * FH),
        w_down=g_wdown)
    dmod = jnp.concatenate([vec_in[0:1], vec_in[1:2], vec_up[3:4], vec_up[0:1], vec_up[1:2], vec_dn[0:1]], axis=1)
    dmodc = jnp.concatenate([vec_ctx[0:1], vec_ctx[1:2]], axis=1)
    return loss, gx, grads, dmod, dmodc


_SMALL = ["c_ctx", "b_mod", "norm_mix_w", "q_norm_w", "k_norm_w", "sink_logit", "conv_w", "conv_b", "conv_norm_w",
          "conv_norm_b", "norm_ffn_w", "ffn_conv_w", "ffn_conv_b"]
_GATHERED = ["w_in", "w_out", "w_up", "w_down"]
_BIG = ["w_mod"] + _GATHERED
_ORDER = ["c_ctx", "w_mod", "b_mod", "norm_mix_w", "w_in", "q_norm_w", "k_norm_w", "sink_logit", "conv_w", "conv_b",
          "conv_norm_w", "conv_norm_b", "w_out", "norm_ffn_w", "w_up", "ffn_conv_w", "ffn_conv_b", "w_down"]
_PACK = [("norm_mix_w", 1), ("norm_ffn_w", 1), ("q_norm_w", 1), ("k_norm_w", 1), ("sink_logit", 1), ("conv_b", 1),
         ("conv_norm_w", 1), ("conv_norm_b", 1), ("ffn_conv_b", 6), ("conv_w", 16), ("ffn_conv_w", 17)]
_PACK_ROWS = 56


def _pack_rows(a, nrows):
    flat = a.reshape(-1)
    return jnp.pad(flat, (0, nrows * D - flat.shape[0])).reshape(nrows, D)


def kernel(x, c, ctx, c_ctx, w_mod, b_mod, norm_mix_w, w_in, q_norm_w, k_norm_w, sink_logit, conv_w, conv_b, conv_norm_w, conv_norm_b, w_out, norm_ffn_w, w_up, ffn_conv_w, ffn_conv_b, w_down, loss_target, m_c_ctx, m_w_mod, m_b_mod, m_norm_mix_w, m_w_in, m_q_norm_w, m_k_norm_w, m_sink_logit, m_conv_w, m_conv_b, m_conv_norm_w, m_conv_norm_b, m_w_out, m_norm_ffn_w, m_w_up, m_ffn_conv_w, m_ffn_conv_b, m_w_down, v_c_ctx, v_w_mod, v_b_mod, v_norm_mix_w, v_w_in, v_q_norm_w, v_k_norm_w, v_sink_logit, v_conv_w, v_conv_b, v_conv_norm_w, v_conv_norm_b, v_w_out, v_norm_ffn_w, v_w_up, v_ffn_conv_w, v_ffn_conv_b, v_w_down):
    w = dict(c_ctx=c_ctx.reshape(1, D), w_mod=w_mod[0], b_mod=b_mod, norm_mix_w=norm_mix_w, w_in=w_in[0],
             q_norm_w=q_norm_w, k_norm_w=k_norm_w, sink_logit=sink_logit, conv_w=conv_w[0], conv_b=conv_b,
             conv_norm_w=conv_norm_w, conv_norm_b=conv_norm_b, w_out=w_out[0], norm_ffn_w=norm_ffn_w, w_up=w_up[0],
             ffn_conv_w=ffn_conv_w[0], ffn_conv_b=ffn_conv_b, w_down=w_down[0])
    m = dict(c_ctx=m_c_ctx.reshape(1, D), w_mod=m_w_mod[0], b_mod=m_b_mod, norm_mix_w=m_norm_mix_w, w_in=m_w_in[0],
             q_norm_w=m_q_norm_w, k_norm_w=m_k_norm_w, sink_logit=m_sink_logit, conv_w=m_conv_w[0], conv_b=m_conv_b,
             conv_norm_w=m_conv_norm_w, conv_norm_b=m_conv_norm_b, w_out=m_w_out[0], norm_ffn_w=m_norm_ffn_w,
             w_up=m_w_up[0], ffn_conv_w=m_ffn_conv_w[0], ffn_conv_b=m_ffn_conv_b, w_down=m_w_down[0])
    v = dict(c_ctx=v_c_ctx.reshape(1, D), w_mod=v_w_mod[0], b_mod=v_b_mod, norm_mix_w=v_norm_mix_w, w_in=v_w_in[0],
             q_norm_w=v_q_norm_w, k_norm_w=v_k_norm_w, sink_logit=v_sink_logit, conv_w=v_conv_w[0], conv_b=v_conv_b,
             conv_norm_w=v_conv_norm_w, conv_norm_b=v_conv_norm_b, w_out=v_w_out[0], norm_ffn_w=v_norm_ffn_w,
             w_up=v_w_up[0], ffn_conv_w=v_ffn_conv_w[0], ffn_conv_b=v_ffn_conv_b, w_down=v_w_down[0])
    xi, yi, ci = _position()
    chip = 2 * xi + yi
    dev = 2 * chip + ci
    s = x.shape[1]
    ncol = w["w_mod"].shape[1]

    blk0 = jnp.concatenate([_pack_rows(w["conv_w"], 4), _pack_rows(w["ffn_conv_w"], 5), c,
                            jnp.zeros((6, D), F32)], axis=0)
    g0 = _small_allgather(blk0, "gather_c_convw").reshape(8, 16, D)
    c_all = g0[:, 9, :]
    cs = jnp.concatenate([c_all, w["c_ctx"], jnp.zeros((7, D), F32)], axis=0)
    cw_full = jnp.concatenate([g0[2 * j, 0:4].reshape(-1)[:CK * 128].reshape(CK, 128) for j in range(4)], axis=1)
    fcw_full = jnp.concatenate([g0[2 * j, 4:9].reshape(-1)[:3 * 1408].reshape(3, 1408) for j in range(4)], axis=1)

    b_loc = lax.dynamic_slice(w["b_mod"], (0, chip * ncol), (1, ncol))
    modp = _mod_fwd(cs, w["w_mod"], b_loc)
    gm = _small_allgather(modp, "gather_mod").reshape(8, 16, ncol)
    mod_all = jnp.concatenate([gm[2 * j] for j in range(4)], axis=1)
    mod = lax.dynamic_slice(mod_all, (dev, 0), (1, 6 * D))
    modc = mod_all[8:9]

    win4, wout4, wup4, wdown4 = _weight_allgather(_cast_shards([w[n] for n in _GATHERED], chip))
    win = win4.transpose(1, 0, 2).reshape(D, INW)
    wout = wout4.reshape(D, D)
    wdown = wdown4.reshape(FH, D)

    loss_loc, gx, gl, dmod, dmodc = _local_step(
        x[0], ctx[0], loss_target[0], mod, modc, w["norm_mix_w"], win, w["q_norm_w"], w["k_norm_w"], w["sink_logit"],
        cw_full, w["conv_b"], w["conv_norm_w"], w["conv_norm_b"], wout, w["norm_ffn_w"], wup4, fcw_full,
        w["ffn_conv_b"], wdown)
    loss = lax.psum(loss_loc, ("x", "y", "c"))

    pack = jnp.concatenate([dmod.reshape(6, D), dmodc.reshape(2, D)]
                           + [_pack_rows(gl[name], nr) for name, nr in _PACK], axis=0)
    pack = jnp.pad(pack, ((0, _PACK_ROWS - pack.shape[0]), (0, 0)))
    gp = _small_allgather(pack, "gather_small_grads").reshape(8, _PACK_ROWS, D)
    tot = _sum_leading(gp, "sum_small_grads")
    g = {}
    r = 8
    for name, nr in _PACK:
        shape = gl[name].shape
        g[name] = tot[r:r + nr].reshape(-1)[:math.prod(shape)].reshape(shape)
        r += nr
    dmod_all = jnp.concatenate([gp[:, 0:6].reshape(8, 6 * D),
                                jnp.pad(tot[6:8].reshape(1, 2 * D), ((0, 0), (0, 4 * D))),
                                jnp.zeros((7, 6 * D), F32)], axis=0)
    g["b_mod"] = _sum_leading(dmod_all.reshape(16, 1, 6 * D), "grad_b_mod")
    g["w_mod"], part = _mod_bwd(cs, lax.dynamic_slice(dmod_all, (0, chip * ncol), (16, ncol)), w["w_mod"])
    parts = _small_allgather(part, "gather_cctx").reshape(8, 16, D)
    g["c_ctx"] = _cctx_grad(parts[0::2], w["c_ctx"])
    g["conv_w"] = lax.dynamic_slice(g["conv_w"], (0, chip * 128), (CK, 128))
    g["ffn_conv_w"] = lax.dynamic_slice(g["ffn_conv_w"], (0, chip * 1408), (3, 1408))

    gsend = [gl["w_in"].reshape(D, 4, INW // 4).transpose(1, 0, 2), gl["w_out"].reshape(4, D // 4, D), gl["w_up"],
             gl["w_down"].reshape(4, FH // 4, D)]
    from_sib = _grad_swap(gsend)
    chip_part = _pair_sum(gsend, from_sib, ci)
    arrived = _chip_exchange(chip_part)
    g["w_in"], g["w_out"], g["w_up"], g["w_down"] = _grad_share(_chip_sum(chip_part, arrived, chip, ci))

    delta, new_m, new_v = {}, {}, {}
    for name in _BIG:
        delta[name], new_m[name], new_v[name] = _adam_big(w[name], g[name], m[name], v[name], "adam_" + name)
    ds, ms, vs = _adam_small([w[n] for n in _SMALL], [g[n] for n in _SMALL], [m[n] for n in _SMALL],
                             [v[n] for n in _SMALL])
    for k, name in enumerate(_SMALL):
        delta[name], new_m[name], new_v[name] = ds[k], ms[k], vs[k]

    def shaped(d, name):
        a = d[name]
        if name == "c_ctx":
            return a.reshape(D)
        if name in ("w_mod", "w_in", "w_out", "w_up", "w_down", "conv_w", "ffn_conv_w"):
            return a[None]
        return a

    outs = [loss, gx[None]]
    for d in (g, delta, new_m, new_v):
        outs += [shaped(d, name) for name in _ORDER]
    return tuple(outs)
```

```python
import functools
import math

import jax
import jax.numpy as jnp
from jax import lax
from jax.experimental import pallas as pl
from jax.experimental.pallas import tpu as pltpu

F32 = jnp.float32
_MX = jnp.bfloat16
EPS = 1e-6
NEG = -1e30
D = 1024
AW = 512
CW = 512
INW = 1792
FH = 2816
LC = 256
CK = 31
GRID_W = 64
TM = 512
TQ = 128
NCH = FH // 128
FQ = 2 * FH // 4
VMEM_LIMIT = 56 * 1024 * 1024
MESH = pl.DeviceIdType.MESH

ADAM_LR, ADAM_B1, ADAM_B2, ADAM_EPS, ADAM_WD, ADAM_STEP = 0.001, 0.9, 0.999, 1e-08, 0.01, 10


def _cp(*sem):
    return pltpu.CompilerParams(dimension_semantics=sem, vmem_limit_bytes=VMEM_LIMIT)


def _vspec():
    return pl.BlockSpec(memory_space=pltpu.VMEM)


def _sig(z):
    return 1.0 / (1.0 + jnp.exp(-z))


def _dot(a, b):
    return jnp.dot(a, b, preferred_element_type=F32)


def _dot_nt(a, b):
    return lax.dot_general(a, b, (((1,), (1,)), ((), ())), preferred_element_type=F32)


def _dot_tn(a, b):
    return lax.dot_general(a, b, (((0,), (0,)), ((), ())), preferred_element_type=F32)


def _gsum(v, g):
    hi = v.astype(_MX)
    lo = (v - hi.astype(F32)).astype(_MX)
    return _dot(hi, g) + _dot(lo, g)


def _headnorm(ch, w, g):
    r = lax.rsqrt(_gsum(ch * ch, g) * (1.0 / 64.0) + EPS)
    return ch * r * w


def _headnorm_bwd(ch, dy, w, g):
    r = lax.rsqrt(_gsum(ch * ch, g) * (1.0 / 64.0) + EPS)
    hat = ch * r
    dhat = dy * w
    dch = r * (dhat - hat * (_gsum(dhat * hat, g) * (1.0 / 64.0)))
    return dch, dy * hat


def _rope(u, cos, sa, sb):
    return u * cos + pltpu.roll(u, 112, 1) * sa + pltpu.roll(u, 16, 1) * sb


def _rope_bwd(d, cos, sa, sb):
    return d * cos - pltpu.roll(d, 112, 1) * sa - pltpu.roll(d, 16, 1) * sb


def _rowsum(v):
    return jnp.sum(v, axis=0, keepdims=True)


def _mix_in(x, nw, sc, sh, win, qnw2, knw2, cos, sa, sb, gmat):
    s = x.shape[0]

    def body(x_ref, nw_ref, sc_ref, sh_ref, win_ref, qnw_ref, knw_ref, cos_ref, sa_ref, sb_ref, g_ref,
             proj_ref, h_ref, q0_ref, q1_ref, k_ref, ksw_ref, v_ref, vsw_ref, u0_ref):
        xv = x_ref[...]
        r = lax.rsqrt(jnp.mean(xv * xv, axis=-1, keepdims=True) + EPS)
        h = xv * r * (nw_ref[...] * (1.0 + sc_ref[...])) + sh_ref[...]
        hb = h.astype(_MX)
        h_ref[...] = hb
        proj = _dot(hb, win_ref[...])
        proj_ref[...] = proj
        cs, sav, sbv, g = cos_ref[...], sa_ref[...], sb_ref[...], g_ref[...]
        lo = lax.broadcasted_iota(jnp.int32, (1, 128), 1) < 64
        for j in range(4):
            q = _rope(_headnorm(proj[:, 128 * j:128 * (j + 1)], qnw_ref[...], g), cs, sav, sbv) * 0.125
            q0_ref[:, 128 * j:128 * (j + 1)] = jnp.where(lo, q, 0.0).astype(_MX)
            q1_ref[:, 128 * j:128 * (j + 1)] = jnp.where(lo, 0.0, q).astype(_MX)
        k = _rope(_headnorm(proj[:, 512:640], knw_ref[...], g), cs, sav, sbv)
        k_ref[...] = k.astype(_MX)
        ksw_ref[...] = pltpu.roll(k, 64, 1).astype(_MX)
        v = proj[:, 640:768]
        v_ref[...] = v.astype(_MX)
        vsw_ref[...] = pltpu.roll(v, 64, 1).astype(_MX)
        u0_ref[...] = proj[:, 768:1280] * _sig(proj[:, 1280:1792])

    row = lambda n: pl.BlockSpec((1, n), lambda i: (0, 0))
    tile = lambda n: pl.BlockSpec((TM, n), lambda i: (i, 0))
    return pl.pallas_call(
        body, name="mix_in", grid=(s // TM,),
        in_specs=[tile(D), row(D), row(D), row(D), pl.BlockSpec((D, INW), lambda i: (0, 0)), row(128), row(128),
                  tile(128), tile(128), tile(128), pl.BlockSpec((128, 128), lambda i: (0, 0))],
        out_specs=[tile(INW), tile(D), tile(AW), tile(AW), tile(128), tile(128), tile(128), tile(128), tile(CW)],
        out_shape=[jax.ShapeDtypeStruct((s, INW), F32), jax.ShapeDtypeStruct((s, D), _MX),
                   jax.ShapeDtypeStruct((s, AW), _MX), jax.ShapeDtypeStruct((s, AW), _MX),
                   jax.ShapeDtypeStruct((s, 128), _MX), jax.ShapeDtypeStruct((s, 128), _MX),
                   jax.ShapeDtypeStruct((s, 128), _MX), jax.ShapeDtypeStruct((s, 128), _MX),
                   jax.ShapeDtypeStruct((s, CW), F32)],
        compiler_params=_cp("arbitrary"),
    )(x, nw, sc, sh, win, qnw2, knw2, cos, sa, sb, gmat)


def _ctx_kv(ctx, nw, scc, shc, winkv, knw2, gmat):
    def body(ctx_ref, nw_ref, sc_ref, sh_ref, w_ref, knw_ref, g_ref,
             kvc_ref, hc_ref, kc_ref, kcsw_ref, vc_ref, vcsw_ref):
        cv = ctx_ref[...]
        r = lax.rsqrt(jnp.mean(cv * cv, axis=-1, keepdims=True) + EPS)
        hc = (cv * r * (nw_ref[...] * (1.0 + sc_ref[...])) + sh_ref[...]).astype(_MX)
        hc_ref[...] = hc
        kvc = _dot(hc, w_ref[...])
        kvc_ref[...] = kvc
        kc = _headnorm(kvc[:, :128], knw_ref[...], g_ref[...])
        kc_ref[...] = kc.astype(_MX)
        kcsw_ref[...] = pltpu.roll(kc, 64, 1).astype(_MX)
        vc = kvc[:, 128:]
        vc_ref[...] = vc.astype(_MX)
        vcsw_ref[...] = pltpu.roll(vc, 64, 1).astype(_MX)

    return pl.pallas_call(
        body, name="ctx_kv",
        in_specs=[_vspec()] * 7, out_specs=[_vspec()] * 6,
        out_shape=[jax.ShapeDtypeStruct((LC, 256), F32), jax.ShapeDtypeStruct((LC, D), _MX)]
        + [jax.ShapeDtypeStruct((LC, 128), _MX)] * 4,
        compiler_params=pltpu.CompilerParams(vmem_limit_bytes=VMEM_LIMIT),
    )(ctx, nw, scc, shc, winkv, knw2, gmat)


def _attn_mask(i, s):
    r = lax.broadcasted_iota(jnp.int32, (2 * TQ, LC + 3 * TQ), 0) % TQ
    cidx = lax.broadcasted_iota(jnp.int32, (2 * TQ, LC + 3 * TQ), 1)
    qpos = i * TQ + r
    kpos = (i - 1) * TQ + (cidx - LC)
    near = (jnp.abs(qpos - kpos) <= 128) & (kpos >= 0) & (kpos < s)
    return (cidx < LC) | near


def _attn_probs(qm, kge, mask, sinkv):
    sc = _dot_nt(qm, kge)
    sc = jnp.where(mask, sc, NEG)
    m = jnp.maximum(jnp.max(sc, axis=-1, keepdims=True), sinkv)
    ex = jnp.exp(sc - m)
    es = jnp.exp(sinkv - m)
    inv = 1.0 / (jnp.sum(ex, axis=-1, keepdims=True) + es)
    return ex * inv, es * inv


def _sink_rows(sink_ref, g, e):
    return jnp.concatenate([jnp.full((TQ, 1), sink_ref[4 * g + e], F32),
                            jnp.full((TQ, 1), sink_ref[4 * g + 2 + e], F32)], axis=0)


def _attn_fwd(q0, q1, kp, kswp, vp, vswp, kc, kcsw, vc, vcsw, sink, gather=()):
    s = q0.shape[0]
    nb = s // TQ
    ng = len(gather)

    def body(*refs):
        (q0_ref, q1_ref, kp_ref, kswp_ref, vp_ref, vswp_ref, kc_ref, kcsw_ref, vc_ref, vcsw_ref,
         sink_ref) = refs[:11]
        o_ref = refs[11 + ng]
        i = pl.program_id(0)
        if ng:
            gat = _Gather(refs[11:11 + ng], refs[12 + ng:12 + 2 * ng], refs[12 + 2 * ng], refs[13 + 2 * ng],
                          [b.shape[1] for b in gather])
            pl.when(i == 0)(gat.start)
        st = pl.multiple_of(i * TQ, TQ)
        kall = (jnp.concatenate([kc_ref[...], kp_ref[pl.ds(st, 3 * TQ), :]], axis=0),
                jnp.concatenate([kcsw_ref[...], kswp_ref[pl.ds(st, 3 * TQ), :]], axis=0))
        vall = (jnp.concatenate([vc_ref[...], vp_ref[pl.ds(st, 3 * TQ), :]], axis=0),
                jnp.concatenate([vcsw_ref[...], vswp_ref[pl.ds(st, 3 * TQ), :]], axis=0))
        mask = _attn_mask(i, s)
        lo = lax.broadcasted_iota(jnp.int32, (1, 128), 1) < 64
        qrefs = (q0_ref, q1_ref)
        out = [jnp.zeros((TQ, 128), F32) for _ in range(4)]
        for g in range(2):
            for e in range(2):
                me = lo if e == 0 else jnp.logical_not(lo)
                qm = jnp.concatenate([qrefs[e][:, 256 * g:256 * g + 128],
                                      qrefs[e][:, 256 * g + 128:256 * g + 256]], axis=0)
                sw = 0 if e == g else 1
                p, _ = _attn_probs(qm, kall[sw], mask, _sink_rows(sink_ref, g, e))
                o2 = _dot(p.astype(_MX), vall[sw])
                out[2 * g] = out[2 * g] + jnp.where(me, o2[:TQ], 0.0)
                out[2 * g + 1] = out[2 * g + 1] + jnp.where(me, o2[TQ:], 0.0)
        for j in range(4):
            o_ref[:, 128 * j:128 * (j + 1)] = out[j]
        if ng:
            pl.when(i == max(nb - 3, 0))(gat.forward)
            pl.when(i == nb - 1)(gat.finish)

    full = lambda a: pl.BlockSpec(a.shape, lambda i: (0, 0))
    qs = pl.BlockSpec((TQ, AW), lambda i: (i, 0))
    res = pl.pallas_call(
        body, name="attn_fwd", grid=(nb,),
        in_specs=[qs, qs, full(kp), full(kp), full(kp), full(kp), full(kc), full(kc), full(kc), full(kc),
                  pl.BlockSpec(memory_space=pltpu.SMEM)] + _any_specs(ng),
        out_specs=[qs] + _any_specs(ng),
        out_shape=[jax.ShapeDtypeStruct((s, AW), F32)] + [jax.ShapeDtypeStruct(b.shape, b.dtype) for b in gather],
        input_output_aliases={11 + w: 1 + w for w in range(ng)},
        scratch_shapes=_Gather.semaphores(ng) if ng else [],
        compiler_params=_cp("arbitrary"),
    )(q0, q1, kp, kswp, vp, vswp, kc, kcsw, vc, vcsw, sink, *gather)
    return res[0], res[1:]


def _conv31(u0, cw32, cb):
    s = u0.shape[0]
    rch = 256

    def body(u_ref, w_ref, b_ref, o_ref, pad_ref):
        pad_ref[0:16, :] = jnp.zeros((16, 128), F32)
        pad_ref[s + 16:s + 32, :] = jnp.zeros((16, 128), F32)
        pad_ref[16:s + 16, :] = u_ref[...]
        for cidx in range(s // rch):
            base = cidx * rch
            acc = jnp.zeros((rch, 128), F32) + b_ref[...]
            for j in range(CK):
                acc = acc + w_ref[j:j + 1, :] * pad_ref[base + j + 1:base + j + 1 + rch, :]
            o_ref[base:base + rch, :] = acc

    return pl.pallas_call(
        body, name="conv31", grid=(CW // 128,),
        in_specs=[pl.BlockSpec((s, 128), lambda c: (0, c)), pl.BlockSpec((32, 128), lambda c: (0, c)),
                  pl.BlockSpec((1, 128), lambda c: (0, c))],
        out_specs=pl.BlockSpec((s, 128), lambda c: (0, c)),
        out_shape=jax.ShapeDtypeStruct((s, CW), F32),
        scratch_shapes=[pltpu.VMEM((s + 32, 128), F32)],
        compiler_params=_cp("arbitrary"),
    )(u0, cw32, cb)


def _ln_stats(u1):
    mu = jnp.mean(u1, axis=-1, keepdims=True)
    xc = u1 - mu
    rstd = lax.rsqrt(jnp.mean(xc * xc, axis=-1, keepdims=True) + EPS)
    return xc * rstd, rstd


def _mix_out(o, u1, cnw, cnb, wout, x, g1):
    s = x.shape[0]

    def body(o_ref, u1_ref, cnw_ref, cnb_ref, w_ref, x_ref, g1_ref, x1_ref, mix_ref, cat_ref):
        u2n, _ = _ln_stats(u1_ref[...])
        u2 = u2n * cnw_ref[...] + cnb_ref[...]
        u3 = u2 * _sig(u2)
        cat = jnp.concatenate([o_ref[...], u3], axis=1).astype(_MX)
        cat_ref[...] = cat
        mix = _dot(cat, w_ref[...])
        mix_ref[...] = mix
        x1_ref[...] = x_ref[...] + g1_ref[...] * mix

    row = lambda n: pl.BlockSpec((1, n), lambda i: (0, 0))
    tile = lambda n: pl.BlockSpec((TM, n), lambda i: (i, 0))
    return pl.pallas_call(
        body, name="mix_out", grid=(s // TM,),
        in_specs=[tile(AW), tile(CW), row(CW), row(CW), pl.BlockSpec((D, D), lambda i: (0, 0)), tile(D), row(D)],
        out_specs=[tile(D), tile(D), tile(D)],
        out_shape=[jax.ShapeDtypeStruct((s, D), F32), jax.ShapeDtypeStruct((s, D), F32),
                   jax.ShapeDtypeStruct((s, D), _MX)],
        compiler_params=_cp("arbitrary"),
    )(o, u1, cnw, cnb, wout, x, g1)


def _ffn_up(x1, nw, sc, sh, wup4):
    s = x1.shape[0]
    tm = min(1024, s)

    def body(x_ref, nw_ref, sc_ref, sh_ref, w_ref, h2_ref, up_ref, h2s):
        @pl.when(pl.program_id(1) == 0)
        def _():
            xv = x_ref[...]
            r = lax.rsqrt(jnp.mean(xv * xv, axis=-1, keepdims=True) + EPS)
            h2 = (xv * r * (nw_ref[...] * (1.0 + sc_ref[...])) + sh_ref[...]).astype(_MX)
            h2s[...] = h2
            h2_ref[...] = h2
        up_ref[...] = _dot(h2s[...], w_ref[...])

    row = pl.BlockSpec((1, D), lambda i, j: (0, 0))
    return pl.pallas_call(
        body, name="ffn_up", grid=(s // tm, 4),
        in_specs=[pl.BlockSpec((tm, D), lambda i, j: (i, 0)), row, row, row,
                  pl.BlockSpec((None, D, FQ), lambda i, j: (j, 0, 0))],
        out_specs=[pl.BlockSpec((tm, D), lambda i, j: (i, 0)),
                   pl.BlockSpec((None, tm, FQ), lambda i, j: (j // 2, i, j % 2))],
        out_shape=[jax.ShapeDtypeStruct((s, D), _MX), jax.ShapeDtypeStruct((2, s, FH), F32)],
        scratch_shapes=[pltpu.VMEM((tm, D), _MX)],
        compiler_params=_cp("arbitrary", "arbitrary"),
    )(x1, nw, sc, sh, wup4)


def _fill_pad8(pad_ref, val, s):
    pad_ref[0:8, :] = jnp.zeros((8, 128), F32)
    pad_ref[s + 8:s + 16, :] = jnp.zeros((8, 128), F32)
    pad_ref[8:s + 8, :] = val


def _conv3_at(pad_ref, w_ref, half, base, rch):
    return (w_ref[half, 0:1, :] * pad_ref[base + 7:base + 7 + rch, :]
            + w_ref[half, 1:2, :] * pad_ref[base + 8:base + 8 + rch, :]
            + w_ref[half, 2:3, :] * pad_ref[base + 9:base + 9 + rch, :])


def _ffn_conv_act(up0, fcw, fcb):
    s = up0.shape[1]
    rch = 256

    def body(up_ref, w_ref, b_ref, act_ref, padg, padv):
        _fill_pad8(padg, up_ref[0], s)
        _fill_pad8(padv, up_ref[1], s)
        for cidx in range(s // rch):
            base = cidx * rch
            gate = _conv3_at(padg, w_ref, 0, base, rch) + b_ref[0]
            val = _conv3_at(padv, w_ref, 1, base, rch) + b_ref[1]
            act_ref[base:base + rch, :] = (gate * _sig(gate) * val).astype(_MX)

    return pl.pallas_call(
        body, name="ffn_conv_act", grid=(NCH,),
        in_specs=[pl.BlockSpec((2, s, 128), lambda c: (0, 0, c)), pl.BlockSpec((2, 8, 128), lambda c: (0, 0, c)),
                  pl.BlockSpec((2, 1, 128), lambda c: (0, 0, c))],
        out_specs=pl.BlockSpec((s, 128), lambda c: (0, c)),
        out_shape=jax.ShapeDtypeStruct((s, FH), _MX),
        scratch_shapes=[pltpu.VMEM((s + 16, 128), F32)] * 2,
        compiler_params=_cp("arbitrary"),
    )(up0, fcw, fcb)


def _ffn_down(act, wdown, x1, g2, tgt):
    s = x1.shape[0]

    def body(act_ref, w_ref, x1_ref, g2_ref, tgt_ref, dy_ref, dact_ref, ddn_ref, vec_ref):
        @pl.when(pl.program_id(0) == 0)
        def _():
            vec_ref[...] = jnp.zeros((8, D), F32)
        dn = _dot(act_ref[...], w_ref[...])
        diff = x1_ref[...] + g2_ref[...] * dn - tgt_ref[...]
        dy = diff * (1.0 / D)
        dy_ref[...] = dy
        ddn = (dy * g2_ref[...]).astype(_MX)
        ddn_ref[...] = ddn
        dact_ref[...] = _dot_nt(ddn, w_ref[...])
        vec_ref[0:1, :] += _rowsum(dy * dn)
        vec_ref[1:2, :] += _rowsum(diff * diff)

    tile = lambda n: pl.BlockSpec((TM, n), lambda i: (i, 0))
    return pl.pallas_call(
        body, name="ffn_down", grid=(s // TM,),
        in_specs=[tile(FH), pl.BlockSpec((FH, D), lambda i: (0, 0)), tile(D), pl.BlockSpec((1, D), lambda i: (0, 0)),
                  tile(D)],
        out_specs=[tile(D), tile(FH), tile(D), pl.BlockSpec((8, D), lambda i: (0, 0))],
        out_shape=[jax.ShapeDtypeStruct((s, D), F32), jax.ShapeDtypeStruct((s, FH), F32),
                   jax.ShapeDtypeStruct((s, D), _MX), jax.ShapeDtypeStruct((8, D), F32)],
        compiler_params=_cp("arbitrary"),
    )(act, wdown, x1, g2, tgt)


def _ffn_conv_bwd(up0, dact, fcw, fcb):
    s = up0.shape[1]
    rch = 256

    def body(up_ref, da_ref, w_ref, b_ref, dup_ref, gw_ref, padg, padv, dpg, dpv):
        _fill_pad8(padg, up_ref[0], s)
        _fill_pad8(padv, up_ref[1], s)
        for p in (dpg, dpv):
            p[0:8, :] = jnp.zeros((8, 128), F32)
            p[s + 8:s + 16, :] = jnp.zeros((8, 128), F32)
        acc = [[jnp.zeros((1, 128), F32) for _ in range(4)] for _ in range(2)]
        for cidx in range(s // rch):
            base = cidx * rch
            gate = _conv3_at(padg, w_ref, 0, base, rch) + b_ref[0]
            val = _conv3_at(padv, w_ref, 1, base, rch) + b_ref[1]
            da = da_ref[base:base + rch, :]
            sg = _sig(gate)
            dgate = da * val * sg * (1.0 + gate * (1.0 - sg))
            dval = da * gate * sg
            dpg[base + 8:base + 8 + rch, :] = dgate
            dpv[base + 8:base + 8 + rch, :] = dval
            for half, (dd, pad) in enumerate(((dgate, padg), (dval, padv))):
                for j in range(3):
                    acc[half][j] = acc[half][j] + _rowsum(dd * pad[base + 7 + j:base + 7 + j + rch, :])
                acc[half][3] = acc[half][3] + _rowsum(dd)
        for half in range(2):
            gw_ref[half] = jnp.zeros((8, 128), F32)
            for j in range(4):
                gw_ref[half, j:j + 1, :] = acc[half][j]
        for cidx in range(s // rch):
            base = cidx * rch
            for half, dp in enumerate((dpg, dpv)):
                dup_ref[half, base:base + rch, :] = (
                    w_ref[half, 0:1, :] * dp[base + 9:base + 9 + rch, :]
                    + w_ref[half, 1:2, :] * dp[base + 8:base + 8 + rch, :]
                    + w_ref[half, 2:3, :] * dp[base + 7:base + 7 + rch, :]).astype(_MX)

    return pl.pallas_call(
        body, name="ffn_conv_bwd", grid=(NCH,),
        in_specs=[pl.BlockSpec((2, s, 128), lambda c: (0, 0, c)), pl.BlockSpec((s, 128), lambda c: (0, c)),
                  pl.BlockSpec((2, 8, 128), lambda c: (0, 0, c)), pl.BlockSpec((2, 1, 128), lambda c: (0, 0, c))],
        out_specs=[pl.BlockSpec((2, s, 128), lambda c: (0, 0, c)), pl.BlockSpec((2, 8, 128), lambda c: (0, 0, c))],
        out_shape=[jax.ShapeDtypeStruct((2, s, FH), _MX), jax.ShapeDtypeStruct((2, 8, FH), F32)],
        scratch_shapes=[pltpu.VMEM((s + 16, 128), F32)] * 4,
        compiler_params=_cp("arbitrary"),
    )(up0, dact, fcw, fcb)


def _tn_matmul(a, b, tm, tn, name, b_split=False, by_chip=False, init=None):
    s, m = a.shape
    n = 2 * b.shape[2] if b_split else b.shape[1]
    ts = min(1024, s)
    nsteps = s // ts
    npb = (n // 2) // tn if b_split else None

    def body(*refs):
        if init is None:
            a_ref, b_ref, o_ref, acc = refs
        else:
            a_ref, b_ref, i_ref, o_ref, acc = refs
        k = pl.program_id(2)

        @pl.when(k == 0)
        def _():
            acc[...] = jnp.zeros((tm, tn), F32)
            if init is not None:
                acc[:, 512:768] = i_ref[...]
        acc[...] += _dot_tn(a_ref[...], b_ref[...])

        @pl.when(k == nsteps - 1)
        def _():
            o_ref[...] = acc[...].astype(_MX)

    if b_split:
        bspec = pl.BlockSpec((None, ts, tn), lambda i, j, k: (j // npb, k, j % npb))
    else:
        bspec = pl.BlockSpec((ts, tn), lambda i, j, k: (k, j))
    in_specs = [pl.BlockSpec((ts, tm), lambda i, j, k: (k, i)), bspec]
    args = [a, b]
    if init is not None:
        in_specs.append(pl.BlockSpec((tm, 256), lambda i, j, k: (i, 0)))
        args.append(init)
    if by_chip:
        out_spec = pl.BlockSpec((None, tm, tn), lambda i, j, k: (j, i, 0))
        out_shape = jax.ShapeDtypeStruct((n // tn, m, tn), _MX)
    else:
        out_spec = pl.BlockSpec((tm, tn), lambda i, j, k: (i, j))
        out_shape = jax.ShapeDtypeStruct((m, n), _MX)
    return pl.pallas_call(
        body, name=name, grid=(m // tm, n // tn, nsteps),
        in_specs=in_specs, out_specs=out_spec, out_shape=out_shape,
        scratch_shapes=[pltpu.VMEM((tm, tn), F32)],
        compiler_params=_cp("arbitrary", "arbitrary", "arbitrary"),
    )(*args)


def _ffn_up_bwd(dup, wup4, x1, dy, mix, nw, sc, g1):
    s = x1.shape[0]
    nk = 4

    def body(dup_ref, w_ref, x1_ref, dy_ref, mix_ref, nw_ref, sc_ref, g1_ref, dx1_ref, dmix_ref, vec_ref, acc):
        i, k = pl.program_id(0), pl.program_id(1)

        @pl.when(k == 0)
        def _():
            acc[...] = jnp.zeros((TM, D), F32)

        @pl.when((k == 0) & (i == 0))
        def _():
            vec_ref[...] = jnp.zeros((8, D), F32)
        acc[...] += _dot_nt(dup_ref[...], w_ref[...])

        @pl.when(k == nk - 1)
        def _():
            dh = acc[...]
            xv = x1_ref[...]
            r = lax.rsqrt(jnp.mean(xv * xv, axis=-1, keepdims=True) + EPS)
            xn = xv * r
            nwv, scv = nw_ref[...], sc_ref[...]
            vec_ref[0:1, :] += _rowsum(dh)
            vec_ref[1:2, :] += _rowsum(dh * xn) * nwv
            vec_ref[2:3, :] += _rowsum(dh * xn) * (1.0 + scv)
            dxn = dh * (nwv * (1.0 + scv))
            dx1 = dy_ref[...] + r * (dxn - xn * jnp.mean(dxn * xn, axis=-1, keepdims=True))
            dx1_ref[...] = dx1
            vec_ref[3:4, :] += _rowsum(dx1 * mix_ref[...])
            dmix_ref[...] = (dx1 * g1_ref[...]).astype(_MX)

    tile = pl.BlockSpec((TM, D), lambda i, k: (i, 0))
    row = pl.BlockSpec((1, D), lambda i, k: (0, 0))
    return pl.pallas_call(
        body, name="ffn_up_bwd", grid=(s // TM, nk),
        in_specs=[pl.BlockSpec((None, TM, FQ), lambda i, k: (k // 2, i, k % 2)),
                  pl.BlockSpec((None, D, FQ), lambda i, k: (k, 0, 0)), tile, tile, tile, row, row, row],
        out_specs=[tile, tile, pl.BlockSpec((8, D), lambda i, k: (0, 0))],
        out_shape=[jax.ShapeDtypeStruct((s, D), F32), jax.ShapeDtypeStruct((s, D), _MX),
                   jax.ShapeDtypeStruct((8, D), F32)],
        scratch_shapes=[pltpu.VMEM((TM, D), F32)],
        compiler_params=_cp("arbitrary", "arbitrary"),
    )(dup, wup4, x1, dy, mix, nw, sc, g1)


def _mix_out_bwd(dmix, wout, u1, cnw, cnb):
    s = u1.shape[0]

    def body(dm_ref, w_ref, u1_ref, cnw_ref, cnb_ref, do_ref, du1_ref, vec_ref):
        @pl.when(pl.program_id(0) == 0)
        def _():
            vec_ref[...] = jnp.zeros((8, CW), F32)
        dcat = _dot_nt(dm_ref[...], w_ref[...])
        do_ref[...] = dcat[:, :AW]
        du3 = dcat[:, AW:]
        u2n, rstd = _ln_stats(u1_ref[...])
        u2 = u2n * cnw_ref[...] + cnb_ref[...]
        sg = _sig(u2)
        du2 = du3 * sg * (1.0 + u2 * (1.0 - sg))
        vec_ref[0:1, :] += _rowsum(du2)
        vec_ref[1:2, :] += _rowsum(du2 * u2n)
        d2n = du2 * cnw_ref[...]
        du1_ref[...] = rstd * (d2n - jnp.mean(d2n, axis=-1, keepdims=True)
                               - u2n * jnp.mean(d2n * u2n, axis=-1, keepdims=True))

    row = lambda n: pl.BlockSpec((1, n), lambda i: (0, 0))
    tile = lambda n: pl.BlockSpec((TM, n), lambda i: (i, 0))
    return pl.pallas_call(
        body, name="mix_out_bwd", grid=(s // TM,),
        in_specs=[tile(D), pl.BlockSpec((D, D), lambda i: (0, 0)), tile(CW), row(CW), row(CW)],
        out_specs=[tile(AW), tile(CW), pl.BlockSpec((8, CW), lambda i: (0, 0))],
        out_shape=[jax.ShapeDtypeStruct((s, AW), F32), jax.ShapeDtypeStruct((s, CW), F32),
                   jax.ShapeDtypeStruct((8, CW), F32)],
        compiler_params=_cp("arbitrary"),
    )(dmix, wout, u1, cnw, cnb)


def _conv31_bwd(du1, u0, cw32, proj):
    s = u0.shape[0]
    rch = 256

    def body(d_ref, u_ref, w_ref, ga_ref, gb_ref, dga_ref, dgb_ref, gw_ref, padu, padd):
        for p, src in ((padu, u_ref), (padd, d_ref)):
            p[0:16, :] = jnp.zeros((16, 128), F32)
            p[s + 16:s + 32, :] = jnp.zeros((16, 128), F32)
            p[16:s + 16, :] = src[...]
        for j in range(CK):
            acc = jnp.zeros((1, 128), F32)
            for cidx in range(s // rch):
                base = cidx * rch
                acc = acc + _rowsum(d_ref[base:base + rch, :] * padu[base + j + 1:base + j + 1 + rch, :])
            gw_ref[j:j + 1, :] = acc
        gw_ref[CK:CK + 1, :] = _rowsum(d_ref[...])
        for cidx in range(s // rch):
            base = cidx * rch
            du0 = jnp.zeros((rch, 128), F32)
            for j in range(CK):
                du0 = du0 + w_ref[j:j + 1, :] * padd[base + 31 - j:base + 31 - j + rch, :]
            sg = _sig(gb_ref[base:base + rch, :])
            ga = ga_ref[base:base + rch, :]
            dga_ref[base:base + rch, :] = (du0 * sg).astype(_MX)
            dgb_ref[base:base + rch, :] = (du0 * ga * sg * (1.0 - sg)).astype(_MX)

    blk = lambda off: pl.BlockSpec((s, 128), lambda c: (0, c + off))
    return pl.pallas_call(
        body, name="conv31_bwd", grid=(CW // 128,),
        in_specs=[blk(0), blk(0), pl.BlockSpec((32, 128), lambda c: (0, c)), blk(6), blk(10)],
        out_specs=[blk(0), blk(0), pl.BlockSpec((32, 128), lambda c: (0, c))],
        out_shape=[jax.ShapeDtypeStruct((s, CW), _MX), jax.ShapeDtypeStruct((s, CW), _MX),
                   jax.ShapeDtypeStruct((32, CW), F32)],
        scratch_shapes=[pltpu.VMEM((s + 32, 128), F32)] * 2,
        compiler_params=_cp("arbitrary"),
    )(du1, u0, cw32, proj, proj)


def _attn_bwd(q0, q1, kp, kswp, vp, vswp, kc, kcsw, vc, vcsw, sink, o, do, exchange=()):
    s = q0.shape[0]
    nb = s // TQ
    ne = len(exchange)

    def body(*refs):
        (q0_ref, q1_ref, kp_ref, kswp_ref, vp_ref, vswp_ref, kc_ref, kcsw_ref, vc_ref, vcsw_ref, sink_ref,
         o_ref, do_ref) = refs[:13]
        dq_ref, dk_ref, dv_ref, dkc_ref, dvc_ref, dsink_ref = refs[13 + ne:19 + ne]
        i = pl.program_id(0)
        if ne:
            exch = _Exchange(refs[13:13 + ne], refs[19 + ne:19 + 2 * ne], refs[19 + 2 * ne], refs[20 + 2 * ne])
            pl.when(i == 0)(exch.start)

        @pl.when(i == 0)
        def _():
            dk_ref[...] = jnp.zeros((s + 2 * TQ, 128), F32)
            dv_ref[...] = jnp.zeros((s + 2 * TQ, 128), F32)
            dkc_ref[...] = jnp.zeros((LC, 128), F32)
            dvc_ref[...] = jnp.zeros((LC, 128), F32)
            dsink_ref[...] = jnp.zeros((8, 128), F32)
        st = pl.multiple_of(i * TQ, TQ)
        kall = (jnp.concatenate([kc_ref[...], kp_ref[pl.ds(st, 3 * TQ), :]], axis=0),
                jnp.concatenate([kcsw_ref[...], kswp_ref[pl.ds(st, 3 * TQ), :]], axis=0))
        vall = (jnp.concatenate([vc_ref[...], vp_ref[pl.ds(st, 3 * TQ), :]], axis=0),
                jnp.concatenate([vcsw_ref[...], vswp_ref[pl.ds(st, 3 * TQ), :]], axis=0))
        mask = _attn_mask(i, s)
        lo = lax.broadcasted_iota(jnp.int32, (1, 128), 1) < 64
        qrefs = (q0_ref, q1_ref)
        dq = [jnp.zeros((TQ, 128), F32) for _ in range(4)]
        dkt = jnp.zeros((LC + 3 * TQ, 128), F32)
        dvt = jnp.zeros((LC + 3 * TQ, 128), F32)
        for g in range(2):
            for e in range(2):
                me = lo if e == 0 else jnp.logical_not(lo)
                c0, c1 = 256 * g, 256 * g + 128
                qm = jnp.concatenate([qrefs[e][:, c0:c0 + 128], qrefs[e][:, c1:c1 + 128]], axis=0)
                sw = 0 if e == g else 1
                p, ps = _attn_probs(qm, kall[sw], mask, _sink_rows(sink_ref, g, e))
                dom = jnp.where(me, jnp.concatenate([do_ref[:, c0:c0 + 128], do_ref[:, c1:c1 + 128]], axis=0), 0.0)
                o2 = jnp.concatenate([o_ref[:, c0:c0 + 128], o_ref[:, c1:c1 + 128]], axis=0)
                dd = jnp.sum(dom * o2, axis=-1, keepdims=True)
                domx = dom.astype(_MX)
                dp = _dot_nt(domx, vall[sw])
                ds = (p * (dp - dd)).astype(_MX)
                sd = ps * dd
                dsink_ref[4 * g + e:4 * g + e + 1, :] -= jnp.sum(sd[:TQ], axis=0, keepdims=True)
                dsink_ref[4 * g + 2 + e:4 * g + 3 + e, :] -= jnp.sum(sd[TQ:], axis=0, keepdims=True)
                dq2 = _dot(ds, kall[sw])
                dq[2 * g] = dq[2 * g] + jnp.where(me, dq2[:TQ], 0.0)
                dq[2 * g + 1] = dq[2 * g + 1] + jnp.where(me, dq2[TQ:], 0.0)
                dk2 = _dot_tn(ds, qm)
                dv2 = _dot_tn(p.astype(_MX), domx)
                if sw:
                    dk2 = pltpu.roll(dk2, 64, 1)
                    dv2 = pltpu.roll(dv2, 64, 1)
                dkt = dkt + dk2
                dvt = dvt + dv2
        for j in range(4):
            dq_ref[:, 128 * j:128 * (j + 1)] = dq[j] * 0.125
        dkc_ref[...] += dkt[:LC]
        dvc_ref[...] += dvt[:LC]
        dk_ref[pl.ds(st, 3 * TQ), :] += dkt[LC:]
        dv_ref[pl.ds(st, 3 * TQ), :] += dvt[LC:]
        if ne:
            pl.when(i == nb - 1)(exch.finish)

    full = lambda a: pl.BlockSpec(a.shape, lambda i: (0, 0))
    fs = lambda r: pl.BlockSpec((r, 128), lambda i: (0, 0))
    qs = pl.BlockSpec((TQ, AW), lambda i: (i, 0))
    res = pl.pallas_call(
        body, name="attn_bwd", grid=(nb,),
        in_specs=[qs, qs, full(kp), full(kp), full(kp), full(kp), full(kc), full(kc), full(kc), full(kc),
                  pl.BlockSpec(memory_space=pltpu.SMEM), qs, qs] + _any_specs(ne),
        out_specs=[qs, fs(s + 2 * TQ), fs(s + 2 * TQ), fs(LC), fs(LC), fs(8)] + _any_specs(ne),
        out_shape=[jax.ShapeDtypeStruct((s, AW), F32), jax.ShapeDtypeStruct((s + 2 * TQ, 128), F32),
                   jax.ShapeDtypeStruct((s + 2 * TQ, 128), F32), jax.ShapeDtypeStruct((LC, 128), F32),
                   jax.ShapeDtypeStruct((LC, 128), F32), jax.ShapeDtypeStruct((8, 128), F32)]
        + _Exchange.out_shapes(exchange),
        scratch_shapes=_Exchange.semaphores(ne) if ne else [],
        compiler_params=_cp("arbitrary"),
    )(q0, q1, kp, kswp, vp, vswp, kc, kcsw, vc, vcsw, sink, o, do, *exchange)
    return res[:6], res[6:]


def _mix_in_bwd(dq, dk, dv, dga, dgb, proj, x, dx1, win, nw, sc, qnw2, knw2, cos, sa, sb, gmat):
    s = x.shape[0]

    def body(dq_ref, dk_ref, dv_ref, dga_ref, dgb_ref, proj_ref, x_ref, dx1_ref, win_ref, nw_ref, sc_ref,
             qnw_ref, knw_ref, cos_ref, sa_ref, sb_ref, g_ref, gx_ref, dproj_ref, vec_ref):
        @pl.when(pl.program_id(0) == 0)
        def _():
            vec_ref[...] = jnp.zeros((8, D), F32)
        cs, sav, sbv, g = cos_ref[...], sa_ref[...], sb_ref[...], g_ref[...]
        gq = jnp.zeros((1, 128), F32)
        for j in range(4):
            dqn = _rope_bwd(dq_ref[:, 128 * j:128 * (j + 1)], cs, sav, sbv)
            dch, gw = _headnorm_bwd(proj_ref[:, 128 * j:128 * (j + 1)], dqn, qnw_ref[...], g)
            dproj_ref[:, 128 * j:128 * (j + 1)] = dch.astype(_MX)
            gq = gq + _rowsum(gw)
        dkn = _rope_bwd(dk_ref[...], cs, sav, sbv)
        dch, gw = _headnorm_bwd(proj_ref[:, 512:640], dkn, knw_ref[...], g)
        dproj_ref[:, 512:640] = dch.astype(_MX)
        dproj_ref[:, 640:768] = dv_ref[...].astype(_MX)
        dproj_ref[:, 768:1280] = dga_ref[...]
        dproj_ref[:, 1280:1792] = dgb_ref[...]
        vec_ref[3:4, 0:128] += gq
        vec_ref[3:4, 128:256] += _rowsum(gw)
        dh = _dot_nt(dproj_ref[...], win_ref[...])
        xv = x_ref[...]
        r = lax.rsqrt(jnp.mean(xv * xv, axis=-1, keepdims=True) + EPS)
        xn = xv * r
        nwv, scv = nw_ref[...], sc_ref[...]
        vec_ref[0:1, :] += _rowsum(dh)
        vec_ref[1:2, :] += _rowsum(dh * xn) * nwv
        vec_ref[2:3, :] += _rowsum(dh * xn) * (1.0 + scv)
        dxn = dh * (nwv * (1.0 + scv))
        gx_ref[...] = dx1_ref[...] + r * (dxn - xn * jnp.mean(dxn * xn, axis=-1, keepdims=True))

    row = lambda n: pl.BlockSpec((1, n), lambda i: (0, 0))
    tile = lambda n: pl.BlockSpec((TM, n), lambda i: (i, 0))
    return pl.pallas_call(
        body, name="mix_in_bwd", grid=(s // TM,),
        in_specs=[tile(AW), tile(128), tile(128), tile(CW), tile(CW), tile(INW), tile(D), tile(D),
                  pl.BlockSpec((D, INW), lambda i: (0, 0)), row(D), row(D), row(128), row(128),
                  tile(128), tile(128), tile(128), pl.BlockSpec((128, 128), lambda i: (0, 0))],
        out_specs=[tile(D), tile(INW), pl.BlockSpec((8, D), lambda i: (0, 0))],
        out_shape=[jax.ShapeDtypeStruct((s, D), F32), jax.ShapeDtypeStruct((s, INW), _MX),
                   jax.ShapeDtypeStruct((8, D), F32)],
        compiler_params=_cp("arbitrary"),
    )(dq, dk, dv, dga, dgb, proj, x, dx1, win, nw, sc, qnw2, knw2, cos, sa, sb, gmat)


def _ctx_bwd(ctx, nw, scc, winkv, kvc, hc, dkc, dvc, knw2, gmat):
    def body(ctx_ref, nw_ref, sc_ref, w_ref, kvc_ref, hc_ref, dkc_ref, dvc_ref, knw_ref, g_ref, gw_ref, vec_ref):
        dkr, gk = _headnorm_bwd(kvc_ref[:, 0:128], dkc_ref[...], knw_ref[...], g_ref[...])
        dkv = jnp.concatenate([dkr, dvc_ref[...]], axis=1).astype(_MX)
        gw_ref[...] = _dot_tn(hc_ref[...], dkv)
        dh = _dot_nt(dkv, w_ref[...])
        cv = ctx_ref[...]
        r = lax.rsqrt(jnp.mean(cv * cv, axis=-1, keepdims=True) + EPS)
        cn = cv * r
        vec_ref[...] = jnp.zeros((8, D), F32)
        vec_ref[0:1, :] = _rowsum(dh)
        vec_ref[1:2, :] = _rowsum(dh * cn) * nw_ref[...]
        vec_ref[2:3, :] = _rowsum(dh * cn) * (1.0 + sc_ref[...])
        vec_ref[3:4, 0:128] = _rowsum(gk)

    return pl.pallas_call(
        body, name="ctx_bwd", in_specs=[_vspec()] * 10, out_specs=[_vspec()] * 2,
        out_shape=[jax.ShapeDtypeStruct((D, 256), F32), jax.ShapeDtypeStruct((8, D), F32)],
        compiler_params=pltpu.CompilerParams(vmem_limit_bytes=VMEM_LIMIT),
    )(ctx, nw, scc, winkv, kvc, hc, dkc, dvc, knw2, gmat)


def _mod_fwd(cs, wmod, bloc):
    def body(c_ref, w_ref, b_ref, o_ref):
        cv = c_ref[...]
        o_ref[...] = _dot((cv * _sig(cv)).astype(_MX), w_ref[...].astype(_MX)) + b_ref[...]

    return pl.pallas_call(
        body, name="mod_fwd", in_specs=[_vspec()] * 3, out_specs=_vspec(),
        out_shape=jax.ShapeDtypeStruct((16, wmod.shape[1]), F32),
        compiler_params=pltpu.CompilerParams(vmem_limit_bytes=VMEM_LIMIT),
    )(cs, wmod, bloc)


def _mod_bwd(cs, dmod, wmod):
    def body(c_ref, d_ref, w_ref, gw_ref, part_ref):
        cv = c_ref[...]
        sl = (cv * _sig(cv)).astype(_MX)
        dm = d_ref[...].astype(_MX)
        gw_ref[...] = _dot_tn(sl, dm)
        part_ref[...] = _dot_nt(dm, w_ref[...].astype(_MX))

    return pl.pallas_call(
        body, name="mod_bwd", in_specs=[_vspec()] * 3, out_specs=[_vspec()] * 2,
        out_shape=[jax.ShapeDtypeStruct(wmod.shape, F32), jax.ShapeDtypeStruct((16, D), F32)],
        compiler_params=pltpu.CompilerParams(vmem_limit_bytes=VMEM_LIMIT),
    )(cs, dmod, wmod)


def _sum_leading(a, name):
    n = a.shape[0]

    def body(a_ref, o_ref):
        acc = a_ref[0]
        for k in range(1, n):
            acc = acc + a_ref[k]
        o_ref[...] = acc

    return pl.pallas_call(
        body, name=name, in_specs=[_vspec()], out_specs=_vspec(),
        out_shape=jax.ShapeDtypeStruct(a.shape[1:], F32),
        compiler_params=pltpu.CompilerParams(vmem_limit_bytes=VMEM_LIMIT),
    )(a)


def _cctx_grad(parts, cc):
    def body(p_ref, c_ref, o_ref):
        acc = p_ref[0, 8:9, :]
        for k in range(1, 4):
            acc = acc + p_ref[k, 8:9, :]
        cv = c_ref[...]
        sg = _sig(cv)
        o_ref[...] = acc * (sg * (1.0 + cv * (1.0 - sg)))

    return pl.pallas_call(
        body, name="cctx_grad", in_specs=[_vspec()] * 2, out_specs=_vspec(),
        out_shape=jax.ShapeDtypeStruct((1, D), F32),
    )(parts, cc)


def _adam_math(w, g, m, v):
    mn = ADAM_B1 * m + (1.0 - ADAM_B1) * g
    vn = ADAM_B2 * v + (1.0 - ADAM_B2) * (g * g)
    mh = mn / (1.0 - ADAM_B1 ** ADAM_STEP)
    vh = vn / (1.0 - ADAM_B2 ** ADAM_STEP)
    delta = -ADAM_LR * (mh / (jnp.sqrt(vh) + ADAM_EPS) + ADAM_WD * w)
    return delta, mn, vn


def _adam_big(w, g, m, v, name):
    r, n = w.shape
    tr = 256 if r % 256 == 0 else 64

    def body(w_ref, g_ref, m_ref, v_ref, d_ref, mo_ref, vo_ref):
        d, mn, vn = _adam_math(w_ref[...], g_ref[...], m_ref[...], v_ref[...])
        d_ref[...] = d
        mo_ref[...] = mn
        vo_ref[...] = vn

    spec = pl.BlockSpec((tr, n), lambda i: (i, 0))
    return pl.pallas_call(
        body, name=name, grid=(r // tr,), in_specs=[spec] * 4, out_specs=[spec] * 3,
        out_shape=[jax.ShapeDtypeStruct((r, n), F32)] * 3,
        compiler_params=_cp("arbitrary"),
    )(w, g, m, v)


def _adam_small(ws, gs, ms, vs):
    n = len(ws)

    def body(*refs):
        ins, outs = refs[:4 * n], refs[4 * n:]
        for k in range(n):
            d, mn, vn = _adam_math(ins[k][...], ins[n + k][...], ins[2 * n + k][...], ins[3 * n + k][...])
            outs[k][...] = d
            outs[n + k][...] = mn
            outs[2 * n + k][...] = vn

    shapes = [jax.ShapeDtypeStruct(w.shape, F32) for w in ws]
    res = pl.pallas_call(
        body, name="adam_small", in_specs=[_vspec()] * (4 * n), out_specs=[_vspec()] * (3 * n),
        out_shape=shapes * 3,
    )(*ws, *gs, *ms, *vs)
    return res[:n], res[n:2 * n], res[2 * n:]


def _pair_sum(grads, from_sib, core, name):
    n = len(grads)

    def body(c_ref, *refs):
        for w in range(n):
            a_ref, b_ref, o_ref = refs[w], refs[n + w], refs[2 * n + w]
            o_ref[...] = (a_ref[...].astype(F32) + b_ref[...].astype(F32)).astype(_MX)

    halves = [(None, g.shape[1] // 2, g.shape[2]) for g in grads]
    return pl.pallas_call(
        body, name=name,
        grid_spec=pltpu.PrefetchScalarGridSpec(
            num_scalar_prefetch=1, grid=(4,),
            in_specs=[pl.BlockSpec(h, lambda j, c: (j, c[0], 0)) for h in halves]
            + [pl.BlockSpec(h, lambda j, c: (j, 0, 0)) for h in halves],
            out_specs=[pl.BlockSpec(h, lambda j, c: (j, 0, 0)) for h in halves]),
        out_shape=[jax.ShapeDtypeStruct(p.shape, _MX) for p in from_sib],
        compiler_params=_cp("arbitrary"),
    )(core.reshape(1), *grads, *from_sib)


def _chip_sum(parts, arrived, chip, core):
    n = len(parts)

    def body(s_ref, *refs):
        for w in range(n):
            own_ref, p_ref, o_ref = refs[w], refs[n + w], refs[2 * n + w]
            acc = own_ref[...].astype(F32)
            for k in range(3):
                acc = acc + p_ref[k].astype(F32)
            o_ref[...] = acc

    blk = [(p.shape[1] // 2, p.shape[2]) for p in parts]
    return pl.pallas_call(
        body, name="grad_chip_sum",
        grid_spec=pltpu.PrefetchScalarGridSpec(
            num_scalar_prefetch=1, grid=(2,),
            in_specs=[pl.BlockSpec((None,) + b, lambda i, s: (s[0], i, 0)) for b in blk]
            + [pl.BlockSpec((3,) + b, lambda i, s: (0, i, 0)) for b in blk],
            out_specs=[pl.BlockSpec(b, lambda i, s: (2 * s[1] + i, 0)) for b in blk]),
        out_shape=[jax.ShapeDtypeStruct((2 * p.shape[1], p.shape[2]), F32) for p in parts],
        compiler_params=_cp("arbitrary"),
    )(jnp.stack([chip, core]), *parts, *arrived)


def _cast_shards(ws, chip):
    n = len(ws)

    def body(s_ref, *refs):
        for w in range(n):
            refs[n + w][...] = refs[w][...].astype(_MX)

    blk = [(a.shape[0] // 2, a.shape[1]) for a in ws]
    return pl.pallas_call(
        body, name="cast_shards",
        grid_spec=pltpu.PrefetchScalarGridSpec(
            num_scalar_prefetch=1, grid=(2,),
            in_specs=[pl.BlockSpec(b, lambda i, s: (i, 0)) for b in blk],
            out_specs=[pl.BlockSpec((None,) + b, lambda i, s: (s[0], i, 0)) for b in blk]),
        out_shape=[jax.ShapeDtypeStruct((4,) + a.shape, _MX) for a in ws],
        compiler_params=_cp("arbitrary"),
    )(chip.reshape(1), *ws)


def _position():
    x, y, c = lax.axis_index("x"), lax.axis_index("y"), lax.axis_index("c")
    return x, y, c


def _small_allgather(blk, name):
    m_per, n = blk.shape

    def body(x_ref, out_ref, send_sems, recv_sems, local_sem):
        x, y, c = _position()
        me, sibling = (x, y, c), (x, y, 1 - c)
        chips = [(1 - x, y), (x, 1 - y), (1 - x, 1 - y)]

        def rows(px, py, pc):
            return out_ref.at[pl.ds((4 * px + 2 * py + pc) * m_per, m_per), :]

        def copy(k, block, to, src=None):
            return pltpu.make_async_remote_copy(
                src_ref=rows(*block) if src is None else src, dst_ref=rows(*block),
                send_sem=send_sems.at[k], recv_sem=recv_sems.at[k], device_id=to, device_id_type=MESH)

        mine = pltpu.make_async_copy(x_ref, rows(*me), local_sem)
        mine.start()
        first = [copy(0, me, sibling, src=x_ref)]
        first += [copy(1 + j, me, (*chip, c), src=x_ref) for j, chip in enumerate(chips)]
        for cp in first:
            cp.start()
        passed = [copy(4 + j, (*chip, c), sibling) for j, chip in enumerate(chips)]
        for j, chip in enumerate(chips):
            copy(1 + j, (*chip, c), me).wait_recv()
            passed[j].start()
        copy(0, sibling, me).wait_recv()
        for j, chip in enumerate(chips):
            copy(4 + j, (*chip, 1 - c), me).wait_recv()
        for cp in first + passed:
            cp.wait_send()
        mine.wait()

    return pl.pallas_call(
        body, name=name, out_shape=jax.ShapeDtypeStruct((8 * m_per, n), blk.dtype),
        in_specs=[_vspec()], out_specs=_vspec(),
        scratch_shapes=[pltpu.SemaphoreType.DMA((7,)), pltpu.SemaphoreType.DMA((7,)), pltpu.SemaphoreType.DMA],
    )(blk)


def _any_specs(n):
    return [pl.BlockSpec(memory_space=pl.ANY)] * n


def _rows(ref, half, nrows):
    return ref.at[pl.ds(half * (nrows // 2), nrows // 2), :]


def _weight_allgather(bufs):
    n = len(bufs)

    def body(*refs):
        gather = _Gather(refs[:n], refs[n:2 * n], refs[2 * n], refs[2 * n + 1], [b.shape[1] for b in bufs])
        gather.start()
        gather.forward()
        gather.finish()

    return pl.pallas_call(
        body, name="weight_allgather",
        out_shape=[jax.ShapeDtypeStruct(a.shape, a.dtype) for a in bufs],
        in_specs=_any_specs(n), out_specs=_any_specs(n), input_output_aliases={w: w for w in range(n)},
        scratch_shapes=_Gather.semaphores(n),
    )(*bufs)


class _Gather:
    def __init__(self, src_refs, out_refs, send_sems, recv_sems, nrows):
        x, y, c = _position()
        chip = 2 * x + y
        sibling = (x, y, 1 - c)

        def copy(k, src, dst, to):
            return pltpu.make_async_remote_copy(src_ref=src, dst_ref=dst, send_sem=send_sems.at[k],
                                                recv_sem=recv_sems.at[k], device_id=to, device_id_type=MESH)

        self.first, self.first_landed, self.passed, self.passed_landed = [], [], [], []
        for w, r in enumerate(nrows):
            for j, ch in enumerate([(1 - x, y), (x, 1 - y), (1 - x, 1 - y)]):
                theirs = _rows(out_refs[w].at[2 * ch[0] + ch[1]], c, r)
                other = _rows(out_refs[w].at[2 * ch[0] + ch[1]], 1 - c, r)
                self.first.append(copy(6 * w + j, _rows(src_refs[w].at[chip], c, r),
                                       _rows(out_refs[w].at[chip], c, r), (*ch, c)))
                self.first_landed.append(copy(6 * w + j, theirs, theirs, sibling))
                self.passed.append(copy(6 * w + 3 + j, theirs, theirs, sibling))
                self.passed_landed.append(copy(6 * w + 3 + j, other, other, sibling))

    @staticmethod
    def semaphores(n):
        return [pltpu.SemaphoreType.DMA((6 * n,)), pltpu.SemaphoreType.DMA((6 * n,))]

    def start(self):
        for cp in self.first:
            cp.start()

    def forward(self):
        for landed, cp in zip(self.first_landed, self.passed):
            landed.wait_recv()
            cp.start()

    def finish(self):
        for landed in self.passed_landed:
            landed.wait_recv()
        for cp in self.first + self.passed:
            cp.wait_send()


def _grad_swap(grads, name):
    n = len(grads)

    def body(*refs):
        g_refs, o_refs = refs[:n], refs[n:2 * n]
        send_sems, recv_sems = refs[2 * n:]
        x, y, c = _position()
        cps = []
        for w in range(n):
            h = grads[w].shape[1] // 2
            cps.append(pltpu.make_async_remote_copy(
                src_ref=g_refs[w].at[:, pl.ds((1 - c) * h, h), :], dst_ref=o_refs[w], send_sem=send_sems.at[w],
                recv_sem=recv_sems.at[w], device_id=(x, y, 1 - c), device_id_type=MESH))
        for cp in cps:
            cp.start()
        for cp in cps:
            cp.wait()

    return pl.pallas_call(
        body, name=name,
        out_shape=[jax.ShapeDtypeStruct((4, g.shape[1] // 2, g.shape[2]), g.dtype) for g in grads],
        in_specs=_any_specs(n), out_specs=_any_specs(n),
        scratch_shapes=[pltpu.SemaphoreType.DMA((n,)), pltpu.SemaphoreType.DMA((n,))],
    )(*grads)


class _Exchange:
    def __init__(self, part_refs, out_refs, send_sems, recv_sems):
        x, y, c = _position()
        self.copies = [
            pltpu.make_async_remote_copy(
                src_ref=part_refs[w].at[2 * ch[0] + ch[1]], dst_ref=out_refs[w].at[k], send_sem=send_sems.at[3 * w + k],
                recv_sem=recv_sems.at[3 * w + k], device_id=(*ch, c), device_id_type=MESH)
            for w in range(len(part_refs)) for k, ch in enumerate([(1 - x, y), (x, 1 - y), (1 - x, 1 - y)])]

    @staticmethod
    def semaphores(n):
        return [pltpu.SemaphoreType.DMA((3 * n,)), pltpu.SemaphoreType.DMA((3 * n,))]

    @staticmethod
    def out_shapes(parts):
        return [jax.ShapeDtypeStruct((3,) + p.shape[1:], p.dtype) for p in parts]

    def start(self):
        for cp in self.copies:
            cp.start()

    def finish(self):
        for cp in self.copies:
            cp.wait()


def _chip_exchange(parts, name):
    n = len(parts)

    def body(*refs):
        exchange = _Exchange(refs[:n], refs[n:2 * n], refs[2 * n], refs[2 * n + 1])
        exchange.start()
        exchange.finish()

    return pl.pallas_call(
        body, name=name, out_shape=_Exchange.out_shapes(parts),
        in_specs=_any_specs(n), out_specs=_any_specs(n), scratch_shapes=_Exchange.semaphores(n),
    )(*parts)


def _grad_share(bufs):
    n = len(bufs)

    def body(*refs):
        s_refs, o_refs = refs[:n], refs[n:2 * n]
        send_sems, recv_sems = refs[2 * n:]
        x, y, c = _position()
        cps = []
        for w in range(n):
            r = bufs[w].shape[0]
            cps.append(pltpu.make_async_remote_copy(
                src_ref=_rows(s_refs[w], c, r), dst_ref=_rows(o_refs[w], c, r), send_sem=send_sems.at[w],
                recv_sem=recv_sems.at[w], device_id=(x, y, 1 - c), device_id_type=MESH))
        for cp in cps:
            cp.start()
        for cp in cps:
            cp.wait()

    return pl.pallas_call(
        body, name="grad_share_half", out_shape=[jax.ShapeDtypeStruct(a.shape, a.dtype) for a in bufs],
        in_specs=_any_specs(n), out_specs=_any_specs(n), input_output_aliases={w: w for w in range(n)},
        scratch_shapes=[pltpu.SemaphoreType.DMA((n,)), pltpu.SemaphoreType.DMA((n,))],
    )(*bufs)


def _rope_tables(s):
    t = jnp.arange(s)
    inv = 10000.0 ** (-jnp.arange(0, 32, 2, dtype=F32) / 32.0)
    ang_r = (t // GRID_W).astype(F32)[:, None] * inv
    ang_c = (t % GRID_W).astype(F32)[:, None] * inv
    ang = jnp.concatenate([ang_r, ang_r, ang_c, ang_c] * 2, axis=-1)
    first = (jnp.arange(128) % 32) < 16
    sin = jnp.sin(ang)
    return jnp.cos(ang), jnp.where(first, -sin, 0.0), jnp.where(first, 0.0, sin)


def _local_step(x, ctx, tgt, mod, modc, nw1, win, qnw, knw, sink, cw, cb, cnw, cnb, wout, nw2, wup, fcw, fcb, wdown4,
                core=None):
    s = x.shape[0]
    sh1, sc1, g1, sh2, sc2, g2 = [mod[:, D * k:D * (k + 1)] for k in range(6)]
    shc, scc = modc[:, :D], modc[:, D:2 * D]
    cos, sa, sb = _rope_tables(s)
    gi = jnp.arange(128) // 64
    gmat = (gi[:, None] == gi[None, :]).astype(_MX)
    qnw2, knw2 = jnp.tile(qnw, (1, 2)), jnp.tile(knw, (1, 2))
    cw32 = jnp.pad(cw, ((0, 1), (0, 0)))
    fcw2 = jnp.pad(fcw, ((0, 5), (0, 0))).reshape(8, 2, FH).transpose(1, 0, 2)
    fcb2 = fcb.reshape(2, 1, FH)
    winkv = win[:, 512:768]
    sinkv = sink.reshape(8)

    proj, h, q0, q1, k, ksw, v, vsw, u0 = _mix_in(x, nw1, sc1, sh1, win, qnw2, knw2, cos, sa, sb, gmat)
    kvc, hc, kc, kcsw, vc, vcsw = _ctx_kv(ctx, nw1, scc, shc, winkv, knw2, gmat)
    padr = lambda a: jnp.pad(a, ((TQ, TQ), (0, 0)))
    kp, kswp, vp, vswp = padr(k), padr(ksw), padr(v), padr(vsw)
    o, gathered = _attn_fwd(q0, q1, kp, kswp, vp, vswp, kc, kcsw, vc, vcsw, sinkv,
                            gather=() if core is None else (wup, wdown4))
    if core is not None:
        wup, wdown4 = gathered
    wdown = wdown4.reshape(FH, D)
    u1 = _conv31(u0, cw32, cb)
    x1, mix, cat = _mix_out(o, u1, cnw, cnb, wout, x, g1)
    h2, up0 = _ffn_up(x1, nw2, sc2, sh2, wup)
    act = _ffn_conv_act(up0, fcw2, fcb2)
    dy, dact, ddn, vec_dn = _ffn_down(act, wdown, x1, g2, tgt)
    loss = (0.5 / D) * jnp.sum(vec_dn[1])

    dup, gfc = _ffn_conv_bwd(up0, dact, fcw2, fcb2)
    g_wdown = _tn_matmul(act, ddn, FQ, D, "gw_down").reshape(4, FH // 4, D)
    g_wup = _tn_matmul(h2, dup, D, FQ, "gw_up", b_split=True, by_chip=True)
    parts = ()
    if core is not None:
        late = [g_wup, g_wdown]
        parts = _pair_sum(late, _grad_swap(late, "grad_swap_up_down"), core, "grad_pair_sum_up_down")
    dx1, dmix, vec_up = _ffn_up_bwd(dup, wup, x1, dy, mix, nw2, sc2, g1)
    g_wout = _tn_matmul(cat, dmix, D, D, "gw_out")
    do, du1, vec_ln = _mix_out_bwd(dmix, wout, u1, cnw, cnb)
    dga, dgb, gcw = _conv31_bwd(du1, u0, cw32, proj)
    (dq, dkp, dvp, dkc, dvc, dsink), arrived = _attn_bwd(q0, q1, kp, kswp, vp, vswp, kc, kcsw, vc, vcsw, sinkv, o, do,
                                                         exchange=parts)
    if core is not None:
        g_wup, g_wdown = (parts[0], arrived[0]), (parts[1], arrived[1])
    gwinkv, vec_ctx = _ctx_bwd(ctx, nw1, scc, winkv, kvc, hc, dkc, dvc, knw2, gmat)
    gx, dproj, vec_in = _mix_in_bwd(dq, dkp[TQ:TQ + s], dvp[TQ:TQ + s], dga, dgb, proj, x, dx1, win, nw1, sc1,
                                    qnw2, knw2, cos, sa, sb, gmat)
    g_win = _tn_matmul(h, dproj, D, INW, "gw_in", init=gwinkv)

    g_qn = vec_in[3:4, 0:128]
    g_kn = vec_in[3:4, 128:256] + vec_ctx[3:4, 0:128]
    grads = dict(
        norm_mix_w=vec_in[2:3] + vec_ctx[2:3], w_in=g_win,
        q_norm_w=g_qn[:, :64] + g_qn[:, 64:], k_norm_w=g_kn[:, :64] + g_kn[:, 64:],
        sink_logit=dsink[:, 0].reshape(1, 8), conv_w=gcw[:CK], conv_b=gcw[CK:CK + 1],
        conv_norm_w=vec_ln[1:2], conv_norm_b=vec_ln[0:1], w_out=g_wout, norm_ffn_w=vec_up[2:3], w_up=g_wup,
        ffn_conv_w=gfc[:, 0:3].transpose(1, 0, 2).reshape(3, 2 * FH), ffn_conv_b=gfc[:, 3].reshape(1, 2 * FH),
        w_down=g_wdown)
    dmod = jnp.concatenate([vec_in[0:1], vec_in[1:2], vec_up[3:4], vec_up[0:1], vec_up[1:2], vec_dn[0:1]], axis=1)
    dmodc = jnp.concatenate([vec_ctx[0:1], vec_ctx[1:2]], axis=1)
    return loss, gx, grads, dmod, dmodc


_SMALL = ["c_ctx", "b_mod", "norm_mix_w", "q_norm_w", "k_norm_w", "sink_logit", "conv_w", "conv_b", "conv_norm_w",
          "conv_norm_b", "norm_ffn_w", "ffn_conv_w", "ffn_conv_b"]
_GATHERED = ["w_in", "w_out", "w_up", "w_down"]
_BIG = ["w_mod"] + _GATHERED
_ORDER = ["c_ctx", "w_mod", "b_mod", "norm_mix_w", "w_in", "q_norm_w", "k_norm_w", "sink_logit", "conv_w", "conv_b",
          "conv_norm_w", "conv_norm_b", "w_out", "norm_ffn_w", "w_up", "ffn_conv_w", "ffn_conv_b", "w_down"]
_PACK = [("norm_mix_w", 1), ("norm_ffn_w", 1), ("q_norm_w", 1), ("k_norm_w", 1), ("sink_logit", 1), ("conv_b", 1),
         ("conv_norm_w", 1), ("conv_norm_b", 1), ("ffn_conv_b", 6), ("conv_w", 16), ("ffn_conv_w", 17)]
_PACK_ROWS = 56


def _pack_rows(a, nrows):
    flat = a.reshape(-1)
    return jnp.pad(flat, (0, nrows * D - flat.shape[0])).reshape(nrows, D)


def kernel(x, c, ctx, c_ctx, w_mod, b_mod, norm_mix_w, w_in, q_norm_w, k_norm_w, sink_logit, conv_w, conv_b, conv_norm_w, conv_norm_b, w_out, norm_ffn_w, w_up, ffn_conv_w, ffn_conv_b, w_down, loss_target, m_c_ctx, m_w_mod, m_b_mod, m_norm_mix_w, m_w_in, m_q_norm_w, m_k_norm_w, m_sink_logit, m_conv_w, m_conv_b, m_conv_norm_w, m_conv_norm_b, m_w_out, m_norm_ffn_w, m_w_up, m_ffn_conv_w, m_ffn_conv_b, m_w_down, v_c_ctx, v_w_mod, v_b_mod, v_norm_mix_w, v_w_in, v_q_norm_w, v_k_norm_w, v_sink_logit, v_conv_w, v_conv_b, v_conv_norm_w, v_conv_norm_b, v_w_out, v_norm_ffn_w, v_w_up, v_ffn_conv_w, v_ffn_conv_b, v_w_down):
    w = dict(c_ctx=c_ctx.reshape(1, D), w_mod=w_mod[0], b_mod=b_mod, norm_mix_w=norm_mix_w, w_in=w_in[0],
             q_norm_w=q_norm_w, k_norm_w=k_norm_w, sink_logit=sink_logit, conv_w=conv_w[0], conv_b=conv_b,
             conv_norm_w=conv_norm_w, conv_norm_b=conv_norm_b, w_out=w_out[0], norm_ffn_w=norm_ffn_w, w_up=w_up[0],
             ffn_conv_w=ffn_conv_w[0], ffn_conv_b=ffn_conv_b, w_down=w_down[0])
    m = dict(c_ctx=m_c_ctx.reshape(1, D), w_mod=m_w_mod[0], b_mod=m_b_mod, norm_mix_w=m_norm_mix_w, w_in=m_w_in[0],
             q_norm_w=m_q_norm_w, k_norm_w=m_k_norm_w, sink_logit=m_sink_logit, conv_w=m_conv_w[0], conv_b=m_conv_b,
             conv_norm_w=m_conv_norm_w, conv_norm_b=m_conv_norm_b, w_out=m_w_out[0], norm_ffn_w=m_norm_ffn_w,
             w_up=m_w_up[0], ffn_conv_w=m_ffn_conv_w[0], ffn_conv_b=m_ffn_conv_b, w_down=m_w_down[0])
    v = dict(c_ctx=v_c_ctx.reshape(1, D), w_mod=v_w_mod[0], b_mod=v_b_mod, norm_mix_w=v_norm_mix_w, w_in=v_w_in[0],
             q_norm_w=v_q_norm_w, k_norm_w=v_k_norm_w, sink_logit=v_sink_logit, conv_w=v_conv_w[0], conv_b=v_conv_b,
             conv_norm_w=v_conv_norm_w, conv_norm_b=v_conv_norm_b, w_out=v_w_out[0], norm_ffn_w=v_norm_ffn_w,
             w_up=v_w_up[0], ffn_conv_w=v_ffn_conv_w[0], ffn_conv_b=v_ffn_conv_b, w_down=v_w_down[0])
    xi, yi, ci = _position()
    chip = 2 * xi + yi
    dev = 2 * chip + ci
    s = x.shape[1]
    ncol = w["w_mod"].shape[1]

    blk0 = jnp.concatenate([_pack_rows(w["conv_w"], 4), _pack_rows(w["ffn_conv_w"], 5), c,
                            jnp.zeros((6, D), F32)], axis=0)
    g0 = _small_allgather(blk0, "gather_c_convw").reshape(8, 16, D)
    c_all = g0[:, 9, :]
    cs = jnp.concatenate([c_all, w["c_ctx"], jnp.zeros((7, D), F32)], axis=0)
    cw_full = jnp.concatenate([g0[2 * j, 0:4].reshape(-1)[:CK * 128].reshape(CK, 128) for j in range(4)], axis=1)
    fcw_full = jnp.concatenate([g0[2 * j, 4:9].reshape(-1)[:3 * 1408].reshape(3, 1408) for j in range(4)], axis=1)

    b_loc = lax.dynamic_slice(w["b_mod"], (0, chip * ncol), (1, ncol))
    modp = _mod_fwd(cs, w["w_mod"], b_loc)
    gm = _small_allgather(modp, "gather_mod").reshape(8, 16, ncol)
    mod_all = jnp.concatenate([gm[2 * j] for j in range(4)], axis=1)
    mod = lax.dynamic_slice(mod_all, (dev, 0), (1, 6 * D))
    modc = mod_all[8:9]

    win_buf, wout_buf, wup_buf, wdown_buf = _cast_shards([w[n] for n in _GATHERED], chip)
    win4, wout4 = _weight_allgather([win_buf, wout_buf])
    win = win4.transpose(1, 0, 2).reshape(D, INW)
    wout = wout4.reshape(D, D)

    loss_loc, gx, gl, dmod, dmodc = _local_step(
        x[0], ctx[0], loss_target[0], mod, modc, w["norm_mix_w"], win, w["q_norm_w"], w["k_norm_w"], w["sink_logit"],
        cw_full, w["conv_b"], w["conv_norm_w"], w["conv_norm_b"], wout, w["norm_ffn_w"], wup_buf, fcw_full,
        w["ffn_conv_b"], wdown_buf, core=ci)
    loss = lax.psum(loss_loc, ("x", "y", "c"))

    pack = jnp.concatenate([dmod.reshape(6, D), dmodc.reshape(2, D)]
                           + [_pack_rows(gl[name], nr) for name, nr in _PACK], axis=0)
    pack = jnp.pad(pack, ((0, _PACK_ROWS - pack.shape[0]), (0, 0)))
    gp = _small_allgather(pack, "gather_small_grads").reshape(8, _PACK_ROWS, D)
    tot = _sum_leading(gp, "sum_small_grads")
    g = {}
    r = 8
    for name, nr in _PACK:
        shape = gl[name].shape
        g[name] = tot[r:r + nr].reshape(-1)[:math.prod(shape)].reshape(shape)
        r += nr
    dmod_all = jnp.concatenate([gp[:, 0:6].reshape(8, 6 * D),
                                jnp.pad(tot[6:8].reshape(1, 2 * D), ((0, 0), (0, 4 * D))),
                                jnp.zeros((7, 6 * D), F32)], axis=0)
    g["b_mod"] = _sum_leading(dmod_all.reshape(16, 1, 6 * D), "grad_b_mod")
    g["w_mod"], part = _mod_bwd(cs, lax.dynamic_slice(dmod_all, (0, chip * ncol), (16, ncol)), w["w_mod"])
    parts = _small_allgather(part, "gather_cctx").reshape(8, 16, D)
    g["c_ctx"] = _cctx_grad(parts[0::2], w["c_ctx"])
    g["conv_w"] = lax.dynamic_slice(g["conv_w"], (0, chip * 128), (CK, 128))
    g["ffn_conv_w"] = lax.dynamic_slice(g["ffn_conv_w"], (0, chip * 1408), (3, 1408))

    early = [gl["w_in"].reshape(D, 4, INW // 4).transpose(1, 0, 2), gl["w_out"].reshape(4, D // 4, D)]
    parts = _pair_sum(early, _grad_swap(early, "grad_swap_in_out"), ci, "grad_pair_sum_in_out")
    arrived = _chip_exchange(parts, "chip_exchange_in_out")
    g["w_in"], g["w_out"], g["w_up"], g["w_down"] = _grad_share(_chip_sum(
        [*parts, gl["w_up"][0], gl["w_down"][0]], [*arrived, gl["w_up"][1], gl["w_down"][1]], chip, ci))

    delta, new_m, new_v = {}, {}, {}
    for name in _BIG:
        delta[name], new_m[name], new_v[name] = _adam_big(w[name], g[name], m[name], v[name], "adam_" + name)
    ds, ms, vs = _adam_small([w[n] for n in _SMALL], [g[n] for n in _SMALL], [m[n] for n in _SMALL],
                             [v[n] for n in _SMALL])
    for k, name in enumerate(_SMALL):
        delta[name], new_m[name], new_v[name] = ds[k], ms[k], vs[k]

    def shaped(d, name):
        a = d[name]
        if name == "c_ctx":
            return a.reshape(D)
        if name in ("w_mod", "w_in", "w_out", "w_up", "w_down", "conv_w", "ffn_conv_w"):
            return a[None]
        return a

    outs = [loss, gx[None]]
    for d in (g, delta, new_m, new_v):
        outs += [shaped(d, name) for name in _ORDER]
    return tuple(outs)
```

```python
import functools
import math

import jax
import jax.numpy as jnp
from jax import lax
from jax.experimental import pallas as pl
from jax.experimental.pallas import tpu as pltpu

F32 = jnp.float32
_MX = jnp.bfloat16
EPS = 1e-6
NEG = -1e30
D = 1024
AW = 512
CW = 512
INW = 1792
FH = 2816
LC = 256
CK = 31
GRID_W = 64
TM = 512
TQ = 128
NCH = FH // 128
FQ = 2 * FH // 4
VMEM_LIMIT = 56 * 1024 * 1024
MESH = pl.DeviceIdType.MESH

ADAM_LR, ADAM_B1, ADAM_B2, ADAM_EPS, ADAM_WD, ADAM_STEP = 0.001, 0.9, 0.999, 1e-08, 0.01, 10


def _cp(*sem):
    return pltpu.CompilerParams(dimension_semantics=sem, vmem_limit_bytes=VMEM_LIMIT)


def _vspec():
    return pl.BlockSpec(memory_space=pltpu.VMEM)


def _sig(z):
    return 1.0 / (1.0 + jnp.exp(-z))


def _dot(a, b):
    return jnp.dot(a, b, preferred_element_type=F32)


def _dot_nt(a, b):
    return lax.dot_general(a, b, (((1,), (1,)), ((), ())), preferred_element_type=F32)


def _dot_tn(a, b):
    return lax.dot_general(a, b, (((0,), (0,)), ((), ())), preferred_element_type=F32)


def _gsum(v, g):
    hi = v.astype(_MX)
    lo = (v - hi.astype(F32)).astype(_MX)
    return _dot(hi, g) + _dot(lo, g)


def _headnorm(ch, w, g):
    r = lax.rsqrt(_gsum(ch * ch, g) * (1.0 / 64.0) + EPS)
    return ch * r * w


def _headnorm_bwd(ch, dy, w, g):
    r = lax.rsqrt(_gsum(ch * ch, g) * (1.0 / 64.0) + EPS)
    hat = ch * r
    dhat = dy * w
    dch = r * (dhat - hat * (_gsum(dhat * hat, g) * (1.0 / 64.0)))
    return dch, dy * hat


def _rope(u, cos, sa, sb):
    return u * cos + pltpu.roll(u, 112, 1) * sa + pltpu.roll(u, 16, 1) * sb


def _rope_bwd(d, cos, sa, sb):
    return d * cos - pltpu.roll(d, 112, 1) * sa - pltpu.roll(d, 16, 1) * sb


def _rowsum(v):
    return jnp.sum(v, axis=0, keepdims=True)


def _mix_in(x, nw, sc, sh, win, qnw2, knw2, cos, sa, sb, gmat):
    s = x.shape[0]

    def body(x_ref, nw_ref, sc_ref, sh_ref, win_ref, qnw_ref, knw_ref, cos_ref, sa_ref, sb_ref, g_ref,
             proj_ref, h_ref, q0_ref, q1_ref, k_ref, ksw_ref, v_ref, vsw_ref, u0_ref):
        xv = x_ref[...]
        r = lax.rsqrt(jnp.mean(xv * xv, axis=-1, keepdims=True) + EPS)
        h = xv * r * (nw_ref[...] * (1.0 + sc_ref[...])) + sh_ref[...]
        hb = h.astype(_MX)
        h_ref[...] = hb
        proj = _dot(hb, win_ref[...])
        proj_ref[...] = proj
        cs, sav, sbv, g = cos_ref[...], sa_ref[...], sb_ref[...], g_ref[...]
        lo = lax.broadcasted_iota(jnp.int32, (1, 128), 1) < 64
        for j in range(4):
            q = _rope(_headnorm(proj[:, 128 * j:128 * (j + 1)], qnw_ref[...], g), cs, sav, sbv) * 0.125
            q0_ref[:, 128 * j:128 * (j + 1)] = jnp.where(lo, q, 0.0).astype(_MX)
            q1_ref[:, 128 * j:128 * (j + 1)] = jnp.where(lo, 0.0, q).astype(_MX)
        k = _rope(_headnorm(proj[:, 512:640], knw_ref[...], g), cs, sav, sbv)
        k_ref[...] = k.astype(_MX)
        ksw_ref[...] = pltpu.roll(k, 64, 1).astype(_MX)
        v = proj[:, 640:768]
        v_ref[...] = v.astype(_MX)
        vsw_ref[...] = pltpu.roll(v, 64, 1).astype(_MX)
        u0_ref[...] = proj[:, 768:1280] * _sig(proj[:, 1280:1792])

    row = lambda n: pl.BlockSpec((1, n), lambda i: (0, 0))
    tile = lambda n: pl.BlockSpec((TM, n), lambda i: (i, 0))
    return pl.pallas_call(
        body, name="mix_in", grid=(s // TM,),
        in_specs=[tile(D), row(D), row(D), row(D), pl.BlockSpec((D, INW), lambda i: (0, 0)), row(128), row(128),
                  tile(128), tile(128), tile(128), pl.BlockSpec((128, 128), lambda i: (0, 0))],
        out_specs=[tile(INW), tile(D), tile(AW), tile(AW), tile(128), tile(128), tile(128), tile(128), tile(CW)],
        out_shape=[jax.ShapeDtypeStruct((s, INW), F32), jax.ShapeDtypeStruct((s, D), _MX),
                   jax.ShapeDtypeStruct((s, AW), _MX), jax.ShapeDtypeStruct((s, AW), _MX),
                   jax.ShapeDtypeStruct((s, 128), _MX), jax.ShapeDtypeStruct((s, 128), _MX),
                   jax.ShapeDtypeStruct((s, 128), _MX), jax.ShapeDtypeStruct((s, 128), _MX),
                   jax.ShapeDtypeStruct((s, CW), F32)],
        compiler_params=_cp("arbitrary"),
    )(x, nw, sc, sh, win, qnw2, knw2, cos, sa, sb, gmat)


def _ctx_kv(ctx, nw, scc, shc, winkv, knw2, gmat):
    def body(ctx_ref, nw_ref, sc_ref, sh_ref, w_ref, knw_ref, g_ref,
             kvc_ref, hc_ref, kc_ref, kcsw_ref, vc_ref, vcsw_ref):
        cv = ctx_ref[...]
        r = lax.rsqrt(jnp.mean(cv * cv, axis=-1, keepdims=True) + EPS)
        hc = (cv * r * (nw_ref[...] * (1.0 + sc_ref[...])) + sh_ref[...]).astype(_MX)
        hc_ref[...] = hc
        kvc = _dot(hc, w_ref[...])
        kvc_ref[...] = kvc
        kc = _headnorm(kvc[:, :128], knw_ref[...], g_ref[...])
        kc_ref[...] = kc.astype(_MX)
        kcsw_ref[...] = pltpu.roll(kc, 64, 1).astype(_MX)
        vc = kvc[:, 128:]
        vc_ref[...] = vc.astype(_MX)
        vcsw_ref[...] = pltpu.roll(vc, 64, 1).astype(_MX)

    return pl.pallas_call(
        body, name="ctx_kv",
        in_specs=[_vspec()] * 7, out_specs=[_vspec()] * 6,
        out_shape=[jax.ShapeDtypeStruct((LC, 256), F32), jax.ShapeDtypeStruct((LC, D), _MX)]
        + [jax.ShapeDtypeStruct((LC, 128), _MX)] * 4,
        compiler_params=pltpu.CompilerParams(vmem_limit_bytes=VMEM_LIMIT),
    )(ctx, nw, scc, shc, winkv, knw2, gmat)


def _attn_mask(i, s):
    r = lax.broadcasted_iota(jnp.int32, (2 * TQ, LC + 3 * TQ), 0) % TQ
    cidx = lax.broadcasted_iota(jnp.int32, (2 * TQ, LC + 3 * TQ), 1)
    qpos = i * TQ + r
    kpos = (i - 1) * TQ + (cidx - LC)
    near = (jnp.abs(qpos - kpos) <= 128) & (kpos >= 0) & (kpos < s)
    return (cidx < LC) | near


def _attn_bias(i, s):
    return jnp.where(_attn_mask(i, s), 0.0, NEG)


def _attn_exp(sc, bias, sinkv):
    sc = sc + bias
    m = jnp.maximum(jnp.max(sc, axis=-1, keepdims=True), sinkv)
    ex = jnp.exp(sc - m)
    es = jnp.exp(sinkv - m)
    return ex, es, 1.0 / (jnp.sum(ex, axis=-1, keepdims=True) + es)


def _sink_rows(sink_ref, g, e):
    return jnp.concatenate([jnp.full((TQ, 1), sink_ref[4 * g + e], F32),
                            jnp.full((TQ, 1), sink_ref[4 * g + 2 + e], F32)], axis=0)


def _attn_fwd(q0, q1, kp, kswp, vp, vswp, kc, kcsw, vc, vcsw, sink, gather=()):
    s = q0.shape[0]
    nb = s // TQ
    ng = len(gather)

    def body(*refs):
        (q0_ref, q1_ref, kp_ref, kswp_ref, vp_ref, vswp_ref, kc_ref, kcsw_ref, vc_ref, vcsw_ref,
         sink_ref) = refs[:11]
        o_ref = refs[11 + ng]
        i = pl.program_id(0)
        if ng:
            gat = _Gather(refs[11:11 + ng], refs[12 + ng:12 + 2 * ng], refs[12 + 2 * ng], refs[13 + 2 * ng],
                          [b.shape[1] for b in gather])
            pl.when(i == 0)(gat.start)
        st = pl.multiple_of(i * TQ, TQ)
        kall = (jnp.concatenate([kc_ref[...], kp_ref[pl.ds(st, 3 * TQ), :]], axis=0),
                jnp.concatenate([kcsw_ref[...], kswp_ref[pl.ds(st, 3 * TQ), :]], axis=0))
        vall = (jnp.concatenate([vc_ref[...], vp_ref[pl.ds(st, 3 * TQ), :]], axis=0),
                jnp.concatenate([vcsw_ref[...], vswp_ref[pl.ds(st, 3 * TQ), :]], axis=0))
        bias = _attn_bias(i, s)
        lo = lax.broadcasted_iota(jnp.int32, (1, 128), 1) < 64
        qrefs = (q0_ref, q1_ref)
        out = [jnp.zeros((TQ, 128), F32) for _ in range(4)]
        combos = [(g, e) for g in range(2) for e in range(2)]
        scores = [_dot_nt(jnp.concatenate([qrefs[e][:, 256 * g:256 * g + 128],
                                           qrefs[e][:, 256 * g + 128:256 * g + 256]], axis=0),
                          kall[0 if e == g else 1]) for g, e in combos]
        parts = []
        for (g, e), sc in zip(combos, scores):
            ex, _, inv = _attn_exp(sc, bias, _sink_rows(sink_ref, g, e))
            parts.append((ex.astype(_MX), inv))
        for (g, e), (ex, inv) in zip(combos, parts):
            me = lo if e == 0 else jnp.logical_not(lo)
            o2 = _dot(ex, vall[0 if e == g else 1]) * inv
            out[2 * g] = out[2 * g] + jnp.where(me, o2[:TQ], 0.0)
            out[2 * g + 1] = out[2 * g + 1] + jnp.where(me, o2[TQ:], 0.0)
        for j in range(4):
            o_ref[:, 128 * j:128 * (j + 1)] = out[j]
        if ng:
            pl.when(i == max(nb - 3, 0))(gat.forward)
            pl.when(i == nb - 1)(gat.finish)

    full = lambda a: pl.BlockSpec(a.shape, lambda i: (0, 0))
    qs = pl.BlockSpec((TQ, AW), lambda i: (i, 0))
    res = pl.pallas_call(
        body, name="attn_fwd", grid=(nb,),
        in_specs=[qs, qs, full(kp), full(kp), full(kp), full(kp), full(kc), full(kc), full(kc), full(kc),
                  pl.BlockSpec(memory_space=pltpu.SMEM)] + _any_specs(ng),
        out_specs=[qs] + _any_specs(ng),
        out_shape=[jax.ShapeDtypeStruct((s, AW), F32)] + [jax.ShapeDtypeStruct(b.shape, b.dtype) for b in gather],
        input_output_aliases={11 + w: 1 + w for w in range(ng)},
        scratch_shapes=_Gather.semaphores(ng) if ng else [],
        compiler_params=_cp("arbitrary"),
    )(q0, q1, kp, kswp, vp, vswp, kc, kcsw, vc, vcsw, sink, *gather)
    return res[0], res[1:]


def _conv31(u0, cw32, cb):
    s = u0.shape[0]
    rch = 256

    def body(u_ref, w_ref, b_ref, o_ref, pad_ref):
        pad_ref[0:16, :] = jnp.zeros((16, 128), F32)
        pad_ref[s + 16:s + 32, :] = jnp.zeros((16, 128), F32)
        pad_ref[16:s + 16, :] = u_ref[...]
        for cidx in range(s // rch):
            base = cidx * rch
            acc = jnp.zeros((rch, 128), F32) + b_ref[...]
            for j in range(CK):
                acc = acc + w_ref[j:j + 1, :] * pad_ref[base + j + 1:base + j + 1 + rch, :]
            o_ref[base:base + rch, :] = acc

    return pl.pallas_call(
        body, name="conv31", grid=(CW // 128,),
        in_specs=[pl.BlockSpec((s, 128), lambda c: (0, c)), pl.BlockSpec((32, 128), lambda c: (0, c)),
                  pl.BlockSpec((1, 128), lambda c: (0, c))],
        out_specs=pl.BlockSpec((s, 128), lambda c: (0, c)),
        out_shape=jax.ShapeDtypeStruct((s, CW), F32),
        scratch_shapes=[pltpu.VMEM((s + 32, 128), F32)],
        compiler_params=_cp("arbitrary"),
    )(u0, cw32, cb)


def _ln_stats(u1):
    mu = jnp.mean(u1, axis=-1, keepdims=True)
    xc = u1 - mu
    rstd = lax.rsqrt(jnp.mean(xc * xc, axis=-1, keepdims=True) + EPS)
    return xc * rstd, rstd


def _mix_out(o, u1, cnw, cnb, wout, x, g1):
    s = x.shape[0]

    def body(o_ref, u1_ref, cnw_ref, cnb_ref, w_ref, x_ref, g1_ref, x1_ref, mix_ref, cat_ref):
        u2n, _ = _ln_stats(u1_ref[...])
        u2 = u2n * cnw_ref[...] + cnb_ref[...]
        u3 = u2 * _sig(u2)
        cat = jnp.concatenate([o_ref[...], u3], axis=1).astype(_MX)
        cat_ref[...] = cat
        mix = _dot(cat, w_ref[...])
        mix_ref[...] = mix
        x1_ref[...] = x_ref[...] + g1_ref[...] * mix

    row = lambda n: pl.BlockSpec((1, n), lambda i: (0, 0))
    tile = lambda n: pl.BlockSpec((TM, n), lambda i: (i, 0))
    return pl.pallas_call(
        body, name="mix_out", grid=(s // TM,),
        in_specs=[tile(AW), tile(CW), row(CW), row(CW), pl.BlockSpec((D, D), lambda i: (0, 0)), tile(D), row(D)],
        out_specs=[tile(D), tile(D), tile(D)],
        out_shape=[jax.ShapeDtypeStruct((s, D), F32), jax.ShapeDtypeStruct((s, D), F32),
                   jax.ShapeDtypeStruct((s, D), _MX)],
        compiler_params=_cp("arbitrary"),
    )(o, u1, cnw, cnb, wout, x, g1)


def _ffn_up(x1, nw, sc, sh, wup4):
    s = x1.shape[0]
    tm = min(1024, s)

    def body(x_ref, nw_ref, sc_ref, sh_ref, w_ref, h2_ref, up_ref, h2s):
        @pl.when(pl.program_id(1) == 0)
        def _():
            xv = x_ref[...]
            r = lax.rsqrt(jnp.mean(xv * xv, axis=-1, keepdims=True) + EPS)
            h2 = (xv * r * (nw_ref[...] * (1.0 + sc_ref[...])) + sh_ref[...]).astype(_MX)
            h2s[...] = h2
            h2_ref[...] = h2
        up_ref[...] = _dot(h2s[...], w_ref[...])

    row = pl.BlockSpec((1, D), lambda i, j: (0, 0))
    return pl.pallas_call(
        body, name="ffn_up", grid=(s // tm, 4),
        in_specs=[pl.BlockSpec((tm, D), lambda i, j: (i, 0)), row, row, row,
                  pl.BlockSpec((None, D, FQ), lambda i, j: (j, 0, 0))],
        out_specs=[pl.BlockSpec((tm, D), lambda i, j: (i, 0)),
                   pl.BlockSpec((None, tm, FQ), lambda i, j: (j // 2, i, j % 2))],
        out_shape=[jax.ShapeDtypeStruct((s, D), _MX), jax.ShapeDtypeStruct((2, s, FH), F32)],
        scratch_shapes=[pltpu.VMEM((tm, D), _MX)],
        compiler_params=_cp("arbitrary", "arbitrary"),
    )(x1, nw, sc, sh, wup4)


def _fill_pad8(pad_ref, val, s):
    pad_ref[0:8, :] = jnp.zeros((8, 128), F32)
    pad_ref[s + 8:s + 16, :] = jnp.zeros((8, 128), F32)
    pad_ref[8:s + 8, :] = val


def _conv3_at(pad_ref, w_ref, half, base, rch):
    return (w_ref[half, 0:1, :] * pad_ref[base + 7:base + 7 + rch, :]
            + w_ref[half, 1:2, :] * pad_ref[base + 8:base + 8 + rch, :]
            + w_ref[half, 2:3, :] * pad_ref[base + 9:base + 9 + rch, :])


def _ffn_conv_act(up0, fcw, fcb):
    s = up0.shape[1]
    rch = 256

    def body(up_ref, w_ref, b_ref, act_ref, padg, padv):
        _fill_pad8(padg, up_ref[0], s)
        _fill_pad8(padv, up_ref[1], s)
        for cidx in range(s // rch):
            base = cidx * rch
            gate = _conv3_at(padg, w_ref, 0, base, rch) + b_ref[0]
            val = _conv3_at(padv, w_ref, 1, base, rch) + b_ref[1]
            act_ref[base:base + rch, :] = (gate * _sig(gate) * val).astype(_MX)

    return pl.pallas_call(
        body, name="ffn_conv_act", grid=(NCH,),
        in_specs=[pl.BlockSpec((2, s, 128), lambda c: (0, 0, c)), pl.BlockSpec((2, 8, 128), lambda c: (0, 0, c)),
                  pl.BlockSpec((2, 1, 128), lambda c: (0, 0, c))],
        out_specs=pl.BlockSpec((s, 128), lambda c: (0, c)),
        out_shape=jax.ShapeDtypeStruct((s, FH), _MX),
        scratch_shapes=[pltpu.VMEM((s + 16, 128), F32)] * 2,
        compiler_params=_cp("arbitrary"),
    )(up0, fcw, fcb)


def _ffn_down(act, wdown, x1, g2, tgt):
    s = x1.shape[0]

    def body(act_ref, w_ref, x1_ref, g2_ref, tgt_ref, dy_ref, dact_ref, ddn_ref, vec_ref):
        @pl.when(pl.program_id(0) == 0)
        def _():
            vec_ref[...] = jnp.zeros((8, D), F32)
        dn = _dot(act_ref[...], w_ref[...])
        diff = x1_ref[...] + g2_ref[...] * dn - tgt_ref[...]
        dy = diff * (1.0 / D)
        dy_ref[...] = dy
        ddn = (dy * g2_ref[...]).astype(_MX)
        ddn_ref[...] = ddn
        dact_ref[...] = _dot_nt(ddn, w_ref[...])
        vec_ref[0:1, :] += _rowsum(dy * dn)
        vec_ref[1:2, :] += _rowsum(diff * diff)

    tile = lambda n: pl.BlockSpec((TM, n), lambda i: (i, 0))
    return pl.pallas_call(
        body, name="ffn_down", grid=(s // TM,),
        in_specs=[tile(FH), pl.BlockSpec((FH, D), lambda i: (0, 0)), tile(D), pl.BlockSpec((1, D), lambda i: (0, 0)),
                  tile(D)],
        out_specs=[tile(D), tile(FH), tile(D), pl.BlockSpec((8, D), lambda i: (0, 0))],
        out_shape=[jax.ShapeDtypeStruct((s, D), F32), jax.ShapeDtypeStruct((s, FH), F32),
                   jax.ShapeDtypeStruct((s, D), _MX), jax.ShapeDtypeStruct((8, D), F32)],
        compiler_params=_cp("arbitrary"),
    )(act, wdown, x1, g2, tgt)


def _ffn_conv_bwd(up0, dact, fcw, fcb):
    s = up0.shape[1]
    rch = 256

    def body(up_ref, da_ref, w_ref, b_ref, dup_ref, gw_ref, padg, padv, dpg, dpv):
        _fill_pad8(padg, up_ref[0], s)
        _fill_pad8(padv, up_ref[1], s)
        for p in (dpg, dpv):
            p[0:8, :] = jnp.zeros((8, 128), F32)
            p[s + 8:s + 16, :] = jnp.zeros((8, 128), F32)
        acc = [[jnp.zeros((1, 128), F32) for _ in range(4)] for _ in range(2)]
        for cidx in range(s // rch):
            base = cidx * rch
            gate = _conv3_at(padg, w_ref, 0, base, rch) + b_ref[0]
            val = _conv3_at(padv, w_ref, 1, base, rch) + b_ref[1]
            da = da_ref[base:base + rch, :]
            sg = _sig(gate)
            dgate = da * val * sg * (1.0 + gate * (1.0 - sg))
            dval = da * gate * sg
            dpg[base + 8:base + 8 + rch, :] = dgate
            dpv[base + 8:base + 8 + rch, :] = dval
            for half, (dd, pad) in enumerate(((dgate, padg), (dval, padv))):
                for j in range(3):
                    acc[half][j] = acc[half][j] + _rowsum(dd * pad[base + 7 + j:base + 7 + j + rch, :])
                acc[half][3] = acc[half][3] + _rowsum(dd)
        for half in range(2):
            gw_ref[half] = jnp.zeros((8, 128), F32)
            for j in range(4):
                gw_ref[half, j:j + 1, :] = acc[half][j]
        for cidx in range(s // rch):
            base = cidx * rch
            for half, dp in enumerate((dpg, dpv)):
                dup_ref[half, base:base + rch, :] = (
                    w_ref[half, 0:1, :] * dp[base + 9:base + 9 + rch, :]
                    + w_ref[half, 1:2, :] * dp[base + 8:base + 8 + rch, :]
                    + w_ref[half, 2:3, :] * dp[base + 7:base + 7 + rch, :]).astype(_MX)

    return pl.pallas_call(
        body, name="ffn_conv_bwd", grid=(NCH,),
        in_specs=[pl.BlockSpec((2, s, 128), lambda c: (0, 0, c)), pl.BlockSpec((s, 128), lambda c: (0, c)),
                  pl.BlockSpec((2, 8, 128), lambda c: (0, 0, c)), pl.BlockSpec((2, 1, 128), lambda c: (0, 0, c))],
        out_specs=[pl.BlockSpec((2, s, 128), lambda c: (0, 0, c)), pl.BlockSpec((2, 8, 128), lambda c: (0, 0, c))],
        out_shape=[jax.ShapeDtypeStruct((2, s, FH), _MX), jax.ShapeDtypeStruct((2, 8, FH), F32)],
        scratch_shapes=[pltpu.VMEM((s + 16, 128), F32)] * 4,
        compiler_params=_cp("arbitrary"),
    )(up0, dact, fcw, fcb)


def _tn_matmul(a, b, tm, tn, name, b_split=False, by_chip=False, init=None):
    s, m = a.shape
    n = 2 * b.shape[2] if b_split else b.shape[1]
    ts = min(1024, s)
    nsteps = s // ts
    npb = (n // 2) // tn if b_split else None

    def body(*refs):
        if init is None:
            a_ref, b_ref, o_ref, acc = refs
        else:
            a_ref, b_ref, i_ref, o_ref, acc = refs
        k = pl.program_id(2)

        @pl.when(k == 0)
        def _():
            acc[...] = jnp.zeros((tm, tn), F32)
            if init is not None:
                acc[:, 512:768] = i_ref[...]
        acc[...] += _dot_tn(a_ref[...], b_ref[...])

        @pl.when(k == nsteps - 1)
        def _():
            o_ref[...] = acc[...].astype(_MX)

    if b_split:
        bspec = pl.BlockSpec((None, ts, tn), lambda i, j, k: (j // npb, k, j % npb))
    else:
        bspec = pl.BlockSpec((ts, tn), lambda i, j, k: (k, j))
    in_specs = [pl.BlockSpec((ts, tm), lambda i, j, k: (k, i)), bspec]
    args = [a, b]
    if init is not None:
        in_specs.append(pl.BlockSpec((tm, 256), lambda i, j, k: (i, 0)))
        args.append(init)
    if by_chip:
        out_spec = pl.BlockSpec((None, tm, tn), lambda i, j, k: (j, i, 0))
        out_shape = jax.ShapeDtypeStruct((n // tn, m, tn), _MX)
    else:
        out_spec = pl.BlockSpec((tm, tn), lambda i, j, k: (i, j))
        out_shape = jax.ShapeDtypeStruct((m, n), _MX)
    return pl.pallas_call(
        body, name=name, grid=(m // tm, n // tn, nsteps),
        in_specs=in_specs, out_specs=out_spec, out_shape=out_shape,
        scratch_shapes=[pltpu.VMEM((tm, tn), F32)],
        compiler_params=_cp("arbitrary", "arbitrary", "arbitrary"),
    )(*args)


def _ffn_up_bwd(dup, wup4, x1, dy, mix, nw, sc, g1):
    s = x1.shape[0]
    nk = 4

    def body(dup_ref, w_ref, x1_ref, dy_ref, mix_ref, nw_ref, sc_ref, g1_ref, dx1_ref, dmix_ref, vec_ref, acc):
        i, k = pl.program_id(0), pl.program_id(1)

        @pl.when(k == 0)
        def _():
            acc[...] = jnp.zeros((TM, D), F32)

        @pl.when((k == 0) & (i == 0))
        def _():
            vec_ref[...] = jnp.zeros((8, D), F32)
        acc[...] += _dot_nt(dup_ref[...], w_ref[...])

        @pl.when(k == nk - 1)
        def _():
            dh = acc[...]
            xv = x1_ref[...]
            r = lax.rsqrt(jnp.mean(xv * xv, axis=-1, keepdims=True) + EPS)
            xn = xv * r
            nwv, scv = nw_ref[...], sc_ref[...]
            vec_ref[0:1, :] += _rowsum(dh)
            vec_ref[1:2, :] += _rowsum(dh * xn) * nwv
            vec_ref[2:3, :] += _rowsum(dh * xn) * (1.0 + scv)
            dxn = dh * (nwv * (1.0 + scv))
            dx1 = dy_ref[...] + r * (dxn - xn * jnp.mean(dxn * xn, axis=-1, keepdims=True))
            dx1_ref[...] = dx1
            vec_ref[3:4, :] += _rowsum(dx1 * mix_ref[...])
            dmix_ref[...] = (dx1 * g1_ref[...]).astype(_MX)

    tile = pl.BlockSpec((TM, D), lambda i, k: (i, 0))
    row = pl.BlockSpec((1, D), lambda i, k: (0, 0))
    return pl.pallas_call(
        body, name="ffn_up_bwd", grid=(s // TM, nk),
        in_specs=[pl.BlockSpec((None, TM, FQ), lambda i, k: (k // 2, i, k % 2)),
                  pl.BlockSpec((None, D, FQ), lambda i, k: (k, 0, 0)), tile, tile, tile, row, row, row],
        out_specs=[tile, tile, pl.BlockSpec((8, D), lambda i, k: (0, 0))],
        out_shape=[jax.ShapeDtypeStruct((s, D), F32), jax.ShapeDtypeStruct((s, D), _MX),
                   jax.ShapeDtypeStruct((8, D), F32)],
        scratch_shapes=[pltpu.VMEM((TM, D), F32)],
        compiler_params=_cp("arbitrary", "arbitrary"),
    )(dup, wup4, x1, dy, mix, nw, sc, g1)


def _mix_out_bwd(dmix, wout, u1, cnw, cnb):
    s = u1.shape[0]

    def body(dm_ref, w_ref, u1_ref, cnw_ref, cnb_ref, do_ref, du1_ref, vec_ref):
        @pl.when(pl.program_id(0) == 0)
        def _():
            vec_ref[...] = jnp.zeros((8, CW), F32)
        dcat = _dot_nt(dm_ref[...], w_ref[...])
        do_ref[...] = dcat[:, :AW]
        du3 = dcat[:, AW:]
        u2n, rstd = _ln_stats(u1_ref[...])
        u2 = u2n * cnw_ref[...] + cnb_ref[...]
        sg = _sig(u2)
        du2 = du3 * sg * (1.0 + u2 * (1.0 - sg))
        vec_ref[0:1, :] += _rowsum(du2)
        vec_ref[1:2, :] += _rowsum(du2 * u2n)
        d2n = du2 * cnw_ref[...]
        du1_ref[...] = rstd * (d2n - jnp.mean(d2n, axis=-1, keepdims=True)
                               - u2n * jnp.mean(d2n * u2n, axis=-1, keepdims=True))

    row = lambda n: pl.BlockSpec((1, n), lambda i: (0, 0))
    tile = lambda n: pl.BlockSpec((TM, n), lambda i: (i, 0))
    return pl.pallas_call(
        body, name="mix_out_bwd", grid=(s // TM,),
        in_specs=[tile(D), pl.BlockSpec((D, D), lambda i: (0, 0)), tile(CW), row(CW), row(CW)],
        out_specs=[tile(AW), tile(CW), pl.BlockSpec((8, CW), lambda i: (0, 0))],
        out_shape=[jax.ShapeDtypeStruct((s, AW), F32), jax.ShapeDtypeStruct((s, CW), F32),
                   jax.ShapeDtypeStruct((8, CW), F32)],
        compiler_params=_cp("arbitrary"),
    )(dmix, wout, u1, cnw, cnb)


def _conv31_bwd(du1, u0, cw32, proj):
    s = u0.shape[0]
    rch = 256

    def body(d_ref, u_ref, w_ref, ga_ref, gb_ref, dga_ref, dgb_ref, gw_ref, padu, padd):
        for p, src in ((padu, u_ref), (padd, d_ref)):
            p[0:16, :] = jnp.zeros((16, 128), F32)
            p[s + 16:s + 32, :] = jnp.zeros((16, 128), F32)
            p[16:s + 16, :] = src[...]
        for j in range(CK):
            acc = jnp.zeros((1, 128), F32)
            for cidx in range(s // rch):
                base = cidx * rch
                acc = acc + _rowsum(d_ref[base:base + rch, :] * padu[base + j + 1:base + j + 1 + rch, :])
            gw_ref[j:j + 1, :] = acc
        gw_ref[CK:CK + 1, :] = _rowsum(d_ref[...])
        for cidx in range(s // rch):
            base = cidx * rch
            du0 = jnp.zeros((rch, 128), F32)
            for j in range(CK):
                du0 = du0 + w_ref[j:j + 1, :] * padd[base + 31 - j:base + 31 - j + rch, :]
            sg = _sig(gb_ref[base:base + rch, :])
            ga = ga_ref[base:base + rch, :]
            dga_ref[base:base + rch, :] = (du0 * sg).astype(_MX)
            dgb_ref[base:base + rch, :] = (du0 * ga * sg * (1.0 - sg)).astype(_MX)

    blk = lambda off: pl.BlockSpec((s, 128), lambda c: (0, c + off))
    return pl.pallas_call(
        body, name="conv31_bwd", grid=(CW // 128,),
        in_specs=[blk(0), blk(0), pl.BlockSpec((32, 128), lambda c: (0, c)), blk(6), blk(10)],
        out_specs=[blk(0), blk(0), pl.BlockSpec((32, 128), lambda c: (0, c))],
        out_shape=[jax.ShapeDtypeStruct((s, CW), _MX), jax.ShapeDtypeStruct((s, CW), _MX),
                   jax.ShapeDtypeStruct((32, CW), F32)],
        scratch_shapes=[pltpu.VMEM((s + 32, 128), F32)] * 2,
        compiler_params=_cp("arbitrary"),
    )(du1, u0, cw32, proj, proj)


def _attn_bwd(q0, q1, kp, kswp, vp, vswp, kc, kcsw, vc, vcsw, sink, o, do, exchange=()):
    s = q0.shape[0]
    nb = s // TQ
    ne = len(exchange)

    def body(*refs):
        (q0_ref, q1_ref, kp_ref, kswp_ref, vp_ref, vswp_ref, kc_ref, kcsw_ref, vc_ref, vcsw_ref, sink_ref,
         o_ref, do_ref) = refs[:13]
        dq_ref, dk_ref, dv_ref, dkc_ref, dvc_ref, dsink_ref = refs[13 + ne:19 + ne]
        i = pl.program_id(0)
        if ne:
            exch = _Exchange(refs[13:13 + ne], refs[19 + ne:19 + 2 * ne], refs[19 + 2 * ne], refs[20 + 2 * ne])
            pl.when(i == 0)(exch.start)

        @pl.when(i == 0)
        def _():
            dk_ref[...] = jnp.zeros((s + 2 * TQ, 128), F32)
            dv_ref[...] = jnp.zeros((s + 2 * TQ, 128), F32)
            dkc_ref[...] = jnp.zeros((LC, 128), F32)
            dvc_ref[...] = jnp.zeros((LC, 128), F32)
            dsink_ref[...] = jnp.zeros((8, 128), F32)
        st = pl.multiple_of(i * TQ, TQ)
        kall = (jnp.concatenate([kc_ref[...], kp_ref[pl.ds(st, 3 * TQ), :]], axis=0),
                jnp.concatenate([kcsw_ref[...], kswp_ref[pl.ds(st, 3 * TQ), :]], axis=0))
        vall = (jnp.concatenate([vc_ref[...], vp_ref[pl.ds(st, 3 * TQ), :]], axis=0),
                jnp.concatenate([vcsw_ref[...], vswp_ref[pl.ds(st, 3 * TQ), :]], axis=0))
        bias = _attn_bias(i, s)
        lo = lax.broadcasted_iota(jnp.int32, (1, 128), 1) < 64
        qrefs = (q0_ref, q1_ref)
        dq = [jnp.zeros((TQ, 128), F32) for _ in range(4)]
        dkt = jnp.zeros((128, LC + 3 * TQ), F32)
        dvt = jnp.zeros((128, LC + 3 * TQ), F32)
        combos = [(g, e) for g in range(2) for e in range(2)]

        def stacked(ref, g):
            return jnp.concatenate([ref[:, 256 * g:256 * g + 128], ref[:, 256 * g + 128:256 * g + 256]], axis=0)

        for g, e in combos:
            me = lo if e == 0 else jnp.logical_not(lo)
            sw = 0 if e == g else 1
            qm = stacked(qrefs[e], g)
            ex, es, inv = _attn_exp(_dot_nt(qm, kall[sw]), bias, _sink_rows(sink_ref, g, e))
            p = ex * inv
            dom = jnp.where(me, stacked(do_ref, g), 0.0)
            dd = jnp.sum(dom * stacked(o_ref, g), axis=-1, keepdims=True)
            domx = dom.astype(_MX)
            sd = es * inv * dd
            dsink_ref[4 * g + e:4 * g + e + 1, :] -= jnp.sum(sd[:TQ], axis=0, keepdims=True)
            dsink_ref[4 * g + 2 + e:4 * g + 3 + e, :] -= jnp.sum(sd[TQ:], axis=0, keepdims=True)
            ds = (p * (_dot_nt(domx, vall[sw]) - dd)).astype(_MX)
            dq2 = _dot(ds, kall[sw])
            dq[2 * g] = dq[2 * g] + jnp.where(me, dq2[:TQ], 0.0)
            dq[2 * g + 1] = dq[2 * g + 1] + jnp.where(me, dq2[TQ:], 0.0)
            dk2 = _dot_tn(qm, ds)
            dv2 = _dot_tn(domx, p.astype(_MX))
            if sw:
                dk2 = pltpu.roll(dk2, 64, 0)
                dv2 = pltpu.roll(dv2, 64, 0)
            dkt = dkt + dk2
            dvt = dvt + dv2
        dkt = dkt.T
        dvt = dvt.T
        for j in range(4):
            dq_ref[:, 128 * j:128 * (j + 1)] = dq[j] * 0.125
        dkc_ref[...] += dkt[:LC]
        dvc_ref[...] += dvt[:LC]
        dk_ref[pl.ds(st, 3 * TQ), :] += dkt[LC:]
        dv_ref[pl.ds(st, 3 * TQ), :] += dvt[LC:]
        if ne:
            pl.when(i == nb - 1)(exch.finish)

    full = lambda a: pl.BlockSpec(a.shape, lambda i: (0, 0))
    fs = lambda r: pl.BlockSpec((r, 128), lambda i: (0, 0))
    qs = pl.BlockSpec((TQ, AW), lambda i: (i, 0))
    res = pl.pallas_call(
        body, name="attn_bwd", grid=(nb,),
        in_specs=[qs, qs, full(kp), full(kp), full(kp), full(kp), full(kc), full(kc), full(kc), full(kc),
                  pl.BlockSpec(memory_space=pltpu.SMEM), qs, qs] + _any_specs(ne),
        out_specs=[qs, fs(s + 2 * TQ), fs(s + 2 * TQ), fs(LC), fs(LC), fs(8)] + _any_specs(ne),
        out_shape=[jax.ShapeDtypeStruct((s, AW), F32), jax.ShapeDtypeStruct((s + 2 * TQ, 128), F32),
                   jax.ShapeDtypeStruct((s + 2 * TQ, 128), F32), jax.ShapeDtypeStruct((LC, 128), F32),
                   jax.ShapeDtypeStruct((LC, 128), F32), jax.ShapeDtypeStruct((8, 128), F32)]
        + _Exchange.out_shapes(exchange),
        scratch_shapes=_Exchange.semaphores(ne) if ne else [],
        compiler_params=_cp("arbitrary"),
    )(q0, q1, kp, kswp, vp, vswp, kc, kcsw, vc, vcsw, sink, o, do, *exchange)
    return res[:6], res[6:]


def _mix_in_bwd(dq, dk, dv, dga, dgb, proj, x, dx1, win, nw, sc, qnw2, knw2, cos, sa, sb, gmat):
    s = x.shape[0]

    def body(dq_ref, dk_ref, dv_ref, dga_ref, dgb_ref, proj_ref, x_ref, dx1_ref, win_ref, nw_ref, sc_ref,
             qnw_ref, knw_ref, cos_ref, sa_ref, sb_ref, g_ref, gx_ref, dproj_ref, vec_ref):
        @pl.when(pl.program_id(0) == 0)
        def _():
            vec_ref[...] = jnp.zeros((8, D), F32)
        cs, sav, sbv, g = cos_ref[...], sa_ref[...], sb_ref[...], g_ref[...]
        gq = jnp.zeros((1, 128), F32)
        for j in range(4):
            dqn = _rope_bwd(dq_ref[:, 128 * j:128 * (j + 1)], cs, sav, sbv)
            dch, gw = _headnorm_bwd(proj_ref[:, 128 * j:128 * (j + 1)], dqn, qnw_ref[...], g)
            dproj_ref[:, 128 * j:128 * (j + 1)] = dch.astype(_MX)
            gq = gq + _rowsum(gw)
        dkn = _rope_bwd(dk_ref[...], cs, sav, sbv)
        dch, gw = _headnorm_bwd(proj_ref[:, 512:640], dkn, knw_ref[...], g)
        dproj_ref[:, 512:640] = dch.astype(_MX)
        dproj_ref[:, 640:768] = dv_ref[...].astype(_MX)
        dproj_ref[:, 768:1280] = dga_ref[...]
        dproj_ref[:, 1280:1792] = dgb_ref[...]
        vec_ref[3:4, 0:128] += gq
        vec_ref[3:4, 128:256] += _rowsum(gw)
        dh = _dot_nt(dproj_ref[...], win_ref[...])
        xv = x_ref[...]
        r = lax.rsqrt(jnp.mean(xv * xv, axis=-1, keepdims=True) + EPS)
        xn = xv * r
        nwv, scv = nw_ref[...], sc_ref[...]
        vec_ref[0:1, :] += _rowsum(dh)
        vec_ref[1:2, :] += _rowsum(dh * xn) * nwv
        vec_ref[2:3, :] += _rowsum(dh * xn) * (1.0 + scv)
        dxn = dh * (nwv * (1.0 + scv))
        gx_ref[...] = dx1_ref[...] + r * (dxn - xn * jnp.mean(dxn * xn, axis=-1, keepdims=True))

    row = lambda n: pl.BlockSpec((1, n), lambda i: (0, 0))
    tile = lambda n: pl.BlockSpec((TM, n), lambda i: (i, 0))
    return pl.pallas_call(
        body, name="mix_in_bwd", grid=(s // TM,),
        in_specs=[tile(AW), tile(128), tile(128), tile(CW), tile(CW), tile(INW), tile(D), tile(D),
                  pl.BlockSpec((D, INW), lambda i: (0, 0)), row(D), row(D), row(128), row(128),
                  tile(128), tile(128), tile(128), pl.BlockSpec((128, 128), lambda i: (0, 0))],
        out_specs=[tile(D), tile(INW), pl.BlockSpec((8, D), lambda i: (0, 0))],
        out_shape=[jax.ShapeDtypeStruct((s, D), F32), jax.ShapeDtypeStruct((s, INW), _MX),
                   jax.ShapeDtypeStruct((8, D), F32)],
        compiler_params=_cp("arbitrary"),
    )(dq, dk, dv, dga, dgb, proj, x, dx1, win, nw, sc, qnw2, knw2, cos, sa, sb, gmat)


def _ctx_bwd(ctx, nw, scc, winkv, kvc, hc, dkc, dvc, knw2, gmat):
    def body(ctx_ref, nw_ref, sc_ref, w_ref, kvc_ref, hc_ref, dkc_ref, dvc_ref, knw_ref, g_ref, gw_ref, vec_ref):
        dkr, gk = _headnorm_bwd(kvc_ref[:, 0:128], dkc_ref[...], knw_ref[...], g_ref[...])
        dkv = jnp.concatenate([dkr, dvc_ref[...]], axis=1).astype(_MX)
        gw_ref[...] = _dot_tn(hc_ref[...], dkv)
        dh = _dot_nt(dkv, w_ref[...])
        cv = ctx_ref[...]
        r = lax.rsqrt(jnp.mean(cv * cv, axis=-1, keepdims=True) + EPS)
        cn = cv * r
        vec_ref[...] = jnp.zeros((8, D), F32)
        vec_ref[0:1, :] = _rowsum(dh)
        vec_ref[1:2, :] = _rowsum(dh * cn) * nw_ref[...]
        vec_ref[2:3, :] = _rowsum(dh * cn) * (1.0 + sc_ref[...])
        vec_ref[3:4, 0:128] = _rowsum(gk)

    return pl.pallas_call(
        body, name="ctx_bwd", in_specs=[_vspec()] * 10, out_specs=[_vspec()] * 2,
        out_shape=[jax.ShapeDtypeStruct((D, 256), F32), jax.ShapeDtypeStruct((8, D), F32)],
        compiler_params=pltpu.CompilerParams(vmem_limit_bytes=VMEM_LIMIT),
    )(ctx, nw, scc, winkv, kvc, hc, dkc, dvc, knw2, gmat)


def _mod_fwd(cs, wmod, bloc):
    def body(c_ref, w_ref, b_ref, o_ref):
        cv = c_ref[...]
        o_ref[...] = _dot((cv * _sig(cv)).astype(_MX), w_ref[...].astype(_MX)) + b_ref[...]

    return pl.pallas_call(
        body, name="mod_fwd", in_specs=[_vspec()] * 3, out_specs=_vspec(),
        out_shape=jax.ShapeDtypeStruct((16, wmod.shape[1]), F32),
        compiler_params=pltpu.CompilerParams(vmem_limit_bytes=VMEM_LIMIT),
    )(cs, wmod, bloc)


def _mod_bwd(cs, dmod, wmod):
    def body(c_ref, d_ref, w_ref, gw_ref, part_ref):
        cv = c_ref[...]
        sl = (cv * _sig(cv)).astype(_MX)
        dm = d_ref[...].astype(_MX)
        gw_ref[...] = _dot_tn(sl, dm)
        part_ref[...] = _dot_nt(dm, w_ref[...].astype(_MX))

    return pl.pallas_call(
        body, name="mod_bwd", in_specs=[_vspec()] * 3, out_specs=[_vspec()] * 2,
        out_shape=[jax.ShapeDtypeStruct(wmod.shape, F32), jax.ShapeDtypeStruct((16, D), F32)],
        compiler_params=pltpu.CompilerParams(vmem_limit_bytes=VMEM_LIMIT),
    )(cs, dmod, wmod)


def _sum_leading(a, name):
    n = a.shape[0]

    def body(a_ref, o_ref):
        acc = a_ref[0]
        for k in range(1, n):
            acc = acc + a_ref[k]
        o_ref[...] = acc

    return pl.pallas_call(
        body, name=name, in_specs=[_vspec()], out_specs=_vspec(),
        out_shape=jax.ShapeDtypeStruct(a.shape[1:], F32),
        compiler_params=pltpu.CompilerParams(vmem_limit_bytes=VMEM_LIMIT),
    )(a)


def _cctx_grad(parts, cc):
    def body(p_ref, c_ref, o_ref):
        acc = p_ref[0, 8:9, :]
        for k in range(1, 4):
            acc = acc + p_ref[k, 8:9, :]
        cv = c_ref[...]
        sg = _sig(cv)
        o_ref[...] = acc * (sg * (1.0 + cv * (1.0 - sg)))

    return pl.pallas_call(
        body, name="cctx_grad", in_specs=[_vspec()] * 2, out_specs=_vspec(),
        out_shape=jax.ShapeDtypeStruct((1, D), F32),
    )(parts, cc)


def _adam_math(w, g, m, v):
    mn = ADAM_B1 * m + (1.0 - ADAM_B1) * g
    vn = ADAM_B2 * v + (1.0 - ADAM_B2) * (g * g)
    mh = mn / (1.0 - ADAM_B1 ** ADAM_STEP)
    vh = vn / (1.0 - ADAM_B2 ** ADAM_STEP)
    delta = -ADAM_LR * (mh / (jnp.sqrt(vh) + ADAM_EPS) + ADAM_WD * w)
    return delta, mn, vn


def _adam_big(w, g, m, v, name):
    r, n = w.shape
    tr = 256 if r % 256 == 0 else 64

    def body(w_ref, g_ref, m_ref, v_ref, d_ref, mo_ref, vo_ref):
        d, mn, vn = _adam_math(w_ref[...], g_ref[...], m_ref[...], v_ref[...])
        d_ref[...] = d
        mo_ref[...] = mn
        vo_ref[...] = vn

    spec = pl.BlockSpec((tr, n), lambda i: (i, 0))
    return pl.pallas_call(
        body, name=name, grid=(r // tr,), in_specs=[spec] * 4, out_specs=[spec] * 3,
        out_shape=[jax.ShapeDtypeStruct((r, n), F32)] * 3,
        compiler_params=_cp("arbitrary"),
    )(w, g, m, v)


def _adam_small(ws, gs, ms, vs):
    n = len(ws)

    def body(*refs):
        ins, outs = refs[:4 * n], refs[4 * n:]
        for k in range(n):
            d, mn, vn = _adam_math(ins[k][...], ins[n + k][...], ins[2 * n + k][...], ins[3 * n + k][...])
            outs[k][...] = d
            outs[n + k][...] = mn
            outs[2 * n + k][...] = vn

    shapes = [jax.ShapeDtypeStruct(w.shape, F32) for w in ws]
    res = pl.pallas_call(
        body, name="adam_small", in_specs=[_vspec()] * (4 * n), out_specs=[_vspec()] * (3 * n),
        out_shape=shapes * 3,
    )(*ws, *gs, *ms, *vs)
    return res[:n], res[n:2 * n], res[2 * n:]


def _pair_sum(grads, from_sib, core, name):
    n = len(grads)

    def body(c_ref, *refs):
        for w in range(n):
            a_ref, b_ref, o_ref = refs[w], refs[n + w], refs[2 * n + w]
            o_ref[...] = (a_ref[...].astype(F32) + b_ref[...].astype(F32)).astype(_MX)

    halves = [(None, g.shape[1] // 2, g.shape[2]) for g in grads]
    return pl.pallas_call(
        body, name=name,
        grid_spec=pltpu.PrefetchScalarGridSpec(
            num_scalar_prefetch=1, grid=(4,),
            in_specs=[pl.BlockSpec(h, lambda j, c: (j, c[0], 0)) for h in halves]
            + [pl.BlockSpec(h, lambda j, c: (j, 0, 0)) for h in halves],
            out_specs=[pl.BlockSpec(h, lambda j, c: (j, 0, 0)) for h in halves]),
        out_shape=[jax.ShapeDtypeStruct(p.shape, _MX) for p in from_sib],
        compiler_params=_cp("arbitrary"),
    )(core.reshape(1), *grads, *from_sib)


def _chip_sum(parts, arrived, chip, core):
    n = len(parts)

    def body(s_ref, *refs):
        for w in range(n):
            own_ref, p_ref, o_ref = refs[w], refs[n + w], refs[2 * n + w]
            acc = own_ref[...].astype(F32)
            for k in range(3):
                acc = acc + p_ref[k].astype(F32)
            o_ref[...] = acc

    blk = [(p.shape[1] // 2, p.shape[2]) for p in parts]
    return pl.pallas_call(
        body, name="grad_chip_sum",
        grid_spec=pltpu.PrefetchScalarGridSpec(
            num_scalar_prefetch=1, grid=(2,),
            in_specs=[pl.BlockSpec((None,) + b, lambda i, s: (s[0], i, 0)) for b in blk]
            + [pl.BlockSpec((3,) + b, lambda i, s: (0, i, 0)) for b in blk],
            out_specs=[pl.BlockSpec(b, lambda i, s: (2 * s[1] + i, 0)) for b in blk]),
        out_shape=[jax.ShapeDtypeStruct((2 * p.shape[1], p.shape[2]), F32) for p in parts],
        compiler_params=_cp("arbitrary"),
    )(jnp.stack([chip, core]), *parts, *arrived)


def _cast_shards(ws, chip):
    n = len(ws)

    def body(s_ref, *refs):
        for w in range(n):
            refs[n + w][...] = refs[w][...].astype(_MX)

    blk = [(a.shape[0] // 2, a.shape[1]) for a in ws]
    return pl.pallas_call(
        body, name="cast_shards",
        grid_spec=pltpu.PrefetchScalarGridSpec(
            num_scalar_prefetch=1, grid=(2,),
            in_specs=[pl.BlockSpec(b, lambda i, s: (i, 0)) for b in blk],
            out_specs=[pl.BlockSpec((None,) + b, lambda i, s: (s[0], i, 0)) for b in blk]),
        out_shape=[jax.ShapeDtypeStruct((4,) + a.shape, _MX) for a in ws],
        compiler_params=_cp("arbitrary"),
    )(chip.reshape(1), *ws)


def _position():
    x, y, c = lax.axis_index("x"), lax.axis_index("y"), lax.axis_index("c")
    return x, y, c


def _small_allgather(blk, name, exchange=(), share=()):
    m_per, n = blk.shape
    ne, ns = len(exchange), len(share)

    def body(*refs):
        x_ref = refs[0]
        out_ref = refs[1 + ne + ns]
        send_sems, recv_sems = refs[2 + 2 * ne + 2 * ns], refs[3 + 2 * ne + 2 * ns]
        extra = refs[4 + 2 * ne + 2 * ns:]
        x, y, c = _position()
        me, sibling = (x, y, c), (x, y, 1 - c)
        chips = [(1 - x, y), (x, 1 - y), (1 - x, 1 - y)]
        if ne:
            exch = _Exchange(refs[1:1 + ne], refs[2 + ne + ns:2 + 2 * ne + ns], extra[0], extra[1])
            exch.start()
        if ns:
            halves = _share_copies(refs[1 + ne:1 + ne + ns], refs[2 + 2 * ne + ns:2 + 2 * ne + 2 * ns],
                                   extra[2 * bool(ne)], extra[2 * bool(ne) + 1], [b.shape[0] for b in share])
            for cp in halves:
                cp.start()

        def rows(px, py, pc):
            return out_ref.at[pl.ds(pl.multiple_of((4 * px + 2 * py + pc) * m_per, 8), m_per), :]

        def copy(k, block, to, src=None):
            return pltpu.make_async_remote_copy(
                src_ref=rows(*block) if src is None else src, dst_ref=rows(*block),
                send_sem=send_sems.at[k], recv_sem=recv_sems.at[k], device_id=to, device_id_type=MESH)

        out_ref[pl.ds(pl.multiple_of((4 * x + 2 * y + c) * m_per, 8), m_per), :] = x_ref[...]
        first = [copy(0, me, sibling, src=x_ref)]
        first += [copy(1 + j, me, (*chip, c), src=x_ref) for j, chip in enumerate(chips)]
        for cp in first:
            cp.start()
        passed = [copy(4 + j, (*chip, c), sibling) for j, chip in enumerate(chips)]
        for j, chip in enumerate(chips):
            copy(1 + j, (*chip, c), me).wait_recv()
            passed[j].start()
        copy(0, sibling, me).wait_recv()
        for j, chip in enumerate(chips):
            copy(4 + j, (*chip, 1 - c), me).wait_recv()
        for cp in first + passed:
            cp.wait_send()
        if ne:
            exch.finish()
        if ns:
            for cp in halves:
                cp.wait()

    res = pl.pallas_call(
        body, name=name,
        out_shape=[jax.ShapeDtypeStruct((8 * m_per, n), blk.dtype)] + _Exchange.out_shapes(exchange)
        + [jax.ShapeDtypeStruct(b.shape, b.dtype) for b in share],
        in_specs=[_vspec()] + _any_specs(ne + ns), out_specs=[_vspec()] + _any_specs(ne + ns),
        input_output_aliases={1 + ne + w: 1 + ne + w for w in range(ns)},
        scratch_shapes=[pltpu.SemaphoreType.DMA((7,)), pltpu.SemaphoreType.DMA((7,))]
        + (_Exchange.semaphores(ne) if ne else [])
        + ([pltpu.SemaphoreType.DMA((ns,)), pltpu.SemaphoreType.DMA((ns,))] if ns else []),
    )(blk, *exchange, *share)
    return res[0], res[1:1 + ne], res[1 + ne:]


def _share_copies(src_refs, out_refs, send_sems, recv_sems, nrows):
    x, y, c = _position()
    return [pltpu.make_async_remote_copy(
        src_ref=_rows(src_refs[w], c, r), dst_ref=_rows(out_refs[w], c, r), send_sem=send_sems.at[w],
        recv_sem=recv_sems.at[w], device_id=(x, y, 1 - c), device_id_type=MESH) for w, r in enumerate(nrows)]


def _any_specs(n):
    return [pl.BlockSpec(memory_space=pl.ANY)] * n


def _rows(ref, half, nrows):
    return ref.at[pl.ds(half * (nrows // 2), nrows // 2), :]


def _weight_allgather(bufs):
    n = len(bufs)

    def body(*refs):
        gather = _Gather(refs[:n], refs[n:2 * n], refs[2 * n], refs[2 * n + 1], [b.shape[1] for b in bufs])
        gather.start()
        gather.forward()
        gather.finish()

    return pl.pallas_call(
        body, name="weight_allgather",
        out_shape=[jax.ShapeDtypeStruct(a.shape, a.dtype) for a in bufs],
        in_specs=_any_specs(n), out_specs=_any_specs(n), input_output_aliases={w: w for w in range(n)},
        scratch_shapes=_Gather.semaphores(n),
    )(*bufs)


class _Gather:
    def __init__(self, src_refs, out_refs, send_sems, recv_sems, nrows):
        x, y, c = _position()
        chip = 2 * x + y
        sibling = (x, y, 1 - c)

        def copy(k, src, dst, to):
            return pltpu.make_async_remote_copy(src_ref=src, dst_ref=dst, send_sem=send_sems.at[k],
                                                recv_sem=recv_sems.at[k], device_id=to, device_id_type=MESH)

        self.first, self.first_landed, self.passed, self.passed_landed = [], [], [], []
        for w, r in enumerate(nrows):
            for j, ch in enumerate([(1 - x, y), (x, 1 - y), (1 - x, 1 - y)]):
                theirs = _rows(out_refs[w].at[2 * ch[0] + ch[1]], c, r)
                other = _rows(out_refs[w].at[2 * ch[0] + ch[1]], 1 - c, r)
                self.first.append(copy(6 * w + j, _rows(src_refs[w].at[chip], c, r),
                                       _rows(out_refs[w].at[chip], c, r), (*ch, c)))
                self.first_landed.append(copy(6 * w + j, theirs, theirs, sibling))
                self.passed.append(copy(6 * w + 3 + j, theirs, theirs, sibling))
                self.passed_landed.append(copy(6 * w + 3 + j, other, other, sibling))

    @staticmethod
    def semaphores(n):
        return [pltpu.SemaphoreType.DMA((6 * n,)), pltpu.SemaphoreType.DMA((6 * n,))]

    def start(self):
        for cp in self.first:
            cp.start()

    def forward(self):
        for landed, cp in zip(self.first_landed, self.passed):
            landed.wait_recv()
            cp.start()

    def finish(self):
        for landed in self.passed_landed:
            landed.wait_recv()
        for cp in self.first + self.passed:
            cp.wait_send()


def _grad_swap(grads, name):
    n = len(grads)

    def body(*refs):
        g_refs, o_refs = refs[:n], refs[n:2 * n]
        send_sems, recv_sems = refs[2 * n:]
        x, y, c = _position()
        cps = []
        for w in range(n):
            h = grads[w].shape[1] // 2
            cps.append(pltpu.make_async_remote_copy(
                src_ref=g_refs[w].at[:, pl.ds((1 - c) * h, h), :], dst_ref=o_refs[w], send_sem=send_sems.at[w],
                recv_sem=recv_sems.at[w], device_id=(x, y, 1 - c), device_id_type=MESH))
        for cp in cps:
            cp.start()
        for cp in cps:
            cp.wait()

    return pl.pallas_call(
        body, name=name,
        out_shape=[jax.ShapeDtypeStruct((4, g.shape[1] // 2, g.shape[2]), g.dtype) for g in grads],
        in_specs=_any_specs(n), out_specs=_any_specs(n),
        scratch_shapes=[pltpu.SemaphoreType.DMA((n,)), pltpu.SemaphoreType.DMA((n,))],
    )(*grads)


class _Exchange:
    def __init__(self, part_refs, out_refs, send_sems, recv_sems):
        x, y, c = _position()
        self.copies = [
            pltpu.make_async_remote_copy(
                src_ref=part_refs[w].at[2 * ch[0] + ch[1]], dst_ref=out_refs[w].at[k], send_sem=send_sems.at[3 * w + k],
                recv_sem=recv_sems.at[3 * w + k], device_id=(*ch, c), device_id_type=MESH)
            for w in range(len(part_refs)) for k, ch in enumerate([(1 - x, y), (x, 1 - y), (1 - x, 1 - y)])]

    @staticmethod
    def semaphores(n):
        return [pltpu.SemaphoreType.DMA((3 * n,)), pltpu.SemaphoreType.DMA((3 * n,))]

    @staticmethod
    def out_shapes(parts):
        return [jax.ShapeDtypeStruct((3,) + p.shape[1:], p.dtype) for p in parts]

    def start(self):
        for cp in self.copies:
            cp.start()

    def finish(self):
        for cp in self.copies:
            cp.wait()


def _rope_tables(s):
    t = jnp.arange(s)
    inv = 10000.0 ** (-jnp.arange(0, 32, 2, dtype=F32) / 32.0)
    ang_r = (t // GRID_W).astype(F32)[:, None] * inv
    ang_c = (t % GRID_W).astype(F32)[:, None] * inv
    ang = jnp.concatenate([ang_r, ang_r, ang_c, ang_c] * 2, axis=-1)
    first = (jnp.arange(128) % 32) < 16
    sin = jnp.sin(ang)
    return jnp.cos(ang), jnp.where(first, -sin, 0.0), jnp.where(first, 0.0, sin)


def _local_step(x, ctx, tgt, mod, modc, nw1, win, qnw, knw, sink, cw, cb, cnw, cnb, wout4, nw2, wup, fcw, fcb, wdown4,
                core=None):
    s = x.shape[0]
    sh1, sc1, g1, sh2, sc2, g2 = [mod[:, D * k:D * (k + 1)] for k in range(6)]
    shc, scc = modc[:, :D], modc[:, D:2 * D]
    cos, sa, sb = _rope_tables(s)
    gi = jnp.arange(128) // 64
    gmat = (gi[:, None] == gi[None, :]).astype(_MX)
    qnw2, knw2 = jnp.tile(qnw, (1, 2)), jnp.tile(knw, (1, 2))
    cw32 = jnp.pad(cw, ((0, 1), (0, 0)))
    fcw2 = jnp.pad(fcw, ((0, 5), (0, 0))).reshape(8, 2, FH).transpose(1, 0, 2)
    fcb2 = fcb.reshape(2, 1, FH)
    winkv = win[:, 512:768]
    sinkv = sink.reshape(8)

    proj, h, q0, q1, k, ksw, v, vsw, u0 = _mix_in(x, nw1, sc1, sh1, win, qnw2, knw2, cos, sa, sb, gmat)
    kvc, hc, kc, kcsw, vc, vcsw = _ctx_kv(ctx, nw1, scc, shc, winkv, knw2, gmat)
    padr = lambda a: jnp.pad(a, ((TQ, TQ), (0, 0)))
    kp, kswp, vp, vswp = padr(k), padr(ksw), padr(v), padr(vsw)
    o, gathered = _attn_fwd(q0, q1, kp, kswp, vp, vswp, kc, kcsw, vc, vcsw, sinkv,
                            gather=() if core is None else (wout4, wup, wdown4))
    if core is not None:
        wout4, wup, wdown4 = gathered
    wout = wout4.reshape(D, D)
    wdown = wdown4.reshape(FH, D)
    u1 = _conv31(u0, cw32, cb)
    x1, mix, cat = _mix_out(o, u1, cnw, cnb, wout, x, g1)
    h2, up0 = _ffn_up(x1, nw2, sc2, sh2, wup)
    act = _ffn_conv_act(up0, fcw2, fcb2)
    dy, dact, ddn, vec_dn = _ffn_down(act, wdown, x1, g2, tgt)
    loss = (0.5 / D) * jnp.sum(vec_dn[1])

    dup, gfc = _ffn_conv_bwd(up0, dact, fcw2, fcb2)
    g_wdown = _tn_matmul(act, ddn, FQ, D, "gw_down").reshape(4, FH // 4, D)
    g_wup = _tn_matmul(h2, dup, D, FQ, "gw_up", b_split=True, by_chip=True)
    parts = ()
    if core is not None:
        late = [g_wup, g_wdown]
        parts = _pair_sum(late, _grad_swap(late, "grad_swap_up_down"), core, "grad_pair_sum_up_down")
    dx1, dmix, vec_up = _ffn_up_bwd(dup, wup, x1, dy, mix, nw2, sc2, g1)
    g_wout = _tn_matmul(cat, dmix, D, D, "gw_out")
    do, du1, vec_ln = _mix_out_bwd(dmix, wout, u1, cnw, cnb)
    dga, dgb, gcw = _conv31_bwd(du1, u0, cw32, proj)
    (dq, dkp, dvp, dkc, dvc, dsink), arrived = _attn_bwd(q0, q1, kp, kswp, vp, vswp, kc, kcsw, vc, vcsw, sinkv, o, do,
                                                         exchange=parts)
    if core is not None:
        g_wup, g_wdown = (parts[0], arrived[0]), (parts[1], arrived[1])
    gwinkv, vec_ctx = _ctx_bwd(ctx, nw1, scc, winkv, kvc, hc, dkc, dvc, knw2, gmat)
    gx, dproj, vec_in = _mix_in_bwd(dq, dkp[TQ:TQ + s], dvp[TQ:TQ + s], dga, dgb, proj, x, dx1, win, nw1, sc1,
                                    qnw2, knw2, cos, sa, sb, gmat)
    g_win = _tn_matmul(h, dproj, D, INW, "gw_in", init=gwinkv)

    g_qn = vec_in[3:4, 0:128]
    g_kn = vec_in[3:4, 128:256] + vec_ctx[3:4, 0:128]
    grads = dict(
        norm_mix_w=vec_in[2:3] + vec_ctx[2:3], w_in=g_win,
        q_norm_w=g_qn[:, :64] + g_qn[:, 64:], k_norm_w=g_kn[:, :64] + g_kn[:, 64:],
        sink_logit=dsink[:, 0].reshape(1, 8), conv_w=gcw[:CK], conv_b=gcw[CK:CK + 1],
        conv_norm_w=vec_ln[1:2], conv_norm_b=vec_ln[0:1], w_out=g_wout, norm_ffn_w=vec_up[2:3], w_up=g_wup,
        ffn_conv_w=gfc[:, 0:3].transpose(1, 0, 2).reshape(3, 2 * FH), ffn_conv_b=gfc[:, 3].reshape(1, 2 * FH),
        w_down=g_wdown)
    dmod = jnp.concatenate([vec_in[0:1], vec_in[1:2], vec_up[3:4], vec_up[0:1], vec_up[1:2], vec_dn[0:1]], axis=1)
    dmodc = jnp.concatenate([vec_ctx[0:1], vec_ctx[1:2]], axis=1)
    return loss, gx, grads, dmod, dmodc


_SMALL = ["c_ctx", "b_mod", "norm_mix_w", "q_norm_w", "k_norm_w", "sink_logit", "conv_w", "conv_b", "conv_norm_w",
          "conv_norm_b", "norm_ffn_w", "ffn_conv_w", "ffn_conv_b"]
_GATHERED = ["w_in", "w_out", "w_up", "w_down"]
_BIG = ["w_mod"] + _GATHERED
_ORDER = ["c_ctx", "w_mod", "b_mod", "norm_mix_w", "w_in", "q_norm_w", "k_norm_w", "sink_logit", "conv_w", "conv_b",
          "conv_norm_w", "conv_norm_b", "w_out", "norm_ffn_w", "w_up", "ffn_conv_w", "ffn_conv_b", "w_down"]
_PACK = [("norm_mix_w", 1), ("norm_ffn_w", 1), ("q_norm_w", 1), ("k_norm_w", 1), ("sink_logit", 1), ("conv_b", 1),
         ("conv_norm_w", 1), ("conv_norm_b", 1), ("ffn_conv_b", 6), ("conv_w", 16), ("ffn_conv_w", 17)]
_PACK_ROWS = 56


def _pack_rows(a, nrows):
    flat = a.reshape(-1)
    return jnp.pad(flat, (0, nrows * D - flat.shape[0])).reshape(nrows, D)


def kernel(x, c, ctx, c_ctx, w_mod, b_mod, norm_mix_w, w_in, q_norm_w, k_norm_w, sink_logit, conv_w, conv_b, conv_norm_w, conv_norm_b, w_out, norm_ffn_w, w_up, ffn_conv_w, ffn_conv_b, w_down, loss_target, m_c_ctx, m_w_mod, m_b_mod, m_norm_mix_w, m_w_in, m_q_norm_w, m_k_norm_w, m_sink_logit, m_conv_w, m_conv_b, m_conv_norm_w, m_conv_norm_b, m_w_out, m_norm_ffn_w, m_w_up, m_ffn_conv_w, m_ffn_conv_b, m_w_down, v_c_ctx, v_w_mod, v_b_mod, v_norm_mix_w, v_w_in, v_q_norm_w, v_k_norm_w, v_sink_logit, v_conv_w, v_conv_b, v_conv_norm_w, v_conv_norm_b, v_w_out, v_norm_ffn_w, v_w_up, v_ffn_conv_w, v_ffn_conv_b, v_w_down):
    w = dict(c_ctx=c_ctx.reshape(1, D), w_mod=w_mod[0], b_mod=b_mod, norm_mix_w=norm_mix_w, w_in=w_in[0],
             q_norm_w=q_norm_w, k_norm_w=k_norm_w, sink_logit=sink_logit, conv_w=conv_w[0], conv_b=conv_b,
             conv_norm_w=conv_norm_w, conv_norm_b=conv_norm_b, w_out=w_out[0], norm_ffn_w=norm_ffn_w, w_up=w_up[0],
             ffn_conv_w=ffn_conv_w[0], ffn_conv_b=ffn_conv_b, w_down=w_down[0])
    m = dict(c_ctx=m_c_ctx.reshape(1, D), w_mod=m_w_mod[0], b_mod=m_b_mod, norm_mix_w=m_norm_mix_w, w_in=m_w_in[0],
             q_norm_w=m_q_norm_w, k_norm_w=m_k_norm_w, sink_logit=m_sink_logit, conv_w=m_conv_w[0], conv_b=m_conv_b,
             conv_norm_w=m_conv_norm_w, conv_norm_b=m_conv_norm_b, w_out=m_w_out[0], norm_ffn_w=m_norm_ffn_w,
             w_up=m_w_up[0], ffn_conv_w=m_ffn_conv_w[0], ffn_conv_b=m_ffn_conv_b, w_down=m_w_down[0])
    v = dict(c_ctx=v_c_ctx.reshape(1, D), w_mod=v_w_mod[0], b_mod=v_b_mod, norm_mix_w=v_norm_mix_w, w_in=v_w_in[0],
             q_norm_w=v_q_norm_w, k_norm_w=v_k_norm_w, sink_logit=v_sink_logit, conv_w=v_conv_w[0], conv_b=v_conv_b,
             conv_norm_w=v_conv_norm_w, conv_norm_b=v_conv_norm_b, w_out=v_w_out[0], norm_ffn_w=v_norm_ffn_w,
             w_up=v_w_up[0], ffn_conv_w=v_ffn_conv_w[0], ffn_conv_b=v_ffn_conv_b, w_down=v_w_down[0])
    xi, yi, ci = _position()
    chip = 2 * xi + yi
    dev = 2 * chip + ci
    s = x.shape[1]
    ncol = w["w_mod"].shape[1]

    blk0 = jnp.concatenate([_pack_rows(w["conv_w"], 4), _pack_rows(w["ffn_conv_w"], 5), c,
                            jnp.zeros((6, D), F32)], axis=0)
    g0 = _small_allgather(blk0, "gather_c_convw")[0].reshape(8, 16, D)
    c_all = g0[:, 9, :]
    cs = jnp.concatenate([c_all, w["c_ctx"], jnp.zeros((7, D), F32)], axis=0)
    cw_full = jnp.concatenate([g0[2 * j, 0:4].reshape(-1)[:CK * 128].reshape(CK, 128) for j in range(4)], axis=1)
    fcw_full = jnp.concatenate([g0[2 * j, 4:9].reshape(-1)[:3 * 1408].reshape(3, 1408) for j in range(4)], axis=1)

    b_loc = lax.dynamic_slice(w["b_mod"], (0, chip * ncol), (1, ncol))
    modp = _mod_fwd(cs, w["w_mod"], b_loc)
    gm = _small_allgather(modp, "gather_mod")[0].reshape(8, 16, ncol)
    mod_all = jnp.concatenate([gm[2 * j] for j in range(4)], axis=1)
    mod = lax.dynamic_slice(mod_all, (dev, 0), (1, 6 * D))
    modc = mod_all[8:9]

    win_buf, wout_buf, wup_buf, wdown_buf = _cast_shards([w[n] for n in _GATHERED], chip)
    win = _weight_allgather([win_buf])[0].transpose(1, 0, 2).reshape(D, INW)

    loss_loc, gx, gl, dmod, dmodc = _local_step(
        x[0], ctx[0], loss_target[0], mod, modc, w["norm_mix_w"], win, w["q_norm_w"], w["k_norm_w"], w["sink_logit"],
        cw_full, w["conv_b"], w["conv_norm_w"], w["conv_norm_b"], wout_buf, w["norm_ffn_w"], wup_buf, fcw_full,
        w["ffn_conv_b"], wdown_buf, core=ci)
    loss = lax.psum(loss_loc, ("x", "y", "c"))

    early = [gl["w_in"].reshape(D, 4, INW // 4).transpose(1, 0, 2), gl["w_out"].reshape(4, D // 4, D)]
    parts = _pair_sum(early, _grad_swap(early, "grad_swap_in_out"), ci, "grad_pair_sum_in_out")

    pack = jnp.concatenate([dmod.reshape(6, D), dmodc.reshape(2, D)]
                           + [_pack_rows(gl[name], nr) for name, nr in _PACK], axis=0)
    pack = jnp.pad(pack, ((0, _PACK_ROWS - pack.shape[0]), (0, 0)))
    gp, arrived, _ = _small_allgather(pack, "gather_small_grads", exchange=parts)
    gp = gp.reshape(8, _PACK_ROWS, D)
    shard_grads = _chip_sum([*parts, gl["w_up"][0], gl["w_down"][0]], [*arrived, gl["w_up"][1], gl["w_down"][1]],
                            chip, ci)
    tot = _sum_leading(gp, "sum_small_grads")
    g = {}
    r = 8
    for name, nr in _PACK:
        shape = gl[name].shape
        g[name] = tot[r:r + nr].reshape(-1)[:math.prod(shape)].reshape(shape)
        r += nr
    dmod_all = jnp.concatenate([gp[:, 0:6].reshape(8, 6 * D),
                                jnp.pad(tot[6:8].reshape(1, 2 * D), ((0, 0), (0, 4 * D))),
                                jnp.zeros((7, 6 * D), F32)], axis=0)
    g["b_mod"] = _sum_leading(dmod_all.reshape(16, 1, 6 * D), "grad_b_mod")
    g["w_mod"], part = _mod_bwd(cs, lax.dynamic_slice(dmod_all, (0, chip * ncol), (16, ncol)), w["w_mod"])
    cparts, _, (g["w_in"], g["w_out"], g["w_up"], g["w_down"]) = _small_allgather(part, "gather_cctx",
                                                                                 share=shard_grads)
    g["c_ctx"] = _cctx_grad(cparts.reshape(8, 16, D)[0::2], w["c_ctx"])
    g["conv_w"] = lax.dynamic_slice(g["conv_w"], (0, chip * 128), (CK, 128))
    g["ffn_conv_w"] = lax.dynamic_slice(g["ffn_conv_w"], (0, chip * 1408), (3, 1408))

    delta, new_m, new_v = {}, {}, {}
    for name in _BIG:
        delta[name], new_m[name], new_v[name] = _adam_big(w[name], g[name], m[name], v[name], "adam_" + name)
    ds, ms, vs = _adam_small([w[n] for n in _SMALL], [g[n] for n in _SMALL], [m[n] for n in _SMALL],
                             [v[n] for n in _SMALL])
    for k, name in enumerate(_SMALL):
        delta[name], new_m[name], new_v[name] = ds[k], ms[k], vs[k]

    def shaped(d, name):
        a = d[name]
        if name == "c_ctx":
            return a.reshape(D)
        if name in ("w_mod", "w_in", "w_out", "w_up", "w_down", "conv_w", "ffn_conv_w"):
            return a[None]
        return a

    outs = [loss, gx[None]]
    for d in (g, delta, new_m, new_v):
        outs += [shaped(d, name) for name in _ORDER]
    return tuple(outs)
```

```python
import functools
import math

import jax
import jax.numpy as jnp
from jax import lax
from jax.experimental import pallas as pl
from jax.experimental.pallas import tpu as pltpu

F32 = jnp.float32
_MX = jnp.bfloat16
EPS = 1e-6
NEG = -1e30
D = 1024
AW = 512
CW = 512
INW = 1792
FH = 2816
LC = 256
CK = 31
GRID_W = 64
TM = 512
TQ = 128
NCH = FH // 128
FQ = 2 * FH // 4
VMEM_LIMIT = 56 * 1024 * 1024
MESH = pl.DeviceIdType.MESH

ADAM_LR, ADAM_B1, ADAM_B2, ADAM_EPS, ADAM_WD, ADAM_STEP = 0.001, 0.9, 0.999, 1e-08, 0.01, 10


def _cp(*sem):
    return pltpu.CompilerParams(dimension_semantics=sem, vmem_limit_bytes=VMEM_LIMIT)


def _vspec():
    return pl.BlockSpec(memory_space=pltpu.VMEM)


def _sig(z):
    return 1.0 / (1.0 + jnp.exp(-z))


def _dot(a, b):
    return jnp.dot(a, b, preferred_element_type=F32)


def _dot_nt(a, b):
    return lax.dot_general(a, b, (((1,), (1,)), ((), ())), preferred_element_type=F32)


def _dot_tn(a, b):
    return lax.dot_general(a, b, (((0,), (0,)), ((), ())), preferred_element_type=F32)


def _gsum(v, g):
    hi = v.astype(_MX)
    lo = (v - hi.astype(F32)).astype(_MX)
    return _dot(hi, g) + _dot(lo, g)


def _headnorm(ch, w, g):
    r = lax.rsqrt(_gsum(ch * ch, g) * (1.0 / 64.0) + EPS)
    return ch * r * w


def _headnorm_bwd(ch, dy, w, g):
    r = lax.rsqrt(_gsum(ch * ch, g) * (1.0 / 64.0) + EPS)
    hat = ch * r
    dhat = dy * w
    dch = r * (dhat - hat * (_gsum(dhat * hat, g) * (1.0 / 64.0)))
    return dch, dy * hat


def _rope(u, cos, sa, sb):
    return u * cos + pltpu.roll(u, 112, 1) * sa + pltpu.roll(u, 16, 1) * sb


def _rope_bwd(d, cos, sa, sb):
    return d * cos - pltpu.roll(d, 112, 1) * sa - pltpu.roll(d, 16, 1) * sb


def _rowsum(v):
    return jnp.sum(v, axis=0, keepdims=True)


def _call_hosting_gather(body, gather, step_of, nsteps, *, in_specs, out_specs, out_shape, scratch_shapes=(), **kw):
    ng, n_in, n_out = len(gather), len(in_specs), len(out_specs)
    if not ng:
        return pl.pallas_call(body, in_specs=in_specs, out_specs=out_specs, out_shape=out_shape,
                              scratch_shapes=list(scratch_shapes), **kw)

    def hosted(*refs):
        ins, outs = refs[:n_in], refs[n_in + ng:n_in + ng + n_out]
        rest = refs[n_in + 2 * ng + n_out:]
        gat = _Gather(refs[n_in:n_in + ng], refs[n_in + ng + n_out:n_in + 2 * ng + n_out], rest[0], rest[1],
                      [b.shape[1] for b in gather])
        step = step_of()
        pl.when(step == 0)(gat.start)
        body(*ins, *outs, *rest[2:])
        pl.when(step == max(nsteps - 3, 0))(gat.forward)
        pl.when(step == nsteps - 1)(gat.finish)

    return pl.pallas_call(
        hosted, in_specs=list(in_specs) + _any_specs(ng), out_specs=list(out_specs) + _any_specs(ng),
        out_shape=list(out_shape) + [jax.ShapeDtypeStruct(b.shape, b.dtype) for b in gather],
        input_output_aliases={n_in + w: n_out + w for w in range(ng)},
        scratch_shapes=_Gather.semaphores(ng) + list(scratch_shapes), **kw)


def _mix_in(x, nw, sc, sh, win, qnw2, knw2, cos, sa, sb, gmat, gather=()):
    s = x.shape[0]

    def body(x_ref, nw_ref, sc_ref, sh_ref, win_ref, qnw_ref, knw_ref, cos_ref, sa_ref, sb_ref, g_ref,
             proj_ref, h_ref, q0_ref, q1_ref, k_ref, ksw_ref, v_ref, vsw_ref, u0_ref):
        xv = x_ref[...]
        r = lax.rsqrt(jnp.mean(xv * xv, axis=-1, keepdims=True) + EPS)
        h = xv * r * (nw_ref[...] * (1.0 + sc_ref[...])) + sh_ref[...]
        hb = h.astype(_MX)
        h_ref[...] = hb
        proj = _dot(hb, win_ref[...])
        proj_ref[...] = proj
        cs, sav, sbv, g = cos_ref[...], sa_ref[...], sb_ref[...], g_ref[...]
        lo = lax.broadcasted_iota(jnp.int32, (1, 128), 1) < 64
        for j in range(4):
            q = _rope(_headnorm(proj[:, 128 * j:128 * (j + 1)], qnw_ref[...], g), cs, sav, sbv) * 0.125
            q0_ref[:, 128 * j:128 * (j + 1)] = jnp.where(lo, q, 0.0).astype(_MX)
            q1_ref[:, 128 * j:128 * (j + 1)] = jnp.where(lo, 0.0, q).astype(_MX)
        k = _rope(_headnorm(proj[:, 512:640], knw_ref[...], g), cs, sav, sbv)
        k_ref[...] = k.astype(_MX)
        ksw_ref[...] = pltpu.roll(k, 64, 1).astype(_MX)
        v = proj[:, 640:768]
        v_ref[...] = v.astype(_MX)
        vsw_ref[...] = pltpu.roll(v, 64, 1).astype(_MX)
        u0_ref[...] = proj[:, 768:1280] * _sig(proj[:, 1280:1792])

    row = lambda n: pl.BlockSpec((1, n), lambda i: (0, 0))
    tile = lambda n: pl.BlockSpec((TM, n), lambda i: (i, 0))
    res = _call_hosting_gather(
        body, gather, lambda: pl.program_id(0), s // TM, name="mix_in", grid=(s // TM,),
        in_specs=[tile(D), row(D), row(D), row(D), pl.BlockSpec((D, INW), lambda i: (0, 0)), row(128), row(128),
                  tile(128), tile(128), tile(128), pl.BlockSpec((128, 128), lambda i: (0, 0))],
        out_specs=[tile(INW), tile(D), tile(AW), tile(AW), tile(128), tile(128), tile(128), tile(128), tile(CW)],
        out_shape=[jax.ShapeDtypeStruct((s, INW), F32), jax.ShapeDtypeStruct((s, D), _MX),
                   jax.ShapeDtypeStruct((s, AW), _MX), jax.ShapeDtypeStruct((s, AW), _MX),
                   jax.ShapeDtypeStruct((s, 128), _MX), jax.ShapeDtypeStruct((s, 128), _MX),
                   jax.ShapeDtypeStruct((s, 128), _MX), jax.ShapeDtypeStruct((s, 128), _MX),
                   jax.ShapeDtypeStruct((s, CW), F32)],
        compiler_params=_cp("arbitrary"),
    )(x, nw, sc, sh, win, qnw2, knw2, cos, sa, sb, gmat, *gather)
    return res[:9], res[9:]


def _ctx_kv(ctx, nw, scc, shc, winkv, knw2, gmat):
    def body(ctx_ref, nw_ref, sc_ref, sh_ref, w_ref, knw_ref, g_ref,
             kvc_ref, hc_ref, kc_ref, kcsw_ref, vc_ref, vcsw_ref):
        cv = ctx_ref[...]
        r = lax.rsqrt(jnp.mean(cv * cv, axis=-1, keepdims=True) + EPS)
        hc = (cv * r * (nw_ref[...] * (1.0 + sc_ref[...])) + sh_ref[...]).astype(_MX)
        hc_ref[...] = hc
        kvc = _dot(hc, w_ref[...])
        kvc_ref[...] = kvc
        kc = _headnorm(kvc[:, :128], knw_ref[...], g_ref[...])
        kc_ref[...] = kc.astype(_MX)
        kcsw_ref[...] = pltpu.roll(kc, 64, 1).astype(_MX)
        vc = kvc[:, 128:]
        vc_ref[...] = vc.astype(_MX)
        vcsw_ref[...] = pltpu.roll(vc, 64, 1).astype(_MX)

    return pl.pallas_call(
        body, name="ctx_kv",
        in_specs=[_vspec()] * 7, out_specs=[_vspec()] * 6,
        out_shape=[jax.ShapeDtypeStruct((LC, 256), F32), jax.ShapeDtypeStruct((LC, D), _MX)]
        + [jax.ShapeDtypeStruct((LC, 128), _MX)] * 4,
        compiler_params=pltpu.CompilerParams(vmem_limit_bytes=VMEM_LIMIT),
    )(ctx, nw, scc, shc, winkv, knw2, gmat)


def _attn_mask(i, s):
    r = lax.broadcasted_iota(jnp.int32, (2 * TQ, LC + 3 * TQ), 0) % TQ
    cidx = lax.broadcasted_iota(jnp.int32, (2 * TQ, LC + 3 * TQ), 1)
    qpos = i * TQ + r
    kpos = (i - 1) * TQ + (cidx - LC)
    near = (jnp.abs(qpos - kpos) <= 128) & (kpos >= 0) & (kpos < s)
    return (cidx < LC) | near


def _attn_bias(i, s):
    return jnp.where(_attn_mask(i, s), 0.0, NEG)


def _attn_exp(sc, bias, sinkv):
    sc = sc + bias
    m = jnp.maximum(jnp.max(sc, axis=-1, keepdims=True), sinkv)
    ex = jnp.exp(sc - m)
    es = jnp.exp(sinkv - m)
    return ex, es, 1.0 / (jnp.sum(ex, axis=-1, keepdims=True) + es)


def _sink_rows(sink_ref, g, e):
    return jnp.concatenate([jnp.full((TQ, 1), sink_ref[4 * g + e], F32),
                            jnp.full((TQ, 1), sink_ref[4 * g + 2 + e], F32)], axis=0)


def _attn_fwd(q0, q1, kp, kswp, vp, vswp, kc, kcsw, vc, vcsw, sink, gather=()):
    s = q0.shape[0]
    nb = s // TQ

    def body(q0_ref, q1_ref, kp_ref, kswp_ref, vp_ref, vswp_ref, kc_ref, kcsw_ref, vc_ref, vcsw_ref, sink_ref,
             o_ref):
        i = pl.program_id(0)
        st = pl.multiple_of(i * TQ, TQ)
        kall = (jnp.concatenate([kc_ref[...], kp_ref[pl.ds(st, 3 * TQ), :]], axis=0),
                jnp.concatenate([kcsw_ref[...], kswp_ref[pl.ds(st, 3 * TQ), :]], axis=0))
        vall = (jnp.concatenate([vc_ref[...], vp_ref[pl.ds(st, 3 * TQ), :]], axis=0),
                jnp.concatenate([vcsw_ref[...], vswp_ref[pl.ds(st, 3 * TQ), :]], axis=0))
        bias = _attn_bias(i, s)
        lo = lax.broadcasted_iota(jnp.int32, (1, 128), 1) < 64
        qrefs = (q0_ref, q1_ref)
        out = [jnp.zeros((TQ, 128), F32) for _ in range(4)]
        combos = [(g, e) for g in range(2) for e in range(2)]
        scores = [_dot_nt(jnp.concatenate([qrefs[e][:, 256 * g:256 * g + 128],
                                           qrefs[e][:, 256 * g + 128:256 * g + 256]], axis=0),
                          kall[0 if e == g else 1]) for g, e in combos]
        parts = []
        for (g, e), sc in zip(combos, scores):
            ex, _, inv = _attn_exp(sc, bias, _sink_rows(sink_ref, g, e))
            parts.append((ex.astype(_MX), inv))
        for (g, e), (ex, inv) in zip(combos, parts):
            me = lo if e == 0 else jnp.logical_not(lo)
            o2 = _dot(ex, vall[0 if e == g else 1]) * inv
            out[2 * g] = out[2 * g] + jnp.where(me, o2[:TQ], 0.0)
            out[2 * g + 1] = out[2 * g + 1] + jnp.where(me, o2[TQ:], 0.0)
        for j in range(4):
            o_ref[:, 128 * j:128 * (j + 1)] = out[j]

    full = lambda a: pl.BlockSpec(a.shape, lambda i: (0, 0))
    qs = pl.BlockSpec((TQ, AW), lambda i: (i, 0))
    res = _call_hosting_gather(
        body, gather, lambda: pl.program_id(0), nb, name="attn_fwd", grid=(nb,),
        in_specs=[qs, qs, full(kp), full(kp), full(kp), full(kp), full(kc), full(kc), full(kc), full(kc),
                  pl.BlockSpec(memory_space=pltpu.SMEM)],
        out_specs=[qs], out_shape=[jax.ShapeDtypeStruct((s, AW), F32)],
        compiler_params=_cp("arbitrary"),
    )(q0, q1, kp, kswp, vp, vswp, kc, kcsw, vc, vcsw, sink, *gather)
    return res[0], res[1:]


def _conv31(u0, cw32, cb):
    s = u0.shape[0]
    rch = 256

    def body(u_ref, w_ref, b_ref, o_ref, pad_ref):
        pad_ref[0:16, :] = jnp.zeros((16, 128), F32)
        pad_ref[s + 16:s + 32, :] = jnp.zeros((16, 128), F32)
        pad_ref[16:s + 16, :] = u_ref[...]
        for cidx in range(s // rch):
            base = cidx * rch
            acc = jnp.zeros((rch, 128), F32) + b_ref[...]
            for j in range(CK):
                acc = acc + w_ref[j:j + 1, :] * pad_ref[base + j + 1:base + j + 1 + rch, :]
            o_ref[base:base + rch, :] = acc

    return pl.pallas_call(
        body, name="conv31", grid=(CW // 128,),
        in_specs=[pl.BlockSpec((s, 128), lambda c: (0, c)), pl.BlockSpec((32, 128), lambda c: (0, c)),
                  pl.BlockSpec((1, 128), lambda c: (0, c))],
        out_specs=pl.BlockSpec((s, 128), lambda c: (0, c)),
        out_shape=jax.ShapeDtypeStruct((s, CW), F32),
        scratch_shapes=[pltpu.VMEM((s + 32, 128), F32)],
        compiler_params=_cp("arbitrary"),
    )(u0, cw32, cb)


def _ln_stats(u1):
    mu = jnp.mean(u1, axis=-1, keepdims=True)
    xc = u1 - mu
    rstd = lax.rsqrt(jnp.mean(xc * xc, axis=-1, keepdims=True) + EPS)
    return xc * rstd, rstd


def _mix_out(o, u1, cnw, cnb, wout, x, g1):
    s = x.shape[0]

    def body(o_ref, u1_ref, cnw_ref, cnb_ref, w_ref, x_ref, g1_ref, x1_ref, mix_ref, cat_ref):
        u2n, _ = _ln_stats(u1_ref[...])
        u2 = u2n * cnw_ref[...] + cnb_ref[...]
        u3 = u2 * _sig(u2)
        cat = jnp.concatenate([o_ref[...], u3], axis=1).astype(_MX)
        cat_ref[...] = cat
        mix = _dot(cat, w_ref[...])
        mix_ref[...] = mix
        x1_ref[...] = x_ref[...] + g1_ref[...] * mix

    row = lambda n: pl.BlockSpec((1, n), lambda i: (0, 0))
    tile = lambda n: pl.BlockSpec((TM, n), lambda i: (i, 0))
    return pl.pallas_call(
        body, name="mix_out", grid=(s // TM,),
        in_specs=[tile(AW), tile(CW), row(CW), row(CW), pl.BlockSpec((D, D), lambda i: (0, 0)), tile(D), row(D)],
        out_specs=[tile(D), tile(D), tile(D)],
        out_shape=[jax.ShapeDtypeStruct((s, D), F32), jax.ShapeDtypeStruct((s, D), F32),
                   jax.ShapeDtypeStruct((s, D), _MX)],
        compiler_params=_cp("arbitrary"),
    )(o, u1, cnw, cnb, wout, x, g1)


def _ffn_up(x1, nw, sc, sh, wup4, gather=()):
    s = x1.shape[0]
    tm = min(1024, s)

    def body(x_ref, nw_ref, sc_ref, sh_ref, w_ref, h2_ref, up_ref, h2s):
        @pl.when(pl.program_id(1) == 0)
        def _():
            xv = x_ref[...]
            r = lax.rsqrt(jnp.mean(xv * xv, axis=-1, keepdims=True) + EPS)
            h2 = (xv * r * (nw_ref[...] * (1.0 + sc_ref[...])) + sh_ref[...]).astype(_MX)
            h2s[...] = h2
            h2_ref[...] = h2
        up_ref[...] = _dot(h2s[...], w_ref[...])

    row = pl.BlockSpec((1, D), lambda i, j: (0, 0))
    res = _call_hosting_gather(
        body, gather, lambda: pl.program_id(0) * 4 + pl.program_id(1), 4 * (s // tm), name="ffn_up",
        grid=(s // tm, 4),
        in_specs=[pl.BlockSpec((tm, D), lambda i, j: (i, 0)), row, row, row,
                  pl.BlockSpec((None, D, FQ), lambda i, j: (j, 0, 0))],
        out_specs=[pl.BlockSpec((tm, D), lambda i, j: (i, 0)),
                   pl.BlockSpec((None, tm, FQ), lambda i, j: (j // 2, i, j % 2))],
        out_shape=[jax.ShapeDtypeStruct((s, D), _MX), jax.ShapeDtypeStruct((2, s, FH), F32)],
        scratch_shapes=[pltpu.VMEM((tm, D), _MX)],
        compiler_params=_cp("arbitrary", "arbitrary"),
    )(x1, nw, sc, sh, wup4, *gather)
    return res[:2], res[2:]


def _fill_pad8(pad_ref, val, s):
    pad_ref[0:8, :] = jnp.zeros((8, 128), F32)
    pad_ref[s + 8:s + 16, :] = jnp.zeros((8, 128), F32)
    pad_ref[8:s + 8, :] = val


def _conv3_at(pad_ref, w_ref, half, base, rch):
    return (w_ref[half, 0:1, :] * pad_ref[base + 7:base + 7 + rch, :]
            + w_ref[half, 1:2, :] * pad_ref[base + 8:base + 8 + rch, :]
            + w_ref[half, 2:3, :] * pad_ref[base + 9:base + 9 + rch, :])


def _ffn_conv_act(up0, fcw, fcb):
    s = up0.shape[1]
    rch = 256

    def body(up_ref, w_ref, b_ref, act_ref, padg, padv):
        _fill_pad8(padg, up_ref[0], s)
        _fill_pad8(padv, up_ref[1], s)
        for cidx in range(s // rch):
            base = cidx * rch
            gate = _conv3_at(padg, w_ref, 0, base, rch) + b_ref[0]
            val = _conv3_at(padv, w_ref, 1, base, rch) + b_ref[1]
            act_ref[base:base + rch, :] = (gate * _sig(gate) * val).astype(_MX)

    return pl.pallas_call(
        body, name="ffn_conv_act", grid=(NCH,),
        in_specs=[pl.BlockSpec((2, s, 128), lambda c: (0, 0, c)), pl.BlockSpec((2, 8, 128), lambda c: (0, 0, c)),
                  pl.BlockSpec((2, 1, 128), lambda c: (0, 0, c))],
        out_specs=pl.BlockSpec((s, 128), lambda c: (0, c)),
        out_shape=jax.ShapeDtypeStruct((s, FH), _MX),
        scratch_shapes=[pltpu.VMEM((s + 16, 128), F32)] * 2,
        compiler_params=_cp("arbitrary"),
    )(up0, fcw, fcb)


def _ffn_down(act, wdown, x1, g2, tgt):
    s = x1.shape[0]

    def body(act_ref, w_ref, x1_ref, g2_ref, tgt_ref, dy_ref, dact_ref, ddn_ref, vec_ref):
        @pl.when(pl.program_id(0) == 0)
        def _():
            vec_ref[...] = jnp.zeros((8, D), F32)
        dn = _dot(act_ref[...], w_ref[...])
        diff = x1_ref[...] + g2_ref[...] * dn - tgt_ref[...]
        dy = diff * (1.0 / D)
        dy_ref[...] = dy
        ddn = (dy * g2_ref[...]).astype(_MX)
        ddn_ref[...] = ddn
        dact_ref[...] = _dot_nt(ddn, w_ref[...])
        vec_ref[0:1, :] += _rowsum(dy * dn)
        vec_ref[1:2, :] += _rowsum(diff * diff)

    tile = lambda n: pl.BlockSpec((TM, n), lambda i: (i, 0))
    return pl.pallas_call(
        body, name="ffn_down", grid=(s // TM,),
        in_specs=[tile(FH), pl.BlockSpec((FH, D), lambda i: (0, 0)), tile(D), pl.BlockSpec((1, D), lambda i: (0, 0)),
                  tile(D)],
        out_specs=[tile(D), tile(FH), tile(D), pl.BlockSpec((8, D), lambda i: (0, 0))],
        out_shape=[jax.ShapeDtypeStruct((s, D), F32), jax.ShapeDtypeStruct((s, FH), F32),
                   jax.ShapeDtypeStruct((s, D), _MX), jax.ShapeDtypeStruct((8, D), F32)],
        compiler_params=_cp("arbitrary"),
    )(act, wdown, x1, g2, tgt)


def _ffn_conv_bwd(up0, dact, fcw, fcb):
    s = up0.shape[1]
    rch = 256

    def body(up_ref, da_ref, w_ref, b_ref, dup_ref, gw_ref, padg, padv, dpg, dpv):
        _fill_pad8(padg, up_ref[0], s)
        _fill_pad8(padv, up_ref[1], s)
        for p in (dpg, dpv):
            p[0:8, :] = jnp.zeros((8, 128), F32)
            p[s + 8:s + 16, :] = jnp.zeros((8, 128), F32)
        acc = [[jnp.zeros((1, 128), F32) for _ in range(4)] for _ in range(2)]
        for cidx in range(s // rch):
            base = cidx * rch
            gate = _conv3_at(padg, w_ref, 0, base, rch) + b_ref[0]
            val = _conv3_at(padv, w_ref, 1, base, rch) + b_ref[1]
            da = da_ref[base:base + rch, :]
            sg = _sig(gate)
            dgate = da * val * sg * (1.0 + gate * (1.0 - sg))
            dval = da * gate * sg
            dpg[base + 8:base + 8 + rch, :] = dgate
            dpv[base + 8:base + 8 + rch, :] = dval
            for half, (dd, pad) in enumerate(((dgate, padg), (dval, padv))):
                for j in range(3):
                    acc[half][j] = acc[half][j] + _rowsum(dd * pad[base + 7 + j:base + 7 + j + rch, :])
                acc[half][3] = acc[half][3] + _rowsum(dd)
        for half in range(2):
            gw_ref[half] = jnp.zeros((8, 128), F32)
            for j in range(4):
                gw_ref[half, j:j + 1, :] = acc[half][j]
        for cidx in range(s // rch):
            base = cidx * rch
            for half, dp in enumerate((dpg, dpv)):
                dup_ref[half, base:base + rch, :] = (
                    w_ref[half, 0:1, :] * dp[base + 9:base + 9 + rch, :]
                    + w_ref[half, 1:2, :] * dp[base + 8:base + 8 + rch, :]
                    + w_ref[half, 2:3, :] * dp[base + 7:base + 7 + rch, :]).astype(_MX)

    return pl.pallas_call(
        body, name="ffn_conv_bwd", grid=(NCH,),
        in_specs=[pl.BlockSpec((2, s, 128), lambda c: (0, 0, c)), pl.BlockSpec((s, 128), lambda c: (0, c)),
                  pl.BlockSpec((2, 8, 128), lambda c: (0, 0, c)), pl.BlockSpec((2, 1, 128), lambda c: (0, 0, c))],
        out_specs=[pl.BlockSpec((2, s, 128), lambda c: (0, 0, c)), pl.BlockSpec((2, 8, 128), lambda c: (0, 0, c))],
        out_shape=[jax.ShapeDtypeStruct((2, s, FH), _MX), jax.ShapeDtypeStruct((2, 8, FH), F32)],
        scratch_shapes=[pltpu.VMEM((s + 16, 128), F32)] * 4,
        compiler_params=_cp("arbitrary"),
    )(up0, dact, fcw, fcb)


def _tn_matmul(a, b, tm, tn, name, b_split=False, by_chip=False, init=None):
    s, m = a.shape
    n = 2 * b.shape[2] if b_split else b.shape[1]
    ts = min(1024, s)
    nsteps = s // ts
    npb = (n // 2) // tn if b_split else None

    def body(*refs):
        if init is None:
            a_ref, b_ref, o_ref, acc = refs
        else:
            a_ref, b_ref, i_ref, o_ref, acc = refs
        k = pl.program_id(2)

        @pl.when(k == 0)
        def _():
            acc[...] = jnp.zeros((tm, tn), F32)
            if init is not None:
                acc[:, 512:768] = i_ref[...]
        acc[...] += _dot_tn(a_ref[...], b_ref[...])

        @pl.when(k == nsteps - 1)
        def _():
            o_ref[...] = acc[...].astype(_MX)

    if b_split:
        bspec = pl.BlockSpec((None, ts, tn), lambda i, j, k: (j // npb, k, j % npb))
    else:
        bspec = pl.BlockSpec((ts, tn), lambda i, j, k: (k, j))
    in_specs = [pl.BlockSpec((ts, tm), lambda i, j, k: (k, i)), bspec]
    args = [a, b]
    if init is not None:
        in_specs.append(pl.BlockSpec((tm, 256), lambda i, j, k: (i, 0)))
        args.append(init)
    if by_chip:
        out_spec = pl.BlockSpec((None, tm, tn), lambda i, j, k: (j, i, 0))
        out_shape = jax.ShapeDtypeStruct((n // tn, m, tn), _MX)
    else:
        out_spec = pl.BlockSpec((tm, tn), lambda i, j, k: (i, j))
        out_shape = jax.ShapeDtypeStruct((m, n), _MX)
    return pl.pallas_call(
        body, name=name, grid=(m // tm, n // tn, nsteps),
        in_specs=in_specs, out_specs=out_spec, out_shape=out_shape,
        scratch_shapes=[pltpu.VMEM((tm, tn), F32)],
        compiler_params=_cp("arbitrary", "arbitrary", "arbitrary"),
    )(*args)


def _ffn_up_bwd(dup, wup4, x1, dy, mix, nw, sc, g1):
    s = x1.shape[0]
    nk = 4

    def body(dup_ref, w_ref, x1_ref, dy_ref, mix_ref, nw_ref, sc_ref, g1_ref, dx1_ref, dmix_ref, vec_ref, acc):
        i, k = pl.program_id(0), pl.program_id(1)

        @pl.when(k == 0)
        def _():
            acc[...] = jnp.zeros((TM, D), F32)

        @pl.when((k == 0) & (i == 0))
        def _():
            vec_ref[...] = jnp.zeros((8, D), F32)
        acc[...] += _dot_nt(dup_ref[...], w_ref[...])

        @pl.when(k == nk - 1)
        def _():
            dh = acc[...]
            xv = x1_ref[...]
            r = lax.rsqrt(jnp.mean(xv * xv, axis=-1, keepdims=True) + EPS)
            xn = xv * r
            nwv, scv = nw_ref[...], sc_ref[...]
            vec_ref[0:1, :] += _rowsum(dh)
            vec_ref[1:2, :] += _rowsum(dh * xn) * nwv
            vec_ref[2:3, :] += _rowsum(dh * xn) * (1.0 + scv)
            dxn = dh * (nwv * (1.0 + scv))
            dx1 = dy_ref[...] + r * (dxn - xn * jnp.mean(dxn * xn, axis=-1, keepdims=True))
            dx1_ref[...] = dx1
            vec_ref[3:4, :] += _rowsum(dx1 * mix_ref[...])
            dmix_ref[...] = (dx1 * g1_ref[...]).astype(_MX)

    tile = pl.BlockSpec((TM, D), lambda i, k: (i, 0))
    row = pl.BlockSpec((1, D), lambda i, k: (0, 0))
    return pl.pallas_call(
        body, name="ffn_up_bwd", grid=(s // TM, nk),
        in_specs=[pl.BlockSpec((None, TM, FQ), lambda i, k: (k // 2, i, k % 2)),
                  pl.BlockSpec((None, D, FQ), lambda i, k: (k, 0, 0)), tile, tile, tile, row, row, row],
        out_specs=[tile, tile, pl.BlockSpec((8, D), lambda i, k: (0, 0))],
        out_shape=[jax.ShapeDtypeStruct((s, D), F32), jax.ShapeDtypeStruct((s, D), _MX),
                   jax.ShapeDtypeStruct((8, D), F32)],
        scratch_shapes=[pltpu.VMEM((TM, D), F32)],
        compiler_params=_cp("arbitrary", "arbitrary"),
    )(dup, wup4, x1, dy, mix, nw, sc, g1)


def _mix_out_bwd(dmix, wout, u1, cnw, cnb):
    s = u1.shape[0]

    def body(dm_ref, w_ref, u1_ref, cnw_ref, cnb_ref, do_ref, du1_ref, vec_ref):
        @pl.when(pl.program_id(0) == 0)
        def _():
            vec_ref[...] = jnp.zeros((8, CW), F32)
        dcat = _dot_nt(dm_ref[...], w_ref[...])
        do_ref[...] = dcat[:, :AW]
        du3 = dcat[:, AW:]
        u2n, rstd = _ln_stats(u1_ref[...])
        u2 = u2n * cnw_ref[...] + cnb_ref[...]
        sg = _sig(u2)
        du2 = du3 * sg * (1.0 + u2 * (1.0 - sg))
        vec_ref[0:1, :] += _rowsum(du2)
        vec_ref[1:2, :] += _rowsum(du2 * u2n)
        d2n = du2 * cnw_ref[...]
        du1_ref[...] = rstd * (d2n - jnp.mean(d2n, axis=-1, keepdims=True)
                               - u2n * jnp.mean(d2n * u2n, axis=-1, keepdims=True))

    row = lambda n: pl.BlockSpec((1, n), lambda i: (0, 0))
    tile = lambda n: pl.BlockSpec((TM, n), lambda i: (i, 0))
    return pl.pallas_call(
        body, name="mix_out_bwd", grid=(s // TM,),
        in_specs=[tile(D), pl.BlockSpec((D, D), lambda i: (0, 0)), tile(CW), row(CW), row(CW)],
        out_specs=[tile(AW), tile(CW), pl.BlockSpec((8, CW), lambda i: (0, 0))],
        out_shape=[jax.ShapeDtypeStruct((s, AW), F32), jax.ShapeDtypeStruct((s, CW), F32),
                   jax.ShapeDtypeStruct((8, CW), F32)],
        compiler_params=_cp("arbitrary"),
    )(dmix, wout, u1, cnw, cnb)


def _conv31_bwd(du1, u0, cw32, proj):
    s = u0.shape[0]
    rch = 256

    def body(d_ref, u_ref, w_ref, ga_ref, gb_ref, dga_ref, dgb_ref, gw_ref, padu, padd):
        for p, src in ((padu, u_ref), (padd, d_ref)):
            p[0:16, :] = jnp.zeros((16, 128), F32)
            p[s + 16:s + 32, :] = jnp.zeros((16, 128), F32)
            p[16:s + 16, :] = src[...]
        for j in range(CK):
            acc = jnp.zeros((1, 128), F32)
            for cidx in range(s // rch):
                base = cidx * rch
                acc = acc + _rowsum(d_ref[base:base + rch, :] * padu[base + j + 1:base + j + 1 + rch, :])
            gw_ref[j:j + 1, :] = acc
        gw_ref[CK:CK + 1, :] = _rowsum(d_ref[...])
        for cidx in range(s // rch):
            base = cidx * rch
            du0 = jnp.zeros((rch, 128), F32)
            for j in range(CK):
                du0 = du0 + w_ref[j:j + 1, :] * padd[base + 31 - j:base + 31 - j + rch, :]
            sg = _sig(gb_ref[base:base + rch, :])
            ga = ga_ref[base:base + rch, :]
            dga_ref[base:base + rch, :] = (du0 * sg).astype(_MX)
            dgb_ref[base:base + rch, :] = (du0 * ga * sg * (1.0 - sg)).astype(_MX)

    blk = lambda off: pl.BlockSpec((s, 128), lambda c: (0, c + off))
    return pl.pallas_call(
        body, name="conv31_bwd", grid=(CW // 128,),
        in_specs=[blk(0), blk(0), pl.BlockSpec((32, 128), lambda c: (0, c)), blk(6), blk(10)],
        out_specs=[blk(0), blk(0), pl.BlockSpec((32, 128), lambda c: (0, c))],
        out_shape=[jax.ShapeDtypeStruct((s, CW), _MX), jax.ShapeDtypeStruct((s, CW), _MX),
                   jax.ShapeDtypeStruct((32, CW), F32)],
        scratch_shapes=[pltpu.VMEM((s + 32, 128), F32)] * 2,
        compiler_params=_cp("arbitrary"),
    )(du1, u0, cw32, proj, proj)


def _attn_bwd(q0, q1, kp, kswp, vp, vswp, kc, kcsw, vc, vcsw, sink, o, do, exchange=()):
    s = q0.shape[0]
    nb = s // TQ
    ne = len(exchange)

    def body(*refs):
        (q0_ref, q1_ref, kp_ref, kswp_ref, vp_ref, vswp_ref, kc_ref, kcsw_ref, vc_ref, vcsw_ref, sink_ref,
         o_ref, do_ref) = refs[:13]
        dq_ref, dk_ref, dv_ref, dkc_ref, dvc_ref, dsink_ref = refs[13 + ne:19 + ne]
        i = pl.program_id(0)
        if ne:
            exch = _Exchange(refs[13:13 + ne], refs[19 + ne:19 + 2 * ne], refs[19 + 2 * ne], refs[20 + 2 * ne])
            pl.when(i == 0)(exch.start)

        @pl.when(i == 0)
        def _():
            dk_ref[...] = jnp.zeros((s + 2 * TQ, 128), F32)
            dv_ref[...] = jnp.zeros((s + 2 * TQ, 128), F32)
            dkc_ref[...] = jnp.zeros((LC, 128), F32)
            dvc_ref[...] = jnp.zeros((LC, 128), F32)
            dsink_ref[...] = jnp.zeros((8, 128), F32)
        st = pl.multiple_of(i * TQ, TQ)
        kall = (jnp.concatenate([kc_ref[...], kp_ref[pl.ds(st, 3 * TQ), :]], axis=0),
                jnp.concatenate([kcsw_ref[...], kswp_ref[pl.ds(st, 3 * TQ), :]], axis=0))
        vall = (jnp.concatenate([vc_ref[...], vp_ref[pl.ds(st, 3 * TQ), :]], axis=0),
                jnp.concatenate([vcsw_ref[...], vswp_ref[pl.ds(st, 3 * TQ), :]], axis=0))
        bias = _attn_bias(i, s)
        lo = lax.broadcasted_iota(jnp.int32, (1, 128), 1) < 64
        qrefs = (q0_ref, q1_ref)
        dq = [jnp.zeros((TQ, 128), F32) for _ in range(4)]
        dkt = jnp.zeros((128, LC + 3 * TQ), F32)
        dvt = jnp.zeros((128, LC + 3 * TQ), F32)
        combos = [(g, e) for g in range(2) for e in range(2)]

        def stacked(ref, g):
            return jnp.concatenate([ref[:, 256 * g:256 * g + 128], ref[:, 256 * g + 128:256 * g + 256]], axis=0)

        for g, e in combos:
            me = lo if e == 0 else jnp.logical_not(lo)
            sw = 0 if e == g else 1
            qm = stacked(qrefs[e], g)
            ex, es, inv = _attn_exp(_dot_nt(qm, kall[sw]), bias, _sink_rows(sink_ref, g, e))
            p = ex * inv
            dom = jnp.where(me, stacked(do_ref, g), 0.0)
            dd = jnp.sum(dom * stacked(o_ref, g), axis=-1, keepdims=True)
            domx = dom.astype(_MX)
            sd = es * inv * dd
            dsink_ref[4 * g + e:4 * g + e + 1, :] -= jnp.sum(sd[:TQ], axis=0, keepdims=True)
            dsink_ref[4 * g + 2 + e:4 * g + 3 + e, :] -= jnp.sum(sd[TQ:], axis=0, keepdims=True)
            ds = (p * (_dot_nt(domx, vall[sw]) - dd)).astype(_MX)
            dq2 = _dot(ds, kall[sw])
            dq[2 * g] = dq[2 * g] + jnp.where(me, dq2[:TQ], 0.0)
            dq[2 * g + 1] = dq[2 * g + 1] + jnp.where(me, dq2[TQ:], 0.0)
            dk2 = _dot_tn(qm, ds)
            dv2 = _dot_tn(domx, p.astype(_MX))
            if sw:
                dk2 = pltpu.roll(dk2, 64, 0)
                dv2 = pltpu.roll(dv2, 64, 0)
            dkt = dkt + dk2
            dvt = dvt + dv2
        dkt = dkt.T
        dvt = dvt.T
        for j in range(4):
            dq_ref[:, 128 * j:128 * (j + 1)] = dq[j] * 0.125
        dkc_ref[...] += dkt[:LC]
        dvc_ref[...] += dvt[:LC]
        dk_ref[pl.ds(st, 3 * TQ), :] += dkt[LC:]
        dv_ref[pl.ds(st, 3 * TQ), :] += dvt[LC:]
        if ne:
            pl.when(i == nb - 1)(exch.finish)

    full = lambda a: pl.BlockSpec(a.shape, lambda i: (0, 0))
    fs = lambda r: pl.BlockSpec((r, 128), lambda i: (0, 0))
    qs = pl.BlockSpec((TQ, AW), lambda i: (i, 0))
    res = pl.pallas_call(
        body, name="attn_bwd", grid=(nb,),
        in_specs=[qs, qs, full(kp), full(kp), full(kp), full(kp), full(kc), full(kc), full(kc), full(kc),
                  pl.BlockSpec(memory_space=pltpu.SMEM), qs, qs] + _any_specs(ne),
        out_specs=[qs, fs(s + 2 * TQ), fs(s + 2 * TQ), fs(LC), fs(LC), fs(8)] + _any_specs(ne),
        out_shape=[jax.ShapeDtypeStruct((s, AW), F32), jax.ShapeDtypeStruct((s + 2 * TQ, 128), F32),
                   jax.ShapeDtypeStruct((s + 2 * TQ, 128), F32), jax.ShapeDtypeStruct((LC, 128), F32),
                   jax.ShapeDtypeStruct((LC, 128), F32), jax.ShapeDtypeStruct((8, 128), F32)]
        + _Exchange.out_shapes(exchange),
        scratch_shapes=_Exchange.semaphores(ne) if ne else [],
        compiler_params=_cp("arbitrary"),
    )(q0, q1, kp, kswp, vp, vswp, kc, kcsw, vc, vcsw, sink, o, do, *exchange)
    return res[:6], res[6:]


def _mix_in_bwd(dq, dk, dv, dga, dgb, proj, x, dx1, win, nw, sc, qnw2, knw2, cos, sa, sb, gmat):
    s = x.shape[0]

    def body(dq_ref, dk_ref, dv_ref, dga_ref, dgb_ref, proj_ref, x_ref, dx1_ref, win_ref, nw_ref, sc_ref,
             qnw_ref, knw_ref, cos_ref, sa_ref, sb_ref, g_ref, gx_ref, dproj_ref, vec_ref):
        @pl.when(pl.program_id(0) == 0)
        def _():
            vec_ref[...] = jnp.zeros((8, D), F32)
        cs, sav, sbv, g = cos_ref[...], sa_ref[...], sb_ref[...], g_ref[...]
        gq = jnp.zeros((1, 128), F32)
        for j in range(4):
            dqn = _rope_bwd(dq_ref[:, 128 * j:128 * (j + 1)], cs, sav, sbv)
            dch, gw = _headnorm_bwd(proj_ref[:, 128 * j:128 * (j + 1)], dqn, qnw_ref[...], g)
            dproj_ref[:, 128 * j:128 * (j + 1)] = dch.astype(_MX)
            gq = gq + _rowsum(gw)
        dkn = _rope_bwd(dk_ref[...], cs, sav, sbv)
        dch, gw = _headnorm_bwd(proj_ref[:, 512:640], dkn, knw_ref[...], g)
        dproj_ref[:, 512:640] = dch.astype(_MX)
        dproj_ref[:, 640:768] = dv_ref[...].astype(_MX)
        dproj_ref[:, 768:1280] = dga_ref[...]
        dproj_ref[:, 1280:1792] = dgb_ref[...]
        vec_ref[3:4, 0:128] += gq
        vec_ref[3:4, 128:256] += _rowsum(gw)
        dh = _dot_nt(dproj_ref[...], win_ref[...])
        xv = x_ref[...]
        r = lax.rsqrt(jnp.mean(xv * xv, axis=-1, keepdims=True) + EPS)
        xn = xv * r
        nwv, scv = nw_ref[...], sc_ref[...]
        vec_ref[0:1, :] += _rowsum(dh)
        vec_ref[1:2, :] += _rowsum(dh * xn) * nwv
        vec_ref[2:3, :] += _rowsum(dh * xn) * (1.0 + scv)
        dxn = dh * (nwv * (1.0 + scv))
        gx_ref[...] = dx1_ref[...] + r * (dxn - xn * jnp.mean(dxn * xn, axis=-1, keepdims=True))

    row = lambda n: pl.BlockSpec((1, n), lambda i: (0, 0))
    tile = lambda n: pl.BlockSpec((TM, n), lambda i: (i, 0))
    return pl.pallas_call(
        body, name="mix_in_bwd", grid=(s // TM,),
        in_specs=[tile(AW), tile(128), tile(128), tile(CW), tile(CW), tile(INW), tile(D), tile(D),
                  pl.BlockSpec((D, INW), lambda i: (0, 0)), row(D), row(D), row(128), row(128),
                  tile(128), tile(128), tile(128), pl.BlockSpec((128, 128), lambda i: (0, 0))],
        out_specs=[tile(D), tile(INW), pl.BlockSpec((8, D), lambda i: (0, 0))],
        out_shape=[jax.ShapeDtypeStruct((s, D), F32), jax.ShapeDtypeStruct((s, INW), _MX),
                   jax.ShapeDtypeStruct((8, D), F32)],
        compiler_params=_cp("arbitrary"),
    )(dq, dk, dv, dga, dgb, proj, x, dx1, win, nw, sc, qnw2, knw2, cos, sa, sb, gmat)


def _ctx_bwd(ctx, nw, scc, winkv, kvc, hc, dkc, dvc, knw2, gmat):
    def body(ctx_ref, nw_ref, sc_ref, w_ref, kvc_ref, hc_ref, dkc_ref, dvc_ref, knw_ref, g_ref, gw_ref, vec_ref):
        dkr, gk = _headnorm_bwd(kvc_ref[:, 0:128], dkc_ref[...], knw_ref[...], g_ref[...])
        dkv = jnp.concatenate([dkr, dvc_ref[...]], axis=1).astype(_MX)
        gw_ref[...] = _dot_tn(hc_ref[...], dkv)
        dh = _dot_nt(dkv, w_ref[...])
        cv = ctx_ref[...]
        r = lax.rsqrt(jnp.mean(cv * cv, axis=-1, keepdims=True) + EPS)
        cn = cv * r
        vec_ref[...] = jnp.zeros((8, D), F32)
        vec_ref[0:1, :] = _rowsum(dh)
        vec_ref[1:2, :] = _rowsum(dh * cn) * nw_ref[...]
        vec_ref[2:3, :] = _rowsum(dh * cn) * (1.0 + sc_ref[...])
        vec_ref[3:4, 0:128] = _rowsum(gk)

    return pl.pallas_call(
        body, name="ctx_bwd", in_specs=[_vspec()] * 10, out_specs=[_vspec()] * 2,
        out_shape=[jax.ShapeDtypeStruct((D, 256), F32), jax.ShapeDtypeStruct((8, D), F32)],
        compiler_params=pltpu.CompilerParams(vmem_limit_bytes=VMEM_LIMIT),
    )(ctx, nw, scc, winkv, kvc, hc, dkc, dvc, knw2, gmat)


def _mod_fwd(cs, wmod, bloc):
    def body(c_ref, w_ref, b_ref, o_ref):
        cv = c_ref[...]
        o_ref[...] = _dot((cv * _sig(cv)).astype(_MX), w_ref[...].astype(_MX)) + b_ref[...]

    return pl.pallas_call(
        body, name="mod_fwd", in_specs=[_vspec()] * 3, out_specs=_vspec(),
        out_shape=jax.ShapeDtypeStruct((16, wmod.shape[1]), F32),
        compiler_params=pltpu.CompilerParams(vmem_limit_bytes=VMEM_LIMIT),
    )(cs, wmod, bloc)


def _mod_bwd(cs, dmod, wmod):
    def body(c_ref, d_ref, w_ref, gw_ref, part_ref):
        cv = c_ref[...]
        sl = (cv * _sig(cv)).astype(_MX)
        dm = d_ref[...].astype(_MX)
        gw_ref[...] = _dot_tn(sl, dm)
        part_ref[...] = _dot_nt(dm, w_ref[...].astype(_MX))

    return pl.pallas_call(
        body, name="mod_bwd", in_specs=[_vspec()] * 3, out_specs=[_vspec()] * 2,
        out_shape=[jax.ShapeDtypeStruct(wmod.shape, F32), jax.ShapeDtypeStruct((16, D), F32)],
        compiler_params=pltpu.CompilerParams(vmem_limit_bytes=VMEM_LIMIT),
    )(cs, dmod, wmod)


def _sum_leading(a, name):
    n = a.shape[0]

    def body(a_ref, o_ref):
        acc = a_ref[0]
        for k in range(1, n):
            acc = acc + a_ref[k]
        o_ref[...] = acc

    return pl.pallas_call(
        body, name=name, in_specs=[_vspec()], out_specs=_vspec(),
        out_shape=jax.ShapeDtypeStruct(a.shape[1:], F32),
        compiler_params=pltpu.CompilerParams(vmem_limit_bytes=VMEM_LIMIT),
    )(a)


def _cctx_grad(parts, cc):
    def body(p_ref, c_ref, o_ref):
        acc = p_ref[0, 8:9, :]
        for k in range(1, 4):
            acc = acc + p_ref[k, 8:9, :]
        cv = c_ref[...]
        sg = _sig(cv)
        o_ref[...] = acc * (sg * (1.0 + cv * (1.0 - sg)))

    return pl.pallas_call(
        body, name="cctx_grad", in_specs=[_vspec()] * 2, out_specs=_vspec(),
        out_shape=jax.ShapeDtypeStruct((1, D), F32),
    )(parts, cc)


def _adam_math(w, g, m, v):
    mn = ADAM_B1 * m + (1.0 - ADAM_B1) * g
    vn = ADAM_B2 * v + (1.0 - ADAM_B2) * (g * g)
    mh = mn / (1.0 - ADAM_B1 ** ADAM_STEP)
    vh = vn / (1.0 - ADAM_B2 ** ADAM_STEP)
    delta = -ADAM_LR * (mh / (jnp.sqrt(vh) + ADAM_EPS) + ADAM_WD * w)
    return delta, mn, vn


def _adam_big(w, g, m, v, name):
    r, n = w.shape
    tr = 256 if r % 256 == 0 else 64

    def body(w_ref, g_ref, m_ref, v_ref, d_ref, mo_ref, vo_ref):
        d, mn, vn = _adam_math(w_ref[...], g_ref[...], m_ref[...], v_ref[...])
        d_ref[...] = d
        mo_ref[...] = mn
        vo_ref[...] = vn

    spec = pl.BlockSpec((tr, n), lambda i: (i, 0))
    return pl.pallas_call(
        body, name=name, grid=(r // tr,), in_specs=[spec] * 4, out_specs=[spec] * 3,
        out_shape=[jax.ShapeDtypeStruct((r, n), F32)] * 3,
        compiler_params=_cp("arbitrary"),
    )(w, g, m, v)


def _adam_small(ws, gs, ms, vs):
    n = len(ws)

    def body(*refs):
        ins, outs = refs[:4 * n], refs[4 * n:]
        for k in range(n):
            d, mn, vn = _adam_math(ins[k][...], ins[n + k][...], ins[2 * n + k][...], ins[3 * n + k][...])
            outs[k][...] = d
            outs[n + k][...] = mn
            outs[2 * n + k][...] = vn

    shapes = [jax.ShapeDtypeStruct(w.shape, F32) for w in ws]
    res = pl.pallas_call(
        body, name="adam_small", in_specs=[_vspec()] * (4 * n), out_specs=[_vspec()] * (3 * n),
        out_shape=shapes * 3,
    )(*ws, *gs, *ms, *vs)
    return res[:n], res[n:2 * n], res[2 * n:]


def _pair_sum(grads, from_sib, core, name):
    n = len(grads)

    def body(c_ref, *refs):
        for w in range(n):
            a_ref, b_ref, o_ref = refs[w], refs[n + w], refs[2 * n + w]
            o_ref[...] = (a_ref[...].astype(F32) + b_ref[...].astype(F32)).astype(_MX)

    halves = [(None, g.shape[1] // 2, g.shape[2]) for g in grads]
    return pl.pallas_call(
        body, name=name,
        grid_spec=pltpu.PrefetchScalarGridSpec(
            num_scalar_prefetch=1, grid=(4,),
            in_specs=[pl.BlockSpec(h, lambda j, c: (j, c[0], 0)) for h in halves]
            + [pl.BlockSpec(h, lambda j, c: (j, 0, 0)) for h in halves],
            out_specs=[pl.BlockSpec(h, lambda j, c: (j, 0, 0)) for h in halves]),
        out_shape=[jax.ShapeDtypeStruct(p.shape, _MX) for p in from_sib],
        compiler_params=_cp("arbitrary"),
    )(core.reshape(1), *grads, *from_sib)


def _chip_sum(parts, arrived, chip, core):
    n = len(parts)

    def body(s_ref, *refs):
        for w in range(n):
            own_ref, p_ref, o_ref = refs[w], refs[n + w], refs[2 * n + w]
            acc = own_ref[...].astype(F32)
            for k in range(3):
                acc = acc + p_ref[k].astype(F32)
            o_ref[...] = acc

    blk = [(p.shape[1] // 2, p.shape[2]) for p in parts]
    return pl.pallas_call(
        body, name="grad_chip_sum",
        grid_spec=pltpu.PrefetchScalarGridSpec(
            num_scalar_prefetch=1, grid=(2,),
            in_specs=[pl.BlockSpec((None,) + b, lambda i, s: (s[0], i, 0)) for b in blk]
            + [pl.BlockSpec((3,) + b, lambda i, s: (0, i, 0)) for b in blk],
            out_specs=[pl.BlockSpec(b, lambda i, s: (2 * s[1] + i, 0)) for b in blk]),
        out_shape=[jax.ShapeDtypeStruct((2 * p.shape[1], p.shape[2]), F32) for p in parts],
        compiler_params=_cp("arbitrary"),
    )(jnp.stack([chip, core]), *parts, *arrived)


def _cast_shards(ws, chip):
    n = len(ws)

    def body(s_ref, *refs):
        for w in range(n):
            refs[n + w][...] = refs[w][...].astype(_MX)

    blk = [(a.shape[0] // 2, a.shape[1]) for a in ws]
    return pl.pallas_call(
        body, name="cast_shards",
        grid_spec=pltpu.PrefetchScalarGridSpec(
            num_scalar_prefetch=1, grid=(2,),
            in_specs=[pl.BlockSpec(b, lambda i, s: (i, 0)) for b in blk],
            out_specs=[pl.BlockSpec((None,) + b, lambda i, s: (s[0], i, 0)) for b in blk]),
        out_shape=[jax.ShapeDtypeStruct((4,) + a.shape, _MX) for a in ws],
        compiler_params=_cp("arbitrary"),
    )(chip.reshape(1), *ws)


def _position():
    x, y, c = lax.axis_index("x"), lax.axis_index("y"), lax.axis_index("c")
    return x, y, c


def _small_allgather(blk, name, exchange=(), share=()):
    m_per, n = blk.shape
    ne, ns = len(exchange), len(share)

    def body(*refs):
        x_ref = refs[0]
        out_ref = refs[1 + ne + ns]
        send_sems, recv_sems = refs[2 + 2 * ne + 2 * ns], refs[3 + 2 * ne + 2 * ns]
        extra = refs[4 + 2 * ne + 2 * ns:]
        x, y, c = _position()
        me, sibling = (x, y, c), (x, y, 1 - c)
        chips = [(1 - x, y), (x, 1 - y), (1 - x, 1 - y)]
        if ne:
            exch = _Exchange(refs[1:1 + ne], refs[2 + ne + ns:2 + 2 * ne + ns], extra[0], extra[1])
            exch.start()
        if ns:
            halves = _share_copies(refs[1 + ne:1 + ne + ns], refs[2 + 2 * ne + ns:2 + 2 * ne + 2 * ns],
                                   extra[2 * bool(ne)], extra[2 * bool(ne) + 1], [b.shape[0] for b in share])
            for cp in halves:
                cp.start()

        def rows(px, py, pc):
            return out_ref.at[pl.ds(pl.multiple_of((4 * px + 2 * py + pc) * m_per, 8), m_per), :]

        def copy(k, block, to, src=None):
            return pltpu.make_async_remote_copy(
                src_ref=rows(*block) if src is None else src, dst_ref=rows(*block),
                send_sem=send_sems.at[k], recv_sem=recv_sems.at[k], device_id=to, device_id_type=MESH)

        out_ref[pl.ds(pl.multiple_of((4 * x + 2 * y + c) * m_per, 8), m_per), :] = x_ref[...]
        first = [copy(0, me, sibling, src=x_ref)]
        first += [copy(1 + j, me, (*chip, c), src=x_ref) for j, chip in enumerate(chips)]
        for cp in first:
            cp.start()
        passed = [copy(4 + j, (*chip, c), sibling) for j, chip in enumerate(chips)]
        for j, chip in enumerate(chips):
            copy(1 + j, (*chip, c), me).wait_recv()
            passed[j].start()
        copy(0, sibling, me).wait_recv()
        for j, chip in enumerate(chips):
            copy(4 + j, (*chip, 1 - c), me).wait_recv()
        for cp in first + passed:
            cp.wait_send()
        if ne:
            exch.finish()
        if ns:
            for cp in halves:
                cp.wait()

    res = pl.pallas_call(
        body, name=name,
        out_shape=[jax.ShapeDtypeStruct((8 * m_per, n), blk.dtype)] + _Exchange.out_shapes(exchange)
        + [jax.ShapeDtypeStruct(b.shape, b.dtype) for b in share],
        in_specs=[_vspec()] + _any_specs(ne + ns), out_specs=[_vspec()] + _any_specs(ne + ns),
        input_output_aliases={1 + ne + w: 1 + ne + w for w in range(ns)},
        scratch_shapes=[pltpu.SemaphoreType.DMA((7,)), pltpu.SemaphoreType.DMA((7,))]
        + (_Exchange.semaphores(ne) if ne else [])
        + ([pltpu.SemaphoreType.DMA((ns,)), pltpu.SemaphoreType.DMA((ns,))] if ns else []),
    )(blk, *exchange, *share)
    return res[0], res[1:1 + ne], res[1 + ne:]


def _share_copies(src_refs, out_refs, send_sems, recv_sems, nrows):
    x, y, c = _position()
    return [pltpu.make_async_remote_copy(
        src_ref=_rows(src_refs[w], c, r), dst_ref=_rows(out_refs[w], c, r), send_sem=send_sems.at[w],
        recv_sem=recv_sems.at[w], device_id=(x, y, 1 - c), device_id_type=MESH) for w, r in enumerate(nrows)]


def _any_specs(n):
    return [pl.BlockSpec(memory_space=pl.ANY)] * n


def _rows(ref, half, nrows):
    return ref.at[pl.ds(half * (nrows // 2), nrows // 2), :]


def _weight_allgather(bufs):
    n = len(bufs)

    def body(*refs):
        gather = _Gather(refs[:n], refs[n:2 * n], refs[2 * n], refs[2 * n + 1], [b.shape[1] for b in bufs])
        gather.start()
        gather.forward()
        gather.finish()

    return pl.pallas_call(
        body, name="weight_allgather",
        out_shape=[jax.ShapeDtypeStruct(a.shape, a.dtype) for a in bufs],
        in_specs=_any_specs(n), out_specs=_any_specs(n), input_output_aliases={w: w for w in range(n)},
        scratch_shapes=_Gather.semaphores(n),
    )(*bufs)


class _Gather:
    def __init__(self, src_refs, out_refs, send_sems, recv_sems, nrows):
        x, y, c = _position()
        chip = 2 * x + y
        sibling = (x, y, 1 - c)

        def copy(k, src, dst, to):
            return pltpu.make_async_remote_copy(src_ref=src, dst_ref=dst, send_sem=send_sems.at[k],
                                                recv_sem=recv_sems.at[k], device_id=to, device_id_type=MESH)

        self.first, self.first_landed, self.passed, self.passed_landed = [], [], [], []
        for w, r in enumerate(nrows):
            for j, ch in enumerate([(1 - x, y), (x, 1 - y), (1 - x, 1 - y)]):
                theirs = _rows(out_refs[w].at[2 * ch[0] + ch[1]], c, r)
                other = _rows(out_refs[w].at[2 * ch[0] + ch[1]], 1 - c, r)
                self.first.append(copy(6 * w + j, _rows(src_refs[w].at[chip], c, r),
                                       _rows(out_refs[w].at[chip], c, r), (*ch, c)))
                self.first_landed.append(copy(6 * w + j, theirs, theirs, sibling))
                self.passed.append(copy(6 * w + 3 + j, theirs, theirs, sibling))
                self.passed_landed.append(copy(6 * w + 3 + j, other, other, sibling))

    @staticmethod
    def semaphores(n):
        return [pltpu.SemaphoreType.DMA((6 * n,)), pltpu.SemaphoreType.DMA((6 * n,))]

    def start(self):
        for cp in self.first:
            cp.start()

    def forward(self):
        for landed, cp in zip(self.first_landed, self.passed):
            landed.wait_recv()
            cp.start()

    def finish(self):
        for landed in self.passed_landed:
            landed.wait_recv()
        for cp in self.first + self.passed:
            cp.wait_send()


def _grad_swap(grads, name):
    n = len(grads)

    def body(*refs):
        g_refs, o_refs = refs[:n], refs[n:2 * n]
        send_sems, recv_sems = refs[2 * n:]
        x, y, c = _position()
        cps = []
        for w in range(n):
            h = grads[w].shape[1] // 2
            cps.append(pltpu.make_async_remote_copy(
                src_ref=g_refs[w].at[:, pl.ds((1 - c) * h, h), :], dst_ref=o_refs[w], send_sem=send_sems.at[w],
                recv_sem=recv_sems.at[w], device_id=(x, y, 1 - c), device_id_type=MESH))
        for cp in cps:
            cp.start()
        for cp in cps:
            cp.wait()

    return pl.pallas_call(
        body, name=name,
        out_shape=[jax.ShapeDtypeStruct((4, g.shape[1] // 2, g.shape[2]), g.dtype) for g in grads],
        in_specs=_any_specs(n), out_specs=_any_specs(n),
        scratch_shapes=[pltpu.SemaphoreType.DMA((n,)), pltpu.SemaphoreType.DMA((n,))],
    )(*grads)


class _Exchange:
    def __init__(self, part_refs, out_refs, send_sems, recv_sems):
        x, y, c = _position()
        self.copies = [
            pltpu.make_async_remote_copy(
                src_ref=part_refs[w].at[2 * ch[0] + ch[1]], dst_ref=out_refs[w].at[k], send_sem=send_sems.at[3 * w + k],
                recv_sem=recv_sems.at[3 * w + k], device_id=(*ch, c), device_id_type=MESH)
            for w in range(len(part_refs)) for k, ch in enumerate([(1 - x, y), (x, 1 - y), (1 - x, 1 - y)])]

    @staticmethod
    def semaphores(n):
        return [pltpu.SemaphoreType.DMA((3 * n,)), pltpu.SemaphoreType.DMA((3 * n,))]

    @staticmethod
    def out_shapes(parts):
        return [jax.ShapeDtypeStruct((3,) + p.shape[1:], p.dtype) for p in parts]

    def start(self):
        for cp in self.copies:
            cp.start()

    def finish(self):
        for cp in self.copies:
            cp.wait()


def _rope_tables(s):
    t = jnp.arange(s)
    inv = 10000.0 ** (-jnp.arange(0, 32, 2, dtype=F32) / 32.0)
    ang_r = (t // GRID_W).astype(F32)[:, None] * inv
    ang_c = (t % GRID_W).astype(F32)[:, None] * inv
    ang = jnp.concatenate([ang_r, ang_r, ang_c, ang_c] * 2, axis=-1)
    first = (jnp.arange(128) % 32) < 16
    sin = jnp.sin(ang)
    return jnp.cos(ang), jnp.where(first, -sin, 0.0), jnp.where(first, 0.0, sin)


def _local_step(x, ctx, tgt, mod, modc, nw1, win, qnw, knw, sink, cw, cb, cnw, cnb, wout4, nw2, wup, fcw, fcb, wdown4,
                core=None):
    s = x.shape[0]
    sh1, sc1, g1, sh2, sc2, g2 = [mod[:, D * k:D * (k + 1)] for k in range(6)]
    shc, scc = modc[:, :D], modc[:, D:2 * D]
    cos, sa, sb = _rope_tables(s)
    gi = jnp.arange(128) // 64
    gmat = (gi[:, None] == gi[None, :]).astype(_MX)
    qnw2, knw2 = jnp.tile(qnw, (1, 2)), jnp.tile(knw, (1, 2))
    cw32 = jnp.pad(cw, ((0, 1), (0, 0)))
    fcw2 = jnp.pad(fcw, ((0, 5), (0, 0))).reshape(8, 2, FH).transpose(1, 0, 2)
    fcb2 = fcb.reshape(2, 1, FH)
    winkv = win[:, 512:768]
    sinkv = sink.reshape(8)

    host = (lambda *bufs: ()) if core is None else (lambda *bufs: bufs)
    (proj, h, q0, q1, k, ksw, v, vsw, u0), got = _mix_in(x, nw1, sc1, sh1, win, qnw2, knw2, cos, sa, sb, gmat,
                                                        gather=host(wout4))
    wout = (got[0] if got else wout4).reshape(D, D)
    kvc, hc, kc, kcsw, vc, vcsw = _ctx_kv(ctx, nw1, scc, shc, winkv, knw2, gmat)
    padr = lambda a: jnp.pad(a, ((TQ, TQ), (0, 0)))
    kp, kswp, vp, vswp = padr(k), padr(ksw), padr(v), padr(vsw)
    o, got = _attn_fwd(q0, q1, kp, kswp, vp, vswp, kc, kcsw, vc, vcsw, sinkv, gather=host(wup))
    wup = got[0] if got else wup
    u1 = _conv31(u0, cw32, cb)
    x1, mix, cat = _mix_out(o, u1, cnw, cnb, wout, x, g1)
    (h2, up0), got = _ffn_up(x1, nw2, sc2, sh2, wup, gather=host(wdown4))
    wdown = (got[0] if got else wdown4).reshape(FH, D)
    act = _ffn_conv_act(up0, fcw2, fcb2)
    dy, dact, ddn, vec_dn = _ffn_down(act, wdown, x1, g2, tgt)
    loss = (0.5 / D) * jnp.sum(vec_dn[1])

    dup, gfc = _ffn_conv_bwd(up0, dact, fcw2, fcb2)
    g_wdown = _tn_matmul(act, ddn, FQ, D, "gw_down").reshape(4, FH // 4, D)
    g_wup = _tn_matmul(h2, dup, D, FQ, "gw_up", b_split=True, by_chip=True)
    dx1, dmix, vec_up = _ffn_up_bwd(dup, wup, x1, dy, mix, nw2, sc2, g1)
    g_wout = _tn_matmul(cat, dmix, D, D, "gw_out").reshape(4, D // 4, D)
    parts = ()
    if core is not None:
        late = [g_wout, g_wup, g_wdown]
        parts = _pair_sum(late, _grad_swap(late, "grad_swap_late"), core, "grad_pair_sum_late")
    do, du1, vec_ln = _mix_out_bwd(dmix, wout, u1, cnw, cnb)
    dga, dgb, gcw = _conv31_bwd(du1, u0, cw32, proj)
    (dq, dkp, dvp, dkc, dvc, dsink), arrived = _attn_bwd(q0, q1, kp, kswp, vp, vswp, kc, kcsw, vc, vcsw, sinkv, o, do,
                                                         exchange=parts)
    if core is not None:
        g_wout, g_wup, g_wdown = zip(parts, arrived)
    gwinkv, vec_ctx = _ctx_bwd(ctx, nw1, scc, winkv, kvc, hc, dkc, dvc, knw2, gmat)
    gx, dproj, vec_in = _mix_in_bwd(dq, dkp[TQ:TQ + s], dvp[TQ:TQ + s], dga, dgb, proj, x, dx1, win, nw1, sc1,
                                    qnw2, knw2, cos, sa, sb, gmat)
    g_win = _tn_matmul(h, dproj, D, INW, "gw_in", init=gwinkv)

    g_qn = vec_in[3:4, 0:128]
    g_kn = vec_in[3:4, 128:256] + vec_ctx[3:4, 0:128]
    grads = dict(
        norm_mix_w=vec_in[2:3] + vec_ctx[2:3], w_in=g_win,
        q_norm_w=g_qn[:, :64] + g_qn[:, 64:], k_norm_w=g_kn[:, :64] + g_kn[:, 64:],
        sink_logit=dsink[:, 0].reshape(1, 8), conv_w=gcw[:CK], conv_b=gcw[CK:CK + 1],
        conv_norm_w=vec_ln[1:2], conv_norm_b=vec_ln[0:1], w_out=g_wout, norm_ffn_w=vec_up[2:3], w_up=g_wup,
        ffn_conv_w=gfc[:, 0:3].transpose(1, 0, 2).reshape(3, 2 * FH), ffn_conv_b=gfc[:, 3].reshape(1, 2 * FH),
        w_down=g_wdown)
    dmod = jnp.concatenate([vec_in[0:1], vec_in[1:2], vec_up[3:4], vec_up[0:1], vec_up[1:2], vec_dn[0:1]], axis=1)
    dmodc = jnp.concatenate([vec_ctx[0:1], vec_ctx[1:2]], axis=1)
    return loss, gx, grads, dmod, dmodc


_SMALL = ["c_ctx", "b_mod", "norm_mix_w", "q_norm_w", "k_norm_w", "sink_logit", "conv_w", "conv_b", "conv_norm_w",
          "conv_norm_b", "norm_ffn_w", "ffn_conv_w", "ffn_conv_b"]
_GATHERED = ["w_in", "w_out", "w_up", "w_down"]
_BIG = ["w_mod"] + _GATHERED
_ORDER = ["c_ctx", "w_mod", "b_mod", "norm_mix_w", "w_in", "q_norm_w", "k_norm_w", "sink_logit", "conv_w", "conv_b",
          "conv_norm_w", "conv_norm_b", "w_out", "norm_ffn_w", "w_up", "ffn_conv_w", "ffn_conv_b", "w_down"]
_PACK = [("norm_mix_w", 1), ("norm_ffn_w", 1), ("q_norm_w", 1), ("k_norm_w", 1), ("sink_logit", 1), ("conv_b", 1),
         ("conv_norm_w", 1), ("conv_norm_b", 1), ("ffn_conv_b", 6), ("conv_w", 16), ("ffn_conv_w", 17)]
_PACK_ROWS = 56


def _pack_rows(a, nrows):
    flat = a.reshape(-1)
    return jnp.pad(flat, (0, nrows * D - flat.shape[0])).reshape(nrows, D)


def kernel(x, c, ctx, c_ctx, w_mod, b_mod, norm_mix_w, w_in, q_norm_w, k_norm_w, sink_logit, conv_w, conv_b, conv_norm_w, conv_norm_b, w_out, norm_ffn_w, w_up, ffn_conv_w, ffn_conv_b, w_down, loss_target, m_c_ctx, m_w_mod, m_b_mod, m_norm_mix_w, m_w_in, m_q_norm_w, m_k_norm_w, m_sink_logit, m_conv_w, m_conv_b, m_conv_norm_w, m_conv_norm_b, m_w_out, m_norm_ffn_w, m_w_up, m_ffn_conv_w, m_ffn_conv_b, m_w_down, v_c_ctx, v_w_mod, v_b_mod, v_norm_mix_w, v_w_in, v_q_norm_w, v_k_norm_w, v_sink_logit, v_conv_w, v_conv_b, v_conv_norm_w, v_conv_norm_b, v_w_out, v_norm_ffn_w, v_w_up, v_ffn_conv_w, v_ffn_conv_b, v_w_down):
    w = dict(c_ctx=c_ctx.reshape(1, D), w_mod=w_mod[0], b_mod=b_mod, norm_mix_w=norm_mix_w, w_in=w_in[0],
             q_norm_w=q_norm_w, k_norm_w=k_norm_w, sink_logit=sink_logit, conv_w=conv_w[0], conv_b=conv_b,
             conv_norm_w=conv_norm_w, conv_norm_b=conv_norm_b, w_out=w_out[0], norm_ffn_w=norm_ffn_w, w_up=w_up[0],
             ffn_conv_w=ffn_conv_w[0], ffn_conv_b=ffn_conv_b, w_down=w_down[0])
    m = dict(c_ctx=m_c_ctx.reshape(1, D), w_mod=m_w_mod[0], b_mod=m_b_mod, norm_mix_w=m_norm_mix_w, w_in=m_w_in[0],
             q_norm_w=m_q_norm_w, k_norm_w=m_k_norm_w, sink_logit=m_sink_logit, conv_w=m_conv_w[0], conv_b=m_conv_b,
             conv_norm_w=m_conv_norm_w, conv_norm_b=m_conv_norm_b, w_out=m_w_out[0], norm_ffn_w=m_norm_ffn_w,
             w_up=m_w_up[0], ffn_conv_w=m_ffn_conv_w[0], ffn_conv_b=m_ffn_conv_b, w_down=m_w_down[0])
    v = dict(c_ctx=v_c_ctx.reshape(1, D), w_mod=v_w_mod[0], b_mod=v_b_mod, norm_mix_w=v_norm_mix_w, w_in=v_w_in[0],
             q_norm_w=v_q_norm_w, k_norm_w=v_k_norm_w, sink_logit=v_sink_logit, conv_w=v_conv_w[0], conv_b=v_conv_b,
             conv_norm_w=v_conv_norm_w, conv_norm_b=v_conv_norm_b, w_out=v_w_out[0], norm_ffn_w=v_norm_ffn_w,
             w_up=v_w_up[0], ffn_conv_w=v_ffn_conv_w[0], ffn_conv_b=v_ffn_conv_b, w_down=v_w_down[0])
    xi, yi, ci = _position()
    chip = 2 * xi + yi
    dev = 2 * chip + ci
    s = x.shape[1]
    ncol = w["w_mod"].shape[1]

    blk0 = jnp.concatenate([_pack_rows(w["conv_w"], 4), _pack_rows(w["ffn_conv_w"], 5), c,
                            jnp.zeros((6, D), F32)], axis=0)
    g0 = _small_allgather(blk0, "gather_c_convw")[0].reshape(8, 16, D)
    c_all = g0[:, 9, :]
    cs = jnp.concatenate([c_all, w["c_ctx"], jnp.zeros((7, D), F32)], axis=0)
    cw_full = jnp.concatenate([g0[2 * j, 0:4].reshape(-1)[:CK * 128].reshape(CK, 128) for j in range(4)], axis=1)
    fcw_full = jnp.concatenate([g0[2 * j, 4:9].reshape(-1)[:3 * 1408].reshape(3, 1408) for j in range(4)], axis=1)

    b_loc = lax.dynamic_slice(w["b_mod"], (0, chip * ncol), (1, ncol))
    modp = _mod_fwd(cs, w["w_mod"], b_loc)
    gm = _small_allgather(modp, "gather_mod")[0].reshape(8, 16, ncol)
    mod_all = jnp.concatenate([gm[2 * j] for j in range(4)], axis=1)
    mod = lax.dynamic_slice(mod_all, (dev, 0), (1, 6 * D))
    modc = mod_all[8:9]

    win_buf, wout_buf, wup_buf, wdown_buf = _cast_shards([w[n] for n in _GATHERED], chip)
    win = _weight_allgather([win_buf])[0].transpose(1, 0, 2).reshape(D, INW)

    loss_loc, gx, gl, dmod, dmodc = _local_step(
        x[0], ctx[0], loss_target[0], mod, modc, w["norm_mix_w"], win, w["q_norm_w"], w["k_norm_w"], w["sink_logit"],
        cw_full, w["conv_b"], w["conv_norm_w"], w["conv_norm_b"], wout_buf, w["norm_ffn_w"], wup_buf, fcw_full,
        w["ffn_conv_b"], wdown_buf, core=ci)
    loss = lax.psum(loss_loc, ("x", "y", "c"))

    last = [gl["w_in"].reshape(D, 4, INW // 4).transpose(1, 0, 2)]
    parts = _pair_sum(last, _grad_swap(last, "grad_swap_in"), ci, "grad_pair_sum_in")

    pack = jnp.concatenate([dmod.reshape(6, D), dmodc.reshape(2, D)]
                           + [_pack_rows(gl[name], nr) for name, nr in _PACK], axis=0)
    pack = jnp.pad(pack, ((0, _PACK_ROWS - pack.shape[0]), (0, 0)))
    gp, arrived, _ = _small_allgather(pack, "gather_small_grads", exchange=parts)
    gp = gp.reshape(8, _PACK_ROWS, D)
    late = [gl[n] for n in ("w_out", "w_up", "w_down")]
    shard_grads = _chip_sum([*parts, *[p for p, _ in late]], [*arrived, *[a for _, a in late]], chip, ci)
    tot = _sum_leading(gp, "sum_small_grads")
    g = {}
    r = 8
    for name, nr in _PACK:
        shape = gl[name].shape
        g[name] = tot[r:r + nr].reshape(-1)[:math.prod(shape)].reshape(shape)
        r += nr
    dmod_all = jnp.concatenate([gp[:, 0:6].reshape(8, 6 * D),
                                jnp.pad(tot[6:8].reshape(1, 2 * D), ((0, 0), (0, 4 * D))),
                                jnp.zeros((7, 6 * D), F32)], axis=0)
    g["b_mod"] = _sum_leading(dmod_all.reshape(16, 1, 6 * D), "grad_b_mod")
    g["w_mod"], part = _mod_bwd(cs, lax.dynamic_slice(dmod_all, (0, chip * ncol), (16, ncol)), w["w_mod"])
    cparts, _, (g["w_in"], g["w_out"], g["w_up"], g["w_down"]) = _small_allgather(part, "gather_cctx",
                                                                                 share=shard_grads)
    g["c_ctx"] = _cctx_grad(cparts.reshape(8, 16, D)[0::2], w["c_ctx"])
    g["conv_w"] = lax.dynamic_slice(g["conv_w"], (0, chip * 128), (CK, 128))
    g["ffn_conv_w"] = lax.dynamic_slice(g["ffn_conv_w"], (0, chip * 1408), (3, 1408))

    delta, new_m, new_v = {}, {}, {}
    for name in _BIG:
        delta[name], new_m[name], new_v[name] = _adam_big(w[name], g[name], m[name], v[name], "adam_" + name)
    ds, ms, vs = _adam_small([w[n] for n in _SMALL], [g[n] for n in _SMALL], [m[n] for n in _SMALL],
                             [v[n] for n in _SMALL])
    for k, name in enumerate(_SMALL):
        delta[name], new_m[name], new_v[name] = ds[k], ms[k], vs[k]

    def shaped(d, name):
        a = d[name]
        if name == "c_ctx":
            return a.reshape(D)
        if name in ("w_mod", "w_in", "w_out", "w_up", "w_down", "conv_w", "ffn_conv_w"):
            return a[None]
        return a

    outs = [loss, gx[None]]
    for d in (g, delta, new_m, new_v):
        outs += [shaped(d, name) for name in _ORDER]
    return tuple(outs)
```

```python
import functools
import math

import jax
import jax.numpy as jnp
from jax import lax
from jax.experimental import pallas as pl
from jax.experimental.pallas import tpu as pltpu

F32 = jnp.float32
_MX = jnp.bfloat16
EPS = 1e-6
NEG = -1e30
D = 1024
AW = 512
CW = 512
INW = 1792
FH = 2816
LC = 256
CK = 31
GRID_W = 64
TM = 512
TQ = 128
NCH = FH // 128
FQ = 2 * FH // 4
VMEM_LIMIT = 56 * 1024 * 1024
MESH = pl.DeviceIdType.MESH

ADAM_LR, ADAM_B1, ADAM_B2, ADAM_EPS, ADAM_WD, ADAM_STEP = 0.001, 0.9, 0.999, 1e-08, 0.01, 10


def _cp(*sem):
    return pltpu.CompilerParams(dimension_semantics=sem, vmem_limit_bytes=VMEM_LIMIT)


def _vspec():
    return pl.BlockSpec(memory_space=pltpu.VMEM)


def _sig(z):
    return 1.0 / (1.0 + jnp.exp(-z))


def _dot(a, b):
    return jnp.dot(a, b, preferred_element_type=F32)


def _dot_nt(a, b):
    return lax.dot_general(a, b, (((1,), (1,)), ((), ())), preferred_element_type=F32)


def _dot_tn(a, b):
    return lax.dot_general(a, b, (((0,), (0,)), ((), ())), preferred_element_type=F32)


def _gsum(v, g):
    hi = v.astype(_MX)
    lo = (v - hi.astype(F32)).astype(_MX)
    return _dot(hi, g) + _dot(lo, g)


def _headnorm(ch, w, g):
    r = lax.rsqrt(_gsum(ch * ch, g) * (1.0 / 64.0) + EPS)
    return ch * r * w


def _headnorm_bwd(ch, dy, w, g):
    r = lax.rsqrt(_gsum(ch * ch, g) * (1.0 / 64.0) + EPS)
    hat = ch * r
    dhat = dy * w
    dch = r * (dhat - hat * (_gsum(dhat * hat, g) * (1.0 / 64.0)))
    return dch, dy * hat


def _rope(u, cos, sa, sb):
    return u * cos + pltpu.roll(u, 112, 1) * sa + pltpu.roll(u, 16, 1) * sb


def _rope_bwd(d, cos, sa, sb):
    return d * cos - pltpu.roll(d, 112, 1) * sa - pltpu.roll(d, 16, 1) * sb


def _rowsum(v):
    return jnp.sum(v, axis=0, keepdims=True)


def _call_hosting_gather(body, gather, step_of, nsteps, *, in_specs, out_specs, out_shape, scratch_shapes=(),
                         comm=None, **kw):
    comm = comm or _Gather
    ng, n_in, n_out = len(gather), len(in_specs), len(out_specs)
    if not ng:
        return pl.pallas_call(body, in_specs=in_specs, out_specs=out_specs, out_shape=out_shape,
                              scratch_shapes=list(scratch_shapes), **kw)

    def hosted(*refs):
        ins, outs = refs[:n_in], refs[n_in + ng:n_in + ng + n_out]
        rest = refs[n_in + 2 * ng + n_out:]
        copies = comm(refs[n_in:n_in + ng], refs[n_in + ng + n_out:n_in + 2 * ng + n_out], rest[0], rest[1],
                      [b.shape[1] for b in gather])
        step = step_of()
        pl.when(step == 0)(copies.start)
        body(*ins, *outs, *rest[2:])
        pl.when(step == max(nsteps - 3, 0))(copies.forward)
        pl.when(step == nsteps - 1)(copies.finish)

    return pl.pallas_call(
        hosted, in_specs=list(in_specs) + _any_specs(ng), out_specs=list(out_specs) + _any_specs(ng),
        out_shape=list(out_shape) + comm.results(gather),
        input_output_aliases={n_in + w: n_out + w for w in range(ng)} if comm.in_place else {},
        scratch_shapes=comm.semaphores(ng) + list(scratch_shapes), **kw)


def _mix_in(x, nw, sc, sh, win, qnw2, knw2, cos, sa, sb, gmat, gather=()):
    s = x.shape[0]

    def body(x_ref, nw_ref, sc_ref, sh_ref, win_ref, qnw_ref, knw_ref, cos_ref, sa_ref, sb_ref, g_ref,
             proj_ref, h_ref, q0_ref, q1_ref, k_ref, ksw_ref, v_ref, vsw_ref, u0_ref):
        xv = x_ref[...]
        r = lax.rsqrt(jnp.mean(xv * xv, axis=-1, keepdims=True) + EPS)
        h = xv * r * (nw_ref[...] * (1.0 + sc_ref[...])) + sh_ref[...]
        hb = h.astype(_MX)
        h_ref[...] = hb
        proj = _dot(hb, win_ref[...])
        proj_ref[...] = proj
        cs, sav, sbv, g = cos_ref[...], sa_ref[...], sb_ref[...], g_ref[...]
        lo = lax.broadcasted_iota(jnp.int32, (1, 128), 1) < 64
        for j in range(4):
            q = _rope(_headnorm(proj[:, 128 * j:128 * (j + 1)], qnw_ref[...], g), cs, sav, sbv) * 0.125
            q0_ref[:, 128 * j:128 * (j + 1)] = jnp.where(lo, q, 0.0).astype(_MX)
            q1_ref[:, 128 * j:128 * (j + 1)] = jnp.where(lo, 0.0, q).astype(_MX)
        k = _rope(_headnorm(proj[:, 512:640], knw_ref[...], g), cs, sav, sbv)
        k_ref[...] = k.astype(_MX)
        ksw_ref[...] = pltpu.roll(k, 64, 1).astype(_MX)
        v = proj[:, 640:768]
        v_ref[...] = v.astype(_MX)
        vsw_ref[...] = pltpu.roll(v, 64, 1).astype(_MX)
        u0_ref[...] = proj[:, 768:1280] * _sig(proj[:, 1280:1792])

    row = lambda n: pl.BlockSpec((1, n), lambda i: (0, 0))
    tile = lambda n: pl.BlockSpec((TM, n), lambda i: (i, 0))
    res = _call_hosting_gather(
        body, gather, lambda: pl.program_id(0), s // TM, name="mix_in", grid=(s // TM,),
        in_specs=[tile(D), row(D), row(D), row(D), pl.BlockSpec((D, INW), lambda i: (0, 0)), row(128), row(128),
                  tile(128), tile(128), tile(128), pl.BlockSpec((128, 128), lambda i: (0, 0))],
        out_specs=[tile(INW), tile(D), tile(AW), tile(AW), tile(128), tile(128), tile(128), tile(128), tile(CW)],
        out_shape=[jax.ShapeDtypeStruct((s, INW), F32), jax.ShapeDtypeStruct((s, D), _MX),
                   jax.ShapeDtypeStruct((s, AW), _MX), jax.ShapeDtypeStruct((s, AW), _MX),
                   jax.ShapeDtypeStruct((s, 128), _MX), jax.ShapeDtypeStruct((s, 128), _MX),
                   jax.ShapeDtypeStruct((s, 128), _MX), jax.ShapeDtypeStruct((s, 128), _MX),
                   jax.ShapeDtypeStruct((s, CW), F32)],
        compiler_params=_cp("arbitrary"),
    )(x, nw, sc, sh, win, qnw2, knw2, cos, sa, sb, gmat, *gather)
    return res[:9], res[9:]


def _ctx_kv(ctx, nw, scc, shc, winkv, knw2, gmat):
    def body(ctx_ref, nw_ref, sc_ref, sh_ref, w_ref, knw_ref, g_ref,
             kvc_ref, hc_ref, kc_ref, kcsw_ref, vc_ref, vcsw_ref):
        cv = ctx_ref[...]
        r = lax.rsqrt(jnp.mean(cv * cv, axis=-1, keepdims=True) + EPS)
        hc = (cv * r * (nw_ref[...] * (1.0 + sc_ref[...])) + sh_ref[...]).astype(_MX)
        hc_ref[...] = hc
        kvc = _dot(hc, w_ref[...])
        kvc_ref[...] = kvc
        kc = _headnorm(kvc[:, :128], knw_ref[...], g_ref[...])
        kc_ref[...] = kc.astype(_MX)
        kcsw_ref[...] = pltpu.roll(kc, 64, 1).astype(_MX)
        vc = kvc[:, 128:]
        vc_ref[...] = vc.astype(_MX)
        vcsw_ref[...] = pltpu.roll(vc, 64, 1).astype(_MX)

    return pl.pallas_call(
        body, name="ctx_kv",
        in_specs=[_vspec()] * 7, out_specs=[_vspec()] * 6,
        out_shape=[jax.ShapeDtypeStruct((LC, 256), F32), jax.ShapeDtypeStruct((LC, D), _MX)]
        + [jax.ShapeDtypeStruct((LC, 128), _MX)] * 4,
        compiler_params=pltpu.CompilerParams(vmem_limit_bytes=VMEM_LIMIT),
    )(ctx, nw, scc, shc, winkv, knw2, gmat)


def _attn_mask(i, s):
    r = lax.broadcasted_iota(jnp.int32, (2 * TQ, LC + 3 * TQ), 0) % TQ
    cidx = lax.broadcasted_iota(jnp.int32, (2 * TQ, LC + 3 * TQ), 1)
    qpos = i * TQ + r
    kpos = (i - 1) * TQ + (cidx - LC)
    near = (jnp.abs(qpos - kpos) <= 128) & (kpos >= 0) & (kpos < s)
    return (cidx < LC) | near


def _attn_bias(i, s):
    return jnp.where(_attn_mask(i, s), 0.0, NEG)


def _attn_exp(sc, bias, sinkv):
    sc = sc + bias
    m = jnp.maximum(jnp.max(sc, axis=-1, keepdims=True), sinkv)
    ex = jnp.exp(sc - m)
    es = jnp.exp(sinkv - m)
    return ex, es, 1.0 / (jnp.sum(ex, axis=-1, keepdims=True) + es)


def _sink_rows(sink_ref, g, e):
    return jnp.concatenate([jnp.full((TQ, 1), sink_ref[4 * g + e], F32),
                            jnp.full((TQ, 1), sink_ref[4 * g + 2 + e], F32)], axis=0)


def _attn_fwd(q0, q1, kp, kswp, vp, vswp, kc, kcsw, vc, vcsw, sink, gather=()):
    s = q0.shape[0]
    nb = s // TQ

    def body(q0_ref, q1_ref, kp_ref, kswp_ref, vp_ref, vswp_ref, kc_ref, kcsw_ref, vc_ref, vcsw_ref, sink_ref,
             o_ref):
        i = pl.program_id(0)
        st = pl.multiple_of(i * TQ, TQ)
        kall = (jnp.concatenate([kc_ref[...], kp_ref[pl.ds(st, 3 * TQ), :]], axis=0),
                jnp.concatenate([kcsw_ref[...], kswp_ref[pl.ds(st, 3 * TQ), :]], axis=0))
        vall = (jnp.concatenate([vc_ref[...], vp_ref[pl.ds(st, 3 * TQ), :]], axis=0),
                jnp.concatenate([vcsw_ref[...], vswp_ref[pl.ds(st, 3 * TQ), :]], axis=0))
        bias = _attn_bias(i, s)
        lo = lax.broadcasted_iota(jnp.int32, (1, 128), 1) < 64
        qrefs = (q0_ref, q1_ref)
        out = [jnp.zeros((TQ, 128), F32) for _ in range(4)]
        combos = [(g, e) for g in range(2) for e in range(2)]
        scores = [_dot_nt(jnp.concatenate([qrefs[e][:, 256 * g:256 * g + 128],
                                           qrefs[e][:, 256 * g + 128:256 * g + 256]], axis=0),
                          kall[0 if e == g else 1]) for g, e in combos]
        parts = []
        for (g, e), sc in zip(combos, scores):
            ex, _, inv = _attn_exp(sc, bias, _sink_rows(sink_ref, g, e))
            parts.append((ex.astype(_MX), inv))
        for (g, e), (ex, inv) in zip(combos, parts):
            me = lo if e == 0 else jnp.logical_not(lo)
            o2 = _dot(ex, vall[0 if e == g else 1]) * inv
            out[2 * g] = out[2 * g] + jnp.where(me, o2[:TQ], 0.0)
            out[2 * g + 1] = out[2 * g + 1] + jnp.where(me, o2[TQ:], 0.0)
        for j in range(4):
            o_ref[:, 128 * j:128 * (j + 1)] = out[j]

    full = lambda a: pl.BlockSpec(a.shape, lambda i: (0, 0))
    qs = pl.BlockSpec((TQ, AW), lambda i: (i, 0))
    res = _call_hosting_gather(
        body, gather, lambda: pl.program_id(0), nb, name="attn_fwd", grid=(nb,),
        in_specs=[qs, qs, full(kp), full(kp), full(kp), full(kp), full(kc), full(kc), full(kc), full(kc),
                  pl.BlockSpec(memory_space=pltpu.SMEM)],
        out_specs=[qs], out_shape=[jax.ShapeDtypeStruct((s, AW), F32)],
        compiler_params=_cp("arbitrary"),
    )(q0, q1, kp, kswp, vp, vswp, kc, kcsw, vc, vcsw, sink, *gather)
    return res[0], res[1:]


def _conv31(u0, cw32, cb):
    s = u0.shape[0]
    rch = 256

    def body(u_ref, w_ref, b_ref, o_ref, pad_ref):
        pad_ref[0:16, :] = jnp.zeros((16, 128), F32)
        pad_ref[s + 16:s + 32, :] = jnp.zeros((16, 128), F32)
        pad_ref[16:s + 16, :] = u_ref[...]
        for cidx in range(s // rch):
            base = cidx * rch
            acc = jnp.zeros((rch, 128), F32) + b_ref[...]
            for j in range(CK):
                acc = acc + w_ref[j:j + 1, :] * pad_ref[base + j + 1:base + j + 1 + rch, :]
            o_ref[base:base + rch, :] = acc

    return pl.pallas_call(
        body, name="conv31", grid=(CW // 128,),
        in_specs=[pl.BlockSpec((s, 128), lambda c: (0, c)), pl.BlockSpec((32, 128), lambda c: (0, c)),
                  pl.BlockSpec((1, 128), lambda c: (0, c))],
        out_specs=pl.BlockSpec((s, 128), lambda c: (0, c)),
        out_shape=jax.ShapeDtypeStruct((s, CW), F32),
        scratch_shapes=[pltpu.VMEM((s + 32, 128), F32)],
        compiler_params=_cp("arbitrary"),
    )(u0, cw32, cb)


def _ln_stats(u1):
    mu = jnp.mean(u1, axis=-1, keepdims=True)
    xc = u1 - mu
    rstd = lax.rsqrt(jnp.mean(xc * xc, axis=-1, keepdims=True) + EPS)
    return xc * rstd, rstd


def _mix_out(o, u1, cnw, cnb, wout, x, g1):
    s = x.shape[0]

    def body(o_ref, u1_ref, cnw_ref, cnb_ref, w_ref, x_ref, g1_ref, x1_ref, mix_ref, cat_ref):
        u2n, _ = _ln_stats(u1_ref[...])
        u2 = u2n * cnw_ref[...] + cnb_ref[...]
        u3 = u2 * _sig(u2)
        cat = jnp.concatenate([o_ref[...], u3], axis=1).astype(_MX)
        cat_ref[...] = cat
        mix = _dot(cat, w_ref[...])
        mix_ref[...] = mix
        x1_ref[...] = x_ref[...] + g1_ref[...] * mix

    row = lambda n: pl.BlockSpec((1, n), lambda i: (0, 0))
    tile = lambda n: pl.BlockSpec((TM, n), lambda i: (i, 0))
    return pl.pallas_call(
        body, name="mix_out", grid=(s // TM,),
        in_specs=[tile(AW), tile(CW), row(CW), row(CW), pl.BlockSpec((D, D), lambda i: (0, 0)), tile(D), row(D)],
        out_specs=[tile(D), tile(D), tile(D)],
        out_shape=[jax.ShapeDtypeStruct((s, D), F32), jax.ShapeDtypeStruct((s, D), F32),
                   jax.ShapeDtypeStruct((s, D), _MX)],
        compiler_params=_cp("arbitrary"),
    )(o, u1, cnw, cnb, wout, x, g1)


def _ffn_up(x1, nw, sc, sh, wup4, gather=()):
    s = x1.shape[0]
    tm = min(1024, s)

    def body(x_ref, nw_ref, sc_ref, sh_ref, w_ref, h2_ref, up_ref, h2s):
        @pl.when(pl.program_id(1) == 0)
        def _():
            xv = x_ref[...]
            r = lax.rsqrt(jnp.mean(xv * xv, axis=-1, keepdims=True) + EPS)
            h2 = (xv * r * (nw_ref[...] * (1.0 + sc_ref[...])) + sh_ref[...]).astype(_MX)
            h2s[...] = h2
            h2_ref[...] = h2
        up_ref[...] = _dot(h2s[...], w_ref[...])

    row = pl.BlockSpec((1, D), lambda i, j: (0, 0))
    res = _call_hosting_gather(
        body, gather, lambda: pl.program_id(0) * 4 + pl.program_id(1), 4 * (s // tm), name="ffn_up",
        grid=(s // tm, 4),
        in_specs=[pl.BlockSpec((tm, D), lambda i, j: (i, 0)), row, row, row,
                  pl.BlockSpec((None, D, FQ), lambda i, j: (j, 0, 0))],
        out_specs=[pl.BlockSpec((tm, D), lambda i, j: (i, 0)),
                   pl.BlockSpec((None, tm, FQ), lambda i, j: (j // 2, i, j % 2))],
        out_shape=[jax.ShapeDtypeStruct((s, D), _MX), jax.ShapeDtypeStruct((2, s, FH), F32)],
        scratch_shapes=[pltpu.VMEM((tm, D), _MX)],
        compiler_params=_cp("arbitrary", "arbitrary"),
    )(x1, nw, sc, sh, wup4, *gather)
    return res[:2], res[2:]


def _fill_pad8(pad_ref, val, s):
    pad_ref[0:8, :] = jnp.zeros((8, 128), F32)
    pad_ref[s + 8:s + 16, :] = jnp.zeros((8, 128), F32)
    pad_ref[8:s + 8, :] = val


def _conv3_at(pad_ref, w_ref, half, base, rch):
    return (w_ref[half, 0:1, :] * pad_ref[base + 7:base + 7 + rch, :]
            + w_ref[half, 1:2, :] * pad_ref[base + 8:base + 8 + rch, :]
            + w_ref[half, 2:3, :] * pad_ref[base + 9:base + 9 + rch, :])


def _ffn_conv_act(up0, fcw, fcb):
    s = up0.shape[1]
    rch = 256

    def body(up_ref, w_ref, b_ref, act_ref, padg, padv):
        _fill_pad8(padg, up_ref[0], s)
        _fill_pad8(padv, up_ref[1], s)
        for cidx in range(s // rch):
            base = cidx * rch
            gate = _conv3_at(padg, w_ref, 0, base, rch) + b_ref[0]
            val = _conv3_at(padv, w_ref, 1, base, rch) + b_ref[1]
            act_ref[base:base + rch, :] = (gate * _sig(gate) * val).astype(_MX)

    return pl.pallas_call(
        body, name="ffn_conv_act", grid=(NCH,),
        in_specs=[pl.BlockSpec((2, s, 128), lambda c: (0, 0, c)), pl.BlockSpec((2, 8, 128), lambda c: (0, 0, c)),
                  pl.BlockSpec((2, 1, 128), lambda c: (0, 0, c))],
        out_specs=pl.BlockSpec((s, 128), lambda c: (0, c)),
        out_shape=jax.ShapeDtypeStruct((s, FH), _MX),
        scratch_shapes=[pltpu.VMEM((s + 16, 128), F32)] * 2,
        compiler_params=_cp("arbitrary"),
    )(up0, fcw, fcb)


def _ffn_down(act, wdown, x1, g2, tgt):
    s = x1.shape[0]

    def body(act_ref, w_ref, x1_ref, g2_ref, tgt_ref, dy_ref, dact_ref, ddn_ref, vec_ref):
        @pl.when(pl.program_id(0) == 0)
        def _():
            vec_ref[...] = jnp.zeros((8, D), F32)
        dn = _dot(act_ref[...], w_ref[...])
        diff = x1_ref[...] + g2_ref[...] * dn - tgt_ref[...]
        dy = diff * (1.0 / D)
        dy_ref[...] = dy
        ddn = (dy * g2_ref[...]).astype(_MX)
        ddn_ref[...] = ddn
        dact_ref[...] = _dot_nt(ddn, w_ref[...])
        vec_ref[0:1, :] += _rowsum(dy * dn)
        vec_ref[1:2, :] += _rowsum(diff * diff)

    tile = lambda n: pl.BlockSpec((TM, n), lambda i: (i, 0))
    return pl.pallas_call(
        body, name="ffn_down", grid=(s // TM,),
        in_specs=[tile(FH), pl.BlockSpec((FH, D), lambda i: (0, 0)), tile(D), pl.BlockSpec((1, D), lambda i: (0, 0)),
                  tile(D)],
        out_specs=[tile(D), tile(FH), tile(D), pl.BlockSpec((8, D), lambda i: (0, 0))],
        out_shape=[jax.ShapeDtypeStruct((s, D), F32), jax.ShapeDtypeStruct((s, FH), F32),
                   jax.ShapeDtypeStruct((s, D), _MX), jax.ShapeDtypeStruct((8, D), F32)],
        compiler_params=_cp("arbitrary"),
    )(act, wdown, x1, g2, tgt)


def _ffn_conv_bwd(up0, dact, fcw, fcb):
    s = up0.shape[1]
    rch = 256

    def body(up_ref, da_ref, w_ref, b_ref, dup_ref, gw_ref, padg, padv, dpg, dpv):
        _fill_pad8(padg, up_ref[0], s)
        _fill_pad8(padv, up_ref[1], s)
        for p in (dpg, dpv):
            p[0:8, :] = jnp.zeros((8, 128), F32)
            p[s + 8:s + 16, :] = jnp.zeros((8, 128), F32)
        acc = [[jnp.zeros((1, 128), F32) for _ in range(4)] for _ in range(2)]
        for cidx in range(s // rch):
            base = cidx * rch
            gate = _conv3_at(padg, w_ref, 0, base, rch) + b_ref[0]
            val = _conv3_at(padv, w_ref, 1, base, rch) + b_ref[1]
            da = da_ref[base:base + rch, :]
            sg = _sig(gate)
            dgate = da * val * sg * (1.0 + gate * (1.0 - sg))
            dval = da * gate * sg
            dpg[base + 8:base + 8 + rch, :] = dgate
            dpv[base + 8:base + 8 + rch, :] = dval
            for half, (dd, pad) in enumerate(((dgate, padg), (dval, padv))):
                for j in range(3):
                    acc[half][j] = acc[half][j] + _rowsum(dd * pad[base + 7 + j:base + 7 + j + rch, :])
                acc[half][3] = acc[half][3] + _rowsum(dd)
        for half in range(2):
            gw_ref[half] = jnp.zeros((8, 128), F32)
            for j in range(4):
                gw_ref[half, j:j + 1, :] = acc[half][j]
        for cidx in range(s // rch):
            base = cidx * rch
            for half, dp in enumerate((dpg, dpv)):
                dup_ref[half, base:base + rch, :] = (
                    w_ref[half, 0:1, :] * dp[base + 9:base + 9 + rch, :]
                    + w_ref[half, 1:2, :] * dp[base + 8:base + 8 + rch, :]
                    + w_ref[half, 2:3, :] * dp[base + 7:base + 7 + rch, :]).astype(_MX)

    return pl.pallas_call(
        body, name="ffn_conv_bwd", grid=(NCH,),
        in_specs=[pl.BlockSpec((2, s, 128), lambda c: (0, 0, c)), pl.BlockSpec((s, 128), lambda c: (0, c)),
                  pl.BlockSpec((2, 8, 128), lambda c: (0, 0, c)), pl.BlockSpec((2, 1, 128), lambda c: (0, 0, c))],
        out_specs=[pl.BlockSpec((2, s, 128), lambda c: (0, 0, c)), pl.BlockSpec((2, 8, 128), lambda c: (0, 0, c))],
        out_shape=[jax.ShapeDtypeStruct((2, s, FH), _MX), jax.ShapeDtypeStruct((2, 8, FH), F32)],
        scratch_shapes=[pltpu.VMEM((s + 16, 128), F32)] * 4,
        compiler_params=_cp("arbitrary"),
    )(up0, dact, fcw, fcb)


def _tn_matmul(a, b, tm, tn, name, b_split=False, by_chip=False, init=None):
    s, m = a.shape
    n = 2 * b.shape[2] if b_split else b.shape[1]
    ts = min(1024, s)
    nsteps = s // ts
    npb = (n // 2) // tn if b_split else None

    def body(*refs):
        if init is None:
            a_ref, b_ref, o_ref, acc = refs
        else:
            a_ref, b_ref, i_ref, o_ref, acc = refs
        k = pl.program_id(2)

        @pl.when(k == 0)
        def _():
            acc[...] = jnp.zeros((tm, tn), F32)
            if init is not None:
                acc[:, 512:768] = i_ref[...]
        acc[...] += _dot_tn(a_ref[...], b_ref[...])

        @pl.when(k == nsteps - 1)
        def _():
            o_ref[...] = acc[...].astype(_MX)

    if b_split:
        bspec = pl.BlockSpec((None, ts, tn), lambda i, j, k: (j // npb, k, j % npb))
    else:
        bspec = pl.BlockSpec((ts, tn), lambda i, j, k: (k, j))
    in_specs = [pl.BlockSpec((ts, tm), lambda i, j, k: (k, i)), bspec]
    args = [a, b]
    if init is not None:
        in_specs.append(pl.BlockSpec((tm, 256), lambda i, j, k: (i, 0)))
        args.append(init)
    if by_chip:
        out_spec = pl.BlockSpec((None, tm, tn), lambda i, j, k: (j, i, 0))
        out_shape = jax.ShapeDtypeStruct((n // tn, m, tn), _MX)
    else:
        out_spec = pl.BlockSpec((tm, tn), lambda i, j, k: (i, j))
        out_shape = jax.ShapeDtypeStruct((m, n), _MX)
    return pl.pallas_call(
        body, name=name, grid=(m // tm, n // tn, nsteps),
        in_specs=in_specs, out_specs=out_spec, out_shape=out_shape,
        scratch_shapes=[pltpu.VMEM((tm, tn), F32)],
        compiler_params=_cp("arbitrary", "arbitrary", "arbitrary"),
    )(*args)


def _ffn_up_bwd(dup, wup4, x1, dy, mix, nw, sc, g1):
    s = x1.shape[0]
    nk = 4

    def body(dup_ref, w_ref, x1_ref, dy_ref, mix_ref, nw_ref, sc_ref, g1_ref, dx1_ref, dmix_ref, vec_ref, acc):
        i, k = pl.program_id(0), pl.program_id(1)

        @pl.when(k == 0)
        def _():
            acc[...] = jnp.zeros((TM, D), F32)

        @pl.when((k == 0) & (i == 0))
        def _():
            vec_ref[...] = jnp.zeros((8, D), F32)
        acc[...] += _dot_nt(dup_ref[...], w_ref[...])

        @pl.when(k == nk - 1)
        def _():
            dh = acc[...]
            xv = x1_ref[...]
            r = lax.rsqrt(jnp.mean(xv * xv, axis=-1, keepdims=True) + EPS)
            xn = xv * r
            nwv, scv = nw_ref[...], sc_ref[...]
            vec_ref[0:1, :] += _rowsum(dh)
            vec_ref[1:2, :] += _rowsum(dh * xn) * nwv
            vec_ref[2:3, :] += _rowsum(dh * xn) * (1.0 + scv)
            dxn = dh * (nwv * (1.0 + scv))
            dx1 = dy_ref[...] + r * (dxn - xn * jnp.mean(dxn * xn, axis=-1, keepdims=True))
            dx1_ref[...] = dx1
            vec_ref[3:4, :] += _rowsum(dx1 * mix_ref[...])
            dmix_ref[...] = (dx1 * g1_ref[...]).astype(_MX)

    tile = pl.BlockSpec((TM, D), lambda i, k: (i, 0))
    row = pl.BlockSpec((1, D), lambda i, k: (0, 0))
    return pl.pallas_call(
        body, name="ffn_up_bwd", grid=(s // TM, nk),
        in_specs=[pl.BlockSpec((None, TM, FQ), lambda i, k: (k // 2, i, k % 2)),
                  pl.BlockSpec((None, D, FQ), lambda i, k: (k, 0, 0)), tile, tile, tile, row, row, row],
        out_specs=[tile, tile, pl.BlockSpec((8, D), lambda i, k: (0, 0))],
        out_shape=[jax.ShapeDtypeStruct((s, D), F32), jax.ShapeDtypeStruct((s, D), _MX),
                   jax.ShapeDtypeStruct((8, D), F32)],
        scratch_shapes=[pltpu.VMEM((TM, D), F32)],
        compiler_params=_cp("arbitrary", "arbitrary"),
    )(dup, wup4, x1, dy, mix, nw, sc, g1)


def _mix_out_bwd(dmix, wout, u1, cnw, cnb, swap=()):
    s = u1.shape[0]

    def body(dm_ref, w_ref, u1_ref, cnw_ref, cnb_ref, do_ref, du1_ref, vec_ref):
        @pl.when(pl.program_id(0) == 0)
        def _():
            vec_ref[...] = jnp.zeros((8, CW), F32)
        dcat = _dot_nt(dm_ref[...], w_ref[...])
        do_ref[...] = dcat[:, :AW]
        du3 = dcat[:, AW:]
        u2n, rstd = _ln_stats(u1_ref[...])
        u2 = u2n * cnw_ref[...] + cnb_ref[...]
        sg = _sig(u2)
        du2 = du3 * sg * (1.0 + u2 * (1.0 - sg))
        vec_ref[0:1, :] += _rowsum(du2)
        vec_ref[1:2, :] += _rowsum(du2 * u2n)
        d2n = du2 * cnw_ref[...]
        du1_ref[...] = rstd * (d2n - jnp.mean(d2n, axis=-1, keepdims=True)
                               - u2n * jnp.mean(d2n * u2n, axis=-1, keepdims=True))

    row = lambda n: pl.BlockSpec((1, n), lambda i: (0, 0))
    tile = lambda n: pl.BlockSpec((TM, n), lambda i: (i, 0))
    res = _call_hosting_gather(
        body, swap, lambda: pl.program_id(0), s // TM, comm=_Swap, name="mix_out_bwd", grid=(s // TM,),
        in_specs=[tile(D), pl.BlockSpec((D, D), lambda i: (0, 0)), tile(CW), row(CW), row(CW)],
        out_specs=[tile(AW), tile(CW), pl.BlockSpec((8, CW), lambda i: (0, 0))],
        out_shape=[jax.ShapeDtypeStruct((s, AW), F32), jax.ShapeDtypeStruct((s, CW), F32),
                   jax.ShapeDtypeStruct((8, CW), F32)],
        compiler_params=_cp("arbitrary"),
    )(dmix, wout, u1, cnw, cnb, *swap)
    return res[:3], res[3:]


def _conv31_bwd(du1, u0, cw32, proj):
    s = u0.shape[0]
    rch = 256

    def body(d_ref, u_ref, w_ref, ga_ref, gb_ref, dga_ref, dgb_ref, gw_ref, padu, padd):
        for p, src in ((padu, u_ref), (padd, d_ref)):
            p[0:16, :] = jnp.zeros((16, 128), F32)
            p[s + 16:s + 32, :] = jnp.zeros((16, 128), F32)
            p[16:s + 16, :] = src[...]
        for j in range(CK):
            acc = jnp.zeros((1, 128), F32)
            for cidx in range(s // rch):
                base = cidx * rch
                acc = acc + _rowsum(d_ref[base:base + rch, :] * padu[base + j + 1:base + j + 1 + rch, :])
            gw_ref[j:j + 1, :] = acc
        gw_ref[CK:CK + 1, :] = _rowsum(d_ref[...])
        for cidx in range(s // rch):
            base = cidx * rch
            du0 = jnp.zeros((rch, 128), F32)
            for j in range(CK):
                du0 = du0 + w_ref[j:j + 1, :] * padd[base + 31 - j:base + 31 - j + rch, :]
            sg = _sig(gb_ref[base:base + rch, :])
            ga = ga_ref[base:base + rch, :]
            dga_ref[base:base + rch, :] = (du0 * sg).astype(_MX)
            dgb_ref[base:base + rch, :] = (du0 * ga * sg * (1.0 - sg)).astype(_MX)

    blk = lambda off: pl.BlockSpec((s, 128), lambda c: (0, c + off))
    return pl.pallas_call(
        body, name="conv31_bwd", grid=(CW // 128,),
        in_specs=[blk(0), blk(0), pl.BlockSpec((32, 128), lambda c: (0, c)), blk(6), blk(10)],
        out_specs=[blk(0), blk(0), pl.BlockSpec((32, 128), lambda c: (0, c))],
        out_shape=[jax.ShapeDtypeStruct((s, CW), _MX), jax.ShapeDtypeStruct((s, CW), _MX),
                   jax.ShapeDtypeStruct((32, CW), F32)],
        scratch_shapes=[pltpu.VMEM((s + 32, 128), F32)] * 2,
        compiler_params=_cp("arbitrary"),
    )(du1, u0, cw32, proj, proj)


def _attn_bwd(q0, q1, kp, kswp, vp, vswp, kc, kcsw, vc, vcsw, sink, o, do, exchange=()):
    s = q0.shape[0]
    nb = s // TQ
    ne = len(exchange)

    def body(*refs):
        (q0_ref, q1_ref, kp_ref, kswp_ref, vp_ref, vswp_ref, kc_ref, kcsw_ref, vc_ref, vcsw_ref, sink_ref,
         o_ref, do_ref) = refs[:13]
        dq_ref, dk_ref, dv_ref, dkc_ref, dvc_ref, dsink_ref = refs[13 + ne:19 + ne]
        i = pl.program_id(0)
        if ne:
            exch = _Exchange(refs[13:13 + ne], refs[19 + ne:19 + 2 * ne], refs[19 + 2 * ne], refs[20 + 2 * ne])
            pl.when(i == 0)(exch.start)

        @pl.when(i == 0)
        def _():
            dk_ref[...] = jnp.zeros((s + 2 * TQ, 128), F32)
            dv_ref[...] = jnp.zeros((s + 2 * TQ, 128), F32)
            dkc_ref[...] = jnp.zeros((LC, 128), F32)
            dvc_ref[...] = jnp.zeros((LC, 128), F32)
            dsink_ref[...] = jnp.zeros((8, 128), F32)
        st = pl.multiple_of(i * TQ, TQ)
        kall = (jnp.concatenate([kc_ref[...], kp_ref[pl.ds(st, 3 * TQ), :]], axis=0),
                jnp.concatenate([kcsw_ref[...], kswp_ref[pl.ds(st, 3 * TQ), :]], axis=0))
        vall = (jnp.concatenate([vc_ref[...], vp_ref[pl.ds(st, 3 * TQ), :]], axis=0),
                jnp.concatenate([vcsw_ref[...], vswp_ref[pl.ds(st, 3 * TQ), :]], axis=0))
        bias = _attn_bias(i, s)
        lo = lax.broadcasted_iota(jnp.int32, (1, 128), 1) < 64
        qrefs = (q0_ref, q1_ref)
        dq = [jnp.zeros((TQ, 128), F32) for _ in range(4)]
        dkt = jnp.zeros((128, LC + 3 * TQ), F32)
        dvt = jnp.zeros((128, LC + 3 * TQ), F32)
        combos = [(g, e) for g in range(2) for e in range(2)]

        def stacked(ref, g):
            return jnp.concatenate([ref[:, 256 * g:256 * g + 128], ref[:, 256 * g + 128:256 * g + 256]], axis=0)

        for g, e in combos:
            me = lo if e == 0 else jnp.logical_not(lo)
            sw = 0 if e == g else 1
            qm = stacked(qrefs[e], g)
            ex, es, inv = _attn_exp(_dot_nt(qm, kall[sw]), bias, _sink_rows(sink_ref, g, e))
            p = ex * inv
            dom = jnp.where(me, stacked(do_ref, g), 0.0)
            dd = jnp.sum(dom * stacked(o_ref, g), axis=-1, keepdims=True)
            domx = dom.astype(_MX)
            sd = es * inv * dd
            dsink_ref[4 * g + e:4 * g + e + 1, :] -= jnp.sum(sd[:TQ], axis=0, keepdims=True)
            dsink_ref[4 * g + 2 + e:4 * g + 3 + e, :] -= jnp.sum(sd[TQ:], axis=0, keepdims=True)
            ds = (p * (_dot_nt(domx, vall[sw]) - dd)).astype(_MX)
            dq2 = _dot(ds, kall[sw])
            dq[2 * g] = dq[2 * g] + jnp.where(me, dq2[:TQ], 0.0)
            dq[2 * g + 1] = dq[2 * g + 1] + jnp.where(me, dq2[TQ:], 0.0)
            dk2 = _dot_tn(qm, ds)
            dv2 = _dot_tn(domx, p.astype(_MX))
            if sw:
                dk2 = pltpu.roll(dk2, 64, 0)
                dv2 = pltpu.roll(dv2, 64, 0)
            dkt = dkt + dk2
            dvt = dvt + dv2
        dkt = dkt.T
        dvt = dvt.T
        for j in range(4):
            dq_ref[:, 128 * j:128 * (j + 1)] = dq[j] * 0.125
        dkc_ref[...] += dkt[:LC]
        dvc_ref[...] += dvt[:LC]
        dk_ref[pl.ds(st, 3 * TQ), :] += dkt[LC:]
        dv_ref[pl.ds(st, 3 * TQ), :] += dvt[LC:]
        if ne:
            pl.when(i == nb - 1)(exch.finish)

    full = lambda a: pl.BlockSpec(a.shape, lambda i: (0, 0))
    fs = lambda r: pl.BlockSpec((r, 128), lambda i: (0, 0))
    qs = pl.BlockSpec((TQ, AW), lambda i: (i, 0))
    res = pl.pallas_call(
        body, name="attn_bwd", grid=(nb,),
        in_specs=[qs, qs, full(kp), full(kp), full(kp), full(kp), full(kc), full(kc), full(kc), full(kc),
                  pl.BlockSpec(memory_space=pltpu.SMEM), qs, qs] + _any_specs(ne),
        out_specs=[qs, fs(s + 2 * TQ), fs(s + 2 * TQ), fs(LC), fs(LC), fs(8)] + _any_specs(ne),
        out_shape=[jax.ShapeDtypeStruct((s, AW), F32), jax.ShapeDtypeStruct((s + 2 * TQ, 128), F32),
                   jax.ShapeDtypeStruct((s + 2 * TQ, 128), F32), jax.ShapeDtypeStruct((LC, 128), F32),
                   jax.ShapeDtypeStruct((LC, 128), F32), jax.ShapeDtypeStruct((8, 128), F32)]
        + _Exchange.out_shapes(exchange),
        scratch_shapes=_Exchange.semaphores(ne) if ne else [],
        compiler_params=_cp("arbitrary"),
    )(q0, q1, kp, kswp, vp, vswp, kc, kcsw, vc, vcsw, sink, o, do, *exchange)
    return res[:6], res[6:]


def _mix_in_bwd(dq, dk, dv, dga, dgb, proj, x, dx1, win, nw, sc, qnw2, knw2, cos, sa, sb, gmat):
    s = x.shape[0]

    def body(dq_ref, dk_ref, dv_ref, dga_ref, dgb_ref, proj_ref, x_ref, dx1_ref, win_ref, nw_ref, sc_ref,
             qnw_ref, knw_ref, cos_ref, sa_ref, sb_ref, g_ref, gx_ref, dproj_ref, vec_ref):
        @pl.when(pl.program_id(0) == 0)
        def _():
            vec_ref[...] = jnp.zeros((8, D), F32)
        cs, sav, sbv, g = cos_ref[...], sa_ref[...], sb_ref[...], g_ref[...]
        gq = jnp.zeros((1, 128), F32)
        for j in range(4):
            dqn = _rope_bwd(dq_ref[:, 128 * j:128 * (j + 1)], cs, sav, sbv)
            dch, gw = _headnorm_bwd(proj_ref[:, 128 * j:128 * (j + 1)], dqn, qnw_ref[...], g)
            dproj_ref[:, 128 * j:128 * (j + 1)] = dch.astype(_MX)
            gq = gq + _rowsum(gw)
        dkn = _rope_bwd(dk_ref[...], cs, sav, sbv)
        dch, gw = _headnorm_bwd(proj_ref[:, 512:640], dkn, knw_ref[...], g)
        dproj_ref[:, 512:640] = dch.astype(_MX)
        dproj_ref[:, 640:768] = dv_ref[...].astype(_MX)
        dproj_ref[:, 768:1280] = dga_ref[...]
        dproj_ref[:, 1280:1792] = dgb_ref[...]
        vec_ref[3:4, 0:128] += gq
        vec_ref[3:4, 128:256] += _rowsum(gw)
        dh = _dot_nt(dproj_ref[...], win_ref[...])
        xv = x_ref[...]
        r = lax.rsqrt(jnp.mean(xv * xv, axis=-1, keepdims=True) + EPS)
        xn = xv * r
        nwv, scv = nw_ref[...], sc_ref[...]
        vec_ref[0:1, :] += _rowsum(dh)
        vec_ref[1:2, :] += _rowsum(dh * xn) * nwv
        vec_ref[2:3, :] += _rowsum(dh * xn) * (1.0 + scv)
        dxn = dh * (nwv * (1.0 + scv))
        gx_ref[...] = dx1_ref[...] + r * (dxn - xn * jnp.mean(dxn * xn, axis=-1, keepdims=True))

    row = lambda n: pl.BlockSpec((1, n), lambda i: (0, 0))
    tile = lambda n: pl.BlockSpec((TM, n), lambda i: (i, 0))
    return pl.pallas_call(
        body, name="mix_in_bwd", grid=(s // TM,),
        in_specs=[tile(AW), tile(128), tile(128), tile(CW), tile(CW), tile(INW), tile(D), tile(D),
                  pl.BlockSpec((D, INW), lambda i: (0, 0)), row(D), row(D), row(128), row(128),
                  tile(128), tile(128), tile(128), pl.BlockSpec((128, 128), lambda i: (0, 0))],
        out_specs=[tile(D), tile(INW), pl.BlockSpec((8, D), lambda i: (0, 0))],
        out_shape=[jax.ShapeDtypeStruct((s, D), F32), jax.ShapeDtypeStruct((s, INW), _MX),
                   jax.ShapeDtypeStruct((8, D), F32)],
        compiler_params=_cp("arbitrary"),
    )(dq, dk, dv, dga, dgb, proj, x, dx1, win, nw, sc, qnw2, knw2, cos, sa, sb, gmat)


def _ctx_bwd(ctx, nw, scc, winkv, kvc, hc, dkc, dvc, knw2, gmat):
    def body(ctx_ref, nw_ref, sc_ref, w_ref, kvc_ref, hc_ref, dkc_ref, dvc_ref, knw_ref, g_ref, gw_ref, vec_ref):
        dkr, gk = _headnorm_bwd(kvc_ref[:, 0:128], dkc_ref[...], knw_ref[...], g_ref[...])
        dkv = jnp.concatenate([dkr, dvc_ref[...]], axis=1).astype(_MX)
        gw_ref[...] = _dot_tn(hc_ref[...], dkv)
        dh = _dot_nt(dkv, w_ref[...])
        cv = ctx_ref[...]
        r = lax.rsqrt(jnp.mean(cv * cv, axis=-1, keepdims=True) + EPS)
        cn = cv * r
        vec_ref[...] = jnp.zeros((8, D), F32)
        vec_ref[0:1, :] = _rowsum(dh)
        vec_ref[1:2, :] = _rowsum(dh * cn) * nw_ref[...]
        vec_ref[2:3, :] = _rowsum(dh * cn) * (1.0 + sc_ref[...])
        vec_ref[3:4, 0:128] = _rowsum(gk)

    return pl.pallas_call(
        body, name="ctx_bwd", in_specs=[_vspec()] * 10, out_specs=[_vspec()] * 2,
        out_shape=[jax.ShapeDtypeStruct((D, 256), F32), jax.ShapeDtypeStruct((8, D), F32)],
        compiler_params=pltpu.CompilerParams(vmem_limit_bytes=VMEM_LIMIT),
    )(ctx, nw, scc, winkv, kvc, hc, dkc, dvc, knw2, gmat)


def _mod_fwd(cs, wmod, bloc):
    def body(c_ref, w_ref, b_ref, o_ref):
        cv = c_ref[...]
        o_ref[...] = _dot((cv * _sig(cv)).astype(_MX), w_ref[...].astype(_MX)) + b_ref[...]

    return pl.pallas_call(
        body, name="mod_fwd", in_specs=[_vspec()] * 3, out_specs=_vspec(),
        out_shape=jax.ShapeDtypeStruct((16, wmod.shape[1]), F32),
        compiler_params=pltpu.CompilerParams(vmem_limit_bytes=VMEM_LIMIT),
    )(cs, wmod, bloc)


def _mod_bwd(cs, dmod, wmod):
    def body(c_ref, d_ref, w_ref, gw_ref, part_ref):
        cv = c_ref[...]
        sl = (cv * _sig(cv)).astype(_MX)
        dm = d_ref[...].astype(_MX)
        gw_ref[...] = _dot_tn(sl, dm)
        part_ref[...] = _dot_nt(dm, w_ref[...].astype(_MX))

    return pl.pallas_call(
        body, name="mod_bwd", in_specs=[_vspec()] * 3, out_specs=[_vspec()] * 2,
        out_shape=[jax.ShapeDtypeStruct(wmod.shape, F32), jax.ShapeDtypeStruct((16, D), F32)],
        compiler_params=pltpu.CompilerParams(vmem_limit_bytes=VMEM_LIMIT),
    )(cs, dmod, wmod)


def _sum_leading(a, name):
    n = a.shape[0]

    def body(a_ref, o_ref):
        acc = a_ref[0]
        for k in range(1, n):
            acc = acc + a_ref[k]
        o_ref[...] = acc

    return pl.pallas_call(
        body, name=name, in_specs=[_vspec()], out_specs=_vspec(),
        out_shape=jax.ShapeDtypeStruct(a.shape[1:], F32),
        compiler_params=pltpu.CompilerParams(vmem_limit_bytes=VMEM_LIMIT),
    )(a)


def _cctx_grad(parts, cc):
    def body(p_ref, c_ref, o_ref):
        acc = p_ref[0, 8:9, :]
        for k in range(1, 4):
            acc = acc + p_ref[k, 8:9, :]
        cv = c_ref[...]
        sg = _sig(cv)
        o_ref[...] = acc * (sg * (1.0 + cv * (1.0 - sg)))

    return pl.pallas_call(
        body, name="cctx_grad", in_specs=[_vspec()] * 2, out_specs=_vspec(),
        out_shape=jax.ShapeDtypeStruct((1, D), F32),
    )(parts, cc)


def _adam_math(w, g, m, v):
    mn = ADAM_B1 * m + (1.0 - ADAM_B1) * g
    vn = ADAM_B2 * v + (1.0 - ADAM_B2) * (g * g)
    mh = mn / (1.0 - ADAM_B1 ** ADAM_STEP)
    vh = vn / (1.0 - ADAM_B2 ** ADAM_STEP)
    delta = -ADAM_LR * (mh / (jnp.sqrt(vh) + ADAM_EPS) + ADAM_WD * w)
    return delta, mn, vn


def _adam_big(w, g, m, v, name):
    r, n = w.shape
    tr = 256 if r % 256 == 0 else 64

    def body(w_ref, g_ref, m_ref, v_ref, d_ref, mo_ref, vo_ref):
        d, mn, vn = _adam_math(w_ref[...], g_ref[...], m_ref[...], v_ref[...])
        d_ref[...] = d
        mo_ref[...] = mn
        vo_ref[...] = vn

    spec = pl.BlockSpec((tr, n), lambda i: (i, 0))
    return pl.pallas_call(
        body, name=name, grid=(r // tr,), in_specs=[spec] * 4, out_specs=[spec] * 3,
        out_shape=[jax.ShapeDtypeStruct((r, n), F32)] * 3,
        compiler_params=_cp("arbitrary"),
    )(w, g, m, v)


def _adam_small(ws, gs, ms, vs):
    n = len(ws)

    def body(*refs):
        ins, outs = refs[:4 * n], refs[4 * n:]
        for k in range(n):
            d, mn, vn = _adam_math(ins[k][...], ins[n + k][...], ins[2 * n + k][...], ins[3 * n + k][...])
            outs[k][...] = d
            outs[n + k][...] = mn
            outs[2 * n + k][...] = vn

    shapes = [jax.ShapeDtypeStruct(w.shape, F32) for w in ws]
    res = pl.pallas_call(
        body, name="adam_small", in_specs=[_vspec()] * (4 * n), out_specs=[_vspec()] * (3 * n),
        out_shape=shapes * 3,
    )(*ws, *gs, *ms, *vs)
    return res[:n], res[n:2 * n], res[2 * n:]


def _pair_sum(grads, from_sib, core, name):
    n = len(grads)

    def body(c_ref, *refs):
        for w in range(n):
            a_ref, b_ref, o_ref = refs[w], refs[n + w], refs[2 * n + w]
            o_ref[...] = (a_ref[...].astype(F32) + b_ref[...].astype(F32)).astype(_MX)

    halves = [(None, g.shape[1] // 2, g.shape[2]) for g in grads]
    return pl.pallas_call(
        body, name=name,
        grid_spec=pltpu.PrefetchScalarGridSpec(
            num_scalar_prefetch=1, grid=(4,),
            in_specs=[pl.BlockSpec(h, lambda j, c: (j, c[0], 0)) for h in halves]
            + [pl.BlockSpec(h, lambda j, c: (j, 0, 0)) for h in halves],
            out_specs=[pl.BlockSpec(h, lambda j, c: (j, 0, 0)) for h in halves]),
        out_shape=[jax.ShapeDtypeStruct(p.shape, _MX) for p in from_sib],
        compiler_params=_cp("arbitrary"),
    )(core.reshape(1), *grads, *from_sib)


def _chip_sum(parts, arrived, chip, core):
    n = len(parts)

    def body(s_ref, *refs):
        for w in range(n):
            own_ref, p_ref, o_ref = refs[w], refs[n + w], refs[2 * n + w]
            acc = own_ref[...].astype(F32)
            for k in range(3):
                acc = acc + p_ref[k].astype(F32)
            o_ref[...] = acc

    blk = [(p.shape[1] // 2, p.shape[2]) for p in parts]
    return pl.pallas_call(
        body, name="grad_chip_sum",
        grid_spec=pltpu.PrefetchScalarGridSpec(
            num_scalar_prefetch=1, grid=(2,),
            in_specs=[pl.BlockSpec((None,) + b, lambda i, s: (s[0], i, 0)) for b in blk]
            + [pl.BlockSpec((3,) + b, lambda i, s: (0, i, 0)) for b in blk],
            out_specs=[pl.BlockSpec(b, lambda i, s: (2 * s[1] + i, 0)) for b in blk]),
        out_shape=[jax.ShapeDtypeStruct((2 * p.shape[1], p.shape[2]), F32) for p in parts],
        compiler_params=_cp("arbitrary"),
    )(jnp.stack([chip, core]), *parts, *arrived)


def _cast_shards(ws, chip):
    n = len(ws)

    def body(s_ref, *refs):
        for w in range(n):
            refs[n + w][...] = refs[w][...].astype(_MX)

    blk = [(a.shape[0] // 2, a.shape[1]) for a in ws]
    return pl.pallas_call(
        body, name="cast_shards",
        grid_spec=pltpu.PrefetchScalarGridSpec(
            num_scalar_prefetch=1, grid=(2,),
            in_specs=[pl.BlockSpec(b, lambda i, s: (i, 0)) for b in blk],
            out_specs=[pl.BlockSpec((None,) + b, lambda i, s: (s[0], i, 0)) for b in blk]),
        out_shape=[jax.ShapeDtypeStruct((4,) + a.shape, _MX) for a in ws],
        compiler_params=_cp("arbitrary"),
    )(chip.reshape(1), *ws)


def _position():
    x, y, c = lax.axis_index("x"), lax.axis_index("y"), lax.axis_index("c")
    return x, y, c


def _small_allgather(blk, name, exchange=(), share=(), gather=()):
    m_per, n = blk.shape
    assert not (share and gather)
    share = tuple(share) + tuple(gather)
    ne, ns = len(exchange), len(share)

    def body(*refs):
        x_ref = refs[0]
        out_ref = refs[1 + ne + ns]
        send_sems, recv_sems = refs[2 + 2 * ne + 2 * ns], refs[3 + 2 * ne + 2 * ns]
        extra = refs[4 + 2 * ne + 2 * ns:]
        x, y, c = _position()
        me, sibling = (x, y, c), (x, y, 1 - c)
        chips = [(1 - x, y), (x, 1 - y), (1 - x, 1 - y)]
        if ne:
            exch = _Exchange(refs[1:1 + ne], refs[2 + ne + ns:2 + 2 * ne + ns], extra[0], extra[1])
            exch.start()
        if gather:
            gat = _Gather(refs[1 + ne:1 + ne + ns], refs[2 + 2 * ne + ns:2 + 2 * ne + 2 * ns],
                          extra[2 * bool(ne)], extra[2 * bool(ne) + 1], [b.shape[1] for b in gather])
            gat.start()
        elif ns:
            halves = _share_copies(refs[1 + ne:1 + ne + ns], refs[2 + 2 * ne + ns:2 + 2 * ne + 2 * ns],
                                   extra[2 * bool(ne)], extra[2 * bool(ne) + 1], [b.shape[0] for b in share])
            for cp in halves:
                cp.start()

        def rows(px, py, pc):
            return out_ref.at[pl.ds(pl.multiple_of((4 * px + 2 * py + pc) * m_per, 8), m_per), :]

        def copy(k, block, to, src=None):
            return pltpu.make_async_remote_copy(
                src_ref=rows(*block) if src is None else src, dst_ref=rows(*block),
                send_sem=send_sems.at[k], recv_sem=recv_sems.at[k], device_id=to, device_id_type=MESH)

        out_ref[pl.ds(pl.multiple_of((4 * x + 2 * y + c) * m_per, 8), m_per), :] = x_ref[...]
        first = [copy(0, me, sibling, src=x_ref)]
        first += [copy(1 + j, me, (*chip, c), src=x_ref) for j, chip in enumerate(chips)]
        for cp in first:
            cp.start()
        passed = [copy(4 + j, (*chip, c), sibling) for j, chip in enumerate(chips)]
        for j, chip in enumerate(chips):
            copy(1 + j, (*chip, c), me).wait_recv()
            passed[j].start()
        copy(0, sibling, me).wait_recv()
        for j, chip in enumerate(chips):
            copy(4 + j, (*chip, 1 - c), me).wait_recv()
        for cp in first + passed:
            cp.wait_send()
        if ne:
            exch.finish()
        if gather:
            gat.forward()
            gat.finish()
        elif ns:
            for cp in halves:
                cp.wait()

    res = pl.pallas_call(
        body, name=name,
        out_shape=[jax.ShapeDtypeStruct((8 * m_per, n), blk.dtype)] + _Exchange.out_shapes(exchange)
        + [jax.ShapeDtypeStruct(b.shape, b.dtype) for b in share],
        in_specs=[_vspec()] + _any_specs(ne + ns), out_specs=[_vspec()] + _any_specs(ne + ns),
        input_output_aliases={1 + ne + w: 1 + ne + w for w in range(ns)},
        scratch_shapes=[pltpu.SemaphoreType.DMA((7,)), pltpu.SemaphoreType.DMA((7,))]
        + (_Exchange.semaphores(ne) if ne else [])
        + (_Gather.semaphores(ns) if gather else
           [pltpu.SemaphoreType.DMA((ns,)), pltpu.SemaphoreType.DMA((ns,))] if ns else []),
    )(blk, *exchange, *share)
    return res[0], res[1:1 + ne], res[1 + ne:]


def _share_copies(src_refs, out_refs, send_sems, recv_sems, nrows):
    x, y, c = _position()
    return [pltpu.make_async_remote_copy(
        src_ref=_rows(src_refs[w], c, r), dst_ref=_rows(out_refs[w], c, r), send_sem=send_sems.at[w],
        recv_sem=recv_sems.at[w], device_id=(x, y, 1 - c), device_id_type=MESH) for w, r in enumerate(nrows)]


def _any_specs(n):
    return [pl.BlockSpec(memory_space=pl.ANY)] * n


def _rows(ref, half, nrows):
    return ref.at[pl.ds(half * (nrows // 2), nrows // 2), :]


class _Gather:
    def __init__(self, src_refs, out_refs, send_sems, recv_sems, nrows):
        x, y, c = _position()
        chip = 2 * x + y
        sibling = (x, y, 1 - c)

        def copy(k, src, dst, to):
            return pltpu.make_async_remote_copy(src_ref=src, dst_ref=dst, send_sem=send_sems.at[k],
                                                recv_sem=recv_sems.at[k], device_id=to, device_id_type=MESH)

        self.first, self.first_landed, self.passed, self.passed_landed = [], [], [], []
        for w, r in enumerate(nrows):
            for j, ch in enumerate([(1 - x, y), (x, 1 - y), (1 - x, 1 - y)]):
                theirs = _rows(out_refs[w].at[2 * ch[0] + ch[1]], c, r)
                other = _rows(out_refs[w].at[2 * ch[0] + ch[1]], 1 - c, r)
                self.first.append(copy(6 * w + j, _rows(src_refs[w].at[chip], c, r),
                                       _rows(out_refs[w].at[chip], c, r), (*ch, c)))
                self.first_landed.append(copy(6 * w + j, theirs, theirs, sibling))
                self.passed.append(copy(6 * w + 3 + j, theirs, theirs, sibling))
                self.passed_landed.append(copy(6 * w + 3 + j, other, other, sibling))

    in_place = True

    @staticmethod
    def semaphores(n):
        return [pltpu.SemaphoreType.DMA((6 * n,)), pltpu.SemaphoreType.DMA((6 * n,))]

    @staticmethod
    def results(bufs):
        return [jax.ShapeDtypeStruct(b.shape, b.dtype) for b in bufs]

    def start(self):
        for cp in self.first:
            cp.start()

    def forward(self):
        for landed, cp in zip(self.first_landed, self.passed):
            landed.wait_recv()
            cp.start()

    def finish(self):
        for landed in self.passed_landed:
            landed.wait_recv()
        for cp in self.first + self.passed:
            cp.wait_send()


class _Swap:
    in_place = False

    def __init__(self, grad_refs, out_refs, send_sems, recv_sems, nrows):
        x, y, c = _position()
        self.copies = [pltpu.make_async_remote_copy(
            src_ref=grad_refs[w].at[:, pl.ds((1 - c) * (r // 2), r // 2), :], dst_ref=out_refs[w],
            send_sem=send_sems.at[w], recv_sem=recv_sems.at[w], device_id=(x, y, 1 - c), device_id_type=MESH)
            for w, r in enumerate(nrows)]

    @staticmethod
    def semaphores(n):
        return [pltpu.SemaphoreType.DMA((n,)), pltpu.SemaphoreType.DMA((n,))]

    @staticmethod
    def results(grads):
        return [jax.ShapeDtypeStruct((4, g.shape[1] // 2, g.shape[2]), g.dtype) for g in grads]

    def start(self):
        for cp in self.copies:
            cp.start()

    def forward(self):
        pass

    def finish(self):
        for cp in self.copies:
            cp.wait()


def _grad_swap(grads, name):
    n = len(grads)

    def body(*refs):
        swap = _Swap(refs[:n], refs[n:2 * n], refs[2 * n], refs[2 * n + 1], [g.shape[1] for g in grads])
        swap.start()
        swap.finish()

    return pl.pallas_call(
        body, name=name, out_shape=_Swap.results(grads), in_specs=_any_specs(n), out_specs=_any_specs(n),
        scratch_shapes=_Swap.semaphores(n),
    )(*grads)


class _Exchange:
    def __init__(self, part_refs, out_refs, send_sems, recv_sems):
        x, y, c = _position()
        self.copies = [
            pltpu.make_async_remote_copy(
                src_ref=part_refs[w].at[2 * ch[0] + ch[1]], dst_ref=out_refs[w].at[k], send_sem=send_sems.at[3 * w + k],
                recv_sem=recv_sems.at[3 * w + k], device_id=(*ch, c), device_id_type=MESH)
            for w in range(len(part_refs)) for k, ch in enumerate([(1 - x, y), (x, 1 - y), (1 - x, 1 - y)])]

    @staticmethod
    def semaphores(n):
        return [pltpu.SemaphoreType.DMA((3 * n,)), pltpu.SemaphoreType.DMA((3 * n,))]

    @staticmethod
    def out_shapes(parts):
        return [jax.ShapeDtypeStruct((3,) + p.shape[1:], p.dtype) for p in parts]

    def start(self):
        for cp in self.copies:
            cp.start()

    def finish(self):
        for cp in self.copies:
            cp.wait()


def _rope_tables(s):
    rows = s // GRID_W
    inv = 10000.0 ** (-jnp.arange(0, 32, 2, dtype=F32) / 32.0)
    ang_r = jnp.arange(rows, dtype=F32)[:, None] * inv
    ang_c = jnp.arange(GRID_W, dtype=F32)[:, None] * inv
    first = (jnp.arange(128) % 32) < 16

    def table(fn):
        tr = jnp.broadcast_to(fn(ang_r)[:, None, :], (rows, GRID_W, 16))
        tc = jnp.broadcast_to(fn(ang_c)[None, :, :], (rows, GRID_W, 16))
        return jnp.concatenate([tr, tr, tc, tc] * 2, axis=-1).reshape(s, 128)

    cos, sin = table(jnp.cos), table(jnp.sin)
    return cos, jnp.where(first, -sin, 0.0), jnp.where(first, 0.0, sin)


def _local_step(x, ctx, tgt, mod, modc, nw1, win, qnw, knw, sink, cw, cb, cnw, cnb, wout4, nw2, wup, fcw, fcb, wdown4,
                core=None):
    s = x.shape[0]
    sh1, sc1, g1, sh2, sc2, g2 = [mod[:, D * k:D * (k + 1)] for k in range(6)]
    shc, scc = modc[:, :D], modc[:, D:2 * D]
    cos, sa, sb = _rope_tables(s)
    gi = jnp.arange(128) // 64
    gmat = (gi[:, None] == gi[None, :]).astype(_MX)
    qnw2, knw2 = jnp.tile(qnw, (1, 2)), jnp.tile(knw, (1, 2))
    cw32 = jnp.pad(cw, ((0, 1), (0, 0)))
    fcw2 = jnp.pad(fcw, ((0, 5), (0, 0))).reshape(8, 2, FH).transpose(1, 0, 2)
    fcb2 = fcb.reshape(2, 1, FH)
    winkv = win[:, 512:768]
    sinkv = sink.reshape(8)

    host = (lambda *bufs: ()) if core is None else (lambda *bufs: bufs)
    (proj, h, q0, q1, k, ksw, v, vsw, u0), got = _mix_in(x, nw1, sc1, sh1, win, qnw2, knw2, cos, sa, sb, gmat,
                                                        gather=host(wout4))
    wout = (got[0] if got else wout4).reshape(D, D)
    kvc, hc, kc, kcsw, vc, vcsw = _ctx_kv(ctx, nw1, scc, shc, winkv, knw2, gmat)
    padr = lambda a: jnp.pad(a, ((TQ, TQ), (0, 0)))
    kp, kswp, vp, vswp = padr(k), padr(ksw), padr(v), padr(vsw)
    o, got = _attn_fwd(q0, q1, kp, kswp, vp, vswp, kc, kcsw, vc, vcsw, sinkv, gather=host(wup))
    wup = got[0] if got else wup
    u1 = _conv31(u0, cw32, cb)
    x1, mix, cat = _mix_out(o, u1, cnw, cnb, wout, x, g1)
    (h2, up0), got = _ffn_up(x1, nw2, sc2, sh2, wup, gather=host(wdown4))
    wdown = (got[0] if got else wdown4).reshape(FH, D)
    act = _ffn_conv_act(up0, fcw2, fcb2)
    dy, dact, ddn, vec_dn = _ffn_down(act, wdown, x1, g2, tgt)
    loss = (0.5 / D) * jnp.sum(vec_dn[1])

    dup, gfc = _ffn_conv_bwd(up0, dact, fcw2, fcb2)
    g_wdown = _tn_matmul(act, ddn, FQ, D, "gw_down").reshape(4, FH // 4, D)
    g_wup = _tn_matmul(h2, dup, D, FQ, "gw_up", b_split=True, by_chip=True)
    dx1, dmix, vec_up = _ffn_up_bwd(dup, wup, x1, dy, mix, nw2, sc2, g1)
    g_wout = _tn_matmul(cat, dmix, D, D, "gw_out").reshape(4, D // 4, D)
    late = [g_wout, g_wup, g_wdown]
    (do, du1, vec_ln), from_sib = _mix_out_bwd(dmix, wout, u1, cnw, cnb, swap=() if core is None else late)
    parts = () if core is None else _pair_sum(late, from_sib, core, "grad_pair_sum_late")
    dga, dgb, gcw = _conv31_bwd(du1, u0, cw32, proj)
    (dq, dkp, dvp, dkc, dvc, dsink), arrived = _attn_bwd(q0, q1, kp, kswp, vp, vswp, kc, kcsw, vc, vcsw, sinkv, o, do,
                                                         exchange=parts)
    if core is not None:
        g_wout, g_wup, g_wdown = zip(parts, arrived)
    gwinkv, vec_ctx = _ctx_bwd(ctx, nw1, scc, winkv, kvc, hc, dkc, dvc, knw2, gmat)
    gx, dproj, vec_in = _mix_in_bwd(dq, dkp[TQ:TQ + s], dvp[TQ:TQ + s], dga, dgb, proj, x, dx1, win, nw1, sc1,
                                    qnw2, knw2, cos, sa, sb, gmat)
    g_win = _tn_matmul(h, dproj, D, INW, "gw_in", init=gwinkv)

    g_qn = vec_in[3:4, 0:128]
    g_kn = vec_in[3:4, 128:256] + vec_ctx[3:4, 0:128]
    grads = dict(
        norm_mix_w=vec_in[2:3] + vec_ctx[2:3], w_in=g_win,
        q_norm_w=g_qn[:, :64] + g_qn[:, 64:], k_norm_w=g_kn[:, :64] + g_kn[:, 64:],
        sink_logit=dsink[:, 0].reshape(1, 8), conv_w=gcw[:CK], conv_b=gcw[CK:CK + 1],
        conv_norm_w=vec_ln[1:2], conv_norm_b=vec_ln[0:1], w_out=g_wout, norm_ffn_w=vec_up[2:3], w_up=g_wup,
        ffn_conv_w=gfc[:, 0:3].transpose(1, 0, 2).reshape(3, 2 * FH), ffn_conv_b=gfc[:, 3].reshape(1, 2 * FH),
        w_down=g_wdown)
    dmod = jnp.concatenate([vec_in[0:1], vec_in[1:2], vec_up[3:4], vec_up[0:1], vec_up[1:2], vec_dn[0:1]], axis=1)
    dmodc = jnp.concatenate([vec_ctx[0:1], vec_ctx[1:2]], axis=1)
    return loss, gx, grads, dmod, dmodc


_SMALL = ["c_ctx", "b_mod", "norm_mix_w", "q_norm_w", "k_norm_w", "sink_logit", "conv_w", "conv_b", "conv_norm_w",
          "conv_norm_b", "norm_ffn_w", "ffn_conv_w", "ffn_conv_b"]
_GATHERED = ["w_in", "w_out", "w_up", "w_down"]
_BIG = ["w_mod"] + _GATHERED
_ORDER = ["c_ctx", "w_mod", "b_mod", "norm_mix_w", "w_in", "q_norm_w", "k_norm_w", "sink_logit", "conv_w", "conv_b",
          "conv_norm_w", "conv_norm_b", "w_out", "norm_ffn_w", "w_up", "ffn_conv_w", "ffn_conv_b", "w_down"]
_PACK = [("norm_mix_w", 1), ("norm_ffn_w", 1), ("q_norm_w", 1), ("k_norm_w", 1), ("sink_logit", 1), ("conv_b", 1),
         ("conv_norm_w", 1), ("conv_norm_b", 1), ("ffn_conv_b", 6), ("conv_w", 16), ("ffn_conv_w", 17)]
_PACK_ROWS = 56


def _pack_rows(a, nrows):
    flat = a.reshape(-1)
    return jnp.pad(flat, (0, nrows * D - flat.shape[0])).reshape(nrows, D)


def kernel(x, c, ctx, c_ctx, w_mod, b_mod, norm_mix_w, w_in, q_norm_w, k_norm_w, sink_logit, conv_w, conv_b, conv_norm_w, conv_norm_b, w_out, norm_ffn_w, w_up, ffn_conv_w, ffn_conv_b, w_down, loss_target, m_c_ctx, m_w_mod, m_b_mod, m_norm_mix_w, m_w_in, m_q_norm_w, m_k_norm_w, m_sink_logit, m_conv_w, m_conv_b, m_conv_norm_w, m_conv_norm_b, m_w_out, m_norm_ffn_w, m_w_up, m_ffn_conv_w, m_ffn_conv_b, m_w_down, v_c_ctx, v_w_mod, v_b_mod, v_norm_mix_w, v_w_in, v_q_norm_w, v_k_norm_w, v_sink_logit, v_conv_w, v_conv_b, v_conv_norm_w, v_conv_norm_b, v_w_out, v_norm_ffn_w, v_w_up, v_ffn_conv_w, v_ffn_conv_b, v_w_down):
    w = dict(c_ctx=c_ctx.reshape(1, D), w_mod=w_mod[0], b_mod=b_mod, norm_mix_w=norm_mix_w, w_in=w_in[0],
             q_norm_w=q_norm_w, k_norm_w=k_norm_w, sink_logit=sink_logit, conv_w=conv_w[0], conv_b=conv_b,
             conv_norm_w=conv_norm_w, conv_norm_b=conv_norm_b, w_out=w_out[0], norm_ffn_w=norm_ffn_w, w_up=w_up[0],
             ffn_conv_w=ffn_conv_w[0], ffn_conv_b=ffn_conv_b, w_down=w_down[0])
    m = dict(c_ctx=m_c_ctx.reshape(1, D), w_mod=m_w_mod[0], b_mod=m_b_mod, norm_mix_w=m_norm_mix_w, w_in=m_w_in[0],
             q_norm_w=m_q_norm_w, k_norm_w=m_k_norm_w, sink_logit=m_sink_logit, conv_w=m_conv_w[0], conv_b=m_conv_b,
             conv_norm_w=m_conv_norm_w, conv_norm_b=m_conv_norm_b, w_out=m_w_out[0], norm_ffn_w=m_norm_ffn_w,
             w_up=m_w_up[0], ffn_conv_w=m_ffn_conv_w[0], ffn_conv_b=m_ffn_conv_b, w_down=m_w_down[0])
    v = dict(c_ctx=v_c_ctx.reshape(1, D), w_mod=v_w_mod[0], b_mod=v_b_mod, norm_mix_w=v_norm_mix_w, w_in=v_w_in[0],
             q_norm_w=v_q_norm_w, k_norm_w=v_k_norm_w, sink_logit=v_sink_logit, conv_w=v_conv_w[0], conv_b=v_conv_b,
             conv_norm_w=v_conv_norm_w, conv_norm_b=v_conv_norm_b, w_out=v_w_out[0], norm_ffn_w=v_norm_ffn_w,
             w_up=v_w_up[0], ffn_conv_w=v_ffn_conv_w[0], ffn_conv_b=v_ffn_conv_b, w_down=v_w_down[0])
    xi, yi, ci = _position()
    chip = 2 * xi + yi
    dev = 2 * chip + ci
    s = x.shape[1]
    ncol = w["w_mod"].shape[1]

    win_buf, wout_buf, wup_buf, wdown_buf = _cast_shards([w[n] for n in _GATHERED], chip)

    blk0 = jnp.concatenate([_pack_rows(w["conv_w"], 4), _pack_rows(w["ffn_conv_w"], 5), c,
                            jnp.zeros((6, D), F32)], axis=0)
    g0, _, (win4,) = _small_allgather(blk0, "gather_c_convw", gather=[win_buf])
    win = win4.transpose(1, 0, 2).reshape(D, INW)
    g0 = g0.reshape(8, 16, D)
    c_all = g0[:, 9, :]
    cs = jnp.concatenate([c_all, w["c_ctx"], jnp.zeros((7, D), F32)], axis=0)
    cw_full = jnp.concatenate([g0[2 * j, 0:4].reshape(-1)[:CK * 128].reshape(CK, 128) for j in range(4)], axis=1)
    fcw_full = jnp.concatenate([g0[2 * j, 4:9].reshape(-1)[:3 * 1408].reshape(3, 1408) for j in range(4)], axis=1)

    b_loc = lax.dynamic_slice(w["b_mod"], (0, chip * ncol), (1, ncol))
    modp = _mod_fwd(cs, w["w_mod"], b_loc)
    gm = _small_allgather(modp, "gather_mod")[0].reshape(8, 16, ncol)
    mod_all = jnp.concatenate([gm[2 * j] for j in range(4)], axis=1)
    mod = lax.dynamic_slice(mod_all, (dev, 0), (1, 6 * D))
    modc = mod_all[8:9]

    loss_loc, gx, gl, dmod, dmodc = _local_step(
        x[0], ctx[0], loss_target[0], mod, modc, w["norm_mix_w"], win, w["q_norm_w"], w["k_norm_w"], w["sink_logit"],
        cw_full, w["conv_b"], w["conv_norm_w"], w["conv_norm_b"], wout_buf, w["norm_ffn_w"], wup_buf, fcw_full,
        w["ffn_conv_b"], wdown_buf, core=ci)

    last = [gl["w_in"].reshape(D, 4, INW // 4).transpose(1, 0, 2)]
    parts = _pair_sum(last, _grad_swap(last, "grad_swap_in"), ci, "grad_pair_sum_in")

    pack = jnp.concatenate([dmod.reshape(6, D), dmodc.reshape(2, D)]
                           + [_pack_rows(gl[name], nr) for name, nr in _PACK], axis=0)
    pack = jnp.pad(pack, ((0, _PACK_ROWS - pack.shape[0]), (0, 0)))
    last_row = lax.broadcasted_iota(jnp.int32, pack.shape, 0) == _PACK_ROWS - 1
    pack = jnp.where(last_row, loss_loc, pack)
    gp, arrived, _ = _small_allgather(pack, "gather_small_grads", exchange=parts)
    gp = gp.reshape(8, _PACK_ROWS, D)
    late = [gl[n] for n in ("w_out", "w_up", "w_down")]
    shard_grads = _chip_sum([*parts, *[p for p, _ in late]], [*arrived, *[a for _, a in late]], chip, ci)
    tot = _sum_leading(gp, "sum_small_grads")
    loss = tot[_PACK_ROWS - 1, 0]
    g = {}
    r = 8
    for name, nr in _PACK:
        shape = gl[name].shape
        g[name] = tot[r:r + nr].reshape(-1)[:math.prod(shape)].reshape(shape)
        r += nr
    dmod_all = jnp.concatenate([gp[:, 0:6].reshape(8, 6 * D),
                                jnp.pad(tot[6:8].reshape(1, 2 * D), ((0, 0), (0, 4 * D))),
                                jnp.zeros((7, 6 * D), F32)], axis=0)
    g["b_mod"] = _sum_leading(dmod_all.reshape(16, 1, 6 * D), "grad_b_mod")
    g["w_mod"], part = _mod_bwd(cs, lax.dynamic_slice(dmod_all, (0, chip * ncol), (16, ncol)), w["w_mod"])
    cparts, _, (g["w_in"], g["w_out"], g["w_up"], g["w_down"]) = _small_allgather(part, "gather_cctx",
                                                                                 share=shard_grads)
    g["c_ctx"] = _cctx_grad(cparts.reshape(8, 16, D)[0::2], w["c_ctx"])
    g["conv_w"] = lax.dynamic_slice(g["conv_w"], (0, chip * 128), (CK, 128))
    g["ffn_conv_w"] = lax.dynamic_slice(g["ffn_conv_w"], (0, chip * 1408), (3, 1408))

    delta, new_m, new_v = {}, {}, {}
    for name in _BIG:
        delta[name], new_m[name], new_v[name] = _adam_big(w[name], g[name], m[name], v[name], "adam_" + name)
    ds, ms, vs = _adam_small([w[n] for n in _SMALL], [g[n] for n in _SMALL], [m[n] for n in _SMALL],
                             [v[n] for n in _SMALL])
    for k, name in enumerate(_SMALL):
        delta[name], new_m[name], new_v[name] = ds[k], ms[k], vs[k]

    def shaped(d, name):
        a = d[name]
        if name == "c_ctx":
            return a.reshape(D)
        if name in ("w_mod", "w_in", "w_out", "w_up", "w_down", "conv_w", "ffn_conv_w"):
            return a[None]
        return a

    outs = [loss, gx[None]]
    for d in (g, delta, new_m, new_v):
        outs += [shaped(d, name) for name in _ORDER]
    return tuple(outs)
```

```python
import functools
import math

import jax
import jax.numpy as jnp
from jax import lax
from jax.experimental import pallas as pl
from jax.experimental.pallas import tpu as pltpu

F32 = jnp.float32
_MX = jnp.bfloat16
EPS = 1e-6
NEG = -1e30
D = 1024
AW = 512
CW = 512
INW = 1792
FH = 2816
LC = 256
CK = 31
GRID_W = 64
TM = 512
TQ = 128
NCH = FH // 128
FQ = 2 * FH // 4
VMEM_LIMIT = 56 * 1024 * 1024
MESH = pl.DeviceIdType.MESH

ADAM_LR, ADAM_B1, ADAM_B2, ADAM_EPS, ADAM_WD, ADAM_STEP = 0.001, 0.9, 0.999, 1e-08, 0.01, 10


def _cp(*sem):
    return pltpu.CompilerParams(dimension_semantics=sem, vmem_limit_bytes=VMEM_LIMIT)


def _vspec():
    return pl.BlockSpec(memory_space=pltpu.VMEM)


def _sig(z):
    return 1.0 / (1.0 + jnp.exp(-z))


def _dot(a, b):
    return jnp.dot(a, b, preferred_element_type=F32)


def _dot_nt(a, b):
    return lax.dot_general(a, b, (((1,), (1,)), ((), ())), preferred_element_type=F32)


def _dot_tn(a, b):
    return lax.dot_general(a, b, (((0,), (0,)), ((), ())), preferred_element_type=F32)


def _gsum(v, g):
    hi = v.astype(_MX)
    lo = (v - hi.astype(F32)).astype(_MX)
    return _dot(hi, g) + _dot(lo, g)


def _headnorm(ch, w, g):
    r = lax.rsqrt(_gsum(ch * ch, g) * (1.0 / 64.0) + EPS)
    return ch * r * w


def _headnorm_bwd(ch, dy, w, g):
    r = lax.rsqrt(_gsum(ch * ch, g) * (1.0 / 64.0) + EPS)
    hat = ch * r
    dhat = dy * w
    dch = r * (dhat - hat * (_gsum(dhat * hat, g) * (1.0 / 64.0)))
    return dch, dy * hat


def _rope(u, cos, sa, sb):
    return u * cos + pltpu.roll(u, 112, 1) * sa + pltpu.roll(u, 16, 1) * sb


def _rope_bwd(d, cos, sa, sb):
    return d * cos - pltpu.roll(d, 112, 1) * sa - pltpu.roll(d, 16, 1) * sb


def _rowsum(v):
    return jnp.sum(v, axis=0, keepdims=True)


def _call_hosting_gather(body, gather, step_of, nsteps, *, in_specs, out_specs, out_shape, scratch_shapes=(),
                         comm=None, **kw):
    comm = comm or _Gather
    ng, n_in, n_out = len(gather), len(in_specs), len(out_specs)
    if not ng:
        return pl.pallas_call(body, in_specs=in_specs, out_specs=out_specs, out_shape=out_shape,
                              scratch_shapes=list(scratch_shapes), **kw)

    def hosted(*refs):
        ins, outs = refs[:n_in], refs[n_in + ng:n_in + ng + n_out]
        rest = refs[n_in + 2 * ng + n_out:]
        copies = comm(refs[n_in:n_in + ng], refs[n_in + ng + n_out:n_in + 2 * ng + n_out], rest[0], rest[1],
                      [b.shape[1] for b in gather])
        step = step_of()
        pl.when(step == 0)(copies.start)
        body(*ins, *outs, *rest[2:])
        pl.when(step == max(nsteps - 3, 0))(copies.forward)
        pl.when(step == nsteps - 1)(copies.finish)

    return pl.pallas_call(
        hosted, in_specs=list(in_specs) + _any_specs(ng), out_specs=list(out_specs) + _any_specs(ng),
        out_shape=list(out_shape) + comm.results(gather),
        input_output_aliases={n_in + w: n_out + w for w in range(ng)} if comm.in_place else {},
        scratch_shapes=comm.semaphores(ng) + list(scratch_shapes), **kw)


def _mix_in(x, nw, sc, sh, win, qnw2, knw2, cos, sa, sb, gmat, gather=()):
    s = x.shape[0]

    def body(x_ref, nw_ref, sc_ref, sh_ref, win_ref, qnw_ref, knw_ref, cos_ref, sa_ref, sb_ref, g_ref,
             proj_ref, h_ref, q0_ref, q1_ref, k_ref, ksw_ref, v_ref, vsw_ref, u0_ref):
        xv = x_ref[...]
        r = lax.rsqrt(jnp.mean(xv * xv, axis=-1, keepdims=True) + EPS)
        h = xv * r * (nw_ref[...] * (1.0 + sc_ref[...])) + sh_ref[...]
        hb = h.astype(_MX)
        h_ref[...] = hb
        proj = _dot(hb, win_ref[...])
        proj_ref[...] = proj
        cs, sav, sbv, g = cos_ref[...], sa_ref[...], sb_ref[...], g_ref[...]
        lo = lax.broadcasted_iota(jnp.int32, (1, 128), 1) < 64
        for j in range(4):
            q = _rope(_headnorm(proj[:, 128 * j:128 * (j + 1)], qnw_ref[...], g), cs, sav, sbv) * 0.125
            q0_ref[:, 128 * j:128 * (j + 1)] = jnp.where(lo, q, 0.0).astype(_MX)
            q1_ref[:, 128 * j:128 * (j + 1)] = jnp.where(lo, 0.0, q).astype(_MX)
        k = _rope(_headnorm(proj[:, 512:640], knw_ref[...], g), cs, sav, sbv)
        k_ref[...] = k.astype(_MX)
        ksw_ref[...] = pltpu.roll(k, 64, 1).astype(_MX)
        v = proj[:, 640:768]
        v_ref[...] = v.astype(_MX)
        vsw_ref[...] = pltpu.roll(v, 64, 1).astype(_MX)
        u0_ref[...] = proj[:, 768:1280] * _sig(proj[:, 1280:1792])

    row = lambda n: pl.BlockSpec((1, n), lambda i: (0, 0))
    tile = lambda n: pl.BlockSpec((TM, n), lambda i: (i, 0))
    res = _call_hosting_gather(
        body, gather, lambda: pl.program_id(0), s // TM, name="mix_in", grid=(s // TM,),
        in_specs=[tile(D), row(D), row(D), row(D), pl.BlockSpec((D, INW), lambda i: (0, 0)), row(128), row(128),
                  tile(128), tile(128), tile(128), pl.BlockSpec((128, 128), lambda i: (0, 0))],
        out_specs=[tile(INW), tile(D), tile(AW), tile(AW), tile(128), tile(128), tile(128), tile(128), tile(CW)],
        out_shape=[jax.ShapeDtypeStruct((s, INW), F32), jax.ShapeDtypeStruct((s, D), _MX),
                   jax.ShapeDtypeStruct((s, AW), _MX), jax.ShapeDtypeStruct((s, AW), _MX),
                   jax.ShapeDtypeStruct((s, 128), _MX), jax.ShapeDtypeStruct((s, 128), _MX),
                   jax.ShapeDtypeStruct((s, 128), _MX), jax.ShapeDtypeStruct((s, 128), _MX),
                   jax.ShapeDtypeStruct((s, CW), F32)],
        compiler_params=_cp("arbitrary"),
    )(x, nw, sc, sh, win, qnw2, knw2, cos, sa, sb, gmat, *gather)
    return res[:9], res[9:]


def _ctx_kv(ctx, nw, scc, shc, winkv, knw2, gmat):
    def body(ctx_ref, nw_ref, sc_ref, sh_ref, w_ref, knw_ref, g_ref,
             kvc_ref, hc_ref, kc_ref, kcsw_ref, vc_ref, vcsw_ref):
        cv = ctx_ref[...]
        r = lax.rsqrt(jnp.mean(cv * cv, axis=-1, keepdims=True) + EPS)
        hc = (cv * r * (nw_ref[...] * (1.0 + sc_ref[...])) + sh_ref[...]).astype(_MX)
        hc_ref[...] = hc
        kvc = _dot(hc, w_ref[...])
        kvc_ref[...] = kvc
        kc = _headnorm(kvc[:, :128], knw_ref[...], g_ref[...])
        kc_ref[...] = kc.astype(_MX)
        kcsw_ref[...] = pltpu.roll(kc, 64, 1).astype(_MX)
        vc = kvc[:, 128:]
        vc_ref[...] = vc.astype(_MX)
        vcsw_ref[...] = pltpu.roll(vc, 64, 1).astype(_MX)

    return pl.pallas_call(
        body, name="ctx_kv",
        in_specs=[_vspec()] * 7, out_specs=[_vspec()] * 6,
        out_shape=[jax.ShapeDtypeStruct((LC, 256), F32), jax.ShapeDtypeStruct((LC, D), _MX)]
        + [jax.ShapeDtypeStruct((LC, 128), _MX)] * 4,
        compiler_params=pltpu.CompilerParams(vmem_limit_bytes=VMEM_LIMIT),
    )(ctx, nw, scc, shc, winkv, knw2, gmat)


def _attn_mask(i, s):
    r = lax.broadcasted_iota(jnp.int32, (2 * TQ, LC + 3 * TQ), 0) % TQ
    cidx = lax.broadcasted_iota(jnp.int32, (2 * TQ, LC + 3 * TQ), 1)
    qpos = i * TQ + r
    kpos = (i - 1) * TQ + (cidx - LC)
    near = (jnp.abs(qpos - kpos) <= 128) & (kpos >= 0) & (kpos < s)
    return (cidx < LC) | near


def _attn_bias(i, s):
    return jnp.where(_attn_mask(i, s), 0.0, NEG)


def _attn_exp(sc, bias, sinkv):
    sc = sc + bias
    m = jnp.maximum(jnp.max(sc, axis=-1, keepdims=True), sinkv)
    ex = jnp.exp(sc - m)
    es = jnp.exp(sinkv - m)
    return ex, es, 1.0 / (jnp.sum(ex, axis=-1, keepdims=True) + es)


def _sink_rows(sink_ref, g, e):
    return jnp.concatenate([jnp.full((TQ, 1), sink_ref[4 * g + e], F32),
                            jnp.full((TQ, 1), sink_ref[4 * g + 2 + e], F32)], axis=0)


def _attn_fwd(q0, q1, kp, kswp, vp, vswp, kc, kcsw, vc, vcsw, sink, gather=()):
    s = q0.shape[0]
    nb = s // TQ

    def body(q0_ref, q1_ref, kp_ref, kswp_ref, vp_ref, vswp_ref, kc_ref, kcsw_ref, vc_ref, vcsw_ref, sink_ref,
             o_ref, ex_ref, stat_ref):
        i = pl.program_id(0)
        st = pl.multiple_of(i * TQ, TQ)
        kall = (jnp.concatenate([kc_ref[...], kp_ref[pl.ds(st, 3 * TQ), :]], axis=0),
                jnp.concatenate([kcsw_ref[...], kswp_ref[pl.ds(st, 3 * TQ), :]], axis=0))
        vall = (jnp.concatenate([vc_ref[...], vp_ref[pl.ds(st, 3 * TQ), :]], axis=0),
                jnp.concatenate([vcsw_ref[...], vswp_ref[pl.ds(st, 3 * TQ), :]], axis=0))
        bias = _attn_bias(i, s)
        lo = lax.broadcasted_iota(jnp.int32, (1, 128), 1) < 64
        qrefs = (q0_ref, q1_ref)
        out = [jnp.zeros((TQ, 128), F32) for _ in range(4)]
        combos = [(g, e) for g in range(2) for e in range(2)]
        scores = [_dot_nt(jnp.concatenate([qrefs[e][:, 256 * g:256 * g + 128],
                                           qrefs[e][:, 256 * g + 128:256 * g + 256]], axis=0),
                          kall[0 if e == g else 1]) for g, e in combos]
        parts = []
        for n, ((g, e), sc) in enumerate(zip(combos, scores)):
            ex, es, inv = _attn_exp(sc, bias, _sink_rows(sink_ref, g, e))
            parts.append((ex.astype(_MX), inv))
            ex_ref[n] = parts[-1][0]
            stat_ref[n] = jnp.where(lo, inv, es * inv)
        for (g, e), (ex, inv) in zip(combos, parts):
            me = lo if e == 0 else jnp.logical_not(lo)
            o2 = _dot(ex, vall[0 if e == g else 1]) * inv
            out[2 * g] = out[2 * g] + jnp.where(me, o2[:TQ], 0.0)
            out[2 * g + 1] = out[2 * g + 1] + jnp.where(me, o2[TQ:], 0.0)
        for j in range(4):
            o_ref[:, 128 * j:128 * (j + 1)] = out[j]

    full = lambda a: pl.BlockSpec(a.shape, lambda i: (0, 0))
    qs = pl.BlockSpec((TQ, AW), lambda i: (i, 0))
    res = _call_hosting_gather(
        body, gather, lambda: pl.program_id(0), nb, name="attn_fwd", grid=(nb,),
        in_specs=[qs, qs, full(kp), full(kp), full(kp), full(kp), full(kc), full(kc), full(kc), full(kc),
                  pl.BlockSpec(memory_space=pltpu.SMEM)],
        out_specs=[qs, pl.BlockSpec((None, 4, 2 * TQ, LC + 3 * TQ), lambda i: (i, 0, 0, 0)),
                   pl.BlockSpec((None, 4, 2 * TQ, 128), lambda i: (i, 0, 0, 0))],
        out_shape=[jax.ShapeDtypeStruct((s, AW), F32), jax.ShapeDtypeStruct((nb, 4, 2 * TQ, LC + 3 * TQ), _MX),
                   jax.ShapeDtypeStruct((nb, 4, 2 * TQ, 128), F32)],
        compiler_params=_cp("arbitrary"),
    )(q0, q1, kp, kswp, vp, vswp, kc, kcsw, vc, vcsw, sink, *gather)
    return res[:3], res[3:]


def _conv31(u0, cw32, cb):
    s = u0.shape[0]
    rch = 256

    def body(u_ref, w_ref, b_ref, o_ref, pad_ref):
        pad_ref[0:16, :] = jnp.zeros((16, 128), F32)
        pad_ref[s + 16:s + 32, :] = jnp.zeros((16, 128), F32)
        pad_ref[16:s + 16, :] = u_ref[...]
        for cidx in range(s // rch):
            base = cidx * rch
            acc = jnp.zeros((rch, 128), F32) + b_ref[...]
            for j in range(CK):
                acc = acc + w_ref[j:j + 1, :] * pad_ref[base + j + 1:base + j + 1 + rch, :]
            o_ref[base:base + rch, :] = acc

    return pl.pallas_call(
        body, name="conv31", grid=(CW // 128,),
        in_specs=[pl.BlockSpec((s, 128), lambda c: (0, c)), pl.BlockSpec((32, 128), lambda c: (0, c)),
                  pl.BlockSpec((1, 128), lambda c: (0, c))],
        out_specs=pl.BlockSpec((s, 128), lambda c: (0, c)),
        out_shape=jax.ShapeDtypeStruct((s, CW), F32),
        scratch_shapes=[pltpu.VMEM((s + 32, 128), F32)],
        compiler_params=_cp("arbitrary"),
    )(u0, cw32, cb)


def _ln_stats(u1):
    mu = jnp.mean(u1, axis=-1, keepdims=True)
    xc = u1 - mu
    rstd = lax.rsqrt(jnp.mean(xc * xc, axis=-1, keepdims=True) + EPS)
    return xc * rstd, rstd


def _mix_out(o, u1, cnw, cnb, wout, x, g1):
    s = x.shape[0]

    def body(o_ref, u1_ref, cnw_ref, cnb_ref, w_ref, x_ref, g1_ref, x1_ref, mix_ref, cat_ref):
        u2n, _ = _ln_stats(u1_ref[...])
        u2 = u2n * cnw_ref[...] + cnb_ref[...]
        u3 = u2 * _sig(u2)
        cat = jnp.concatenate([o_ref[...], u3], axis=1).astype(_MX)
        cat_ref[...] = cat
        mix = _dot(cat, w_ref[...])
        mix_ref[...] = mix
        x1_ref[...] = x_ref[...] + g1_ref[...] * mix

    row = lambda n: pl.BlockSpec((1, n), lambda i: (0, 0))
    tile = lambda n: pl.BlockSpec((TM, n), lambda i: (i, 0))
    return pl.pallas_call(
        body, name="mix_out", grid=(s // TM,),
        in_specs=[tile(AW), tile(CW), row(CW), row(CW), pl.BlockSpec((D, D), lambda i: (0, 0)), tile(D), row(D)],
        out_specs=[tile(D), tile(D), tile(D)],
        out_shape=[jax.ShapeDtypeStruct((s, D), F32), jax.ShapeDtypeStruct((s, D), F32),
                   jax.ShapeDtypeStruct((s, D), _MX)],
        compiler_params=_cp("arbitrary"),
    )(o, u1, cnw, cnb, wout, x, g1)


def _ffn_up(x1, nw, sc, sh, wup4, gather=()):
    s = x1.shape[0]
    tm = min(1024, s)

    def body(x_ref, nw_ref, sc_ref, sh_ref, w_ref, h2_ref, up_ref, h2s):
        @pl.when(pl.program_id(1) == 0)
        def _():
            xv = x_ref[...]
            r = lax.rsqrt(jnp.mean(xv * xv, axis=-1, keepdims=True) + EPS)
            h2 = (xv * r * (nw_ref[...] * (1.0 + sc_ref[...])) + sh_ref[...]).astype(_MX)
            h2s[...] = h2
            h2_ref[...] = h2
        up_ref[...] = _dot(h2s[...], w_ref[...])

    row = pl.BlockSpec((1, D), lambda i, j: (0, 0))
    res = _call_hosting_gather(
        body, gather, lambda: pl.program_id(0) * 4 + pl.program_id(1), 4 * (s // tm), name="ffn_up",
        grid=(s // tm, 4),
        in_specs=[pl.BlockSpec((tm, D), lambda i, j: (i, 0)), row, row, row,
                  pl.BlockSpec((None, D, FQ), lambda i, j: (j, 0, 0))],
        out_specs=[pl.BlockSpec((tm, D), lambda i, j: (i, 0)),
                   pl.BlockSpec((None, tm, FQ), lambda i, j: (j // 2, i, j % 2))],
        out_shape=[jax.ShapeDtypeStruct((s, D), _MX), jax.ShapeDtypeStruct((2, s, FH), F32)],
        scratch_shapes=[pltpu.VMEM((tm, D), _MX)],
        compiler_params=_cp("arbitrary", "arbitrary"),
    )(x1, nw, sc, sh, wup4, *gather)
    return res[:2], res[2:]


def _fill_pad8(pad_ref, val, s):
    pad_ref[0:8, :] = jnp.zeros((8, 128), F32)
    pad_ref[s + 8:s + 16, :] = jnp.zeros((8, 128), F32)
    pad_ref[8:s + 8, :] = val


def _conv3_at(pad_ref, w_ref, half, base, rch):
    return (w_ref[half, 0:1, :] * pad_ref[base + 7:base + 7 + rch, :]
            + w_ref[half, 1:2, :] * pad_ref[base + 8:base + 8 + rch, :]
            + w_ref[half, 2:3, :] * pad_ref[base + 9:base + 9 + rch, :])


def _ffn_conv_act(up0, fcw, fcb):
    s = up0.shape[1]
    rch = 256

    def body(up_ref, w_ref, b_ref, act_ref, padg, padv):
        _fill_pad8(padg, up_ref[0], s)
        _fill_pad8(padv, up_ref[1], s)
        for cidx in range(s // rch):
            base = cidx * rch
            gate = _conv3_at(padg, w_ref, 0, base, rch) + b_ref[0]
            val = _conv3_at(padv, w_ref, 1, base, rch) + b_ref[1]
            act_ref[base:base + rch, :] = (gate * _sig(gate) * val).astype(_MX)

    return pl.pallas_call(
        body, name="ffn_conv_act", grid=(NCH,),
        in_specs=[pl.BlockSpec((2, s, 128), lambda c: (0, 0, c)), pl.BlockSpec((2, 8, 128), lambda c: (0, 0, c)),
                  pl.BlockSpec((2, 1, 128), lambda c: (0, 0, c))],
        out_specs=pl.BlockSpec((s, 128), lambda c: (0, c)),
        out_shape=jax.ShapeDtypeStruct((s, FH), _MX),
        scratch_shapes=[pltpu.VMEM((s + 16, 128), F32)] * 2,
        compiler_params=_cp("arbitrary"),
    )(up0, fcw, fcb)


def _ffn_down(act, wdown, x1, g2, tgt):
    s = x1.shape[0]

    def body(act_ref, w_ref, x1_ref, g2_ref, tgt_ref, dy_ref, dact_ref, ddn_ref, vec_ref):
        @pl.when(pl.program_id(0) == 0)
        def _():
            vec_ref[...] = jnp.zeros((8, D), F32)
        dn = _dot(act_ref[...], w_ref[...])
        diff = x1_ref[...] + g2_ref[...] * dn - tgt_ref[...]
        dy = diff * (1.0 / D)
        dy_ref[...] = dy
        ddn = (dy * g2_ref[...]).astype(_MX)
        ddn_ref[...] = ddn
        dact_ref[...] = _dot_nt(ddn, w_ref[...])
        vec_ref[0:1, :] += _rowsum(dy * dn)
        vec_ref[1:2, :] += _rowsum(diff * diff)

    tile = lambda n: pl.BlockSpec((TM, n), lambda i: (i, 0))
    return pl.pallas_call(
        body, name="ffn_down", grid=(s // TM,),
        in_specs=[tile(FH), pl.BlockSpec((FH, D), lambda i: (0, 0)), tile(D), pl.BlockSpec((1, D), lambda i: (0, 0)),
                  tile(D)],
        out_specs=[tile(D), tile(FH), tile(D), pl.BlockSpec((8, D), lambda i: (0, 0))],
        out_shape=[jax.ShapeDtypeStruct((s, D), F32), jax.ShapeDtypeStruct((s, FH), F32),
                   jax.ShapeDtypeStruct((s, D), _MX), jax.ShapeDtypeStruct((8, D), F32)],
        compiler_params=_cp("arbitrary"),
    )(act, wdown, x1, g2, tgt)


def _ffn_conv_bwd(up0, dact, fcw, fcb):
    s = up0.shape[1]
    rch = 256

    def body(up_ref, da_ref, w_ref, b_ref, dup_ref, gw_ref, padg, padv, dpg, dpv):
        _fill_pad8(padg, up_ref[0], s)
        _fill_pad8(padv, up_ref[1], s)
        for p in (dpg, dpv):
            p[0:8, :] = jnp.zeros((8, 128), F32)
            p[s + 8:s + 16, :] = jnp.zeros((8, 128), F32)
        acc = [[jnp.zeros((1, 128), F32) for _ in range(4)] for _ in range(2)]
        for cidx in range(s // rch):
            base = cidx * rch
            gate = _conv3_at(padg, w_ref, 0, base, rch) + b_ref[0]
            val = _conv3_at(padv, w_ref, 1, base, rch) + b_ref[1]
            da = da_ref[base:base + rch, :]
            sg = _sig(gate)
            dgate = da * val * sg * (1.0 + gate * (1.0 - sg))
            dval = da * gate * sg
            dpg[base + 8:base + 8 + rch, :] = dgate
            dpv[base + 8:base + 8 + rch, :] = dval
            for half, (dd, pad) in enumerate(((dgate, padg), (dval, padv))):
                for j in range(3):
                    acc[half][j] = acc[half][j] + _rowsum(dd * pad[base + 7 + j:base + 7 + j + rch, :])
                acc[half][3] = acc[half][3] + _rowsum(dd)
        for half in range(2):
            gw_ref[half] = jnp.zeros((8, 128), F32)
            for j in range(4):
                gw_ref[half, j:j + 1, :] = acc[half][j]
        for cidx in range(s // rch):
            base = cidx * rch
            for half, dp in enumerate((dpg, dpv)):
                dup_ref[half, base:base + rch, :] = (
                    w_ref[half, 0:1, :] * dp[base + 9:base + 9 + rch, :]
                    + w_ref[half, 1:2, :] * dp[base + 8:base + 8 + rch, :]
                    + w_ref[half, 2:3, :] * dp[base + 7:base + 7 + rch, :]).astype(_MX)

    return pl.pallas_call(
        body, name="ffn_conv_bwd", grid=(NCH,),
        in_specs=[pl.BlockSpec((2, s, 128), lambda c: (0, 0, c)), pl.BlockSpec((s, 128), lambda c: (0, c)),
                  pl.BlockSpec((2, 8, 128), lambda c: (0, 0, c)), pl.BlockSpec((2, 1, 128), lambda c: (0, 0, c))],
        out_specs=[pl.BlockSpec((2, s, 128), lambda c: (0, 0, c)), pl.BlockSpec((2, 8, 128), lambda c: (0, 0, c))],
        out_shape=[jax.ShapeDtypeStruct((2, s, FH), _MX), jax.ShapeDtypeStruct((2, 8, FH), F32)],
        scratch_shapes=[pltpu.VMEM((s + 16, 128), F32)] * 4,
        compiler_params=_cp("arbitrary"),
    )(up0, dact, fcw, fcb)


def _tn_matmul(a, b, tm, tn, name, b_split=False, by_chip=False, init=None):
    s, m = a.shape
    n = 2 * b.shape[2] if b_split else b.shape[1]
    ts = min(1024, s)
    nsteps = s // ts
    npb = (n // 2) // tn if b_split else None

    def body(*refs):
        if init is None:
            a_ref, b_ref, o_ref, acc = refs
        else:
            a_ref, b_ref, i_ref, o_ref, acc = refs
        k = pl.program_id(2)

        @pl.when(k == 0)
        def _():
            acc[...] = jnp.zeros((tm, tn), F32)
            if init is not None:
                acc[:, 512:768] = i_ref[...]
        acc[...] += _dot_tn(a_ref[...], b_ref[...])

        @pl.when(k == nsteps - 1)
        def _():
            o_ref[...] = acc[...].astype(_MX)

    if b_split:
        bspec = pl.BlockSpec((None, ts, tn), lambda i, j, k: (j // npb, k, j % npb))
    else:
        bspec = pl.BlockSpec((ts, tn), lambda i, j, k: (k, j))
    in_specs = [pl.BlockSpec((ts, tm), lambda i, j, k: (k, i)), bspec]
    args = [a, b]
    if init is not None:
        in_specs.append(pl.BlockSpec((tm, 256), lambda i, j, k: (i, 0)))
        args.append(init)
    if by_chip:
        out_spec = pl.BlockSpec((None, tm, tn), lambda i, j, k: (j, i, 0))
        out_shape = jax.ShapeDtypeStruct((n // tn, m, tn), _MX)
    else:
        out_spec = pl.BlockSpec((tm, tn), lambda i, j, k: (i, j))
        out_shape = jax.ShapeDtypeStruct((m, n), _MX)
    return pl.pallas_call(
        body, name=name, grid=(m // tm, n // tn, nsteps),
        in_specs=in_specs, out_specs=out_spec, out_shape=out_shape,
        scratch_shapes=[pltpu.VMEM((tm, tn), F32)],
        compiler_params=_cp("arbitrary", "arbitrary", "arbitrary"),
    )(*args)


def _ffn_up_bwd(dup, wup4, x1, dy, mix, nw, sc, g1):
    s = x1.shape[0]
    nk = 4

    def body(dup_ref, w_ref, x1_ref, dy_ref, mix_ref, nw_ref, sc_ref, g1_ref, dx1_ref, dmix_ref, vec_ref, acc):
        i, k = pl.program_id(0), pl.program_id(1)

        @pl.when(k == 0)
        def _():
            acc[...] = jnp.zeros((TM, D), F32)

        @pl.when((k == 0) & (i == 0))
        def _():
            vec_ref[...] = jnp.zeros((8, D), F32)
        acc[...] += _dot_nt(dup_ref[...], w_ref[...])

        @pl.when(k == nk - 1)
        def _():
            dh = acc[...]
            xv = x1_ref[...]
            r = lax.rsqrt(jnp.mean(xv * xv, axis=-1, keepdims=True) + EPS)
            xn = xv * r
            nwv, scv = nw_ref[...], sc_ref[...]
            vec_ref[0:1, :] += _rowsum(dh)
            vec_ref[1:2, :] += _rowsum(dh * xn) * nwv
            vec_ref[2:3, :] += _rowsum(dh * xn) * (1.0 + scv)
            dxn = dh * (nwv * (1.0 + scv))
            dx1 = dy_ref[...] + r * (dxn - xn * jnp.mean(dxn * xn, axis=-1, keepdims=True))
            dx1_ref[...] = dx1
            vec_ref[3:4, :] += _rowsum(dx1 * mix_ref[...])
            dmix_ref[...] = (dx1 * g1_ref[...]).astype(_MX)

    tile = pl.BlockSpec((TM, D), lambda i, k: (i, 0))
    row = pl.BlockSpec((1, D), lambda i, k: (0, 0))
    return pl.pallas_call(
        body, name="ffn_up_bwd", grid=(s // TM, nk),
        in_specs=[pl.BlockSpec((None, TM, FQ), lambda i, k: (k // 2, i, k % 2)),
                  pl.BlockSpec((None, D, FQ), lambda i, k: (k, 0, 0)), tile, tile, tile, row, row, row],
        out_specs=[tile, tile, pl.BlockSpec((8, D), lambda i, k: (0, 0))],
        out_shape=[jax.ShapeDtypeStruct((s, D), F32), jax.ShapeDtypeStruct((s, D), _MX),
                   jax.ShapeDtypeStruct((8, D), F32)],
        scratch_shapes=[pltpu.VMEM((TM, D), F32)],
        compiler_params=_cp("arbitrary", "arbitrary"),
    )(dup, wup4, x1, dy, mix, nw, sc, g1)


def _mix_out_bwd(dmix, wout, u1, cnw, cnb, swap=()):
    s = u1.shape[0]

    def body(dm_ref, w_ref, u1_ref, cnw_ref, cnb_ref, do_ref, du1_ref, vec_ref):
        @pl.when(pl.program_id(0) == 0)
        def _():
            vec_ref[...] = jnp.zeros((8, CW), F32)
        dcat = _dot_nt(dm_ref[...], w_ref[...])
        do_ref[...] = dcat[:, :AW]
        du3 = dcat[:, AW:]
        u2n, rstd = _ln_stats(u1_ref[...])
        u2 = u2n * cnw_ref[...] + cnb_ref[...]
        sg = _sig(u2)
        du2 = du3 * sg * (1.0 + u2 * (1.0 - sg))
        vec_ref[0:1, :] += _rowsum(du2)
        vec_ref[1:2, :] += _rowsum(du2 * u2n)
        d2n = du2 * cnw_ref[...]
        du1_ref[...] = rstd * (d2n - jnp.mean(d2n, axis=-1, keepdims=True)
                               - u2n * jnp.mean(d2n * u2n, axis=-1, keepdims=True))

    row = lambda n: pl.BlockSpec((1, n), lambda i: (0, 0))
    tile = lambda n: pl.BlockSpec((TM, n), lambda i: (i, 0))
    res = _call_hosting_gather(
        body, swap, lambda: pl.program_id(0), s // TM, comm=_Swap, name="mix_out_bwd", grid=(s // TM,),
        in_specs=[tile(D), pl.BlockSpec((D, D), lambda i: (0, 0)), tile(CW), row(CW), row(CW)],
        out_specs=[tile(AW), tile(CW), pl.BlockSpec((8, CW), lambda i: (0, 0))],
        out_shape=[jax.ShapeDtypeStruct((s, AW), F32), jax.ShapeDtypeStruct((s, CW), F32),
                   jax.ShapeDtypeStruct((8, CW), F32)],
        compiler_params=_cp("arbitrary"),
    )(dmix, wout, u1, cnw, cnb, *swap)
    return res[:3], res[3:]


def _conv31_bwd(du1, u0, cw32, proj):
    s = u0.shape[0]
    rch = 256

    def body(d_ref, u_ref, w_ref, ga_ref, gb_ref, dga_ref, dgb_ref, gw_ref, padu, padd):
        for p, src in ((padu, u_ref), (padd, d_ref)):
            p[0:16, :] = jnp.zeros((16, 128), F32)
            p[s + 16:s + 32, :] = jnp.zeros((16, 128), F32)
            p[16:s + 16, :] = src[...]
        for j in range(CK):
            acc = jnp.zeros((1, 128), F32)
            for cidx in range(s // rch):
                base = cidx * rch
                acc = acc + _rowsum(d_ref[base:base + rch, :] * padu[base + j + 1:base + j + 1 + rch, :])
            gw_ref[j:j + 1, :] = acc
        gw_ref[CK:CK + 1, :] = _rowsum(d_ref[...])
        for cidx in range(s // rch):
            base = cidx * rch
            du0 = jnp.zeros((rch, 128), F32)
            for j in range(CK):
                du0 = du0 + w_ref[j:j + 1, :] * padd[base + 31 - j:base + 31 - j + rch, :]
            sg = _sig(gb_ref[base:base + rch, :])
            ga = ga_ref[base:base + rch, :]
            dga_ref[base:base + rch, :] = (du0 * sg).astype(_MX)
            dgb_ref[base:base + rch, :] = (du0 * ga * sg * (1.0 - sg)).astype(_MX)

    blk = lambda off: pl.BlockSpec((s, 128), lambda c: (0, c + off))
    return pl.pallas_call(
        body, name="conv31_bwd", grid=(CW // 128,),
        in_specs=[blk(0), blk(0), pl.BlockSpec((32, 128), lambda c: (0, c)), blk(6), blk(10)],
        out_specs=[blk(0), blk(0), pl.BlockSpec((32, 128), lambda c: (0, c))],
        out_shape=[jax.ShapeDtypeStruct((s, CW), _MX), jax.ShapeDtypeStruct((s, CW), _MX),
                   jax.ShapeDtypeStruct((32, CW), F32)],
        scratch_shapes=[pltpu.VMEM((s + 32, 128), F32)] * 2,
        compiler_params=_cp("arbitrary"),
    )(du1, u0, cw32, proj, proj)


def _attn_bwd(q0, q1, kp, kswp, vp, vswp, kc, kcsw, vc, vcsw, ex, stat, o, do, exchange=()):
    s = q0.shape[0]
    nb = s // TQ
    ne = len(exchange)

    def body(*refs):
        (q0_ref, q1_ref, kp_ref, kswp_ref, vp_ref, vswp_ref, kc_ref, kcsw_ref, vc_ref, vcsw_ref, ex_ref, stat_ref,
         o_ref, do_ref) = refs[:14]
        dq_ref, dk_ref, dv_ref, dkc_ref, dvc_ref, dsink_ref = refs[14 + ne:20 + ne]
        i = pl.program_id(0)
        if ne:
            exch = _Exchange(refs[14:14 + ne], refs[20 + ne:20 + 2 * ne], refs[20 + 2 * ne], refs[21 + 2 * ne])
            pl.when(i == 0)(exch.start)

        @pl.when(i == 0)
        def _():
            dk_ref[...] = jnp.zeros((s + 2 * TQ, 128), F32)
            dv_ref[...] = jnp.zeros((s + 2 * TQ, 128), F32)
            dkc_ref[...] = jnp.zeros((LC, 128), F32)
            dvc_ref[...] = jnp.zeros((LC, 128), F32)
            dsink_ref[...] = jnp.zeros((8, 128), F32)
        st = pl.multiple_of(i * TQ, TQ)
        kall = (jnp.concatenate([kc_ref[...], kp_ref[pl.ds(st, 3 * TQ), :]], axis=0),
                jnp.concatenate([kcsw_ref[...], kswp_ref[pl.ds(st, 3 * TQ), :]], axis=0))
        vall = (jnp.concatenate([vc_ref[...], vp_ref[pl.ds(st, 3 * TQ), :]], axis=0),
                jnp.concatenate([vcsw_ref[...], vswp_ref[pl.ds(st, 3 * TQ), :]], axis=0))
        lo = lax.broadcasted_iota(jnp.int32, (1, 128), 1) < 64
        qrefs = (q0_ref, q1_ref)
        dq = [jnp.zeros((TQ, 128), F32) for _ in range(4)]
        dkt = jnp.zeros((128, LC + 3 * TQ), F32)
        dvt = jnp.zeros((128, LC + 3 * TQ), F32)
        combos = [(g, e) for g in range(2) for e in range(2)]

        def stacked(ref, g):
            return jnp.concatenate([ref[:, 256 * g:256 * g + 128], ref[:, 256 * g + 128:256 * g + 256]], axis=0)

        for n, (g, e) in enumerate(combos):
            me = lo if e == 0 else jnp.logical_not(lo)
            sw = 0 if e == g else 1
            qm = stacked(qrefs[e], g)
            p = ex_ref[n].astype(F32) * stat_ref[n, :, 0:1]
            dom = jnp.where(me, stacked(do_ref, g), 0.0)
            dd = jnp.sum(dom * stacked(o_ref, g), axis=-1, keepdims=True)
            domx = dom.astype(_MX)
            sd = stat_ref[n, :, 64:65] * dd
            dsink_ref[4 * g + e:4 * g + e + 1, :] -= jnp.sum(sd[:TQ], axis=0, keepdims=True)
            dsink_ref[4 * g + 2 + e:4 * g + 3 + e, :] -= jnp.sum(sd[TQ:], axis=0, keepdims=True)
            ds = (p * (_dot_nt(domx, vall[sw]) - dd)).astype(_MX)
            dq2 = _dot(ds, kall[sw])
            dq[2 * g] = dq[2 * g] + jnp.where(me, dq2[:TQ], 0.0)
            dq[2 * g + 1] = dq[2 * g + 1] + jnp.where(me, dq2[TQ:], 0.0)
            dk2 = _dot_tn(qm, ds)
            dv2 = _dot_tn(domx, p.astype(_MX))
            if sw:
                dk2 = pltpu.roll(dk2, 64, 0)
                dv2 = pltpu.roll(dv2, 64, 0)
            dkt = dkt + dk2
            dvt = dvt + dv2
        dkt = dkt.T
        dvt = dvt.T
        for j in range(4):
            dq_ref[:, 128 * j:128 * (j + 1)] = dq[j] * 0.125
        dkc_ref[...] += dkt[:LC]
        dvc_ref[...] += dvt[:LC]
        dk_ref[pl.ds(st, 3 * TQ), :] += dkt[LC:]
        dv_ref[pl.ds(st, 3 * TQ), :] += dvt[LC:]
        if ne:
            pl.when(i == nb - 1)(exch.finish)

    full = lambda a: pl.BlockSpec(a.shape, lambda i: (0, 0))
    fs = lambda r: pl.BlockSpec((r, 128), lambda i: (0, 0))
    qs = pl.BlockSpec((TQ, AW), lambda i: (i, 0))
    res = pl.pallas_call(
        body, name="attn_bwd", grid=(nb,),
        in_specs=[qs, qs, full(kp), full(kp), full(kp), full(kp), full(kc), full(kc), full(kc), full(kc),
                  pl.BlockSpec((None, 4, 2 * TQ, LC + 3 * TQ), lambda i: (i, 0, 0, 0)),
                  pl.BlockSpec((None, 4, 2 * TQ, 128), lambda i: (i, 0, 0, 0)), qs, qs] + _any_specs(ne),
        out_specs=[qs, fs(s + 2 * TQ), fs(s + 2 * TQ), fs(LC), fs(LC), fs(8)] + _any_specs(ne),
        out_shape=[jax.ShapeDtypeStruct((s, AW), F32), jax.ShapeDtypeStruct((s + 2 * TQ, 128), F32),
                   jax.ShapeDtypeStruct((s + 2 * TQ, 128), F32), jax.ShapeDtypeStruct((LC, 128), F32),
                   jax.ShapeDtypeStruct((LC, 128), F32), jax.ShapeDtypeStruct((8, 128), F32)]
        + _Exchange.out_shapes(exchange),
        scratch_shapes=_Exchange.semaphores(ne) if ne else [],
        compiler_params=_cp("arbitrary"),
    )(q0, q1, kp, kswp, vp, vswp, kc, kcsw, vc, vcsw, ex, stat, o, do, *exchange)
    return res[:6], res[6:]


def _mix_in_bwd(dq, dk, dv, dga, dgb, proj, x, dx1, win, nw, sc, qnw2, knw2, cos, sa, sb, gmat):
    s = x.shape[0]

    def body(dq_ref, dk_ref, dv_ref, dga_ref, dgb_ref, proj_ref, x_ref, dx1_ref, win_ref, nw_ref, sc_ref,
             qnw_ref, knw_ref, cos_ref, sa_ref, sb_ref, g_ref, gx_ref, dproj_ref, vec_ref):
        @pl.when(pl.program_id(0) == 0)
        def _():
            vec_ref[...] = jnp.zeros((8, D), F32)
        cs, sav, sbv, g = cos_ref[...], sa_ref[...], sb_ref[...], g_ref[...]
        gq = jnp.zeros((1, 128), F32)
        for j in range(4):
            dqn = _rope_bwd(dq_ref[:, 128 * j:128 * (j + 1)], cs, sav, sbv)
            dch, gw = _headnorm_bwd(proj_ref[:, 128 * j:128 * (j + 1)], dqn, qnw_ref[...], g)
            dproj_ref[:, 128 * j:128 * (j + 1)] = dch.astype(_MX)
            gq = gq + _rowsum(gw)
        dkn = _rope_bwd(dk_ref[...], cs, sav, sbv)
        dch, gw = _headnorm_bwd(proj_ref[:, 512:640], dkn, knw_ref[...], g)
        dproj_ref[:, 512:640] = dch.astype(_MX)
        dproj_ref[:, 640:768] = dv_ref[...].astype(_MX)
        dproj_ref[:, 768:1280] = dga_ref[...]
        dproj_ref[:, 1280:1792] = dgb_ref[...]
        vec_ref[3:4, 0:128] += gq
        vec_ref[3:4, 128:256] += _rowsum(gw)
        dh = _dot_nt(dproj_ref[...], win_ref[...])
        xv = x_ref[...]
        r = lax.rsqrt(jnp.mean(xv * xv, axis=-1, keepdims=True) + EPS)
        xn = xv * r
        nwv, scv = nw_ref[...], sc_ref[...]
        vec_ref[0:1, :] += _rowsum(dh)
        vec_ref[1:2, :] += _rowsum(dh * xn) * nwv
        vec_ref[2:3, :] += _rowsum(dh * xn) * (1.0 + scv)
        dxn = dh * (nwv * (1.0 + scv))
        gx_ref[...] = dx1_ref[...] + r * (dxn - xn * jnp.mean(dxn * xn, axis=-1, keepdims=True))

    row = lambda n: pl.BlockSpec((1, n), lambda i: (0, 0))
    tile = lambda n: pl.BlockSpec((TM, n), lambda i: (i, 0))
    return pl.pallas_call(
        body, name="mix_in_bwd", grid=(s // TM,),
        in_specs=[tile(AW), tile(128), tile(128), tile(CW), tile(CW), tile(INW), tile(D), tile(D),
                  pl.BlockSpec((D, INW), lambda i: (0, 0)), row(D), row(D), row(128), row(128),
                  tile(128), tile(128), tile(128), pl.BlockSpec((128, 128), lambda i: (0, 0))],
        out_specs=[tile(D), tile(INW), pl.BlockSpec((8, D), lambda i: (0, 0))],
        out_shape=[jax.ShapeDtypeStruct((s, D), F32), jax.ShapeDtypeStruct((s, INW), _MX),
                   jax.ShapeDtypeStruct((8, D), F32)],
        compiler_params=_cp("arbitrary"),
    )(dq, dk, dv, dga, dgb, proj, x, dx1, win, nw, sc, qnw2, knw2, cos, sa, sb, gmat)


def _ctx_bwd(ctx, nw, scc, winkv, kvc, hc, dkc, dvc, knw2, gmat):
    def body(ctx_ref, nw_ref, sc_ref, w_ref, kvc_ref, hc_ref, dkc_ref, dvc_ref, knw_ref, g_ref, gw_ref, vec_ref):
        dkr, gk = _headnorm_bwd(kvc_ref[:, 0:128], dkc_ref[...], knw_ref[...], g_ref[...])
        dkv = jnp.concatenate([dkr, dvc_ref[...]], axis=1).astype(_MX)
        gw_ref[...] = _dot_tn(hc_ref[...], dkv)
        dh = _dot_nt(dkv, w_ref[...])
        cv = ctx_ref[...]
        r = lax.rsqrt(jnp.mean(cv * cv, axis=-1, keepdims=True) + EPS)
        cn = cv * r
        vec_ref[...] = jnp.zeros((8, D), F32)
        vec_ref[0:1, :] = _rowsum(dh)
        vec_ref[1:2, :] = _rowsum(dh * cn) * nw_ref[...]
        vec_ref[2:3, :] = _rowsum(dh * cn) * (1.0 + sc_ref[...])
        vec_ref[3:4, 0:128] = _rowsum(gk)

    return pl.pallas_call(
        body, name="ctx_bwd", in_specs=[_vspec()] * 10, out_specs=[_vspec()] * 2,
        out_shape=[jax.ShapeDtypeStruct((D, 256), F32), jax.ShapeDtypeStruct((8, D), F32)],
        compiler_params=pltpu.CompilerParams(vmem_limit_bytes=VMEM_LIMIT),
    )(ctx, nw, scc, winkv, kvc, hc, dkc, dvc, knw2, gmat)


def _mod_fwd(cs, wmod, bloc):
    def body(c_ref, w_ref, b_ref, o_ref):
        cv = c_ref[...]
        o_ref[...] = _dot((cv * _sig(cv)).astype(_MX), w_ref[...].astype(_MX)) + b_ref[...]

    return pl.pallas_call(
        body, name="mod_fwd", in_specs=[_vspec()] * 3, out_specs=_vspec(),
        out_shape=jax.ShapeDtypeStruct((16, wmod.shape[1]), F32),
        compiler_params=pltpu.CompilerParams(vmem_limit_bytes=VMEM_LIMIT),
    )(cs, wmod, bloc)


def _mod_bwd(cs, dmod, wmod):
    def body(c_ref, d_ref, w_ref, gw_ref, part_ref):
        cv = c_ref[...]
        sl = (cv * _sig(cv)).astype(_MX)
        dm = d_ref[...].astype(_MX)
        gw_ref[...] = _dot_tn(sl, dm)
        part_ref[...] = _dot_nt(dm, w_ref[...].astype(_MX))

    return pl.pallas_call(
        body, name="mod_bwd", in_specs=[_vspec()] * 3, out_specs=[_vspec()] * 2,
        out_shape=[jax.ShapeDtypeStruct(wmod.shape, F32), jax.ShapeDtypeStruct((16, D), F32)],
        compiler_params=pltpu.CompilerParams(vmem_limit_bytes=VMEM_LIMIT),
    )(cs, dmod, wmod)


def _sum_leading(a, name):
    n = a.shape[0]

    def body(a_ref, o_ref):
        acc = a_ref[0]
        for k in range(1, n):
            acc = acc + a_ref[k]
        o_ref[...] = acc

    return pl.pallas_call(
        body, name=name, in_specs=[_vspec()], out_specs=_vspec(),
        out_shape=jax.ShapeDtypeStruct(a.shape[1:], F32),
        compiler_params=pltpu.CompilerParams(vmem_limit_bytes=VMEM_LIMIT),
    )(a)


def _cctx_grad(parts, cc):
    def body(p_ref, c_ref, o_ref):
        acc = p_ref[0, 8:9, :]
        for k in range(1, 4):
            acc = acc + p_ref[k, 8:9, :]
        cv = c_ref[...]
        sg = _sig(cv)
        o_ref[...] = acc * (sg * (1.0 + cv * (1.0 - sg)))

    return pl.pallas_call(
        body, name="cctx_grad", in_specs=[_vspec()] * 2, out_specs=_vspec(),
        out_shape=jax.ShapeDtypeStruct((1, D), F32),
    )(parts, cc)


def _adam_math(w, g, m, v):
    mn = ADAM_B1 * m + (1.0 - ADAM_B1) * g
    vn = ADAM_B2 * v + (1.0 - ADAM_B2) * (g * g)
    mh = mn / (1.0 - ADAM_B1 ** ADAM_STEP)
    vh = vn / (1.0 - ADAM_B2 ** ADAM_STEP)
    delta = -ADAM_LR * (mh / (jnp.sqrt(vh) + ADAM_EPS) + ADAM_WD * w)
    return delta, mn, vn


def _adam_big(w, g, m, v, name):
    r, n = w.shape
    tr = 256 if r % 256 == 0 else 64

    def body(w_ref, g_ref, m_ref, v_ref, d_ref, mo_ref, vo_ref):
        d, mn, vn = _adam_math(w_ref[...], g_ref[...], m_ref[...], v_ref[...])
        d_ref[...] = d
        mo_ref[...] = mn
        vo_ref[...] = vn

    spec = pl.BlockSpec((tr, n), lambda i: (i, 0))
    return pl.pallas_call(
        body, name=name, grid=(r // tr,), in_specs=[spec] * 4, out_specs=[spec] * 3,
        out_shape=[jax.ShapeDtypeStruct((r, n), F32)] * 3,
        compiler_params=_cp("arbitrary"),
    )(w, g, m, v)


def _adam_small(ws, gs, ms, vs):
    n = len(ws)

    def body(*refs):
        ins, outs = refs[:4 * n], refs[4 * n:]
        for k in range(n):
            d, mn, vn = _adam_math(ins[k][...], ins[n + k][...], ins[2 * n + k][...], ins[3 * n + k][...])
            outs[k][...] = d
            outs[n + k][...] = mn
            outs[2 * n + k][...] = vn

    shapes = [jax.ShapeDtypeStruct(w.shape, F32) for w in ws]
    res = pl.pallas_call(
        body, name="adam_small", in_specs=[_vspec()] * (4 * n), out_specs=[_vspec()] * (3 * n),
        out_shape=shapes * 3,
    )(*ws, *gs, *ms, *vs)
    return res[:n], res[n:2 * n], res[2 * n:]


def _pair_sum(grads, from_sib, core, name):
    n = len(grads)

    def body(c_ref, *refs):
        for w in range(n):
            a_ref, b_ref, o_ref = refs[w], refs[n + w], refs[2 * n + w]
            o_ref[...] = (a_ref[...].astype(F32) + b_ref[...].astype(F32)).astype(_MX)

    halves = [(None, g.shape[1] // 2, g.shape[2]) for g in grads]
    return pl.pallas_call(
        body, name=name,
        grid_spec=pltpu.PrefetchScalarGridSpec(
            num_scalar_prefetch=1, grid=(4,),
            in_specs=[pl.BlockSpec(h, lambda j, c: (j, c[0], 0)) for h in halves]
            + [pl.BlockSpec(h, lambda j, c: (j, 0, 0)) for h in halves],
            out_specs=[pl.BlockSpec(h, lambda j, c: (j, 0, 0)) for h in halves]),
        out_shape=[jax.ShapeDtypeStruct(p.shape, _MX) for p in from_sib],
        compiler_params=_cp("arbitrary"),
    )(core.reshape(1), *grads, *from_sib)


def _chip_sum(parts, arrived, chip, core):
    n = len(parts)

    def body(s_ref, *refs):
        for w in range(n):
            own_ref, p_ref, o_ref = refs[w], refs[n + w], refs[2 * n + w]
            acc = own_ref[...].astype(F32)
            for k in range(3):
                acc = acc + p_ref[k].astype(F32)
            o_ref[...] = acc

    blk = [(p.shape[1] // 2, p.shape[2]) for p in parts]
    return pl.pallas_call(
        body, name="grad_chip_sum",
        grid_spec=pltpu.PrefetchScalarGridSpec(
            num_scalar_prefetch=1, grid=(2,),
            in_specs=[pl.BlockSpec((None,) + b, lambda i, s: (s[0], i, 0)) for b in blk]
            + [pl.BlockSpec((3,) + b, lambda i, s: (0, i, 0)) for b in blk],
            out_specs=[pl.BlockSpec(b, lambda i, s: (2 * s[1] + i, 0)) for b in blk]),
        out_shape=[jax.ShapeDtypeStruct((2 * p.shape[1], p.shape[2]), F32) for p in parts],
        compiler_params=_cp("arbitrary"),
    )(jnp.stack([chip, core]), *parts, *arrived)


def _cast_shards(ws, chip):
    n = len(ws)

    def body(s_ref, *refs):
        for w in range(n):
            refs[n + w][...] = refs[w][...].astype(_MX)

    blk = [(a.shape[0] // 2, a.shape[1]) for a in ws]
    return pl.pallas_call(
        body, name="cast_shards",
        grid_spec=pltpu.PrefetchScalarGridSpec(
            num_scalar_prefetch=1, grid=(2,),
            in_specs=[pl.BlockSpec(b, lambda i, s: (i, 0)) for b in blk],
            out_specs=[pl.BlockSpec((None,) + b, lambda i, s: (s[0], i, 0)) for b in blk]),
        out_shape=[jax.ShapeDtypeStruct((4,) + a.shape, _MX) for a in ws],
        compiler_params=_cp("arbitrary"),
    )(chip.reshape(1), *ws)


def _position():
    x, y, c = lax.axis_index("x"), lax.axis_index("y"), lax.axis_index("c")
    return x, y, c


def _small_allgather(blk, name, exchange=(), share=(), gather=()):
    m_per, n = blk.shape
    assert not (share and gather)
    share = tuple(share) + tuple(gather)
    ne, ns = len(exchange), len(share)

    def body(*refs):
        x_ref = refs[0]
        out_ref = refs[1 + ne + ns]
        send_sems, recv_sems = refs[2 + 2 * ne + 2 * ns], refs[3 + 2 * ne + 2 * ns]
        extra = refs[4 + 2 * ne + 2 * ns:]
        x, y, c = _position()
        me, sibling = (x, y, c), (x, y, 1 - c)
        chips = [(1 - x, y), (x, 1 - y), (1 - x, 1 - y)]
        if ne:
            exch = _Exchange(refs[1:1 + ne], refs[2 + ne + ns:2 + 2 * ne + ns], extra[0], extra[1])
            exch.start()
        if gather:
            gat = _Gather(refs[1 + ne:1 + ne + ns], refs[2 + 2 * ne + ns:2 + 2 * ne + 2 * ns],
                          extra[2 * bool(ne)], extra[2 * bool(ne) + 1], [b.shape[1] for b in gather])
            gat.start()
        elif ns:
            halves = _share_copies(refs[1 + ne:1 + ne + ns], refs[2 + 2 * ne + ns:2 + 2 * ne + 2 * ns],
                                   extra[2 * bool(ne)], extra[2 * bool(ne) + 1], [b.shape[0] for b in share])
            for cp in halves:
                cp.start()

        def rows(px, py, pc):
            return out_ref.at[pl.ds(pl.multiple_of((4 * px + 2 * py + pc) * m_per, 8), m_per), :]

        def copy(k, block, to, src=None):
            return pltpu.make_async_remote_copy(
                src_ref=rows(*block) if src is None else src, dst_ref=rows(*block),
                send_sem=send_sems.at[k], recv_sem=recv_sems.at[k], device_id=to, device_id_type=MESH)

        out_ref[pl.ds(pl.multiple_of((4 * x + 2 * y + c) * m_per, 8), m_per), :] = x_ref[...]
        first = [copy(0, me, sibling, src=x_ref)]
        first += [copy(1 + j, me, (*chip, c), src=x_ref) for j, chip in enumerate(chips)]
        for cp in first:
            cp.start()
        passed = [copy(4 + j, (*chip, c), sibling) for j, chip in enumerate(chips)]
        for j, chip in enumerate(chips):
            copy(1 + j, (*chip, c), me).wait_recv()
            passed[j].start()
        copy(0, sibling, me).wait_recv()
        for j, chip in enumerate(chips):
            copy(4 + j, (*chip, 1 - c), me).wait_recv()
        for cp in first + passed:
            cp.wait_send()
        if ne:
            exch.finish()
        if gather:
            gat.forward()
            gat.finish()
        elif ns:
            for cp in halves:
                cp.wait()

    res = pl.pallas_call(
        body, name=name,
        out_shape=[jax.ShapeDtypeStruct((8 * m_per, n), blk.dtype)] + _Exchange.out_shapes(exchange)
        + [jax.ShapeDtypeStruct(b.shape, b.dtype) for b in share],
        in_specs=[_vspec()] + _any_specs(ne + ns), out_specs=[_vspec()] + _any_specs(ne + ns),
        input_output_aliases={1 + ne + w: 1 + ne + w for w in range(ns)},
        scratch_shapes=[pltpu.SemaphoreType.DMA((7,)), pltpu.SemaphoreType.DMA((7,))]
        + (_Exchange.semaphores(ne) if ne else [])
        + (_Gather.semaphores(ns) if gather else
           [pltpu.SemaphoreType.DMA((ns,)), pltpu.SemaphoreType.DMA((ns,))] if ns else []),
    )(blk, *exchange, *share)
    return res[0], res[1:1 + ne], res[1 + ne:]


def _share_copies(src_refs, out_refs, send_sems, recv_sems, nrows):
    x, y, c = _position()
    return [pltpu.make_async_remote_copy(
        src_ref=_rows(src_refs[w], c, r), dst_ref=_rows(out_refs[w], c, r), send_sem=send_sems.at[w],
        recv_sem=recv_sems.at[w], device_id=(x, y, 1 - c), device_id_type=MESH) for w, r in enumerate(nrows)]


def _any_specs(n):
    return [pl.BlockSpec(memory_space=pl.ANY)] * n


def _rows(ref, half, nrows):
    return ref.at[pl.ds(half * (nrows // 2), nrows // 2), :]


class _Gather:
    def __init__(self, src_refs, out_refs, send_sems, recv_sems, nrows):
        x, y, c = _position()
        chip = 2 * x + y
        sibling = (x, y, 1 - c)

        def copy(k, src, dst, to):
            return pltpu.make_async_remote_copy(src_ref=src, dst_ref=dst, send_sem=send_sems.at[k],
                                                recv_sem=recv_sems.at[k], device_id=to, device_id_type=MESH)

        self.first, self.first_landed, self.passed, self.passed_landed = [], [], [], []
        for w, r in enumerate(nrows):
            for j, ch in enumerate([(1 - x, y), (x, 1 - y), (1 - x, 1 - y)]):
                theirs = _rows(out_refs[w].at[2 * ch[0] + ch[1]], c, r)
                other = _rows(out_refs[w].at[2 * ch[0] + ch[1]], 1 - c, r)
                self.first.append(copy(6 * w + j, _rows(src_refs[w].at[chip], c, r),
                                       _rows(out_refs[w].at[chip], c, r), (*ch, c)))
                self.first_landed.append(copy(6 * w + j, theirs, theirs, sibling))
                self.passed.append(copy(6 * w + 3 + j, theirs, theirs, sibling))
                self.passed_landed.append(copy(6 * w + 3 + j, other, other, sibling))

    in_place = True

    @staticmethod
    def semaphores(n):
        return [pltpu.SemaphoreType.DMA((6 * n,)), pltpu.SemaphoreType.DMA((6 * n,))]

    @staticmethod
    def results(bufs):
        return [jax.ShapeDtypeStruct(b.shape, b.dtype) for b in bufs]

    def start(self):
        for cp in self.first:
            cp.start()

    def forward(self):
        for landed, cp in zip(self.first_landed, self.passed):
            landed.wait_recv()
            cp.start()

    def finish(self):
        for landed in self.passed_landed:
            landed.wait_recv()
        for cp in self.first + self.passed:
            cp.wait_send()


class _Swap:
    in_place = False

    def __init__(self, grad_refs, out_refs, send_sems, recv_sems, nrows):
        x, y, c = _position()
        self.copies = [pltpu.make_async_remote_copy(
            src_ref=grad_refs[w].at[:, pl.ds((1 - c) * (r // 2), r // 2), :], dst_ref=out_refs[w],
            send_sem=send_sems.at[w], recv_sem=recv_sems.at[w], device_id=(x, y, 1 - c), device_id_type=MESH)
            for w, r in enumerate(nrows)]

    @staticmethod
    def semaphores(n):
        return [pltpu.SemaphoreType.DMA((n,)), pltpu.SemaphoreType.DMA((n,))]

    @staticmethod
    def results(grads):
        return [jax.ShapeDtypeStruct((4, g.shape[1] // 2, g.shape[2]), g.dtype) for g in grads]

    def start(self):
        for cp in self.copies:
            cp.start()

    def forward(self):
        pass

    def finish(self):
        for cp in self.copies:
            cp.wait()


def _grad_swap(grads, name):
    n = len(grads)

    def body(*refs):
        swap = _Swap(refs[:n], refs[n:2 * n], refs[2 * n], refs[2 * n + 1], [g.shape[1] for g in grads])
        swap.start()
        swap.finish()

    return pl.pallas_call(
        body, name=name, out_shape=_Swap.results(grads), in_specs=_any_specs(n), out_specs=_any_specs(n),
        scratch_shapes=_Swap.semaphores(n),
    )(*grads)


class _Exchange:
    def __init__(self, part_refs, out_refs, send_sems, recv_sems):
        x, y, c = _position()
        self.copies = [
            pltpu.make_async_remote_copy(
                src_ref=part_refs[w].at[2 * ch[0] + ch[1]], dst_ref=out_refs[w].at[k], send_sem=send_sems.at[3 * w + k],
                recv_sem=recv_sems.at[3 * w + k], device_id=(*ch, c), device_id_type=MESH)
            for w in range(len(part_refs)) for k, ch in enumerate([(1 - x, y), (x, 1 - y), (1 - x, 1 - y)])]

    @staticmethod
    def semaphores(n):
        return [pltpu.SemaphoreType.DMA((3 * n,)), pltpu.SemaphoreType.DMA((3 * n,))]

    @staticmethod
    def out_shapes(parts):
        return [jax.ShapeDtypeStruct((3,) + p.shape[1:], p.dtype) for p in parts]

    def start(self):
        for cp in self.copies:
            cp.start()

    def finish(self):
        for cp in self.copies:
            cp.wait()


def _rope_tables(s):
    rows = s // GRID_W
    inv = 10000.0 ** (-jnp.arange(0, 32, 2, dtype=F32) / 32.0)
    ang_r = jnp.arange(rows, dtype=F32)[:, None] * inv
    ang_c = jnp.arange(GRID_W, dtype=F32)[:, None] * inv
    lane = jnp.arange(128)
    first = (lane % 32) < 16
    by_row = (lane % 64) < 32

    def table(fn):
        tr = jnp.tile(fn(ang_r), (1, 8))[:, None, :]
        tc = jnp.tile(fn(ang_c), (1, 8))[None, :, :]
        return jnp.where(by_row, tr, tc).reshape(s, 128)

    cos, sin = table(jnp.cos), table(jnp.sin)
    return cos, jnp.where(first, -sin, 0.0), jnp.where(first, 0.0, sin)


def _local_step(x, ctx, tgt, mod, modc, nw1, win, qnw, knw, sink, cw, cb, cnw, cnb, wout4, nw2, wup, fcw, fcb, wdown4,
                core=None):
    s = x.shape[0]
    sh1, sc1, g1, sh2, sc2, g2 = [mod[:, D * k:D * (k + 1)] for k in range(6)]
    shc, scc = modc[:, :D], modc[:, D:2 * D]
    cos, sa, sb = _rope_tables(s)
    gi = jnp.arange(128) // 64
    gmat = (gi[:, None] == gi[None, :]).astype(_MX)
    qnw2, knw2 = jnp.tile(qnw, (1, 2)), jnp.tile(knw, (1, 2))
    cw32 = jnp.pad(cw, ((0, 1), (0, 0)))
    fcw2 = jnp.pad(fcw, ((0, 5), (0, 0))).reshape(8, 2, FH).transpose(1, 0, 2)
    fcb2 = fcb.reshape(2, 1, FH)
    winkv = win[:, 512:768]
    sinkv = sink.reshape(8)

    host = (lambda *bufs: ()) if core is None else (lambda *bufs: bufs)
    (proj, h, q0, q1, k, ksw, v, vsw, u0), got = _mix_in(x, nw1, sc1, sh1, win, qnw2, knw2, cos, sa, sb, gmat,
                                                        gather=host(wout4))
    wout = (got[0] if got else wout4).reshape(D, D)
    kvc, hc, kc, kcsw, vc, vcsw = _ctx_kv(ctx, nw1, scc, shc, winkv, knw2, gmat)
    padr = lambda a: jnp.pad(a, ((TQ, TQ), (0, 0)))
    kp, kswp, vp, vswp = padr(k), padr(ksw), padr(v), padr(vsw)
    (o, pex, pstat), got = _attn_fwd(q0, q1, kp, kswp, vp, vswp, kc, kcsw, vc, vcsw, sinkv, gather=host(wup))
    wup = got[0] if got else wup
    u1 = _conv31(u0, cw32, cb)
    x1, mix, cat = _mix_out(o, u1, cnw, cnb, wout, x, g1)
    (h2, up0), got = _ffn_up(x1, nw2, sc2, sh2, wup, gather=host(wdown4))
    wdown = (got[0] if got else wdown4).reshape(FH, D)
    act = _ffn_conv_act(up0, fcw2, fcb2)
    dy, dact, ddn, vec_dn = _ffn_down(act, wdown, x1, g2, tgt)
    loss = (0.5 / D) * jnp.sum(vec_dn[1])

    dup, gfc = _ffn_conv_bwd(up0, dact, fcw2, fcb2)
    g_wdown = _tn_matmul(act, ddn, FQ, D, "gw_down").reshape(4, FH // 4, D)
    g_wup = _tn_matmul(h2, dup, D, FQ, "gw_up", b_split=True, by_chip=True)
    dx1, dmix, vec_up = _ffn_up_bwd(dup, wup, x1, dy, mix, nw2, sc2, g1)
    g_wout = _tn_matmul(cat, dmix, D, D, "gw_out").reshape(4, D // 4, D)
    late = [g_wout, g_wup, g_wdown]
    (do, du1, vec_ln), from_sib = _mix_out_bwd(dmix, wout, u1, cnw, cnb, swap=() if core is None else late)
    parts = () if core is None else _pair_sum(late, from_sib, core, "grad_pair_sum_late")
    dga, dgb, gcw = _conv31_bwd(du1, u0, cw32, proj)
    (dq, dkp, dvp, dkc, dvc, dsink), arrived = _attn_bwd(q0, q1, kp, kswp, vp, vswp, kc, kcsw, vc, vcsw, pex, pstat, o, do,
                                                         exchange=parts)
    if core is not None:
        g_wout, g_wup, g_wdown = zip(parts, arrived)
    gwinkv, vec_ctx = _ctx_bwd(ctx, nw1, scc, winkv, kvc, hc, dkc, dvc, knw2, gmat)
    gx, dproj, vec_in = _mix_in_bwd(dq, dkp[TQ:TQ + s], dvp[TQ:TQ + s], dga, dgb, proj, x, dx1, win, nw1, sc1,
                                    qnw2, knw2, cos, sa, sb, gmat)
    g_win = _tn_matmul(h, dproj, D, INW, "gw_in", init=gwinkv)

    g_qn = vec_in[3:4, 0:128]
    g_kn = vec_in[3:4, 128:256] + vec_ctx[3:4, 0:128]
    grads = dict(
        norm_mix_w=vec_in[2:3] + vec_ctx[2:3], w_in=g_win,
        q_norm_w=g_qn[:, :64] + g_qn[:, 64:], k_norm_w=g_kn[:, :64] + g_kn[:, 64:],
        sink_logit=dsink[:, 0].reshape(1, 8), conv_w=gcw[:CK], conv_b=gcw[CK:CK + 1],
        conv_norm_w=vec_ln[1:2], conv_norm_b=vec_ln[0:1], w_out=g_wout, norm_ffn_w=vec_up[2:3], w_up=g_wup,
        ffn_conv_w=gfc[:, 0:3].transpose(1, 0, 2).reshape(3, 2 * FH), ffn_conv_b=gfc[:, 3].reshape(1, 2 * FH),
        w_down=g_wdown)
    dmod = jnp.concatenate([vec_in[0:1], vec_in[1:2], vec_up[3:4], vec_up[0:1], vec_up[1:2], vec_dn[0:1]], axis=1)
    dmodc = jnp.concatenate([vec_ctx[0:1], vec_ctx[1:2]], axis=1)
    return loss, gx, grads, dmod, dmodc


_SMALL = ["c_ctx", "b_mod", "norm_mix_w", "q_norm_w", "k_norm_w", "sink_logit", "conv_w", "conv_b", "conv_norm_w",
          "conv_norm_b", "norm_ffn_w", "ffn_conv_w", "ffn_conv_b"]
_GATHERED = ["w_in", "w_out", "w_up", "w_down"]
_BIG = ["w_mod"] + _GATHERED
_ORDER = ["c_ctx", "w_mod", "b_mod", "norm_mix_w", "w_in", "q_norm_w", "k_norm_w", "sink_logit", "conv_w", "conv_b",
          "conv_norm_w", "conv_norm_b", "w_out", "norm_ffn_w", "w_up", "ffn_conv_w", "ffn_conv_b", "w_down"]
_PACK = [("norm_mix_w", 1), ("norm_ffn_w", 1), ("q_norm_w", 1), ("k_norm_w", 1), ("sink_logit", 1), ("conv_b", 1),
         ("conv_norm_w", 1), ("conv_norm_b", 1), ("ffn_conv_b", 6), ("conv_w", 16), ("ffn_conv_w", 17)]
_PACK_ROWS = 56


def _pack_rows(a, nrows):
    flat = a.reshape(-1)
    return jnp.pad(flat, (0, nrows * D - flat.shape[0])).reshape(nrows, D)


def kernel(x, c, ctx, c_ctx, w_mod, b_mod, norm_mix_w, w_in, q_norm_w, k_norm_w, sink_logit, conv_w, conv_b, conv_norm_w, conv_norm_b, w_out, norm_ffn_w, w_up, ffn_conv_w, ffn_conv_b, w_down, loss_target, m_c_ctx, m_w_mod, m_b_mod, m_norm_mix_w, m_w_in, m_q_norm_w, m_k_norm_w, m_sink_logit, m_conv_w, m_conv_b, m_conv_norm_w, m_conv_norm_b, m_w_out, m_norm_ffn_w, m_w_up, m_ffn_conv_w, m_ffn_conv_b, m_w_down, v_c_ctx, v_w_mod, v_b_mod, v_norm_mix_w, v_w_in, v_q_norm_w, v_k_norm_w, v_sink_logit, v_conv_w, v_conv_b, v_conv_norm_w, v_conv_norm_b, v_w_out, v_norm_ffn_w, v_w_up, v_ffn_conv_w, v_ffn_conv_b, v_w_down):
    w = dict(c_ctx=c_ctx.reshape(1, D), w_mod=w_mod[0], b_mod=b_mod, norm_mix_w=norm_mix_w, w_in=w_in[0],
             q_norm_w=q_norm_w, k_norm_w=k_norm_w, sink_logit=sink_logit, conv_w=conv_w[0], conv_b=conv_b,
             conv_norm_w=conv_norm_w, conv_norm_b=conv_norm_b, w_out=w_out[0], norm_ffn_w=norm_ffn_w, w_up=w_up[0],
             ffn_conv_w=ffn_conv_w[0], ffn_conv_b=ffn_conv_b, w_down=w_down[0])
    m = dict(c_ctx=m_c_ctx.reshape(1, D), w_mod=m_w_mod[0], b_mod=m_b_mod, norm_mix_w=m_norm_mix_w, w_in=m_w_in[0],
             q_norm_w=m_q_norm_w, k_norm_w=m_k_norm_w, sink_logit=m_sink_logit, conv_w=m_conv_w[0], conv_b=m_conv_b,
             conv_norm_w=m_conv_norm_w, conv_norm_b=m_conv_norm_b, w_out=m_w_out[0], norm_ffn_w=m_norm_ffn_w,
             w_up=m_w_up[0], ffn_conv_w=m_ffn_conv_w[0], ffn_conv_b=m_ffn_conv_b, w_down=m_w_down[0])
    v = dict(c_ctx=v_c_ctx.reshape(1, D), w_mod=v_w_mod[0], b_mod=v_b_mod, norm_mix_w=v_norm_mix_w, w_in=v_w_in[0],
             q_norm_w=v_q_norm_w, k_norm_w=v_k_norm_w, sink_logit=v_sink_logit, conv_w=v_conv_w[0], conv_b=v_conv_b,
             conv_norm_w=v_conv_norm_w, conv_norm_b=v_conv_norm_b, w_out=v_w_out[0], norm_ffn_w=v_norm_ffn_w,
             w_up=v_w_up[0], ffn_conv_w=v_ffn_conv_w[0], ffn_conv_b=v_ffn_conv_b, w_down=v_w_down[0])
    xi, yi, ci = _position()
    chip = 2 * xi + yi
    dev = 2 * chip + ci
    s = x.shape[1]
    ncol = w["w_mod"].shape[1]

    win_buf, wout_buf, wup_buf, wdown_buf = _cast_shards([w[n] for n in _GATHERED], chip)

    blk0 = jnp.concatenate([_pack_rows(w["conv_w"], 4), _pack_rows(w["ffn_conv_w"], 5), c,
                            jnp.zeros((6, D), F32)], axis=0)
    g0, _, (win4,) = _small_allgather(blk0, "gather_c_convw", gather=[win_buf])
    win = win4.transpose(1, 0, 2).reshape(D, INW)
    g0 = g0.reshape(8, 16, D)
    c_all = g0[:, 9, :]
    cs = jnp.concatenate([c_all, w["c_ctx"], jnp.zeros((7, D), F32)], axis=0)
    cw_full = jnp.concatenate([g0[2 * j, 0:4].reshape(-1)[:CK * 128].reshape(CK, 128) for j in range(4)], axis=1)
    fcw_full = jnp.concatenate([g0[2 * j, 4:9].reshape(-1)[:3 * 1408].reshape(3, 1408) for j in range(4)], axis=1)

    b_loc = lax.dynamic_slice(w["b_mod"], (0, chip * ncol), (1, ncol))
    modp = _mod_fwd(cs, w["w_mod"], b_loc)
    gm = _small_allgather(modp, "gather_mod")[0].reshape(8, 16, ncol)
    mod_all = jnp.concatenate([gm[2 * j] for j in range(4)], axis=1)
    mod = lax.dynamic_slice(mod_all, (dev, 0), (1, 6 * D))
    modc = mod_all[8:9]

    loss_loc, gx, gl, dmod, dmodc = _local_step(
        x[0], ctx[0], loss_target[0], mod, modc, w["norm_mix_w"], win, w["q_norm_w"], w["k_norm_w"], w["sink_logit"],
        cw_full, w["conv_b"], w["conv_norm_w"], w["conv_norm_b"], wout_buf, w["norm_ffn_w"], wup_buf, fcw_full,
        w["ffn_conv_b"], wdown_buf, core=ci)

    last = [gl["w_in"].reshape(D, 4, INW // 4).transpose(1, 0, 2)]
    parts = _pair_sum(last, _grad_swap(last, "grad_swap_in"), ci, "grad_pair_sum_in")

    pack = jnp.concatenate([dmod.reshape(6, D), dmodc.reshape(2, D)]
                           + [_pack_rows(gl[name], nr) for name, nr in _PACK], axis=0)
    pack = jnp.pad(pack, ((0, _PACK_ROWS - pack.shape[0]), (0, 0)))
    last_row = lax.broadcasted_iota(jnp.int32, pack.shape, 0) == _PACK_ROWS - 1
    pack = jnp.where(last_row, loss_loc, pack)
    gp, arrived, _ = _small_allgather(pack, "gather_small_grads", exchange=parts)
    gp = gp.reshape(8, _PACK_ROWS, D)
    late = [gl[n] for n in ("w_out", "w_up", "w_down")]
    shard_grads = _chip_sum([*parts, *[p for p, _ in late]], [*arrived, *[a for _, a in late]], chip, ci)
    tot = _sum_leading(gp, "sum_small_grads")
    loss = tot[_PACK_ROWS - 1, 0]
    g = {}
    r = 8
    for name, nr in _PACK:
        shape = gl[name].shape
        g[name] = tot[r:r + nr].reshape(-1)[:math.prod(shape)].reshape(shape)
        r += nr
    dmod_all = jnp.concatenate([gp[:, 0:6].reshape(8, 6 * D),
                                jnp.pad(tot[6:8].reshape(1, 2 * D), ((0, 0), (0, 4 * D))),
                                jnp.zeros((7, 6 * D), F32)], axis=0)
    g["b_mod"] = _sum_leading(dmod_all.reshape(16, 1, 6 * D), "grad_b_mod")
    g["w_mod"], part = _mod_bwd(cs, lax.dynamic_slice(dmod_all, (0, chip * ncol), (16, ncol)), w["w_mod"])
    cparts, _, (g["w_in"], g["w_out"], g["w_up"], g["w_down"]) = _small_allgather(part, "gather_cctx",
                                                                                 share=shard_grads)
    g["c_ctx"] = _cctx_grad(cparts.reshape(8, 16, D)[0::2], w["c_ctx"])
    g["conv_w"] = lax.dynamic_slice(g["conv_w"], (0, chip * 128), (CK, 128))
    g["ffn_conv_w"] = lax.dynamic_slice(g["ffn_conv_w"], (0, chip * 1408), (3, 1408))

    delta, new_m, new_v = {}, {}, {}
    for name in _BIG:
        delta[name], new_m[name], new_v[name] = _adam_big(w[name], g[name], m[name], v[name], "adam_" + name)
    ds, ms, vs = _adam_small([w[n] for n in _SMALL], [g[n] for n in _SMALL], [m[n] for n in _SMALL],
                             [v[n] for n in _SMALL])
    for k, name in enumerate(_SMALL):
        delta[name], new_m[name], new_v[name] = ds[k], ms[k], vs[k]

    def shaped(d, name):
        a = d[name]
        if name == "c_ctx":
            return a.reshape(D)
        if name in ("w_mod", "w_in", "w_out", "w_up", "w_down", "conv_w", "ffn_conv_w"):
            return a[None]
        return a

    outs = [loss, gx[None]]
    for d in (g, delta, new_m, new_v):
        outs += [shaped(d, name) for name in _ORDER]
    return tuple(outs)
```

```python
import functools
import math

import jax
import jax.numpy as jnp
from jax import lax
from jax.experimental import pallas as pl
from jax.experimental.pallas import tpu as pltpu

F32 = jnp.float32
_MX = jnp.bfloat16
EPS = 1e-6
NEG = -1e30
D = 1024
AW = 512
CW = 512
INW = 1792
FH = 2816
LC = 256
CK = 31
GRID_W = 64
TM = 512
TQ = 128
NCH = FH // 128
FQ = 2 * FH // 4
VMEM_LIMIT = 56 * 1024 * 1024
MESH = pl.DeviceIdType.MESH

ADAM_LR, ADAM_B1, ADAM_B2, ADAM_EPS, ADAM_WD, ADAM_STEP = 0.001, 0.9, 0.999, 1e-08, 0.01, 10


def _cp(*sem):
    return pltpu.CompilerParams(dimension_semantics=sem, vmem_limit_bytes=VMEM_LIMIT)


def _vspec():
    return pl.BlockSpec(memory_space=pltpu.VMEM)


def _sig(z):
    return 1.0 / (1.0 + jnp.exp(-z))


def _dot(a, b):
    return jnp.dot(a, b, preferred_element_type=F32)


def _dot_nt(a, b):
    return lax.dot_general(a, b, (((1,), (1,)), ((), ())), preferred_element_type=F32)


def _dot_tn(a, b):
    return lax.dot_general(a, b, (((0,), (0,)), ((), ())), preferred_element_type=F32)


def _gsum(v, g):
    hi = v.astype(_MX)
    lo = (v - hi.astype(F32)).astype(_MX)
    return _dot(hi, g) + _dot(lo, g)


def _headnorm(ch, w, g):
    r = lax.rsqrt(_gsum(ch * ch, g) * (1.0 / 64.0) + EPS)
    return ch * r * w


def _headnorm_bwd(ch, dy, w, g):
    r = lax.rsqrt(_gsum(ch * ch, g) * (1.0 / 64.0) + EPS)
    hat = ch * r
    dhat = dy * w
    dch = r * (dhat - hat * (_gsum(dhat * hat, g) * (1.0 / 64.0)))
    return dch, dy * hat


def _rope(u, cos, sa, sb):
    return u * cos + pltpu.roll(u, 112, 1) * sa + pltpu.roll(u, 16, 1) * sb


def _rope_bwd(d, cos, sa, sb):
    return d * cos - pltpu.roll(d, 112, 1) * sa - pltpu.roll(d, 16, 1) * sb


def _rowsum(v):
    return jnp.sum(v, axis=0, keepdims=True)


def _call_hosting_gather(body, gather, step_of, nsteps, *, in_specs, out_specs, out_shape, scratch_shapes=(),
                         comm=None, **kw):
    comm = comm or _Gather
    ng, n_in, n_out = len(gather), len(in_specs), len(out_specs)
    if not ng:
        return pl.pallas_call(body, in_specs=in_specs, out_specs=out_specs, out_shape=out_shape,
                              scratch_shapes=list(scratch_shapes), **kw)

    def hosted(*refs):
        ins, outs = refs[:n_in], refs[n_in + ng:n_in + ng + n_out]
        rest = refs[n_in + 2 * ng + n_out:]
        copies = comm(refs[n_in:n_in + ng], refs[n_in + ng + n_out:n_in + 2 * ng + n_out], rest[0], rest[1],
                      [b.shape[1] for b in gather])
        step = step_of()
        pl.when(step == 0)(copies.start)
        body(*ins, *outs, *rest[2:])
        pl.when(step == max(nsteps - 3, 0))(copies.forward)
        pl.when(step == nsteps - 1)(copies.finish)

    return pl.pallas_call(
        hosted, in_specs=list(in_specs) + _any_specs(ng), out_specs=list(out_specs) + _any_specs(ng),
        out_shape=list(out_shape) + comm.results(gather),
        input_output_aliases={n_in + w: n_out + w for w in range(ng)} if comm.in_place else {},
        scratch_shapes=comm.semaphores(ng) + list(scratch_shapes), **kw)


def _mix_in(x, nw, sc, sh, win, qnw2, knw2, cos, sa, sb, gmat, gather=()):
    s = x.shape[0]

    def body(x_ref, nw_ref, sc_ref, sh_ref, win_ref, qnw_ref, knw_ref, cos_ref, sa_ref, sb_ref, g_ref,
             proj_ref, h_ref, q0_ref, q1_ref, k_ref, ksw_ref, v_ref, vsw_ref, u0_ref):
        xv = x_ref[...]
        r = lax.rsqrt(jnp.mean(xv * xv, axis=-1, keepdims=True) + EPS)
        h = xv * r * (nw_ref[...] * (1.0 + sc_ref[...])) + sh_ref[...]
        hb = h.astype(_MX)
        h_ref[...] = hb
        proj = _dot(hb, win_ref[...])
        proj_ref[...] = proj
        cs, sav, sbv, g = cos_ref[...], sa_ref[...], sb_ref[...], g_ref[...]
        lo = lax.broadcasted_iota(jnp.int32, (1, 128), 1) < 64
        for j in range(4):
            q = _rope(_headnorm(proj[:, 128 * j:128 * (j + 1)], qnw_ref[...], g), cs, sav, sbv) * 0.125
            q0_ref[:, 128 * j:128 * (j + 1)] = jnp.where(lo, q, 0.0).astype(_MX)
            q1_ref[:, 128 * j:128 * (j + 1)] = jnp.where(lo, 0.0, q).astype(_MX)
        k = _rope(_headnorm(proj[:, 512:640], knw_ref[...], g), cs, sav, sbv)
        k_ref[...] = k.astype(_MX)
        ksw_ref[...] = pltpu.roll(k, 64, 1).astype(_MX)
        v = proj[:, 640:768]
        v_ref[...] = v.astype(_MX)
        vsw_ref[...] = pltpu.roll(v, 64, 1).astype(_MX)
        u0_ref[...] = proj[:, 768:1280] * _sig(proj[:, 1280:1792])

    row = lambda n: pl.BlockSpec((1, n), lambda i: (0, 0))
    tile = lambda n: pl.BlockSpec((TM, n), lambda i: (i, 0))
    res = _call_hosting_gather(
        body, gather, lambda: pl.program_id(0), s // TM, name="mix_in", grid=(s // TM,),
        in_specs=[tile(D), row(D), row(D), row(D), pl.BlockSpec((D, INW), lambda i: (0, 0)), row(128), row(128),
                  tile(128), tile(128), tile(128), pl.BlockSpec((128, 128), lambda i: (0, 0))],
        out_specs=[tile(INW), tile(D), tile(AW), tile(AW), tile(128), tile(128), tile(128), tile(128), tile(CW)],
        out_shape=[jax.ShapeDtypeStruct((s, INW), F32), jax.ShapeDtypeStruct((s, D), _MX),
                   jax.ShapeDtypeStruct((s, AW), _MX), jax.ShapeDtypeStruct((s, AW), _MX),
                   jax.ShapeDtypeStruct((s, 128), _MX), jax.ShapeDtypeStruct((s, 128), _MX),
                   jax.ShapeDtypeStruct((s, 128), _MX), jax.ShapeDtypeStruct((s, 128), _MX),
                   jax.ShapeDtypeStruct((s, CW), F32)],
        compiler_params=_cp("arbitrary"),
    )(x, nw, sc, sh, win, qnw2, knw2, cos, sa, sb, gmat, *gather)
    return res[:9], res[9:]


def _ctx_kv(ctx, nw, scc, shc, winkv, knw2, gmat):
    def body(ctx_ref, nw_ref, sc_ref, sh_ref, w_ref, knw_ref, g_ref,
             kvc_ref, hc_ref, kc_ref, kcsw_ref, vc_ref, vcsw_ref):
        cv = ctx_ref[...]
        r = lax.rsqrt(jnp.mean(cv * cv, axis=-1, keepdims=True) + EPS)
        hc = (cv * r * (nw_ref[...] * (1.0 + sc_ref[...])) + sh_ref[...]).astype(_MX)
        hc_ref[...] = hc
        kvc = _dot(hc, w_ref[...])
        kvc_ref[...] = kvc
        kc = _headnorm(kvc[:, :128], knw_ref[...], g_ref[...])
        kc_ref[...] = kc.astype(_MX)
        kcsw_ref[...] = pltpu.roll(kc, 64, 1).astype(_MX)
        vc = kvc[:, 128:]
        vc_ref[...] = vc.astype(_MX)
        vcsw_ref[...] = pltpu.roll(vc, 64, 1).astype(_MX)

    return pl.pallas_call(
        body, name="ctx_kv",
        in_specs=[_vspec()] * 7, out_specs=[_vspec()] * 6,
        out_shape=[jax.ShapeDtypeStruct((LC, 256), F32), jax.ShapeDtypeStruct((LC, D), _MX)]
        + [jax.ShapeDtypeStruct((LC, 128), _MX)] * 4,
        compiler_params=pltpu.CompilerParams(vmem_limit_bytes=VMEM_LIMIT),
    )(ctx, nw, scc, shc, winkv, knw2, gmat)


def _attn_mask(i, s):
    r = lax.broadcasted_iota(jnp.int32, (2 * TQ, LC + 3 * TQ), 0) % TQ
    cidx = lax.broadcasted_iota(jnp.int32, (2 * TQ, LC + 3 * TQ), 1)
    qpos = i * TQ + r
    kpos = (i - 1) * TQ + (cidx - LC)
    near = (jnp.abs(qpos - kpos) <= 128) & (kpos >= 0) & (kpos < s)
    return (cidx < LC) | near


def _attn_bias(i, s):
    return jnp.where(_attn_mask(i, s), 0.0, NEG)


def _attn_exp(sc, bias, sinkv):
    sc = sc + bias
    m = jnp.maximum(jnp.max(sc, axis=-1, keepdims=True), sinkv)
    ex = jnp.exp(sc - m)
    es = jnp.exp(sinkv - m)
    return ex, es, 1.0 / (jnp.sum(ex, axis=-1, keepdims=True) + es)


def _sink_rows(sink_ref, g, e):
    return jnp.concatenate([jnp.full((TQ, 1), sink_ref[4 * g + e], F32),
                            jnp.full((TQ, 1), sink_ref[4 * g + 2 + e], F32)], axis=0)


def _attn_fwd(q0, q1, kp, kswp, vp, vswp, kc, kcsw, vc, vcsw, sink, gather=()):
    s = q0.shape[0]
    nb = s // TQ

    def body(q0_ref, q1_ref, kp_ref, kswp_ref, vp_ref, vswp_ref, kc_ref, kcsw_ref, vc_ref, vcsw_ref, sink_ref,
             o_ref, ex_ref, stat_ref):
        i = pl.program_id(0)
        st = pl.multiple_of(i * TQ, TQ)
        kall = (jnp.concatenate([kc_ref[...], kp_ref[pl.ds(st, 3 * TQ), :]], axis=0),
                jnp.concatenate([kcsw_ref[...], kswp_ref[pl.ds(st, 3 * TQ), :]], axis=0))
        vall = (jnp.concatenate([vc_ref[...], vp_ref[pl.ds(st, 3 * TQ), :]], axis=0),
                jnp.concatenate([vcsw_ref[...], vswp_ref[pl.ds(st, 3 * TQ), :]], axis=0))
        bias = _attn_bias(i, s)
        lo = lax.broadcasted_iota(jnp.int32, (1, 128), 1) < 64
        qrefs = (q0_ref, q1_ref)
        out = [jnp.zeros((TQ, 128), F32) for _ in range(4)]
        combos = [(g, e) for g in range(2) for e in range(2)]
        scores = [_dot_nt(jnp.concatenate([qrefs[e][:, 256 * g:256 * g + 128],
                                           qrefs[e][:, 256 * g + 128:256 * g + 256]], axis=0),
                          kall[0 if e == g else 1]) for g, e in combos]
        parts = []
        for n, ((g, e), sc) in enumerate(zip(combos, scores)):
            ex, es, inv = _attn_exp(sc, bias, _sink_rows(sink_ref, g, e))
            parts.append((ex.astype(_MX), inv))
            ex_ref[n] = parts[-1][0]
            stat_ref[n] = jnp.where(lo, inv, es * inv)
        for (g, e), (ex, inv) in zip(combos, parts):
            me = lo if e == 0 else jnp.logical_not(lo)
            o2 = _dot(ex, vall[0 if e == g else 1]) * inv
            out[2 * g] = out[2 * g] + jnp.where(me, o2[:TQ], 0.0)
            out[2 * g + 1] = out[2 * g + 1] + jnp.where(me, o2[TQ:], 0.0)
        for j in range(4):
            o_ref[:, 128 * j:128 * (j + 1)] = out[j]

    full = lambda a: pl.BlockSpec(a.shape, lambda i: (0, 0))
    qs = pl.BlockSpec((TQ, AW), lambda i: (i, 0))
    res = _call_hosting_gather(
        body, gather, lambda: pl.program_id(0), nb, name="attn_fwd", grid=(nb,),
        in_specs=[qs, qs, full(kp), full(kp), full(kp), full(kp), full(kc), full(kc), full(kc), full(kc),
                  pl.BlockSpec(memory_space=pltpu.SMEM)],
        out_specs=[qs, pl.BlockSpec((None, 4, 2 * TQ, LC + 3 * TQ), lambda i: (i, 0, 0, 0)),
                   pl.BlockSpec((None, 4, 2 * TQ, 128), lambda i: (i, 0, 0, 0))],
        out_shape=[jax.ShapeDtypeStruct((s, AW), F32), jax.ShapeDtypeStruct((nb, 4, 2 * TQ, LC + 3 * TQ), _MX),
                   jax.ShapeDtypeStruct((nb, 4, 2 * TQ, 128), F32)],
        compiler_params=_cp("arbitrary"),
    )(q0, q1, kp, kswp, vp, vswp, kc, kcsw, vc, vcsw, sink, *gather)
    return res[:3], res[3:]


def _conv31(u0, cw32, cb):
    s = u0.shape[0]
    rch = 256

    def body(u_ref, w_ref, b_ref, o_ref, pad_ref):
        pad_ref[0:16, :] = jnp.zeros((16, 128), F32)
        pad_ref[s + 16:s + 32, :] = jnp.zeros((16, 128), F32)
        pad_ref[16:s + 16, :] = u_ref[...]
        for cidx in range(s // rch):
            base = cidx * rch
            acc = jnp.zeros((rch, 128), F32) + b_ref[...]
            for j in range(CK):
                acc = acc + w_ref[j:j + 1, :] * pad_ref[base + j + 1:base + j + 1 + rch, :]
            o_ref[base:base + rch, :] = acc

    return pl.pallas_call(
        body, name="conv31", grid=(CW // 128,),
        in_specs=[pl.BlockSpec((s, 128), lambda c: (0, c)), pl.BlockSpec((32, 128), lambda c: (0, c)),
                  pl.BlockSpec((1, 128), lambda c: (0, c))],
        out_specs=pl.BlockSpec((s, 128), lambda c: (0, c)),
        out_shape=jax.ShapeDtypeStruct((s, CW), F32),
        scratch_shapes=[pltpu.VMEM((s + 32, 128), F32)],
        compiler_params=_cp("arbitrary"),
    )(u0, cw32, cb)


def _ln_stats(u1):
    mu = jnp.mean(u1, axis=-1, keepdims=True)
    xc = u1 - mu
    rstd = lax.rsqrt(jnp.mean(xc * xc, axis=-1, keepdims=True) + EPS)
    return xc * rstd, rstd


def _mix_out(o, u1, cnw, cnb, wout, x, g1):
    s = x.shape[0]

    def body(o_ref, u1_ref, cnw_ref, cnb_ref, w_ref, x_ref, g1_ref, x1_ref, mix_ref, cat_ref):
        u2n, _ = _ln_stats(u1_ref[...])
        u2 = u2n * cnw_ref[...] + cnb_ref[...]
        u3 = u2 * _sig(u2)
        cat = jnp.concatenate([o_ref[...], u3], axis=1).astype(_MX)
        cat_ref[...] = cat
        mix = _dot(cat, w_ref[...])
        mix_ref[...] = mix
        x1_ref[...] = x_ref[...] + g1_ref[...] * mix

    row = lambda n: pl.BlockSpec((1, n), lambda i: (0, 0))
    tile = lambda n: pl.BlockSpec((TM, n), lambda i: (i, 0))
    return pl.pallas_call(
        body, name="mix_out", grid=(s // TM,),
        in_specs=[tile(AW), tile(CW), row(CW), row(CW), pl.BlockSpec((D, D), lambda i: (0, 0)), tile(D), row(D)],
        out_specs=[tile(D), tile(D), tile(D)],
        out_shape=[jax.ShapeDtypeStruct((s, D), F32), jax.ShapeDtypeStruct((s, D), F32),
                   jax.ShapeDtypeStruct((s, D), _MX)],
        compiler_params=_cp("arbitrary"),
    )(o, u1, cnw, cnb, wout, x, g1)


def _ffn_up(x1, nw, sc, sh, wup4, gather=()):
    s = x1.shape[0]
    tm = min(1024, s)

    def body(x_ref, nw_ref, sc_ref, sh_ref, w_ref, h2_ref, up_ref, h2s):
        @pl.when(pl.program_id(1) == 0)
        def _():
            xv = x_ref[...]
            r = lax.rsqrt(jnp.mean(xv * xv, axis=-1, keepdims=True) + EPS)
            h2 = (xv * r * (nw_ref[...] * (1.0 + sc_ref[...])) + sh_ref[...]).astype(_MX)
            h2s[...] = h2
            h2_ref[...] = h2
        up_ref[...] = _dot(h2s[...], w_ref[...])

    row = pl.BlockSpec((1, D), lambda i, j: (0, 0))
    res = _call_hosting_gather(
        body, gather, lambda: pl.program_id(0) * 4 + pl.program_id(1), 4 * (s // tm), name="ffn_up",
        grid=(s // tm, 4),
        in_specs=[pl.BlockSpec((tm, D), lambda i, j: (i, 0)), row, row, row,
                  pl.BlockSpec((None, D, FQ), lambda i, j: (j, 0, 0))],
        out_specs=[pl.BlockSpec((tm, D), lambda i, j: (i, 0)),
                   pl.BlockSpec((None, tm, FQ), lambda i, j: (j // 2, i, j % 2))],
        out_shape=[jax.ShapeDtypeStruct((s, D), _MX), jax.ShapeDtypeStruct((2, s, FH), F32)],
        scratch_shapes=[pltpu.VMEM((tm, D), _MX)],
        compiler_params=_cp("arbitrary", "arbitrary"),
    )(x1, nw, sc, sh, wup4, *gather)
    return res[:2], res[2:]


def _fill_pad8(pad_ref, val, s):
    pad_ref[0:8, :] = jnp.zeros((8, 128), F32)
    pad_ref[s + 8:s + 16, :] = jnp.zeros((8, 128), F32)
    pad_ref[8:s + 8, :] = val


def _conv3_at(pad_ref, w_ref, half, base, rch):
    return (w_ref[half, 0:1, :] * pad_ref[base + 7:base + 7 + rch, :]
            + w_ref[half, 1:2, :] * pad_ref[base + 8:base + 8 + rch, :]
            + w_ref[half, 2:3, :] * pad_ref[base + 9:base + 9 + rch, :])


def _ffn_conv_act(up0, fcw, fcb):
    s = up0.shape[1]
    rch = 256

    def body(up_ref, w_ref, b_ref, act_ref, padg, padv):
        _fill_pad8(padg, up_ref[0], s)
        _fill_pad8(padv, up_ref[1], s)
        for cidx in range(s // rch):
            base = cidx * rch
            gate = _conv3_at(padg, w_ref, 0, base, rch) + b_ref[0]
            val = _conv3_at(padv, w_ref, 1, base, rch) + b_ref[1]
            act_ref[base:base + rch, :] = (gate * _sig(gate) * val).astype(_MX)

    return pl.pallas_call(
        body, name="ffn_conv_act", grid=(NCH,),
        in_specs=[pl.BlockSpec((2, s, 128), lambda c: (0, 0, c)), pl.BlockSpec((2, 8, 128), lambda c: (0, 0, c)),
                  pl.BlockSpec((2, 1, 128), lambda c: (0, 0, c))],
        out_specs=pl.BlockSpec((s, 128), lambda c: (0, c)),
        out_shape=jax.ShapeDtypeStruct((s, FH), _MX),
        scratch_shapes=[pltpu.VMEM((s + 16, 128), F32)] * 2,
        compiler_params=_cp("arbitrary"),
    )(up0, fcw, fcb)


def _ffn_down(act, wdown, x1, g2, tgt):
    s = x1.shape[0]

    def body(act_ref, w_ref, x1_ref, g2_ref, tgt_ref, dy_ref, dact_ref, ddn_ref, vec_ref):
        @pl.when(pl.program_id(0) == 0)
        def _():
            vec_ref[...] = jnp.zeros((8, D), F32)
        dn = _dot(act_ref[...], w_ref[...])
        diff = x1_ref[...] + g2_ref[...] * dn - tgt_ref[...]
        dy = diff * (1.0 / D)
        dy_ref[...] = dy
        ddn = (dy * g2_ref[...]).astype(_MX)
        ddn_ref[...] = ddn
        dact_ref[...] = _dot_nt(ddn, w_ref[...])
        vec_ref[0:1, :] += _rowsum(dy * dn)
        vec_ref[1:2, :] += _rowsum(diff * diff)

    tile = lambda n: pl.BlockSpec((TM, n), lambda i: (i, 0))
    return pl.pallas_call(
        body, name="ffn_down", grid=(s // TM,),
        in_specs=[tile(FH), pl.BlockSpec((FH, D), lambda i: (0, 0)), tile(D), pl.BlockSpec((1, D), lambda i: (0, 0)),
                  tile(D)],
        out_specs=[tile(D), tile(FH), tile(D), pl.BlockSpec((8, D), lambda i: (0, 0))],
        out_shape=[jax.ShapeDtypeStruct((s, D), F32), jax.ShapeDtypeStruct((s, FH), F32),
                   jax.ShapeDtypeStruct((s, D), _MX), jax.ShapeDtypeStruct((8, D), F32)],
        compiler_params=_cp("arbitrary"),
    )(act, wdown, x1, g2, tgt)


def _ffn_conv_bwd(up0, dact, fcw, fcb):
    s = up0.shape[1]
    rch = 256

    def body(up_ref, da_ref, w_ref, b_ref, dup_ref, gw_ref, padg, padv, dpg, dpv):
        _fill_pad8(padg, up_ref[0], s)
        _fill_pad8(padv, up_ref[1], s)
        for p in (dpg, dpv):
            p[0:8, :] = jnp.zeros((8, 128), F32)
            p[s + 8:s + 16, :] = jnp.zeros((8, 128), F32)
        acc = [[jnp.zeros((1, 128), F32) for _ in range(4)] for _ in range(2)]
        for cidx in range(s // rch):
            base = cidx * rch
            gate = _conv3_at(padg, w_ref, 0, base, rch) + b_ref[0]
            val = _conv3_at(padv, w_ref, 1, base, rch) + b_ref[1]
            da = da_ref[base:base + rch, :]
            sg = _sig(gate)
            dgate = da * val * sg * (1.0 + gate * (1.0 - sg))
            dval = da * gate * sg
            dpg[base + 8:base + 8 + rch, :] = dgate
            dpv[base + 8:base + 8 + rch, :] = dval
            for half, (dd, pad) in enumerate(((dgate, padg), (dval, padv))):
                for j in range(3):
                    acc[half][j] = acc[half][j] + _rowsum(dd * pad[base + 7 + j:base + 7 + j + rch, :])
                acc[half][3] = acc[half][3] + _rowsum(dd)
        for half in range(2):
            gw_ref[half] = jnp.zeros((8, 128), F32)
            for j in range(4):
                gw_ref[half, j:j + 1, :] = acc[half][j]
        for cidx in range(s // rch):
            base = cidx * rch
            for half, dp in enumerate((dpg, dpv)):
                dup_ref[half, base:base + rch, :] = (
                    w_ref[half, 0:1, :] * dp[base + 9:base + 9 + rch, :]
                    + w_ref[half, 1:2, :] * dp[base + 8:base + 8 + rch, :]
                    + w_ref[half, 2:3, :] * dp[base + 7:base + 7 + rch, :]).astype(_MX)

    return pl.pallas_call(
        body, name="ffn_conv_bwd", grid=(NCH,),
        in_specs=[pl.BlockSpec((2, s, 128), lambda c: (0, 0, c)), pl.BlockSpec((s, 128), lambda c: (0, c)),
                  pl.BlockSpec((2, 8, 128), lambda c: (0, 0, c)), pl.BlockSpec((2, 1, 128), lambda c: (0, 0, c))],
        out_specs=[pl.BlockSpec((2, s, 128), lambda c: (0, 0, c)), pl.BlockSpec((2, 8, 128), lambda c: (0, 0, c))],
        out_shape=[jax.ShapeDtypeStruct((2, s, FH), _MX), jax.ShapeDtypeStruct((2, 8, FH), F32)],
        scratch_shapes=[pltpu.VMEM((s + 16, 128), F32)] * 4,
        compiler_params=_cp("arbitrary"),
    )(up0, dact, fcw, fcb)


def _tn_matmul(a, b, tm, tn, name, b_split=False, by_chip=False, init=None):
    s, m = a.shape
    n = 2 * b.shape[2] if b_split else b.shape[1]
    ts = min(1024, s)
    nsteps = s // ts
    npb = (n // 2) // tn if b_split else None

    def body(*refs):
        if init is None:
            a_ref, b_ref, o_ref, acc = refs
        else:
            a_ref, b_ref, i_ref, o_ref, acc = refs
        k = pl.program_id(2)

        @pl.when(k == 0)
        def _():
            acc[...] = jnp.zeros((tm, tn), F32)
            if init is not None:
                acc[:, 512:768] = i_ref[...]
        acc[...] += _dot_tn(a_ref[...], b_ref[...])

        @pl.when(k == nsteps - 1)
        def _():
            o_ref[...] = acc[...].astype(_MX)

    if b_split:
        bspec = pl.BlockSpec((None, ts, tn), lambda i, j, k: (j // npb, k, j % npb))
    else:
        bspec = pl.BlockSpec((ts, tn), lambda i, j, k: (k, j))
    in_specs = [pl.BlockSpec((ts, tm), lambda i, j, k: (k, i)), bspec]
    args = [a, b]
    if init is not None:
        in_specs.append(pl.BlockSpec((tm, 256), lambda i, j, k: (i, 0)))
        args.append(init)
    if by_chip:
        out_spec = pl.BlockSpec((None, tm, tn), lambda i, j, k: (j, i, 0))
        out_shape = jax.ShapeDtypeStruct((n // tn, m, tn), _MX)
    else:
        out_spec = pl.BlockSpec((tm, tn), lambda i, j, k: (i, j))
        out_shape = jax.ShapeDtypeStruct((m, n), _MX)
    return pl.pallas_call(
        body, name=name, grid=(m // tm, n // tn, nsteps),
        in_specs=in_specs, out_specs=out_spec, out_shape=out_shape,
        scratch_shapes=[pltpu.VMEM((tm, tn), F32)],
        compiler_params=_cp("arbitrary", "arbitrary", "arbitrary"),
    )(*args)


def _ffn_up_bwd(dup, wup4, x1, dy, mix, nw, sc, g1):
    s = x1.shape[0]
    nk = 4

    def body(dup_ref, w_ref, x1_ref, dy_ref, mix_ref, nw_ref, sc_ref, g1_ref, dx1_ref, dmix_ref, vec_ref, acc):
        i, k = pl.program_id(0), pl.program_id(1)

        @pl.when(k == 0)
        def _():
            acc[...] = jnp.zeros((TM, D), F32)

        @pl.when((k == 0) & (i == 0))
        def _():
            vec_ref[...] = jnp.zeros((8, D), F32)
        acc[...] += _dot_nt(dup_ref[...], w_ref[...])

        @pl.when(k == nk - 1)
        def _():
            dh = acc[...]
            xv = x1_ref[...]
            r = lax.rsqrt(jnp.mean(xv * xv, axis=-1, keepdims=True) + EPS)
            xn = xv * r
            nwv, scv = nw_ref[...], sc_ref[...]
            vec_ref[0:1, :] += _rowsum(dh)
            vec_ref[1:2, :] += _rowsum(dh * xn) * nwv
            vec_ref[2:3, :] += _rowsum(dh * xn) * (1.0 + scv)
            dxn = dh * (nwv * (1.0 + scv))
            dx1 = dy_ref[...] + r * (dxn - xn * jnp.mean(dxn * xn, axis=-1, keepdims=True))
            dx1_ref[...] = dx1
            vec_ref[3:4, :] += _rowsum(dx1 * mix_ref[...])
            dmix_ref[...] = (dx1 * g1_ref[...]).astype(_MX)

    tile = pl.BlockSpec((TM, D), lambda i, k: (i, 0))
    row = pl.BlockSpec((1, D), lambda i, k: (0, 0))
    return pl.pallas_call(
        body, name="ffn_up_bwd", grid=(s // TM, nk),
        in_specs=[pl.BlockSpec((None, TM, FQ), lambda i, k: (k // 2, i, k % 2)),
                  pl.BlockSpec((None, D, FQ), lambda i, k: (k, 0, 0)), tile, tile, tile, row, row, row],
        out_specs=[tile, tile, pl.BlockSpec((8, D), lambda i, k: (0, 0))],
        out_shape=[jax.ShapeDtypeStruct((s, D), F32), jax.ShapeDtypeStruct((s, D), _MX),
                   jax.ShapeDtypeStruct((8, D), F32)],
        scratch_shapes=[pltpu.VMEM((TM, D), F32)],
        compiler_params=_cp("arbitrary", "arbitrary"),
    )(dup, wup4, x1, dy, mix, nw, sc, g1)


def _mix_out_bwd(dmix, wout, u1, cnw, cnb, swap=()):
    s = u1.shape[0]

    def body(dm_ref, w_ref, u1_ref, cnw_ref, cnb_ref, do_ref, du1_ref, vec_ref):
        @pl.when(pl.program_id(0) == 0)
        def _():
            vec_ref[...] = jnp.zeros((8, CW), F32)
        dcat = _dot_nt(dm_ref[...], w_ref[...])
        do_ref[...] = dcat[:, :AW]
        du3 = dcat[:, AW:]
        u2n, rstd = _ln_stats(u1_ref[...])
        u2 = u2n * cnw_ref[...] + cnb_ref[...]
        sg = _sig(u2)
        du2 = du3 * sg * (1.0 + u2 * (1.0 - sg))
        vec_ref[0:1, :] += _rowsum(du2)
        vec_ref[1:2, :] += _rowsum(du2 * u2n)
        d2n = du2 * cnw_ref[...]
        du1_ref[...] = rstd * (d2n - jnp.mean(d2n, axis=-1, keepdims=True)
                               - u2n * jnp.mean(d2n * u2n, axis=-1, keepdims=True))

    row = lambda n: pl.BlockSpec((1, n), lambda i: (0, 0))
    tile = lambda n: pl.BlockSpec((TM, n), lambda i: (i, 0))
    res = _call_hosting_gather(
        body, swap, lambda: pl.program_id(0), s // TM, comm=_Swap, name="mix_out_bwd", grid=(s // TM,),
        in_specs=[tile(D), pl.BlockSpec((D, D), lambda i: (0, 0)), tile(CW), row(CW), row(CW)],
        out_specs=[tile(AW), tile(CW), pl.BlockSpec((8, CW), lambda i: (0, 0))],
        out_shape=[jax.ShapeDtypeStruct((s, AW), F32), jax.ShapeDtypeStruct((s, CW), F32),
                   jax.ShapeDtypeStruct((8, CW), F32)],
        compiler_params=_cp("arbitrary"),
    )(dmix, wout, u1, cnw, cnb, *swap)
    return res[:3], res[3:]


def _conv31_bwd(du1, u0, cw32, proj, exchange=()):
    s = u0.shape[0]
    rch = 256

    def body(d_ref, u_ref, w_ref, ga_ref, gb_ref, dga_ref, dgb_ref, gw_ref, padu, padd):
        for p, src in ((padu, u_ref), (padd, d_ref)):
            p[0:16, :] = jnp.zeros((16, 128), F32)
            p[s + 16:s + 32, :] = jnp.zeros((16, 128), F32)
            p[16:s + 16, :] = src[...]
        for j in range(CK):
            acc = jnp.zeros((1, 128), F32)
            for cidx in range(s // rch):
                base = cidx * rch
                acc = acc + _rowsum(d_ref[base:base + rch, :] * padu[base + j + 1:base + j + 1 + rch, :])
            gw_ref[j:j + 1, :] = acc
        gw_ref[CK:CK + 1, :] = _rowsum(d_ref[...])
        for cidx in range(s // rch):
            base = cidx * rch
            du0 = jnp.zeros((rch, 128), F32)
            for j in range(CK):
                du0 = du0 + w_ref[j:j + 1, :] * padd[base + 31 - j:base + 31 - j + rch, :]
            sg = _sig(gb_ref[base:base + rch, :])
            ga = ga_ref[base:base + rch, :]
            dga_ref[base:base + rch, :] = (du0 * sg).astype(_MX)
            dgb_ref[base:base + rch, :] = (du0 * ga * sg * (1.0 - sg)).astype(_MX)

    blk = lambda off: pl.BlockSpec((s, 128), lambda c: (0, c + off))
    res = _call_hosting_gather(
        body, exchange, lambda: pl.program_id(0), CW // 128, comm=_Exchange, name="conv31_bwd", grid=(CW // 128,),
        in_specs=[blk(0), blk(0), pl.BlockSpec((32, 128), lambda c: (0, c)), blk(6), blk(10)],
        out_specs=[blk(0), blk(0), pl.BlockSpec((32, 128), lambda c: (0, c))],
        out_shape=[jax.ShapeDtypeStruct((s, CW), _MX), jax.ShapeDtypeStruct((s, CW), _MX),
                   jax.ShapeDtypeStruct((32, CW), F32)],
        scratch_shapes=[pltpu.VMEM((s + 32, 128), F32)] * 2,
        compiler_params=_cp("arbitrary"),
    )(du1, u0, cw32, proj, proj, *exchange)
    return res[:3], res[3:]


def _attn_bwd(q0, q1, kp, kswp, vp, vswp, kc, kcsw, vc, vcsw, ex, stat, o, do, exchange=()):
    s = q0.shape[0]
    nb = s // TQ
    ne = len(exchange)

    def body(*refs):
        (q0_ref, q1_ref, kp_ref, kswp_ref, vp_ref, vswp_ref, kc_ref, kcsw_ref, vc_ref, vcsw_ref, ex_ref, stat_ref,
         o_ref, do_ref) = refs[:14]
        dq_ref, dk_ref, dv_ref, dkc_ref, dvc_ref, dsink_ref = refs[14 + ne:20 + ne]
        i = pl.program_id(0)
        if ne:
            exch = _Exchange(refs[14:14 + ne], refs[20 + ne:20 + 2 * ne], refs[20 + 2 * ne], refs[21 + 2 * ne])
            pl.when(i == 0)(exch.start)

        @pl.when(i == 0)
        def _():
            dk_ref[...] = jnp.zeros((s + 2 * TQ, 128), F32)
            dv_ref[...] = jnp.zeros((s + 2 * TQ, 128), F32)
            dkc_ref[...] = jnp.zeros((LC, 128), F32)
            dvc_ref[...] = jnp.zeros((LC, 128), F32)
            dsink_ref[...] = jnp.zeros((8, 128), F32)
        st = pl.multiple_of(i * TQ, TQ)
        kall = (jnp.concatenate([kc_ref[...], kp_ref[pl.ds(st, 3 * TQ), :]], axis=0),
                jnp.concatenate([kcsw_ref[...], kswp_ref[pl.ds(st, 3 * TQ), :]], axis=0))
        vall = (jnp.concatenate([vc_ref[...], vp_ref[pl.ds(st, 3 * TQ), :]], axis=0),
                jnp.concatenate([vcsw_ref[...], vswp_ref[pl.ds(st, 3 * TQ), :]], axis=0))
        lo = lax.broadcasted_iota(jnp.int32, (1, 128), 1) < 64
        qrefs = (q0_ref, q1_ref)
        dq = [jnp.zeros((TQ, 128), F32) for _ in range(4)]
        dkt = jnp.zeros((128, LC + 3 * TQ), F32)
        dvt = jnp.zeros((128, LC + 3 * TQ), F32)
        combos = [(g, e) for g in range(2) for e in range(2)]

        def stacked(ref, g):
            return jnp.concatenate([ref[:, 256 * g:256 * g + 128], ref[:, 256 * g + 128:256 * g + 256]], axis=0)

        for n, (g, e) in enumerate(combos):
            me = lo if e == 0 else jnp.logical_not(lo)
            sw = 0 if e == g else 1
            qm = stacked(qrefs[e], g)
            p = ex_ref[n].astype(F32) * stat_ref[n, :, 0:1]
            dom = jnp.where(me, stacked(do_ref, g), 0.0)
            dd = jnp.sum(dom * stacked(o_ref, g), axis=-1, keepdims=True)
            domx = dom.astype(_MX)
            sd = stat_ref[n, :, 64:65] * dd
            dsink_ref[4 * g + e:4 * g + e + 1, :] -= jnp.sum(sd[:TQ], axis=0, keepdims=True)
            dsink_ref[4 * g + 2 + e:4 * g + 3 + e, :] -= jnp.sum(sd[TQ:], axis=0, keepdims=True)
            ds = (p * (_dot_nt(domx, vall[sw]) - dd)).astype(_MX)
            dq2 = _dot(ds, kall[sw])
            dq[2 * g] = dq[2 * g] + jnp.where(me, dq2[:TQ], 0.0)
            dq[2 * g + 1] = dq[2 * g + 1] + jnp.where(me, dq2[TQ:], 0.0)
            dk2 = _dot_tn(qm, ds)
            dv2 = _dot_tn(domx, p.astype(_MX))
            if sw:
                dk2 = pltpu.roll(dk2, 64, 0)
                dv2 = pltpu.roll(dv2, 64, 0)
            dkt = dkt + dk2
            dvt = dvt + dv2
        dkt = dkt.T
        dvt = dvt.T
        for j in range(4):
            dq_ref[:, 128 * j:128 * (j + 1)] = dq[j] * 0.125
        dkc_ref[...] += dkt[:LC]
        dvc_ref[...] += dvt[:LC]
        dk_ref[pl.ds(st, 3 * TQ), :] += dkt[LC:]
        dv_ref[pl.ds(st, 3 * TQ), :] += dvt[LC:]
        if ne:
            pl.when(i == nb - 1)(exch.finish)

    full = lambda a: pl.BlockSpec(a.shape, lambda i: (0, 0))
    fs = lambda r: pl.BlockSpec((r, 128), lambda i: (0, 0))
    qs = pl.BlockSpec((TQ, AW), lambda i: (i, 0))
    res = pl.pallas_call(
        body, name="attn_bwd", grid=(nb,),
        in_specs=[qs, qs, full(kp), full(kp), full(kp), full(kp), full(kc), full(kc), full(kc), full(kc),
                  pl.BlockSpec((None, 4, 2 * TQ, LC + 3 * TQ), lambda i: (i, 0, 0, 0)),
                  pl.BlockSpec((None, 4, 2 * TQ, 128), lambda i: (i, 0, 0, 0)), qs, qs] + _any_specs(ne),
        out_specs=[qs, fs(s + 2 * TQ), fs(s + 2 * TQ), fs(LC), fs(LC), fs(8)] + _any_specs(ne),
        out_shape=[jax.ShapeDtypeStruct((s, AW), F32), jax.ShapeDtypeStruct((s + 2 * TQ, 128), F32),
                   jax.ShapeDtypeStruct((s + 2 * TQ, 128), F32), jax.ShapeDtypeStruct((LC, 128), F32),
                   jax.ShapeDtypeStruct((LC, 128), F32), jax.ShapeDtypeStruct((8, 128), F32)]
        + _Exchange.out_shapes(exchange),
        scratch_shapes=_Exchange.semaphores(ne) if ne else [],
        compiler_params=_cp("arbitrary"),
    )(q0, q1, kp, kswp, vp, vswp, kc, kcsw, vc, vcsw, ex, stat, o, do, *exchange)
    return res[:6], res[6:]


def _mix_in_bwd(dq, dk, dv, dga, dgb, proj, x, dx1, win, nw, sc, qnw2, knw2, cos, sa, sb, gmat):
    s = x.shape[0]

    def body(dq_ref, dk_ref, dv_ref, dga_ref, dgb_ref, proj_ref, x_ref, dx1_ref, win_ref, nw_ref, sc_ref,
             qnw_ref, knw_ref, cos_ref, sa_ref, sb_ref, g_ref, gx_ref, dproj_ref, vec_ref):
        @pl.when(pl.program_id(0) == 0)
        def _():
            vec_ref[...] = jnp.zeros((8, D), F32)
        cs, sav, sbv, g = cos_ref[...], sa_ref[...], sb_ref[...], g_ref[...]
        gq = jnp.zeros((1, 128), F32)
        for j in range(4):
            dqn = _rope_bwd(dq_ref[:, 128 * j:128 * (j + 1)], cs, sav, sbv)
            dch, gw = _headnorm_bwd(proj_ref[:, 128 * j:128 * (j + 1)], dqn, qnw_ref[...], g)
            dproj_ref[:, 128 * j:128 * (j + 1)] = dch.astype(_MX)
            gq = gq + _rowsum(gw)
        dkn = _rope_bwd(dk_ref[...], cs, sav, sbv)
        dch, gw = _headnorm_bwd(proj_ref[:, 512:640], dkn, knw_ref[...], g)
        dproj_ref[:, 512:640] = dch.astype(_MX)
        dproj_ref[:, 640:768] = dv_ref[...].astype(_MX)
        dproj_ref[:, 768:1280] = dga_ref[...]
        dproj_ref[:, 1280:1792] = dgb_ref[...]
        vec_ref[3:4, 0:128] += gq
        vec_ref[3:4, 128:256] += _rowsum(gw)
        dh = _dot_nt(dproj_ref[...], win_ref[...])
        xv = x_ref[...]
        r = lax.rsqrt(jnp.mean(xv * xv, axis=-1, keepdims=True) + EPS)
        xn = xv * r
        nwv, scv = nw_ref[...], sc_ref[...]
        vec_ref[0:1, :] += _rowsum(dh)
        vec_ref[1:2, :] += _rowsum(dh * xn) * nwv
        vec_ref[2:3, :] += _rowsum(dh * xn) * (1.0 + scv)
        dxn = dh * (nwv * (1.0 + scv))
        gx_ref[...] = dx1_ref[...] + r * (dxn - xn * jnp.mean(dxn * xn, axis=-1, keepdims=True))

    row = lambda n: pl.BlockSpec((1, n), lambda i: (0, 0))
    tile = lambda n: pl.BlockSpec((TM, n), lambda i: (i, 0))
    return pl.pallas_call(
        body, name="mix_in_bwd", grid=(s // TM,),
        in_specs=[tile(AW), tile(128), tile(128), tile(CW), tile(CW), tile(INW), tile(D), tile(D),
                  pl.BlockSpec((D, INW), lambda i: (0, 0)), row(D), row(D), row(128), row(128),
                  tile(128), tile(128), tile(128), pl.BlockSpec((128, 128), lambda i: (0, 0))],
        out_specs=[tile(D), tile(INW), pl.BlockSpec((8, D), lambda i: (0, 0))],
        out_shape=[jax.ShapeDtypeStruct((s, D), F32), jax.ShapeDtypeStruct((s, INW), _MX),
                   jax.ShapeDtypeStruct((8, D), F32)],
        compiler_params=_cp("arbitrary"),
    )(dq, dk, dv, dga, dgb, proj, x, dx1, win, nw, sc, qnw2, knw2, cos, sa, sb, gmat)


def _ctx_bwd(ctx, nw, scc, winkv, kvc, hc, dkc, dvc, knw2, gmat):
    def body(ctx_ref, nw_ref, sc_ref, w_ref, kvc_ref, hc_ref, dkc_ref, dvc_ref, knw_ref, g_ref, gw_ref, vec_ref):
        dkr, gk = _headnorm_bwd(kvc_ref[:, 0:128], dkc_ref[...], knw_ref[...], g_ref[...])
        dkv = jnp.concatenate([dkr, dvc_ref[...]], axis=1).astype(_MX)
        gw_ref[...] = _dot_tn(hc_ref[...], dkv)
        dh = _dot_nt(dkv, w_ref[...])
        cv = ctx_ref[...]
        r = lax.rsqrt(jnp.mean(cv * cv, axis=-1, keepdims=True) + EPS)
        cn = cv * r
        vec_ref[...] = jnp.zeros((8, D), F32)
        vec_ref[0:1, :] = _rowsum(dh)
        vec_ref[1:2, :] = _rowsum(dh * cn) * nw_ref[...]
        vec_ref[2:3, :] = _rowsum(dh * cn) * (1.0 + sc_ref[...])
        vec_ref[3:4, 0:128] = _rowsum(gk)

    return pl.pallas_call(
        body, name="ctx_bwd", in_specs=[_vspec()] * 10, out_specs=[_vspec()] * 2,
        out_shape=[jax.ShapeDtypeStruct((D, 256), F32), jax.ShapeDtypeStruct((8, D), F32)],
        compiler_params=pltpu.CompilerParams(vmem_limit_bytes=VMEM_LIMIT),
    )(ctx, nw, scc, winkv, kvc, hc, dkc, dvc, knw2, gmat)


def _mod_fwd(cs, wmod, bloc):
    def body(c_ref, w_ref, b_ref, o_ref):
        cv = c_ref[...]
        o_ref[...] = _dot((cv * _sig(cv)).astype(_MX), w_ref[...].astype(_MX)) + b_ref[...]

    return pl.pallas_call(
        body, name="mod_fwd", in_specs=[_vspec()] * 3, out_specs=_vspec(),
        out_shape=jax.ShapeDtypeStruct((16, wmod.shape[1]), F32),
        compiler_params=pltpu.CompilerParams(vmem_limit_bytes=VMEM_LIMIT),
    )(cs, wmod, bloc)


def _mod_bwd(cs, dmod, wmod):
    def body(c_ref, d_ref, w_ref, gw_ref, part_ref):
        cv = c_ref[...]
        sl = (cv * _sig(cv)).astype(_MX)
        dm = d_ref[...].astype(_MX)
        gw_ref[...] = _dot_tn(sl, dm)
        part_ref[...] = _dot_nt(dm, w_ref[...].astype(_MX))

    return pl.pallas_call(
        body, name="mod_bwd", in_specs=[_vspec()] * 3, out_specs=[_vspec()] * 2,
        out_shape=[jax.ShapeDtypeStruct(wmod.shape, F32), jax.ShapeDtypeStruct((16, D), F32)],
        compiler_params=pltpu.CompilerParams(vmem_limit_bytes=VMEM_LIMIT),
    )(cs, dmod, wmod)


def _sum_leading(a, name):
    n = a.shape[0]

    def body(a_ref, o_ref):
        acc = a_ref[0]
        for k in range(1, n):
            acc = acc + a_ref[k]
        o_ref[...] = acc

    return pl.pallas_call(
        body, name=name, in_specs=[_vspec()], out_specs=_vspec(),
        out_shape=jax.ShapeDtypeStruct(a.shape[1:], F32),
        compiler_params=pltpu.CompilerParams(vmem_limit_bytes=VMEM_LIMIT),
    )(a)


def _cctx_grad(parts, cc):
    def body(p_ref, c_ref, o_ref):
        acc = p_ref[0, 8:9, :]
        for k in range(1, 4):
            acc = acc + p_ref[k, 8:9, :]
        cv = c_ref[...]
        sg = _sig(cv)
        o_ref[...] = acc * (sg * (1.0 + cv * (1.0 - sg)))

    return pl.pallas_call(
        body, name="cctx_grad", in_specs=[_vspec()] * 2, out_specs=_vspec(),
        out_shape=jax.ShapeDtypeStruct((1, D), F32),
    )(parts, cc)


def _adam_math(w, g, m, v):
    mn = ADAM_B1 * m + (1.0 - ADAM_B1) * g
    vn = ADAM_B2 * v + (1.0 - ADAM_B2) * (g * g)
    mh = mn / (1.0 - ADAM_B1 ** ADAM_STEP)
    vh = vn / (1.0 - ADAM_B2 ** ADAM_STEP)
    delta = -ADAM_LR * (mh / (jnp.sqrt(vh) + ADAM_EPS) + ADAM_WD * w)
    return delta, mn, vn


def _adam_big(w, g, m, v, name):
    r, n = w.shape
    tr = 256 if r % 256 == 0 else 64

    def body(w_ref, g_ref, m_ref, v_ref, d_ref, mo_ref, vo_ref):
        d, mn, vn = _adam_math(w_ref[...], g_ref[...], m_ref[...], v_ref[...])
        d_ref[...] = d
        mo_ref[...] = mn
        vo_ref[...] = vn

    spec = pl.BlockSpec((tr, n), lambda i: (i, 0))
    return pl.pallas_call(
        body, name=name, grid=(r // tr,), in_specs=[spec] * 4, out_specs=[spec] * 3,
        out_shape=[jax.ShapeDtypeStruct((r, n), F32)] * 3,
        compiler_params=_cp("arbitrary"),
    )(w, g, m, v)


def _adam_small(ws, gs, ms, vs):
    n = len(ws)

    def body(*refs):
        ins, outs = refs[:4 * n], refs[4 * n:]
        for k in range(n):
            d, mn, vn = _adam_math(ins[k][...], ins[n + k][...], ins[2 * n + k][...], ins[3 * n + k][...])
            outs[k][...] = d
            outs[n + k][...] = mn
            outs[2 * n + k][...] = vn

    shapes = [jax.ShapeDtypeStruct(w.shape, F32) for w in ws]
    res = pl.pallas_call(
        body, name="adam_small", in_specs=[_vspec()] * (4 * n), out_specs=[_vspec()] * (3 * n),
        out_shape=shapes * 3,
    )(*ws, *gs, *ms, *vs)
    return res[:n], res[n:2 * n], res[2 * n:]


def _pair_sum(grads, from_sib, core, name):
    n = len(grads)

    def body(c_ref, *refs):
        for w in range(n):
            a_ref, b_ref, o_ref = refs[w], refs[n + w], refs[2 * n + w]
            o_ref[...] = (a_ref[...].astype(F32) + b_ref[...].astype(F32)).astype(_MX)

    halves = [(None, g.shape[1] // 2, g.shape[2]) for g in grads]
    return pl.pallas_call(
        body, name=name,
        grid_spec=pltpu.PrefetchScalarGridSpec(
            num_scalar_prefetch=1, grid=(4,),
            in_specs=[pl.BlockSpec(h, lambda j, c: (j, c[0], 0)) for h in halves]
            + [pl.BlockSpec(h, lambda j, c: (j, 0, 0)) for h in halves],
            out_specs=[pl.BlockSpec(h, lambda j, c: (j, 0, 0)) for h in halves]),
        out_shape=[jax.ShapeDtypeStruct(p.shape, _MX) for p in from_sib],
        compiler_params=_cp("arbitrary"),
    )(core.reshape(1), *grads, *from_sib)


def _chip_sum(parts, arrived, chip, core):
    n = len(parts)

    def body(s_ref, *refs):
        for w in range(n):
            own_ref, p_ref, o_ref = refs[w], refs[n + w], refs[2 * n + w]
            acc = own_ref[...].astype(F32)
            for k in range(3):
                acc = acc + p_ref[k].astype(F32)
            o_ref[...] = acc

    blk = [(p.shape[1] // 2, p.shape[2]) for p in parts]
    return pl.pallas_call(
        body, name="grad_chip_sum",
        grid_spec=pltpu.PrefetchScalarGridSpec(
            num_scalar_prefetch=1, grid=(2,),
            in_specs=[pl.BlockSpec((None,) + b, lambda i, s: (s[0], i, 0)) for b in blk]
            + [pl.BlockSpec((3,) + b, lambda i, s: (0, i, 0)) for b in blk],
            out_specs=[pl.BlockSpec(b, lambda i, s: (2 * s[1] + i, 0)) for b in blk]),
        out_shape=[jax.ShapeDtypeStruct((2 * p.shape[1], p.shape[2]), F32) for p in parts],
        compiler_params=_cp("arbitrary"),
    )(jnp.stack([chip, core]), *parts, *arrived)


def _cast_shards(ws, chip):
    n = len(ws)

    def body(s_ref, *refs):
        for w in range(n):
            refs[n + w][...] = refs[w][...].astype(_MX)

    blk = [(a.shape[0] // 2, a.shape[1]) for a in ws]
    return pl.pallas_call(
        body, name="cast_shards",
        grid_spec=pltpu.PrefetchScalarGridSpec(
            num_scalar_prefetch=1, grid=(2,),
            in_specs=[pl.BlockSpec(b, lambda i, s: (i, 0)) for b in blk],
            out_specs=[pl.BlockSpec((None,) + b, lambda i, s: (s[0], i, 0)) for b in blk]),
        out_shape=[jax.ShapeDtypeStruct((4,) + a.shape, _MX) for a in ws],
        compiler_params=_cp("arbitrary"),
    )(chip.reshape(1), *ws)


def _position():
    x, y, c = lax.axis_index("x"), lax.axis_index("y"), lax.axis_index("c")
    return x, y, c


def _small_allgather(blk, name, exchange=(), share=(), gather=()):
    m_per, n = blk.shape
    assert not (share and gather)
    share = tuple(share) + tuple(gather)
    ne, ns = len(exchange), len(share)

    def body(*refs):
        x_ref = refs[0]
        out_ref = refs[1 + ne + ns]
        send_sems, recv_sems = refs[2 + 2 * ne + 2 * ns], refs[3 + 2 * ne + 2 * ns]
        extra = refs[4 + 2 * ne + 2 * ns:]
        x, y, c = _position()
        me, sibling = (x, y, c), (x, y, 1 - c)
        chips = [(1 - x, y), (x, 1 - y), (1 - x, 1 - y)]
        if ne:
            exch = _Exchange(refs[1:1 + ne], refs[2 + ne + ns:2 + 2 * ne + ns], extra[0], extra[1])
            exch.start()
        if gather:
            gat = _Gather(refs[1 + ne:1 + ne + ns], refs[2 + 2 * ne + ns:2 + 2 * ne + 2 * ns],
                          extra[2 * bool(ne)], extra[2 * bool(ne) + 1], [b.shape[1] for b in gather])
            gat.start()
        elif ns:
            halves = _share_copies(refs[1 + ne:1 + ne + ns], refs[2 + 2 * ne + ns:2 + 2 * ne + 2 * ns],
                                   extra[2 * bool(ne)], extra[2 * bool(ne) + 1], [b.shape[0] for b in share])
            for cp in halves:
                cp.start()

        def rows(px, py, pc):
            return out_ref.at[pl.ds(pl.multiple_of((4 * px + 2 * py + pc) * m_per, 8), m_per), :]

        def copy(k, block, to, src=None):
            return pltpu.make_async_remote_copy(
                src_ref=rows(*block) if src is None else src, dst_ref=rows(*block),
                send_sem=send_sems.at[k], recv_sem=recv_sems.at[k], device_id=to, device_id_type=MESH)

        out_ref[pl.ds(pl.multiple_of((4 * x + 2 * y + c) * m_per, 8), m_per), :] = x_ref[...]
        first = [copy(0, me, sibling, src=x_ref)]
        first += [copy(1 + j, me, (*chip, c), src=x_ref) for j, chip in enumerate(chips)]
        for cp in first:
            cp.start()
        passed = [copy(4 + j, (*chip, c), sibling) for j, chip in enumerate(chips)]
        for j, chip in enumerate(chips):
            copy(1 + j, (*chip, c), me).wait_recv()
            passed[j].start()
        copy(0, sibling, me).wait_recv()
        for j, chip in enumerate(chips):
            copy(4 + j, (*chip, 1 - c), me).wait_recv()
        for cp in first + passed:
            cp.wait_send()
        if ne:
            exch.finish()
        if gather:
            gat.forward()
            gat.finish()
        elif ns:
            for cp in halves:
                cp.wait()

    res = pl.pallas_call(
        body, name=name,
        out_shape=[jax.ShapeDtypeStruct((8 * m_per, n), blk.dtype)] + _Exchange.out_shapes(exchange)
        + [jax.ShapeDtypeStruct(b.shape, b.dtype) for b in share],
        in_specs=[_vspec()] + _any_specs(ne + ns), out_specs=[_vspec()] + _any_specs(ne + ns),
        input_output_aliases={1 + ne + w: 1 + ne + w for w in range(ns)},
        scratch_shapes=[pltpu.SemaphoreType.DMA((7,)), pltpu.SemaphoreType.DMA((7,))]
        + (_Exchange.semaphores(ne) if ne else [])
        + (_Gather.semaphores(ns) if gather else
           [pltpu.SemaphoreType.DMA((ns,)), pltpu.SemaphoreType.DMA((ns,))] if ns else []),
    )(blk, *exchange, *share)
    return res[0], res[1:1 + ne], res[1 + ne:]


def _share_copies(src_refs, out_refs, send_sems, recv_sems, nrows):
    x, y, c = _position()
    return [pltpu.make_async_remote_copy(
        src_ref=_rows(src_refs[w], c, r), dst_ref=_rows(out_refs[w], c, r), send_sem=send_sems.at[w],
        recv_sem=recv_sems.at[w], device_id=(x, y, 1 - c), device_id_type=MESH) for w, r in enumerate(nrows)]


def _any_specs(n):
    return [pl.BlockSpec(memory_space=pl.ANY)] * n


def _rows(ref, half, nrows):
    return ref.at[pl.ds(half * (nrows // 2), nrows // 2), :]


class _Gather:
    def __init__(self, src_refs, out_refs, send_sems, recv_sems, nrows):
        x, y, c = _position()
        chip = 2 * x + y
        sibling = (x, y, 1 - c)

        def copy(k, src, dst, to):
            return pltpu.make_async_remote_copy(src_ref=src, dst_ref=dst, send_sem=send_sems.at[k],
                                                recv_sem=recv_sems.at[k], device_id=to, device_id_type=MESH)

        self.first, self.first_landed, self.passed, self.passed_landed = [], [], [], []
        for w, r in enumerate(nrows):
            for j, ch in enumerate([(1 - x, y), (x, 1 - y), (1 - x, 1 - y)]):
                theirs = _rows(out_refs[w].at[2 * ch[0] + ch[1]], c, r)
                other = _rows(out_refs[w].at[2 * ch[0] + ch[1]], 1 - c, r)
                self.first.append(copy(6 * w + j, _rows(src_refs[w].at[chip], c, r),
                                       _rows(out_refs[w].at[chip], c, r), (*ch, c)))
                self.first_landed.append(copy(6 * w + j, theirs, theirs, sibling))
                self.passed.append(copy(6 * w + 3 + j, theirs, theirs, sibling))
                self.passed_landed.append(copy(6 * w + 3 + j, other, other, sibling))

    in_place = True

    @staticmethod
    def semaphores(n):
        return [pltpu.SemaphoreType.DMA((6 * n,)), pltpu.SemaphoreType.DMA((6 * n,))]

    @staticmethod
    def results(bufs):
        return [jax.ShapeDtypeStruct(b.shape, b.dtype) for b in bufs]

    def start(self):
        for cp in self.first:
            cp.start()

    def forward(self):
        for landed, cp in zip(self.first_landed, self.passed):
            landed.wait_recv()
            cp.start()

    def finish(self):
        for landed in self.passed_landed:
            landed.wait_recv()
        for cp in self.first + self.passed:
            cp.wait_send()


class _Swap:
    in_place = False

    def __init__(self, grad_refs, out_refs, send_sems, recv_sems, nrows):
        x, y, c = _position()
        self.copies = [pltpu.make_async_remote_copy(
            src_ref=grad_refs[w].at[:, pl.ds((1 - c) * (r // 2), r // 2), :], dst_ref=out_refs[w],
            send_sem=send_sems.at[w], recv_sem=recv_sems.at[w], device_id=(x, y, 1 - c), device_id_type=MESH)
            for w, r in enumerate(nrows)]

    @staticmethod
    def semaphores(n):
        return [pltpu.SemaphoreType.DMA((n,)), pltpu.SemaphoreType.DMA((n,))]

    @staticmethod
    def results(grads):
        return [jax.ShapeDtypeStruct((4, g.shape[1] // 2, g.shape[2]), g.dtype) for g in grads]

    def start(self):
        for cp in self.copies:
            cp.start()

    def forward(self):
        pass

    def finish(self):
        for cp in self.copies:
            cp.wait()


def _grad_swap(grads, name):
    n = len(grads)

    def body(*refs):
        swap = _Swap(refs[:n], refs[n:2 * n], refs[2 * n], refs[2 * n + 1], [g.shape[1] for g in grads])
        swap.start()
        swap.finish()

    return pl.pallas_call(
        body, name=name, out_shape=_Swap.results(grads), in_specs=_any_specs(n), out_specs=_any_specs(n),
        scratch_shapes=_Swap.semaphores(n),
    )(*grads)


class _Exchange:
    in_place = False

    def __init__(self, part_refs, out_refs, send_sems, recv_sems, nrows=None):
        x, y, c = _position()
        self.copies = [
            pltpu.make_async_remote_copy(
                src_ref=part_refs[w].at[2 * ch[0] + ch[1]], dst_ref=out_refs[w].at[k], send_sem=send_sems.at[3 * w + k],
                recv_sem=recv_sems.at[3 * w + k], device_id=(*ch, c), device_id_type=MESH)
            for w in range(len(part_refs)) for k, ch in enumerate([(1 - x, y), (x, 1 - y), (1 - x, 1 - y)])]

    @staticmethod
    def semaphores(n):
        return [pltpu.SemaphoreType.DMA((3 * n,)), pltpu.SemaphoreType.DMA((3 * n,))]

    @staticmethod
    def out_shapes(parts):
        return [jax.ShapeDtypeStruct((3,) + p.shape[1:], p.dtype) for p in parts]

    results = out_shapes

    def start(self):
        for cp in self.copies:
            cp.start()

    def forward(self):
        pass

    def finish(self):
        for cp in self.copies:
            cp.wait()


def _rope_tables(s):
    rows = s // GRID_W
    inv = 10000.0 ** (-jnp.arange(0, 32, 2, dtype=F32) / 32.0)
    ang_r = jnp.arange(rows, dtype=F32)[:, None] * inv
    ang_c = jnp.arange(GRID_W, dtype=F32)[:, None] * inv
    lane = jnp.arange(128)
    first = (lane % 32) < 16
    by_row = (lane % 64) < 32

    def table(fn):
        tr = jnp.tile(fn(ang_r), (1, 8))[:, None, :]
        tc = jnp.tile(fn(ang_c), (1, 8))[None, :, :]
        return jnp.where(by_row, tr, tc).reshape(s, 128)

    cos, sin = table(jnp.cos), table(jnp.sin)
    return cos, jnp.where(first, -sin, 0.0), jnp.where(first, 0.0, sin)


def _local_step(x, ctx, tgt, mod, modc, nw1, win, qnw, knw, sink, cw, cb, cnw, cnb, wout4, nw2, wup, fcw, fcb, wdown4,
                core=None):
    s = x.shape[0]
    sh1, sc1, g1, sh2, sc2, g2 = [mod[:, D * k:D * (k + 1)] for k in range(6)]
    shc, scc = modc[:, :D], modc[:, D:2 * D]
    cos, sa, sb = _rope_tables(s)
    gi = jnp.arange(128) // 64
    gmat = (gi[:, None] == gi[None, :]).astype(_MX)
    qnw2, knw2 = jnp.tile(qnw, (1, 2)), jnp.tile(knw, (1, 2))
    cw32 = jnp.pad(cw, ((0, 1), (0, 0)))
    fcw2 = jnp.pad(fcw, ((0, 5), (0, 0))).reshape(8, 2, FH).transpose(1, 0, 2)
    fcb2 = fcb.reshape(2, 1, FH)
    winkv = win[:, 512:768]
    sinkv = sink.reshape(8)

    host = (lambda *bufs: ()) if core is None else (lambda *bufs: bufs)
    (proj, h, q0, q1, k, ksw, v, vsw, u0), got = _mix_in(x, nw1, sc1, sh1, win, qnw2, knw2, cos, sa, sb, gmat,
                                                        gather=host(wout4))
    wout = (got[0] if got else wout4).reshape(D, D)
    kvc, hc, kc, kcsw, vc, vcsw = _ctx_kv(ctx, nw1, scc, shc, winkv, knw2, gmat)
    padr = lambda a: jnp.pad(a, ((TQ, TQ), (0, 0)))
    kp, kswp, vp, vswp = padr(k), padr(ksw), padr(v), padr(vsw)
    (o, pex, pstat), got = _attn_fwd(q0, q1, kp, kswp, vp, vswp, kc, kcsw, vc, vcsw, sinkv, gather=host(wup))
    wup = got[0] if got else wup
    u1 = _conv31(u0, cw32, cb)
    x1, mix, cat = _mix_out(o, u1, cnw, cnb, wout, x, g1)
    (h2, up0), got = _ffn_up(x1, nw2, sc2, sh2, wup, gather=host(wdown4))
    wdown = (got[0] if got else wdown4).reshape(FH, D)
    act = _ffn_conv_act(up0, fcw2, fcb2)
    dy, dact, ddn, vec_dn = _ffn_down(act, wdown, x1, g2, tgt)
    loss = (0.5 / D) * jnp.sum(vec_dn[1])

    dup, gfc = _ffn_conv_bwd(up0, dact, fcw2, fcb2)
    g_wdown = _tn_matmul(act, ddn, FQ, D, "gw_down").reshape(4, FH // 4, D)
    g_wup = _tn_matmul(h2, dup, D, FQ, "gw_up", b_split=True, by_chip=True)
    dx1, dmix, vec_up = _ffn_up_bwd(dup, wup, x1, dy, mix, nw2, sc2, g1)
    g_wout = _tn_matmul(cat, dmix, D, D, "gw_out").reshape(4, D // 4, D)
    late = [g_wout, g_wup, g_wdown]
    (do, du1, vec_ln), from_sib = _mix_out_bwd(dmix, wout, u1, cnw, cnb, swap=() if core is None else late)
    parts = () if core is None else _pair_sum(late, from_sib, core, "grad_pair_sum_late")
    (dga, dgb, gcw), arrived_a = _conv31_bwd(du1, u0, cw32, proj, exchange=parts[0:1] + parts[2:3])
    (dq, dkp, dvp, dkc, dvc, dsink), arrived_b = _attn_bwd(q0, q1, kp, kswp, vp, vswp, kc, kcsw, vc, vcsw, pex, pstat,
                                                           o, do, exchange=parts[1:2])
    if core is not None:
        g_wout, g_wup, g_wdown = zip(parts, [arrived_a[0], arrived_b[0], arrived_a[1]])
    gwinkv, vec_ctx = _ctx_bwd(ctx, nw1, scc, winkv, kvc, hc, dkc, dvc, knw2, gmat)
    gx, dproj, vec_in = _mix_in_bwd(dq, dkp[TQ:TQ + s], dvp[TQ:TQ + s], dga, dgb, proj, x, dx1, win, nw1, sc1,
                                    qnw2, knw2, cos, sa, sb, gmat)
    g_win = _tn_matmul(h, dproj, D, INW, "gw_in", init=gwinkv)

    g_qn = vec_in[3:4, 0:128]
    g_kn = vec_in[3:4, 128:256] + vec_ctx[3:4, 0:128]
    grads = dict(
        norm_mix_w=vec_in[2:3] + vec_ctx[2:3], w_in=g_win,
        q_norm_w=g_qn[:, :64] + g_qn[:, 64:], k_norm_w=g_kn[:, :64] + g_kn[:, 64:],
        sink_logit=dsink[:, 0].reshape(1, 8), conv_w=gcw[:CK], conv_b=gcw[CK:CK + 1],
        conv_norm_w=vec_ln[1:2], conv_norm_b=vec_ln[0:1], w_out=g_wout, norm_ffn_w=vec_up[2:3], w_up=g_wup,
        ffn_conv_w=gfc[:, 0:3].transpose(1, 0, 2).reshape(3, 2 * FH), ffn_conv_b=gfc[:, 3].reshape(1, 2 * FH),
        w_down=g_wdown)
    dmod = jnp.concatenate([vec_in[0:1], vec_in[1:2], vec_up[3:4], vec_up[0:1], vec_up[1:2], vec_dn[0:1]], axis=1)
    dmodc = jnp.concatenate([vec_ctx[0:1], vec_ctx[1:2]], axis=1)
    return loss, gx, grads, dmod, dmodc


_SMALL = ["c_ctx", "b_mod", "norm_mix_w", "q_norm_w", "k_norm_w", "sink_logit", "conv_w", "conv_b", "conv_norm_w",
          "conv_norm_b", "norm_ffn_w", "ffn_conv_w", "ffn_conv_b"]
_GATHERED = ["w_in", "w_out", "w_up", "w_down"]
_BIG = ["w_mod"] + _GATHERED
_ORDER = ["c_ctx", "w_mod", "b_mod", "norm_mix_w", "w_in", "q_norm_w", "k_norm_w", "sink_logit", "conv_w", "conv_b",
          "conv_norm_w", "conv_norm_b", "w_out", "norm_ffn_w", "w_up", "ffn_conv_w", "ffn_conv_b", "w_down"]
_PACK = [("norm_mix_w", 1), ("norm_ffn_w", 1), ("q_norm_w", 1), ("k_norm_w", 1), ("sink_logit", 1), ("conv_b", 1),
         ("conv_norm_w", 1), ("conv_norm_b", 1), ("ffn_conv_b", 6), ("conv_w", 16), ("ffn_conv_w", 17)]
_PACK_ROWS = 56


def _pack_rows(a, nrows):
    flat = a.reshape(-1)
    return jnp.pad(flat, (0, nrows * D - flat.shape[0])).reshape(nrows, D)


def kernel(x, c, ctx, c_ctx, w_mod, b_mod, norm_mix_w, w_in, q_norm_w, k_norm_w, sink_logit, conv_w, conv_b, conv_norm_w, conv_norm_b, w_out, norm_ffn_w, w_up, ffn_conv_w, ffn_conv_b, w_down, loss_target, m_c_ctx, m_w_mod, m_b_mod, m_norm_mix_w, m_w_in, m_q_norm_w, m_k_norm_w, m_sink_logit, m_conv_w, m_conv_b, m_conv_norm_w, m_conv_norm_b, m_w_out, m_norm_ffn_w, m_w_up, m_ffn_conv_w, m_ffn_conv_b, m_w_down, v_c_ctx, v_w_mod, v_b_mod, v_norm_mix_w, v_w_in, v_q_norm_w, v_k_norm_w, v_sink_logit, v_conv_w, v_conv_b, v_conv_norm_w, v_conv_norm_b, v_w_out, v_norm_ffn_w, v_w_up, v_ffn_conv_w, v_ffn_conv_b, v_w_down):
    w = dict(c_ctx=c_ctx.reshape(1, D), w_mod=w_mod[0], b_mod=b_mod, norm_mix_w=norm_mix_w, w_in=w_in[0],
             q_norm_w=q_norm_w, k_norm_w=k_norm_w, sink_logit=sink_logit, conv_w=conv_w[0], conv_b=conv_b,
             conv_norm_w=conv_norm_w, conv_norm_b=conv_norm_b, w_out=w_out[0], norm_ffn_w=norm_ffn_w, w_up=w_up[0],
             ffn_conv_w=ffn_conv_w[0], ffn_conv_b=ffn_conv_b, w_down=w_down[0])
    m = dict(c_ctx=m_c_ctx.reshape(1, D), w_mod=m_w_mod[0], b_mod=m_b_mod, norm_mix_w=m_norm_mix_w, w_in=m_w_in[0],
             q_norm_w=m_q_norm_w, k_norm_w=m_k_norm_w, sink_logit=m_sink_logit, conv_w=m_conv_w[0], conv_b=m_conv_b,
             conv_norm_w=m_conv_norm_w, conv_norm_b=m_conv_norm_b, w_out=m_w_out[0], norm_ffn_w=m_norm_ffn_w,
             w_up=m_w_up[0], ffn_conv_w=m_ffn_conv_w[0], ffn_conv_b=m_ffn_conv_b, w_down=m_w_down[0])
    v = dict(c_ctx=v_c_ctx.reshape(1, D), w_mod=v_w_mod[0], b_mod=v_b_mod, norm_mix_w=v_norm_mix_w, w_in=v_w_in[0],
             q_norm_w=v_q_norm_w, k_norm_w=v_k_norm_w, sink_logit=v_sink_logit, conv_w=v_conv_w[0], conv_b=v_conv_b,
             conv_norm_w=v_conv_norm_w, conv_norm_b=v_conv_norm_b, w_out=v_w_out[0], norm_ffn_w=v_norm_ffn_w,
             w_up=v_w_up[0], ffn_conv_w=v_ffn_conv_w[0], ffn_conv_b=v_ffn_conv_b, w_down=v_w_down[0])
    xi, yi, ci = _position()
    chip = 2 * xi + yi
    dev = 2 * chip + ci
    s = x.shape[1]
    ncol = w["w_mod"].shape[1]

    win_buf, wout_buf, wup_buf, wdown_buf = _cast_shards([w[n] for n in _GATHERED], chip)

    blk0 = jnp.concatenate([_pack_rows(w["conv_w"], 4), _pack_rows(w["ffn_conv_w"], 5), c,
                            jnp.zeros((6, D), F32)], axis=0)
    g0, _, (win4,) = _small_allgather(blk0, "gather_c_convw", gather=[win_buf])
    win = win4.transpose(1, 0, 2).reshape(D, INW)
    g0 = g0.reshape(8, 16, D)
    c_all = g0[:, 9, :]
    cs = jnp.concatenate([c_all, w["c_ctx"], jnp.zeros((7, D), F32)], axis=0)
    cw_full = jnp.concatenate([g0[2 * j, 0:4].reshape(-1)[:CK * 128].reshape(CK, 128) for j in range(4)], axis=1)
    fcw_full = jnp.concatenate([g0[2 * j, 4:9].reshape(-1)[:3 * 1408].reshape(3, 1408) for j in range(4)], axis=1)

    b_loc = lax.dynamic_slice(w["b_mod"], (0, chip * ncol), (1, ncol))
    modp = _mod_fwd(cs, w["w_mod"], b_loc)
    gm = _small_allgather(modp, "gather_mod")[0].reshape(8, 16, ncol)
    mod_all = jnp.concatenate([gm[2 * j] for j in range(4)], axis=1)
    mod = lax.dynamic_slice(mod_all, (dev, 0), (1, 6 * D))
    modc = mod_all[8:9]

    loss_loc, gx, gl, dmod, dmodc = _local_step(
        x[0], ctx[0], loss_target[0], mod, modc, w["norm_mix_w"], win, w["q_norm_w"], w["k_norm_w"], w["sink_logit"],
        cw_full, w["conv_b"], w["conv_norm_w"], w["conv_norm_b"], wout_buf, w["norm_ffn_w"], wup_buf, fcw_full,
        w["ffn_conv_b"], wdown_buf, core=ci)

    last = [gl["w_in"].reshape(D, 4, INW // 4).transpose(1, 0, 2)]
    parts = _pair_sum(last, _grad_swap(last, "grad_swap_in"), ci, "grad_pair_sum_in")

    pack = jnp.concatenate([dmod.reshape(6, D), dmodc.reshape(2, D)]
                           + [_pack_rows(gl[name], nr) for name, nr in _PACK], axis=0)
    pack = jnp.pad(pack, ((0, _PACK_ROWS - pack.shape[0]), (0, 0)))
    last_row = lax.broadcasted_iota(jnp.int32, pack.shape, 0) == _PACK_ROWS - 1
    pack = jnp.where(last_row, loss_loc, pack)
    gp, arrived, _ = _small_allgather(pack, "gather_small_grads", exchange=parts)
    gp = gp.reshape(8, _PACK_ROWS, D)
    late = [gl[n] for n in ("w_out", "w_up", "w_down")]
    shard_grads = _chip_sum([*parts, *[p for p, _ in late]], [*arrived, *[a for _, a in late]], chip, ci)
    tot = _sum_leading(gp, "sum_small_grads")
    loss = tot[_PACK_ROWS - 1, 0]
    g = {}
    r = 8
    for name, nr in _PACK:
        shape = gl[name].shape
        g[name] = tot[r:r + nr].reshape(-1)[:math.prod(shape)].reshape(shape)
        r += nr
    dmod_all = jnp.concatenate([gp[:, 0:6].reshape(8, 6 * D),
                                jnp.pad(tot[6:8].reshape(1, 2 * D), ((0, 0), (0, 4 * D))),
                                jnp.zeros((7, 6 * D), F32)], axis=0)
    g["b_mod"] = _sum_leading(dmod_all.reshape(16, 1, 6 * D), "grad_b_mod")
    g["w_mod"], part = _mod_bwd(cs, lax.dynamic_slice(dmod_all, (0, chip * ncol), (16, ncol)), w["w_mod"])
    cparts, _, (g["w_in"], g["w_out"], g["w_up"], g["w_down"]) = _small_allgather(part, "gather_cctx",
                                                                                 share=shard_grads)
    g["c_ctx"] = _cctx_grad(cparts.reshape(8, 16, D)[0::2], w["c_ctx"])
    g["conv_w"] = lax.dynamic_slice(g["conv_w"], (0, chip * 128), (CK, 128))
    g["ffn_conv_w"] = lax.dynamic_slice(g["ffn_conv_w"], (0, chip * 1408), (3, 1408))

    delta, new_m, new_v = {}, {}, {}
    for name in _BIG:
        delta[name], new_m[name], new_v[name] = _adam_big(w[name], g[name], m[name], v[name], "adam_" + name)
    ds, ms, vs = _adam_small([w[n] for n in _SMALL], [g[n] for n in _SMALL], [m[n] for n in _SMALL],
                             [v[n] for n in _SMALL])
    for k, name in enumerate(_SMALL):
        delta[name], new_m[name], new_v[name] = ds[k], ms[k], vs[k]

    def shaped(d, name):
        a = d[name]
        if name == "c_ctx":
            return a.reshape(D)
        if name in ("w_mod", "w_in", "w_out", "w_up", "w_down", "conv_w", "ffn_conv_w"):
            return a[None]
        return a

    outs = [loss, gx[None]]
    for d in (g, delta, new_m, new_v):
        outs += [shaped(d, name) for name in _ORDER]
    return tuple(outs)
```

```python
import functools
import math

import jax
import jax.numpy as jnp
from jax import lax
from jax.experimental import pallas as pl
from jax.experimental.pallas import tpu as pltpu

F32 = jnp.float32
_MX = jnp.bfloat16
EPS = 1e-6
NEG = -1e30
D = 1024
AW = 512
CW = 512
INW = 1792
FH = 2816
LC = 256
CK = 31
GRID_W = 64
TM = 512
TQ = 128
NCH = FH // 128
FQ = 2 * FH // 4
VMEM_LIMIT = 56 * 1024 * 1024
MESH = pl.DeviceIdType.MESH

ADAM_LR, ADAM_B1, ADAM_B2, ADAM_EPS, ADAM_WD, ADAM_STEP = 0.001, 0.9, 0.999, 1e-08, 0.01, 10


def _cp(*sem):
    return pltpu.CompilerParams(dimension_semantics=sem, vmem_limit_bytes=VMEM_LIMIT)


def _vspec():
    return pl.BlockSpec(memory_space=pltpu.VMEM)


def _sig(z):
    return 1.0 / (1.0 + jnp.exp(-z))


def _dot(a, b):
    return jnp.dot(a, b, preferred_element_type=F32)


def _dot_nt(a, b):
    return lax.dot_general(a, b, (((1,), (1,)), ((), ())), preferred_element_type=F32)


def _dot_tn(a, b):
    return lax.dot_general(a, b, (((0,), (0,)), ((), ())), preferred_element_type=F32)


def _gsum(v, g):
    hi = v.astype(_MX)
    lo = (v - hi.astype(F32)).astype(_MX)
    return _dot(hi, g) + _dot(lo, g)


def _headnorm(ch, w, g):
    r = lax.rsqrt(_gsum(ch * ch, g) * (1.0 / 64.0) + EPS)
    return ch * r * w


def _headnorm_bwd(ch, dy, w, g):
    r = lax.rsqrt(_gsum(ch * ch, g) * (1.0 / 64.0) + EPS)
    hat = ch * r
    dhat = dy * w
    dch = r * (dhat - hat * (_gsum(dhat * hat, g) * (1.0 / 64.0)))
    return dch, dy * hat


def _rope(u, cos, sa, sb):
    return u * cos + pltpu.roll(u, 112, 1) * sa + pltpu.roll(u, 16, 1) * sb


def _rope_bwd(d, cos, sa, sb):
    return d * cos - pltpu.roll(d, 112, 1) * sa - pltpu.roll(d, 16, 1) * sb


def _rowsum(v):
    return jnp.sum(v, axis=0, keepdims=True)


def _call_hosting_gather(body, gather, step_of, nsteps, *, in_specs, out_specs, out_shape, scratch_shapes=(),
                         comm=None, **kw):
    comm = comm or _Gather
    ng, n_in, n_out = len(gather), len(in_specs), len(out_specs)
    if not ng:
        return pl.pallas_call(body, in_specs=in_specs, out_specs=out_specs, out_shape=out_shape,
                              scratch_shapes=list(scratch_shapes), **kw)

    def hosted(*refs):
        ins, outs = refs[:n_in], refs[n_in + ng:n_in + ng + n_out]
        rest = refs[n_in + 2 * ng + n_out:]
        copies = comm(refs[n_in:n_in + ng], refs[n_in + ng + n_out:n_in + 2 * ng + n_out], rest[0], rest[1],
                      [b.shape[1] for b in gather])
        step = step_of()
        pl.when(step == 0)(copies.start)
        body(*ins, *outs, *rest[2:])
        pl.when(step == max(nsteps - 3, 0))(copies.forward)
        pl.when(step == nsteps - 1)(copies.finish)

    return pl.pallas_call(
        hosted, in_specs=list(in_specs) + _any_specs(ng), out_specs=list(out_specs) + _any_specs(ng),
        out_shape=list(out_shape) + comm.results(gather),
        input_output_aliases={n_in + w: n_out + w for w in range(ng)} if comm.in_place else {},
        scratch_shapes=comm.semaphores(ng) + list(scratch_shapes), **kw)


def _mix_in(x, nw, sc, sh, win, qnw2, knw2, cos, sa, sb, gmat, gather=()):
    s = x.shape[0]

    def body(x_ref, nw_ref, sc_ref, sh_ref, win_ref, qnw_ref, knw_ref, cos_ref, sa_ref, sb_ref, g_ref,
             proj_ref, h_ref, q0_ref, q1_ref, k_ref, ksw_ref, v_ref, vsw_ref, u0_ref):
        xv = x_ref[...]
        r = lax.rsqrt(jnp.mean(xv * xv, axis=-1, keepdims=True) + EPS)
        h = xv * r * (nw_ref[...] * (1.0 + sc_ref[...])) + sh_ref[...]
        hb = h.astype(_MX)
        h_ref[...] = hb
        proj = _dot(hb, win_ref[...])
        proj_ref[...] = proj
        cs, sav, sbv, g = cos_ref[...], sa_ref[...], sb_ref[...], g_ref[...]
        lo = lax.broadcasted_iota(jnp.int32, (1, 128), 1) < 64
        for j in range(4):
            q = _rope(_headnorm(proj[:, 128 * j:128 * (j + 1)], qnw_ref[...], g), cs, sav, sbv) * 0.125
            q0_ref[:, 128 * j:128 * (j + 1)] = jnp.where(lo, q, 0.0).astype(_MX)
            q1_ref[:, 128 * j:128 * (j + 1)] = jnp.where(lo, 0.0, q).astype(_MX)
        k = _rope(_headnorm(proj[:, 512:640], knw_ref[...], g), cs, sav, sbv)
        k_ref[...] = k.astype(_MX)
        ksw_ref[...] = pltpu.roll(k, 64, 1).astype(_MX)
        v = proj[:, 640:768]
        v_ref[...] = v.astype(_MX)
        vsw_ref[...] = pltpu.roll(v, 64, 1).astype(_MX)
        u0_ref[...] = proj[:, 768:1280] * _sig(proj[:, 1280:1792])

    row = lambda n: pl.BlockSpec((1, n), lambda i: (0, 0))
    tile = lambda n: pl.BlockSpec((TM, n), lambda i: (i, 0))
    res = _call_hosting_gather(
        body, gather, lambda: pl.program_id(0), s // TM, name="mix_in", grid=(s // TM,),
        in_specs=[tile(D), row(D), row(D), row(D), pl.BlockSpec((D, INW), lambda i: (0, 0)), row(128), row(128),
                  tile(128), tile(128), tile(128), pl.BlockSpec((128, 128), lambda i: (0, 0))],
        out_specs=[tile(INW), tile(D), tile(AW), tile(AW), tile(128), tile(128), tile(128), tile(128), tile(CW)],
        out_shape=[jax.ShapeDtypeStruct((s, INW), F32), jax.ShapeDtypeStruct((s, D), _MX),
                   jax.ShapeDtypeStruct((s, AW), _MX), jax.ShapeDtypeStruct((s, AW), _MX),
                   jax.ShapeDtypeStruct((s, 128), _MX), jax.ShapeDtypeStruct((s, 128), _MX),
                   jax.ShapeDtypeStruct((s, 128), _MX), jax.ShapeDtypeStruct((s, 128), _MX),
                   jax.ShapeDtypeStruct((s, CW), F32)],
        compiler_params=_cp("arbitrary"),
    )(x, nw, sc, sh, win, qnw2, knw2, cos, sa, sb, gmat, *gather)
    return res[:9], res[9:]


def _ctx_kv(ctx, nw, scc, shc, winkv, knw2, gmat):
    def body(ctx_ref, nw_ref, sc_ref, sh_ref, w_ref, knw_ref, g_ref,
             kvc_ref, hc_ref, kc_ref, kcsw_ref, vc_ref, vcsw_ref):
        cv = ctx_ref[...]
        r = lax.rsqrt(jnp.mean(cv * cv, axis=-1, keepdims=True) + EPS)
        hc = (cv * r * (nw_ref[...] * (1.0 + sc_ref[...])) + sh_ref[...]).astype(_MX)
        hc_ref[...] = hc
        kvc = _dot(hc, w_ref[...])
        kvc_ref[...] = kvc
        kc = _headnorm(kvc[:, :128], knw_ref[...], g_ref[...])
        kc_ref[...] = kc.astype(_MX)
        kcsw_ref[...] = pltpu.roll(kc, 64, 1).astype(_MX)
        vc = kvc[:, 128:]
        vc_ref[...] = vc.astype(_MX)
        vcsw_ref[...] = pltpu.roll(vc, 64, 1).astype(_MX)

    return pl.pallas_call(
        body, name="ctx_kv",
        in_specs=[_vspec()] * 7, out_specs=[_vspec()] * 6,
        out_shape=[jax.ShapeDtypeStruct((LC, 256), F32), jax.ShapeDtypeStruct((LC, D), _MX)]
        + [jax.ShapeDtypeStruct((LC, 128), _MX)] * 4,
        compiler_params=pltpu.CompilerParams(vmem_limit_bytes=VMEM_LIMIT),
    )(ctx, nw, scc, shc, winkv, knw2, gmat)


def _attn_mask(i, s):
    r = lax.broadcasted_iota(jnp.int32, (2 * TQ, LC + 3 * TQ), 0) % TQ
    cidx = lax.broadcasted_iota(jnp.int32, (2 * TQ, LC + 3 * TQ), 1)
    qpos = i * TQ + r
    kpos = (i - 1) * TQ + (cidx - LC)
    near = (jnp.abs(qpos - kpos) <= 128) & (kpos >= 0) & (kpos < s)
    return (cidx < LC) | near


def _attn_bias(i, s):
    return jnp.where(_attn_mask(i, s), 0.0, NEG)


def _attn_exp(sc, bias, sinkv):
    sc = sc + bias
    m = jnp.maximum(jnp.max(sc, axis=-1, keepdims=True), sinkv)
    ex = jnp.exp(sc - m)
    es = jnp.exp(sinkv - m)
    return ex, es, 1.0 / (jnp.sum(ex, axis=-1, keepdims=True) + es)


def _sink_rows(sink_ref, g, e):
    return jnp.concatenate([jnp.full((TQ, 1), sink_ref[4 * g + e], F32),
                            jnp.full((TQ, 1), sink_ref[4 * g + 2 + e], F32)], axis=0)


def _attn_fwd(q0, q1, kp, kswp, vp, vswp, kc, kcsw, vc, vcsw, sink, gather=()):
    s = q0.shape[0]
    nb = s // TQ

    def body(q0_ref, q1_ref, kp_ref, kswp_ref, vp_ref, vswp_ref, kc_ref, kcsw_ref, vc_ref, vcsw_ref, sink_ref,
             o_ref, ex_ref, stat_ref):
        i = pl.program_id(0)
        st = pl.multiple_of(i * TQ, TQ)
        kall = (jnp.concatenate([kc_ref[...], kp_ref[pl.ds(st, 3 * TQ), :]], axis=0),
                jnp.concatenate([kcsw_ref[...], kswp_ref[pl.ds(st, 3 * TQ), :]], axis=0))
        vall = (jnp.concatenate([vc_ref[...], vp_ref[pl.ds(st, 3 * TQ), :]], axis=0),
                jnp.concatenate([vcsw_ref[...], vswp_ref[pl.ds(st, 3 * TQ), :]], axis=0))
        bias = _attn_bias(i, s)
        lo = lax.broadcasted_iota(jnp.int32, (1, 128), 1) < 64
        qrefs = (q0_ref, q1_ref)
        out = [jnp.zeros((TQ, 128), F32) for _ in range(4)]
        combos = [(g, e) for g in range(2) for e in range(2)]
        scores = [_dot_nt(jnp.concatenate([qrefs[e][:, 256 * g:256 * g + 128],
                                           qrefs[e][:, 256 * g + 128:256 * g + 256]], axis=0),
                          kall[0 if e == g else 1]) for g, e in combos]
        parts = []
        for n, ((g, e), sc) in enumerate(zip(combos, scores)):
            ex, es, inv = _attn_exp(sc, bias, _sink_rows(sink_ref, g, e))
            parts.append((ex.astype(_MX), inv))
            ex_ref[n] = parts[-1][0]
            stat_ref[n] = jnp.where(lo, inv, es * inv)
        for (g, e), (ex, inv) in zip(combos, parts):
            me = lo if e == 0 else jnp.logical_not(lo)
            o2 = _dot(ex, vall[0 if e == g else 1]) * inv
            out[2 * g] = out[2 * g] + jnp.where(me, o2[:TQ], 0.0)
            out[2 * g + 1] = out[2 * g + 1] + jnp.where(me, o2[TQ:], 0.0)
        for j in range(4):
            o_ref[:, 128 * j:128 * (j + 1)] = out[j]

    full = lambda a: pl.BlockSpec(a.shape, lambda i: (0, 0))
    qs = pl.BlockSpec((TQ, AW), lambda i: (i, 0))
    res = _call_hosting_gather(
        body, gather, lambda: pl.program_id(0), nb, name="attn_fwd", grid=(nb,),
        in_specs=[qs, qs, full(kp), full(kp), full(kp), full(kp), full(kc), full(kc), full(kc), full(kc),
                  pl.BlockSpec(memory_space=pltpu.SMEM)],
        out_specs=[qs, pl.BlockSpec((None, 4, 2 * TQ, LC + 3 * TQ), lambda i: (i, 0, 0, 0)),
                   pl.BlockSpec((None, 4, 2 * TQ, 128), lambda i: (i, 0, 0, 0))],
        out_shape=[jax.ShapeDtypeStruct((s, AW), F32), jax.ShapeDtypeStruct((nb, 4, 2 * TQ, LC + 3 * TQ), _MX),
                   jax.ShapeDtypeStruct((nb, 4, 2 * TQ, 128), F32)],
        compiler_params=_cp("arbitrary"),
    )(q0, q1, kp, kswp, vp, vswp, kc, kcsw, vc, vcsw, sink, *gather)
    return res[:3], res[3:]


def _conv31(u0, cw32, cb):
    s = u0.shape[0]
    rch = 256

    def body(u_ref, w_ref, b_ref, o_ref, pad_ref):
        pad_ref[0:16, :] = jnp.zeros((16, 128), F32)
        pad_ref[s + 16:s + 32, :] = jnp.zeros((16, 128), F32)
        pad_ref[16:s + 16, :] = u_ref[...]
        for cidx in range(s // rch):
            base = cidx * rch
            acc = jnp.zeros((rch, 128), F32) + b_ref[...]
            for j in range(CK):
                acc = acc + w_ref[j:j + 1, :] * pad_ref[base + j + 1:base + j + 1 + rch, :]
            o_ref[base:base + rch, :] = acc

    return pl.pallas_call(
        body, name="conv31", grid=(CW // 128,),
        in_specs=[pl.BlockSpec((s, 128), lambda c: (0, c)), pl.BlockSpec((32, 128), lambda c: (0, c)),
                  pl.BlockSpec((1, 128), lambda c: (0, c))],
        out_specs=pl.BlockSpec((s, 128), lambda c: (0, c)),
        out_shape=jax.ShapeDtypeStruct((s, CW), F32),
        scratch_shapes=[pltpu.VMEM((s + 32, 128), F32)],
        compiler_params=_cp("arbitrary"),
    )(u0, cw32, cb)


def _ln_stats(u1):
    mu = jnp.mean(u1, axis=-1, keepdims=True)
    xc = u1 - mu
    rstd = lax.rsqrt(jnp.mean(xc * xc, axis=-1, keepdims=True) + EPS)
    return xc * rstd, rstd


def _mix_out(o, u1, cnw, cnb, wout, x, g1):
    s = x.shape[0]

    def body(o_ref, u1_ref, cnw_ref, cnb_ref, w_ref, x_ref, g1_ref, x1_ref, mix_ref, cat_ref):
        u2n, _ = _ln_stats(u1_ref[...])
        u2 = u2n * cnw_ref[...] + cnb_ref[...]
        u3 = u2 * _sig(u2)
        cat = jnp.concatenate([o_ref[...], u3], axis=1).astype(_MX)
        cat_ref[...] = cat
        mix = _dot(cat, w_ref[...])
        mix_ref[...] = mix
        x1_ref[...] = x_ref[...] + g1_ref[...] * mix

    row = lambda n: pl.BlockSpec((1, n), lambda i: (0, 0))
    tile = lambda n: pl.BlockSpec((TM, n), lambda i: (i, 0))
    return pl.pallas_call(
        body, name="mix_out", grid=(s // TM,),
        in_specs=[tile(AW), tile(CW), row(CW), row(CW), pl.BlockSpec((D, D), lambda i: (0, 0)), tile(D), row(D)],
        out_specs=[tile(D), tile(D), tile(D)],
        out_shape=[jax.ShapeDtypeStruct((s, D), F32), jax.ShapeDtypeStruct((s, D), F32),
                   jax.ShapeDtypeStruct((s, D), _MX)],
        compiler_params=_cp("arbitrary"),
    )(o, u1, cnw, cnb, wout, x, g1)


def _ffn_up(x1, nw, sc, sh, wup4, gather=()):
    s = x1.shape[0]
    tm = min(1024, s)

    def body(x_ref, nw_ref, sc_ref, sh_ref, w_ref, h2_ref, up_ref, h2s):
        @pl.when(pl.program_id(1) == 0)
        def _():
            xv = x_ref[...]
            r = lax.rsqrt(jnp.mean(xv * xv, axis=-1, keepdims=True) + EPS)
            h2 = (xv * r * (nw_ref[...] * (1.0 + sc_ref[...])) + sh_ref[...]).astype(_MX)
            h2s[...] = h2
            h2_ref[...] = h2
        up_ref[...] = _dot(h2s[...], w_ref[...])

    row = pl.BlockSpec((1, D), lambda i, j: (0, 0))
    res = _call_hosting_gather(
        body, gather, lambda: pl.program_id(0) * 4 + pl.program_id(1), 4 * (s // tm), name="ffn_up",
        grid=(s // tm, 4),
        in_specs=[pl.BlockSpec((tm, D), lambda i, j: (i, 0)), row, row, row,
                  pl.BlockSpec((None, D, FQ), lambda i, j: (j, 0, 0))],
        out_specs=[pl.BlockSpec((tm, D), lambda i, j: (i, 0)),
                   pl.BlockSpec((None, tm, FQ), lambda i, j: (j // 2, i, j % 2))],
        out_shape=[jax.ShapeDtypeStruct((s, D), _MX), jax.ShapeDtypeStruct((2, s, FH), F32)],
        scratch_shapes=[pltpu.VMEM((tm, D), _MX)],
        compiler_params=_cp("arbitrary", "arbitrary"),
    )(x1, nw, sc, sh, wup4, *gather)
    return res[:2], res[2:]


def _fill_pad8(pad_ref, val, s):
    pad_ref[0:8, :] = jnp.zeros((8, 128), F32)
    pad_ref[s + 8:s + 16, :] = jnp.zeros((8, 128), F32)
    pad_ref[8:s + 8, :] = val


def _conv3_at(pad_ref, w_ref, half, base, rch):
    return (w_ref[half, 0:1, :] * pad_ref[base + 7:base + 7 + rch, :]
            + w_ref[half, 1:2, :] * pad_ref[base + 8:base + 8 + rch, :]
            + w_ref[half, 2:3, :] * pad_ref[base + 9:base + 9 + rch, :])


def _ffn_conv_act(up0, fcw, fcb):
    s = up0.shape[1]
    rch = 256

    def body(up_ref, w_ref, b_ref, act_ref, gv_ref, padg, padv):
        _fill_pad8(padg, up_ref[0], s)
        _fill_pad8(padv, up_ref[1], s)
        for cidx in range(s // rch):
            base = cidx * rch
            gate = _conv3_at(padg, w_ref, 0, base, rch) + b_ref[0]
            val = _conv3_at(padv, w_ref, 1, base, rch) + b_ref[1]
            act_ref[base:base + rch, :] = (gate * _sig(gate) * val).astype(_MX)
            gv_ref[0, base:base + rch, :] = gate.astype(_MX)
            gv_ref[1, base:base + rch, :] = val.astype(_MX)

    return pl.pallas_call(
        body, name="ffn_conv_act", grid=(NCH,),
        in_specs=[pl.BlockSpec((2, s, 128), lambda c: (0, 0, c)), pl.BlockSpec((2, 8, 128), lambda c: (0, 0, c)),
                  pl.BlockSpec((2, 1, 128), lambda c: (0, 0, c))],
        out_specs=[pl.BlockSpec((s, 128), lambda c: (0, c)), pl.BlockSpec((2, s, 128), lambda c: (0, 0, c))],
        out_shape=[jax.ShapeDtypeStruct((s, FH), _MX), jax.ShapeDtypeStruct((2, s, FH), _MX)],
        scratch_shapes=[pltpu.VMEM((s + 16, 128), F32)] * 2,
        compiler_params=_cp("arbitrary"),
    )(up0, fcw, fcb)


def _ffn_down(act, wdown, x1, g2, tgt):
    s = x1.shape[0]

    def body(act_ref, w_ref, x1_ref, g2_ref, tgt_ref, dy_ref, dact_ref, ddn_ref, vec_ref):
        @pl.when(pl.program_id(0) == 0)
        def _():
            vec_ref[...] = jnp.zeros((8, D), F32)
        dn = _dot(act_ref[...], w_ref[...])
        diff = x1_ref[...] + g2_ref[...] * dn - tgt_ref[...]
        dy = diff * (1.0 / D)
        dy_ref[...] = dy
        ddn = (dy * g2_ref[...]).astype(_MX)
        ddn_ref[...] = ddn
        dact_ref[...] = _dot_nt(ddn, w_ref[...])
        vec_ref[0:1, :] += _rowsum(dy * dn)
        vec_ref[1:2, :] += _rowsum(diff * diff)

    tile = lambda n: pl.BlockSpec((TM, n), lambda i: (i, 0))
    return pl.pallas_call(
        body, name="ffn_down", grid=(s // TM,),
        in_specs=[tile(FH), pl.BlockSpec((FH, D), lambda i: (0, 0)), tile(D), pl.BlockSpec((1, D), lambda i: (0, 0)),
                  tile(D)],
        out_specs=[tile(D), tile(FH), tile(D), pl.BlockSpec((8, D), lambda i: (0, 0))],
        out_shape=[jax.ShapeDtypeStruct((s, D), F32), jax.ShapeDtypeStruct((s, FH), F32),
                   jax.ShapeDtypeStruct((s, D), _MX), jax.ShapeDtypeStruct((8, D), F32)],
        compiler_params=_cp("arbitrary"),
    )(act, wdown, x1, g2, tgt)


def _ffn_conv_bwd(up0, gv, dact, fcw):
    s = up0.shape[1]
    rch = 256

    def body(up_ref, gv_ref, da_ref, w_ref, dup_ref, gw_ref, dpg, dpv):
        for p in (dpg, dpv):
            p[0:8, :] = jnp.zeros((8, 128), F32)
            p[s + 8:s + 16, :] = jnp.zeros((8, 128), F32)
        for cidx in range(s // rch):
            rows = slice(cidx * rch, (cidx + 1) * rch)
            gate = gv_ref[0, rows, :].astype(F32)
            da = da_ref[rows, :]
            sg = _sig(gate)
            silu = gate * sg
            dpg[8 + cidx * rch:8 + (cidx + 1) * rch, :] = da * gv_ref[1, rows, :].astype(F32) * (sg + silu * (1.0 - sg))
            dpv[8 + cidx * rch:8 + (cidx + 1) * rch, :] = da * silu
        acc = [[jnp.zeros((1, 128), F32) for _ in range(4)] for _ in range(2)]
        for cidx in range(s // rch):
            base = cidx * rch
            for half, dp in enumerate((dpg, dpv)):
                shifted = [dp[base + 9 - j:base + 9 - j + rch, :] for j in range(3)]
                dup_ref[half, base:base + rch, :] = (w_ref[half, 0:1, :] * shifted[0] + w_ref[half, 1:2, :] * shifted[1]
                                                     + w_ref[half, 2:3, :] * shifted[2]).astype(_MX)
                xv = up_ref[half, base:base + rch, :]
                for j in range(3):
                    acc[half][j] = acc[half][j] + _rowsum(xv * shifted[j])
                acc[half][3] = acc[half][3] + _rowsum(shifted[1])
        for half in range(2):
            gw_ref[half] = jnp.zeros((8, 128), F32)
            for j in range(4):
                gw_ref[half, j:j + 1, :] = acc[half][j]

    both = pl.BlockSpec((2, s, 128), lambda c: (0, 0, c))
    return pl.pallas_call(
        body, name="ffn_conv_bwd", grid=(NCH,),
        in_specs=[both, both, pl.BlockSpec((s, 128), lambda c: (0, c)), pl.BlockSpec((2, 8, 128), lambda c: (0, 0, c))],
        out_specs=[both, pl.BlockSpec((2, 8, 128), lambda c: (0, 0, c))],
        out_shape=[jax.ShapeDtypeStruct((2, s, FH), _MX), jax.ShapeDtypeStruct((2, 8, FH), F32)],
        scratch_shapes=[pltpu.VMEM((s + 16, 128), F32)] * 2,
        compiler_params=_cp("arbitrary"),
    )(up0, gv, dact, fcw)


def _tn_matmul(a, b, tm, tn, name, b_split=False, by_chip=False, init=None):
    s, m = a.shape
    n = 2 * b.shape[2] if b_split else b.shape[1]
    ts = min(1024, s)
    nsteps = s // ts
    npb = (n // 2) // tn if b_split else None

    def body(*refs):
        if init is None:
            a_ref, b_ref, o_ref, acc = refs
        else:
            a_ref, b_ref, i_ref, o_ref, acc = refs
        k = pl.program_id(2)

        @pl.when(k == 0)
        def _():
            acc[...] = jnp.zeros((tm, tn), F32)
            if init is not None:
                acc[:, 512:768] = i_ref[...]
        acc[...] += _dot_tn(a_ref[...], b_ref[...])

        @pl.when(k == nsteps - 1)
        def _():
            o_ref[...] = acc[...].astype(_MX)

    if b_split:
        bspec = pl.BlockSpec((None, ts, tn), lambda i, j, k: (j // npb, k, j % npb))
    else:
        bspec = pl.BlockSpec((ts, tn), lambda i, j, k: (k, j))
    in_specs = [pl.BlockSpec((ts, tm), lambda i, j, k: (k, i)), bspec]
    args = [a, b]
    if init is not None:
        in_specs.append(pl.BlockSpec((tm, 256), lambda i, j, k: (i, 0)))
        args.append(init)
    if by_chip:
        out_spec = pl.BlockSpec((None, tm, tn), lambda i, j, k: (j, i, 0))
        out_shape = jax.ShapeDtypeStruct((n // tn, m, tn), _MX)
    else:
        out_spec = pl.BlockSpec((tm, tn), lambda i, j, k: (i, j))
        out_shape = jax.ShapeDtypeStruct((m, n), _MX)
    return pl.pallas_call(
        body, name=name, grid=(m // tm, n // tn, nsteps),
        in_specs=in_specs, out_specs=out_spec, out_shape=out_shape,
        scratch_shapes=[pltpu.VMEM((tm, tn), F32)],
        compiler_params=_cp("arbitrary", "arbitrary", "arbitrary"),
    )(*args)


def _ffn_up_bwd(dup, wup4, x1, dy, mix, nw, sc, g1):
    s = x1.shape[0]
    nk = 4

    def body(dup_ref, w_ref, x1_ref, dy_ref, mix_ref, nw_ref, sc_ref, g1_ref, dx1_ref, dmix_ref, vec_ref, acc):
        i, k = pl.program_id(0), pl.program_id(1)

        @pl.when(k == 0)
        def _():
            acc[...] = jnp.zeros((TM, D), F32)

        @pl.when((k == 0) & (i == 0))
        def _():
            vec_ref[...] = jnp.zeros((8, D), F32)
        acc[...] += _dot_nt(dup_ref[...], w_ref[...])

        @pl.when(k == nk - 1)
        def _():
            dh = acc[...]
            xv = x1_ref[...]
            r = lax.rsqrt(jnp.mean(xv * xv, axis=-1, keepdims=True) + EPS)
            xn = xv * r
            nwv, scv = nw_ref[...], sc_ref[...]
            vec_ref[0:1, :] += _rowsum(dh)
            vec_ref[1:2, :] += _rowsum(dh * xn) * nwv
            vec_ref[2:3, :] += _rowsum(dh * xn) * (1.0 + scv)
            dxn = dh * (nwv * (1.0 + scv))
            dx1 = dy_ref[...] + r * (dxn - xn * jnp.mean(dxn * xn, axis=-1, keepdims=True))
            dx1_ref[...] = dx1
            vec_ref[3:4, :] += _rowsum(dx1 * mix_ref[...])
            dmix_ref[...] = (dx1 * g1_ref[...]).astype(_MX)

    tile = pl.BlockSpec((TM, D), lambda i, k: (i, 0))
    row = pl.BlockSpec((1, D), lambda i, k: (0, 0))
    return pl.pallas_call(
        body, name="ffn_up_bwd", grid=(s // TM, nk),
        in_specs=[pl.BlockSpec((None, TM, FQ), lambda i, k: (k // 2, i, k % 2)),
                  pl.BlockSpec((None, D, FQ), lambda i, k: (k, 0, 0)), tile, tile, tile, row, row, row],
        out_specs=[tile, tile, pl.BlockSpec((8, D), lambda i, k: (0, 0))],
        out_shape=[jax.ShapeDtypeStruct((s, D), F32), jax.ShapeDtypeStruct((s, D), _MX),
                   jax.ShapeDtypeStruct((8, D), F32)],
        scratch_shapes=[pltpu.VMEM((TM, D), F32)],
        compiler_params=_cp("arbitrary", "arbitrary"),
    )(dup, wup4, x1, dy, mix, nw, sc, g1)


def _mix_out_bwd(dmix, wout, u1, cnw, cnb, swap=()):
    s = u1.shape[0]

    def body(dm_ref, w_ref, u1_ref, cnw_ref, cnb_ref, do_ref, du1_ref, vec_ref):
        @pl.when(pl.program_id(0) == 0)
        def _():
            vec_ref[...] = jnp.zeros((8, CW), F32)
        dcat = _dot_nt(dm_ref[...], w_ref[...])
        do_ref[...] = dcat[:, :AW]
        du3 = dcat[:, AW:]
        u2n, rstd = _ln_stats(u1_ref[...])
        u2 = u2n * cnw_ref[...] + cnb_ref[...]
        sg = _sig(u2)
        du2 = du3 * sg * (1.0 + u2 * (1.0 - sg))
        vec_ref[0:1, :] += _rowsum(du2)
        vec_ref[1:2, :] += _rowsum(du2 * u2n)
        d2n = du2 * cnw_ref[...]
        du1_ref[...] = rstd * (d2n - jnp.mean(d2n, axis=-1, keepdims=True)
                               - u2n * jnp.mean(d2n * u2n, axis=-1, keepdims=True))

    row = lambda n: pl.BlockSpec((1, n), lambda i: (0, 0))
    tile = lambda n: pl.BlockSpec((TM, n), lambda i: (i, 0))
    res = _call_hosting_gather(
        body, swap, lambda: pl.program_id(0), s // TM, comm=_Swap, name="mix_out_bwd", grid=(s // TM,),
        in_specs=[tile(D), pl.BlockSpec((D, D), lambda i: (0, 0)), tile(CW), row(CW), row(CW)],
        out_specs=[tile(AW), tile(CW), pl.BlockSpec((8, CW), lambda i: (0, 0))],
        out_shape=[jax.ShapeDtypeStruct((s, AW), F32), jax.ShapeDtypeStruct((s, CW), F32),
                   jax.ShapeDtypeStruct((8, CW), F32)],
        compiler_params=_cp("arbitrary"),
    )(dmix, wout, u1, cnw, cnb, *swap)
    return res[:3], res[3:]


def _conv31_bwd(du1, u0, cw32, proj, exchange=()):
    s = u0.shape[0]
    rch = 256

    def body(d_ref, u_ref, w_ref, ga_ref, gb_ref, dga_ref, dgb_ref, gw_ref, padu, padd):
        for p, src in ((padu, u_ref), (padd, d_ref)):
            p[0:16, :] = jnp.zeros((16, 128), F32)
            p[s + 16:s + 32, :] = jnp.zeros((16, 128), F32)
            p[16:s + 16, :] = src[...]
        for j in range(CK):
            acc = jnp.zeros((1, 128), F32)
            for cidx in range(s // rch):
                base = cidx * rch
                acc = acc + _rowsum(d_ref[base:base + rch, :] * padu[base + j + 1:base + j + 1 + rch, :])
            gw_ref[j:j + 1, :] = acc
        gw_ref[CK:CK + 1, :] = _rowsum(d_ref[...])
        for cidx in range(s // rch):
            base = cidx * rch
            du0 = jnp.zeros((rch, 128), F32)
            for j in range(CK):
                du0 = du0 + w_ref[j:j + 1, :] * padd[base + 31 - j:base + 31 - j + rch, :]
            sg = _sig(gb_ref[base:base + rch, :])
            ga = ga_ref[base:base + rch, :]
            dga_ref[base:base + rch, :] = (du0 * sg).astype(_MX)
            dgb_ref[base:base + rch, :] = (du0 * ga * sg * (1.0 - sg)).astype(_MX)

    blk = lambda off: pl.BlockSpec((s, 128), lambda c: (0, c + off))
    res = _call_hosting_gather(
        body, exchange, lambda: pl.program_id(0), CW // 128, comm=_Exchange, name="conv31_bwd", grid=(CW // 128,),
        in_specs=[blk(0), blk(0), pl.BlockSpec((32, 128), lambda c: (0, c)), blk(6), blk(10)],
        out_specs=[blk(0), blk(0), pl.BlockSpec((32, 128), lambda c: (0, c))],
        out_shape=[jax.ShapeDtypeStruct((s, CW), _MX), jax.ShapeDtypeStruct((s, CW), _MX),
                   jax.ShapeDtypeStruct((32, CW), F32)],
        scratch_shapes=[pltpu.VMEM((s + 32, 128), F32)] * 2,
        compiler_params=_cp("arbitrary"),
    )(du1, u0, cw32, proj, proj, *exchange)
    return res[:3], res[3:]


def _attn_bwd(q0, q1, kp, kswp, vp, vswp, kc, kcsw, vc, vcsw, ex, stat, o, do, exchange=()):
    s = q0.shape[0]
    nb = s // TQ
    ne = len(exchange)

    def body(*refs):
        (q0_ref, q1_ref, kp_ref, kswp_ref, vp_ref, vswp_ref, kc_ref, kcsw_ref, vc_ref, vcsw_ref, ex_ref, stat_ref,
         o_ref, do_ref) = refs[:14]
        dq_ref, dk_ref, dv_ref, dkc_ref, dvc_ref, dsink_ref = refs[14 + ne:20 + ne]
        i = pl.program_id(0)
        if ne:
            exch = _Exchange(refs[14:14 + ne], refs[20 + ne:20 + 2 * ne], refs[20 + 2 * ne], refs[21 + 2 * ne])
            pl.when(i == 0)(exch.start)

        @pl.when(i == 0)
        def _():
            dk_ref[...] = jnp.zeros((s + 2 * TQ, 128), F32)
            dv_ref[...] = jnp.zeros((s + 2 * TQ, 128), F32)
            dkc_ref[...] = jnp.zeros((LC, 128), F32)
            dvc_ref[...] = jnp.zeros((LC, 128), F32)
            dsink_ref[...] = jnp.zeros((8, 128), F32)
        st = pl.multiple_of(i * TQ, TQ)
        kall = (jnp.concatenate([kc_ref[...], kp_ref[pl.ds(st, 3 * TQ), :]], axis=0),
                jnp.concatenate([kcsw_ref[...], kswp_ref[pl.ds(st, 3 * TQ), :]], axis=0))
        vall = (jnp.concatenate([vc_ref[...], vp_ref[pl.ds(st, 3 * TQ), :]], axis=0),
                jnp.concatenate([vcsw_ref[...], vswp_ref[pl.ds(st, 3 * TQ), :]], axis=0))
        lo = lax.broadcasted_iota(jnp.int32, (1, 128), 1) < 64
        qrefs = (q0_ref, q1_ref)
        dq = [jnp.zeros((TQ, 128), F32) for _ in range(4)]
        dkt = jnp.zeros((128, LC + 3 * TQ), F32)
        dvt = jnp.zeros((128, LC + 3 * TQ), F32)
        combos = [(g, e) for g in range(2) for e in range(2)]

        def stacked(ref, g):
            return jnp.concatenate([ref[:, 256 * g:256 * g + 128], ref[:, 256 * g + 128:256 * g + 256]], axis=0)

        for n, (g, e) in enumerate(combos):
            me = lo if e == 0 else jnp.logical_not(lo)
            sw = 0 if e == g else 1
            qm = stacked(qrefs[e], g)
            p = ex_ref[n].astype(F32) * stat_ref[n, :, 0:1]
            dom = jnp.where(me, stacked(do_ref, g), 0.0)
            dd = jnp.sum(dom * stacked(o_ref, g), axis=-1, keepdims=True)
            domx = dom.astype(_MX)
            sd = stat_ref[n, :, 64:65] * dd
            dsink_ref[4 * g + e:4 * g + e + 1, :] -= jnp.sum(sd[:TQ], axis=0, keepdims=True)
            dsink_ref[4 * g + 2 + e:4 * g + 3 + e, :] -= jnp.sum(sd[TQ:], axis=0, keepdims=True)
            ds = (p * (_dot_nt(domx, vall[sw]) - dd)).astype(_MX)
            dq2 = _dot(ds, kall[sw])
            dq[2 * g] = dq[2 * g] + jnp.where(me, dq2[:TQ], 0.0)
            dq[2 * g + 1] = dq[2 * g + 1] + jnp.where(me, dq2[TQ:], 0.0)
            dk2 = _dot_tn(qm, ds)
            dv2 = _dot_tn(domx, p.astype(_MX))
            if sw:
                dk2 = pltpu.roll(dk2, 64, 0)
                dv2 = pltpu.roll(dv2, 64, 0)
            dkt = dkt + dk2
            dvt = dvt + dv2
        dkt = dkt.T
        dvt = dvt.T
        for j in range(4):
            dq_ref[:, 128 * j:128 * (j + 1)] = dq[j] * 0.125
        dkc_ref[...] += dkt[:LC]
        dvc_ref[...] += dvt[:LC]
        dk_ref[pl.ds(st, 3 * TQ), :] += dkt[LC:]
        dv_ref[pl.ds(st, 3 * TQ), :] += dvt[LC:]
        if ne:
            pl.when(i == nb - 1)(exch.finish)

    full = lambda a: pl.BlockSpec(a.shape, lambda i: (0, 0))
    fs = lambda r: pl.BlockSpec((r, 128), lambda i: (0, 0))
    qs = pl.BlockSpec((TQ, AW), lambda i: (i, 0))
    res = pl.pallas_call(
        body, name="attn_bwd", grid=(nb,),
        in_specs=[qs, qs, full(kp), full(kp), full(kp), full(kp), full(kc), full(kc), full(kc), full(kc),
                  pl.BlockSpec((None, 4, 2 * TQ, LC + 3 * TQ), lambda i: (i, 0, 0, 0)),
                  pl.BlockSpec((None, 4, 2 * TQ, 128), lambda i: (i, 0, 0, 0)), qs, qs] + _any_specs(ne),
        out_specs=[qs, fs(s + 2 * TQ), fs(s + 2 * TQ), fs(LC), fs(LC), fs(8)] + _any_specs(ne),
        out_shape=[jax.ShapeDtypeStruct((s, AW), F32), jax.ShapeDtypeStruct((s + 2 * TQ, 128), F32),
                   jax.ShapeDtypeStruct((s + 2 * TQ, 128), F32), jax.ShapeDtypeStruct((LC, 128), F32),
                   jax.ShapeDtypeStruct((LC, 128), F32), jax.ShapeDtypeStruct((8, 128), F32)]
        + _Exchange.out_shapes(exchange),
        scratch_shapes=_Exchange.semaphores(ne) if ne else [],
        compiler_params=_cp("arbitrary"),
    )(q0, q1, kp, kswp, vp, vswp, kc, kcsw, vc, vcsw, ex, stat, o, do, *exchange)
    return res[:6], res[6:]


def _mix_in_bwd(dq, dk, dv, dga, dgb, proj, x, dx1, win, nw, sc, qnw2, knw2, cos, sa, sb, gmat):
    s = x.shape[0]

    def body(dq_ref, dk_ref, dv_ref, dga_ref, dgb_ref, proj_ref, x_ref, dx1_ref, win_ref, nw_ref, sc_ref,
             qnw_ref, knw_ref, cos_ref, sa_ref, sb_ref, g_ref, gx_ref, dproj_ref, vec_ref):
        @pl.when(pl.program_id(0) == 0)
        def _():
            vec_ref[...] = jnp.zeros((8, D), F32)
        cs, sav, sbv, g = cos_ref[...], sa_ref[...], sb_ref[...], g_ref[...]
        gq = jnp.zeros((1, 128), F32)
        for j in range(4):
            dqn = _rope_bwd(dq_ref[:, 128 * j:128 * (j + 1)], cs, sav, sbv)
            dch, gw = _headnorm_bwd(proj_ref[:, 128 * j:128 * (j + 1)], dqn, qnw_ref[...], g)
            dproj_ref[:, 128 * j:128 * (j + 1)] = dch.astype(_MX)
            gq = gq + _rowsum(gw)
        dkn = _rope_bwd(dk_ref[...], cs, sav, sbv)
        dch, gw = _headnorm_bwd(proj_ref[:, 512:640], dkn, knw_ref[...], g)
        dproj_ref[:, 512:640] = dch.astype(_MX)
        dproj_ref[:, 640:768] = dv_ref[...].astype(_MX)
        dproj_ref[:, 768:1280] = dga_ref[...]
        dproj_ref[:, 1280:1792] = dgb_ref[...]
        vec_ref[3:4, 0:128] += gq
        vec_ref[3:4, 128:256] += _rowsum(gw)
        dh = _dot_nt(dproj_ref[...], win_ref[...])
        xv = x_ref[...]
        r = lax.rsqrt(jnp.mean(xv * xv, axis=-1, keepdims=True) + EPS)
        xn = xv * r
        nwv, scv = nw_ref[...], sc_ref[...]
        vec_ref[0:1, :] += _rowsum(dh)
        vec_ref[1:2, :] += _rowsum(dh * xn) * nwv
        vec_ref[2:3, :] += _rowsum(dh * xn) * (1.0 + scv)
        dxn = dh * (nwv * (1.0 + scv))
        gx_ref[...] = dx1_ref[...] + r * (dxn - xn * jnp.mean(dxn * xn, axis=-1, keepdims=True))

    row = lambda n: pl.BlockSpec((1, n), lambda i: (0, 0))
    tile = lambda n: pl.BlockSpec((TM, n), lambda i: (i, 0))
    return pl.pallas_call(
        body, name="mix_in_bwd", grid=(s // TM,),
        in_specs=[tile(AW), tile(128), tile(128), tile(CW), tile(CW), tile(INW), tile(D), tile(D),
                  pl.BlockSpec((D, INW), lambda i: (0, 0)), row(D), row(D), row(128), row(128),
                  tile(128), tile(128), tile(128), pl.BlockSpec((128, 128), lambda i: (0, 0))],
        out_specs=[tile(D), tile(INW), pl.BlockSpec((8, D), lambda i: (0, 0))],
        out_shape=[jax.ShapeDtypeStruct((s, D), F32), jax.ShapeDtypeStruct((s, INW), _MX),
                   jax.ShapeDtypeStruct((8, D), F32)],
        compiler_params=_cp("arbitrary"),
    )(dq, dk, dv, dga, dgb, proj, x, dx1, win, nw, sc, qnw2, knw2, cos, sa, sb, gmat)


def _ctx_bwd(ctx, nw, scc, winkv, kvc, hc, dkc, dvc, knw2, gmat):
    def body(ctx_ref, nw_ref, sc_ref, w_ref, kvc_ref, hc_ref, dkc_ref, dvc_ref, knw_ref, g_ref, gw_ref, vec_ref):
        dkr, gk = _headnorm_bwd(kvc_ref[:, 0:128], dkc_ref[...], knw_ref[...], g_ref[...])
        dkv = jnp.concatenate([dkr, dvc_ref[...]], axis=1).astype(_MX)
        gw_ref[...] = _dot_tn(hc_ref[...], dkv)
        dh = _dot_nt(dkv, w_ref[...])
        cv = ctx_ref[...]
        r = lax.rsqrt(jnp.mean(cv * cv, axis=-1, keepdims=True) + EPS)
        cn = cv * r
        vec_ref[...] = jnp.zeros((8, D), F32)
        vec_ref[0:1, :] = _rowsum(dh)
        vec_ref[1:2, :] = _rowsum(dh * cn) * nw_ref[...]
        vec_ref[2:3, :] = _rowsum(dh * cn) * (1.0 + sc_ref[...])
        vec_ref[3:4, 0:128] = _rowsum(gk)

    return pl.pallas_call(
        body, name="ctx_bwd", in_specs=[_vspec()] * 10, out_specs=[_vspec()] * 2,
        out_shape=[jax.ShapeDtypeStruct((D, 256), F32), jax.ShapeDtypeStruct((8, D), F32)],
        compiler_params=pltpu.CompilerParams(vmem_limit_bytes=VMEM_LIMIT),
    )(ctx, nw, scc, winkv, kvc, hc, dkc, dvc, knw2, gmat)


def _mod_fwd(cs, wmod, bloc):
    def body(c_ref, w_ref, b_ref, o_ref):
        cv = c_ref[...]
        o_ref[...] = _dot((cv * _sig(cv)).astype(_MX), w_ref[...].astype(_MX)) + b_ref[...]

    return pl.pallas_call(
        body, name="mod_fwd", in_specs=[_vspec()] * 3, out_specs=_vspec(),
        out_shape=jax.ShapeDtypeStruct((16, wmod.shape[1]), F32),
        compiler_params=pltpu.CompilerParams(vmem_limit_bytes=VMEM_LIMIT),
    )(cs, wmod, bloc)


def _mod_bwd(cs, dmod, wmod):
    def body(c_ref, d_ref, w_ref, gw_ref, part_ref):
        cv = c_ref[...]
        sl = (cv * _sig(cv)).astype(_MX)
        dm = d_ref[...].astype(_MX)
        gw_ref[...] = _dot_tn(sl, dm)
        part_ref[...] = _dot_nt(dm, w_ref[...].astype(_MX))

    return pl.pallas_call(
        body, name="mod_bwd", in_specs=[_vspec()] * 3, out_specs=[_vspec()] * 2,
        out_shape=[jax.ShapeDtypeStruct(wmod.shape, F32), jax.ShapeDtypeStruct((16, D), F32)],
        compiler_params=pltpu.CompilerParams(vmem_limit_bytes=VMEM_LIMIT),
    )(cs, dmod, wmod)


def _sum_leading(a, name):
    n = a.shape[0]

    def body(a_ref, o_ref):
        acc = a_ref[0]
        for k in range(1, n):
            acc = acc + a_ref[k]
        o_ref[...] = acc

    return pl.pallas_call(
        body, name=name, in_specs=[_vspec()], out_specs=_vspec(),
        out_shape=jax.ShapeDtypeStruct(a.shape[1:], F32),
        compiler_params=pltpu.CompilerParams(vmem_limit_bytes=VMEM_LIMIT),
    )(a)


def _cctx_grad(parts, cc):
    def body(p_ref, c_ref, o_ref):
        acc = p_ref[0, 8:9, :]
        for k in range(1, 4):
            acc = acc + p_ref[k, 8:9, :]
        cv = c_ref[...]
        sg = _sig(cv)
        o_ref[...] = acc * (sg * (1.0 + cv * (1.0 - sg)))

    return pl.pallas_call(
        body, name="cctx_grad", in_specs=[_vspec()] * 2, out_specs=_vspec(),
        out_shape=jax.ShapeDtypeStruct((1, D), F32),
    )(parts, cc)


def _adam_math(w, g, m, v):
    mn = ADAM_B1 * m + (1.0 - ADAM_B1) * g
    vn = ADAM_B2 * v + (1.0 - ADAM_B2) * (g * g)
    mh = mn / (1.0 - ADAM_B1 ** ADAM_STEP)
    vh = vn / (1.0 - ADAM_B2 ** ADAM_STEP)
    delta = -ADAM_LR * (mh / (jnp.sqrt(vh) + ADAM_EPS) + ADAM_WD * w)
    return delta, mn, vn


def _adam_big(w, g, m, v, name):
    r, n = w.shape
    tr = 256 if r % 256 == 0 else 64

    def body(w_ref, g_ref, m_ref, v_ref, d_ref, mo_ref, vo_ref):
        d, mn, vn = _adam_math(w_ref[...], g_ref[...], m_ref[...], v_ref[...])
        d_ref[...] = d
        mo_ref[...] = mn
        vo_ref[...] = vn

    spec = pl.BlockSpec((tr, n), lambda i: (i, 0))
    return pl.pallas_call(
        body, name=name, grid=(r // tr,), in_specs=[spec] * 4, out_specs=[spec] * 3,
        out_shape=[jax.ShapeDtypeStruct((r, n), F32)] * 3,
        compiler_params=_cp("arbitrary"),
    )(w, g, m, v)


def _adam_small(ws, gs, ms, vs):
    n = len(ws)

    def body(*refs):
        ins, outs = refs[:4 * n], refs[4 * n:]
        for k in range(n):
            d, mn, vn = _adam_math(ins[k][...], ins[n + k][...], ins[2 * n + k][...], ins[3 * n + k][...])
            outs[k][...] = d
            outs[n + k][...] = mn
            outs[2 * n + k][...] = vn

    shapes = [jax.ShapeDtypeStruct(w.shape, F32) for w in ws]
    res = pl.pallas_call(
        body, name="adam_small", in_specs=[_vspec()] * (4 * n), out_specs=[_vspec()] * (3 * n),
        out_shape=shapes * 3,
    )(*ws, *gs, *ms, *vs)
    return res[:n], res[n:2 * n], res[2 * n:]


def _pair_sum(grads, from_sib, core, name):
    n = len(grads)

    def body(c_ref, *refs):
        for w in range(n):
            a_ref, b_ref, o_ref = refs[w], refs[n + w], refs[2 * n + w]
            o_ref[...] = (a_ref[...].astype(F32) + b_ref[...].astype(F32)).astype(_MX)

    halves = [(None, g.shape[1] // 2, g.shape[2]) for g in grads]
    return pl.pallas_call(
        body, name=name,
        grid_spec=pltpu.PrefetchScalarGridSpec(
            num_scalar_prefetch=1, grid=(4,),
            in_specs=[pl.BlockSpec(h, lambda j, c: (j, c[0], 0)) for h in halves]
            + [pl.BlockSpec(h, lambda j, c: (j, 0, 0)) for h in halves],
            out_specs=[pl.BlockSpec(h, lambda j, c: (j, 0, 0)) for h in halves]),
        out_shape=[jax.ShapeDtypeStruct(p.shape, _MX) for p in from_sib],
        compiler_params=_cp("arbitrary"),
    )(core.reshape(1), *grads, *from_sib)


def _chip_sum(parts, arrived, chip, core):
    n = len(parts)

    def body(s_ref, *refs):
        for w in range(n):
            own_ref, p_ref, o_ref = refs[w], refs[n + w], refs[2 * n + w]
            acc = own_ref[...].astype(F32)
            for k in range(3):
                acc = acc + p_ref[k].astype(F32)
            o_ref[...] = acc

    blk = [(p.shape[1] // 2, p.shape[2]) for p in parts]
    return pl.pallas_call(
        body, name="grad_chip_sum",
        grid_spec=pltpu.PrefetchScalarGridSpec(
            num_scalar_prefetch=1, grid=(2,),
            in_specs=[pl.BlockSpec((None,) + b, lambda i, s: (s[0], i, 0)) for b in blk]
            + [pl.BlockSpec((3,) + b, lambda i, s: (0, i, 0)) for b in blk],
            out_specs=[pl.BlockSpec(b, lambda i, s: (2 * s[1] + i, 0)) for b in blk]),
        out_shape=[jax.ShapeDtypeStruct((2 * p.shape[1], p.shape[2]), F32) for p in parts],
        compiler_params=_cp("arbitrary"),
    )(jnp.stack([chip, core]), *parts, *arrived)


def _cast_shards(ws, chip):
    n = len(ws)

    def body(s_ref, *refs):
        for w in range(n):
            refs[n + w][...] = refs[w][...].astype(_MX)

    blk = [(a.shape[0] // 2, a.shape[1]) for a in ws]
    return pl.pallas_call(
        body, name="cast_shards",
        grid_spec=pltpu.PrefetchScalarGridSpec(
            num_scalar_prefetch=1, grid=(2,),
            in_specs=[pl.BlockSpec(b, lambda i, s: (i, 0)) for b in blk],
            out_specs=[pl.BlockSpec((None,) + b, lambda i, s: (s[0], i, 0)) for b in blk]),
        out_shape=[jax.ShapeDtypeStruct((4,) + a.shape, _MX) for a in ws],
        compiler_params=_cp("arbitrary"),
    )(chip.reshape(1), *ws)


def _position():
    x, y, c = lax.axis_index("x"), lax.axis_index("y"), lax.axis_index("c")
    return x, y, c


def _small_allgather(blk, name, exchange=(), share=(), gather=()):
    m_per, n = blk.shape
    assert not (share and gather)
    share = tuple(share) + tuple(gather)
    ne, ns = len(exchange), len(share)

    def body(*refs):
        x_ref = refs[0]
        out_ref = refs[1 + ne + ns]
        send_sems, recv_sems = refs[2 + 2 * ne + 2 * ns], refs[3 + 2 * ne + 2 * ns]
        extra = refs[4 + 2 * ne + 2 * ns:]
        x, y, c = _position()
        me, sibling = (x, y, c), (x, y, 1 - c)
        chips = [(1 - x, y), (x, 1 - y), (1 - x, 1 - y)]
        if ne:
            exch = _Exchange(refs[1:1 + ne], refs[2 + ne + ns:2 + 2 * ne + ns], extra[0], extra[1])
            exch.start()
        if gather:
            gat = _Gather(refs[1 + ne:1 + ne + ns], refs[2 + 2 * ne + ns:2 + 2 * ne + 2 * ns],
                          extra[2 * bool(ne)], extra[2 * bool(ne) + 1], [b.shape[1] for b in gather])
            gat.start()
        elif ns:
            halves = _share_copies(refs[1 + ne:1 + ne + ns], refs[2 + 2 * ne + ns:2 + 2 * ne + 2 * ns],
                                   extra[2 * bool(ne)], extra[2 * bool(ne) + 1], [b.shape[0] for b in share])
            for cp in halves:
                cp.start()

        def rows(px, py, pc):
            return out_ref.at[pl.ds(pl.multiple_of((4 * px + 2 * py + pc) * m_per, 8), m_per), :]

        def copy(k, block, to, src=None):
            return pltpu.make_async_remote_copy(
                src_ref=rows(*block) if src is None else src, dst_ref=rows(*block),
                send_sem=send_sems.at[k], recv_sem=recv_sems.at[k], device_id=to, device_id_type=MESH)

        out_ref[pl.ds(pl.multiple_of((4 * x + 2 * y + c) * m_per, 8), m_per), :] = x_ref[...]
        first = [copy(0, me, sibling, src=x_ref)]
        first += [copy(1 + j, me, (*chip, c), src=x_ref) for j, chip in enumerate(chips)]
        for cp in first:
            cp.start()
        passed = [copy(4 + j, (*chip, c), sibling) for j, chip in enumerate(chips)]
        for j, chip in enumerate(chips):
            copy(1 + j, (*chip, c), me).wait_recv()
            passed[j].start()
        copy(0, sibling, me).wait_recv()
        for j, chip in enumerate(chips):
            copy(4 + j, (*chip, 1 - c), me).wait_recv()
        for cp in first + passed:
            cp.wait_send()
        if ne:
            exch.finish()
        if gather:
            gat.forward()
            gat.finish()
        elif ns:
            for cp in halves:
                cp.wait()

    res = pl.pallas_call(
        body, name=name,
        out_shape=[jax.ShapeDtypeStruct((8 * m_per, n), blk.dtype)] + _Exchange.out_shapes(exchange)
        + [jax.ShapeDtypeStruct(b.shape, b.dtype) for b in share],
        in_specs=[_vspec()] + _any_specs(ne + ns), out_specs=[_vspec()] + _any_specs(ne + ns),
        input_output_aliases={1 + ne + w: 1 + ne + w for w in range(ns)},
        scratch_shapes=[pltpu.SemaphoreType.DMA((7,)), pltpu.SemaphoreType.DMA((7,))]
        + (_Exchange.semaphores(ne) if ne else [])
        + (_Gather.semaphores(ns) if gather else
           [pltpu.SemaphoreType.DMA((ns,)), pltpu.SemaphoreType.DMA((ns,))] if ns else []),
    )(blk, *exchange, *share)
    return res[0], res[1:1 + ne], res[1 + ne:]


def _share_copies(src_refs, out_refs, send_sems, recv_sems, nrows):
    x, y, c = _position()
    return [pltpu.make_async_remote_copy(
        src_ref=_rows(src_refs[w], c, r), dst_ref=_rows(out_refs[w], c, r), send_sem=send_sems.at[w],
        recv_sem=recv_sems.at[w], device_id=(x, y, 1 - c), device_id_type=MESH) for w, r in enumerate(nrows)]


def _any_specs(n):
    return [pl.BlockSpec(memory_space=pl.ANY)] * n


def _rows(ref, half, nrows):
    return ref.at[pl.ds(half * (nrows // 2), nrows // 2), :]


class _Gather:
    def __init__(self, src_refs, out_refs, send_sems, recv_sems, nrows):
        x, y, c = _position()
        chip = 2 * x + y
        sibling = (x, y, 1 - c)

        def copy(k, src, dst, to):
            return pltpu.make_async_remote_copy(src_ref=src, dst_ref=dst, send_sem=send_sems.at[k],
                                                recv_sem=recv_sems.at[k], device_id=to, device_id_type=MESH)

        self.first, self.first_landed, self.passed, self.passed_landed = [], [], [], []
        for w, r in enumerate(nrows):
            for j, ch in enumerate([(1 - x, y), (x, 1 - y), (1 - x, 1 - y)]):
                theirs = _rows(out_refs[w].at[2 * ch[0] + ch[1]], c, r)
                other = _rows(out_refs[w].at[2 * ch[0] + ch[1]], 1 - c, r)
                self.first.append(copy(6 * w + j, _rows(src_refs[w].at[chip], c, r),
                                       _rows(out_refs[w].at[chip], c, r), (*ch, c)))
                self.first_landed.append(copy(6 * w + j, theirs, theirs, sibling))
                self.passed.append(copy(6 * w + 3 + j, theirs, theirs, sibling))
                self.passed_landed.append(copy(6 * w + 3 + j, other, other, sibling))

    in_place = True

    @staticmethod
    def semaphores(n):
        return [pltpu.SemaphoreType.DMA((6 * n,)), pltpu.SemaphoreType.DMA((6 * n,))]

    @staticmethod
    def results(bufs):
        return [jax.ShapeDtypeStruct(b.shape, b.dtype) for b in bufs]

    def start(self):
        for cp in self.first:
            cp.start()

    def forward(self):
        for landed, cp in zip(self.first_landed, self.passed):
            landed.wait_recv()
            cp.start()

    def finish(self):
        for landed in self.passed_landed:
            landed.wait_recv()
        for cp in self.first + self.passed:
            cp.wait_send()


class _Swap:
    in_place = False

    def __init__(self, grad_refs, out_refs, send_sems, recv_sems, nrows):
        x, y, c = _position()
        self.copies = [pltpu.make_async_remote_copy(
            src_ref=grad_refs[w].at[:, pl.ds((1 - c) * (r // 2), r // 2), :], dst_ref=out_refs[w],
            send_sem=send_sems.at[w], recv_sem=recv_sems.at[w], device_id=(x, y, 1 - c), device_id_type=MESH)
            for w, r in enumerate(nrows)]

    @staticmethod
    def semaphores(n):
        return [pltpu.SemaphoreType.DMA((n,)), pltpu.SemaphoreType.DMA((n,))]

    @staticmethod
    def results(grads):
        return [jax.ShapeDtypeStruct((4, g.shape[1] // 2, g.shape[2]), g.dtype) for g in grads]

    def start(self):
        for cp in self.copies:
            cp.start()

    def forward(self):
        pass

    def finish(self):
        for cp in self.copies:
            cp.wait()


def _grad_swap(grads, name):
    n = len(grads)

    def body(*refs):
        swap = _Swap(refs[:n], refs[n:2 * n], refs[2 * n], refs[2 * n + 1], [g.shape[1] for g in grads])
        swap.start()
        swap.finish()

    return pl.pallas_call(
        body, name=name, out_shape=_Swap.results(grads), in_specs=_any_specs(n), out_specs=_any_specs(n),
        scratch_shapes=_Swap.semaphores(n),
    )(*grads)


class _Exchange:
    in_place = False

    def __init__(self, part_refs, out_refs, send_sems, recv_sems, nrows=None):
        x, y, c = _position()
        self.copies = [
            pltpu.make_async_remote_copy(
                src_ref=part_refs[w].at[2 * ch[0] + ch[1]], dst_ref=out_refs[w].at[k], send_sem=send_sems.at[3 * w + k],
                recv_sem=recv_sems.at[3 * w + k], device_id=(*ch, c), device_id_type=MESH)
            for w in range(len(part_refs)) for k, ch in enumerate([(1 - x, y), (x, 1 - y), (1 - x, 1 - y)])]

    @staticmethod
    def semaphores(n):
        return [pltpu.SemaphoreType.DMA((3 * n,)), pltpu.SemaphoreType.DMA((3 * n,))]

    @staticmethod
    def out_shapes(parts):
        return [jax.ShapeDtypeStruct((3,) + p.shape[1:], p.dtype) for p in parts]

    results = out_shapes

    def start(self):
        for cp in self.copies:
            cp.start()

    def forward(self):
        pass

    def finish(self):
        for cp in self.copies:
            cp.wait()


def _rope_tables(s):
    rows = s // GRID_W
    inv = 10000.0 ** (-jnp.arange(0, 32, 2, dtype=F32) / 32.0)
    ang_r = jnp.arange(rows, dtype=F32)[:, None] * inv
    ang_c = jnp.arange(GRID_W, dtype=F32)[:, None] * inv
    lane = jnp.arange(128)
    first = (lane % 32) < 16
    by_row = (lane % 64) < 32

    def table(fn):
        tr = jnp.tile(fn(ang_r), (1, 8))[:, None, :]
        tc = jnp.tile(fn(ang_c), (1, 8))[None, :, :]
        return jnp.where(by_row, tr, tc).reshape(s, 128)

    cos, sin = table(jnp.cos), table(jnp.sin)
    return cos, jnp.where(first, -sin, 0.0), jnp.where(first, 0.0, sin)


def _local_step(x, ctx, tgt, mod, modc, nw1, win, qnw, knw, sink, cw, cb, cnw, cnb, wout4, nw2, wup, fcw, fcb, wdown4,
                core=None):
    s = x.shape[0]
    sh1, sc1, g1, sh2, sc2, g2 = [mod[:, D * k:D * (k + 1)] for k in range(6)]
    shc, scc = modc[:, :D], modc[:, D:2 * D]
    cos, sa, sb = _rope_tables(s)
    gi = jnp.arange(128) // 64
    gmat = (gi[:, None] == gi[None, :]).astype(_MX)
    qnw2, knw2 = jnp.tile(qnw, (1, 2)), jnp.tile(knw, (1, 2))
    cw32 = jnp.pad(cw, ((0, 1), (0, 0)))
    fcw2 = jnp.pad(fcw, ((0, 5), (0, 0))).reshape(8, 2, FH).transpose(1, 0, 2)
    fcb2 = fcb.reshape(2, 1, FH)
    winkv = win[:, 512:768]
    sinkv = sink.reshape(8)

    host = (lambda *bufs: ()) if core is None else (lambda *bufs: bufs)
    (proj, h, q0, q1, k, ksw, v, vsw, u0), got = _mix_in(x, nw1, sc1, sh1, win, qnw2, knw2, cos, sa, sb, gmat,
                                                        gather=host(wout4))
    wout = (got[0] if got else wout4).reshape(D, D)
    kvc, hc, kc, kcsw, vc, vcsw = _ctx_kv(ctx, nw1, scc, shc, winkv, knw2, gmat)
    padr = lambda a: jnp.pad(a, ((TQ, TQ), (0, 0)))
    kp, kswp, vp, vswp = padr(k), padr(ksw), padr(v), padr(vsw)
    (o, pex, pstat), got = _attn_fwd(q0, q1, kp, kswp, vp, vswp, kc, kcsw, vc, vcsw, sinkv, gather=host(wup))
    wup = got[0] if got else wup
    u1 = _conv31(u0, cw32, cb)
    x1, mix, cat = _mix_out(o, u1, cnw, cnb, wout, x, g1)
    (h2, up0), got = _ffn_up(x1, nw2, sc2, sh2, wup, gather=host(wdown4))
    wdown = (got[0] if got else wdown4).reshape(FH, D)
    act, gv = _ffn_conv_act(up0, fcw2, fcb2)
    dy, dact, ddn, vec_dn = _ffn_down(act, wdown, x1, g2, tgt)
    loss = (0.5 / D) * jnp.sum(vec_dn[1])

    dup, gfc = _ffn_conv_bwd(up0, gv, dact, fcw2)
    g_wdown = _tn_matmul(act, ddn, FQ, D, "gw_down").reshape(4, FH // 4, D)
    g_wup = _tn_matmul(h2, dup, D, FQ, "gw_up", b_split=True, by_chip=True)
    dx1, dmix, vec_up = _ffn_up_bwd(dup, wup, x1, dy, mix, nw2, sc2, g1)
    g_wout = _tn_matmul(cat, dmix, D, D, "gw_out").reshape(4, D // 4, D)
    late = [g_wout, g_wup, g_wdown]
    (do, du1, vec_ln), from_sib = _mix_out_bwd(dmix, wout, u1, cnw, cnb, swap=() if core is None else late)
    parts = () if core is None else _pair_sum(late, from_sib, core, "grad_pair_sum_late")
    (dga, dgb, gcw), arrived_a = _conv31_bwd(du1, u0, cw32, proj, exchange=parts[0:1] + parts[2:3])
    (dq, dkp, dvp, dkc, dvc, dsink), arrived_b = _attn_bwd(q0, q1, kp, kswp, vp, vswp, kc, kcsw, vc, vcsw, pex, pstat,
                                                           o, do, exchange=parts[1:2])
    if core is not None:
        g_wout, g_wup, g_wdown = zip(parts, [arrived_a[0], arrived_b[0], arrived_a[1]])
    gwinkv, vec_ctx = _ctx_bwd(ctx, nw1, scc, winkv, kvc, hc, dkc, dvc, knw2, gmat)
    gx, dproj, vec_in = _mix_in_bwd(dq, dkp[TQ:TQ + s], dvp[TQ:TQ + s], dga, dgb, proj, x, dx1, win, nw1, sc1,
                                    qnw2, knw2, cos, sa, sb, gmat)
    g_win = _tn_matmul(h, dproj, D, INW, "gw_in", init=gwinkv)

    g_qn = vec_in[3:4, 0:128]
    g_kn = vec_in[3:4, 128:256] + vec_ctx[3:4, 0:128]
    grads = dict(
        norm_mix_w=vec_in[2:3] + vec_ctx[2:3], w_in=g_win,
        q_norm_w=g_qn[:, :64] + g_qn[:, 64:], k_norm_w=g_kn[:, :64] + g_kn[:, 64:],
        sink_logit=dsink[:, 0].reshape(1, 8), conv_w=gcw[:CK], conv_b=gcw[CK:CK + 1],
        conv_norm_w=vec_ln[1:2], conv_norm_b=vec_ln[0:1], w_out=g_wout, norm_ffn_w=vec_up[2:3], w_up=g_wup,
        ffn_conv_w=gfc[:, 0:3].transpose(1, 0, 2).reshape(3, 2 * FH), ffn_conv_b=gfc[:, 3].reshape(1, 2 * FH),
        w_down=g_wdown)
    dmod = jnp.concatenate([vec_in[0:1], vec_in[1:2], vec_up[3:4], vec_up[0:1], vec_up[1:2], vec_dn[0:1]], axis=1)
    dmodc = jnp.concatenate([vec_ctx[0:1], vec_ctx[1:2]], axis=1)
    return loss, gx, grads, dmod, dmodc


_SMALL = ["c_ctx", "b_mod", "norm_mix_w", "q_norm_w", "k_norm_w", "sink_logit", "conv_w", "conv_b", "conv_norm_w",
          "conv_norm_b", "norm_ffn_w", "ffn_conv_w", "ffn_conv_b"]
_GATHERED = ["w_in", "w_out", "w_up", "w_down"]
_BIG = ["w_mod"] + _GATHERED
_ORDER = ["c_ctx", "w_mod", "b_mod", "norm_mix_w", "w_in", "q_norm_w", "k_norm_w", "sink_logit", "conv_w", "conv_b",
          "conv_norm_w", "conv_norm_b", "w_out", "norm_ffn_w", "w_up", "ffn_conv_w", "ffn_conv_b", "w_down"]
_PACK = [("norm_mix_w", 1), ("norm_ffn_w", 1), ("q_norm_w", 1), ("k_norm_w", 1), ("sink_logit", 1), ("conv_b", 1),
         ("conv_norm_w", 1), ("conv_norm_b", 1), ("ffn_conv_b", 6), ("conv_w", 16), ("ffn_conv_w", 17)]
_PACK_ROWS = 56


def _pack_rows(a, nrows):
    flat = a.reshape(-1)
    return jnp.pad(flat, (0, nrows * D - flat.shape[0])).reshape(nrows, D)


def kernel(x, c, ctx, c_ctx, w_mod, b_mod, norm_mix_w, w_in, q_norm_w, k_norm_w, sink_logit, conv_w, conv_b, conv_norm_w, conv_norm_b, w_out, norm_ffn_w, w_up, ffn_conv_w, ffn_conv_b, w_down, loss_target, m_c_ctx, m_w_mod, m_b_mod, m_norm_mix_w, m_w_in, m_q_norm_w, m_k_norm_w, m_sink_logit, m_conv_w, m_conv_b, m_conv_norm_w, m_conv_norm_b, m_w_out, m_norm_ffn_w, m_w_up, m_ffn_conv_w, m_ffn_conv_b, m_w_down, v_c_ctx, v_w_mod, v_b_mod, v_norm_mix_w, v_w_in, v_q_norm_w, v_k_norm_w, v_sink_logit, v_conv_w, v_conv_b, v_conv_norm_w, v_conv_norm_b, v_w_out, v_norm_ffn_w, v_w_up, v_ffn_conv_w, v_ffn_conv_b, v_w_down):
    w = dict(c_ctx=c_ctx.reshape(1, D), w_mod=w_mod[0], b_mod=b_mod, norm_mix_w=norm_mix_w, w_in=w_in[0],
             q_norm_w=q_norm_w, k_norm_w=k_norm_w, sink_logit=sink_logit, conv_w=conv_w[0], conv_b=conv_b,
             conv_norm_w=conv_norm_w, conv_norm_b=conv_norm_b, w_out=w_out[0], norm_ffn_w=norm_ffn_w, w_up=w_up[0],
             ffn_conv_w=ffn_conv_w[0], ffn_conv_b=ffn_conv_b, w_down=w_down[0])
    m = dict(c_ctx=m_c_ctx.reshape(1, D), w_mod=m_w_mod[0], b_mod=m_b_mod, norm_mix_w=m_norm_mix_w, w_in=m_w_in[0],
             q_norm_w=m_q_norm_w, k_norm_w=m_k_norm_w, sink_logit=m_sink_logit, conv_w=m_conv_w[0], conv_b=m_conv_b,
             conv_norm_w=m_conv_norm_w, conv_norm_b=m_conv_norm_b, w_out=m_w_out[0], norm_ffn_w=m_norm_ffn_w,
             w_up=m_w_up[0], ffn_conv_w=m_ffn_conv_w[0], ffn_conv_b=m_ffn_conv_b, w_down=m_w_down[0])
    v = dict(c_ctx=v_c_ctx.reshape(1, D), w_mod=v_w_mod[0], b_mod=v_b_mod, norm_mix_w=v_norm_mix_w, w_in=v_w_in[0],
             q_norm_w=v_q_norm_w, k_norm_w=v_k_norm_w, sink_logit=v_sink_logit, conv_w=v_conv_w[0], conv_b=v_conv_b,
             conv_norm_w=v_conv_norm_w, conv_norm_b=v_conv_norm_b, w_out=v_w_out[0], norm_ffn_w=v_norm_ffn_w,
             w_up=v_w_up[0], ffn_conv_w=v_ffn_conv_w[0], ffn_conv_b=v_ffn_conv_b, w_down=v_w_down[0])
    xi, yi, ci = _position()
    chip = 2 * xi + yi
    dev = 2 * chip + ci
    s = x.shape[1]
    ncol = w["w_mod"].shape[1]

    win_buf, wout_buf, wup_buf, wdown_buf = _cast_shards([w[n] for n in _GATHERED], chip)

    blk0 = jnp.concatenate([_pack_rows(w["conv_w"], 4), _pack_rows(w["ffn_conv_w"], 5), c,
                            jnp.zeros((6, D), F32)], axis=0)
    g0, _, (win4,) = _small_allgather(blk0, "gather_c_convw", gather=[win_buf])
    win = win4.transpose(1, 0, 2).reshape(D, INW)
    g0 = g0.reshape(8, 16, D)
    c_all = g0[:, 9, :]
    cs = jnp.concatenate([c_all, w["c_ctx"], jnp.zeros((7, D), F32)], axis=0)
    cw_full = jnp.concatenate([g0[2 * j, 0:4].reshape(-1)[:CK * 128].reshape(CK, 128) for j in range(4)], axis=1)
    fcw_full = jnp.concatenate([g0[2 * j, 4:9].reshape(-1)[:3 * 1408].reshape(3, 1408) for j in range(4)], axis=1)

    b_loc = lax.dynamic_slice(w["b_mod"], (0, chip * ncol), (1, ncol))
    modp = _mod_fwd(cs, w["w_mod"], b_loc)
    gm = _small_allgather(modp, "gather_mod")[0].reshape(8, 16, ncol)
    mod_all = jnp.concatenate([gm[2 * j] for j in range(4)], axis=1)
    mod = lax.dynamic_slice(mod_all, (dev, 0), (1, 6 * D))
    modc = mod_all[8:9]

    loss_loc, gx, gl, dmod, dmodc = _local_step(
        x[0], ctx[0], loss_target[0], mod, modc, w["norm_mix_w"], win, w["q_norm_w"], w["k_norm_w"], w["sink_logit"],
        cw_full, w["conv_b"], w["conv_norm_w"], w["conv_norm_b"], wout_buf, w["norm_ffn_w"], wup_buf, fcw_full,
        w["ffn_conv_b"], wdown_buf, core=ci)

    last = [gl["w_in"].reshape(D, 4, INW // 4).transpose(1, 0, 2)]
    parts = _pair_sum(last, _grad_swap(last, "grad_swap_in"), ci, "grad_pair_sum_in")

    pack = jnp.concatenate([dmod.reshape(6, D), dmodc.reshape(2, D)]
                           + [_pack_rows(gl[name], nr) for name, nr in _PACK], axis=0)
    pack = jnp.pad(pack, ((0, _PACK_ROWS - pack.shape[0]), (0, 0)))
    last_row = lax.broadcasted_iota(jnp.int32, pack.shape, 0) == _PACK_ROWS - 1
    pack = jnp.where(last_row, loss_loc, pack)
    gp, arrived, _ = _small_allgather(pack, "gather_small_grads", exchange=parts)
    gp = gp.reshape(8, _PACK_ROWS, D)
    late = [gl[n] for n in ("w_out", "w_up", "w_down")]
    shard_grads = _chip_sum([*parts, *[p for p, _ in late]], [*arrived, *[a for _, a in late]], chip, ci)
    tot = _sum_leading(gp, "sum_small_grads")
    loss = tot[_PACK_ROWS - 1, 0]
    g = {}
    r = 8
    for name, nr in _PACK:
        shape = gl[name].shape
        g[name] = tot[r:r + nr].reshape(-1)[:math.prod(shape)].reshape(shape)
        r += nr
    dmod_all = jnp.concatenate([gp[:, 0:6].reshape(8, 6 * D),
                                jnp.pad(tot[6:8].reshape(1, 2 * D), ((0, 0), (0, 4 * D))),
                                jnp.zeros((7, 6 * D), F32)], axis=0)
    g["b_mod"] = _sum_leading(dmod_all.reshape(16, 1, 6 * D), "grad_b_mod")
    g["w_mod"], part = _mod_bwd(cs, lax.dynamic_slice(dmod_all, (0, chip * ncol), (16, ncol)), w["w_mod"])
    cparts, _, (g["w_in"], g["w_out"], g["w_up"], g["w_down"]) = _small_allgather(part, "gather_cctx",
                                                                                 share=shard_grads)
    g["c_ctx"] = _cctx_grad(cparts.reshape(8, 16, D)[0::2], w["c_ctx"])
    g["conv_w"] = lax.dynamic_slice(g["conv_w"], (0, chip * 128), (CK, 128))
    g["ffn_conv_w"] = lax.dynamic_slice(g["ffn_conv_w"], (0, chip * 1408), (3, 1408))

    delta, new_m, new_v = {}, {}, {}
    for name in _BIG:
        delta[name], new_m[name], new_v[name] = _adam_big(w[name], g[name], m[name], v[name], "adam_" + name)
    ds, ms, vs = _adam_small([w[n] for n in _SMALL], [g[n] for n in _SMALL], [m[n] for n in _SMALL],
                             [v[n] for n in _SMALL])
    for k, name in enumerate(_SMALL):
        delta[name], new_m[name], new_v[name] = ds[k], ms[k], vs[k]

    def shaped(d, name):
        a = d[name]
        if name == "c_ctx":
            return a.reshape(D)
        if name in ("w_mod", "w_in", "w_out", "w_up", "w_down", "conv_w", "ffn_conv_w"):
            return a[None]
        return a

    outs = [loss, gx[None]]
    for d in (g, delta, new_m, new_v):
        outs += [shaped(d, name) for name in _ORDER]
    return tuple(outs)
```

```python
import functools
import math

import jax
import jax.numpy as jnp
from jax import lax
from jax.experimental import pallas as pl
from jax.experimental.pallas import tpu as pltpu

F32 = jnp.float32
_MX = jnp.bfloat16
EPS = 1e-6
NEG = -1e30
D = 1024
AW = 512
CW = 512
INW = 1792
FH = 2816
LC = 256
CK = 31
GRID_W = 64
TM = 512
TQ = 128
NCH = FH // 128
FQ = 2 * FH // 4
VMEM_LIMIT = 56 * 1024 * 1024
MESH = pl.DeviceIdType.MESH

ADAM_LR, ADAM_B1, ADAM_B2, ADAM_EPS, ADAM_WD, ADAM_STEP = 0.001, 0.9, 0.999, 1e-08, 0.01, 10


def _cp(*sem):
    return pltpu.CompilerParams(dimension_semantics=sem, vmem_limit_bytes=VMEM_LIMIT)


def _vspec():
    return pl.BlockSpec(memory_space=pltpu.VMEM)


def _sig(z):
    return 1.0 / (1.0 + jnp.exp(-z))


def _dot(a, b):
    return jnp.dot(a, b, preferred_element_type=F32)


def _dot_nt(a, b):
    return lax.dot_general(a, b, (((1,), (1,)), ((), ())), preferred_element_type=F32)


def _dot_tn(a, b):
    return lax.dot_general(a, b, (((0,), (0,)), ((), ())), preferred_element_type=F32)


def _gsum(v, g):
    hi = v.astype(_MX)
    lo = (v - hi.astype(F32)).astype(_MX)
    return _dot(hi, g) + _dot(lo, g)


def _headnorm(ch, w, g):
    r = lax.rsqrt(_gsum(ch * ch, g) * (1.0 / 64.0) + EPS)
    return ch * r * w


def _headnorm_bwd(ch, dy, w, g):
    r = lax.rsqrt(_gsum(ch * ch, g) * (1.0 / 64.0) + EPS)
    hat = ch * r
    dhat = dy * w
    dch = r * (dhat - hat * (_gsum(dhat * hat, g) * (1.0 / 64.0)))
    return dch, dy * hat


def _rope(u, cos, sa, sb):
    return u * cos + pltpu.roll(u, 112, 1) * sa + pltpu.roll(u, 16, 1) * sb


def _rope_bwd(d, cos, sa, sb):
    return d * cos - pltpu.roll(d, 112, 1) * sa - pltpu.roll(d, 16, 1) * sb


def _rowsum(v):
    return jnp.sum(v, axis=0, keepdims=True)


def _call_hosting_gather(body, gather, step_of, nsteps, *, in_specs, out_specs, out_shape, scratch_shapes=(),
                         comm=None, **kw):
    comm = comm or _Gather
    ng, n_in, n_out = len(gather), len(in_specs), len(out_specs)
    if not ng:
        return pl.pallas_call(body, in_specs=in_specs, out_specs=out_specs, out_shape=out_shape,
                              scratch_shapes=list(scratch_shapes), **kw)

    def hosted(*refs):
        ins, outs = refs[:n_in], refs[n_in + ng:n_in + ng + n_out]
        rest = refs[n_in + 2 * ng + n_out:]
        copies = comm(refs[n_in:n_in + ng], refs[n_in + ng + n_out:n_in + 2 * ng + n_out], rest[0], rest[1],
                      [b.shape[1] for b in gather])
        step = step_of()
        pl.when(step == 0)(copies.start)
        body(*ins, *outs, *rest[2:])
        pl.when(step == max(nsteps - 3, 0))(copies.forward)
        pl.when(step == nsteps - 1)(copies.finish)

    return pl.pallas_call(
        hosted, in_specs=list(in_specs) + _any_specs(ng), out_specs=list(out_specs) + _any_specs(ng),
        out_shape=list(out_shape) + comm.results(gather),
        input_output_aliases={n_in + w: n_out + w for w in range(ng)} if comm.in_place else {},
        scratch_shapes=comm.semaphores(ng) + list(scratch_shapes), **kw)


def _mix_in(x, nw, sc, sh, win, qnw2, knw2, cos, sa, sb, gmat, gather=()):
    s = x.shape[0]

    def body(x_ref, nw_ref, sc_ref, sh_ref, win_ref, qnw_ref, knw_ref, cos_ref, sa_ref, sb_ref, g_ref,
             proj_ref, h_ref, q0_ref, q1_ref, k_ref, ksw_ref, v_ref, vsw_ref, u0_ref):
        xv = x_ref[...]
        r = lax.rsqrt(jnp.mean(xv * xv, axis=-1, keepdims=True) + EPS)
        h = xv * r * (nw_ref[...] * (1.0 + sc_ref[...])) + sh_ref[...]
        hb = h.astype(_MX)
        h_ref[...] = hb
        proj = _dot(hb, win_ref[...])
        proj_ref[...] = proj
        cs, sav, sbv, g = cos_ref[...], sa_ref[...], sb_ref[...], g_ref[...]
        lo = lax.broadcasted_iota(jnp.int32, (1, 128), 1) < 64
        for j in range(4):
            q = _rope(_headnorm(proj[:, 128 * j:128 * (j + 1)], qnw_ref[...], g), cs, sav, sbv) * 0.125
            q0_ref[:, 128 * j:128 * (j + 1)] = jnp.where(lo, q, 0.0).astype(_MX)
            q1_ref[:, 128 * j:128 * (j + 1)] = jnp.where(lo, 0.0, q).astype(_MX)
        k = _rope(_headnorm(proj[:, 512:640], knw_ref[...], g), cs, sav, sbv)
        k_ref[...] = k.astype(_MX)
        ksw_ref[...] = pltpu.roll(k, 64, 1).astype(_MX)
        v = proj[:, 640:768]
        v_ref[...] = v.astype(_MX)
        vsw_ref[...] = pltpu.roll(v, 64, 1).astype(_MX)
        u0_ref[...] = proj[:, 768:1280] * _sig(proj[:, 1280:1792])

    row = lambda n: pl.BlockSpec((1, n), lambda i: (0, 0))
    tile = lambda n: pl.BlockSpec((TM, n), lambda i: (i, 0))
    res = _call_hosting_gather(
        body, gather, lambda: pl.program_id(0), s // TM, name="mix_in", grid=(s // TM,),
        in_specs=[tile(D), row(D), row(D), row(D), pl.BlockSpec((D, INW), lambda i: (0, 0)), row(128), row(128),
                  tile(128), tile(128), tile(128), pl.BlockSpec((128, 128), lambda i: (0, 0))],
        out_specs=[tile(INW), tile(D), tile(AW), tile(AW), tile(128), tile(128), tile(128), tile(128), tile(CW)],
        out_shape=[jax.ShapeDtypeStruct((s, INW), F32), jax.ShapeDtypeStruct((s, D), _MX),
                   jax.ShapeDtypeStruct((s, AW), _MX), jax.ShapeDtypeStruct((s, AW), _MX),
                   jax.ShapeDtypeStruct((s, 128), _MX), jax.ShapeDtypeStruct((s, 128), _MX),
                   jax.ShapeDtypeStruct((s, 128), _MX), jax.ShapeDtypeStruct((s, 128), _MX),
                   jax.ShapeDtypeStruct((s, CW), F32)],
        compiler_params=_cp("arbitrary"),
    )(x, nw, sc, sh, win, qnw2, knw2, cos, sa, sb, gmat, *gather)
    return res[:9], res[9:]


def _ctx_kv(ctx, nw, scc, shc, winkv, knw2, gmat):
    def body(ctx_ref, nw_ref, sc_ref, sh_ref, w_ref, knw_ref, g_ref,
             kvc_ref, hc_ref, kc_ref, kcsw_ref, vc_ref, vcsw_ref):
        cv = ctx_ref[...]
        r = lax.rsqrt(jnp.mean(cv * cv, axis=-1, keepdims=True) + EPS)
        hc = (cv * r * (nw_ref[...] * (1.0 + sc_ref[...])) + sh_ref[...]).astype(_MX)
        hc_ref[...] = hc
        kvc = _dot(hc, w_ref[...])
        kvc_ref[...] = kvc
        kc = _headnorm(kvc[:, :128], knw_ref[...], g_ref[...])
        kc_ref[...] = kc.astype(_MX)
        kcsw_ref[...] = pltpu.roll(kc, 64, 1).astype(_MX)
        vc = kvc[:, 128:]
        vc_ref[...] = vc.astype(_MX)
        vcsw_ref[...] = pltpu.roll(vc, 64, 1).astype(_MX)

    return pl.pallas_call(
        body, name="ctx_kv",
        in_specs=[_vspec()] * 7, out_specs=[_vspec()] * 6,
        out_shape=[jax.ShapeDtypeStruct((LC, 256), F32), jax.ShapeDtypeStruct((LC, D), _MX)]
        + [jax.ShapeDtypeStruct((LC, 128), _MX)] * 4,
        compiler_params=pltpu.CompilerParams(vmem_limit_bytes=VMEM_LIMIT),
    )(ctx, nw, scc, shc, winkv, knw2, gmat)


def _attn_mask(i, s):
    r = lax.broadcasted_iota(jnp.int32, (2 * TQ, LC + 3 * TQ), 0) % TQ
    cidx = lax.broadcasted_iota(jnp.int32, (2 * TQ, LC + 3 * TQ), 1)
    qpos = i * TQ + r
    kpos = (i - 1) * TQ + (cidx - LC)
    near = (jnp.abs(qpos - kpos) <= 128) & (kpos >= 0) & (kpos < s)
    return (cidx < LC) | near


def _attn_bias(i, s):
    return jnp.where(_attn_mask(i, s), 0.0, NEG)


def _attn_exp(sc, bias, sinkv):
    sc = sc + bias
    m = jnp.maximum(jnp.max(sc, axis=-1, keepdims=True), sinkv)
    ex = jnp.exp(sc - m)
    es = jnp.exp(sinkv - m)
    return ex, es, 1.0 / (jnp.sum(ex, axis=-1, keepdims=True) + es)


def _sink_rows(sink_ref, g, e):
    return jnp.concatenate([jnp.full((TQ, 1), sink_ref[4 * g + e], F32),
                            jnp.full((TQ, 1), sink_ref[4 * g + 2 + e], F32)], axis=0)


def _attn_fwd(q0, q1, kp, kswp, vp, vswp, kc, kcsw, vc, vcsw, sink, gather=()):
    s = q0.shape[0]
    nb = s // TQ

    def body(q0_ref, q1_ref, kp_ref, kswp_ref, vp_ref, vswp_ref, kc_ref, kcsw_ref, vc_ref, vcsw_ref, sink_ref,
             o_ref, ex_ref, stat_ref):
        i = pl.program_id(0)
        st = pl.multiple_of(i * TQ, TQ)
        kall = (jnp.concatenate([kc_ref[...], kp_ref[pl.ds(st, 3 * TQ), :]], axis=0),
                jnp.concatenate([kcsw_ref[...], kswp_ref[pl.ds(st, 3 * TQ), :]], axis=0))
        vall = (jnp.concatenate([vc_ref[...], vp_ref[pl.ds(st, 3 * TQ), :]], axis=0),
                jnp.concatenate([vcsw_ref[...], vswp_ref[pl.ds(st, 3 * TQ), :]], axis=0))
        bias = _attn_bias(i, s)
        lo = lax.broadcasted_iota(jnp.int32, (1, 128), 1) < 64
        qrefs = (q0_ref, q1_ref)
        out = [jnp.zeros((TQ, 128), F32) for _ in range(4)]
        combos = [(g, e) for g in range(2) for e in range(2)]
        scores = [_dot_nt(jnp.concatenate([qrefs[e][:, 256 * g:256 * g + 128],
                                           qrefs[e][:, 256 * g + 128:256 * g + 256]], axis=0),
                          kall[0 if e == g else 1]) for g, e in combos]
        parts = []
        for n, ((g, e), sc) in enumerate(zip(combos, scores)):
            ex, es, inv = _attn_exp(sc, bias, _sink_rows(sink_ref, g, e))
            parts.append((ex.astype(_MX), inv))
            ex_ref[n] = parts[-1][0]
            stat_ref[n] = jnp.where(lo, inv, es * inv)
        for (g, e), (ex, inv) in zip(combos, parts):
            me = lo if e == 0 else jnp.logical_not(lo)
            o2 = _dot(ex, vall[0 if e == g else 1]) * inv
            out[2 * g] = out[2 * g] + jnp.where(me, o2[:TQ], 0.0)
            out[2 * g + 1] = out[2 * g + 1] + jnp.where(me, o2[TQ:], 0.0)
        for j in range(4):
            o_ref[:, 128 * j:128 * (j + 1)] = out[j]

    full = lambda a: pl.BlockSpec(a.shape, lambda i: (0, 0))
    qs = pl.BlockSpec((TQ, AW), lambda i: (i, 0))
    res = _call_hosting_gather(
        body, gather, lambda: pl.program_id(0), nb, name="attn_fwd", grid=(nb,),
        in_specs=[qs, qs, full(kp), full(kp), full(kp), full(kp), full(kc), full(kc), full(kc), full(kc),
                  pl.BlockSpec(memory_space=pltpu.SMEM)],
        out_specs=[qs, pl.BlockSpec((None, 4, 2 * TQ, LC + 3 * TQ), lambda i: (i, 0, 0, 0)),
                   pl.BlockSpec((None, 4, 2 * TQ, 128), lambda i: (i, 0, 0, 0))],
        out_shape=[jax.ShapeDtypeStruct((s, AW), F32), jax.ShapeDtypeStruct((nb, 4, 2 * TQ, LC + 3 * TQ), _MX),
                   jax.ShapeDtypeStruct((nb, 4, 2 * TQ, 128), F32)],
        compiler_params=_cp("arbitrary"),
    )(q0, q1, kp, kswp, vp, vswp, kc, kcsw, vc, vcsw, sink, *gather)
    return res[:3], res[3:]


def _conv31(u0, cw32, cb):
    s = u0.shape[0]
    rch = 256

    def body(u_ref, w_ref, b_ref, o_ref, pad_ref):
        pad_ref[0:16, :] = jnp.zeros((16, 128), F32)
        pad_ref[s + 16:s + 32, :] = jnp.zeros((16, 128), F32)
        pad_ref[16:s + 16, :] = u_ref[...]
        for cidx in range(s // rch):
            base = cidx * rch
            acc = jnp.zeros((rch, 128), F32) + b_ref[...]
            for j in range(CK):
                acc = acc + w_ref[j:j + 1, :] * pad_ref[base + j + 1:base + j + 1 + rch, :]
            o_ref[base:base + rch, :] = acc

    return pl.pallas_call(
        body, name="conv31", grid=(CW // 128,),
        in_specs=[pl.BlockSpec((s, 128), lambda c: (0, c)), pl.BlockSpec((32, 128), lambda c: (0, c)),
                  pl.BlockSpec((1, 128), lambda c: (0, c))],
        out_specs=pl.BlockSpec((s, 128), lambda c: (0, c)),
        out_shape=jax.ShapeDtypeStruct((s, CW), F32),
        scratch_shapes=[pltpu.VMEM((s + 32, 128), F32)],
        compiler_params=_cp("arbitrary"),
    )(u0, cw32, cb)


def _ln_stats(u1):
    mu = jnp.mean(u1, axis=-1, keepdims=True)
    xc = u1 - mu
    rstd = lax.rsqrt(jnp.mean(xc * xc, axis=-1, keepdims=True) + EPS)
    return xc * rstd, rstd


def _mix_out(o, u1, cnw, cnb, wout, x, g1):
    s = x.shape[0]

    def body(o_ref, u1_ref, cnw_ref, cnb_ref, w_ref, x_ref, g1_ref, x1_ref, mix_ref, cat_ref):
        u2n, _ = _ln_stats(u1_ref[...])
        u2 = u2n * cnw_ref[...] + cnb_ref[...]
        u3 = u2 * _sig(u2)
        cat = jnp.concatenate([o_ref[...], u3], axis=1).astype(_MX)
        cat_ref[...] = cat
        mix = _dot(cat, w_ref[...])
        mix_ref[...] = mix
        x1_ref[...] = x_ref[...] + g1_ref[...] * mix

    row = lambda n: pl.BlockSpec((1, n), lambda i: (0, 0))
    tile = lambda n: pl.BlockSpec((TM, n), lambda i: (i, 0))
    return pl.pallas_call(
        body, name="mix_out", grid=(s // TM,),
        in_specs=[tile(AW), tile(CW), row(CW), row(CW), pl.BlockSpec((D, D), lambda i: (0, 0)), tile(D), row(D)],
        out_specs=[tile(D), tile(D), tile(D)],
        out_shape=[jax.ShapeDtypeStruct((s, D), F32), jax.ShapeDtypeStruct((s, D), F32),
                   jax.ShapeDtypeStruct((s, D), _MX)],
        compiler_params=_cp("arbitrary"),
    )(o, u1, cnw, cnb, wout, x, g1)


def _ffn_up(x1, nw, sc, sh, wup4, gather=()):
    s = x1.shape[0]
    tm = min(1024, s)

    def body(x_ref, nw_ref, sc_ref, sh_ref, w_ref, h2_ref, up_ref, h2s):
        @pl.when(pl.program_id(1) == 0)
        def _():
            xv = x_ref[...]
            r = lax.rsqrt(jnp.mean(xv * xv, axis=-1, keepdims=True) + EPS)
            h2 = (xv * r * (nw_ref[...] * (1.0 + sc_ref[...])) + sh_ref[...]).astype(_MX)
            h2s[...] = h2
            h2_ref[...] = h2
        up_ref[...] = _dot(h2s[...], w_ref[...]).astype(_MX)

    row = pl.BlockSpec((1, D), lambda i, j: (0, 0))
    res = _call_hosting_gather(
        body, gather, lambda: pl.program_id(0) * 4 + pl.program_id(1), 4 * (s // tm), name="ffn_up",
        grid=(s // tm, 4),
        in_specs=[pl.BlockSpec((tm, D), lambda i, j: (i, 0)), row, row, row,
                  pl.BlockSpec((None, D, FQ), lambda i, j: (j, 0, 0))],
        out_specs=[pl.BlockSpec((tm, D), lambda i, j: (i, 0)),
                   pl.BlockSpec((None, tm, FQ), lambda i, j: (j // 2, i, j % 2))],
        out_shape=[jax.ShapeDtypeStruct((s, D), _MX), jax.ShapeDtypeStruct((2, s, FH), _MX)],
        scratch_shapes=[pltpu.VMEM((tm, D), _MX)],
        compiler_params=_cp("arbitrary", "arbitrary"),
    )(x1, nw, sc, sh, wup4, *gather)
    return res[:2], res[2:]


def _fill_pad8(pad_ref, val, s):
    pad_ref[0:8, :] = jnp.zeros((8, 128), F32)
    pad_ref[s + 8:s + 16, :] = jnp.zeros((8, 128), F32)
    pad_ref[8:s + 8, :] = val


def _conv3_at(pad_ref, w_ref, half, base, rch):
    return (w_ref[half, 0:1, :] * pad_ref[base + 7:base + 7 + rch, :]
            + w_ref[half, 1:2, :] * pad_ref[base + 8:base + 8 + rch, :]
            + w_ref[half, 2:3, :] * pad_ref[base + 9:base + 9 + rch, :])


def _ffn_conv_act(up0, fcw, fcb):
    s = up0.shape[1]
    rch = 256

    def body(up_ref, w_ref, b_ref, act_ref, gv_ref, padg, padv):
        _fill_pad8(padg, up_ref[0].astype(F32), s)
        _fill_pad8(padv, up_ref[1].astype(F32), s)
        for cidx in range(s // rch):
            base = cidx * rch
            gate = _conv3_at(padg, w_ref, 0, base, rch) + b_ref[0]
            val = _conv3_at(padv, w_ref, 1, base, rch) + b_ref[1]
            act_ref[base:base + rch, :] = (gate * _sig(gate) * val).astype(_MX)
            gv_ref[0, base:base + rch, :] = gate.astype(_MX)
            gv_ref[1, base:base + rch, :] = val.astype(_MX)

    return pl.pallas_call(
        body, name="ffn_conv_act", grid=(NCH,),
        in_specs=[pl.BlockSpec((2, s, 128), lambda c: (0, 0, c)), pl.BlockSpec((2, 8, 128), lambda c: (0, 0, c)),
                  pl.BlockSpec((2, 1, 128), lambda c: (0, 0, c))],
        out_specs=[pl.BlockSpec((s, 128), lambda c: (0, c)), pl.BlockSpec((2, s, 128), lambda c: (0, 0, c))],
        out_shape=[jax.ShapeDtypeStruct((s, FH), _MX), jax.ShapeDtypeStruct((2, s, FH), _MX)],
        scratch_shapes=[pltpu.VMEM((s + 16, 128), F32)] * 2,
        compiler_params=_cp("arbitrary"),
    )(up0, fcw, fcb)


def _ffn_down(act, wdown, x1, g2, tgt):
    s = x1.shape[0]

    def body(act_ref, w_ref, x1_ref, g2_ref, tgt_ref, dy_ref, dact_ref, ddn_ref, vec_ref):
        @pl.when(pl.program_id(0) == 0)
        def _():
            vec_ref[...] = jnp.zeros((8, D), F32)
        dn = _dot(act_ref[...], w_ref[...])
        diff = x1_ref[...] + g2_ref[...] * dn - tgt_ref[...]
        dy = diff * (1.0 / D)
        dy_ref[...] = dy
        ddn = (dy * g2_ref[...]).astype(_MX)
        ddn_ref[...] = ddn
        dact_ref[...] = _dot_nt(ddn, w_ref[...]).astype(_MX)
        vec_ref[0:1, :] += _rowsum(dy * dn)
        vec_ref[1:2, :] += _rowsum(diff * diff)

    tile = lambda n: pl.BlockSpec((TM, n), lambda i: (i, 0))
    return pl.pallas_call(
        body, name="ffn_down", grid=(s // TM,),
        in_specs=[tile(FH), pl.BlockSpec((FH, D), lambda i: (0, 0)), tile(D), pl.BlockSpec((1, D), lambda i: (0, 0)),
                  tile(D)],
        out_specs=[tile(D), tile(FH), tile(D), pl.BlockSpec((8, D), lambda i: (0, 0))],
        out_shape=[jax.ShapeDtypeStruct((s, D), F32), jax.ShapeDtypeStruct((s, FH), _MX),
                   jax.ShapeDtypeStruct((s, D), _MX), jax.ShapeDtypeStruct((8, D), F32)],
        compiler_params=_cp("arbitrary"),
    )(act, wdown, x1, g2, tgt)


def _ffn_conv_bwd(up0, gv, dact, fcw):
    s = up0.shape[1]
    rch = 256

    def body(up_ref, gv_ref, da_ref, w_ref, dup_ref, gw_ref, dpg, dpv):
        for p in (dpg, dpv):
            p[0:8, :] = jnp.zeros((8, 128), F32)
            p[s + 8:s + 16, :] = jnp.zeros((8, 128), F32)
        for cidx in range(s // rch):
            rows = slice(cidx * rch, (cidx + 1) * rch)
            gate = gv_ref[0, rows, :].astype(F32)
            da = da_ref[rows, :].astype(F32)
            sg = _sig(gate)
            silu = gate * sg
            dpg[8 + cidx * rch:8 + (cidx + 1) * rch, :] = da * gv_ref[1, rows, :].astype(F32) * (sg + silu * (1.0 - sg))
            dpv[8 + cidx * rch:8 + (cidx + 1) * rch, :] = da * silu
        acc = [[jnp.zeros((1, 128), F32) for _ in range(4)] for _ in range(2)]
        for cidx in range(s // rch):
            base = cidx * rch
            for half, dp in enumerate((dpg, dpv)):
                shifted = [dp[base + 9 - j:base + 9 - j + rch, :] for j in range(3)]
                dup_ref[half, base:base + rch, :] = (w_ref[half, 0:1, :] * shifted[0] + w_ref[half, 1:2, :] * shifted[1]
                                                     + w_ref[half, 2:3, :] * shifted[2]).astype(_MX)
                xv = up_ref[half, base:base + rch, :].astype(F32)
                for j in range(3):
                    acc[half][j] = acc[half][j] + _rowsum(xv * shifted[j])
                acc[half][3] = acc[half][3] + _rowsum(shifted[1])
        for half in range(2):
            gw_ref[half] = jnp.zeros((8, 128), F32)
            for j in range(4):
                gw_ref[half, j:j + 1, :] = acc[half][j]

    both = pl.BlockSpec((2, s, 128), lambda c: (0, 0, c))
    return pl.pallas_call(
        body, name="ffn_conv_bwd", grid=(NCH,),
        in_specs=[both, both, pl.BlockSpec((s, 128), lambda c: (0, c)), pl.BlockSpec((2, 8, 128), lambda c: (0, 0, c))],
        out_specs=[both, pl.BlockSpec((2, 8, 128), lambda c: (0, 0, c))],
        out_shape=[jax.ShapeDtypeStruct((2, s, FH), _MX), jax.ShapeDtypeStruct((2, 8, FH), F32)],
        scratch_shapes=[pltpu.VMEM((s + 16, 128), F32)] * 2,
        compiler_params=_cp("arbitrary"),
    )(up0, gv, dact, fcw)


def _tn_matmul(a, b, tm, tn, name, b_split=False, by_chip=False, init=None):
    s, m = a.shape
    n = 2 * b.shape[2] if b_split else b.shape[1]
    ts = min(1024, s)
    nsteps = s // ts
    npb = (n // 2) // tn if b_split else None

    def body(*refs):
        if init is None:
            a_ref, b_ref, o_ref, acc = refs
        else:
            a_ref, b_ref, i_ref, o_ref, acc = refs
        k = pl.program_id(2)

        @pl.when(k == 0)
        def _():
            acc[...] = jnp.zeros((tm, tn), F32)
            if init is not None:
                acc[:, 512:768] = i_ref[...]
        acc[...] += _dot_tn(a_ref[...], b_ref[...])

        @pl.when(k == nsteps - 1)
        def _():
            o_ref[...] = acc[...].astype(_MX)

    if b_split:
        bspec = pl.BlockSpec((None, ts, tn), lambda i, j, k: (j // npb, k, j % npb))
    else:
        bspec = pl.BlockSpec((ts, tn), lambda i, j, k: (k, j))
    in_specs = [pl.BlockSpec((ts, tm), lambda i, j, k: (k, i)), bspec]
    args = [a, b]
    if init is not None:
        in_specs.append(pl.BlockSpec((tm, 256), lambda i, j, k: (i, 0)))
        args.append(init)
    if by_chip:
        out_spec = pl.BlockSpec((None, tm, tn), lambda i, j, k: (j, i, 0))
        out_shape = jax.ShapeDtypeStruct((n // tn, m, tn), _MX)
    else:
        out_spec = pl.BlockSpec((tm, tn), lambda i, j, k: (i, j))
        out_shape = jax.ShapeDtypeStruct((m, n), _MX)
    return pl.pallas_call(
        body, name=name, grid=(m // tm, n // tn, nsteps),
        in_specs=in_specs, out_specs=out_spec, out_shape=out_shape,
        scratch_shapes=[pltpu.VMEM((tm, tn), F32)],
        compiler_params=_cp("arbitrary", "arbitrary", "arbitrary"),
    )(*args)


def _ffn_up_bwd(dup, wup4, x1, dy, mix, nw, sc, g1):
    s = x1.shape[0]
    nk = 4

    def body(dup_ref, w_ref, x1_ref, dy_ref, mix_ref, nw_ref, sc_ref, g1_ref, dx1_ref, dmix_ref, vec_ref, acc):
        i, k = pl.program_id(0), pl.program_id(1)

        @pl.when(k == 0)
        def _():
            acc[...] = jnp.zeros((TM, D), F32)

        @pl.when((k == 0) & (i == 0))
        def _():
            vec_ref[...] = jnp.zeros((8, D), F32)
        acc[...] += _dot_nt(dup_ref[...], w_ref[...])

        @pl.when(k == nk - 1)
        def _():
            dh = acc[...]
            xv = x1_ref[...]
            r = lax.rsqrt(jnp.mean(xv * xv, axis=-1, keepdims=True) + EPS)
            xn = xv * r
            nwv, scv = nw_ref[...], sc_ref[...]
            vec_ref[0:1, :] += _rowsum(dh)
            vec_ref[1:2, :] += _rowsum(dh * xn) * nwv
            vec_ref[2:3, :] += _rowsum(dh * xn) * (1.0 + scv)
            dxn = dh * (nwv * (1.0 + scv))
            dx1 = dy_ref[...] + r * (dxn - xn * jnp.mean(dxn * xn, axis=-1, keepdims=True))
            dx1_ref[...] = dx1
            vec_ref[3:4, :] += _rowsum(dx1 * mix_ref[...])
            dmix_ref[...] = (dx1 * g1_ref[...]).astype(_MX)

    tile = pl.BlockSpec((TM, D), lambda i, k: (i, 0))
    row = pl.BlockSpec((1, D), lambda i, k: (0, 0))
    return pl.pallas_call(
        body, name="ffn_up_bwd", grid=(s // TM, nk),
        in_specs=[pl.BlockSpec((None, TM, FQ), lambda i, k: (k // 2, i, k % 2)),
                  pl.BlockSpec((None, D, FQ), lambda i, k: (k, 0, 0)), tile, tile, tile, row, row, row],
        out_specs=[tile, tile, pl.BlockSpec((8, D), lambda i, k: (0, 0))],
        out_shape=[jax.ShapeDtypeStruct((s, D), F32), jax.ShapeDtypeStruct((s, D), _MX),
                   jax.ShapeDtypeStruct((8, D), F32)],
        scratch_shapes=[pltpu.VMEM((TM, D), F32)],
        compiler_params=_cp("arbitrary", "arbitrary"),
    )(dup, wup4, x1, dy, mix, nw, sc, g1)


def _mix_out_bwd(dmix, wout, u1, cnw, cnb, swap=()):
    s = u1.shape[0]

    def body(dm_ref, w_ref, u1_ref, cnw_ref, cnb_ref, do_ref, du1_ref, vec_ref):
        @pl.when(pl.program_id(0) == 0)
        def _():
            vec_ref[...] = jnp.zeros((8, CW), F32)
        dcat = _dot_nt(dm_ref[...], w_ref[...])
        do_ref[...] = dcat[:, :AW]
        du3 = dcat[:, AW:]
        u2n, rstd = _ln_stats(u1_ref[...])
        u2 = u2n * cnw_ref[...] + cnb_ref[...]
        sg = _sig(u2)
        du2 = du3 * sg * (1.0 + u2 * (1.0 - sg))
        vec_ref[0:1, :] += _rowsum(du2)
        vec_ref[1:2, :] += _rowsum(du2 * u2n)
        d2n = du2 * cnw_ref[...]
        du1_ref[...] = rstd * (d2n - jnp.mean(d2n, axis=-1, keepdims=True)
                               - u2n * jnp.mean(d2n * u2n, axis=-1, keepdims=True))

    row = lambda n: pl.BlockSpec((1, n), lambda i: (0, 0))
    tile = lambda n: pl.BlockSpec((TM, n), lambda i: (i, 0))
    res = _call_hosting_gather(
        body, swap, lambda: pl.program_id(0), s // TM, comm=_Swap, name="mix_out_bwd", grid=(s // TM,),
        in_specs=[tile(D), pl.BlockSpec((D, D), lambda i: (0, 0)), tile(CW), row(CW), row(CW)],
        out_specs=[tile(AW), tile(CW), pl.BlockSpec((8, CW), lambda i: (0, 0))],
        out_shape=[jax.ShapeDtypeStruct((s, AW), F32), jax.ShapeDtypeStruct((s, CW), F32),
                   jax.ShapeDtypeStruct((8, CW), F32)],
        compiler_params=_cp("arbitrary"),
    )(dmix, wout, u1, cnw, cnb, *swap)
    return res[:3], res[3:]


def _conv31_bwd(du1, u0, cw32, proj, exchange=()):
    s = u0.shape[0]
    rch = 256

    def body(d_ref, u_ref, w_ref, ga_ref, gb_ref, dga_ref, dgb_ref, gw_ref, padu, padd):
        for p, src in ((padu, u_ref), (padd, d_ref)):
            p[0:16, :] = jnp.zeros((16, 128), F32)
            p[s + 16:s + 32, :] = jnp.zeros((16, 128), F32)
            p[16:s + 16, :] = src[...]
        for j in range(CK):
            acc = jnp.zeros((1, 128), F32)
            for cidx in range(s // rch):
                base = cidx * rch
                acc = acc + _rowsum(d_ref[base:base + rch, :] * padu[base + j + 1:base + j + 1 + rch, :])
            gw_ref[j:j + 1, :] = acc
        gw_ref[CK:CK + 1, :] = _rowsum(d_ref[...])
        for cidx in range(s // rch):
            base = cidx * rch
            du0 = jnp.zeros((rch, 128), F32)
            for j in range(CK):
                du0 = du0 + w_ref[j:j + 1, :] * padd[base + 31 - j:base + 31 - j + rch, :]
            sg = _sig(gb_ref[base:base + rch, :])
            ga = ga_ref[base:base + rch, :]
            dga_ref[base:base + rch, :] = (du0 * sg).astype(_MX)
            dgb_ref[base:base + rch, :] = (du0 * ga * sg * (1.0 - sg)).astype(_MX)

    blk = lambda off: pl.BlockSpec((s, 128), lambda c: (0, c + off))
    res = _call_hosting_gather(
        body, exchange, lambda: pl.program_id(0), CW // 128, comm=_Exchange, name="conv31_bwd", grid=(CW // 128,),
        in_specs=[blk(0), blk(0), pl.BlockSpec((32, 128), lambda c: (0, c)), blk(6), blk(10)],
        out_specs=[blk(0), blk(0), pl.BlockSpec((32, 128), lambda c: (0, c))],
        out_shape=[jax.ShapeDtypeStruct((s, CW), _MX), jax.ShapeDtypeStruct((s, CW), _MX),
                   jax.ShapeDtypeStruct((32, CW), F32)],
        scratch_shapes=[pltpu.VMEM((s + 32, 128), F32)] * 2,
        compiler_params=_cp("arbitrary"),
    )(du1, u0, cw32, proj, proj, *exchange)
    return res[:3], res[3:]


def _attn_bwd(q0, q1, kp, kswp, vp, vswp, kc, kcsw, vc, vcsw, ex, stat, o, do, exchange=()):
    s = q0.shape[0]
    nb = s // TQ
    ne = len(exchange)

    def body(*refs):
        (q0_ref, q1_ref, kp_ref, kswp_ref, vp_ref, vswp_ref, kc_ref, kcsw_ref, vc_ref, vcsw_ref, ex_ref, stat_ref,
         o_ref, do_ref) = refs[:14]
        dq_ref, dk_ref, dv_ref, dkc_ref, dvc_ref, dsink_ref = refs[14 + ne:20 + ne]
        i = pl.program_id(0)
        if ne:
            exch = _Exchange(refs[14:14 + ne], refs[20 + ne:20 + 2 * ne], refs[20 + 2 * ne], refs[21 + 2 * ne])
            pl.when(i == 0)(exch.start)

        @pl.when(i == 0)
        def _():
            dk_ref[...] = jnp.zeros((s + 2 * TQ, 128), F32)
            dv_ref[...] = jnp.zeros((s + 2 * TQ, 128), F32)
            dkc_ref[...] = jnp.zeros((LC, 128), F32)
            dvc_ref[...] = jnp.zeros((LC, 128), F32)
            dsink_ref[...] = jnp.zeros((8, 128), F32)
        st = pl.multiple_of(i * TQ, TQ)
        kall = (jnp.concatenate([kc_ref[...], kp_ref[pl.ds(st, 3 * TQ), :]], axis=0),
                jnp.concatenate([kcsw_ref[...], kswp_ref[pl.ds(st, 3 * TQ), :]], axis=0))
        vall = (jnp.concatenate([vc_ref[...], vp_ref[pl.ds(st, 3 * TQ), :]], axis=0),
                jnp.concatenate([vcsw_ref[...], vswp_ref[pl.ds(st, 3 * TQ), :]], axis=0))
        lo = lax.broadcasted_iota(jnp.int32, (1, 128), 1) < 64
        qrefs = (q0_ref, q1_ref)
        dq = [jnp.zeros((TQ, 128), F32) for _ in range(4)]
        dkt = jnp.zeros((128, LC + 3 * TQ), F32)
        dvt = jnp.zeros((128, LC + 3 * TQ), F32)
        combos = [(g, e) for g in range(2) for e in range(2)]

        def stacked(ref, g):
            return jnp.concatenate([ref[:, 256 * g:256 * g + 128], ref[:, 256 * g + 128:256 * g + 256]], axis=0)

        for n, (g, e) in enumerate(combos):
            me = lo if e == 0 else jnp.logical_not(lo)
            sw = 0 if e == g else 1
            qm = stacked(qrefs[e], g)
            p = ex_ref[n].astype(F32) * stat_ref[n, :, 0:1]
            dom = jnp.where(me, stacked(do_ref, g), 0.0)
            dd = jnp.sum(dom * stacked(o_ref, g), axis=-1, keepdims=True)
            domx = dom.astype(_MX)
            sd = stat_ref[n, :, 64:65] * dd
            dsink_ref[4 * g + e:4 * g + e + 1, :] -= jnp.sum(sd[:TQ], axis=0, keepdims=True)
            dsink_ref[4 * g + 2 + e:4 * g + 3 + e, :] -= jnp.sum(sd[TQ:], axis=0, keepdims=True)
            ds = (p * (_dot_nt(domx, vall[sw]) - dd)).astype(_MX)
            dq2 = _dot(ds, kall[sw])
            dq[2 * g] = dq[2 * g] + jnp.where(me, dq2[:TQ], 0.0)
            dq[2 * g + 1] = dq[2 * g + 1] + jnp.where(me, dq2[TQ:], 0.0)
            dk2 = _dot_tn(qm, ds)
            dv2 = _dot_tn(domx, p.astype(_MX))
            if sw:
                dk2 = pltpu.roll(dk2, 64, 0)
                dv2 = pltpu.roll(dv2, 64, 0)
            dkt = dkt + dk2
            dvt = dvt + dv2
        dkt = dkt.T
        dvt = dvt.T
        for j in range(4):
            dq_ref[:, 128 * j:128 * (j + 1)] = dq[j] * 0.125
        dkc_ref[...] += dkt[:LC]
        dvc_ref[...] += dvt[:LC]
        dk_ref[pl.ds(st, 3 * TQ), :] += dkt[LC:]
        dv_ref[pl.ds(st, 3 * TQ), :] += dvt[LC:]
        if ne:
            pl.when(i == nb - 1)(exch.finish)

    full = lambda a: pl.BlockSpec(a.shape, lambda i: (0, 0))
    fs = lambda r: pl.BlockSpec((r, 128), lambda i: (0, 0))
    qs = pl.BlockSpec((TQ, AW), lambda i: (i, 0))
    res = pl.pallas_call(
        body, name="attn_bwd", grid=(nb,),
        in_specs=[qs, qs, full(kp), full(kp), full(kp), full(kp), full(kc), full(kc), full(kc), full(kc),
                  pl.BlockSpec((None, 4, 2 * TQ, LC + 3 * TQ), lambda i: (i, 0, 0, 0)),
                  pl.BlockSpec((None, 4, 2 * TQ, 128), lambda i: (i, 0, 0, 0)), qs, qs] + _any_specs(ne),
        out_specs=[qs, fs(s + 2 * TQ), fs(s + 2 * TQ), fs(LC), fs(LC), fs(8)] + _any_specs(ne),
        out_shape=[jax.ShapeDtypeStruct((s, AW), F32), jax.ShapeDtypeStruct((s + 2 * TQ, 128), F32),
                   jax.ShapeDtypeStruct((s + 2 * TQ, 128), F32), jax.ShapeDtypeStruct((LC, 128), F32),
                   jax.ShapeDtypeStruct((LC, 128), F32), jax.ShapeDtypeStruct((8, 128), F32)]
        + _Exchange.out_shapes(exchange),
        scratch_shapes=_Exchange.semaphores(ne) if ne else [],
        compiler_params=_cp("arbitrary"),
    )(q0, q1, kp, kswp, vp, vswp, kc, kcsw, vc, vcsw, ex, stat, o, do, *exchange)
    return res[:6], res[6:]


def _mix_in_bwd(dq, dk, dv, dga, dgb, proj, x, dx1, win, nw, sc, qnw2, knw2, cos, sa, sb, gmat):
    s = x.shape[0]

    def body(dq_ref, dk_ref, dv_ref, dga_ref, dgb_ref, proj_ref, x_ref, dx1_ref, win_ref, nw_ref, sc_ref,
             qnw_ref, knw_ref, cos_ref, sa_ref, sb_ref, g_ref, gx_ref, dproj_ref, vec_ref):
        @pl.when(pl.program_id(0) == 0)
        def _():
            vec_ref[...] = jnp.zeros((8, D), F32)
        cs, sav, sbv, g = cos_ref[...], sa_ref[...], sb_ref[...], g_ref[...]
        gq = jnp.zeros((1, 128), F32)
        for j in range(4):
            dqn = _rope_bwd(dq_ref[:, 128 * j:128 * (j + 1)], cs, sav, sbv)
            dch, gw = _headnorm_bwd(proj_ref[:, 128 * j:128 * (j + 1)], dqn, qnw_ref[...], g)
            dproj_ref[:, 128 * j:128 * (j + 1)] = dch.astype(_MX)
            gq = gq + _rowsum(gw)
        dkn = _rope_bwd(dk_ref[...], cs, sav, sbv)
        dch, gw = _headnorm_bwd(proj_ref[:, 512:640], dkn, knw_ref[...], g)
        dproj_ref[:, 512:640] = dch.astype(_MX)
        dproj_ref[:, 640:768] = dv_ref[...].astype(_MX)
        dproj_ref[:, 768:1280] = dga_ref[...]
        dproj_ref[:, 1280:1792] = dgb_ref[...]
        vec_ref[3:4, 0:128] += gq
        vec_ref[3:4, 128:256] += _rowsum(gw)
        dh = _dot_nt(dproj_ref[...], win_ref[...])
        xv = x_ref[...]
        r = lax.rsqrt(jnp.mean(xv * xv, axis=-1, keepdims=True) + EPS)
        xn = xv * r
        nwv, scv = nw_ref[...], sc_ref[...]
        vec_ref[0:1, :] += _rowsum(dh)
        vec_ref[1:2, :] += _rowsum(dh * xn) * nwv
        vec_ref[2:3, :] += _rowsum(dh * xn) * (1.0 + scv)
        dxn = dh * (nwv * (1.0 + scv))
        gx_ref[...] = dx1_ref[...] + r * (dxn - xn * jnp.mean(dxn * xn, axis=-1, keepdims=True))

    row = lambda n: pl.BlockSpec((1, n), lambda i: (0, 0))
    tile = lambda n: pl.BlockSpec((TM, n), lambda i: (i, 0))
    return pl.pallas_call(
        body, name="mix_in_bwd", grid=(s // TM,),
        in_specs=[tile(AW), tile(128), tile(128), tile(CW), tile(CW), tile(INW), tile(D), tile(D),
                  pl.BlockSpec((D, INW), lambda i: (0, 0)), row(D), row(D), row(128), row(128),
                  tile(128), tile(128), tile(128), pl.BlockSpec((128, 128), lambda i: (0, 0))],
        out_specs=[tile(D), tile(INW), pl.BlockSpec((8, D), lambda i: (0, 0))],
        out_shape=[jax.ShapeDtypeStruct((s, D), F32), jax.ShapeDtypeStruct((s, INW), _MX),
                   jax.ShapeDtypeStruct((8, D), F32)],
        compiler_params=_cp("arbitrary"),
    )(dq, dk, dv, dga, dgb, proj, x, dx1, win, nw, sc, qnw2, knw2, cos, sa, sb, gmat)


def _ctx_bwd(ctx, nw, scc, winkv, kvc, hc, dkc, dvc, knw2, gmat):
    def body(ctx_ref, nw_ref, sc_ref, w_ref, kvc_ref, hc_ref, dkc_ref, dvc_ref, knw_ref, g_ref, gw_ref, vec_ref):
        dkr, gk = _headnorm_bwd(kvc_ref[:, 0:128], dkc_ref[...], knw_ref[...], g_ref[...])
        dkv = jnp.concatenate([dkr, dvc_ref[...]], axis=1).astype(_MX)
        gw_ref[...] = _dot_tn(hc_ref[...], dkv)
        dh = _dot_nt(dkv, w_ref[...])
        cv = ctx_ref[...]
        r = lax.rsqrt(jnp.mean(cv * cv, axis=-1, keepdims=True) + EPS)
        cn = cv * r
        vec_ref[...] = jnp.zeros((8, D), F32)
        vec_ref[0:1, :] = _rowsum(dh)
        vec_ref[1:2, :] = _rowsum(dh * cn) * nw_ref[...]
        vec_ref[2:3, :] = _rowsum(dh * cn) * (1.0 + sc_ref[...])
        vec_ref[3:4, 0:128] = _rowsum(gk)

    return pl.pallas_call(
        body, name="ctx_bwd", in_specs=[_vspec()] * 10, out_specs=[_vspec()] * 2,
        out_shape=[jax.ShapeDtypeStruct((D, 256), F32), jax.ShapeDtypeStruct((8, D), F32)],
        compiler_params=pltpu.CompilerParams(vmem_limit_bytes=VMEM_LIMIT),
    )(ctx, nw, scc, winkv, kvc, hc, dkc, dvc, knw2, gmat)


def _mod_fwd(cs, wmod, bloc):
    def body(c_ref, w_ref, b_ref, o_ref):
        cv = c_ref[...]
        o_ref[...] = _dot((cv * _sig(cv)).astype(_MX), w_ref[...].astype(_MX)) + b_ref[...]

    return pl.pallas_call(
        body, name="mod_fwd", in_specs=[_vspec()] * 3, out_specs=_vspec(),
        out_shape=jax.ShapeDtypeStruct((16, wmod.shape[1]), F32),
        compiler_params=pltpu.CompilerParams(vmem_limit_bytes=VMEM_LIMIT),
    )(cs, wmod, bloc)


def _mod_bwd(cs, dmod, wmod):
    def body(c_ref, d_ref, w_ref, gw_ref, part_ref):
        cv = c_ref[...]
        sl = (cv * _sig(cv)).astype(_MX)
        dm = d_ref[...].astype(_MX)
        gw_ref[...] = _dot_tn(sl, dm)
        part_ref[...] = _dot_nt(dm, w_ref[...].astype(_MX))

    return pl.pallas_call(
        body, name="mod_bwd", in_specs=[_vspec()] * 3, out_specs=[_vspec()] * 2,
        out_shape=[jax.ShapeDtypeStruct(wmod.shape, F32), jax.ShapeDtypeStruct((16, D), F32)],
        compiler_params=pltpu.CompilerParams(vmem_limit_bytes=VMEM_LIMIT),
    )(cs, dmod, wmod)


def _sum_leading(a, name):
    n = a.shape[0]

    def body(a_ref, o_ref):
        acc = a_ref[0]
        for k in range(1, n):
            acc = acc + a_ref[k]
        o_ref[...] = acc

    return pl.pallas_call(
        body, name=name, in_specs=[_vspec()], out_specs=_vspec(),
        out_shape=jax.ShapeDtypeStruct(a.shape[1:], F32),
        compiler_params=pltpu.CompilerParams(vmem_limit_bytes=VMEM_LIMIT),
    )(a)


def _cctx_grad(parts, cc):
    def body(p_ref, c_ref, o_ref):
        acc = p_ref[0, 8:9, :]
        for k in range(1, 4):
            acc = acc + p_ref[k, 8:9, :]
        cv = c_ref[...]
        sg = _sig(cv)
        o_ref[...] = acc * (sg * (1.0 + cv * (1.0 - sg)))

    return pl.pallas_call(
        body, name="cctx_grad", in_specs=[_vspec()] * 2, out_specs=_vspec(),
        out_shape=jax.ShapeDtypeStruct((1, D), F32),
    )(parts, cc)


def _adam_math(w, g, m, v):
    mn = ADAM_B1 * m + (1.0 - ADAM_B1) * g
    vn = ADAM_B2 * v + (1.0 - ADAM_B2) * (g * g)
    mh = mn / (1.0 - ADAM_B1 ** ADAM_STEP)
    vh = vn / (1.0 - ADAM_B2 ** ADAM_STEP)
    delta = -ADAM_LR * (mh / (jnp.sqrt(vh) + ADAM_EPS) + ADAM_WD * w)
    return delta, mn, vn


def _adam_big(w, g, m, v, name):
    r, n = w.shape
    tr = 256 if r % 256 == 0 else 64

    def body(w_ref, g_ref, m_ref, v_ref, d_ref, mo_ref, vo_ref):
        d, mn, vn = _adam_math(w_ref[...], g_ref[...], m_ref[...], v_ref[...])
        d_ref[...] = d
        mo_ref[...] = mn
        vo_ref[...] = vn

    spec = pl.BlockSpec((tr, n), lambda i: (i, 0))
    return pl.pallas_call(
        body, name=name, grid=(r // tr,), in_specs=[spec] * 4, out_specs=[spec] * 3,
        out_shape=[jax.ShapeDtypeStruct((r, n), F32)] * 3,
        compiler_params=_cp("arbitrary"),
    )(w, g, m, v)


def _adam_small(ws, gs, ms, vs):
    n = len(ws)

    def body(*refs):
        ins, outs = refs[:4 * n], refs[4 * n:]
        for k in range(n):
            d, mn, vn = _adam_math(ins[k][...], ins[n + k][...], ins[2 * n + k][...], ins[3 * n + k][...])
            outs[k][...] = d
            outs[n + k][...] = mn
            outs[2 * n + k][...] = vn

    shapes = [jax.ShapeDtypeStruct(w.shape, F32) for w in ws]
    res = pl.pallas_call(
        body, name="adam_small", in_specs=[_vspec()] * (4 * n), out_specs=[_vspec()] * (3 * n),
        out_shape=shapes * 3,
    )(*ws, *gs, *ms, *vs)
    return res[:n], res[n:2 * n], res[2 * n:]


def _pair_sum(grads, from_sib, core, name):
    n = len(grads)

    def body(c_ref, *refs):
        for w in range(n):
            a_ref, b_ref, o_ref = refs[w], refs[n + w], refs[2 * n + w]
            o_ref[...] = (a_ref[...].astype(F32) + b_ref[...].astype(F32)).astype(_MX)

    halves = [(None, g.shape[1] // 2, g.shape[2]) for g in grads]
    return pl.pallas_call(
        body, name=name,
        grid_spec=pltpu.PrefetchScalarGridSpec(
            num_scalar_prefetch=1, grid=(4,),
            in_specs=[pl.BlockSpec(h, lambda j, c: (j, c[0], 0)) for h in halves]
            + [pl.BlockSpec(h, lambda j, c: (j, 0, 0)) for h in halves],
            out_specs=[pl.BlockSpec(h, lambda j, c: (j, 0, 0)) for h in halves]),
        out_shape=[jax.ShapeDtypeStruct(p.shape, _MX) for p in from_sib],
        compiler_params=_cp("arbitrary"),
    )(core.reshape(1), *grads, *from_sib)


def _chip_sum(parts, arrived, chip, core):
    n = len(parts)

    def body(s_ref, *refs):
        for w in range(n):
            own_ref, p_ref, o_ref = refs[w], refs[n + w], refs[2 * n + w]
            acc = own_ref[...].astype(F32)
            for k in range(3):
                acc = acc + p_ref[k].astype(F32)
            o_ref[...] = acc

    blk = [(p.shape[1] // 2, p.shape[2]) for p in parts]
    return pl.pallas_call(
        body, name="grad_chip_sum",
        grid_spec=pltpu.PrefetchScalarGridSpec(
            num_scalar_prefetch=1, grid=(2,),
            in_specs=[pl.BlockSpec((None,) + b, lambda i, s: (s[0], i, 0)) for b in blk]
            + [pl.BlockSpec((3,) + b, lambda i, s: (0, i, 0)) for b in blk],
            out_specs=[pl.BlockSpec(b, lambda i, s: (2 * s[1] + i, 0)) for b in blk]),
        out_shape=[jax.ShapeDtypeStruct((2 * p.shape[1], p.shape[2]), F32) for p in parts],
        compiler_params=_cp("arbitrary"),
    )(jnp.stack([chip, core]), *parts, *arrived)


def _cast_shards(ws, chip):
    n = len(ws)

    def body(s_ref, *refs):
        for w in range(n):
            refs[n + w][...] = refs[w][...].astype(_MX)

    blk = [(a.shape[0] // 2, a.shape[1]) for a in ws]
    return pl.pallas_call(
        body, name="cast_shards",
        grid_spec=pltpu.PrefetchScalarGridSpec(
            num_scalar_prefetch=1, grid=(2,),
            in_specs=[pl.BlockSpec(b, lambda i, s: (i, 0)) for b in blk],
            out_specs=[pl.BlockSpec((None,) + b, lambda i, s: (s[0], i, 0)) for b in blk]),
        out_shape=[jax.ShapeDtypeStruct((4,) + a.shape, _MX) for a in ws],
        compiler_params=_cp("arbitrary"),
    )(chip.reshape(1), *ws)


def _position():
    x, y, c = lax.axis_index("x"), lax.axis_index("y"), lax.axis_index("c")
    return x, y, c


def _small_allgather(blk, name, exchange=(), share=(), gather=()):
    m_per, n = blk.shape
    assert not (share and gather)
    share = tuple(share) + tuple(gather)
    ne, ns = len(exchange), len(share)

    def body(*refs):
        x_ref = refs[0]
        out_ref = refs[1 + ne + ns]
        send_sems, recv_sems = refs[2 + 2 * ne + 2 * ns], refs[3 + 2 * ne + 2 * ns]
        extra = refs[4 + 2 * ne + 2 * ns:]
        x, y, c = _position()
        me, sibling = (x, y, c), (x, y, 1 - c)
        chips = [(1 - x, y), (x, 1 - y), (1 - x, 1 - y)]
        if ne:
            exch = _Exchange(refs[1:1 + ne], refs[2 + ne + ns:2 + 2 * ne + ns], extra[0], extra[1])
            exch.start()
        if gather:
            gat = _Gather(refs[1 + ne:1 + ne + ns], refs[2 + 2 * ne + ns:2 + 2 * ne + 2 * ns],
                          extra[2 * bool(ne)], extra[2 * bool(ne) + 1], [b.shape[1] for b in gather])
            gat.start()
        elif ns:
            halves = _share_copies(refs[1 + ne:1 + ne + ns], refs[2 + 2 * ne + ns:2 + 2 * ne + 2 * ns],
                                   extra[2 * bool(ne)], extra[2 * bool(ne) + 1], [b.shape[0] for b in share])
            for cp in halves:
                cp.start()

        def rows(px, py, pc):
            return out_ref.at[pl.ds(pl.multiple_of((4 * px + 2 * py + pc) * m_per, 8), m_per), :]

        def copy(k, block, to, src=None):
            return pltpu.make_async_remote_copy(
                src_ref=rows(*block) if src is None else src, dst_ref=rows(*block),
                send_sem=send_sems.at[k], recv_sem=recv_sems.at[k], device_id=to, device_id_type=MESH)

        out_ref[pl.ds(pl.multiple_of((4 * x + 2 * y + c) * m_per, 8), m_per), :] = x_ref[...]
        first = [copy(0, me, sibling, src=x_ref)]
        first += [copy(1 + j, me, (*chip, c), src=x_ref) for j, chip in enumerate(chips)]
        for cp in first:
            cp.start()
        passed = [copy(4 + j, (*chip, c), sibling) for j, chip in enumerate(chips)]
        for j, chip in enumerate(chips):
            copy(1 + j, (*chip, c), me).wait_recv()
            passed[j].start()
        copy(0, sibling, me).wait_recv()
        for j, chip in enumerate(chips):
            copy(4 + j, (*chip, 1 - c), me).wait_recv()
        for cp in first + passed:
            cp.wait_send()
        if ne:
            exch.finish()
        if gather:
            gat.forward()
            gat.finish()
        elif ns:
            for cp in halves:
                cp.wait()

    res = pl.pallas_call(
        body, name=name,
        out_shape=[jax.ShapeDtypeStruct((8 * m_per, n), blk.dtype)] + _Exchange.out_shapes(exchange)
        + [jax.ShapeDtypeStruct(b.shape, b.dtype) for b in share],
        in_specs=[_vspec()] + _any_specs(ne + ns), out_specs=[_vspec()] + _any_specs(ne + ns),
        input_output_aliases={1 + ne + w: 1 + ne + w for w in range(ns)},
        scratch_shapes=[pltpu.SemaphoreType.DMA((7,)), pltpu.SemaphoreType.DMA((7,))]
        + (_Exchange.semaphores(ne) if ne else [])
        + (_Gather.semaphores(ns) if gather else
           [pltpu.SemaphoreType.DMA((ns,)), pltpu.SemaphoreType.DMA((ns,))] if ns else []),
    )(blk, *exchange, *share)
    return res[0], res[1:1 + ne], res[1 + ne:]


def _share_copies(src_refs, out_refs, send_sems, recv_sems, nrows):
    x, y, c = _position()
    return [pltpu.make_async_remote_copy(
        src_ref=_rows(src_refs[w], c, r), dst_ref=_rows(out_refs[w], c, r), send_sem=send_sems.at[w],
        recv_sem=recv_sems.at[w], device_id=(x, y, 1 - c), device_id_type=MESH) for w, r in enumerate(nrows)]


def _any_specs(n):
    return [pl.BlockSpec(memory_space=pl.ANY)] * n


def _rows(ref, half, nrows):
    return ref.at[pl.ds(half * (nrows // 2), nrows // 2), :]


class _Gather:
    def __init__(self, src_refs, out_refs, send_sems, recv_sems, nrows):
        x, y, c = _position()
        chip = 2 * x + y
        sibling = (x, y, 1 - c)

        def copy(k, src, dst, to):
            return pltpu.make_async_remote_copy(src_ref=src, dst_ref=dst, send_sem=send_sems.at[k],
                                                recv_sem=recv_sems.at[k], device_id=to, device_id_type=MESH)

        self.first, self.first_landed, self.passed, self.passed_landed = [], [], [], []
        for w, r in enumerate(nrows):
            for j, ch in enumerate([(1 - x, y), (x, 1 - y), (1 - x, 1 - y)]):
                theirs = _rows(out_refs[w].at[2 * ch[0] + ch[1]], c, r)
                other = _rows(out_refs[w].at[2 * ch[0] + ch[1]], 1 - c, r)
                self.first.append(copy(6 * w + j, _rows(src_refs[w].at[chip], c, r),
                                       _rows(out_refs[w].at[chip], c, r), (*ch, c)))
                self.first_landed.append(copy(6 * w + j, theirs, theirs, sibling))
                self.passed.append(copy(6 * w + 3 + j, theirs, theirs, sibling))
                self.passed_landed.append(copy(6 * w + 3 + j, other, other, sibling))

    in_place = True

    @staticmethod
    def semaphores(n):
        return [pltpu.SemaphoreType.DMA((6 * n,)), pltpu.SemaphoreType.DMA((6 * n,))]

    @staticmethod
    def results(bufs):
        return [jax.ShapeDtypeStruct(b.shape, b.dtype) for b in bufs]

    def start(self):
        for cp in self.first:
            cp.start()

    def forward(self):
        for landed, cp in zip(self.first_landed, self.passed):
            landed.wait_recv()
            cp.start()

    def finish(self):
        for landed in self.passed_landed:
            landed.wait_recv()
        for cp in self.first + self.passed:
            cp.wait_send()


class _Swap:
    in_place = False

    def __init__(self, grad_refs, out_refs, send_sems, recv_sems, nrows):
        x, y, c = _position()
        self.copies = [pltpu.make_async_remote_copy(
            src_ref=grad_refs[w].at[:, pl.ds((1 - c) * (r // 2), r // 2), :], dst_ref=out_refs[w],
            send_sem=send_sems.at[w], recv_sem=recv_sems.at[w], device_id=(x, y, 1 - c), device_id_type=MESH)
            for w, r in enumerate(nrows)]

    @staticmethod
    def semaphores(n):
        return [pltpu.SemaphoreType.DMA((n,)), pltpu.SemaphoreType.DMA((n,))]

    @staticmethod
    def results(grads):
        return [jax.ShapeDtypeStruct((4, g.shape[1] // 2, g.shape[2]), g.dtype) for g in grads]

    def start(self):
        for cp in self.copies:
            cp.start()

    def forward(self):
        pass

    def finish(self):
        for cp in self.copies:
            cp.wait()


def _grad_swap(grads, name):
    n = len(grads)

    def body(*refs):
        swap = _Swap(refs[:n], refs[n:2 * n], refs[2 * n], refs[2 * n + 1], [g.shape[1] for g in grads])
        swap.start()
        swap.finish()

    return pl.pallas_call(
        body, name=name, out_shape=_Swap.results(grads), in_specs=_any_specs(n), out_specs=_any_specs(n),
        scratch_shapes=_Swap.semaphores(n),
    )(*grads)


class _Exchange:
    in_place = False

    def __init__(self, part_refs, out_refs, send_sems, recv_sems, nrows=None):
        x, y, c = _position()
        self.copies = [
            pltpu.make_async_remote_copy(
                src_ref=part_refs[w].at[2 * ch[0] + ch[1]], dst_ref=out_refs[w].at[k], send_sem=send_sems.at[3 * w + k],
                recv_sem=recv_sems.at[3 * w + k], device_id=(*ch, c), device_id_type=MESH)
            for w in range(len(part_refs)) for k, ch in enumerate([(1 - x, y), (x, 1 - y), (1 - x, 1 - y)])]

    @staticmethod
    def semaphores(n):
        return [pltpu.SemaphoreType.DMA((3 * n,)), pltpu.SemaphoreType.DMA((3 * n,))]

    @staticmethod
    def out_shapes(parts):
        return [jax.ShapeDtypeStruct((3,) + p.shape[1:], p.dtype) for p in parts]

    results = out_shapes

    def start(self):
        for cp in self.copies:
            cp.start()

    def forward(self):
        pass

    def finish(self):
        for cp in self.copies:
            cp.wait()


def _rope_tables(s):
    rows = s // GRID_W
    inv = 10000.0 ** (-jnp.arange(0, 32, 2, dtype=F32) / 32.0)
    ang_r = jnp.arange(rows, dtype=F32)[:, None] * inv
    ang_c = jnp.arange(GRID_W, dtype=F32)[:, None] * inv
    lane = jnp.arange(128)
    first = (lane % 32) < 16
    by_row = (lane % 64) < 32

    def table(fn):
        tr = jnp.tile(fn(ang_r), (1, 8))[:, None, :]
        tc = jnp.tile(fn(ang_c), (1, 8))[None, :, :]
        return jnp.where(by_row, tr, tc).reshape(s, 128)

    cos, sin = table(jnp.cos), table(jnp.sin)
    return cos, jnp.where(first, -sin, 0.0), jnp.where(first, 0.0, sin)


def _local_step(x, ctx, tgt, mod, modc, nw1, win, qnw, knw, sink, cw, cb, cnw, cnb, wout4, nw2, wup, fcw, fcb, wdown4,
                core=None):
    s = x.shape[0]
    sh1, sc1, g1, sh2, sc2, g2 = [mod[:, D * k:D * (k + 1)] for k in range(6)]
    shc, scc = modc[:, :D], modc[:, D:2 * D]
    cos, sa, sb = _rope_tables(s)
    gi = jnp.arange(128) // 64
    gmat = (gi[:, None] == gi[None, :]).astype(_MX)
    qnw2, knw2 = jnp.tile(qnw, (1, 2)), jnp.tile(knw, (1, 2))
    cw32 = jnp.pad(cw, ((0, 1), (0, 0)))
    fcw2 = jnp.pad(fcw, ((0, 5), (0, 0))).reshape(8, 2, FH).transpose(1, 0, 2)
    fcb2 = fcb.reshape(2, 1, FH)
    winkv = win[:, 512:768]
    sinkv = sink.reshape(8)

    host = (lambda *bufs: ()) if core is None else (lambda *bufs: bufs)
    (proj, h, q0, q1, k, ksw, v, vsw, u0), got = _mix_in(x, nw1, sc1, sh1, win, qnw2, knw2, cos, sa, sb, gmat,
                                                        gather=host(wout4))
    wout = (got[0] if got else wout4).reshape(D, D)
    kvc, hc, kc, kcsw, vc, vcsw = _ctx_kv(ctx, nw1, scc, shc, winkv, knw2, gmat)
    padr = lambda a: jnp.pad(a, ((TQ, TQ), (0, 0)))
    kp, kswp, vp, vswp = padr(k), padr(ksw), padr(v), padr(vsw)
    (o, pex, pstat), got = _attn_fwd(q0, q1, kp, kswp, vp, vswp, kc, kcsw, vc, vcsw, sinkv, gather=host(wup))
    wup = got[0] if got else wup
    u1 = _conv31(u0, cw32, cb)
    x1, mix, cat = _mix_out(o, u1, cnw, cnb, wout, x, g1)
    (h2, up0), got = _ffn_up(x1, nw2, sc2, sh2, wup, gather=host(wdown4))
    wdown = (got[0] if got else wdown4).reshape(FH, D)
    act, gv = _ffn_conv_act(up0, fcw2, fcb2)
    dy, dact, ddn, vec_dn = _ffn_down(act, wdown, x1, g2, tgt)
    loss = (0.5 / D) * jnp.sum(vec_dn[1])

    dup, gfc = _ffn_conv_bwd(up0, gv, dact, fcw2)
    g_wdown = _tn_matmul(act, ddn, FQ, D, "gw_down").reshape(4, FH // 4, D)
    g_wup = _tn_matmul(h2, dup, D, FQ, "gw_up", b_split=True, by_chip=True)
    dx1, dmix, vec_up = _ffn_up_bwd(dup, wup, x1, dy, mix, nw2, sc2, g1)
    g_wout = _tn_matmul(cat, dmix, D, D, "gw_out").reshape(4, D // 4, D)
    late = [g_wout, g_wup, g_wdown]
    (do, du1, vec_ln), from_sib = _mix_out_bwd(dmix, wout, u1, cnw, cnb, swap=() if core is None else late)
    parts = () if core is None else _pair_sum(late, from_sib, core, "grad_pair_sum_late")
    (dga, dgb, gcw), arrived_a = _conv31_bwd(du1, u0, cw32, proj, exchange=parts[0:1] + parts[2:3])
    (dq, dkp, dvp, dkc, dvc, dsink), arrived_b = _attn_bwd(q0, q1, kp, kswp, vp, vswp, kc, kcsw, vc, vcsw, pex, pstat,
                                                           o, do, exchange=parts[1:2])
    if core is not None:
        g_wout, g_wup, g_wdown = zip(parts, [arrived_a[0], arrived_b[0], arrived_a[1]])
    gwinkv, vec_ctx = _ctx_bwd(ctx, nw1, scc, winkv, kvc, hc, dkc, dvc, knw2, gmat)
    gx, dproj, vec_in = _mix_in_bwd(dq, dkp[TQ:TQ + s], dvp[TQ:TQ + s], dga, dgb, proj, x, dx1, win, nw1, sc1,
                                    qnw2, knw2, cos, sa, sb, gmat)
    g_win = _tn_matmul(h, dproj, D, INW, "gw_in", init=gwinkv)

    g_qn = vec_in[3:4, 0:128]
    g_kn = vec_in[3:4, 128:256] + vec_ctx[3:4, 0:128]
    grads = dict(
        norm_mix_w=vec_in[2:3] + vec_ctx[2:3], w_in=g_win,
        q_norm_w=g_qn[:, :64] + g_qn[:, 64:], k_norm_w=g_kn[:, :64] + g_kn[:, 64:],
        sink_logit=dsink[:, 0].reshape(1, 8), conv_w=gcw[:CK], conv_b=gcw[CK:CK + 1],
        conv_norm_w=vec_ln[1:2], conv_norm_b=vec_ln[0:1], w_out=g_wout, norm_ffn_w=vec_up[2:3], w_up=g_wup,
        ffn_conv_w=gfc[:, 0:3].transpose(1, 0, 2).reshape(3, 2 * FH), ffn_conv_b=gfc[:, 3].reshape(1, 2 * FH),
        w_down=g_wdown)
    dmod = jnp.concatenate([vec_in[0:1], vec_in[1:2], vec_up[3:4], vec_up[0:1], vec_up[1:2], vec_dn[0:1]], axis=1)
    dmodc = jnp.concatenate([vec_ctx[0:1], vec_ctx[1:2]], axis=1)
    return loss, gx, grads, dmod, dmodc


_SMALL = ["c_ctx", "b_mod", "norm_mix_w", "q_norm_w", "k_norm_w", "sink_logit", "conv_w", "conv_b", "conv_norm_w",
          "conv_norm_b", "norm_ffn_w", "ffn_conv_w", "ffn_conv_b"]
_GATHERED = ["w_in", "w_out", "w_up", "w_down"]
_BIG = ["w_mod"] + _GATHERED
_ORDER = ["c_ctx", "w_mod", "b_mod", "norm_mix_w", "w_in", "q_norm_w", "k_norm_w", "sink_logit", "conv_w", "conv_b",
          "conv_norm_w", "conv_norm_b", "w_out", "norm_ffn_w", "w_up", "ffn_conv_w", "ffn_conv_b", "w_down"]
_PACK = [("norm_mix_w", 1), ("norm_ffn_w", 1), ("q_norm_w", 1), ("k_norm_w", 1), ("sink_logit", 1), ("conv_b", 1),
         ("conv_norm_w", 1), ("conv_norm_b", 1), ("ffn_conv_b", 6), ("conv_w", 16), ("ffn_conv_w", 17)]
_PACK_ROWS = 56


def _pack_rows(a, nrows):
    flat = a.reshape(-1)
    return jnp.pad(flat, (0, nrows * D - flat.shape[0])).reshape(nrows, D)


def kernel(x, c, ctx, c_ctx, w_mod, b_mod, norm_mix_w, w_in, q_norm_w, k_norm_w, sink_logit, conv_w, conv_b, conv_norm_w, conv_norm_b, w_out, norm_ffn_w, w_up, ffn_conv_w, ffn_conv_b, w_down, loss_target, m_c_ctx, m_w_mod, m_b_mod, m_norm_mix_w, m_w_in, m_q_norm_w, m_k_norm_w, m_sink_logit, m_conv_w, m_conv_b, m_conv_norm_w, m_conv_norm_b, m_w_out, m_norm_ffn_w, m_w_up, m_ffn_conv_w, m_ffn_conv_b, m_w_down, v_c_ctx, v_w_mod, v_b_mod, v_norm_mix_w, v_w_in, v_q_norm_w, v_k_norm_w, v_sink_logit, v_conv_w, v_conv_b, v_conv_norm_w, v_conv_norm_b, v_w_out, v_norm_ffn_w, v_w_up, v_ffn_conv_w, v_ffn_conv_b, v_w_down):
    w = dict(c_ctx=c_ctx.reshape(1, D), w_mod=w_mod[0], b_mod=b_mod, norm_mix_w=norm_mix_w, w_in=w_in[0],
             q_norm_w=q_norm_w, k_norm_w=k_norm_w, sink_logit=sink_logit, conv_w=conv_w[0], conv_b=conv_b,
             conv_norm_w=conv_norm_w, conv_norm_b=conv_norm_b, w_out=w_out[0], norm_ffn_w=norm_ffn_w, w_up=w_up[0],
             ffn_conv_w=ffn_conv_w[0], ffn_conv_b=ffn_conv_b, w_down=w_down[0])
    m = dict(c_ctx=m_c_ctx.reshape(1, D), w_mod=m_w_mod[0], b_mod=m_b_mod, norm_mix_w=m_norm_mix_w, w_in=m_w_in[0],
             q_norm_w=m_q_norm_w, k_norm_w=m_k_norm_w, sink_logit=m_sink_logit, conv_w=m_conv_w[0], conv_b=m_conv_b,
             conv_norm_w=m_conv_norm_w, conv_norm_b=m_conv_norm_b, w_out=m_w_out[0], norm_ffn_w=m_norm_ffn_w,
             w_up=m_w_up[0], ffn_conv_w=m_ffn_conv_w[0], ffn_conv_b=m_ffn_conv_b, w_down=m_w_down[0])
    v = dict(c_ctx=v_c_ctx.reshape(1, D), w_mod=v_w_mod[0], b_mod=v_b_mod, norm_mix_w=v_norm_mix_w, w_in=v_w_in[0],
             q_norm_w=v_q_norm_w, k_norm_w=v_k_norm_w, sink_logit=v_sink_logit, conv_w=v_conv_w[0], conv_b=v_conv_b,
             conv_norm_w=v_conv_norm_w, conv_norm_b=v_conv_norm_b, w_out=v_w_out[0], norm_ffn_w=v_norm_ffn_w,
             w_up=v_w_up[0], ffn_conv_w=v_ffn_conv_w[0], ffn_conv_b=v_ffn_conv_b, w_down=v_w_down[0])
    xi, yi, ci = _position()
    chip = 2 * xi + yi
    dev = 2 * chip + ci
    s = x.shape[1]
    ncol = w["w_mod"].shape[1]

    win_buf, wout_buf, wup_buf, wdown_buf = _cast_shards([w[n] for n in _GATHERED], chip)

    blk0 = jnp.concatenate([_pack_rows(w["conv_w"], 4), _pack_rows(w["ffn_conv_w"], 5), c,
                            jnp.zeros((6, D), F32)], axis=0)
    g0, _, (win4,) = _small_allgather(blk0, "gather_c_convw", gather=[win_buf])
    win = win4.transpose(1, 0, 2).reshape(D, INW)
    g0 = g0.reshape(8, 16, D)
    c_all = g0[:, 9, :]
    cs = jnp.concatenate([c_all, w["c_ctx"], jnp.zeros((7, D), F32)], axis=0)
    cw_full = jnp.concatenate([g0[2 * j, 0:4].reshape(-1)[:CK * 128].reshape(CK, 128) for j in range(4)], axis=1)
    fcw_full = jnp.concatenate([g0[2 * j, 4:9].reshape(-1)[:3 * 1408].reshape(3, 1408) for j in range(4)], axis=1)

    b_loc = lax.dynamic_slice(w["b_mod"], (0, chip * ncol), (1, ncol))
    modp = _mod_fwd(cs, w["w_mod"], b_loc)
    gm = _small_allgather(modp, "gather_mod")[0].reshape(8, 16, ncol)
    mod_all = jnp.concatenate([gm[2 * j] for j in range(4)], axis=1)
    mod = lax.dynamic_slice(mod_all, (dev, 0), (1, 6 * D))
    modc = mod_all[8:9]

    loss_loc, gx, gl, dmod, dmodc = _local_step(
        x[0], ctx[0], loss_target[0], mod, modc, w["norm_mix_w"], win, w["q_norm_w"], w["k_norm_w"], w["sink_logit"],
        cw_full, w["conv_b"], w["conv_norm_w"], w["conv_norm_b"], wout_buf, w["norm_ffn_w"], wup_buf, fcw_full,
        w["ffn_conv_b"], wdown_buf, core=ci)

    last = [gl["w_in"].reshape(D, 4, INW // 4).transpose(1, 0, 2)]
    parts = _pair_sum(last, _grad_swap(last, "grad_swap_in"), ci, "grad_pair_sum_in")

    pack = jnp.concatenate([dmod.reshape(6, D), dmodc.reshape(2, D)]
                           + [_pack_rows(gl[name], nr) for name, nr in _PACK], axis=0)
    pack = jnp.pad(pack, ((0, _PACK_ROWS - pack.shape[0]), (0, 0)))
    last_row = lax.broadcasted_iota(jnp.int32, pack.shape, 0) == _PACK_ROWS - 1
    pack = jnp.where(last_row, loss_loc, pack)
    gp, arrived, _ = _small_allgather(pack, "gather_small_grads", exchange=parts)
    gp = gp.reshape(8, _PACK_ROWS, D)
    late = [gl[n] for n in ("w_out", "w_up", "w_down")]
    shard_grads = _chip_sum([*parts, *[p for p, _ in late]], [*arrived, *[a for _, a in late]], chip, ci)
    tot = _sum_leading(gp, "sum_small_grads")
    loss = tot[_PACK_ROWS - 1, 0]
    g = {}
    r = 8
    for name, nr in _PACK:
        shape = gl[name].shape
        g[name] = tot[r:r + nr].reshape(-1)[:math.prod(shape)].reshape(shape)
        r += nr
    dmod_all = jnp.concatenate([gp[:, 0:6].reshape(8, 6 * D),
                                jnp.pad(tot[6:8].reshape(1, 2 * D), ((0, 0), (0, 4 * D))),
                                jnp.zeros((7, 6 * D), F32)], axis=0)
    g["b_mod"] = _sum_leading(dmod_all.reshape(16, 1, 6 * D), "grad_b_mod")
    g["w_mod"], part = _mod_bwd(cs, lax.dynamic_slice(dmod_all, (0, chip * ncol), (16, ncol)), w["w_mod"])
    cparts, _, (g["w_in"], g["w_out"], g["w_up"], g["w_down"]) = _small_allgather(part, "gather_cctx",
                                                                                 share=shard_grads)
    g["c_ctx"] = _cctx_grad(cparts.reshape(8, 16, D)[0::2], w["c_ctx"])
    g["conv_w"] = lax.dynamic_slice(g["conv_w"], (0, chip * 128), (CK, 128))
    g["ffn_conv_w"] = lax.dynamic_slice(g["ffn_conv_w"], (0, chip * 1408), (3, 1408))

    delta, new_m, new_v = {}, {}, {}
    for name in _BIG:
        delta[name], new_m[name], new_v[name] = _adam_big(w[name], g[name], m[name], v[name], "adam_" + name)
    ds, ms, vs = _adam_small([w[n] for n in _SMALL], [g[n] for n in _SMALL], [m[n] for n in _SMALL],
                             [v[n] for n in _SMALL])
    for k, name in enumerate(_SMALL):
        delta[name], new_m[name], new_v[name] = ds[k], ms[k], vs[k]

    def shaped(d, name):
        a = d[name]
        if name == "c_ctx":
            return a.reshape(D)
        if name in ("w_mod", "w_in", "w_out", "w_up", "w_down", "conv_w", "ffn_conv_w"):
            return a[None]
        return a

    outs = [loss, gx[None]]
    for d in (g, delta, new_m, new_v):
        outs += [shaped(d, name) for name in _ORDER]
    return tuple(outs)
```

```python
import functools
import math

import jax
import jax.numpy as jnp
from jax import lax
from jax.experimental import pallas as pl
from jax.experimental.pallas import tpu as pltpu

F32 = jnp.float32
_MX = jnp.bfloat16
EPS = 1e-6
NEG = -1e30
D = 1024
AW = 512
CW = 512
INW = 1792
FH = 2816
LC = 256
CK = 31
GRID_W = 64
TM = 512
TQ = 128
NCH = FH // 128
FQ = 2 * FH // 4
VMEM_LIMIT = 56 * 1024 * 1024
MESH = pl.DeviceIdType.MESH

ADAM_LR, ADAM_B1, ADAM_B2, ADAM_EPS, ADAM_WD, ADAM_STEP = 0.001, 0.9, 0.999, 1e-08, 0.01, 10


def _cp(*sem):
    return pltpu.CompilerParams(dimension_semantics=sem, vmem_limit_bytes=VMEM_LIMIT)


def _vspec():
    return pl.BlockSpec(memory_space=pltpu.VMEM)


def _sig(z):
    return 1.0 / (1.0 + jnp.exp(-z))


def _dot(a, b):
    return jnp.dot(a, b, preferred_element_type=F32)


def _dot_nt(a, b):
    return lax.dot_general(a, b, (((1,), (1,)), ((), ())), preferred_element_type=F32)


def _dot_tn(a, b):
    return lax.dot_general(a, b, (((0,), (0,)), ((), ())), preferred_element_type=F32)


def _gsum(v, g):
    hi = v.astype(_MX)
    lo = (v - hi.astype(F32)).astype(_MX)
    return _dot(hi, g) + _dot(lo, g)


def _headnorm(ch, w, g):
    r = lax.rsqrt(_gsum(ch * ch, g) * (1.0 / 64.0) + EPS)
    return ch * r * w


def _headnorm_bwd(ch, dy, w, g):
    r = lax.rsqrt(_gsum(ch * ch, g) * (1.0 / 64.0) + EPS)
    hat = ch * r
    dhat = dy * w
    dch = r * (dhat - hat * (_gsum(dhat * hat, g) * (1.0 / 64.0)))
    return dch, dy * hat


def _rope(u, cos, sa, sb):
    return u * cos + pltpu.roll(u, 112, 1) * sa + pltpu.roll(u, 16, 1) * sb


def _rope_bwd(d, cos, sa, sb):
    return d * cos - pltpu.roll(d, 112, 1) * sa - pltpu.roll(d, 16, 1) * sb


def _rowsum(v):
    return jnp.sum(v, axis=0, keepdims=True)


def _call_hosting_gather(body, gather, step_of, nsteps, *, in_specs, out_specs, out_shape, scratch_shapes=(),
                         comm=None, **kw):
    comm = comm or _Gather
    ng, n_in, n_out = len(gather), len(in_specs), len(out_specs)
    if not ng:
        return pl.pallas_call(body, in_specs=in_specs, out_specs=out_specs, out_shape=out_shape,
                              scratch_shapes=list(scratch_shapes), **kw)

    def hosted(*refs):
        ins, outs = refs[:n_in], refs[n_in + ng:n_in + ng + n_out]
        rest = refs[n_in + 2 * ng + n_out:]
        copies = comm(refs[n_in:n_in + ng], refs[n_in + ng + n_out:n_in + 2 * ng + n_out], rest[0], rest[1],
                      [b.shape[1] for b in gather])
        step = step_of()
        pl.when(step == 0)(copies.start)
        body(*ins, *outs, *rest[2:])
        pl.when(step == max(nsteps - 3, 0))(copies.forward)
        pl.when(step == nsteps - 1)(copies.finish)

    return pl.pallas_call(
        hosted, in_specs=list(in_specs) + _any_specs(ng), out_specs=list(out_specs) + _any_specs(ng),
        out_shape=list(out_shape) + comm.results(gather),
        input_output_aliases={n_in + w: n_out + w for w in range(ng)} if comm.in_place else {},
        scratch_shapes=comm.semaphores(ng) + list(scratch_shapes), **kw)


def _mix_in(x, nw, sc, sh, win, qnw2, knw2, cos, sa, sb, gmat, gather=()):
    s = x.shape[0]

    def body(x_ref, nw_ref, sc_ref, sh_ref, win_ref, qnw_ref, knw_ref, cos_ref, sa_ref, sb_ref, g_ref,
             proj_ref, h_ref, q0_ref, q1_ref, k_ref, ksw_ref, v_ref, vsw_ref, u0_ref):
        xv = x_ref[...]
        r = lax.rsqrt(jnp.mean(xv * xv, axis=-1, keepdims=True) + EPS)
        h = xv * r * (nw_ref[...] * (1.0 + sc_ref[...])) + sh_ref[...]
        hb = h.astype(_MX)
        h_ref[...] = hb
        proj = _dot(hb, win_ref[...])
        proj_ref[...] = proj
        cs, sav, sbv, g = cos_ref[...], sa_ref[...], sb_ref[...], g_ref[...]
        lo = lax.broadcasted_iota(jnp.int32, (1, 128), 1) < 64
        for j in range(4):
            q = _rope(_headnorm(proj[:, 128 * j:128 * (j + 1)], qnw_ref[...], g), cs, sav, sbv) * 0.125
            q0_ref[:, 128 * j:128 * (j + 1)] = jnp.where(lo, q, 0.0).astype(_MX)
            q1_ref[:, 128 * j:128 * (j + 1)] = jnp.where(lo, 0.0, q).astype(_MX)
        k = _rope(_headnorm(proj[:, 512:640], knw_ref[...], g), cs, sav, sbv)
        k_ref[...] = k.astype(_MX)
        ksw_ref[...] = pltpu.roll(k, 64, 1).astype(_MX)
        v = proj[:, 640:768]
        v_ref[...] = v.astype(_MX)
        vsw_ref[...] = pltpu.roll(v, 64, 1).astype(_MX)
        u0_ref[...] = proj[:, 768:1280] * _sig(proj[:, 1280:1792])

    row = lambda n: pl.BlockSpec((1, n), lambda i: (0, 0))
    tile = lambda n: pl.BlockSpec((TM, n), lambda i: (i, 0))
    res = _call_hosting_gather(
        body, gather, lambda: pl.program_id(0), s // TM, name="mix_in", grid=(s // TM,),
        in_specs=[tile(D), row(D), row(D), row(D), pl.BlockSpec((D, INW), lambda i: (0, 0)), row(128), row(128),
                  tile(128), tile(128), tile(128), pl.BlockSpec((128, 128), lambda i: (0, 0))],
        out_specs=[tile(INW), tile(D), tile(AW), tile(AW), tile(128), tile(128), tile(128), tile(128), tile(CW)],
        out_shape=[jax.ShapeDtypeStruct((s, INW), F32), jax.ShapeDtypeStruct((s, D), _MX),
                   jax.ShapeDtypeStruct((s, AW), _MX), jax.ShapeDtypeStruct((s, AW), _MX),
                   jax.ShapeDtypeStruct((s, 128), _MX), jax.ShapeDtypeStruct((s, 128), _MX),
                   jax.ShapeDtypeStruct((s, 128), _MX), jax.ShapeDtypeStruct((s, 128), _MX),
                   jax.ShapeDtypeStruct((s, CW), F32)],
        compiler_params=_cp("arbitrary"),
    )(x, nw, sc, sh, win, qnw2, knw2, cos, sa, sb, gmat, *gather)
    return res[:9], res[9:]


def _ctx_kv(ctx, nw, scc, shc, winkv, knw2, gmat):
    def body(ctx_ref, nw_ref, sc_ref, sh_ref, w_ref, knw_ref, g_ref,
             kvc_ref, hc_ref, kc_ref, kcsw_ref, vc_ref, vcsw_ref):
        cv = ctx_ref[...]
        r = lax.rsqrt(jnp.mean(cv * cv, axis=-1, keepdims=True) + EPS)
        hc = (cv * r * (nw_ref[...] * (1.0 + sc_ref[...])) + sh_ref[...]).astype(_MX)
        hc_ref[...] = hc
        kvc = _dot(hc, w_ref[...])
        kvc_ref[...] = kvc
        kc = _headnorm(kvc[:, :128], knw_ref[...], g_ref[...])
        kc_ref[...] = kc.astype(_MX)
        kcsw_ref[...] = pltpu.roll(kc, 64, 1).astype(_MX)
        vc = kvc[:, 128:]
        vc_ref[...] = vc.astype(_MX)
        vcsw_ref[...] = pltpu.roll(vc, 64, 1).astype(_MX)

    return pl.pallas_call(
        body, name="ctx_kv",
        in_specs=[_vspec()] * 7, out_specs=[_vspec()] * 6,
        out_shape=[jax.ShapeDtypeStruct((LC, 256), F32), jax.ShapeDtypeStruct((LC, D), _MX)]
        + [jax.ShapeDtypeStruct((LC, 128), _MX)] * 4,
        compiler_params=pltpu.CompilerParams(vmem_limit_bytes=VMEM_LIMIT),
    )(ctx, nw, scc, shc, winkv, knw2, gmat)


def _attn_mask(i, s):
    r = lax.broadcasted_iota(jnp.int32, (2 * TQ, LC + 3 * TQ), 0) % TQ
    cidx = lax.broadcasted_iota(jnp.int32, (2 * TQ, LC + 3 * TQ), 1)
    qpos = i * TQ + r
    kpos = (i - 1) * TQ + (cidx - LC)
    near = (jnp.abs(qpos - kpos) <= 128) & (kpos >= 0) & (kpos < s)
    return (cidx < LC) | near


def _attn_bias(i, s):
    return jnp.where(_attn_mask(i, s), 0.0, NEG)


def _attn_exp(sc, bias, sinkv):
    sc = sc + bias
    m = jnp.maximum(jnp.max(sc, axis=-1, keepdims=True), sinkv)
    ex = jnp.exp(sc - m)
    es = jnp.exp(sinkv - m)
    return ex, es, 1.0 / (jnp.sum(ex, axis=-1, keepdims=True) + es)


def _sink_rows(sink_ref, g, e):
    return jnp.concatenate([jnp.full((TQ, 1), sink_ref[4 * g + e], F32),
                            jnp.full((TQ, 1), sink_ref[4 * g + 2 + e], F32)], axis=0)


def _attn_fwd(q0, q1, kp, kswp, vp, vswp, kc, kcsw, vc, vcsw, sink, gather=()):
    s = q0.shape[0]
    nb = s // TQ

    def body(q0_ref, q1_ref, kp_ref, kswp_ref, vp_ref, vswp_ref, kc_ref, kcsw_ref, vc_ref, vcsw_ref, sink_ref,
             o_ref, ex_ref, stat_ref):
        i = pl.program_id(0)
        st = pl.multiple_of(i * TQ, TQ)
        kall = (jnp.concatenate([kc_ref[...], kp_ref[pl.ds(st, 3 * TQ), :]], axis=0),
                jnp.concatenate([kcsw_ref[...], kswp_ref[pl.ds(st, 3 * TQ), :]], axis=0))
        vall = (jnp.concatenate([vc_ref[...], vp_ref[pl.ds(st, 3 * TQ), :]], axis=0),
                jnp.concatenate([vcsw_ref[...], vswp_ref[pl.ds(st, 3 * TQ), :]], axis=0))
        bias = _attn_bias(i, s)
        lo = lax.broadcasted_iota(jnp.int32, (1, 128), 1) < 64
        qrefs = (q0_ref, q1_ref)
        out = [jnp.zeros((TQ, 128), F32) for _ in range(4)]
        combos = [(g, e) for g in range(2) for e in range(2)]
        scores = [_dot_nt(jnp.concatenate([qrefs[e][:, 256 * g:256 * g + 128],
                                           qrefs[e][:, 256 * g + 128:256 * g + 256]], axis=0),
                          kall[0 if e == g else 1]) for g, e in combos]
        parts = []
        for n, ((g, e), sc) in enumerate(zip(combos, scores)):
            ex, es, inv = _attn_exp(sc, bias, _sink_rows(sink_ref, g, e))
            parts.append((ex.astype(_MX), inv))
            ex_ref[n] = parts[-1][0]
            stat_ref[n] = jnp.where(lo, inv, es * inv)
        for (g, e), (ex, inv) in zip(combos, parts):
            me = lo if e == 0 else jnp.logical_not(lo)
            o2 = _dot(ex, vall[0 if e == g else 1]) * inv
            out[2 * g] = out[2 * g] + jnp.where(me, o2[:TQ], 0.0)
            out[2 * g + 1] = out[2 * g + 1] + jnp.where(me, o2[TQ:], 0.0)
        for j in range(4):
            o_ref[:, 128 * j:128 * (j + 1)] = out[j]

    full = lambda a: pl.BlockSpec(a.shape, lambda i: (0, 0))
    qs = pl.BlockSpec((TQ, AW), lambda i: (i, 0))
    res = _call_hosting_gather(
        body, gather, lambda: pl.program_id(0), nb, name="attn_fwd", grid=(nb,),
        in_specs=[qs, qs, full(kp), full(kp), full(kp), full(kp), full(kc), full(kc), full(kc), full(kc),
                  pl.BlockSpec(memory_space=pltpu.SMEM)],
        out_specs=[qs, pl.BlockSpec((None, 4, 2 * TQ, LC + 3 * TQ), lambda i: (i, 0, 0, 0)),
                   pl.BlockSpec((None, 4, 2 * TQ, 128), lambda i: (i, 0, 0, 0))],
        out_shape=[jax.ShapeDtypeStruct((s, AW), F32), jax.ShapeDtypeStruct((nb, 4, 2 * TQ, LC + 3 * TQ), _MX),
                   jax.ShapeDtypeStruct((nb, 4, 2 * TQ, 128), F32)],
        compiler_params=_cp("arbitrary"),
    )(q0, q1, kp, kswp, vp, vswp, kc, kcsw, vc, vcsw, sink, *gather)
    return res[:3], res[3:]


def _conv31(u0, cw32, cb):
    s = u0.shape[0]
    rch = 256

    def body(u_ref, w_ref, b_ref, o_ref, pad_ref):
        pad_ref[0:16, :] = jnp.zeros((16, 128), F32)
        pad_ref[s + 16:s + 32, :] = jnp.zeros((16, 128), F32)
        pad_ref[16:s + 16, :] = u_ref[...]
        for cidx in range(s // rch):
            base = cidx * rch
            acc = jnp.zeros((rch, 128), F32) + b_ref[...]
            for j in range(CK):
                acc = acc + w_ref[j:j + 1, :] * pad_ref[base + j + 1:base + j + 1 + rch, :]
            o_ref[base:base + rch, :] = acc

    return pl.pallas_call(
        body, name="conv31", grid=(CW // 128,),
        in_specs=[pl.BlockSpec((s, 128), lambda c: (0, c)), pl.BlockSpec((32, 128), lambda c: (0, c)),
                  pl.BlockSpec((1, 128), lambda c: (0, c))],
        out_specs=pl.BlockSpec((s, 128), lambda c: (0, c)),
        out_shape=jax.ShapeDtypeStruct((s, CW), F32),
        scratch_shapes=[pltpu.VMEM((s + 32, 128), F32)],
        compiler_params=_cp("arbitrary"),
    )(u0, cw32, cb)


def _ln_stats(u1):
    mu = jnp.mean(u1, axis=-1, keepdims=True)
    xc = u1 - mu
    rstd = lax.rsqrt(jnp.mean(xc * xc, axis=-1, keepdims=True) + EPS)
    return xc * rstd, rstd


def _mix_out(o, u1, cnw, cnb, wout, x, g1):
    s = x.shape[0]

    def body(o_ref, u1_ref, cnw_ref, cnb_ref, w_ref, x_ref, g1_ref, x1_ref, mix_ref, cat_ref):
        u2n, _ = _ln_stats(u1_ref[...])
        u2 = u2n * cnw_ref[...] + cnb_ref[...]
        u3 = u2 * _sig(u2)
        cat = jnp.concatenate([o_ref[...], u3], axis=1).astype(_MX)
        cat_ref[...] = cat
        mix = _dot(cat, w_ref[...])
        mix_ref[...] = mix
        x1_ref[...] = x_ref[...] + g1_ref[...] * mix

    row = lambda n: pl.BlockSpec((1, n), lambda i: (0, 0))
    tile = lambda n: pl.BlockSpec((TM, n), lambda i: (i, 0))
    return pl.pallas_call(
        body, name="mix_out", grid=(s // TM,),
        in_specs=[tile(AW), tile(CW), row(CW), row(CW), pl.BlockSpec((D, D), lambda i: (0, 0)), tile(D), row(D)],
        out_specs=[tile(D), tile(D), tile(D)],
        out_shape=[jax.ShapeDtypeStruct((s, D), F32), jax.ShapeDtypeStruct((s, D), F32),
                   jax.ShapeDtypeStruct((s, D), _MX)],
        compiler_params=_cp("arbitrary"),
    )(o, u1, cnw, cnb, wout, x, g1)


def _ffn_up(x1, nw, sc, sh, wup4, gather=()):
    s = x1.shape[0]
    tm = min(1024, s)

    def body(x_ref, nw_ref, sc_ref, sh_ref, w_ref, h2_ref, up_ref, h2s):
        @pl.when(pl.program_id(1) == 0)
        def _():
            xv = x_ref[...]
            r = lax.rsqrt(jnp.mean(xv * xv, axis=-1, keepdims=True) + EPS)
            h2 = (xv * r * (nw_ref[...] * (1.0 + sc_ref[...])) + sh_ref[...]).astype(_MX)
            h2s[...] = h2
            h2_ref[...] = h2
        up_ref[...] = _dot(h2s[...], w_ref[...]).astype(_MX)

    row = pl.BlockSpec((1, D), lambda i, j: (0, 0))
    res = _call_hosting_gather(
        body, gather, lambda: pl.program_id(0) * 4 + pl.program_id(1), 4 * (s // tm), name="ffn_up",
        grid=(s // tm, 4),
        in_specs=[pl.BlockSpec((tm, D), lambda i, j: (i, 0)), row, row, row,
                  pl.BlockSpec((None, D, FQ), lambda i, j: (j, 0, 0))],
        out_specs=[pl.BlockSpec((tm, D), lambda i, j: (i, 0)),
                   pl.BlockSpec((None, tm, FQ), lambda i, j: (j // 2, i, j % 2))],
        out_shape=[jax.ShapeDtypeStruct((s, D), _MX), jax.ShapeDtypeStruct((2, s, FH), _MX)],
        scratch_shapes=[pltpu.VMEM((tm, D), _MX)],
        compiler_params=_cp("arbitrary", "arbitrary"),
    )(x1, nw, sc, sh, wup4, *gather)
    return res[:2], res[2:]


def _fill_pad8(pad_ref, val, s):
    pad_ref[0:8, :] = jnp.zeros((8, 128), F32)
    pad_ref[s + 8:s + 16, :] = jnp.zeros((8, 128), F32)
    pad_ref[8:s + 8, :] = val


def _conv3_at(pad_ref, w_ref, half, base, rch):
    return (w_ref[half, 0:1, :] * pad_ref[base + 7:base + 7 + rch, :]
            + w_ref[half, 1:2, :] * pad_ref[base + 8:base + 8 + rch, :]
            + w_ref[half, 2:3, :] * pad_ref[base + 9:base + 9 + rch, :])


def _ffn_conv_act(up0, fcw, fcb):
    s = up0.shape[1]
    rch = 256

    def body(up_ref, w_ref, b_ref, act_ref, gv_ref, padg, padv):
        _fill_pad8(padg, up_ref[0].astype(F32), s)
        _fill_pad8(padv, up_ref[1].astype(F32), s)
        for cidx in range(s // rch):
            base = cidx * rch
            gate = _conv3_at(padg, w_ref, 0, base, rch) + b_ref[0]
            val = _conv3_at(padv, w_ref, 1, base, rch) + b_ref[1]
            act_ref[base:base + rch, :] = (gate * _sig(gate) * val).astype(_MX)
            gv_ref[0, base:base + rch, :] = gate.astype(_MX)
            gv_ref[1, base:base + rch, :] = val.astype(_MX)

    return pl.pallas_call(
        body, name="ffn_conv_act", grid=(NCH,),
        in_specs=[pl.BlockSpec((2, s, 128), lambda c: (0, 0, c)), pl.BlockSpec((2, 8, 128), lambda c: (0, 0, c)),
                  pl.BlockSpec((2, 1, 128), lambda c: (0, 0, c))],
        out_specs=[pl.BlockSpec((s, 128), lambda c: (0, c)), pl.BlockSpec((2, s, 128), lambda c: (0, 0, c))],
        out_shape=[jax.ShapeDtypeStruct((s, FH), _MX), jax.ShapeDtypeStruct((2, s, FH), _MX)],
        scratch_shapes=[pltpu.VMEM((s + 16, 128), F32)] * 2,
        compiler_params=_cp("arbitrary"),
    )(up0, fcw, fcb)


def _ffn_down(act, wdown, x1, g2, tgt):
    s = x1.shape[0]

    def body(act_ref, w_ref, x1_ref, g2_ref, tgt_ref, dy_ref, dact_ref, ddn_ref, vec_ref):
        @pl.when(pl.program_id(0) == 0)
        def _():
            vec_ref[...] = jnp.zeros((8, D), F32)
        dn = _dot(act_ref[...], w_ref[...])
        diff = x1_ref[...] + g2_ref[...] * dn - tgt_ref[...]
        dy = diff * (1.0 / D)
        dy_ref[...] = dy
        ddn = (dy * g2_ref[...]).astype(_MX)
        ddn_ref[...] = ddn
        dact_ref[...] = _dot_nt(ddn, w_ref[...]).astype(_MX)
        vec_ref[0:1, :] += _rowsum(dy * dn)
        vec_ref[1:2, :] += _rowsum(diff * diff)

    tile = lambda n: pl.BlockSpec((TM, n), lambda i: (i, 0))
    return pl.pallas_call(
        body, name="ffn_down", grid=(s // TM,),
        in_specs=[tile(FH), pl.BlockSpec((FH, D), lambda i: (0, 0)), tile(D), pl.BlockSpec((1, D), lambda i: (0, 0)),
                  tile(D)],
        out_specs=[tile(D), tile(FH), tile(D), pl.BlockSpec((8, D), lambda i: (0, 0))],
        out_shape=[jax.ShapeDtypeStruct((s, D), F32), jax.ShapeDtypeStruct((s, FH), _MX),
                   jax.ShapeDtypeStruct((s, D), _MX), jax.ShapeDtypeStruct((8, D), F32)],
        compiler_params=_cp("arbitrary"),
    )(act, wdown, x1, g2, tgt)


def _ffn_conv_bwd(up0, gv, dact, fcw):
    s = up0.shape[1]
    rch = 256

    def body(up_ref, gv_ref, da_ref, w_ref, dup_ref, gw_ref, dpg, dpv):
        for p in (dpg, dpv):
            p[0:8, :] = jnp.zeros((8, 128), F32)
            p[s + 8:s + 16, :] = jnp.zeros((8, 128), F32)
        for cidx in range(s // rch):
            rows = slice(cidx * rch, (cidx + 1) * rch)
            gate = gv_ref[0, rows, :].astype(F32)
            da = da_ref[rows, :].astype(F32)
            sg = _sig(gate)
            silu = gate * sg
            dpg[8 + cidx * rch:8 + (cidx + 1) * rch, :] = da * gv_ref[1, rows, :].astype(F32) * (sg + silu * (1.0 - sg))
            dpv[8 + cidx * rch:8 + (cidx + 1) * rch, :] = da * silu
        acc = [[jnp.zeros((1, 128), F32) for _ in range(4)] for _ in range(2)]
        for cidx in range(s // rch):
            base = cidx * rch
            for half, dp in enumerate((dpg, dpv)):
                shifted = [dp[base + 9 - j:base + 9 - j + rch, :] for j in range(3)]
                dup_ref[half, base:base + rch, :] = (w_ref[half, 0:1, :] * shifted[0] + w_ref[half, 1:2, :] * shifted[1]
                                                     + w_ref[half, 2:3, :] * shifted[2]).astype(_MX)
                xv = up_ref[half, base:base + rch, :].astype(F32)
                for j in range(3):
                    acc[half][j] = acc[half][j] + _rowsum(xv * shifted[j])
                acc[half][3] = acc[half][3] + _rowsum(shifted[1])
        for half in range(2):
            gw_ref[half] = jnp.zeros((8, 128), F32)
            for j in range(4):
                gw_ref[half, j:j + 1, :] = acc[half][j]

    both = pl.BlockSpec((2, s, 128), lambda c: (0, 0, c))
    return pl.pallas_call(
        body, name="ffn_conv_bwd", grid=(NCH,),
        in_specs=[both, both, pl.BlockSpec((s, 128), lambda c: (0, c)), pl.BlockSpec((2, 8, 128), lambda c: (0, 0, c))],
        out_specs=[both, pl.BlockSpec((2, 8, 128), lambda c: (0, 0, c))],
        out_shape=[jax.ShapeDtypeStruct((2, s, FH), _MX), jax.ShapeDtypeStruct((2, 8, FH), F32)],
        scratch_shapes=[pltpu.VMEM((s + 16, 128), F32)] * 2,
        compiler_params=_cp("arbitrary"),
    )(up0, gv, dact, fcw)


def _tn_matmul(a, b, tm, tn, name, b_split=False, by_chip=False, init=None):
    s, m = a.shape
    n = 2 * b.shape[2] if b_split else b.shape[1]
    ts = min(1024, s)
    nsteps = s // ts
    npb = (n // 2) // tn if b_split else None

    def body(*refs):
        if init is None:
            a_ref, b_ref, o_ref, acc = refs
        else:
            a_ref, b_ref, i_ref, o_ref, acc = refs
        k = pl.program_id(2)

        @pl.when(k == 0)
        def _():
            acc[...] = jnp.zeros((tm, tn), F32)
            if init is not None:
                acc[:, 512:768] = i_ref[...]
        acc[...] += _dot_tn(a_ref[...], b_ref[...])

        @pl.when(k == nsteps - 1)
        def _():
            o_ref[...] = acc[...].astype(_MX)

    if b_split:
        bspec = pl.BlockSpec((None, ts, tn), lambda i, j, k: (j // npb, k, j % npb))
    else:
        bspec = pl.BlockSpec((ts, tn), lambda i, j, k: (k, j))
    in_specs = [pl.BlockSpec((ts, tm), lambda i, j, k: (k, i)), bspec]
    args = [a, b]
    if init is not None:
        in_specs.append(pl.BlockSpec((tm, 256), lambda i, j, k: (i, 0)))
        args.append(init)
    if by_chip:
        out_spec = pl.BlockSpec((None, tm, tn), lambda i, j, k: (j, i, 0))
        out_shape = jax.ShapeDtypeStruct((n // tn, m, tn), _MX)
    else:
        out_spec = pl.BlockSpec((tm, tn), lambda i, j, k: (i, j))
        out_shape = jax.ShapeDtypeStruct((m, n), _MX)
    return pl.pallas_call(
        body, name=name, grid=(m // tm, n // tn, nsteps),
        in_specs=in_specs, out_specs=out_spec, out_shape=out_shape,
        scratch_shapes=[pltpu.VMEM((tm, tn), F32)],
        compiler_params=_cp("arbitrary", "arbitrary", "arbitrary"),
    )(*args)


def _ffn_up_bwd(dup, wup4, x1, dy, mix, nw, sc, g1):
    s = x1.shape[0]
    nk = 2

    def body(dup_ref, w_ref, x1_ref, dy_ref, mix_ref, nw_ref, sc_ref, g1_ref, dx1_ref, dmix_ref, vec_ref, acc):
        i, k = pl.program_id(0), pl.program_id(1)

        @pl.when(k == 0)
        def _():
            acc[...] = jnp.zeros((TM, D), F32)

        @pl.when((k == 0) & (i == 0))
        def _():
            vec_ref[...] = jnp.zeros((8, D), F32)
        acc[...] += _dot_nt(dup_ref[:, :FQ], w_ref[0]) + _dot_nt(dup_ref[:, FQ:], w_ref[1])

        @pl.when(k == nk - 1)
        def _():
            dh = acc[...]
            xv = x1_ref[...]
            r = lax.rsqrt(jnp.mean(xv * xv, axis=-1, keepdims=True) + EPS)
            xn = xv * r
            nwv, scv = nw_ref[...], sc_ref[...]
            vec_ref[0:1, :] += _rowsum(dh)
            vec_ref[1:2, :] += _rowsum(dh * xn) * nwv
            vec_ref[2:3, :] += _rowsum(dh * xn) * (1.0 + scv)
            dxn = dh * (nwv * (1.0 + scv))
            dx1 = dy_ref[...] + r * (dxn - xn * jnp.mean(dxn * xn, axis=-1, keepdims=True))
            dx1_ref[...] = dx1
            vec_ref[3:4, :] += _rowsum(dx1 * mix_ref[...])
            dmix_ref[...] = (dx1 * g1_ref[...]).astype(_MX)

    tile = pl.BlockSpec((TM, D), lambda i, k: (i, 0))
    row = pl.BlockSpec((1, D), lambda i, k: (0, 0))
    return pl.pallas_call(
        body, name="ffn_up_bwd", grid=(s // TM, nk),
        in_specs=[pl.BlockSpec((None, TM, FH), lambda i, k: (k, i, 0)),
                  pl.BlockSpec((2, D, FQ), lambda i, k: (k, 0, 0)), tile, tile, tile, row, row, row],
        out_specs=[tile, tile, pl.BlockSpec((8, D), lambda i, k: (0, 0))],
        out_shape=[jax.ShapeDtypeStruct((s, D), F32), jax.ShapeDtypeStruct((s, D), _MX),
                   jax.ShapeDtypeStruct((8, D), F32)],
        scratch_shapes=[pltpu.VMEM((TM, D), F32)],
        compiler_params=_cp("arbitrary", "arbitrary"),
    )(dup, wup4, x1, dy, mix, nw, sc, g1)


def _mix_out_bwd(dmix, wout, u1, cnw, cnb, swap=()):
    s = u1.shape[0]

    def body(dm_ref, w_ref, u1_ref, cnw_ref, cnb_ref, do_ref, du1_ref, vec_ref):
        @pl.when(pl.program_id(0) == 0)
        def _():
            vec_ref[...] = jnp.zeros((8, CW), F32)
        dcat = _dot_nt(dm_ref[...], w_ref[...])
        do_ref[...] = dcat[:, :AW]
        du3 = dcat[:, AW:]
        u2n, rstd = _ln_stats(u1_ref[...])
        u2 = u2n * cnw_ref[...] + cnb_ref[...]
        sg = _sig(u2)
        du2 = du3 * sg * (1.0 + u2 * (1.0 - sg))
        vec_ref[0:1, :] += _rowsum(du2)
        vec_ref[1:2, :] += _rowsum(du2 * u2n)
        d2n = du2 * cnw_ref[...]
        du1_ref[...] = rstd * (d2n - jnp.mean(d2n, axis=-1, keepdims=True)
                               - u2n * jnp.mean(d2n * u2n, axis=-1, keepdims=True))

    row = lambda n: pl.BlockSpec((1, n), lambda i: (0, 0))
    tile = lambda n: pl.BlockSpec((TM, n), lambda i: (i, 0))
    res = _call_hosting_gather(
        body, swap, lambda: pl.program_id(0), s // TM, comm=_Swap, name="mix_out_bwd", grid=(s // TM,),
        in_specs=[tile(D), pl.BlockSpec((D, D), lambda i: (0, 0)), tile(CW), row(CW), row(CW)],
        out_specs=[tile(AW), tile(CW), pl.BlockSpec((8, CW), lambda i: (0, 0))],
        out_shape=[jax.ShapeDtypeStruct((s, AW), F32), jax.ShapeDtypeStruct((s, CW), F32),
                   jax.ShapeDtypeStruct((8, CW), F32)],
        compiler_params=_cp("arbitrary"),
    )(dmix, wout, u1, cnw, cnb, *swap)
    return res[:3], res[3:]


def _conv31_bwd(du1, u0, cw32, proj, exchange=()):
    s = u0.shape[0]
    rch = 256

    def body(d_ref, u_ref, w_ref, ga_ref, gb_ref, dga_ref, dgb_ref, gw_ref, padu, padd):
        for p, src in ((padu, u_ref), (padd, d_ref)):
            p[0:16, :] = jnp.zeros((16, 128), F32)
            p[s + 16:s + 32, :] = jnp.zeros((16, 128), F32)
            p[16:s + 16, :] = src[...]
        for j in range(CK):
            acc = jnp.zeros((1, 128), F32)
            for cidx in range(s // rch):
                base = cidx * rch
                acc = acc + _rowsum(d_ref[base:base + rch, :] * padu[base + j + 1:base + j + 1 + rch, :])
            gw_ref[j:j + 1, :] = acc
        gw_ref[CK:CK + 1, :] = _rowsum(d_ref[...])
        for cidx in range(s // rch):
            base = cidx * rch
            du0 = jnp.zeros((rch, 128), F32)
            for j in range(CK):
                du0 = du0 + w_ref[j:j + 1, :] * padd[base + 31 - j:base + 31 - j + rch, :]
            sg = _sig(gb_ref[base:base + rch, :])
            ga = ga_ref[base:base + rch, :]
            dga_ref[base:base + rch, :] = (du0 * sg).astype(_MX)
            dgb_ref[base:base + rch, :] = (du0 * ga * sg * (1.0 - sg)).astype(_MX)

    blk = lambda off: pl.BlockSpec((s, 128), lambda c: (0, c + off))
    res = _call_hosting_gather(
        body, exchange, lambda: pl.program_id(0), CW // 128, comm=_Exchange, name="conv31_bwd", grid=(CW // 128,),
        in_specs=[blk(0), blk(0), pl.BlockSpec((32, 128), lambda c: (0, c)), blk(6), blk(10)],
        out_specs=[blk(0), blk(0), pl.BlockSpec((32, 128), lambda c: (0, c))],
        out_shape=[jax.ShapeDtypeStruct((s, CW), _MX), jax.ShapeDtypeStruct((s, CW), _MX),
                   jax.ShapeDtypeStruct((32, CW), F32)],
        scratch_shapes=[pltpu.VMEM((s + 32, 128), F32)] * 2,
        compiler_params=_cp("arbitrary"),
    )(du1, u0, cw32, proj, proj, *exchange)
    return res[:3], res[3:]


def _attn_bwd(q0, q1, kp, kswp, vp, vswp, kc, kcsw, vc, vcsw, ex, stat, o, do, exchange=()):
    s = q0.shape[0]
    nb = s // TQ
    ne = len(exchange)

    def body(*refs):
        (q0_ref, q1_ref, kp_ref, kswp_ref, vp_ref, vswp_ref, kc_ref, kcsw_ref, vc_ref, vcsw_ref, ex_ref, stat_ref,
         o_ref, do_ref) = refs[:14]
        dq_ref, dk_ref, dv_ref, dkc_ref, dvc_ref, dsink_ref = refs[14 + ne:20 + ne]
        i = pl.program_id(0)
        if ne:
            exch = _Exchange(refs[14:14 + ne], refs[20 + ne:20 + 2 * ne], refs[20 + 2 * ne], refs[21 + 2 * ne])
            pl.when(i == 0)(exch.start)

        @pl.when(i == 0)
        def _():
            dk_ref[...] = jnp.zeros((s + 2 * TQ, 128), F32)
            dv_ref[...] = jnp.zeros((s + 2 * TQ, 128), F32)
            dkc_ref[...] = jnp.zeros((LC, 128), F32)
            dvc_ref[...] = jnp.zeros((LC, 128), F32)
            dsink_ref[...] = jnp.zeros((8, 128), F32)
        st = pl.multiple_of(i * TQ, TQ)
        kall = (jnp.concatenate([kc_ref[...], kp_ref[pl.ds(st, 3 * TQ), :]], axis=0),
                jnp.concatenate([kcsw_ref[...], kswp_ref[pl.ds(st, 3 * TQ), :]], axis=0))
        vall = (jnp.concatenate([vc_ref[...], vp_ref[pl.ds(st, 3 * TQ), :]], axis=0),
                jnp.concatenate([vcsw_ref[...], vswp_ref[pl.ds(st, 3 * TQ), :]], axis=0))
        lo = lax.broadcasted_iota(jnp.int32, (1, 128), 1) < 64
        qrefs = (q0_ref, q1_ref)
        dq = [jnp.zeros((TQ, 128), F32) for _ in range(4)]
        dkt = jnp.zeros((128, LC + 3 * TQ), F32)
        dvt = jnp.zeros((128, LC + 3 * TQ), F32)
        combos = [(g, e) for g in range(2) for e in range(2)]

        def stacked(ref, g):
            return jnp.concatenate([ref[:, 256 * g:256 * g + 128], ref[:, 256 * g + 128:256 * g + 256]], axis=0)

        for n, (g, e) in enumerate(combos):
            me = lo if e == 0 else jnp.logical_not(lo)
            sw = 0 if e == g else 1
            qm = stacked(qrefs[e], g)
            p = ex_ref[n].astype(F32) * stat_ref[n, :, 0:1]
            dom = jnp.where(me, stacked(do_ref, g), 0.0)
            dd = jnp.sum(dom * stacked(o_ref, g), axis=-1, keepdims=True)
            domx = dom.astype(_MX)
            sd = stat_ref[n, :, 64:65] * dd
            dsink_ref[4 * g + e:4 * g + e + 1, :] -= jnp.sum(sd[:TQ], axis=0, keepdims=True)
            dsink_ref[4 * g + 2 + e:4 * g + 3 + e, :] -= jnp.sum(sd[TQ:], axis=0, keepdims=True)
            ds = (p * (_dot_nt(domx, vall[sw]) - dd)).astype(_MX)
            dq2 = _dot(ds, kall[sw])
            dq[2 * g] = dq[2 * g] + jnp.where(me, dq2[:TQ], 0.0)
            dq[2 * g + 1] = dq[2 * g + 1] + jnp.where(me, dq2[TQ:], 0.0)
            dk2 = _dot_tn(qm, ds)
            dv2 = _dot_tn(domx, p.astype(_MX))
            if sw:
                dk2 = pltpu.roll(dk2, 64, 0)
                dv2 = pltpu.roll(dv2, 64, 0)
            dkt = dkt + dk2
            dvt = dvt + dv2
        dkt = dkt.T
        dvt = dvt.T
        for j in range(4):
            dq_ref[:, 128 * j:128 * (j + 1)] = dq[j] * 0.125
        dkc_ref[...] += dkt[:LC]
        dvc_ref[...] += dvt[:LC]
        dk_ref[pl.ds(st, 3 * TQ), :] += dkt[LC:]
        dv_ref[pl.ds(st, 3 * TQ), :] += dvt[LC:]
        if ne:
            pl.when(i == nb - 1)(exch.finish)

    full = lambda a: pl.BlockSpec(a.shape, lambda i: (0, 0))
    fs = lambda r: pl.BlockSpec((r, 128), lambda i: (0, 0))
    qs = pl.BlockSpec((TQ, AW), lambda i: (i, 0))
    res = pl.pallas_call(
        body, name="attn_bwd", grid=(nb,),
        in_specs=[qs, qs, full(kp), full(kp), full(kp), full(kp), full(kc), full(kc), full(kc), full(kc),
                  pl.BlockSpec((None, 4, 2 * TQ, LC + 3 * TQ), lambda i: (i, 0, 0, 0)),
                  pl.BlockSpec((None, 4, 2 * TQ, 128), lambda i: (i, 0, 0, 0)), qs, qs] + _any_specs(ne),
        out_specs=[qs, fs(s + 2 * TQ), fs(s + 2 * TQ), fs(LC), fs(LC), fs(8)] + _any_specs(ne),
        out_shape=[jax.ShapeDtypeStruct((s, AW), F32), jax.ShapeDtypeStruct((s + 2 * TQ, 128), F32),
                   jax.ShapeDtypeStruct((s + 2 * TQ, 128), F32), jax.ShapeDtypeStruct((LC, 128), F32),
                   jax.ShapeDtypeStruct((LC, 128), F32), jax.ShapeDtypeStruct((8, 128), F32)]
        + _Exchange.out_shapes(exchange),
        scratch_shapes=_Exchange.semaphores(ne) if ne else [],
        compiler_params=_cp("arbitrary"),
    )(q0, q1, kp, kswp, vp, vswp, kc, kcsw, vc, vcsw, ex, stat, o, do, *exchange)
    return res[:6], res[6:]


def _mix_in_bwd(dq, dk, dv, dga, dgb, proj, x, dx1, win, nw, sc, qnw2, knw2, cos, sa, sb, gmat):
    s = x.shape[0]

    def body(dq_ref, dk_ref, dv_ref, dga_ref, dgb_ref, proj_ref, x_ref, dx1_ref, win_ref, nw_ref, sc_ref,
             qnw_ref, knw_ref, cos_ref, sa_ref, sb_ref, g_ref, gx_ref, dproj_ref, vec_ref):
        @pl.when(pl.program_id(0) == 0)
        def _():
            vec_ref[...] = jnp.zeros((8, D), F32)
        cs, sav, sbv, g = cos_ref[...], sa_ref[...], sb_ref[...], g_ref[...]
        gq = jnp.zeros((1, 128), F32)
        for j in range(4):
            dqn = _rope_bwd(dq_ref[:, 128 * j:128 * (j + 1)], cs, sav, sbv)
            dch, gw = _headnorm_bwd(proj_ref[:, 128 * j:128 * (j + 1)], dqn, qnw_ref[...], g)
            dproj_ref[:, 128 * j:128 * (j + 1)] = dch.astype(_MX)
            gq = gq + _rowsum(gw)
        dkn = _rope_bwd(dk_ref[...], cs, sav, sbv)
        dch, gw = _headnorm_bwd(proj_ref[:, 512:640], dkn, knw_ref[...], g)
        dproj_ref[:, 512:640] = dch.astype(_MX)
        dproj_ref[:, 640:768] = dv_ref[...].astype(_MX)
        dproj_ref[:, 768:1280] = dga_ref[...]
        dproj_ref[:, 1280:1792] = dgb_ref[...]
        vec_ref[3:4, 0:128] += gq
        vec_ref[3:4, 128:256] += _rowsum(gw)
        dh = _dot_nt(dproj_ref[...], win_ref[...])
        xv = x_ref[...]
        r = lax.rsqrt(jnp.mean(xv * xv, axis=-1, keepdims=True) + EPS)
        xn = xv * r
        nwv, scv = nw_ref[...], sc_ref[...]
        vec_ref[0:1, :] += _rowsum(dh)
        vec_ref[1:2, :] += _rowsum(dh * xn) * nwv
        vec_ref[2:3, :] += _rowsum(dh * xn) * (1.0 + scv)
        dxn = dh * (nwv * (1.0 + scv))
        gx_ref[...] = dx1_ref[...] + r * (dxn - xn * jnp.mean(dxn * xn, axis=-1, keepdims=True))

    row = lambda n: pl.BlockSpec((1, n), lambda i: (0, 0))
    tile = lambda n: pl.BlockSpec((TM, n), lambda i: (i, 0))
    return pl.pallas_call(
        body, name="mix_in_bwd", grid=(s // TM,),
        in_specs=[tile(AW), tile(128), tile(128), tile(CW), tile(CW), tile(INW), tile(D), tile(D),
                  pl.BlockSpec((D, INW), lambda i: (0, 0)), row(D), row(D), row(128), row(128),
                  tile(128), tile(128), tile(128), pl.BlockSpec((128, 128), lambda i: (0, 0))],
        out_specs=[tile(D), tile(INW), pl.BlockSpec((8, D), lambda i: (0, 0))],
        out_shape=[jax.ShapeDtypeStruct((s, D), F32), jax.ShapeDtypeStruct((s, INW), _MX),
                   jax.ShapeDtypeStruct((8, D), F32)],
        compiler_params=_cp("arbitrary"),
    )(dq, dk, dv, dga, dgb, proj, x, dx1, win, nw, sc, qnw2, knw2, cos, sa, sb, gmat)


def _ctx_bwd(ctx, nw, scc, winkv, kvc, hc, dkc, dvc, knw2, gmat):
    def body(ctx_ref, nw_ref, sc_ref, w_ref, kvc_ref, hc_ref, dkc_ref, dvc_ref, knw_ref, g_ref, gw_ref, vec_ref):
        dkr, gk = _headnorm_bwd(kvc_ref[:, 0:128], dkc_ref[...], knw_ref[...], g_ref[...])
        dkv = jnp.concatenate([dkr, dvc_ref[...]], axis=1).astype(_MX)
        gw_ref[...] = _dot_tn(hc_ref[...], dkv)
        dh = _dot_nt(dkv, w_ref[...])
        cv = ctx_ref[...]
        r = lax.rsqrt(jnp.mean(cv * cv, axis=-1, keepdims=True) + EPS)
        cn = cv * r
        vec_ref[...] = jnp.zeros((8, D), F32)
        vec_ref[0:1, :] = _rowsum(dh)
        vec_ref[1:2, :] = _rowsum(dh * cn) * nw_ref[...]
        vec_ref[2:3, :] = _rowsum(dh * cn) * (1.0 + sc_ref[...])
        vec_ref[3:4, 0:128] = _rowsum(gk)

    return pl.pallas_call(
        body, name="ctx_bwd", in_specs=[_vspec()] * 10, out_specs=[_vspec()] * 2,
        out_shape=[jax.ShapeDtypeStruct((D, 256), F32), jax.ShapeDtypeStruct((8, D), F32)],
        compiler_params=pltpu.CompilerParams(vmem_limit_bytes=VMEM_LIMIT),
    )(ctx, nw, scc, winkv, kvc, hc, dkc, dvc, knw2, gmat)


def _mod_fwd(cs, wmod, bloc):
    def body(c_ref, w_ref, b_ref, o_ref):
        cv = c_ref[...]
        o_ref[...] = _dot((cv * _sig(cv)).astype(_MX), w_ref[...].astype(_MX)) + b_ref[...]

    return pl.pallas_call(
        body, name="mod_fwd", in_specs=[_vspec()] * 3, out_specs=_vspec(),
        out_shape=jax.ShapeDtypeStruct((16, wmod.shape[1]), F32),
        compiler_params=pltpu.CompilerParams(vmem_limit_bytes=VMEM_LIMIT),
    )(cs, wmod, bloc)


def _mod_bwd(cs, dmod, wmod):
    def body(c_ref, d_ref, w_ref, gw_ref, part_ref):
        cv = c_ref[...]
        sl = (cv * _sig(cv)).astype(_MX)
        dm = d_ref[...].astype(_MX)
        gw_ref[...] = _dot_tn(sl, dm)
        part_ref[...] = _dot_nt(dm, w_ref[...].astype(_MX))

    return pl.pallas_call(
        body, name="mod_bwd", in_specs=[_vspec()] * 3, out_specs=[_vspec()] * 2,
        out_shape=[jax.ShapeDtypeStruct(wmod.shape, F32), jax.ShapeDtypeStruct((16, D), F32)],
        compiler_params=pltpu.CompilerParams(vmem_limit_bytes=VMEM_LIMIT),
    )(cs, dmod, wmod)


def _sum_leading(a, name):
    n = a.shape[0]

    def body(a_ref, o_ref):
        acc = a_ref[0]
        for k in range(1, n):
            acc = acc + a_ref[k]
        o_ref[...] = acc

    return pl.pallas_call(
        body, name=name, in_specs=[_vspec()], out_specs=_vspec(),
        out_shape=jax.ShapeDtypeStruct(a.shape[1:], F32),
        compiler_params=pltpu.CompilerParams(vmem_limit_bytes=VMEM_LIMIT),
    )(a)


def _cctx_grad(parts, cc):
    def body(p_ref, c_ref, o_ref):
        acc = p_ref[0, 8:9, :]
        for k in range(1, 4):
            acc = acc + p_ref[k, 8:9, :]
        cv = c_ref[...]
        sg = _sig(cv)
        o_ref[...] = acc * (sg * (1.0 + cv * (1.0 - sg)))

    return pl.pallas_call(
        body, name="cctx_grad", in_specs=[_vspec()] * 2, out_specs=_vspec(),
        out_shape=jax.ShapeDtypeStruct((1, D), F32),
    )(parts, cc)


def _adam_math(w, g, m, v):
    mn = ADAM_B1 * m + (1.0 - ADAM_B1) * g
    vn = ADAM_B2 * v + (1.0 - ADAM_B2) * (g * g)
    mh = mn / (1.0 - ADAM_B1 ** ADAM_STEP)
    vh = vn / (1.0 - ADAM_B2 ** ADAM_STEP)
    delta = -ADAM_LR * (mh / (jnp.sqrt(vh) + ADAM_EPS) + ADAM_WD * w)
    return delta, mn, vn


def _adam_big(w, g, m, v, name):
    r, n = w.shape
    tr = 256 if r % 256 == 0 else 64

    def body(w_ref, g_ref, m_ref, v_ref, d_ref, mo_ref, vo_ref):
        d, mn, vn = _adam_math(w_ref[...], g_ref[...], m_ref[...], v_ref[...])
        d_ref[...] = d
        mo_ref[...] = mn
        vo_ref[...] = vn

    spec = pl.BlockSpec((tr, n), lambda i: (i, 0))
    return pl.pallas_call(
        body, name=name, grid=(r // tr,), in_specs=[spec] * 4, out_specs=[spec] * 3,
        out_shape=[jax.ShapeDtypeStruct((r, n), F32)] * 3,
        compiler_params=_cp("arbitrary"),
    )(w, g, m, v)


def _adam_small(ws, gs, ms, vs):
    n = len(ws)

    def body(*refs):
        ins, outs = refs[:4 * n], refs[4 * n:]
        for k in range(n):
            d, mn, vn = _adam_math(ins[k][...], ins[n + k][...], ins[2 * n + k][...], ins[3 * n + k][...])
            outs[k][...] = d
            outs[n + k][...] = mn
            outs[2 * n + k][...] = vn

    shapes = [jax.ShapeDtypeStruct(w.shape, F32) for w in ws]
    res = pl.pallas_call(
        body, name="adam_small", in_specs=[_vspec()] * (4 * n), out_specs=[_vspec()] * (3 * n),
        out_shape=shapes * 3,
    )(*ws, *gs, *ms, *vs)
    return res[:n], res[n:2 * n], res[2 * n:]


def _pair_sum(grads, from_sib, core, name):
    n = len(grads)

    def body(c_ref, *refs):
        for w in range(n):
            a_ref, b_ref, o_ref = refs[w], refs[n + w], refs[2 * n + w]
            o_ref[...] = (a_ref[...].astype(F32) + b_ref[...].astype(F32)).astype(_MX)

    halves = [(None, g.shape[1] // 2, g.shape[2]) for g in grads]
    return pl.pallas_call(
        body, name=name,
        grid_spec=pltpu.PrefetchScalarGridSpec(
            num_scalar_prefetch=1, grid=(4,),
            in_specs=[pl.BlockSpec(h, lambda j, c: (j, c[0], 0)) for h in halves]
            + [pl.BlockSpec(h, lambda j, c: (j, 0, 0)) for h in halves],
            out_specs=[pl.BlockSpec(h, lambda j, c: (j, 0, 0)) for h in halves]),
        out_shape=[jax.ShapeDtypeStruct(p.shape, _MX) for p in from_sib],
        compiler_params=_cp("arbitrary"),
    )(core.reshape(1), *grads, *from_sib)


def _chip_sum(parts, arrived, chip, core):
    n = len(parts)

    def body(s_ref, *refs):
        for w in range(n):
            own_ref, p_ref, o_ref = refs[w], refs[n + w], refs[2 * n + w]
            acc = own_ref[...].astype(F32)
            for k in range(3):
                acc = acc + p_ref[k].astype(F32)
            o_ref[...] = acc

    blk = [(p.shape[1] // 2, p.shape[2]) for p in parts]
    return pl.pallas_call(
        body, name="grad_chip_sum",
        grid_spec=pltpu.PrefetchScalarGridSpec(
            num_scalar_prefetch=1, grid=(2,),
            in_specs=[pl.BlockSpec((None,) + b, lambda i, s: (s[0], i, 0)) for b in blk]
            + [pl.BlockSpec((3,) + b, lambda i, s: (0, i, 0)) for b in blk],
            out_specs=[pl.BlockSpec(b, lambda i, s: (2 * s[1] + i, 0)) for b in blk]),
        out_shape=[jax.ShapeDtypeStruct((2 * p.shape[1], p.shape[2]), F32) for p in parts],
        compiler_params=_cp("arbitrary"),
    )(jnp.stack([chip, core]), *parts, *arrived)


def _cast_shards(ws, chip):
    n = len(ws)

    def body(s_ref, *refs):
        for w in range(n):
            refs[n + w][...] = refs[w][...].astype(_MX)

    blk = [(a.shape[0] // 2, a.shape[1]) for a in ws]
    return pl.pallas_call(
        body, name="cast_shards",
        grid_spec=pltpu.PrefetchScalarGridSpec(
            num_scalar_prefetch=1, grid=(2,),
            in_specs=[pl.BlockSpec(b, lambda i, s: (i, 0)) for b in blk],
            out_specs=[pl.BlockSpec((None,) + b, lambda i, s: (s[0], i, 0)) for b in blk]),
        out_shape=[jax.ShapeDtypeStruct((4,) + a.shape, _MX) for a in ws],
        compiler_params=_cp("arbitrary"),
    )(chip.reshape(1), *ws)


def _position():
    x, y, c = lax.axis_index("x"), lax.axis_index("y"), lax.axis_index("c")
    return x, y, c


def _small_allgather(blk, name, exchange=(), share=(), gather=()):
    m_per, n = blk.shape
    assert not (share and gather)
    share = tuple(share) + tuple(gather)
    ne, ns = len(exchange), len(share)

    def body(*refs):
        x_ref = refs[0]
        out_ref = refs[1 + ne + ns]
        send_sems, recv_sems = refs[2 + 2 * ne + 2 * ns], refs[3 + 2 * ne + 2 * ns]
        extra = refs[4 + 2 * ne + 2 * ns:]
        x, y, c = _position()
        me, sibling = (x, y, c), (x, y, 1 - c)
        chips = [(1 - x, y), (x, 1 - y), (1 - x, 1 - y)]
        if ne:
            exch = _Exchange(refs[1:1 + ne], refs[2 + ne + ns:2 + 2 * ne + ns], extra[0], extra[1])
            exch.start()
        if gather:
            gat = _Gather(refs[1 + ne:1 + ne + ns], refs[2 + 2 * ne + ns:2 + 2 * ne + 2 * ns],
                          extra[2 * bool(ne)], extra[2 * bool(ne) + 1], [b.shape[1] for b in gather])
            gat.start()
        elif ns:
            halves = _share_copies(refs[1 + ne:1 + ne + ns], refs[2 + 2 * ne + ns:2 + 2 * ne + 2 * ns],
                                   extra[2 * bool(ne)], extra[2 * bool(ne) + 1], [b.shape[0] for b in share])
            for cp in halves:
                cp.start()

        def rows(px, py, pc):
            return out_ref.at[pl.ds(pl.multiple_of((4 * px + 2 * py + pc) * m_per, 8), m_per), :]

        def copy(k, block, to, src=None):
            return pltpu.make_async_remote_copy(
                src_ref=rows(*block) if src is None else src, dst_ref=rows(*block),
                send_sem=send_sems.at[k], recv_sem=recv_sems.at[k], device_id=to, device_id_type=MESH)

        out_ref[pl.ds(pl.multiple_of((4 * x + 2 * y + c) * m_per, 8), m_per), :] = x_ref[...]
        first = [copy(0, me, sibling, src=x_ref)]
        first += [copy(1 + j, me, (*chip, c), src=x_ref) for j, chip in enumerate(chips)]
        for cp in first:
            cp.start()
        passed = [copy(4 + j, (*chip, c), sibling) for j, chip in enumerate(chips)]
        for j, chip in enumerate(chips):
            copy(1 + j, (*chip, c), me).wait_recv()
            passed[j].start()
        copy(0, sibling, me).wait_recv()
        for j, chip in enumerate(chips):
            copy(4 + j, (*chip, 1 - c), me).wait_recv()
        for cp in first + passed:
            cp.wait_send()
        if ne:
            exch.finish()
        if gather:
            gat.forward()
            gat.finish()
        elif ns:
            for cp in halves:
                cp.wait()

    res = pl.pallas_call(
        body, name=name,
        out_shape=[jax.ShapeDtypeStruct((8 * m_per, n), blk.dtype)] + _Exchange.out_shapes(exchange)
        + [jax.ShapeDtypeStruct(b.shape, b.dtype) for b in share],
        in_specs=[_vspec()] + _any_specs(ne + ns), out_specs=[_vspec()] + _any_specs(ne + ns),
        input_output_aliases={1 + ne + w: 1 + ne + w for w in range(ns)},
        scratch_shapes=[pltpu.SemaphoreType.DMA((7,)), pltpu.SemaphoreType.DMA((7,))]
        + (_Exchange.semaphores(ne) if ne else [])
        + (_Gather.semaphores(ns) if gather else
           [pltpu.SemaphoreType.DMA((ns,)), pltpu.SemaphoreType.DMA((ns,))] if ns else []),
    )(blk, *exchange, *share)
    return res[0], res[1:1 + ne], res[1 + ne:]


def _share_copies(src_refs, out_refs, send_sems, recv_sems, nrows):
    x, y, c = _position()
    return [pltpu.make_async_remote_copy(
        src_ref=_rows(src_refs[w], c, r), dst_ref=_rows(out_refs[w], c, r), send_sem=send_sems.at[w],
        recv_sem=recv_sems.at[w], device_id=(x, y, 1 - c), device_id_type=MESH) for w, r in enumerate(nrows)]


def _any_specs(n):
    return [pl.BlockSpec(memory_space=pl.ANY)] * n


def _rows(ref, half, nrows):
    return ref.at[pl.ds(half * (nrows // 2), nrows // 2), :]


class _Gather:
    def __init__(self, src_refs, out_refs, send_sems, recv_sems, nrows):
        x, y, c = _position()
        chip = 2 * x + y
        sibling = (x, y, 1 - c)

        def copy(k, src, dst, to):
            return pltpu.make_async_remote_copy(src_ref=src, dst_ref=dst, send_sem=send_sems.at[k],
                                                recv_sem=recv_sems.at[k], device_id=to, device_id_type=MESH)

        self.first, self.first_landed, self.passed, self.passed_landed = [], [], [], []
        for w, r in enumerate(nrows):
            for j, ch in enumerate([(1 - x, y), (x, 1 - y), (1 - x, 1 - y)]):
                theirs = _rows(out_refs[w].at[2 * ch[0] + ch[1]], c, r)
                other = _rows(out_refs[w].at[2 * ch[0] + ch[1]], 1 - c, r)
                self.first.append(copy(6 * w + j, _rows(src_refs[w].at[chip], c, r),
                                       _rows(out_refs[w].at[chip], c, r), (*ch, c)))
                self.first_landed.append(copy(6 * w + j, theirs, theirs, sibling))
                self.passed.append(copy(6 * w + 3 + j, theirs, theirs, sibling))
                self.passed_landed.append(copy(6 * w + 3 + j, other, other, sibling))

    in_place = True

    @staticmethod
    def semaphores(n):
        return [pltpu.SemaphoreType.DMA((6 * n,)), pltpu.SemaphoreType.DMA((6 * n,))]

    @staticmethod
    def results(bufs):
        return [jax.ShapeDtypeStruct(b.shape, b.dtype) for b in bufs]

    def start(self):
        for cp in self.first:
            cp.start()

    def forward(self):
        for landed, cp in zip(self.first_landed, self.passed):
            landed.wait_recv()
            cp.start()

    def finish(self):
        for landed in self.passed_landed:
            landed.wait_recv()
        for cp in self.first + self.passed:
            cp.wait_send()


class _Swap:
    in_place = False

    def __init__(self, grad_refs, out_refs, send_sems, recv_sems, nrows):
        x, y, c = _position()
        self.copies = [pltpu.make_async_remote_copy(
            src_ref=grad_refs[w].at[:, pl.ds((1 - c) * (r // 2), r // 2), :], dst_ref=out_refs[w],
            send_sem=send_sems.at[w], recv_sem=recv_sems.at[w], device_id=(x, y, 1 - c), device_id_type=MESH)
            for w, r in enumerate(nrows)]

    @staticmethod
    def semaphores(n):
        return [pltpu.SemaphoreType.DMA((n,)), pltpu.SemaphoreType.DMA((n,))]

    @staticmethod
    def results(grads):
        return [jax.ShapeDtypeStruct((4, g.shape[1] // 2, g.shape[2]), g.dtype) for g in grads]

    def start(self):
        for cp in self.copies:
            cp.start()

    def forward(self):
        pass

    def finish(self):
        for cp in self.copies:
            cp.wait()


def _grad_swap(grads, name):
    n = len(grads)

    def body(*refs):
        swap = _Swap(refs[:n], refs[n:2 * n], refs[2 * n], refs[2 * n + 1], [g.shape[1] for g in grads])
        swap.start()
        swap.finish()

    return pl.pallas_call(
        body, name=name, out_shape=_Swap.results(grads), in_specs=_any_specs(n), out_specs=_any_specs(n),
        scratch_shapes=_Swap.semaphores(n),
    )(*grads)


class _Exchange:
    in_place = False

    def __init__(self, part_refs, out_refs, send_sems, recv_sems, nrows=None):
        x, y, c = _position()
        self.copies = [
            pltpu.make_async_remote_copy(
                src_ref=part_refs[w].at[2 * ch[0] + ch[1]], dst_ref=out_refs[w].at[k], send_sem=send_sems.at[3 * w + k],
                recv_sem=recv_sems.at[3 * w + k], device_id=(*ch, c), device_id_type=MESH)
            for w in range(len(part_refs)) for k, ch in enumerate([(1 - x, y), (x, 1 - y), (1 - x, 1 - y)])]

    @staticmethod
    def semaphores(n):
        return [pltpu.SemaphoreType.DMA((3 * n,)), pltpu.SemaphoreType.DMA((3 * n,))]

    @staticmethod
    def out_shapes(parts):
        return [jax.ShapeDtypeStruct((3,) + p.shape[1:], p.dtype) for p in parts]

    results = out_shapes

    def start(self):
        for cp in self.copies:
            cp.start()

    def forward(self):
        pass

    def finish(self):
        for cp in self.copies:
            cp.wait()


def _rope_tables(s):
    rows = s // GRID_W
    inv = 10000.0 ** (-jnp.arange(0, 32, 2, dtype=F32) / 32.0)
    ang_r = jnp.arange(rows, dtype=F32)[:, None] * inv
    ang_c = jnp.arange(GRID_W, dtype=F32)[:, None] * inv
    lane = jnp.arange(128)
    first = (lane % 32) < 16
    by_row = (lane % 64) < 32

    def table(fn):
        tr = jnp.tile(fn(ang_r), (1, 8))[:, None, :]
        tc = jnp.tile(fn(ang_c), (1, 8))[None, :, :]
        return jnp.where(by_row, tr, tc).reshape(s, 128)

    cos, sin = table(jnp.cos), table(jnp.sin)
    return cos, jnp.where(first, -sin, 0.0), jnp.where(first, 0.0, sin)


def _local_step(x, ctx, tgt, mod, modc, nw1, win, qnw, knw, sink, cw, cb, cnw, cnb, wout4, nw2, wup, fcw, fcb, wdown4,
                core=None):
    s = x.shape[0]
    sh1, sc1, g1, sh2, sc2, g2 = [mod[:, D * k:D * (k + 1)] for k in range(6)]
    shc, scc = modc[:, :D], modc[:, D:2 * D]
    cos, sa, sb = _rope_tables(s)
    gi = jnp.arange(128) // 64
    gmat = (gi[:, None] == gi[None, :]).astype(_MX)
    qnw2, knw2 = jnp.tile(qnw, (1, 2)), jnp.tile(knw, (1, 2))
    cw32 = jnp.pad(cw, ((0, 1), (0, 0)))
    fcw2 = jnp.pad(fcw, ((0, 5), (0, 0))).reshape(8, 2, FH).transpose(1, 0, 2)
    fcb2 = fcb.reshape(2, 1, FH)
    winkv = win[:, 512:768]
    sinkv = sink.reshape(8)

    host = (lambda *bufs: ()) if core is None else (lambda *bufs: bufs)
    (proj, h, q0, q1, k, ksw, v, vsw, u0), got = _mix_in(x, nw1, sc1, sh1, win, qnw2, knw2, cos, sa, sb, gmat,
                                                        gather=host(wout4))
    wout = (got[0] if got else wout4).reshape(D, D)
    kvc, hc, kc, kcsw, vc, vcsw = _ctx_kv(ctx, nw1, scc, shc, winkv, knw2, gmat)
    padr = lambda a: jnp.pad(a, ((TQ, TQ), (0, 0)))
    kp, kswp, vp, vswp = padr(k), padr(ksw), padr(v), padr(vsw)
    (o, pex, pstat), got = _attn_fwd(q0, q1, kp, kswp, vp, vswp, kc, kcsw, vc, vcsw, sinkv, gather=host(wup))
    wup = got[0] if got else wup
    u1 = _conv31(u0, cw32, cb)
    x1, mix, cat = _mix_out(o, u1, cnw, cnb, wout, x, g1)
    (h2, up0), got = _ffn_up(x1, nw2, sc2, sh2, wup, gather=host(wdown4))
    wdown = (got[0] if got else wdown4).reshape(FH, D)
    act, gv = _ffn_conv_act(up0, fcw2, fcb2)
    dy, dact, ddn, vec_dn = _ffn_down(act, wdown, x1, g2, tgt)
    loss = (0.5 / D) * jnp.sum(vec_dn[1])

    dup, gfc = _ffn_conv_bwd(up0, gv, dact, fcw2)
    g_wdown = _tn_matmul(act, ddn, FQ, D, "gw_down").reshape(4, FH // 4, D)
    g_wup = _tn_matmul(h2, dup, D, FQ, "gw_up", b_split=True, by_chip=True)
    dx1, dmix, vec_up = _ffn_up_bwd(dup, wup, x1, dy, mix, nw2, sc2, g1)
    g_wout = _tn_matmul(cat, dmix, D, D, "gw_out").reshape(4, D // 4, D)
    late = [g_wout, g_wup, g_wdown]
    (do, du1, vec_ln), from_sib = _mix_out_bwd(dmix, wout, u1, cnw, cnb, swap=() if core is None else late)
    parts = () if core is None else _pair_sum(late, from_sib, core, "grad_pair_sum_late")
    (dga, dgb, gcw), arrived_a = _conv31_bwd(du1, u0, cw32, proj, exchange=parts[0:1] + parts[2:3])
    (dq, dkp, dvp, dkc, dvc, dsink), arrived_b = _attn_bwd(q0, q1, kp, kswp, vp, vswp, kc, kcsw, vc, vcsw, pex, pstat,
                                                           o, do, exchange=parts[1:2])
    if core is not None:
        g_wout, g_wup, g_wdown = zip(parts, [arrived_a[0], arrived_b[0], arrived_a[1]])
    gwinkv, vec_ctx = _ctx_bwd(ctx, nw1, scc, winkv, kvc, hc, dkc, dvc, knw2, gmat)
    gx, dproj, vec_in = _mix_in_bwd(dq, dkp[TQ:TQ + s], dvp[TQ:TQ + s], dga, dgb, proj, x, dx1, win, nw1, sc1,
                                    qnw2, knw2, cos, sa, sb, gmat)
    g_win = _tn_matmul(h, dproj, D, INW, "gw_in", init=gwinkv)

    g_qn = vec_in[3:4, 0:128]
    g_kn = vec_in[3:4, 128:256] + vec_ctx[3:4, 0:128]
    grads = dict(
        norm_mix_w=vec_in[2:3] + vec_ctx[2:3], w_in=g_win,
        q_norm_w=g_qn[:, :64] + g_qn[:, 64:], k_norm_w=g_kn[:, :64] + g_kn[:, 64:],
        sink_logit=dsink[:, 0].reshape(1, 8), conv_w=gcw[:CK], conv_b=gcw[CK:CK + 1],
        conv_norm_w=vec_ln[1:2], conv_norm_b=vec_ln[0:1], w_out=g_wout, norm_ffn_w=vec_up[2:3], w_up=g_wup,
        ffn_conv_w=gfc[:, 0:3].transpose(1, 0, 2).reshape(3, 2 * FH), ffn_conv_b=gfc[:, 3].reshape(1, 2 * FH),
        w_down=g_wdown)
    dmod = jnp.concatenate([vec_in[0:1], vec_in[1:2], vec_up[3:4], vec_up[0:1], vec_up[1:2], vec_dn[0:1]], axis=1)
    dmodc = jnp.concatenate([vec_ctx[0:1], vec_ctx[1:2]], axis=1)
    return loss, gx, grads, dmod, dmodc


_SMALL = ["c_ctx", "b_mod", "norm_mix_w", "q_norm_w", "k_norm_w", "sink_logit", "conv_w", "conv_b", "conv_norm_w",
          "conv_norm_b", "norm_ffn_w", "ffn_conv_w", "ffn_conv_b"]
_GATHERED = ["w_in", "w_out", "w_up", "w_down"]
_BIG = ["w_mod"] + _GATHERED
_ORDER = ["c_ctx", "w_mod", "b_mod", "norm_mix_w", "w_in", "q_norm_w", "k_norm_w", "sink_logit", "conv_w", "conv_b",
          "conv_norm_w", "conv_norm_b", "w_out", "norm_ffn_w", "w_up", "ffn_conv_w", "ffn_conv_b", "w_down"]
_PACK = [("norm_mix_w", 1), ("norm_ffn_w", 1), ("q_norm_w", 1), ("k_norm_w", 1), ("sink_logit", 1), ("conv_b", 1),
         ("conv_norm_w", 1), ("conv_norm_b", 1), ("ffn_conv_b", 6), ("conv_w", 16), ("ffn_conv_w", 17)]
_PACK_ROWS = 56


def _pack_rows(a, nrows):
    flat = a.reshape(-1)
    return jnp.pad(flat, (0, nrows * D - flat.shape[0])).reshape(nrows, D)


def kernel(x, c, ctx, c_ctx, w_mod, b_mod, norm_mix_w, w_in, q_norm_w, k_norm_w, sink_logit, conv_w, conv_b, conv_norm_w, conv_norm_b, w_out, norm_ffn_w, w_up, ffn_conv_w, ffn_conv_b, w_down, loss_target, m_c_ctx, m_w_mod, m_b_mod, m_norm_mix_w, m_w_in, m_q_norm_w, m_k_norm_w, m_sink_logit, m_conv_w, m_conv_b, m_conv_norm_w, m_conv_norm_b, m_w_out, m_norm_ffn_w, m_w_up, m_ffn_conv_w, m_ffn_conv_b, m_w_down, v_c_ctx, v_w_mod, v_b_mod, v_norm_mix_w, v_w_in, v_q_norm_w, v_k_norm_w, v_sink_logit, v_conv_w, v_conv_b, v_conv_norm_w, v_conv_norm_b, v_w_out, v_norm_ffn_w, v_w_up, v_ffn_conv_w, v_ffn_conv_b, v_w_down):
    w = dict(c_ctx=c_ctx.reshape(1, D), w_mod=w_mod[0], b_mod=b_mod, norm_mix_w=norm_mix_w, w_in=w_in[0],
             q_norm_w=q_norm_w, k_norm_w=k_norm_w, sink_logit=sink_logit, conv_w=conv_w[0], conv_b=conv_b,
             conv_norm_w=conv_norm_w, conv_norm_b=conv_norm_b, w_out=w_out[0], norm_ffn_w=norm_ffn_w, w_up=w_up[0],
             ffn_conv_w=ffn_conv_w[0], ffn_conv_b=ffn_conv_b, w_down=w_down[0])
    m = dict(c_ctx=m_c_ctx.reshape(1, D), w_mod=m_w_mod[0], b_mod=m_b_mod, norm_mix_w=m_norm_mix_w, w_in=m_w_in[0],
             q_norm_w=m_q_norm_w, k_norm_w=m_k_norm_w, sink_logit=m_sink_logit, conv_w=m_conv_w[0], conv_b=m_conv_b,
             conv_norm_w=m_conv_norm_w, conv_norm_b=m_conv_norm_b, w_out=m_w_out[0], norm_ffn_w=m_norm_ffn_w,
             w_up=m_w_up[0], ffn_conv_w=m_ffn_conv_w[0], ffn_conv_b=m_ffn_conv_b, w_down=m_w_down[0])
    v = dict(c_ctx=v_c_ctx.reshape(1, D), w_mod=v_w_mod[0], b_mod=v_b_mod, norm_mix_w=v_norm_mix_w, w_in=v_w_in[0],
             q_norm_w=v_q_norm_w, k_norm_w=v_k_norm_w, sink_logit=v_sink_logit, conv_w=v_conv_w[0], conv_b=v_conv_b,
             conv_norm_w=v_conv_norm_w, conv_norm_b=v_conv_norm_b, w_out=v_w_out[0], norm_ffn_w=v_norm_ffn_w,
             w_up=v_w_up[0], ffn_conv_w=v_ffn_conv_w[0], ffn_conv_b=v_ffn_conv_b, w_down=v_w_down[0])
    xi, yi, ci = _position()
    chip = 2 * xi + yi
    dev = 2 * chip + ci
    s = x.shape[1]
    ncol = w["w_mod"].shape[1]

    win_buf, wout_buf, wup_buf, wdown_buf = _cast_shards([w[n] for n in _GATHERED], chip)

    blk0 = jnp.concatenate([_pack_rows(w["conv_w"], 4), _pack_rows(w["ffn_conv_w"], 5), c,
                            jnp.zeros((6, D), F32)], axis=0)
    g0, _, (win4,) = _small_allgather(blk0, "gather_c_convw", gather=[win_buf])
    win = win4.transpose(1, 0, 2).reshape(D, INW)
    g0 = g0.reshape(8, 16, D)
    c_all = g0[:, 9, :]
    cs = jnp.concatenate([c_all, w["c_ctx"], jnp.zeros((7, D), F32)], axis=0)
    cw_full = jnp.concatenate([g0[2 * j, 0:4].reshape(-1)[:CK * 128].reshape(CK, 128) for j in range(4)], axis=1)
    fcw_full = jnp.concatenate([g0[2 * j, 4:9].reshape(-1)[:3 * 1408].reshape(3, 1408) for j in range(4)], axis=1)

    b_loc = lax.dynamic_slice(w["b_mod"], (0, chip * ncol), (1, ncol))
    modp = _mod_fwd(cs, w["w_mod"], b_loc)
    gm = _small_allgather(modp, "gather_mod")[0].reshape(8, 16, ncol)
    mod_all = jnp.concatenate([gm[2 * j] for j in range(4)], axis=1)
    mod = lax.dynamic_slice(mod_all, (dev, 0), (1, 6 * D))
    modc = mod_all[8:9]

    loss_loc, gx, gl, dmod, dmodc = _local_step(
        x[0], ctx[0], loss_target[0], mod, modc, w["norm_mix_w"], win, w["q_norm_w"], w["k_norm_w"], w["sink_logit"],
        cw_full, w["conv_b"], w["conv_norm_w"], w["conv_norm_b"], wout_buf, w["norm_ffn_w"], wup_buf, fcw_full,
        w["ffn_conv_b"], wdown_buf, core=ci)

    last = [gl["w_in"].reshape(D, 4, INW // 4).transpose(1, 0, 2)]
    parts = _pair_sum(last, _grad_swap(last, "grad_swap_in"), ci, "grad_pair_sum_in")

    pack = jnp.concatenate([dmod.reshape(6, D), dmodc.reshape(2, D)]
                           + [_pack_rows(gl[name], nr) for name, nr in _PACK], axis=0)
    pack = jnp.pad(pack, ((0, _PACK_ROWS - pack.shape[0]), (0, 0)))
    last_row = lax.broadcasted_iota(jnp.int32, pack.shape, 0) == _PACK_ROWS - 1
    pack = jnp.where(last_row, loss_loc, pack)
    gp, arrived, _ = _small_allgather(pack, "gather_small_grads", exchange=parts)
    gp = gp.reshape(8, _PACK_ROWS, D)
    late = [gl[n] for n in ("w_out", "w_up", "w_down")]
    shard_grads = _chip_sum([*parts, *[p for p, _ in late]], [*arrived, *[a for _, a in late]], chip, ci)
    tot = _sum_leading(gp, "sum_small_grads")
    loss = tot[_PACK_ROWS - 1, 0]
    g = {}
    r = 8
    for name, nr in _PACK:
        shape = gl[name].shape
        g[name] = tot[r:r + nr].reshape(-1)[:math.prod(shape)].reshape(shape)
        r += nr
    dmod_all = jnp.concatenate([gp[:, 0:6].reshape(8, 6 * D),
                                jnp.pad(tot[6:8].reshape(1, 2 * D), ((0, 0), (0, 4 * D))),
                                jnp.zeros((7, 6 * D), F32)], axis=0)
    g["b_mod"] = _sum_leading(dmod_all.reshape(16, 1, 6 * D), "grad_b_mod")
    g["w_mod"], part = _mod_bwd(cs, lax.dynamic_slice(dmod_all, (0, chip * ncol), (16, ncol)), w["w_mod"])
    cparts, _, (g["w_in"], g["w_out"], g["w_up"], g["w_down"]) = _small_allgather(part, "gather_cctx",
                                                                                 share=shard_grads)
    g["c_ctx"] = _cctx_grad(cparts.reshape(8, 16, D)[0::2], w["c_ctx"])
    g["conv_w"] = lax.dynamic_slice(g["conv_w"], (0, chip * 128), (CK, 128))
    g["ffn_conv_w"] = lax.dynamic_slice(g["ffn_conv_w"], (0, chip * 1408), (3, 1408))

    delta, new_m, new_v = {}, {}, {}
    for name in _BIG:
        delta[name], new_m[name], new_v[name] = _adam_big(w[name], g[name], m[name], v[name], "adam_" + name)
    ds, ms, vs = _adam_small([w[n] for n in _SMALL], [g[n] for n in _SMALL], [m[n] for n in _SMALL],
                             [v[n] for n in _SMALL])
    for k, name in enumerate(_SMALL):
        delta[name], new_m[name], new_v[name] = ds[k], ms[k], vs[k]

    def shaped(d, name):
        a = d[name]
        if name == "c_ctx":
            return a.reshape(D)
        if name in ("w_mod", "w_in", "w_out", "w_up", "w_down", "conv_w", "ffn_conv_w"):
            return a[None]
        return a

    outs = [loss, gx[None]]
    for d in (g, delta, new_m, new_v):
        outs += [shaped(d, name) for name in _ORDER]
    return tuple(outs)
```

```python
import functools
import math

import jax
import jax.numpy as jnp
from jax import lax
from jax.experimental import pallas as pl
from jax.experimental.pallas import tpu as pltpu

F32 = jnp.float32
_MX = jnp.bfloat16
EPS = 1e-6
NEG = -1e30
D = 1024
AW = 512
CW = 512
INW = 1792
FH = 2816
LC = 256
CK = 31
GRID_W = 64
TM = 512
TQ = 128
NCH = FH // 128
FQ = 2 * FH // 4
VMEM_LIMIT = 56 * 1024 * 1024
MESH = pl.DeviceIdType.MESH

ADAM_LR, ADAM_B1, ADAM_B2, ADAM_EPS, ADAM_WD, ADAM_STEP = 0.001, 0.9, 0.999, 1e-08, 0.01, 10


def _cp(*sem):
    return pltpu.CompilerParams(dimension_semantics=sem, vmem_limit_bytes=VMEM_LIMIT)


def _vspec():
    return pl.BlockSpec(memory_space=pltpu.VMEM)


def _sig(z):
    return 1.0 / (1.0 + jnp.exp(-z))


def _dot(a, b):
    return jnp.dot(a, b, preferred_element_type=F32)


def _dot_nt(a, b):
    return lax.dot_general(a, b, (((1,), (1,)), ((), ())), preferred_element_type=F32)


def _dot_tn(a, b):
    return lax.dot_general(a, b, (((0,), (0,)), ((), ())), preferred_element_type=F32)


def _gsum(v, g):
    hi = v.astype(_MX)
    lo = (v - hi.astype(F32)).astype(_MX)
    return _dot(hi, g) + _dot(lo, g)


def _headnorm(ch, w, g):
    r = lax.rsqrt(_gsum(ch * ch, g) * (1.0 / 64.0) + EPS)
    return ch * r * w


def _headnorm_bwd(ch, dy, w, g):
    r = lax.rsqrt(_gsum(ch * ch, g) * (1.0 / 64.0) + EPS)
    hat = ch * r
    dhat = dy * w
    dch = r * (dhat - hat * (_gsum(dhat * hat, g) * (1.0 / 64.0)))
    return dch, dy * hat


def _rope(u, cos, sa, sb):
    return u * cos + pltpu.roll(u, 112, 1) * sa + pltpu.roll(u, 16, 1) * sb


def _rope_bwd(d, cos, sa, sb):
    return d * cos - pltpu.roll(d, 112, 1) * sa - pltpu.roll(d, 16, 1) * sb


def _rowsum(v):
    return jnp.sum(v, axis=0, keepdims=True)


def _call_hosting_gather(body, gather, step_of, nsteps, *, in_specs, out_specs, out_shape, scratch_shapes=(),
                         comm=None, **kw):
    comm = comm or _Gather
    ng, n_in, n_out = len(gather), len(in_specs), len(out_specs)
    if not ng:
        return pl.pallas_call(body, in_specs=in_specs, out_specs=out_specs, out_shape=out_shape,
                              scratch_shapes=list(scratch_shapes), **kw)

    def hosted(*refs):
        ins, outs = refs[:n_in], refs[n_in + ng:n_in + ng + n_out]
        rest = refs[n_in + 2 * ng + n_out:]
        copies = comm(refs[n_in:n_in + ng], refs[n_in + ng + n_out:n_in + 2 * ng + n_out], rest[0], rest[1],
                      [b.shape[1] for b in gather])
        step = step_of()
        pl.when(step == 0)(copies.start)
        body(*ins, *outs, *rest[2:])
        pl.when(step == max(nsteps - 3, 0))(copies.forward)
        pl.when(step == nsteps - 1)(copies.finish)

    return pl.pallas_call(
        hosted, in_specs=list(in_specs) + _any_specs(ng), out_specs=list(out_specs) + _any_specs(ng),
        out_shape=list(out_shape) + comm.results(gather),
        input_output_aliases={n_in + w: n_out + w for w in range(ng)} if comm.in_place else {},
        scratch_shapes=comm.semaphores(ng) + list(scratch_shapes), **kw)


def _mix_in(x, nw, sc, sh, win, qnw2, knw2, cos, sa, sb, gmat, gather=()):
    s = x.shape[0]

    def body(x_ref, nw_ref, sc_ref, sh_ref, win_ref, qnw_ref, knw_ref, cos_ref, sa_ref, sb_ref, g_ref,
             proj_ref, h_ref, q0_ref, q1_ref, k_ref, ksw_ref, v_ref, vsw_ref, u0_ref):
        xv = x_ref[...]
        r = lax.rsqrt(jnp.mean(xv * xv, axis=-1, keepdims=True) + EPS)
        h = xv * r * (nw_ref[...] * (1.0 + sc_ref[...])) + sh_ref[...]
        hb = h.astype(_MX)
        h_ref[...] = hb
        proj = _dot(hb, win_ref[...])
        proj_ref[...] = proj.astype(_MX)
        cs, sav, sbv, g = cos_ref[...], sa_ref[...], sb_ref[...], g_ref[...]
        lo = lax.broadcasted_iota(jnp.int32, (1, 128), 1) < 64
        for j in range(4):
            q = _rope(_headnorm(proj[:, 128 * j:128 * (j + 1)], qnw_ref[...], g), cs, sav, sbv) * 0.125
            q0_ref[:, 128 * j:128 * (j + 1)] = jnp.where(lo, q, 0.0).astype(_MX)
            q1_ref[:, 128 * j:128 * (j + 1)] = jnp.where(lo, 0.0, q).astype(_MX)
        k = _rope(_headnorm(proj[:, 512:640], knw_ref[...], g), cs, sav, sbv)
        k_ref[...] = k.astype(_MX)
        ksw_ref[...] = pltpu.roll(k, 64, 1).astype(_MX)
        v = proj[:, 640:768]
        v_ref[...] = v.astype(_MX)
        vsw_ref[...] = pltpu.roll(v, 64, 1).astype(_MX)
        u0_ref[...] = proj[:, 768:1280] * _sig(proj[:, 1280:1792])

    row = lambda n: pl.BlockSpec((1, n), lambda i: (0, 0))
    tile = lambda n: pl.BlockSpec((TM, n), lambda i: (i, 0))
    res = _call_hosting_gather(
        body, gather, lambda: pl.program_id(0), s // TM, name="mix_in", grid=(s // TM,),
        in_specs=[tile(D), row(D), row(D), row(D), pl.BlockSpec((D, INW), lambda i: (0, 0)), row(128), row(128),
                  tile(128), tile(128), tile(128), pl.BlockSpec((128, 128), lambda i: (0, 0))],
        out_specs=[tile(INW), tile(D), tile(AW), tile(AW), tile(128), tile(128), tile(128), tile(128), tile(CW)],
        out_shape=[jax.ShapeDtypeStruct((s, INW), _MX), jax.ShapeDtypeStruct((s, D), _MX),
                   jax.ShapeDtypeStruct((s, AW), _MX), jax.ShapeDtypeStruct((s, AW), _MX),
                   jax.ShapeDtypeStruct((s, 128), _MX), jax.ShapeDtypeStruct((s, 128), _MX),
                   jax.ShapeDtypeStruct((s, 128), _MX), jax.ShapeDtypeStruct((s, 128), _MX),
                   jax.ShapeDtypeStruct((s, CW), F32)],
        compiler_params=_cp("arbitrary"),
    )(x, nw, sc, sh, win, qnw2, knw2, cos, sa, sb, gmat, *gather)
    return res[:9], res[9:]


def _ctx_kv(ctx, nw, scc, shc, winkv, knw2, gmat):
    def body(ctx_ref, nw_ref, sc_ref, sh_ref, w_ref, knw_ref, g_ref,
             kvc_ref, hc_ref, kc_ref, kcsw_ref, vc_ref, vcsw_ref):
        cv = ctx_ref[...]
        r = lax.rsqrt(jnp.mean(cv * cv, axis=-1, keepdims=True) + EPS)
        hc = (cv * r * (nw_ref[...] * (1.0 + sc_ref[...])) + sh_ref[...]).astype(_MX)
        hc_ref[...] = hc
        kvc = _dot(hc, w_ref[...])
        kvc_ref[...] = kvc
        kc = _headnorm(kvc[:, :128], knw_ref[...], g_ref[...])
        kc_ref[...] = kc.astype(_MX)
        kcsw_ref[...] = pltpu.roll(kc, 64, 1).astype(_MX)
        vc = kvc[:, 128:]
        vc_ref[...] = vc.astype(_MX)
        vcsw_ref[...] = pltpu.roll(vc, 64, 1).astype(_MX)

    return pl.pallas_call(
        body, name="ctx_kv",
        in_specs=[_vspec()] * 7, out_specs=[_vspec()] * 6,
        out_shape=[jax.ShapeDtypeStruct((LC, 256), F32), jax.ShapeDtypeStruct((LC, D), _MX)]
        + [jax.ShapeDtypeStruct((LC, 128), _MX)] * 4,
        compiler_params=pltpu.CompilerParams(vmem_limit_bytes=VMEM_LIMIT),
    )(ctx, nw, scc, shc, winkv, knw2, gmat)


def _attn_mask(i, s):
    r = lax.broadcasted_iota(jnp.int32, (2 * TQ, LC + 3 * TQ), 0) % TQ
    cidx = lax.broadcasted_iota(jnp.int32, (2 * TQ, LC + 3 * TQ), 1)
    qpos = i * TQ + r
    kpos = (i - 1) * TQ + (cidx - LC)
    near = (jnp.abs(qpos - kpos) <= 128) & (kpos >= 0) & (kpos < s)
    return (cidx < LC) | near


def _attn_bias(i, s):
    return jnp.where(_attn_mask(i, s), 0.0, NEG)


def _attn_exp(sc, bias, sinkv):
    sc = sc + bias
    m = jnp.maximum(jnp.max(sc, axis=-1, keepdims=True), sinkv)
    ex = jnp.exp(sc - m)
    es = jnp.exp(sinkv - m)
    return ex, es, 1.0 / (jnp.sum(ex, axis=-1, keepdims=True) + es)


def _sink_rows(sink_ref, g, e):
    return jnp.concatenate([jnp.full((TQ, 1), sink_ref[4 * g + e], F32),
                            jnp.full((TQ, 1), sink_ref[4 * g + 2 + e], F32)], axis=0)


def _attn_fwd(q0, q1, kp, kswp, vp, vswp, kc, kcsw, vc, vcsw, sink, gather=()):
    s = q0.shape[0]
    nb = s // TQ

    def body(q0_ref, q1_ref, kp_ref, kswp_ref, vp_ref, vswp_ref, kc_ref, kcsw_ref, vc_ref, vcsw_ref, sink_ref,
             o_ref, ex_ref, stat_ref):
        i = pl.program_id(0)
        st = pl.multiple_of(i * TQ, TQ)
        kall = (jnp.concatenate([kc_ref[...], kp_ref[pl.ds(st, 3 * TQ), :]], axis=0),
                jnp.concatenate([kcsw_ref[...], kswp_ref[pl.ds(st, 3 * TQ), :]], axis=0))
        vall = (jnp.concatenate([vc_ref[...], vp_ref[pl.ds(st, 3 * TQ), :]], axis=0),
                jnp.concatenate([vcsw_ref[...], vswp_ref[pl.ds(st, 3 * TQ), :]], axis=0))
        bias = _attn_bias(i, s)
        lo = lax.broadcasted_iota(jnp.int32, (1, 128), 1) < 64
        qrefs = (q0_ref, q1_ref)
        out = [jnp.zeros((TQ, 128), F32) for _ in range(4)]
        combos = [(g, e) for g in range(2) for e in range(2)]
        scores = [_dot_nt(jnp.concatenate([qrefs[e][:, 256 * g:256 * g + 128],
                                           qrefs[e][:, 256 * g + 128:256 * g + 256]], axis=0),
                          kall[0 if e == g else 1]) for g, e in combos]
        parts = []
        for n, ((g, e), sc) in enumerate(zip(combos, scores)):
            ex, es, inv = _attn_exp(sc, bias, _sink_rows(sink_ref, g, e))
            parts.append((ex.astype(_MX), inv))
            ex_ref[n] = parts[-1][0]
            stat_ref[n] = jnp.where(lo, inv, es * inv)
        for (g, e), (ex, inv) in zip(combos, parts):
            me = lo if e == 0 else jnp.logical_not(lo)
            o2 = _dot(ex, vall[0 if e == g else 1]) * inv
            out[2 * g] = out[2 * g] + jnp.where(me, o2[:TQ], 0.0)
            out[2 * g + 1] = out[2 * g + 1] + jnp.where(me, o2[TQ:], 0.0)
        for j in range(4):
            o_ref[:, 128 * j:128 * (j + 1)] = out[j]

    full = lambda a: pl.BlockSpec(a.shape, lambda i: (0, 0))
    qs = pl.BlockSpec((TQ, AW), lambda i: (i, 0))
    res = _call_hosting_gather(
        body, gather, lambda: pl.program_id(0), nb, name="attn_fwd", grid=(nb,),
        in_specs=[qs, qs, full(kp), full(kp), full(kp), full(kp), full(kc), full(kc), full(kc), full(kc),
                  pl.BlockSpec(memory_space=pltpu.SMEM)],
        out_specs=[qs, pl.BlockSpec((None, 4, 2 * TQ, LC + 3 * TQ), lambda i: (i, 0, 0, 0)),
                   pl.BlockSpec((None, 4, 2 * TQ, 128), lambda i: (i, 0, 0, 0))],
        out_shape=[jax.ShapeDtypeStruct((s, AW), F32), jax.ShapeDtypeStruct((nb, 4, 2 * TQ, LC + 3 * TQ), _MX),
                   jax.ShapeDtypeStruct((nb, 4, 2 * TQ, 128), F32)],
        compiler_params=_cp("arbitrary"),
    )(q0, q1, kp, kswp, vp, vswp, kc, kcsw, vc, vcsw, sink, *gather)
    return res[:3], res[3:]


def _conv31(u0, cw32, cb):
    s = u0.shape[0]
    rch = 256

    def body(u_ref, w_ref, b_ref, o_ref, pad_ref):
        pad_ref[0:16, :] = jnp.zeros((16, 128), F32)
        pad_ref[s + 16:s + 32, :] = jnp.zeros((16, 128), F32)
        pad_ref[16:s + 16, :] = u_ref[...]
        for cidx in range(s // rch):
            base = cidx * rch
            acc = jnp.zeros((rch, 128), F32) + b_ref[...]
            for j in range(CK):
                acc = acc + w_ref[j:j + 1, :] * pad_ref[base + j + 1:base + j + 1 + rch, :]
            o_ref[base:base + rch, :] = acc

    return pl.pallas_call(
        body, name="conv31", grid=(CW // 128,),
        in_specs=[pl.BlockSpec((s, 128), lambda c: (0, c)), pl.BlockSpec((32, 128), lambda c: (0, c)),
                  pl.BlockSpec((1, 128), lambda c: (0, c))],
        out_specs=pl.BlockSpec((s, 128), lambda c: (0, c)),
        out_shape=jax.ShapeDtypeStruct((s, CW), F32),
        scratch_shapes=[pltpu.VMEM((s + 32, 128), F32)],
        compiler_params=_cp("arbitrary"),
    )(u0, cw32, cb)


def _ln_stats(u1):
    mu = jnp.mean(u1, axis=-1, keepdims=True)
    xc = u1 - mu
    rstd = lax.rsqrt(jnp.mean(xc * xc, axis=-1, keepdims=True) + EPS)
    return xc * rstd, rstd


def _mix_out(o, u1, cnw, cnb, wout, x, g1):
    s = x.shape[0]

    def body(o_ref, u1_ref, cnw_ref, cnb_ref, w_ref, x_ref, g1_ref, x1_ref, mix_ref, cat_ref):
        u2n, _ = _ln_stats(u1_ref[...])
        u2 = u2n * cnw_ref[...] + cnb_ref[...]
        u3 = u2 * _sig(u2)
        cat = jnp.concatenate([o_ref[...], u3], axis=1).astype(_MX)
        cat_ref[...] = cat
        mix = _dot(cat, w_ref[...])
        mix_ref[...] = mix
        x1_ref[...] = x_ref[...] + g1_ref[...] * mix

    row = lambda n: pl.BlockSpec((1, n), lambda i: (0, 0))
    tile = lambda n: pl.BlockSpec((TM, n), lambda i: (i, 0))
    return pl.pallas_call(
        body, name="mix_out", grid=(s // TM,),
        in_specs=[tile(AW), tile(CW), row(CW), row(CW), pl.BlockSpec((D, D), lambda i: (0, 0)), tile(D), row(D)],
        out_specs=[tile(D), tile(D), tile(D)],
        out_shape=[jax.ShapeDtypeStruct((s, D), F32), jax.ShapeDtypeStruct((s, D), F32),
                   jax.ShapeDtypeStruct((s, D), _MX)],
        compiler_params=_cp("arbitrary"),
    )(o, u1, cnw, cnb, wout, x, g1)


def _ffn_up(x1, nw, sc, sh, wup4, gather=()):
    s = x1.shape[0]
    tm = min(1024, s)

    def body(x_ref, nw_ref, sc_ref, sh_ref, w_ref, h2_ref, up_ref, h2s):
        @pl.when(pl.program_id(1) == 0)
        def _():
            xv = x_ref[...]
            r = lax.rsqrt(jnp.mean(xv * xv, axis=-1, keepdims=True) + EPS)
            h2 = (xv * r * (nw_ref[...] * (1.0 + sc_ref[...])) + sh_ref[...]).astype(_MX)
            h2s[...] = h2
            h2_ref[...] = h2
        up_ref[...] = _dot(h2s[...], w_ref[...]).astype(_MX)

    row = pl.BlockSpec((1, D), lambda i, j: (0, 0))
    res = _call_hosting_gather(
        body, gather, lambda: pl.program_id(0) * 4 + pl.program_id(1), 4 * (s // tm), name="ffn_up",
        grid=(s // tm, 4),
        in_specs=[pl.BlockSpec((tm, D), lambda i, j: (i, 0)), row, row, row,
                  pl.BlockSpec((None, D, FQ), lambda i, j: (j, 0, 0))],
        out_specs=[pl.BlockSpec((tm, D), lambda i, j: (i, 0)),
                   pl.BlockSpec((None, tm, FQ), lambda i, j: (j // 2, i, j % 2))],
        out_shape=[jax.ShapeDtypeStruct((s, D), _MX), jax.ShapeDtypeStruct((2, s, FH), _MX)],
        scratch_shapes=[pltpu.VMEM((tm, D), _MX)],
        compiler_params=_cp("arbitrary", "arbitrary"),
    )(x1, nw, sc, sh, wup4, *gather)
    return res[:2], res[2:]


def _fill_pad8(pad_ref, val, s):
    pad_ref[0:8, :] = jnp.zeros((8, 128), F32)
    pad_ref[s + 8:s + 16, :] = jnp.zeros((8, 128), F32)
    pad_ref[8:s + 8, :] = val


def _conv3_at(pad_ref, w_ref, half, base, rch):
    return (w_ref[half, 0:1, :] * pad_ref[base + 7:base + 7 + rch, :]
            + w_ref[half, 1:2, :] * pad_ref[base + 8:base + 8 + rch, :]
            + w_ref[half, 2:3, :] * pad_ref[base + 9:base + 9 + rch, :])


def _ffn_conv_act(up0, fcw, fcb):
    s = up0.shape[1]
    rch = 256

    def body(up_ref, w_ref, b_ref, act_ref, gv_ref, padg, padv):
        _fill_pad8(padg, up_ref[0].astype(F32), s)
        _fill_pad8(padv, up_ref[1].astype(F32), s)
        for cidx in range(s // rch):
            base = cidx * rch
            gate = _conv3_at(padg, w_ref, 0, base, rch) + b_ref[0]
            val = _conv3_at(padv, w_ref, 1, base, rch) + b_ref[1]
            act_ref[base:base + rch, :] = (gate * _sig(gate) * val).astype(_MX)
            gv_ref[0, base:base + rch, :] = gate.astype(_MX)
            gv_ref[1, base:base + rch, :] = val.astype(_MX)

    return pl.pallas_call(
        body, name="ffn_conv_act", grid=(NCH,),
        in_specs=[pl.BlockSpec((2, s, 128), lambda c: (0, 0, c)), pl.BlockSpec((2, 8, 128), lambda c: (0, 0, c)),
                  pl.BlockSpec((2, 1, 128), lambda c: (0, 0, c))],
        out_specs=[pl.BlockSpec((s, 128), lambda c: (0, c)), pl.BlockSpec((2, s, 128), lambda c: (0, 0, c))],
        out_shape=[jax.ShapeDtypeStruct((s, FH), _MX), jax.ShapeDtypeStruct((2, s, FH), _MX)],
        scratch_shapes=[pltpu.VMEM((s + 16, 128), F32)] * 2,
        compiler_params=_cp("arbitrary"),
    )(up0, fcw, fcb)


def _ffn_down(act, wdown, x1, g2, tgt):
    s = x1.shape[0]

    def body(act_ref, w_ref, x1_ref, g2_ref, tgt_ref, dy_ref, dact_ref, ddn_ref, vec_ref):
        @pl.when(pl.program_id(0) == 0)
        def _():
            vec_ref[...] = jnp.zeros((8, D), F32)
        dn = _dot(act_ref[...], w_ref[...])
        diff = x1_ref[...] + g2_ref[...] * dn - tgt_ref[...]
        dy = diff * (1.0 / D)
        dy_ref[...] = dy
        ddn = (dy * g2_ref[...]).astype(_MX)
        ddn_ref[...] = ddn
        dact_ref[...] = _dot_nt(ddn, w_ref[...]).astype(_MX)
        vec_ref[0:1, :] += _rowsum(dy * dn)
        vec_ref[1:2, :] += _rowsum(diff * diff)

    tile = lambda n: pl.BlockSpec((TM, n), lambda i: (i, 0))
    return pl.pallas_call(
        body, name="ffn_down", grid=(s // TM,),
        in_specs=[tile(FH), pl.BlockSpec((FH, D), lambda i: (0, 0)), tile(D), pl.BlockSpec((1, D), lambda i: (0, 0)),
                  tile(D)],
        out_specs=[tile(D), tile(FH), tile(D), pl.BlockSpec((8, D), lambda i: (0, 0))],
        out_shape=[jax.ShapeDtypeStruct((s, D), F32), jax.ShapeDtypeStruct((s, FH), _MX),
                   jax.ShapeDtypeStruct((s, D), _MX), jax.ShapeDtypeStruct((8, D), F32)],
        compiler_params=_cp("arbitrary"),
    )(act, wdown, x1, g2, tgt)


def _ffn_conv_bwd(up0, gv, dact, fcw):
    s = up0.shape[1]
    rch = 256

    def body(up_ref, gv_ref, da_ref, w_ref, dup_ref, gw_ref, dpg, dpv):
        for p in (dpg, dpv):
            p[0:8, :] = jnp.zeros((8, 128), F32)
            p[s + 8:s + 16, :] = jnp.zeros((8, 128), F32)
        for cidx in range(s // rch):
            rows = slice(cidx * rch, (cidx + 1) * rch)
            gate = gv_ref[0, rows, :].astype(F32)
            da = da_ref[rows, :].astype(F32)
            sg = _sig(gate)
            silu = gate * sg
            dpg[8 + cidx * rch:8 + (cidx + 1) * rch, :] = da * gv_ref[1, rows, :].astype(F32) * (sg + silu * (1.0 - sg))
            dpv[8 + cidx * rch:8 + (cidx + 1) * rch, :] = da * silu
        acc = [[jnp.zeros((1, 128), F32) for _ in range(4)] for _ in range(2)]
        for cidx in range(s // rch):
            base = cidx * rch
            for half, dp in enumerate((dpg, dpv)):
                shifted = [dp[base + 9 - j:base + 9 - j + rch, :] for j in range(3)]
                dup_ref[half, base:base + rch, :] = (w_ref[half, 0:1, :] * shifted[0] + w_ref[half, 1:2, :] * shifted[1]
                                                     + w_ref[half, 2:3, :] * shifted[2]).astype(_MX)
                xv = up_ref[half, base:base + rch, :].astype(F32)
                for j in range(3):
                    acc[half][j] = acc[half][j] + _rowsum(xv * shifted[j])
                acc[half][3] = acc[half][3] + _rowsum(shifted[1])
        for half in range(2):
            gw_ref[half] = jnp.zeros((8, 128), F32)
            for j in range(4):
                gw_ref[half, j:j + 1, :] = acc[half][j]

    both = pl.BlockSpec((2, s, 128), lambda c: (0, 0, c))
    return pl.pallas_call(
        body, name="ffn_conv_bwd", grid=(NCH,),
        in_specs=[both, both, pl.BlockSpec((s, 128), lambda c: (0, c)), pl.BlockSpec((2, 8, 128), lambda c: (0, 0, c))],
        out_specs=[both, pl.BlockSpec((2, 8, 128), lambda c: (0, 0, c))],
        out_shape=[jax.ShapeDtypeStruct((2, s, FH), _MX), jax.ShapeDtypeStruct((2, 8, FH), F32)],
        scratch_shapes=[pltpu.VMEM((s + 16, 128), F32)] * 2,
        compiler_params=_cp("arbitrary"),
    )(up0, gv, dact, fcw)


def _tn_matmul(a, b, tm, tn, name, b_split=False, by_chip=False, init=None):
    s, m = a.shape
    n = 2 * b.shape[2] if b_split else b.shape[1]
    ts = min(1024, s)
    nsteps = s // ts
    npb = (n // 2) // tn if b_split else None

    def body(*refs):
        if init is None:
            a_ref, b_ref, o_ref, acc = refs
        else:
            a_ref, b_ref, i_ref, o_ref, acc = refs
        k = pl.program_id(2)

        @pl.when(k == 0)
        def _():
            acc[...] = jnp.zeros((tm, tn), F32)
            if init is not None:
                acc[:, 512:768] = i_ref[...]
        acc[...] += _dot_tn(a_ref[...], b_ref[...])

        @pl.when(k == nsteps - 1)
        def _():
            o_ref[...] = acc[...].astype(_MX)

    if b_split:
        bspec = pl.BlockSpec((None, ts, tn), lambda i, j, k: (j // npb, k, j % npb))
    else:
        bspec = pl.BlockSpec((ts, tn), lambda i, j, k: (k, j))
    in_specs = [pl.BlockSpec((ts, tm), lambda i, j, k: (k, i)), bspec]
    args = [a, b]
    if init is not None:
        in_specs.append(pl.BlockSpec((tm, 256), lambda i, j, k: (i, 0)))
        args.append(init)
    if by_chip:
        out_spec = pl.BlockSpec((None, tm, tn), lambda i, j, k: (j, i, 0))
        out_shape = jax.ShapeDtypeStruct((n // tn, m, tn), _MX)
    else:
        out_spec = pl.BlockSpec((tm, tn), lambda i, j, k: (i, j))
        out_shape = jax.ShapeDtypeStruct((m, n), _MX)
    return pl.pallas_call(
        body, name=name, grid=(m // tm, n // tn, nsteps),
        in_specs=in_specs, out_specs=out_spec, out_shape=out_shape,
        scratch_shapes=[pltpu.VMEM((tm, tn), F32)],
        compiler_params=_cp("arbitrary", "arbitrary", "arbitrary"),
    )(*args)


def _ffn_up_bwd(dup, wup4, x1, dy, mix, nw, sc, g1):
    s = x1.shape[0]
    nk = 2

    def body(dup_ref, w_ref, x1_ref, dy_ref, mix_ref, nw_ref, sc_ref, g1_ref, dx1_ref, dmix_ref, vec_ref, acc):
        i, k = pl.program_id(0), pl.program_id(1)

        @pl.when(k == 0)
        def _():
            acc[...] = jnp.zeros((TM, D), F32)

        @pl.when((k == 0) & (i == 0))
        def _():
            vec_ref[...] = jnp.zeros((8, D), F32)
        acc[...] += _dot_nt(dup_ref[:, :FQ], w_ref[0]) + _dot_nt(dup_ref[:, FQ:], w_ref[1])

        @pl.when(k == nk - 1)
        def _():
            dh = acc[...]
            xv = x1_ref[...]
            r = lax.rsqrt(jnp.mean(xv * xv, axis=-1, keepdims=True) + EPS)
            xn = xv * r
            nwv, scv = nw_ref[...], sc_ref[...]
            vec_ref[0:1, :] += _rowsum(dh)
            vec_ref[1:2, :] += _rowsum(dh * xn) * nwv
            vec_ref[2:3, :] += _rowsum(dh * xn) * (1.0 + scv)
            dxn = dh * (nwv * (1.0 + scv))
            dx1 = dy_ref[...] + r * (dxn - xn * jnp.mean(dxn * xn, axis=-1, keepdims=True))
            dx1_ref[...] = dx1
            vec_ref[3:4, :] += _rowsum(dx1 * mix_ref[...])
            dmix_ref[...] = (dx1 * g1_ref[...]).astype(_MX)

    tile = pl.BlockSpec((TM, D), lambda i, k: (i, 0))
    row = pl.BlockSpec((1, D), lambda i, k: (0, 0))
    return pl.pallas_call(
        body, name="ffn_up_bwd", grid=(s // TM, nk),
        in_specs=[pl.BlockSpec((None, TM, FH), lambda i, k: (k, i, 0)),
                  pl.BlockSpec((2, D, FQ), lambda i, k: (k, 0, 0)), tile, tile, tile, row, row, row],
        out_specs=[tile, tile, pl.BlockSpec((8, D), lambda i, k: (0, 0))],
        out_shape=[jax.ShapeDtypeStruct((s, D), F32), jax.ShapeDtypeStruct((s, D), _MX),
                   jax.ShapeDtypeStruct((8, D), F32)],
        scratch_shapes=[pltpu.VMEM((TM, D), F32)],
        compiler_params=_cp("arbitrary", "arbitrary"),
    )(dup, wup4, x1, dy, mix, nw, sc, g1)


def _mix_out_bwd(dmix, wout, u1, cnw, cnb, swap=()):
    s = u1.shape[0]

    def body(dm_ref, w_ref, u1_ref, cnw_ref, cnb_ref, do_ref, du1_ref, vec_ref):
        @pl.when(pl.program_id(0) == 0)
        def _():
            vec_ref[...] = jnp.zeros((8, CW), F32)
        dcat = _dot_nt(dm_ref[...], w_ref[...])
        do_ref[...] = dcat[:, :AW]
        du3 = dcat[:, AW:]
        u2n, rstd = _ln_stats(u1_ref[...])
        u2 = u2n * cnw_ref[...] + cnb_ref[...]
        sg = _sig(u2)
        du2 = du3 * sg * (1.0 + u2 * (1.0 - sg))
        vec_ref[0:1, :] += _rowsum(du2)
        vec_ref[1:2, :] += _rowsum(du2 * u2n)
        d2n = du2 * cnw_ref[...]
        du1_ref[...] = rstd * (d2n - jnp.mean(d2n, axis=-1, keepdims=True)
                               - u2n * jnp.mean(d2n * u2n, axis=-1, keepdims=True))

    row = lambda n: pl.BlockSpec((1, n), lambda i: (0, 0))
    tile = lambda n: pl.BlockSpec((TM, n), lambda i: (i, 0))
    res = _call_hosting_gather(
        body, swap, lambda: pl.program_id(0), s // TM, comm=_Swap, name="mix_out_bwd", grid=(s // TM,),
        in_specs=[tile(D), pl.BlockSpec((D, D), lambda i: (0, 0)), tile(CW), row(CW), row(CW)],
        out_specs=[tile(AW), tile(CW), pl.BlockSpec((8, CW), lambda i: (0, 0))],
        out_shape=[jax.ShapeDtypeStruct((s, AW), F32), jax.ShapeDtypeStruct((s, CW), F32),
                   jax.ShapeDtypeStruct((8, CW), F32)],
        compiler_params=_cp("arbitrary"),
    )(dmix, wout, u1, cnw, cnb, *swap)
    return res[:3], res[3:]


def _conv31_bwd(du1, u0, cw32, proj, exchange=()):
    s = u0.shape[0]
    rch = 128

    def body(d_ref, u_ref, w_ref, ga_ref, gb_ref, dga_ref, dgb_ref, gw_ref, padd):
        padd[0:16, :] = jnp.zeros((16, 128), F32)
        padd[s + 16:s + 32, :] = jnp.zeros((16, 128), F32)
        padd[16:s + 16, :] = d_ref[...]
        gw_ref[...] = jnp.zeros((32, 128), F32)
        gw_ref[CK:CK + 1, :] = _rowsum(d_ref[...])
        for cidx in range(s // rch):
            base = cidx * rch
            uv = u_ref[base:base + rch, :]
            du0 = jnp.zeros((rch, 128), F32)
            for j in range(CK):
                shifted = padd[base + 31 - j:base + 31 - j + rch, :]
                du0 = du0 + w_ref[j:j + 1, :] * shifted
                gw_ref[j:j + 1, :] += _rowsum(uv * shifted)
            sg = _sig(gb_ref[base:base + rch, :].astype(F32))
            ga = ga_ref[base:base + rch, :].astype(F32)
            dga_ref[base:base + rch, :] = (du0 * sg).astype(_MX)
            dgb_ref[base:base + rch, :] = (du0 * ga * sg * (1.0 - sg)).astype(_MX)

    blk = lambda off: pl.BlockSpec((s, 128), lambda c: (0, c + off))
    res = _call_hosting_gather(
        body, exchange, lambda: pl.program_id(0), CW // 128, comm=_Exchange, name="conv31_bwd", grid=(CW // 128,),
        in_specs=[blk(0), blk(0), pl.BlockSpec((32, 128), lambda c: (0, c)), blk(6), blk(10)],
        out_specs=[blk(0), blk(0), pl.BlockSpec((32, 128), lambda c: (0, c))],
        out_shape=[jax.ShapeDtypeStruct((s, CW), _MX), jax.ShapeDtypeStruct((s, CW), _MX),
                   jax.ShapeDtypeStruct((32, CW), F32)],
        scratch_shapes=[pltpu.VMEM((s + 32, 128), F32)],
        compiler_params=_cp("arbitrary"),
    )(du1, u0, cw32, proj, proj, *exchange)
    return res[:3], res[3:]


def _attn_bwd(q0, q1, kp, kswp, vp, vswp, kc, kcsw, vc, vcsw, ex, stat, o, do, exchange=()):
    s = q0.shape[0]
    nb = s // TQ
    ne = len(exchange)

    def body(*refs):
        (q0_ref, q1_ref, kp_ref, kswp_ref, vp_ref, vswp_ref, kc_ref, kcsw_ref, vc_ref, vcsw_ref, ex_ref, stat_ref,
         o_ref, do_ref) = refs[:14]
        dq_ref, dk_ref, dv_ref, dkc_ref, dvc_ref, dsink_ref = refs[14 + ne:20 + ne]
        i = pl.program_id(0)
        if ne:
            exch = _Exchange(refs[14:14 + ne], refs[20 + ne:20 + 2 * ne], refs[20 + 2 * ne], refs[21 + 2 * ne])
            pl.when(i == 0)(exch.start)

        @pl.when(i == 0)
        def _():
            dk_ref[...] = jnp.zeros((s + 2 * TQ, 128), F32)
            dv_ref[...] = jnp.zeros((s + 2 * TQ, 128), F32)
            dkc_ref[...] = jnp.zeros((LC, 128), F32)
            dvc_ref[...] = jnp.zeros((LC, 128), F32)
            dsink_ref[...] = jnp.zeros((8, 128), F32)
        st = pl.multiple_of(i * TQ, TQ)
        kall = (jnp.concatenate([kc_ref[...], kp_ref[pl.ds(st, 3 * TQ), :]], axis=0),
                jnp.concatenate([kcsw_ref[...], kswp_ref[pl.ds(st, 3 * TQ), :]], axis=0))
        vall = (jnp.concatenate([vc_ref[...], vp_ref[pl.ds(st, 3 * TQ), :]], axis=0),
                jnp.concatenate([vcsw_ref[...], vswp_ref[pl.ds(st, 3 * TQ), :]], axis=0))
        lo = lax.broadcasted_iota(jnp.int32, (1, 128), 1) < 64
        qrefs = (q0_ref, q1_ref)
        dq = [jnp.zeros((TQ, 128), F32) for _ in range(4)]
        dkt = jnp.zeros((128, LC + 3 * TQ), F32)
        dvt = jnp.zeros((128, LC + 3 * TQ), F32)
        combos = [(g, e) for g in range(2) for e in range(2)]

        def stacked(ref, g):
            return jnp.concatenate([ref[:, 256 * g:256 * g + 128], ref[:, 256 * g + 128:256 * g + 256]], axis=0)

        for n, (g, e) in enumerate(combos):
            me = lo if e == 0 else jnp.logical_not(lo)
            sw = 0 if e == g else 1
            qm = stacked(qrefs[e], g)
            p = ex_ref[n].astype(F32) * stat_ref[n, :, 0:1]
            dom = jnp.where(me, stacked(do_ref, g), 0.0)
            dd = jnp.sum(dom * stacked(o_ref, g), axis=-1, keepdims=True)
            domx = dom.astype(_MX)
            sd = stat_ref[n, :, 64:65] * dd
            dsink_ref[4 * g + e:4 * g + e + 1, :] -= jnp.sum(sd[:TQ], axis=0, keepdims=True)
            dsink_ref[4 * g + 2 + e:4 * g + 3 + e, :] -= jnp.sum(sd[TQ:], axis=0, keepdims=True)
            ds = (p * (_dot_nt(domx, vall[sw]) - dd)).astype(_MX)
            dq2 = _dot(ds, kall[sw])
            dq[2 * g] = dq[2 * g] + jnp.where(me, dq2[:TQ], 0.0)
            dq[2 * g + 1] = dq[2 * g + 1] + jnp.where(me, dq2[TQ:], 0.0)
            dk2 = _dot_tn(qm, ds)
            dv2 = _dot_tn(domx, p.astype(_MX))
            if sw:
                dk2 = pltpu.roll(dk2, 64, 0)
                dv2 = pltpu.roll(dv2, 64, 0)
            dkt = dkt + dk2
            dvt = dvt + dv2
        dkt = dkt.T
        dvt = dvt.T
        for j in range(4):
            dq_ref[:, 128 * j:128 * (j + 1)] = dq[j] * 0.125
        dkc_ref[...] += dkt[:LC]
        dvc_ref[...] += dvt[:LC]
        dk_ref[pl.ds(st, 3 * TQ), :] += dkt[LC:]
        dv_ref[pl.ds(st, 3 * TQ), :] += dvt[LC:]
        if ne:
            pl.when(i == nb - 1)(exch.finish)

    full = lambda a: pl.BlockSpec(a.shape, lambda i: (0, 0))
    fs = lambda r: pl.BlockSpec((r, 128), lambda i: (0, 0))
    qs = pl.BlockSpec((TQ, AW), lambda i: (i, 0))
    res = pl.pallas_call(
        body, name="attn_bwd", grid=(nb,),
        in_specs=[qs, qs, full(kp), full(kp), full(kp), full(kp), full(kc), full(kc), full(kc), full(kc),
                  pl.BlockSpec((None, 4, 2 * TQ, LC + 3 * TQ), lambda i: (i, 0, 0, 0)),
                  pl.BlockSpec((None, 4, 2 * TQ, 128), lambda i: (i, 0, 0, 0)), qs, qs] + _any_specs(ne),
        out_specs=[qs, fs(s + 2 * TQ), fs(s + 2 * TQ), fs(LC), fs(LC), fs(8)] + _any_specs(ne),
        out_shape=[jax.ShapeDtypeStruct((s, AW), F32), jax.ShapeDtypeStruct((s + 2 * TQ, 128), F32),
                   jax.ShapeDtypeStruct((s + 2 * TQ, 128), F32), jax.ShapeDtypeStruct((LC, 128), F32),
                   jax.ShapeDtypeStruct((LC, 128), F32), jax.ShapeDtypeStruct((8, 128), F32)]
        + _Exchange.out_shapes(exchange),
        scratch_shapes=_Exchange.semaphores(ne) if ne else [],
        compiler_params=_cp("arbitrary"),
    )(q0, q1, kp, kswp, vp, vswp, kc, kcsw, vc, vcsw, ex, stat, o, do, *exchange)
    return res[:6], res[6:]


def _mix_in_bwd(dq, dk, dv, dga, dgb, proj, x, dx1, win, nw, sc, qnw2, knw2, cos, sa, sb, gmat):
    s = x.shape[0]

    def body(dq_ref, dk_ref, dv_ref, dga_ref, dgb_ref, proj_ref, x_ref, dx1_ref, win_ref, nw_ref, sc_ref,
             qnw_ref, knw_ref, cos_ref, sa_ref, sb_ref, g_ref, gx_ref, dproj_ref, vec_ref):
        @pl.when(pl.program_id(0) == 0)
        def _():
            vec_ref[...] = jnp.zeros((8, D), F32)
        cs, sav, sbv, g = cos_ref[...], sa_ref[...], sb_ref[...], g_ref[...]
        gq = jnp.zeros((1, 128), F32)
        for j in range(4):
            dqn = _rope_bwd(dq_ref[:, 128 * j:128 * (j + 1)], cs, sav, sbv)
            dch, gw = _headnorm_bwd(proj_ref[:, 128 * j:128 * (j + 1)].astype(F32), dqn, qnw_ref[...], g)
            dproj_ref[:, 128 * j:128 * (j + 1)] = dch.astype(_MX)
            gq = gq + _rowsum(gw)
        dkn = _rope_bwd(dk_ref[...], cs, sav, sbv)
        dch, gw = _headnorm_bwd(proj_ref[:, 512:640].astype(F32), dkn, knw_ref[...], g)
        dproj_ref[:, 512:640] = dch.astype(_MX)
        dproj_ref[:, 640:768] = dv_ref[...].astype(_MX)
        dproj_ref[:, 768:1280] = dga_ref[...]
        dproj_ref[:, 1280:1792] = dgb_ref[...]
        vec_ref[3:4, 0:128] += gq
        vec_ref[3:4, 128:256] += _rowsum(gw)
        dh = _dot_nt(dproj_ref[...], win_ref[...])
        xv = x_ref[...]
        r = lax.rsqrt(jnp.mean(xv * xv, axis=-1, keepdims=True) + EPS)
        xn = xv * r
        nwv, scv = nw_ref[...], sc_ref[...]
        vec_ref[0:1, :] += _rowsum(dh)
        vec_ref[1:2, :] += _rowsum(dh * xn) * nwv
        vec_ref[2:3, :] += _rowsum(dh * xn) * (1.0 + scv)
        dxn = dh * (nwv * (1.0 + scv))
        gx_ref[...] = dx1_ref[...] + r * (dxn - xn * jnp.mean(dxn * xn, axis=-1, keepdims=True))

    row = lambda n: pl.BlockSpec((1, n), lambda i: (0, 0))
    tile = lambda n: pl.BlockSpec((TM, n), lambda i: (i, 0))
    return pl.pallas_call(
        body, name="mix_in_bwd", grid=(s // TM,),
        in_specs=[tile(AW), tile(128), tile(128), tile(CW), tile(CW), tile(INW), tile(D), tile(D),
                  pl.BlockSpec((D, INW), lambda i: (0, 0)), row(D), row(D), row(128), row(128),
                  tile(128), tile(128), tile(128), pl.BlockSpec((128, 128), lambda i: (0, 0))],
        out_specs=[tile(D), tile(INW), pl.BlockSpec((8, D), lambda i: (0, 0))],
        out_shape=[jax.ShapeDtypeStruct((s, D), F32), jax.ShapeDtypeStruct((s, INW), _MX),
                   jax.ShapeDtypeStruct((8, D), F32)],
        compiler_params=_cp("arbitrary"),
    )(dq, dk, dv, dga, dgb, proj, x, dx1, win, nw, sc, qnw2, knw2, cos, sa, sb, gmat)


def _ctx_bwd(ctx, nw, scc, winkv, kvc, hc, dkc, dvc, knw2, gmat):
    def body(ctx_ref, nw_ref, sc_ref, w_ref, kvc_ref, hc_ref, dkc_ref, dvc_ref, knw_ref, g_ref, gw_ref, vec_ref):
        dkr, gk = _headnorm_bwd(kvc_ref[:, 0:128], dkc_ref[...], knw_ref[...], g_ref[...])
        dkv = jnp.concatenate([dkr, dvc_ref[...]], axis=1).astype(_MX)
        gw_ref[...] = _dot_tn(hc_ref[...], dkv)
        dh = _dot_nt(dkv, w_ref[...])
        cv = ctx_ref[...]
        r = lax.rsqrt(jnp.mean(cv * cv, axis=-1, keepdims=True) + EPS)
        cn = cv * r
        vec_ref[...] = jnp.zeros((8, D), F32)
        vec_ref[0:1, :] = _rowsum(dh)
        vec_ref[1:2, :] = _rowsum(dh * cn) * nw_ref[...]
        vec_ref[2:3, :] = _rowsum(dh * cn) * (1.0 + sc_ref[...])
        vec_ref[3:4, 0:128] = _rowsum(gk)

    return pl.pallas_call(
        body, name="ctx_bwd", in_specs=[_vspec()] * 10, out_specs=[_vspec()] * 2,
        out_shape=[jax.ShapeDtypeStruct((D, 256), F32), jax.ShapeDtypeStruct((8, D), F32)],
        compiler_params=pltpu.CompilerParams(vmem_limit_bytes=VMEM_LIMIT),
    )(ctx, nw, scc, winkv, kvc, hc, dkc, dvc, knw2, gmat)


def _mod_fwd(cs, wmod, bloc):
    def body(c_ref, w_ref, b_ref, o_ref):
        cv = c_ref[...]
        o_ref[...] = _dot((cv * _sig(cv)).astype(_MX), w_ref[...].astype(_MX)) + b_ref[...]

    return pl.pallas_call(
        body, name="mod_fwd", in_specs=[_vspec()] * 3, out_specs=_vspec(),
        out_shape=jax.ShapeDtypeStruct((16, wmod.shape[1]), F32),
        compiler_params=pltpu.CompilerParams(vmem_limit_bytes=VMEM_LIMIT),
    )(cs, wmod, bloc)


def _mod_bwd(cs, dmod, wmod):
    def body(c_ref, d_ref, w_ref, gw_ref, part_ref):
        cv = c_ref[...]
        sl = (cv * _sig(cv)).astype(_MX)
        dm = d_ref[...].astype(_MX)
        gw_ref[...] = _dot_tn(sl, dm)
        part_ref[...] = _dot_nt(dm, w_ref[...].astype(_MX))

    return pl.pallas_call(
        body, name="mod_bwd", in_specs=[_vspec()] * 3, out_specs=[_vspec()] * 2,
        out_shape=[jax.ShapeDtypeStruct(wmod.shape, F32), jax.ShapeDtypeStruct((16, D), F32)],
        compiler_params=pltpu.CompilerParams(vmem_limit_bytes=VMEM_LIMIT),
    )(cs, dmod, wmod)


def _sum_leading(a, name):
    n = a.shape[0]

    def body(a_ref, o_ref):
        acc = a_ref[0]
        for k in range(1, n):
            acc = acc + a_ref[k]
        o_ref[...] = acc

    return pl.pallas_call(
        body, name=name, in_specs=[_vspec()], out_specs=_vspec(),
        out_shape=jax.ShapeDtypeStruct(a.shape[1:], F32),
        compiler_params=pltpu.CompilerParams(vmem_limit_bytes=VMEM_LIMIT),
    )(a)


def _cctx_grad(parts, cc):
    def body(p_ref, c_ref, o_ref):
        acc = p_ref[0, 8:9, :]
        for k in range(1, 4):
            acc = acc + p_ref[k, 8:9, :]
        cv = c_ref[...]
        sg = _sig(cv)
        o_ref[...] = acc * (sg * (1.0 + cv * (1.0 - sg)))

    return pl.pallas_call(
        body, name="cctx_grad", in_specs=[_vspec()] * 2, out_specs=_vspec(),
        out_shape=jax.ShapeDtypeStruct((1, D), F32),
    )(parts, cc)


def _adam_math(w, g, m, v):
    mn = ADAM_B1 * m + (1.0 - ADAM_B1) * g
    vn = ADAM_B2 * v + (1.0 - ADAM_B2) * (g * g)
    mh = mn / (1.0 - ADAM_B1 ** ADAM_STEP)
    vh = vn / (1.0 - ADAM_B2 ** ADAM_STEP)
    delta = -ADAM_LR * (mh / (jnp.sqrt(vh) + ADAM_EPS) + ADAM_WD * w)
    return delta, mn, vn


def _adam_big(w, g, m, v, name):
    r, n = w.shape
    tr = 256 if r % 256 == 0 else 64

    def body(w_ref, g_ref, m_ref, v_ref, d_ref, mo_ref, vo_ref):
        d, mn, vn = _adam_math(w_ref[...], g_ref[...], m_ref[...], v_ref[...])
        d_ref[...] = d
        mo_ref[...] = mn
        vo_ref[...] = vn

    spec = pl.BlockSpec((tr, n), lambda i: (i, 0))
    return pl.pallas_call(
        body, name=name, grid=(r // tr,), in_specs=[spec] * 4, out_specs=[spec] * 3,
        out_shape=[jax.ShapeDtypeStruct((r, n), F32)] * 3,
        compiler_params=_cp("arbitrary"),
    )(w, g, m, v)


def _adam_small(ws, gs, ms, vs):
    n = len(ws)

    def body(*refs):
        ins, outs = refs[:4 * n], refs[4 * n:]
        for k in range(n):
            d, mn, vn = _adam_math(ins[k][...], ins[n + k][...], ins[2 * n + k][...], ins[3 * n + k][...])
            outs[k][...] = d
            outs[n + k][...] = mn
            outs[2 * n + k][...] = vn

    shapes = [jax.ShapeDtypeStruct(w.shape, F32) for w in ws]
    res = pl.pallas_call(
        body, name="adam_small", in_specs=[_vspec()] * (4 * n), out_specs=[_vspec()] * (3 * n),
        out_shape=shapes * 3,
    )(*ws, *gs, *ms, *vs)
    return res[:n], res[n:2 * n], res[2 * n:]


def _pair_sum(grads, from_sib, core, name):
    n = len(grads)

    def body(c_ref, *refs):
        for w in range(n):
            a_ref, b_ref, o_ref = refs[w], refs[n + w], refs[2 * n + w]
            o_ref[...] = (a_ref[...].astype(F32) + b_ref[...].astype(F32)).astype(_MX)

    halves = [(None, g.shape[1] // 2, g.shape[2]) for g in grads]
    return pl.pallas_call(
        body, name=name,
        grid_spec=pltpu.PrefetchScalarGridSpec(
            num_scalar_prefetch=1, grid=(4,),
            in_specs=[pl.BlockSpec(h, lambda j, c: (j, c[0], 0)) for h in halves]
            + [pl.BlockSpec(h, lambda j, c: (j, 0, 0)) for h in halves],
            out_specs=[pl.BlockSpec(h, lambda j, c: (j, 0, 0)) for h in halves]),
        out_shape=[jax.ShapeDtypeStruct(p.shape, _MX) for p in from_sib],
        compiler_params=_cp("arbitrary"),
    )(core.reshape(1), *grads, *from_sib)


def _chip_sum(parts, arrived, chip, core):
    n = len(parts)

    def body(s_ref, *refs):
        for w in range(n):
            own_ref, p_ref, o_ref = refs[w], refs[n + w], refs[2 * n + w]
            acc = own_ref[...].astype(F32)
            for k in range(3):
                acc = acc + p_ref[k].astype(F32)
            o_ref[...] = acc

    blk = [(p.shape[1] // 2, p.shape[2]) for p in parts]
    return pl.pallas_call(
        body, name="grad_chip_sum",
        grid_spec=pltpu.PrefetchScalarGridSpec(
            num_scalar_prefetch=1, grid=(2,),
            in_specs=[pl.BlockSpec((None,) + b, lambda i, s: (s[0], i, 0)) for b in blk]
            + [pl.BlockSpec((3,) + b, lambda i, s: (0, i, 0)) for b in blk],
            out_specs=[pl.BlockSpec(b, lambda i, s: (2 * s[1] + i, 0)) for b in blk]),
        out_shape=[jax.ShapeDtypeStruct((2 * p.shape[1], p.shape[2]), F32) for p in parts],
        compiler_params=_cp("arbitrary"),
    )(jnp.stack([chip, core]), *parts, *arrived)


def _cast_shards(ws, chip):
    n = len(ws)

    def body(s_ref, *refs):
        for w in range(n):
            refs[n + w][...] = refs[w][...].astype(_MX)

    blk = [(a.shape[0] // 2, a.shape[1]) for a in ws]
    return pl.pallas_call(
        body, name="cast_shards",
        grid_spec=pltpu.PrefetchScalarGridSpec(
            num_scalar_prefetch=1, grid=(2,),
            in_specs=[pl.BlockSpec(b, lambda i, s: (i, 0)) for b in blk],
            out_specs=[pl.BlockSpec((None,) + b, lambda i, s: (s[0], i, 0)) for b in blk]),
        out_shape=[jax.ShapeDtypeStruct((4,) + a.shape, _MX) for a in ws],
        compiler_params=_cp("arbitrary"),
    )(chip.reshape(1), *ws)


def _position():
    x, y, c = lax.axis_index("x"), lax.axis_index("y"), lax.axis_index("c")
    return x, y, c


def _small_allgather(blk, name, exchange=(), share=(), gather=()):
    m_per, n = blk.shape
    assert not (share and gather)
    share = tuple(share) + tuple(gather)
    ne, ns = len(exchange), len(share)

    def body(*refs):
        x_ref = refs[0]
        out_ref = refs[1 + ne + ns]
        send_sems, recv_sems = refs[2 + 2 * ne + 2 * ns], refs[3 + 2 * ne + 2 * ns]
        extra = refs[4 + 2 * ne + 2 * ns:]
        x, y, c = _position()
        me, sibling = (x, y, c), (x, y, 1 - c)
        chips = [(1 - x, y), (x, 1 - y), (1 - x, 1 - y)]
        if ne:
            exch = _Exchange(refs[1:1 + ne], refs[2 + ne + ns:2 + 2 * ne + ns], extra[0], extra[1])
            exch.start()
        if gather:
            gat = _Gather(refs[1 + ne:1 + ne + ns], refs[2 + 2 * ne + ns:2 + 2 * ne + 2 * ns],
                          extra[2 * bool(ne)], extra[2 * bool(ne) + 1], [b.shape[1] for b in gather])
            gat.start()
        elif ns:
            halves = _share_copies(refs[1 + ne:1 + ne + ns], refs[2 + 2 * ne + ns:2 + 2 * ne + 2 * ns],
                                   extra[2 * bool(ne)], extra[2 * bool(ne) + 1], [b.shape[0] for b in share])
            for cp in halves:
                cp.start()

        def rows(px, py, pc):
            return out_ref.at[pl.ds(pl.multiple_of((4 * px + 2 * py + pc) * m_per, 8), m_per), :]

        def copy(k, block, to, src=None):
            return pltpu.make_async_remote_copy(
                src_ref=rows(*block) if src is None else src, dst_ref=rows(*block),
                send_sem=send_sems.at[k], recv_sem=recv_sems.at[k], device_id=to, device_id_type=MESH)

        out_ref[pl.ds(pl.multiple_of((4 * x + 2 * y + c) * m_per, 8), m_per), :] = x_ref[...]
        first = [copy(0, me, sibling, src=x_ref)]
        first += [copy(1 + j, me, (*chip, c), src=x_ref) for j, chip in enumerate(chips)]
        for cp in first:
            cp.start()
        passed = [copy(4 + j, (*chip, c), sibling) for j, chip in enumerate(chips)]
        for j, chip in enumerate(chips):
            copy(1 + j, (*chip, c), me).wait_recv()
            passed[j].start()
        copy(0, sibling, me).wait_recv()
        for j, chip in enumerate(chips):
            copy(4 + j, (*chip, 1 - c), me).wait_recv()
        for cp in first + passed:
            cp.wait_send()
        if ne:
            exch.finish()
        if gather:
            gat.forward()
            gat.finish()
        elif ns:
            for cp in halves:
                cp.wait()

    res = pl.pallas_call(
        body, name=name,
        out_shape=[jax.ShapeDtypeStruct((8 * m_per, n), blk.dtype)] + _Exchange.out_shapes(exchange)
        + [jax.ShapeDtypeStruct(b.shape, b.dtype) for b in share],
        in_specs=[_vspec()] + _any_specs(ne + ns), out_specs=[_vspec()] + _any_specs(ne + ns),
        input_output_aliases={1 + ne + w: 1 + ne + w for w in range(ns)},
        scratch_shapes=[pltpu.SemaphoreType.DMA((7,)), pltpu.SemaphoreType.DMA((7,))]
        + (_Exchange.semaphores(ne) if ne else [])
        + (_Gather.semaphores(ns) if gather else
           [pltpu.SemaphoreType.DMA((ns,)), pltpu.SemaphoreType.DMA((ns,))] if ns else []),
    )(blk, *exchange, *share)
    return res[0], res[1:1 + ne], res[1 + ne:]


def _share_copies(src_refs, out_refs, send_sems, recv_sems, nrows):
    x, y, c = _position()
    return [pltpu.make_async_remote_copy(
        src_ref=_rows(src_refs[w], c, r), dst_ref=_rows(out_refs[w], c, r), send_sem=send_sems.at[w],
        recv_sem=recv_sems.at[w], device_id=(x, y, 1 - c), device_id_type=MESH) for w, r in enumerate(nrows)]


def _any_specs(n):
    return [pl.BlockSpec(memory_space=pl.ANY)] * n


def _rows(ref, half, nrows):
    return ref.at[pl.ds(half * (nrows // 2), nrows // 2), :]


class _Gather:
    def __init__(self, src_refs, out_refs, send_sems, recv_sems, nrows):
        x, y, c = _position()
        chip = 2 * x + y
        sibling = (x, y, 1 - c)

        def copy(k, src, dst, to):
            return pltpu.make_async_remote_copy(src_ref=src, dst_ref=dst, send_sem=send_sems.at[k],
                                                recv_sem=recv_sems.at[k], device_id=to, device_id_type=MESH)

        self.first, self.first_landed, self.passed, self.passed_landed = [], [], [], []
        for w, r in enumerate(nrows):
            for j, ch in enumerate([(1 - x, y), (x, 1 - y), (1 - x, 1 - y)]):
                theirs = _rows(out_refs[w].at[2 * ch[0] + ch[1]], c, r)
                other = _rows(out_refs[w].at[2 * ch[0] + ch[1]], 1 - c, r)
                self.first.append(copy(6 * w + j, _rows(src_refs[w].at[chip], c, r),
                                       _rows(out_refs[w].at[chip], c, r), (*ch, c)))
                self.first_landed.append(copy(6 * w + j, theirs, theirs, sibling))
                self.passed.append(copy(6 * w + 3 + j, theirs, theirs, sibling))
                self.passed_landed.append(copy(6 * w + 3 + j, other, other, sibling))

    in_place = True

    @staticmethod
    def semaphores(n):
        return [pltpu.SemaphoreType.DMA((6 * n,)), pltpu.SemaphoreType.DMA((6 * n,))]

    @staticmethod
    def results(bufs):
        return [jax.ShapeDtypeStruct(b.shape, b.dtype) for b in bufs]

    def start(self):
        for cp in self.first:
            cp.start()

    def forward(self):
        for landed, cp in zip(self.first_landed, self.passed):
            landed.wait_recv()
            cp.start()

    def finish(self):
        for landed in self.passed_landed:
            landed.wait_recv()
        for cp in self.first + self.passed:
            cp.wait_send()


class _Swap:
    in_place = False

    def __init__(self, grad_refs, out_refs, send_sems, recv_sems, nrows):
        x, y, c = _position()
        self.copies = [pltpu.make_async_remote_copy(
            src_ref=grad_refs[w].at[:, pl.ds((1 - c) * (r // 2), r // 2), :], dst_ref=out_refs[w],
            send_sem=send_sems.at[w], recv_sem=recv_sems.at[w], device_id=(x, y, 1 - c), device_id_type=MESH)
            for w, r in enumerate(nrows)]

    @staticmethod
    def semaphores(n):
        return [pltpu.SemaphoreType.DMA((n,)), pltpu.SemaphoreType.DMA((n,))]

    @staticmethod
    def results(grads):
        return [jax.ShapeDtypeStruct((4, g.shape[1] // 2, g.shape[2]), g.dtype) for g in grads]

    def start(self):
        for cp in self.copies:
            cp.start()

    def forward(self):
        pass

    def finish(self):
        for cp in self.copies:
            cp.wait()


def _grad_swap(grads, name):
    n = len(grads)

    def body(*refs):
        swap = _Swap(refs[:n], refs[n:2 * n], refs[2 * n], refs[2 * n + 1], [g.shape[1] for g in grads])
        swap.start()
        swap.finish()

    return pl.pallas_call(
        body, name=name, out_shape=_Swap.results(grads), in_specs=_any_specs(n), out_specs=_any_specs(n),
        scratch_shapes=_Swap.semaphores(n),
    )(*grads)


class _Exchange:
    in_place = False

    def __init__(self, part_refs, out_refs, send_sems, recv_sems, nrows=None):
        x, y, c = _position()
        self.copies = [
            pltpu.make_async_remote_copy(
                src_ref=part_refs[w].at[2 * ch[0] + ch[1]], dst_ref=out_refs[w].at[k], send_sem=send_sems.at[3 * w + k],
                recv_sem=recv_sems.at[3 * w + k], device_id=(*ch, c), device_id_type=MESH)
            for w in range(len(part_refs)) for k, ch in enumerate([(1 - x, y), (x, 1 - y), (1 - x, 1 - y)])]

    @staticmethod
    def semaphores(n):
        return [pltpu.SemaphoreType.DMA((3 * n,)), pltpu.SemaphoreType.DMA((3 * n,))]

    @staticmethod
    def out_shapes(parts):
        return [jax.ShapeDtypeStruct((3,) + p.shape[1:], p.dtype) for p in parts]

    results = out_shapes

    def start(self):
        for cp in self.copies:
            cp.start()

    def forward(self):
        pass

    def finish(self):
        for cp in self.copies:
            cp.wait()


def _rope_tables(s):
    rows = s // GRID_W
    inv = 10000.0 ** (-jnp.arange(0, 32, 2, dtype=F32) / 32.0)
    ang_r = jnp.arange(rows, dtype=F32)[:, None] * inv
    ang_c = jnp.arange(GRID_W, dtype=F32)[:, None] * inv
    lane = jnp.arange(128)
    first = (lane % 32) < 16
    by_row = (lane % 64) < 32

    def table(fn):
        tr = jnp.tile(fn(ang_r), (1, 8))[:, None, :]
        tc = jnp.tile(fn(ang_c), (1, 8))[None, :, :]
        return jnp.where(by_row, tr, tc).reshape(s, 128)

    cos, sin = table(jnp.cos), table(jnp.sin)
    return cos, jnp.where(first, -sin, 0.0), jnp.where(first, 0.0, sin)


def _local_step(x, ctx, tgt, mod, modc, nw1, win, qnw, knw, sink, cw, cb, cnw, cnb, wout4, nw2, wup, fcw, fcb, wdown4,
                core=None):
    s = x.shape[0]
    sh1, sc1, g1, sh2, sc2, g2 = [mod[:, D * k:D * (k + 1)] for k in range(6)]
    shc, scc = modc[:, :D], modc[:, D:2 * D]
    cos, sa, sb = _rope_tables(s)
    gi = jnp.arange(128) // 64
    gmat = (gi[:, None] == gi[None, :]).astype(_MX)
    qnw2, knw2 = jnp.tile(qnw, (1, 2)), jnp.tile(knw, (1, 2))
    cw32 = jnp.pad(cw, ((0, 1), (0, 0)))
    fcw2 = jnp.pad(fcw, ((0, 5), (0, 0))).reshape(8, 2, FH).transpose(1, 0, 2)
    fcb2 = fcb.reshape(2, 1, FH)
    winkv = win[:, 512:768]
    sinkv = sink.reshape(8)

    host = (lambda *bufs: ()) if core is None else (lambda *bufs: bufs)
    (proj, h, q0, q1, k, ksw, v, vsw, u0), got = _mix_in(x, nw1, sc1, sh1, win, qnw2, knw2, cos, sa, sb, gmat,
                                                        gather=host(wout4))
    wout = (got[0] if got else wout4).reshape(D, D)
    kvc, hc, kc, kcsw, vc, vcsw = _ctx_kv(ctx, nw1, scc, shc, winkv, knw2, gmat)
    padr = lambda a: jnp.pad(a, ((TQ, TQ), (0, 0)))
    kp, kswp, vp, vswp = padr(k), padr(ksw), padr(v), padr(vsw)
    (o, pex, pstat), got = _attn_fwd(q0, q1, kp, kswp, vp, vswp, kc, kcsw, vc, vcsw, sinkv, gather=host(wup))
    wup = got[0] if got else wup
    u1 = _conv31(u0, cw32, cb)
    x1, mix, cat = _mix_out(o, u1, cnw, cnb, wout, x, g1)
    (h2, up0), got = _ffn_up(x1, nw2, sc2, sh2, wup, gather=host(wdown4))
    wdown = (got[0] if got else wdown4).reshape(FH, D)
    act, gv = _ffn_conv_act(up0, fcw2, fcb2)
    dy, dact, ddn, vec_dn = _ffn_down(act, wdown, x1, g2, tgt)
    loss = (0.5 / D) * jnp.sum(vec_dn[1])

    dup, gfc = _ffn_conv_bwd(up0, gv, dact, fcw2)
    g_wdown = _tn_matmul(act, ddn, FQ, D, "gw_down").reshape(4, FH // 4, D)
    g_wup = _tn_matmul(h2, dup, D, FQ, "gw_up", b_split=True, by_chip=True)
    dx1, dmix, vec_up = _ffn_up_bwd(dup, wup, x1, dy, mix, nw2, sc2, g1)
    g_wout = _tn_matmul(cat, dmix, D, D, "gw_out").reshape(4, D // 4, D)
    late = [g_wout, g_wup, g_wdown]
    (do, du1, vec_ln), from_sib = _mix_out_bwd(dmix, wout, u1, cnw, cnb, swap=() if core is None else late)
    parts = () if core is None else _pair_sum(late, from_sib, core, "grad_pair_sum_late")
    (dga, dgb, gcw), arrived_a = _conv31_bwd(du1, u0, cw32, proj, exchange=parts[0:1] + parts[2:3])
    (dq, dkp, dvp, dkc, dvc, dsink), arrived_b = _attn_bwd(q0, q1, kp, kswp, vp, vswp, kc, kcsw, vc, vcsw, pex, pstat,
                                                           o, do, exchange=parts[1:2])
    if core is not None:
        g_wout, g_wup, g_wdown = zip(parts, [arrived_a[0], arrived_b[0], arrived_a[1]])
    gwinkv, vec_ctx = _ctx_bwd(ctx, nw1, scc, winkv, kvc, hc, dkc, dvc, knw2, gmat)
    gx, dproj, vec_in = _mix_in_bwd(dq, dkp[TQ:TQ + s], dvp[TQ:TQ + s], dga, dgb, proj, x, dx1, win, nw1, sc1,
                                    qnw2, knw2, cos, sa, sb, gmat)
    g_win = _tn_matmul(h, dproj, D, INW, "gw_in", init=gwinkv)

    g_qn = vec_in[3:4, 0:128]
    g_kn = vec_in[3:4, 128:256] + vec_ctx[3:4, 0:128]
    grads = dict(
        norm_mix_w=vec_in[2:3] + vec_ctx[2:3], w_in=g_win,
        q_norm_w=g_qn[:, :64] + g_qn[:, 64:], k_norm_w=g_kn[:, :64] + g_kn[:, 64:],
        sink_logit=dsink[:, 0].reshape(1, 8), conv_w=gcw[:CK], conv_b=gcw[CK:CK + 1],
        conv_norm_w=vec_ln[1:2], conv_norm_b=vec_ln[0:1], w_out=g_wout, norm_ffn_w=vec_up[2:3], w_up=g_wup,
        ffn_conv_w=gfc[:, 0:3].transpose(1, 0, 2).reshape(3, 2 * FH), ffn_conv_b=gfc[:, 3].reshape(1, 2 * FH),
        w_down=g_wdown)
    dmod = jnp.concatenate([vec_in[0:1], vec_in[1:2], vec_up[3:4], vec_up[0:1], vec_up[1:2], vec_dn[0:1]], axis=1)
    dmodc = jnp.concatenate([vec_ctx[0:1], vec_ctx[1:2]], axis=1)
    return loss, gx, grads, dmod, dmodc


_SMALL = ["c_ctx", "b_mod", "norm_mix_w", "q_norm_w", "k_norm_w", "sink_logit", "conv_w", "conv_b", "conv_norm_w",
          "conv_norm_b", "norm_ffn_w", "ffn_conv_w", "ffn_conv_b"]
_GATHERED = ["w_in", "w_out", "w_up", "w_down"]
_BIG = ["w_mod"] + _GATHERED
_ORDER = ["c_ctx", "w_mod", "b_mod", "norm_mix_w", "w_in", "q_norm_w", "k_norm_w", "sink_logit", "conv_w", "conv_b",
          "conv_norm_w", "conv_norm_b", "w_out", "norm_ffn_w", "w_up", "ffn_conv_w", "ffn_conv_b", "w_down"]
_PACK = [("norm_mix_w", 1), ("norm_ffn_w", 1), ("q_norm_w", 1), ("k_norm_w", 1), ("sink_logit", 1), ("conv_b", 1),
         ("conv_norm_w", 1), ("conv_norm_b", 1), ("ffn_conv_b", 6), ("conv_w", 16), ("ffn_conv_w", 17)]
_PACK_ROWS = 56


def _pack_rows(a, nrows):
    flat = a.reshape(-1)
    return jnp.pad(flat, (0, nrows * D - flat.shape[0])).reshape(nrows, D)


def kernel(x, c, ctx, c_ctx, w_mod, b_mod, norm_mix_w, w_in, q_norm_w, k_norm_w, sink_logit, conv_w, conv_b, conv_norm_w, conv_norm_b, w_out, norm_ffn_w, w_up, ffn_conv_w, ffn_conv_b, w_down, loss_target, m_c_ctx, m_w_mod, m_b_mod, m_norm_mix_w, m_w_in, m_q_norm_w, m_k_norm_w, m_sink_logit, m_conv_w, m_conv_b, m_conv_norm_w, m_conv_norm_b, m_w_out, m_norm_ffn_w, m_w_up, m_ffn_conv_w, m_ffn_conv_b, m_w_down, v_c_ctx, v_w_mod, v_b_mod, v_norm_mix_w, v_w_in, v_q_norm_w, v_k_norm_w, v_sink_logit, v_conv_w, v_conv_b, v_conv_norm_w, v_conv_norm_b, v_w_out, v_norm_ffn_w, v_w_up, v_ffn_conv_w, v_ffn_conv_b, v_w_down):
    w = dict(c_ctx=c_ctx.reshape(1, D), w_mod=w_mod[0], b_mod=b_mod, norm_mix_w=norm_mix_w, w_in=w_in[0],
             q_norm_w=q_norm_w, k_norm_w=k_norm_w, sink_logit=sink_logit, conv_w=conv_w[0], conv_b=conv_b,
             conv_norm_w=conv_norm_w, conv_norm_b=conv_norm_b, w_out=w_out[0], norm_ffn_w=norm_ffn_w, w_up=w_up[0],
             ffn_conv_w=ffn_conv_w[0], ffn_conv_b=ffn_conv_b, w_down=w_down[0])
    m = dict(c_ctx=m_c_ctx.reshape(1, D), w_mod=m_w_mod[0], b_mod=m_b_mod, norm_mix_w=m_norm_mix_w, w_in=m_w_in[0],
             q_norm_w=m_q_norm_w, k_norm_w=m_k_norm_w, sink_logit=m_sink_logit, conv_w=m_conv_w[0], conv_b=m_conv_b,
             conv_norm_w=m_conv_norm_w, conv_norm_b=m_conv_norm_b, w_out=m_w_out[0], norm_ffn_w=m_norm_ffn_w,
             w_up=m_w_up[0], ffn_conv_w=m_ffn_conv_w[0], ffn_conv_b=m_ffn_conv_b, w_down=m_w_down[0])
    v = dict(c_ctx=v_c_ctx.reshape(1, D), w_mod=v_w_mod[0], b_mod=v_b_mod, norm_mix_w=v_norm_mix_w, w_in=v_w_in[0],
             q_norm_w=v_q_norm_w, k_norm_w=v_k_norm_w, sink_logit=v_sink_logit, conv_w=v_conv_w[0], conv_b=v_conv_b,
             conv_norm_w=v_conv_norm_w, conv_norm_b=v_conv_norm_b, w_out=v_w_out[0], norm_ffn_w=v_norm_ffn_w,
             w_up=v_w_up[0], ffn_conv_w=v_ffn_conv_w[0], ffn_conv_b=v_ffn_conv_b, w_down=v_w_down[0])
    xi, yi, ci = _position()
    chip = 2 * xi + yi
    dev = 2 * chip + ci
    s = x.shape[1]
    ncol = w["w_mod"].shape[1]

    win_buf, wout_buf, wup_buf, wdown_buf = _cast_shards([w[n] for n in _GATHERED], chip)

    blk0 = jnp.concatenate([_pack_rows(w["conv_w"], 4), _pack_rows(w["ffn_conv_w"], 5), c,
                            jnp.zeros((6, D), F32)], axis=0)
    g0, _, (win4,) = _small_allgather(blk0, "gather_c_convw", gather=[win_buf])
    win = win4.transpose(1, 0, 2).reshape(D, INW)
    g0 = g0.reshape(8, 16, D)
    c_all = g0[:, 9, :]
    cs = jnp.concatenate([c_all, w["c_ctx"], jnp.zeros((7, D), F32)], axis=0)
    cw_full = jnp.concatenate([g0[2 * j, 0:4].reshape(-1)[:CK * 128].reshape(CK, 128) for j in range(4)], axis=1)
    fcw_full = jnp.concatenate([g0[2 * j, 4:9].reshape(-1)[:3 * 1408].reshape(3, 1408) for j in range(4)], axis=1)

    b_loc = lax.dynamic_slice(w["b_mod"], (0, chip * ncol), (1, ncol))
    modp = _mod_fwd(cs, w["w_mod"], b_loc)
    gm = _small_allgather(modp, "gather_mod")[0].reshape(8, 16, ncol)
    mod_all = jnp.concatenate([gm[2 * j] for j in range(4)], axis=1)
    mod = lax.dynamic_slice(mod_all, (dev, 0), (1, 6 * D))
    modc = mod_all[8:9]

    loss_loc, gx, gl, dmod, dmodc = _local_step(
        x[0], ctx[0], loss_target[0], mod, modc, w["norm_mix_w"], win, w["q_norm_w"], w["k_norm_w"], w["sink_logit"],
        cw_full, w["conv_b"], w["conv_norm_w"], w["conv_norm_b"], wout_buf, w["norm_ffn_w"], wup_buf, fcw_full,
        w["ffn_conv_b"], wdown_buf, core=ci)

    last = [gl["w_in"].reshape(D, 4, INW // 4).transpose(1, 0, 2)]
    parts = _pair_sum(last, _grad_swap(last, "grad_swap_in"), ci, "grad_pair_sum_in")

    pack = jnp.concatenate([dmod.reshape(6, D), dmodc.reshape(2, D)]
                           + [_pack_rows(gl[name], nr) for name, nr in _PACK], axis=0)
    pack = jnp.pad(pack, ((0, _PACK_ROWS - pack.shape[0]), (0, 0)))
    last_row = lax.broadcasted_iota(jnp.int32, pack.shape, 0) == _PACK_ROWS - 1
    pack = jnp.where(last_row, loss_loc, pack)
    gp, arrived, _ = _small_allgather(pack, "gather_small_grads", exchange=parts)
    gp = gp.reshape(8, _PACK_ROWS, D)
    late = [gl[n] for n in ("w_out", "w_up", "w_down")]
    shard_grads = _chip_sum([*parts, *[p for p, _ in late]], [*arrived, *[a for _, a in late]], chip, ci)
    tot = _sum_leading(gp, "sum_small_grads")
    loss = tot[_PACK_ROWS - 1, 0]
    g = {}
    r = 8
    for name, nr in _PACK:
        shape = gl[name].shape
        g[name] = tot[r:r + nr].reshape(-1)[:math.prod(shape)].reshape(shape)
        r += nr
    dmod_all = jnp.concatenate([gp[:, 0:6].reshape(8, 6 * D),
                                jnp.pad(tot[6:8].reshape(1, 2 * D), ((0, 0), (0, 4 * D))),
                                jnp.zeros((7, 6 * D), F32)], axis=0)
    g["b_mod"] = _sum_leading(dmod_all.reshape(16, 1, 6 * D), "grad_b_mod")
    g["w_mod"], part = _mod_bwd(cs, lax.dynamic_slice(dmod_all, (0, chip * ncol), (16, ncol)), w["w_mod"])
    cparts, _, (g["w_in"], g["w_out"], g["w_up"], g["w_down"]) = _small_allgather(part, "gather_cctx",
                                                                                 share=shard_grads)
    g["c_ctx"] = _cctx_grad(cparts.reshape(8, 16, D)[0::2], w["c_ctx"])
    g["conv_w"] = lax.dynamic_slice(g["conv_w"], (0, chip * 128), (CK, 128))
    g["ffn_conv_w"] = lax.dynamic_slice(g["ffn_conv_w"], (0, chip * 1408), (3, 1408))

    delta, new_m, new_v = {}, {}, {}
    for name in _BIG:
        delta[name], new_m[name], new_v[name] = _adam_big(w[name], g[name], m[name], v[name], "adam_" + name)
    ds, ms, vs = _adam_small([w[n] for n in _SMALL], [g[n] for n in _SMALL], [m[n] for n in _SMALL],
                             [v[n] for n in _SMALL])
    for k, name in enumerate(_SMALL):
        delta[name], new_m[name], new_v[name] = ds[k], ms[k], vs[k]

    def shaped(d, name):
        a = d[name]
        if name == "c_ctx":
            return a.reshape(D)
        if name in ("w_mod", "w_in", "w_out", "w_up", "w_down", "conv_w", "ffn_conv_w"):
            return a[None]
        return a

    outs = [loss, gx[None]]
    for d in (g, delta, new_m, new_v):
        outs += [shaped(d, name) for name in _ORDER]
    return tuple(outs)
```

```python
import functools
import math

import jax
import jax.numpy as jnp
from jax import lax
from jax.experimental import pallas as pl
from jax.experimental.pallas import tpu as pltpu

F32 = jnp.float32
_MX = jnp.bfloat16
EPS = 1e-6
NEG = -1e30
D = 1024
AW = 512
CW = 512
INW = 1792
FH = 2816
LC = 256
CK = 31
GRID_W = 64
TM = 512
TQ = 128
NCH = FH // 128
FQ = 2 * FH // 4
VMEM_LIMIT = 56 * 1024 * 1024
MESH = pl.DeviceIdType.MESH

ADAM_LR, ADAM_B1, ADAM_B2, ADAM_EPS, ADAM_WD, ADAM_STEP = 0.001, 0.9, 0.999, 1e-08, 0.01, 10


def _cp(*sem):
    return pltpu.CompilerParams(dimension_semantics=sem, vmem_limit_bytes=VMEM_LIMIT)


def _vspec():
    return pl.BlockSpec(memory_space=pltpu.VMEM)


def _sig(z):
    return 1.0 / (1.0 + jnp.exp(-z))


def _dot(a, b):
    return jnp.dot(a, b, preferred_element_type=F32)


def _dot_nt(a, b):
    return lax.dot_general(a, b, (((1,), (1,)), ((), ())), preferred_element_type=F32)


def _dot_tn(a, b):
    return lax.dot_general(a, b, (((0,), (0,)), ((), ())), preferred_element_type=F32)


def _gsum(v, g):
    hi = v.astype(_MX)
    lo = (v - hi.astype(F32)).astype(_MX)
    return _dot(hi, g) + _dot(lo, g)


def _headnorm(ch, w, g):
    r = lax.rsqrt(_gsum(ch * ch, g) * (1.0 / 64.0) + EPS)
    return ch * r * w


def _headnorm_bwd(ch, dy, w, g):
    r = lax.rsqrt(_gsum(ch * ch, g) * (1.0 / 64.0) + EPS)
    hat = ch * r
    dhat = dy * w
    dch = r * (dhat - hat * (_gsum(dhat * hat, g) * (1.0 / 64.0)))
    return dch, dy * hat


def _rope(u, cos, sa, sb):
    return u * cos + pltpu.roll(u, 112, 1) * sa + pltpu.roll(u, 16, 1) * sb


def _rope_bwd(d, cos, sa, sb):
    return d * cos - pltpu.roll(d, 112, 1) * sa - pltpu.roll(d, 16, 1) * sb


def _rowsum(v):
    return jnp.sum(v, axis=0, keepdims=True)


def _call_hosting_gather(body, gather, step_of, nsteps, *, in_specs, out_specs, out_shape, scratch_shapes=(),
                         comm=None, **kw):
    comm = comm or _Gather
    ng, n_in, n_out = len(gather), len(in_specs), len(out_specs)
    if not ng:
        return pl.pallas_call(body, in_specs=in_specs, out_specs=out_specs, out_shape=out_shape,
                              scratch_shapes=list(scratch_shapes), **kw)

    def hosted(*refs):
        ins, outs = refs[:n_in], refs[n_in + ng:n_in + ng + n_out]
        rest = refs[n_in + 2 * ng + n_out:]
        copies = comm(refs[n_in:n_in + ng], refs[n_in + ng + n_out:n_in + 2 * ng + n_out], rest[0], rest[1],
                      [b.shape[1] for b in gather])
        step = step_of()
        pl.when(step == 0)(copies.start)
        body(*ins, *outs, *rest[2:])
        pl.when(step == max(nsteps - 3, 0))(copies.forward)
        pl.when(step == nsteps - 1)(copies.finish)

    return pl.pallas_call(
        hosted, in_specs=list(in_specs) + _any_specs(ng), out_specs=list(out_specs) + _any_specs(ng),
        out_shape=list(out_shape) + comm.results(gather),
        input_output_aliases={n_in + w: n_out + w for w in range(ng)} if comm.in_place else {},
        scratch_shapes=comm.semaphores(ng) + list(scratch_shapes), **kw)


def _mix_in(x, nw, sc, sh, win, qnw2, knw2, cos, sa, sb, gmat, gather=()):
    s = x.shape[0]

    def body(x_ref, nw_ref, sc_ref, sh_ref, win_ref, qnw_ref, knw_ref, cos_ref, sa_ref, sb_ref, g_ref,
             proj_ref, h_ref, q0_ref, q1_ref, k_ref, ksw_ref, v_ref, vsw_ref, u0_ref):
        xv = x_ref[...]
        r = lax.rsqrt(jnp.mean(xv * xv, axis=-1, keepdims=True) + EPS)
        h = xv * r * (nw_ref[...] * (1.0 + sc_ref[...])) + sh_ref[...]
        hb = h.astype(_MX)
        h_ref[...] = hb
        proj = _dot(hb, win_ref[...])
        proj_ref[...] = proj.astype(_MX)
        cs, sav, sbv, g = cos_ref[...], sa_ref[...], sb_ref[...], g_ref[...]
        lo = lax.broadcasted_iota(jnp.int32, (1, 128), 1) < 64
        for j in range(4):
            q = _rope(_headnorm(proj[:, 128 * j:128 * (j + 1)], qnw_ref[...], g), cs, sav, sbv) * 0.125
            q0_ref[:, 128 * j:128 * (j + 1)] = jnp.where(lo, q, 0.0).astype(_MX)
            q1_ref[:, 128 * j:128 * (j + 1)] = jnp.where(lo, 0.0, q).astype(_MX)
        k = _rope(_headnorm(proj[:, 512:640], knw_ref[...], g), cs, sav, sbv)
        k_ref[...] = k.astype(_MX)
        ksw_ref[...] = pltpu.roll(k, 64, 1).astype(_MX)
        v = proj[:, 640:768]
        v_ref[...] = v.astype(_MX)
        vsw_ref[...] = pltpu.roll(v, 64, 1).astype(_MX)
        u0_ref[...] = proj[:, 768:1280] * _sig(proj[:, 1280:1792])

    row = lambda n: pl.BlockSpec((1, n), lambda i: (0, 0))
    tile = lambda n: pl.BlockSpec((TM, n), lambda i: (i, 0))
    res = _call_hosting_gather(
        body, gather, lambda: pl.program_id(0), s // TM, name="mix_in", grid=(s // TM,),
        in_specs=[tile(D), row(D), row(D), row(D), pl.BlockSpec((D, INW), lambda i: (0, 0)), row(128), row(128),
                  tile(128), tile(128), tile(128), pl.BlockSpec((128, 128), lambda i: (0, 0))],
        out_specs=[tile(INW), tile(D), tile(AW), tile(AW), tile(128), tile(128), tile(128), tile(128), tile(CW)],
        out_shape=[jax.ShapeDtypeStruct((s, INW), _MX), jax.ShapeDtypeStruct((s, D), _MX),
                   jax.ShapeDtypeStruct((s, AW), _MX), jax.ShapeDtypeStruct((s, AW), _MX),
                   jax.ShapeDtypeStruct((s, 128), _MX), jax.ShapeDtypeStruct((s, 128), _MX),
                   jax.ShapeDtypeStruct((s, 128), _MX), jax.ShapeDtypeStruct((s, 128), _MX),
                   jax.ShapeDtypeStruct((s, CW), F32)],
        compiler_params=_cp("arbitrary"),
    )(x, nw, sc, sh, win, qnw2, knw2, cos, sa, sb, gmat, *gather)
    return res[:9], res[9:]


def _ctx_kv(ctx, nw, scc, shc, winkv, knw2, gmat):
    def body(ctx_ref, nw_ref, sc_ref, sh_ref, w_ref, knw_ref, g_ref,
             kvc_ref, hc_ref, kc_ref, kcsw_ref, vc_ref, vcsw_ref):
        cv = ctx_ref[...]
        r = lax.rsqrt(jnp.mean(cv * cv, axis=-1, keepdims=True) + EPS)
        hc = (cv * r * (nw_ref[...] * (1.0 + sc_ref[...])) + sh_ref[...]).astype(_MX)
        hc_ref[...] = hc
        kvc = _dot(hc, w_ref[...])
        kvc_ref[...] = kvc
        kc = _headnorm(kvc[:, :128], knw_ref[...], g_ref[...])
        kc_ref[...] = kc.astype(_MX)
        kcsw_ref[...] = pltpu.roll(kc, 64, 1).astype(_MX)
        vc = kvc[:, 128:]
        vc_ref[...] = vc.astype(_MX)
        vcsw_ref[...] = pltpu.roll(vc, 64, 1).astype(_MX)

    return pl.pallas_call(
        body, name="ctx_kv",
        in_specs=[_vspec()] * 7, out_specs=[_vspec()] * 6,
        out_shape=[jax.ShapeDtypeStruct((LC, 256), F32), jax.ShapeDtypeStruct((LC, D), _MX)]
        + [jax.ShapeDtypeStruct((LC, 128), _MX)] * 4,
        compiler_params=pltpu.CompilerParams(vmem_limit_bytes=VMEM_LIMIT),
    )(ctx, nw, scc, shc, winkv, knw2, gmat)


def _attn_mask(i, s):
    r = lax.broadcasted_iota(jnp.int32, (2 * TQ, LC + 3 * TQ), 0) % TQ
    cidx = lax.broadcasted_iota(jnp.int32, (2 * TQ, LC + 3 * TQ), 1)
    qpos = i * TQ + r
    kpos = (i - 1) * TQ + (cidx - LC)
    near = (jnp.abs(qpos - kpos) <= 128) & (kpos >= 0) & (kpos < s)
    return (cidx < LC) | near


def _attn_bias(i, s):
    return jnp.where(_attn_mask(i, s), 0.0, NEG)


def _attn_exp(sc, bias, sinkv):
    sc = sc + bias
    m = jnp.maximum(jnp.max(sc, axis=-1, keepdims=True), sinkv)
    ex = jnp.exp(sc - m)
    es = jnp.exp(sinkv - m)
    return ex, es, 1.0 / (jnp.sum(ex, axis=-1, keepdims=True) + es)


def _sink_rows(sink_ref, g, e):
    return jnp.concatenate([jnp.full((TQ, 1), sink_ref[4 * g + e], F32),
                            jnp.full((TQ, 1), sink_ref[4 * g + 2 + e], F32)], axis=0)


def _attn_fwd(q0, q1, kp, kswp, vp, vswp, kc, kcsw, vc, vcsw, sink, gather=()):
    s = q0.shape[0]
    nb = s // TQ

    def body(q0_ref, q1_ref, kp_ref, kswp_ref, vp_ref, vswp_ref, kc_ref, kcsw_ref, vc_ref, vcsw_ref, sink_ref,
             o_ref, ex_ref, stat_ref):
        i = pl.program_id(0)
        st = pl.multiple_of(i * TQ, TQ)
        kall = (jnp.concatenate([kc_ref[...], kp_ref[pl.ds(st, 3 * TQ), :]], axis=0),
                jnp.concatenate([kcsw_ref[...], kswp_ref[pl.ds(st, 3 * TQ), :]], axis=0))
        vall = (jnp.concatenate([vc_ref[...], vp_ref[pl.ds(st, 3 * TQ), :]], axis=0),
                jnp.concatenate([vcsw_ref[...], vswp_ref[pl.ds(st, 3 * TQ), :]], axis=0))
        bias = _attn_bias(i, s)
        lo = lax.broadcasted_iota(jnp.int32, (1, 128), 1) < 64
        qrefs = (q0_ref, q1_ref)
        out = [jnp.zeros((TQ, 128), F32) for _ in range(4)]
        combos = [(g, e) for g in range(2) for e in range(2)]
        scores = [_dot_nt(jnp.concatenate([qrefs[e][:, 256 * g:256 * g + 128],
                                           qrefs[e][:, 256 * g + 128:256 * g + 256]], axis=0),
                          kall[0 if e == g else 1]) for g, e in combos]
        parts = []
        for n, ((g, e), sc) in enumerate(zip(combos, scores)):
            ex, es, inv = _attn_exp(sc, bias, _sink_rows(sink_ref, g, e))
            parts.append((ex.astype(_MX), inv))
            ex_ref[n] = parts[-1][0]
            stat_ref[n] = jnp.where(lo, inv, es * inv)
        for (g, e), (ex, inv) in zip(combos, parts):
            me = lo if e == 0 else jnp.logical_not(lo)
            o2 = _dot(ex, vall[0 if e == g else 1]) * inv
            out[2 * g] = out[2 * g] + jnp.where(me, o2[:TQ], 0.0)
            out[2 * g + 1] = out[2 * g + 1] + jnp.where(me, o2[TQ:], 0.0)
        for j in range(4):
            o_ref[:, 128 * j:128 * (j + 1)] = out[j]

    full = lambda a: pl.BlockSpec(a.shape, lambda i: (0, 0))
    qs = pl.BlockSpec((TQ, AW), lambda i: (i, 0))
    res = _call_hosting_gather(
        body, gather, lambda: pl.program_id(0), nb, name="attn_fwd", grid=(nb,),
        in_specs=[qs, qs, full(kp), full(kp), full(kp), full(kp), full(kc), full(kc), full(kc), full(kc),
                  pl.BlockSpec(memory_space=pltpu.SMEM)],
        out_specs=[qs, pl.BlockSpec((None, 4, 2 * TQ, LC + 3 * TQ), lambda i: (i, 0, 0, 0)),
                   pl.BlockSpec((None, 4, 2 * TQ, 128), lambda i: (i, 0, 0, 0))],
        out_shape=[jax.ShapeDtypeStruct((s, AW), F32), jax.ShapeDtypeStruct((nb, 4, 2 * TQ, LC + 3 * TQ), _MX),
                   jax.ShapeDtypeStruct((nb, 4, 2 * TQ, 128), F32)],
        compiler_params=_cp("arbitrary"),
    )(q0, q1, kp, kswp, vp, vswp, kc, kcsw, vc, vcsw, sink, *gather)
    return res[:3], res[3:]


def _conv31(u0, cw32, cb):
    s = u0.shape[0]
    rch = 256

    def body(u_ref, w_ref, b_ref, o_ref, pad_ref):
        pad_ref[0:16, :] = jnp.zeros((16, 128), F32)
        pad_ref[s + 16:s + 32, :] = jnp.zeros((16, 128), F32)
        pad_ref[16:s + 16, :] = u_ref[...]
        for cidx in range(s // rch):
            base = cidx * rch
            acc = jnp.zeros((rch, 128), F32) + b_ref[...]
            for j in range(CK):
                acc = acc + w_ref[j:j + 1, :] * pad_ref[base + j + 1:base + j + 1 + rch, :]
            o_ref[base:base + rch, :] = acc

    return pl.pallas_call(
        body, name="conv31", grid=(CW // 128,),
        in_specs=[pl.BlockSpec((s, 128), lambda c: (0, c)), pl.BlockSpec((32, 128), lambda c: (0, c)),
                  pl.BlockSpec((1, 128), lambda c: (0, c))],
        out_specs=pl.BlockSpec((s, 128), lambda c: (0, c)),
        out_shape=jax.ShapeDtypeStruct((s, CW), F32),
        scratch_shapes=[pltpu.VMEM((s + 32, 128), F32)],
        compiler_params=_cp("arbitrary"),
    )(u0, cw32, cb)


def _ln_stats(u1):
    mu = jnp.mean(u1, axis=-1, keepdims=True)
    xc = u1 - mu
    rstd = lax.rsqrt(jnp.mean(xc * xc, axis=-1, keepdims=True) + EPS)
    return xc * rstd, rstd


def _mix_out(o, u1, cnw, cnb, wout, x, g1):
    s = x.shape[0]

    def body(o_ref, u1_ref, cnw_ref, cnb_ref, w_ref, x_ref, g1_ref, x1_ref, mix_ref, cat_ref):
        u2n, _ = _ln_stats(u1_ref[...])
        u2 = u2n * cnw_ref[...] + cnb_ref[...]
        u3 = u2 * _sig(u2)
        cat = jnp.concatenate([o_ref[...], u3], axis=1).astype(_MX)
        cat_ref[...] = cat
        mix = _dot(cat, w_ref[...])
        mix_ref[...] = mix
        x1_ref[...] = x_ref[...] + g1_ref[...] * mix

    row = lambda n: pl.BlockSpec((1, n), lambda i: (0, 0))
    tile = lambda n: pl.BlockSpec((TM, n), lambda i: (i, 0))
    return pl.pallas_call(
        body, name="mix_out", grid=(s // TM,),
        in_specs=[tile(AW), tile(CW), row(CW), row(CW), pl.BlockSpec((D, D), lambda i: (0, 0)), tile(D), row(D)],
        out_specs=[tile(D), tile(D), tile(D)],
        out_shape=[jax.ShapeDtypeStruct((s, D), F32), jax.ShapeDtypeStruct((s, D), F32),
                   jax.ShapeDtypeStruct((s, D), _MX)],
        compiler_params=_cp("arbitrary"),
    )(o, u1, cnw, cnb, wout, x, g1)


def _ffn_up(x1, nw, sc, sh, wup4, gather=()):
    s = x1.shape[0]
    tm = min(1024, s)

    def body(x_ref, nw_ref, sc_ref, sh_ref, w_ref, h2_ref, up_ref, h2s):
        @pl.when(pl.program_id(1) == 0)
        def _():
            xv = x_ref[...]
            r = lax.rsqrt(jnp.mean(xv * xv, axis=-1, keepdims=True) + EPS)
            h2 = (xv * r * (nw_ref[...] * (1.0 + sc_ref[...])) + sh_ref[...]).astype(_MX)
            h2s[...] = h2
            h2_ref[...] = h2
        up_ref[...] = _dot(h2s[...], w_ref[...]).astype(_MX)

    row = pl.BlockSpec((1, D), lambda i, j: (0, 0))
    res = _call_hosting_gather(
        body, gather, lambda: pl.program_id(0) * 4 + pl.program_id(1), 4 * (s // tm), name="ffn_up",
        grid=(s // tm, 4),
        in_specs=[pl.BlockSpec((tm, D), lambda i, j: (i, 0)), row, row, row,
                  pl.BlockSpec((None, D, FQ), lambda i, j: (j, 0, 0))],
        out_specs=[pl.BlockSpec((tm, D), lambda i, j: (i, 0)),
                   pl.BlockSpec((None, tm, FQ), lambda i, j: (j // 2, i, j % 2))],
        out_shape=[jax.ShapeDtypeStruct((s, D), _MX), jax.ShapeDtypeStruct((2, s, FH), _MX)],
        scratch_shapes=[pltpu.VMEM((tm, D), _MX)],
        compiler_params=_cp("arbitrary", "arbitrary"),
    )(x1, nw, sc, sh, wup4, *gather)
    return res[:2], res[2:]


def _fill_pad8(pad_ref, val, s):
    pad_ref[0:8, :] = jnp.zeros((8, 128), F32)
    pad_ref[s + 8:s + 16, :] = jnp.zeros((8, 128), F32)
    pad_ref[8:s + 8, :] = val


def _conv3_at(pad_ref, w_ref, half, base, rch):
    return (w_ref[half, 0:1, :] * pad_ref[base + 7:base + 7 + rch, :]
            + w_ref[half, 1:2, :] * pad_ref[base + 8:base + 8 + rch, :]
            + w_ref[half, 2:3, :] * pad_ref[base + 9:base + 9 + rch, :])


def _ffn_conv_act(up0, fcw, fcb):
    s = up0.shape[1]
    rch = 256

    def body(up_ref, w_ref, b_ref, act_ref, gv_ref, padg, padv):
        _fill_pad8(padg, up_ref[0].astype(F32), s)
        _fill_pad8(padv, up_ref[1].astype(F32), s)
        for cidx in range(s // rch):
            base = cidx * rch
            gate = _conv3_at(padg, w_ref, 0, base, rch) + b_ref[0]
            val = _conv3_at(padv, w_ref, 1, base, rch) + b_ref[1]
            act_ref[base:base + rch, :] = (gate * _sig(gate) * val).astype(_MX)
            gv_ref[0, base:base + rch, :] = gate.astype(_MX)
            gv_ref[1, base:base + rch, :] = val.astype(_MX)

    return pl.pallas_call(
        body, name="ffn_conv_act", grid=(NCH,),
        in_specs=[pl.BlockSpec((2, s, 128), lambda c: (0, 0, c)), pl.BlockSpec((2, 8, 128), lambda c: (0, 0, c)),
                  pl.BlockSpec((2, 1, 128), lambda c: (0, 0, c))],
        out_specs=[pl.BlockSpec((s, 128), lambda c: (0, c)), pl.BlockSpec((2, s, 128), lambda c: (0, 0, c))],
        out_shape=[jax.ShapeDtypeStruct((s, FH), _MX), jax.ShapeDtypeStruct((2, s, FH), _MX)],
        scratch_shapes=[pltpu.VMEM((s + 16, 128), F32)] * 2,
        compiler_params=_cp("arbitrary"),
    )(up0, fcw, fcb)


def _ffn_down(act, wdown, x1, g2, tgt):
    s = x1.shape[0]

    def body(act_ref, w_ref, x1_ref, g2_ref, tgt_ref, dy_ref, dact_ref, ddn_ref, vec_ref):
        @pl.when(pl.program_id(0) == 0)
        def _():
            vec_ref[...] = jnp.zeros((8, D), F32)
        dn = _dot(act_ref[...], w_ref[...])
        diff = x1_ref[...] + g2_ref[...] * dn - tgt_ref[...]
        dy = diff * (1.0 / D)
        dy_ref[...] = dy
        ddn = (dy * g2_ref[...]).astype(_MX)
        ddn_ref[...] = ddn
        dact_ref[...] = _dot_nt(ddn, w_ref[...]).astype(_MX)
        vec_ref[0:1, :] += _rowsum(dy * dn)
        vec_ref[1:2, :] += _rowsum(diff * diff)

    tile = lambda n: pl.BlockSpec((TM, n), lambda i: (i, 0))
    return pl.pallas_call(
        body, name="ffn_down", grid=(s // TM,),
        in_specs=[tile(FH), pl.BlockSpec((FH, D), lambda i: (0, 0)), tile(D), pl.BlockSpec((1, D), lambda i: (0, 0)),
                  tile(D)],
        out_specs=[tile(D), tile(FH), tile(D), pl.BlockSpec((8, D), lambda i: (0, 0))],
        out_shape=[jax.ShapeDtypeStruct((s, D), F32), jax.ShapeDtypeStruct((s, FH), _MX),
                   jax.ShapeDtypeStruct((s, D), _MX), jax.ShapeDtypeStruct((8, D), F32)],
        compiler_params=_cp("arbitrary"),
    )(act, wdown, x1, g2, tgt)


def _ffn_conv_bwd(up0, gv, dact, fcw):
    s = up0.shape[1]
    rch = 256

    def body(up_ref, gv_ref, da_ref, w_ref, dup_ref, gw_ref, dpg, dpv):
        for p in (dpg, dpv):
            p[0:8, :] = jnp.zeros((8, 128), F32)
            p[s + 8:s + 16, :] = jnp.zeros((8, 128), F32)
        for cidx in range(s // rch):
            rows = slice(cidx * rch, (cidx + 1) * rch)
            gate = gv_ref[0, rows, :].astype(F32)
            da = da_ref[rows, :].astype(F32)
            sg = _sig(gate)
            silu = gate * sg
            dpg[8 + cidx * rch:8 + (cidx + 1) * rch, :] = da * gv_ref[1, rows, :].astype(F32) * (sg + silu * (1.0 - sg))
            dpv[8 + cidx * rch:8 + (cidx + 1) * rch, :] = da * silu
        acc = [[jnp.zeros((1, 128), F32) for _ in range(4)] for _ in range(2)]
        for cidx in range(s // rch):
            base = cidx * rch
            for half, dp in enumerate((dpg, dpv)):
                shifted = [dp[base + 9 - j:base + 9 - j + rch, :] for j in range(3)]
                dup_ref[half, base:base + rch, :] = (w_ref[half, 0:1, :] * shifted[0] + w_ref[half, 1:2, :] * shifted[1]
                                                     + w_ref[half, 2:3, :] * shifted[2]).astype(_MX)
                xv = up_ref[half, base:base + rch, :].astype(F32)
                for j in range(3):
                    acc[half][j] = acc[half][j] + _rowsum(xv * shifted[j])
                acc[half][3] = acc[half][3] + _rowsum(shifted[1])
        for half in range(2):
            gw_ref[half] = jnp.zeros((8, 128), F32)
            for j in range(4):
                gw_ref[half, j:j + 1, :] = acc[half][j]

    both = pl.BlockSpec((2, s, 128), lambda c: (0, 0, c))
    return pl.pallas_call(
        body, name="ffn_conv_bwd", grid=(NCH,),
        in_specs=[both, both, pl.BlockSpec((s, 128), lambda c: (0, c)), pl.BlockSpec((2, 8, 128), lambda c: (0, 0, c))],
        out_specs=[both, pl.BlockSpec((2, 8, 128), lambda c: (0, 0, c))],
        out_shape=[jax.ShapeDtypeStruct((2, s, FH), _MX), jax.ShapeDtypeStruct((2, 8, FH), F32)],
        scratch_shapes=[pltpu.VMEM((s + 16, 128), F32)] * 2,
        compiler_params=_cp("arbitrary"),
    )(up0, gv, dact, fcw)


def _tn_matmul(a, b, tm, tn, name, b_split=False, by_chip=False, init=None):
    s, m = a.shape
    n = 2 * b.shape[2] if b_split else b.shape[1]
    ts = min(1024, s)
    nsteps = s // ts
    npb = (n // 2) // tn if b_split else None

    def body(*refs):
        if init is None:
            a_ref, b_ref, o_ref, acc = refs
        else:
            a_ref, b_ref, i_ref, o_ref, acc = refs
        k = pl.program_id(2)

        @pl.when(k == 0)
        def _():
            acc[...] = jnp.zeros((tm, tn), F32)
            if init is not None:
                acc[:, 512:768] = i_ref[...]
        acc[...] += _dot_tn(a_ref[...], b_ref[...])

        @pl.when(k == nsteps - 1)
        def _():
            o_ref[...] = acc[...].astype(_MX)

    if b_split:
        bspec = pl.BlockSpec((None, ts, tn), lambda i, j, k: (j // npb, k, j % npb))
    else:
        bspec = pl.BlockSpec((ts, tn), lambda i, j, k: (k, j))
    in_specs = [pl.BlockSpec((ts, tm), lambda i, j, k: (k, i)), bspec]
    args = [a, b]
    if init is not None:
        in_specs.append(pl.BlockSpec((tm, 256), lambda i, j, k: (i, 0)))
        args.append(init)
    if by_chip:
        out_spec = pl.BlockSpec((None, tm, tn), lambda i, j, k: (j, i, 0))
        out_shape = jax.ShapeDtypeStruct((n // tn, m, tn), _MX)
    else:
        out_spec = pl.BlockSpec((tm, tn), lambda i, j, k: (i, j))
        out_shape = jax.ShapeDtypeStruct((m, n), _MX)
    return pl.pallas_call(
        body, name=name, grid=(m // tm, n // tn, nsteps),
        in_specs=in_specs, out_specs=out_spec, out_shape=out_shape,
        scratch_shapes=[pltpu.VMEM((tm, tn), F32)],
        compiler_params=_cp("arbitrary", "arbitrary", "arbitrary"),
    )(*args)


def _ffn_up_bwd(dup, wup4, x1, dy, mix, nw, sc, g1):
    s = x1.shape[0]
    nk = 2

    def body(dup_ref, w_ref, x1_ref, dy_ref, mix_ref, nw_ref, sc_ref, g1_ref, dx1_ref, dmix_ref, vec_ref, acc):
        i, k = pl.program_id(0), pl.program_id(1)

        @pl.when(k == 0)
        def _():
            acc[...] = jnp.zeros((TM, D), F32)

        @pl.when((k == 0) & (i == 0))
        def _():
            vec_ref[...] = jnp.zeros((8, D), F32)
        acc[...] += _dot_nt(dup_ref[:, :FQ], w_ref[0]) + _dot_nt(dup_ref[:, FQ:], w_ref[1])

        @pl.when(k == nk - 1)
        def _():
            dh = acc[...]
            xv = x1_ref[...]
            r = lax.rsqrt(jnp.mean(xv * xv, axis=-1, keepdims=True) + EPS)
            xn = xv * r
            nwv, scv = nw_ref[...], sc_ref[...]
            vec_ref[0:1, :] += _rowsum(dh)
            vec_ref[1:2, :] += _rowsum(dh * xn) * nwv
            vec_ref[2:3, :] += _rowsum(dh * xn) * (1.0 + scv)
            dxn = dh * (nwv * (1.0 + scv))
            dx1 = dy_ref[...] + r * (dxn - xn * jnp.mean(dxn * xn, axis=-1, keepdims=True))
            dx1_ref[...] = dx1
            vec_ref[3:4, :] += _rowsum(dx1 * mix_ref[...])
            dmix_ref[...] = (dx1 * g1_ref[...]).astype(_MX)

    tile = pl.BlockSpec((TM, D), lambda i, k: (i, 0))
    row = pl.BlockSpec((1, D), lambda i, k: (0, 0))
    return pl.pallas_call(
        body, name="ffn_up_bwd", grid=(s // TM, nk),
        in_specs=[pl.BlockSpec((None, TM, FH), lambda i, k: (k, i, 0)),
                  pl.BlockSpec((2, D, FQ), lambda i, k: (k, 0, 0)), tile, tile, tile, row, row, row],
        out_specs=[tile, tile, pl.BlockSpec((8, D), lambda i, k: (0, 0))],
        out_shape=[jax.ShapeDtypeStruct((s, D), F32), jax.ShapeDtypeStruct((s, D), _MX),
                   jax.ShapeDtypeStruct((8, D), F32)],
        scratch_shapes=[pltpu.VMEM((TM, D), F32)],
        compiler_params=_cp("arbitrary", "arbitrary"),
    )(dup, wup4, x1, dy, mix, nw, sc, g1)


def _mix_out_bwd(dmix, wout, u1, cnw, cnb, swap=()):
    s = u1.shape[0]

    def body(dm_ref, w_ref, u1_ref, cnw_ref, cnb_ref, do_ref, du1_ref, vec_ref):
        @pl.when(pl.program_id(0) == 0)
        def _():
            vec_ref[...] = jnp.zeros((8, CW), F32)
        dcat = _dot_nt(dm_ref[...], w_ref[...])
        do_ref[...] = dcat[:, :AW]
        du3 = dcat[:, AW:]
        u2n, rstd = _ln_stats(u1_ref[...])
        u2 = u2n * cnw_ref[...] + cnb_ref[...]
        sg = _sig(u2)
        du2 = du3 * sg * (1.0 + u2 * (1.0 - sg))
        vec_ref[0:1, :] += _rowsum(du2)
        vec_ref[1:2, :] += _rowsum(du2 * u2n)
        d2n = du2 * cnw_ref[...]
        du1_ref[...] = rstd * (d2n - jnp.mean(d2n, axis=-1, keepdims=True)
                               - u2n * jnp.mean(d2n * u2n, axis=-1, keepdims=True))

    row = lambda n: pl.BlockSpec((1, n), lambda i: (0, 0))
    tile = lambda n: pl.BlockSpec((TM, n), lambda i: (i, 0))
    res = _call_hosting_gather(
        body, swap, lambda: pl.program_id(0), s // TM, comm=_Swap, name="mix_out_bwd", grid=(s // TM,),
        in_specs=[tile(D), pl.BlockSpec((D, D), lambda i: (0, 0)), tile(CW), row(CW), row(CW)],
        out_specs=[tile(AW), tile(CW), pl.BlockSpec((8, CW), lambda i: (0, 0))],
        out_shape=[jax.ShapeDtypeStruct((s, AW), F32), jax.ShapeDtypeStruct((s, CW), F32),
                   jax.ShapeDtypeStruct((8, CW), F32)],
        compiler_params=_cp("arbitrary"),
    )(dmix, wout, u1, cnw, cnb, *swap)
    return res[:3], res[3:]


def _conv31_bwd(du1, u0, cw32, proj, exchange=()):
    s = u0.shape[0]
    rch = 128

    def body(d_ref, u_ref, w_ref, ga_ref, gb_ref, dga_ref, dgb_ref, gw_ref, padd):
        padd[0:16, :] = jnp.zeros((16, 128), F32)
        padd[s + 16:s + 32, :] = jnp.zeros((16, 128), F32)
        padd[16:s + 16, :] = d_ref[...]
        gw_ref[...] = jnp.zeros((32, 128), F32)
        gw_ref[CK:CK + 1, :] = _rowsum(d_ref[...])
        for cidx in range(s // rch):
            base = cidx * rch
            uv = u_ref[base:base + rch, :]
            du0 = jnp.zeros((rch, 128), F32)
            for j in range(CK):
                shifted = padd[base + 31 - j:base + 31 - j + rch, :]
                du0 = du0 + w_ref[j:j + 1, :] * shifted
                gw_ref[j:j + 1, :] += _rowsum(uv * shifted)
            sg = _sig(gb_ref[base:base + rch, :].astype(F32))
            ga = ga_ref[base:base + rch, :].astype(F32)
            dga_ref[base:base + rch, :] = (du0 * sg).astype(_MX)
            dgb_ref[base:base + rch, :] = (du0 * ga * sg * (1.0 - sg)).astype(_MX)

    blk = lambda off: pl.BlockSpec((s, 128), lambda c: (0, c + off))
    res = _call_hosting_gather(
        body, exchange, lambda: pl.program_id(0), CW // 128, comm=_Exchange, name="conv31_bwd", grid=(CW // 128,),
        in_specs=[blk(0), blk(0), pl.BlockSpec((32, 128), lambda c: (0, c)), blk(6), blk(10)],
        out_specs=[blk(0), blk(0), pl.BlockSpec((32, 128), lambda c: (0, c))],
        out_shape=[jax.ShapeDtypeStruct((s, CW), _MX), jax.ShapeDtypeStruct((s, CW), _MX),
                   jax.ShapeDtypeStruct((32, CW), F32)],
        scratch_shapes=[pltpu.VMEM((s + 32, 128), F32)],
        compiler_params=_cp("arbitrary"),
    )(du1, u0, cw32, proj, proj, *exchange)
    return res[:3], res[3:]


def _attn_bwd(q0, q1, kp, kswp, vp, vswp, kc, kcsw, vc, vcsw, ex, stat, o, do, exchange=()):
    s = q0.shape[0]
    nb = s // TQ
    ne = len(exchange)

    def body(*refs):
        (q0_ref, q1_ref, kp_ref, kswp_ref, vp_ref, vswp_ref, kc_ref, kcsw_ref, vc_ref, vcsw_ref, ex_ref, stat_ref,
         o_ref, do_ref) = refs[:14]
        dq_ref, dk_ref, dv_ref, dkc_ref, dvc_ref, dsink_ref = refs[14 + ne:20 + ne]
        i = pl.program_id(0)
        if ne:
            exch = _Exchange(refs[14:14 + ne], refs[20 + ne:20 + 2 * ne], refs[20 + 2 * ne], refs[21 + 2 * ne])
            pl.when(i == 0)(exch.start)

        @pl.when(i == 0)
        def _():
            dk_ref[...] = jnp.zeros((s + 2 * TQ, 128), F32)
            dv_ref[...] = jnp.zeros((s + 2 * TQ, 128), F32)
            dkc_ref[...] = jnp.zeros((LC, 128), F32)
            dvc_ref[...] = jnp.zeros((LC, 128), F32)
            dsink_ref[...] = jnp.zeros((8, 128), F32)
        st = pl.multiple_of(i * TQ, TQ)
        kall = (jnp.concatenate([kc_ref[...], kp_ref[pl.ds(st, 3 * TQ), :]], axis=0),
                jnp.concatenate([kcsw_ref[...], kswp_ref[pl.ds(st, 3 * TQ), :]], axis=0))
        vall = (jnp.concatenate([vc_ref[...], vp_ref[pl.ds(st, 3 * TQ), :]], axis=0),
                jnp.concatenate([vcsw_ref[...], vswp_ref[pl.ds(st, 3 * TQ), :]], axis=0))
        lo = lax.broadcasted_iota(jnp.int32, (1, 128), 1) < 64
        qrefs = (q0_ref, q1_ref)
        dq = [jnp.zeros((TQ, 128), F32) for _ in range(4)]
        dkt = jnp.zeros((128, LC + 3 * TQ), F32)
        dvt = jnp.zeros((128, LC + 3 * TQ), F32)
        combos = [(g, e) for g in range(2) for e in range(2)]

        def stacked(ref, g):
            return jnp.concatenate([ref[:, 256 * g:256 * g + 128], ref[:, 256 * g + 128:256 * g + 256]], axis=0)

        for n, (g, e) in enumerate(combos):
            me = lo if e == 0 else jnp.logical_not(lo)
            sw = 0 if e == g else 1
            qm = stacked(qrefs[e], g)
            p = ex_ref[n].astype(F32) * stat_ref[n, :, 0:1]
            dom = jnp.where(me, stacked(do_ref, g), 0.0)
            dd = jnp.sum(dom * stacked(o_ref, g), axis=-1, keepdims=True)
            domx = dom.astype(_MX)
            sd = stat_ref[n, :, 64:65] * dd
            dsink_ref[4 * g + e:4 * g + e + 1, :] -= jnp.sum(sd[:TQ], axis=0, keepdims=True)
            dsink_ref[4 * g + 2 + e:4 * g + 3 + e, :] -= jnp.sum(sd[TQ:], axis=0, keepdims=True)
            ds = (p * (_dot_nt(domx, vall[sw]) - dd)).astype(_MX)
            dq2 = _dot(ds, kall[sw])
            dq[2 * g] = dq[2 * g] + jnp.where(me, dq2[:TQ], 0.0)
            dq[2 * g + 1] = dq[2 * g + 1] + jnp.where(me, dq2[TQ:], 0.0)
            dk2 = _dot_tn(qm, ds)
            dv2 = _dot_tn(domx, p.astype(_MX))
            if sw:
                dk2 = pltpu.roll(dk2, 64, 0)
                dv2 = pltpu.roll(dv2, 64, 0)
            dkt = dkt + dk2
            dvt = dvt + dv2
        dkt = dkt.T
        dvt = dvt.T
        for j in range(4):
            dq_ref[:, 128 * j:128 * (j + 1)] = dq[j] * 0.125
        dkc_ref[...] += dkt[:LC]
        dvc_ref[...] += dvt[:LC]
        dk_ref[pl.ds(st, 3 * TQ), :] += dkt[LC:]
        dv_ref[pl.ds(st, 3 * TQ), :] += dvt[LC:]
        if ne:
            pl.when(i == nb - 1)(exch.finish)

    full = lambda a: pl.BlockSpec(a.shape, lambda i: (0, 0))
    fs = lambda r: pl.BlockSpec((r, 128), lambda i: (0, 0))
    qs = pl.BlockSpec((TQ, AW), lambda i: (i, 0))
    res = pl.pallas_call(
        body, name="attn_bwd", grid=(nb,),
        in_specs=[qs, qs, full(kp), full(kp), full(kp), full(kp), full(kc), full(kc), full(kc), full(kc),
                  pl.BlockSpec((None, 4, 2 * TQ, LC + 3 * TQ), lambda i: (i, 0, 0, 0)),
                  pl.BlockSpec((None, 4, 2 * TQ, 128), lambda i: (i, 0, 0, 0)), qs, qs] + _any_specs(ne),
        out_specs=[qs, fs(s + 2 * TQ), fs(s + 2 * TQ), fs(LC), fs(LC), fs(8)] + _any_specs(ne),
        out_shape=[jax.ShapeDtypeStruct((s, AW), F32), jax.ShapeDtypeStruct((s + 2 * TQ, 128), F32),
                   jax.ShapeDtypeStruct((s + 2 * TQ, 128), F32), jax.ShapeDtypeStruct((LC, 128), F32),
                   jax.ShapeDtypeStruct((LC, 128), F32), jax.ShapeDtypeStruct((8, 128), F32)]
        + _Exchange.out_shapes(exchange),
        scratch_shapes=_Exchange.semaphores(ne) if ne else [],
        compiler_params=_cp("arbitrary"),
    )(q0, q1, kp, kswp, vp, vswp, kc, kcsw, vc, vcsw, ex, stat, o, do, *exchange)
    return res[:6], res[6:]


def _mix_in_bwd(dq, dk, dv, dga, dgb, proj, x, dx1, win, nw, sc, qnw2, knw2, cos, sa, sb, gmat):
    s = x.shape[0]

    def body(dq_ref, dk_ref, dv_ref, dga_ref, dgb_ref, proj_ref, x_ref, dx1_ref, win_ref, nw_ref, sc_ref,
             qnw_ref, knw_ref, cos_ref, sa_ref, sb_ref, g_ref, gx_ref, dproj_ref, vec_ref):
        @pl.when(pl.program_id(0) == 0)
        def _():
            vec_ref[...] = jnp.zeros((8, D), F32)
        cs, sav, sbv, g = cos_ref[...], sa_ref[...], sb_ref[...], g_ref[...]
        gq = jnp.zeros((1, 128), F32)
        for j in range(4):
            dqn = _rope_bwd(dq_ref[:, 128 * j:128 * (j + 1)], cs, sav, sbv)
            dch, gw = _headnorm_bwd(proj_ref[:, 128 * j:128 * (j + 1)].astype(F32), dqn, qnw_ref[...], g)
            dproj_ref[:, 128 * j:128 * (j + 1)] = dch.astype(_MX)
            gq = gq + _rowsum(gw)
        dkn = _rope_bwd(dk_ref[...], cs, sav, sbv)
        dch, gw = _headnorm_bwd(proj_ref[:, 512:640].astype(F32), dkn, knw_ref[...], g)
        dproj_ref[:, 512:640] = dch.astype(_MX)
        dproj_ref[:, 640:768] = dv_ref[...].astype(_MX)
        dproj_ref[:, 768:1280] = dga_ref[...]
        dproj_ref[:, 1280:1792] = dgb_ref[...]
        vec_ref[3:4, 0:128] += gq
        vec_ref[3:4, 128:256] += _rowsum(gw)
        dh = _dot_nt(dproj_ref[...], win_ref[...])
        xv = x_ref[...]
        r = lax.rsqrt(jnp.mean(xv * xv, axis=-1, keepdims=True) + EPS)
        xn = xv * r
        nwv, scv = nw_ref[...], sc_ref[...]
        vec_ref[0:1, :] += _rowsum(dh)
        vec_ref[1:2, :] += _rowsum(dh * xn) * nwv
        vec_ref[2:3, :] += _rowsum(dh * xn) * (1.0 + scv)
        dxn = dh * (nwv * (1.0 + scv))
        gx_ref[...] = dx1_ref[...] + r * (dxn - xn * jnp.mean(dxn * xn, axis=-1, keepdims=True))

    row = lambda n: pl.BlockSpec((1, n), lambda i: (0, 0))
    tile = lambda n: pl.BlockSpec((TM, n), lambda i: (i, 0))
    return pl.pallas_call(
        body, name="mix_in_bwd", grid=(s // TM,),
        in_specs=[tile(AW), tile(128), tile(128), tile(CW), tile(CW), tile(INW), tile(D), tile(D),
                  pl.BlockSpec((D, INW), lambda i: (0, 0)), row(D), row(D), row(128), row(128),
                  tile(128), tile(128), tile(128), pl.BlockSpec((128, 128), lambda i: (0, 0))],
        out_specs=[tile(D), tile(INW), pl.BlockSpec((8, D), lambda i: (0, 0))],
        out_shape=[jax.ShapeDtypeStruct((s, D), F32), jax.ShapeDtypeStruct((s, INW), _MX),
                   jax.ShapeDtypeStruct((8, D), F32)],
        compiler_params=_cp("arbitrary"),
    )(dq, dk, dv, dga, dgb, proj, x, dx1, win, nw, sc, qnw2, knw2, cos, sa, sb, gmat)


def _ctx_bwd(ctx, nw, scc, winkv, kvc, hc, dkc, dvc, knw2, gmat):
    def body(ctx_ref, nw_ref, sc_ref, w_ref, kvc_ref, hc_ref, dkc_ref, dvc_ref, knw_ref, g_ref, gw_ref, vec_ref):
        dkr, gk = _headnorm_bwd(kvc_ref[:, 0:128], dkc_ref[...], knw_ref[...], g_ref[...])
        dkv = jnp.concatenate([dkr, dvc_ref[...]], axis=1).astype(_MX)
        gw_ref[...] = _dot_tn(hc_ref[...], dkv)
        dh = _dot_nt(dkv, w_ref[...])
        cv = ctx_ref[...]
        r = lax.rsqrt(jnp.mean(cv * cv, axis=-1, keepdims=True) + EPS)
        cn = cv * r
        vec_ref[...] = jnp.zeros((8, D), F32)
        vec_ref[0:1, :] = _rowsum(dh)
        vec_ref[1:2, :] = _rowsum(dh * cn) * nw_ref[...]
        vec_ref[2:3, :] = _rowsum(dh * cn) * (1.0 + sc_ref[...])
        vec_ref[3:4, 0:128] = _rowsum(gk)

    return pl.pallas_call(
        body, name="ctx_bwd", in_specs=[_vspec()] * 10, out_specs=[_vspec()] * 2,
        out_shape=[jax.ShapeDtypeStruct((D, 256), F32), jax.ShapeDtypeStruct((8, D), F32)],
        compiler_params=pltpu.CompilerParams(vmem_limit_bytes=VMEM_LIMIT),
    )(ctx, nw, scc, winkv, kvc, hc, dkc, dvc, knw2, gmat)


def _mod_fwd(cs, wmod, bloc):
    def body(c_ref, w_ref, b_ref, o_ref):
        cv = c_ref[...]
        o_ref[...] = _dot((cv * _sig(cv)).astype(_MX), w_ref[...].astype(_MX)) + b_ref[...]

    return pl.pallas_call(
        body, name="mod_fwd", in_specs=[_vspec()] * 3, out_specs=_vspec(),
        out_shape=jax.ShapeDtypeStruct((16, wmod.shape[1]), F32),
        compiler_params=pltpu.CompilerParams(vmem_limit_bytes=VMEM_LIMIT),
    )(cs, wmod, bloc)


def _mod_bwd(cs, dmod, wmod):
    def body(c_ref, d_ref, w_ref, gw_ref, part_ref):
        cv = c_ref[...]
        sl = (cv * _sig(cv)).astype(_MX)
        dm = d_ref[...].astype(_MX)
        gw_ref[...] = _dot_tn(sl, dm)
        part_ref[...] = _dot_nt(dm, w_ref[...].astype(_MX))

    return pl.pallas_call(
        body, name="mod_bwd", in_specs=[_vspec()] * 3, out_specs=[_vspec()] * 2,
        out_shape=[jax.ShapeDtypeStruct(wmod.shape, F32), jax.ShapeDtypeStruct((16, D), F32)],
        compiler_params=pltpu.CompilerParams(vmem_limit_bytes=VMEM_LIMIT),
    )(cs, dmod, wmod)


def _sum_leading(a, name):
    n = a.shape[0]

    def body(a_ref, o_ref):
        acc = a_ref[0]
        for k in range(1, n):
            acc = acc + a_ref[k]
        o_ref[...] = acc

    return pl.pallas_call(
        body, name=name, in_specs=[_vspec()], out_specs=_vspec(),
        out_shape=jax.ShapeDtypeStruct(a.shape[1:], F32),
        compiler_params=pltpu.CompilerParams(vmem_limit_bytes=VMEM_LIMIT),
    )(a)


def _cctx_grad(parts, cc):
    def body(p_ref, c_ref, o_ref):
        acc = p_ref[0, 8:9, :]
        for k in range(1, 4):
            acc = acc + p_ref[k, 8:9, :]
        cv = c_ref[...]
        sg = _sig(cv)
        o_ref[...] = acc * (sg * (1.0 + cv * (1.0 - sg)))

    return pl.pallas_call(
        body, name="cctx_grad", in_specs=[_vspec()] * 2, out_specs=_vspec(),
        out_shape=jax.ShapeDtypeStruct((1, D), F32),
    )(parts, cc)


def _adam_math(w, g, m, v):
    mn = ADAM_B1 * m + (1.0 - ADAM_B1) * g
    vn = ADAM_B2 * v + (1.0 - ADAM_B2) * (g * g)
    mh = mn / (1.0 - ADAM_B1 ** ADAM_STEP)
    vh = vn / (1.0 - ADAM_B2 ** ADAM_STEP)
    delta = -ADAM_LR * (mh / (jnp.sqrt(vh) + ADAM_EPS) + ADAM_WD * w)
    return delta, mn, vn


def _adam_big(w, g, m, v, name, exchange=()):
    r, n = w.shape
    tr = 256 if r % 256 == 0 else 64

    def body(w_ref, g_ref, m_ref, v_ref, d_ref, mo_ref, vo_ref):
        d, mn, vn = _adam_math(w_ref[...], g_ref[...], m_ref[...], v_ref[...])
        d_ref[...] = d
        mo_ref[...] = mn
        vo_ref[...] = vn

    spec = pl.BlockSpec((tr, n), lambda i: (i, 0))
    res = _call_hosting_gather(
        body, exchange, lambda: pl.program_id(0), r // tr, comm=_Exchange, name=name, grid=(r // tr,),
        in_specs=[spec] * 4, out_specs=[spec] * 3, out_shape=[jax.ShapeDtypeStruct((r, n), F32)] * 3,
        compiler_params=_cp("arbitrary"),
    )(w, g, m, v, *exchange)
    return (res[:3], res[3:]) if exchange else res


def _adam_small(ws, gs, ms, vs):
    n = len(ws)

    def body(*refs):
        ins, outs = refs[:4 * n], refs[4 * n:]
        for k in range(n):
            d, mn, vn = _adam_math(ins[k][...], ins[n + k][...], ins[2 * n + k][...], ins[3 * n + k][...])
            outs[k][...] = d
            outs[n + k][...] = mn
            outs[2 * n + k][...] = vn

    shapes = [jax.ShapeDtypeStruct(w.shape, F32) for w in ws]
    res = pl.pallas_call(
        body, name="adam_small", in_specs=[_vspec()] * (4 * n), out_specs=[_vspec()] * (3 * n),
        out_shape=shapes * 3,
    )(*ws, *gs, *ms, *vs)
    return res[:n], res[n:2 * n], res[2 * n:]


def _pair_sum(grads, from_sib, core, name):
    n = len(grads)

    def body(c_ref, *refs):
        for w in range(n):
            a_ref, b_ref, o_ref = refs[w], refs[n + w], refs[2 * n + w]
            o_ref[...] = (a_ref[...].astype(F32) + b_ref[...].astype(F32)).astype(_MX)

    halves = [(None, g.shape[1] // 2, g.shape[2]) for g in grads]
    return pl.pallas_call(
        body, name=name,
        grid_spec=pltpu.PrefetchScalarGridSpec(
            num_scalar_prefetch=1, grid=(4,),
            in_specs=[pl.BlockSpec(h, lambda j, c: (j, c[0], 0)) for h in halves]
            + [pl.BlockSpec(h, lambda j, c: (j, 0, 0)) for h in halves],
            out_specs=[pl.BlockSpec(h, lambda j, c: (j, 0, 0)) for h in halves]),
        out_shape=[jax.ShapeDtypeStruct(p.shape, _MX) for p in from_sib],
        compiler_params=_cp("arbitrary"),
    )(core.reshape(1), *grads, *from_sib)


def _chip_sum(parts, arrived, chip, core):
    n = len(parts)

    def body(s_ref, *refs):
        for w in range(n):
            own_ref, p_ref, o_ref = refs[w], refs[n + w], refs[2 * n + w]
            acc = own_ref[...].astype(F32)
            for k in range(3):
                acc = acc + p_ref[k].astype(F32)
            o_ref[...] = acc

    blk = [(p.shape[1] // 2, p.shape[2]) for p in parts]
    return pl.pallas_call(
        body, name="grad_chip_sum",
        grid_spec=pltpu.PrefetchScalarGridSpec(
            num_scalar_prefetch=1, grid=(2,),
            in_specs=[pl.BlockSpec((None,) + b, lambda i, s: (s[0], i, 0)) for b in blk]
            + [pl.BlockSpec((3,) + b, lambda i, s: (0, i, 0)) for b in blk],
            out_specs=[pl.BlockSpec(b, lambda i, s: (2 * s[1] + i, 0)) for b in blk]),
        out_shape=[jax.ShapeDtypeStruct((2 * p.shape[1], p.shape[2]), F32) for p in parts],
        compiler_params=_cp("arbitrary"),
    )(jnp.stack([chip, core]), *parts, *arrived)


def _cast_shards(ws, chip):
    n = len(ws)

    def body(s_ref, *refs):
        for w in range(n):
            refs[n + w][...] = refs[w][...].astype(_MX)

    blk = [(a.shape[0] // 2, a.shape[1]) for a in ws]
    return pl.pallas_call(
        body, name="cast_shards",
        grid_spec=pltpu.PrefetchScalarGridSpec(
            num_scalar_prefetch=1, grid=(2,),
            in_specs=[pl.BlockSpec(b, lambda i, s: (i, 0)) for b in blk],
            out_specs=[pl.BlockSpec((None,) + b, lambda i, s: (s[0], i, 0)) for b in blk]),
        out_shape=[jax.ShapeDtypeStruct((4,) + a.shape, _MX) for a in ws],
        compiler_params=_cp("arbitrary"),
    )(chip.reshape(1), *ws)


def _position():
    x, y, c = lax.axis_index("x"), lax.axis_index("y"), lax.axis_index("c")
    return x, y, c


def _small_allgather(blk, name, exchange=(), share=(), gather=()):
    m_per, n = blk.shape
    assert not (share and gather)
    share = tuple(share) + tuple(gather)
    ne, ns = len(exchange), len(share)

    def body(*refs):
        x_ref = refs[0]
        out_ref = refs[1 + ne + ns]
        send_sems, recv_sems = refs[2 + 2 * ne + 2 * ns], refs[3 + 2 * ne + 2 * ns]
        extra = refs[4 + 2 * ne + 2 * ns:]
        x, y, c = _position()
        me, sibling = (x, y, c), (x, y, 1 - c)
        chips = [(1 - x, y), (x, 1 - y), (1 - x, 1 - y)]
        if ne:
            exch = _Exchange(refs[1:1 + ne], refs[2 + ne + ns:2 + 2 * ne + ns], extra[0], extra[1])
            exch.start()
        if gather:
            gat = _Gather(refs[1 + ne:1 + ne + ns], refs[2 + 2 * ne + ns:2 + 2 * ne + 2 * ns],
                          extra[2 * bool(ne)], extra[2 * bool(ne) + 1], [b.shape[1] for b in gather])
            gat.start()
        elif ns:
            halves = _share_copies(refs[1 + ne:1 + ne + ns], refs[2 + 2 * ne + ns:2 + 2 * ne + 2 * ns],
                                   extra[2 * bool(ne)], extra[2 * bool(ne) + 1], [b.shape[0] for b in share])
            for cp in halves:
                cp.start()

        def rows(px, py, pc):
            return out_ref.at[pl.ds(pl.multiple_of((4 * px + 2 * py + pc) * m_per, 8), m_per), :]

        def copy(k, block, to, src=None):
            return pltpu.make_async_remote_copy(
                src_ref=rows(*block) if src is None else src, dst_ref=rows(*block),
                send_sem=send_sems.at[k], recv_sem=recv_sems.at[k], device_id=to, device_id_type=MESH)

        out_ref[pl.ds(pl.multiple_of((4 * x + 2 * y + c) * m_per, 8), m_per), :] = x_ref[...]
        first = [copy(0, me, sibling, src=x_ref)]
        first += [copy(1 + j, me, (*chip, c), src=x_ref) for j, chip in enumerate(chips)]
        for cp in first:
            cp.start()
        passed = [copy(4 + j, (*chip, c), sibling) for j, chip in enumerate(chips)]
        for j, chip in enumerate(chips):
            copy(1 + j, (*chip, c), me).wait_recv()
            passed[j].start()
        copy(0, sibling, me).wait_recv()
        for j, chip in enumerate(chips):
            copy(4 + j, (*chip, 1 - c), me).wait_recv()
        for cp in first + passed:
            cp.wait_send()
        if ne:
            exch.finish()
        if gather:
            gat.forward()
            gat.finish()
        elif ns:
            for cp in halves:
                cp.wait()

    res = pl.pallas_call(
        body, name=name,
        out_shape=[jax.ShapeDtypeStruct((8 * m_per, n), blk.dtype)] + _Exchange.out_shapes(exchange)
        + [jax.ShapeDtypeStruct(b.shape, b.dtype) for b in share],
        in_specs=[_vspec()] + _any_specs(ne + ns), out_specs=[_vspec()] + _any_specs(ne + ns),
        input_output_aliases={1 + ne + w: 1 + ne + w for w in range(ns)},
        scratch_shapes=[pltpu.SemaphoreType.DMA((7,)), pltpu.SemaphoreType.DMA((7,))]
        + (_Exchange.semaphores(ne) if ne else [])
        + (_Gather.semaphores(ns) if gather else
           [pltpu.SemaphoreType.DMA((ns,)), pltpu.SemaphoreType.DMA((ns,))] if ns else []),
    )(blk, *exchange, *share)
    return res[0], res[1:1 + ne], res[1 + ne:]


def _share_copies(src_refs, out_refs, send_sems, recv_sems, nrows):
    x, y, c = _position()
    return [pltpu.make_async_remote_copy(
        src_ref=_rows(src_refs[w], c, r), dst_ref=_rows(out_refs[w], c, r), send_sem=send_sems.at[w],
        recv_sem=recv_sems.at[w], device_id=(x, y, 1 - c), device_id_type=MESH) for w, r in enumerate(nrows)]


def _any_specs(n):
    return [pl.BlockSpec(memory_space=pl.ANY)] * n


def _rows(ref, half, nrows):
    return ref.at[pl.ds(half * (nrows // 2), nrows // 2), :]


class _Gather:
    def __init__(self, src_refs, out_refs, send_sems, recv_sems, nrows):
        x, y, c = _position()
        chip = 2 * x + y
        sibling = (x, y, 1 - c)

        def copy(k, src, dst, to):
            return pltpu.make_async_remote_copy(src_ref=src, dst_ref=dst, send_sem=send_sems.at[k],
                                                recv_sem=recv_sems.at[k], device_id=to, device_id_type=MESH)

        self.first, self.first_landed, self.passed, self.passed_landed = [], [], [], []
        for w, r in enumerate(nrows):
            for j, ch in enumerate([(1 - x, y), (x, 1 - y), (1 - x, 1 - y)]):
                theirs = _rows(out_refs[w].at[2 * ch[0] + ch[1]], c, r)
                other = _rows(out_refs[w].at[2 * ch[0] + ch[1]], 1 - c, r)
                self.first.append(copy(6 * w + j, _rows(src_refs[w].at[chip], c, r),
                                       _rows(out_refs[w].at[chip], c, r), (*ch, c)))
                self.first_landed.append(copy(6 * w + j, theirs, theirs, sibling))
                self.passed.append(copy(6 * w + 3 + j, theirs, theirs, sibling))
                self.passed_landed.append(copy(6 * w + 3 + j, other, other, sibling))

    in_place = True

    @staticmethod
    def semaphores(n):
        return [pltpu.SemaphoreType.DMA((6 * n,)), pltpu.SemaphoreType.DMA((6 * n,))]

    @staticmethod
    def results(bufs):
        return [jax.ShapeDtypeStruct(b.shape, b.dtype) for b in bufs]

    def start(self):
        for cp in self.first:
            cp.start()

    def forward(self):
        for landed, cp in zip(self.first_landed, self.passed):
            landed.wait_recv()
            cp.start()

    def finish(self):
        for landed in self.passed_landed:
            landed.wait_recv()
        for cp in self.first + self.passed:
            cp.wait_send()


class _Swap:
    in_place = False

    def __init__(self, grad_refs, out_refs, send_sems, recv_sems, nrows):
        x, y, c = _position()
        self.copies = [pltpu.make_async_remote_copy(
            src_ref=grad_refs[w].at[:, pl.ds((1 - c) * (r // 2), r // 2), :], dst_ref=out_refs[w],
            send_sem=send_sems.at[w], recv_sem=recv_sems.at[w], device_id=(x, y, 1 - c), device_id_type=MESH)
            for w, r in enumerate(nrows)]

    @staticmethod
    def semaphores(n):
        return [pltpu.SemaphoreType.DMA((n,)), pltpu.SemaphoreType.DMA((n,))]

    @staticmethod
    def results(grads):
        return [jax.ShapeDtypeStruct((4, g.shape[1] // 2, g.shape[2]), g.dtype) for g in grads]

    def start(self):
        for cp in self.copies:
            cp.start()

    def forward(self):
        pass

    def finish(self):
        for cp in self.copies:
            cp.wait()


def _grad_swap(grads, name):
    n = len(grads)

    def body(*refs):
        swap = _Swap(refs[:n], refs[n:2 * n], refs[2 * n], refs[2 * n + 1], [g.shape[1] for g in grads])
        swap.start()
        swap.finish()

    return pl.pallas_call(
        body, name=name, out_shape=_Swap.results(grads), in_specs=_any_specs(n), out_specs=_any_specs(n),
        scratch_shapes=_Swap.semaphores(n),
    )(*grads)


class _Exchange:
    in_place = False

    def __init__(self, part_refs, out_refs, send_sems, recv_sems, nrows=None):
        x, y, c = _position()
        self.copies = [
            pltpu.make_async_remote_copy(
                src_ref=part_refs[w].at[2 * ch[0] + ch[1]], dst_ref=out_refs[w].at[k], send_sem=send_sems.at[3 * w + k],
                recv_sem=recv_sems.at[3 * w + k], device_id=(*ch, c), device_id_type=MESH)
            for w in range(len(part_refs)) for k, ch in enumerate([(1 - x, y), (x, 1 - y), (1 - x, 1 - y)])]

    @staticmethod
    def semaphores(n):
        return [pltpu.SemaphoreType.DMA((3 * n,)), pltpu.SemaphoreType.DMA((3 * n,))]

    @staticmethod
    def out_shapes(parts):
        return [jax.ShapeDtypeStruct((3,) + p.shape[1:], p.dtype) for p in parts]

    results = out_shapes

    def start(self):
        for cp in self.copies:
            cp.start()

    def forward(self):
        pass

    def finish(self):
        for cp in self.copies:
            cp.wait()


def _rope_tables(s):
    rows = s // GRID_W
    inv = 10000.0 ** (-jnp.arange(0, 32, 2, dtype=F32) / 32.0)
    ang_r = jnp.arange(rows, dtype=F32)[:, None] * inv
    ang_c = jnp.arange(GRID_W, dtype=F32)[:, None] * inv
    lane = jnp.arange(128)
    first = (lane % 32) < 16
    by_row = (lane % 64) < 32

    def table(fn):
        tr = jnp.tile(fn(ang_r), (1, 8))[:, None, :]
        tc = jnp.tile(fn(ang_c), (1, 8))[None, :, :]
        return jnp.where(by_row, tr, tc).reshape(s, 128)

    cos, sin = table(jnp.cos), table(jnp.sin)
    return cos, jnp.where(first, -sin, 0.0), jnp.where(first, 0.0, sin)


def _local_step(x, ctx, tgt, mod, modc, nw1, win, qnw, knw, sink, cw, cb, cnw, cnb, wout4, nw2, wup, fcw, fcb, wdown4,
                core=None):
    s = x.shape[0]
    sh1, sc1, g1, sh2, sc2, g2 = [mod[:, D * k:D * (k + 1)] for k in range(6)]
    shc, scc = modc[:, :D], modc[:, D:2 * D]
    cos, sa, sb = _rope_tables(s)
    gi = jnp.arange(128) // 64
    gmat = (gi[:, None] == gi[None, :]).astype(_MX)
    qnw2, knw2 = jnp.tile(qnw, (1, 2)), jnp.tile(knw, (1, 2))
    cw32 = jnp.pad(cw, ((0, 1), (0, 0)))
    fcw2 = jnp.pad(fcw, ((0, 5), (0, 0))).reshape(8, 2, FH).transpose(1, 0, 2)
    fcb2 = fcb.reshape(2, 1, FH)
    winkv = win[:, 512:768]
    sinkv = sink.reshape(8)

    host = (lambda *bufs: ()) if core is None else (lambda *bufs: bufs)
    (proj, h, q0, q1, k, ksw, v, vsw, u0), got = _mix_in(x, nw1, sc1, sh1, win, qnw2, knw2, cos, sa, sb, gmat,
                                                        gather=host(wout4))
    wout = (got[0] if got else wout4).reshape(D, D)
    kvc, hc, kc, kcsw, vc, vcsw = _ctx_kv(ctx, nw1, scc, shc, winkv, knw2, gmat)
    padr = lambda a: jnp.pad(a, ((TQ, TQ), (0, 0)))
    kp, kswp, vp, vswp = padr(k), padr(ksw), padr(v), padr(vsw)
    (o, pex, pstat), got = _attn_fwd(q0, q1, kp, kswp, vp, vswp, kc, kcsw, vc, vcsw, sinkv, gather=host(wup))
    wup = got[0] if got else wup
    u1 = _conv31(u0, cw32, cb)
    x1, mix, cat = _mix_out(o, u1, cnw, cnb, wout, x, g1)
    (h2, up0), got = _ffn_up(x1, nw2, sc2, sh2, wup, gather=host(wdown4))
    wdown = (got[0] if got else wdown4).reshape(FH, D)
    act, gv = _ffn_conv_act(up0, fcw2, fcb2)
    dy, dact, ddn, vec_dn = _ffn_down(act, wdown, x1, g2, tgt)
    loss = (0.5 / D) * jnp.sum(vec_dn[1])

    dup, gfc = _ffn_conv_bwd(up0, gv, dact, fcw2)
    g_wdown = _tn_matmul(act, ddn, FQ, D, "gw_down").reshape(4, FH // 4, D)
    g_wup = _tn_matmul(h2, dup, D, FQ, "gw_up", b_split=True, by_chip=True)
    dx1, dmix, vec_up = _ffn_up_bwd(dup, wup, x1, dy, mix, nw2, sc2, g1)
    g_wout = _tn_matmul(cat, dmix, D, D, "gw_out").reshape(4, D // 4, D)
    late = [g_wout, g_wup, g_wdown]
    (do, du1, vec_ln), from_sib = _mix_out_bwd(dmix, wout, u1, cnw, cnb, swap=() if core is None else late)
    parts = () if core is None else _pair_sum(late, from_sib, core, "grad_pair_sum_late")
    (dga, dgb, gcw), arrived_a = _conv31_bwd(du1, u0, cw32, proj, exchange=parts[0:1] + parts[2:3])
    (dq, dkp, dvp, dkc, dvc, dsink), arrived_b = _attn_bwd(q0, q1, kp, kswp, vp, vswp, kc, kcsw, vc, vcsw, pex, pstat,
                                                           o, do, exchange=parts[1:2])
    if core is not None:
        g_wout, g_wup, g_wdown = zip(parts, [arrived_a[0], arrived_b[0], arrived_a[1]])
    gwinkv, vec_ctx = _ctx_bwd(ctx, nw1, scc, winkv, kvc, hc, dkc, dvc, knw2, gmat)
    gx, dproj, vec_in = _mix_in_bwd(dq, dkp[TQ:TQ + s], dvp[TQ:TQ + s], dga, dgb, proj, x, dx1, win, nw1, sc1,
                                    qnw2, knw2, cos, sa, sb, gmat)
    g_win = _tn_matmul(h, dproj, D, INW, "gw_in", init=gwinkv)

    g_qn = vec_in[3:4, 0:128]
    g_kn = vec_in[3:4, 128:256] + vec_ctx[3:4, 0:128]
    grads = dict(
        norm_mix_w=vec_in[2:3] + vec_ctx[2:3], w_in=g_win,
        q_norm_w=g_qn[:, :64] + g_qn[:, 64:], k_norm_w=g_kn[:, :64] + g_kn[:, 64:],
        sink_logit=dsink[:, 0].reshape(1, 8), conv_w=gcw[:CK], conv_b=gcw[CK:CK + 1],
        conv_norm_w=vec_ln[1:2], conv_norm_b=vec_ln[0:1], w_out=g_wout, norm_ffn_w=vec_up[2:3], w_up=g_wup,
        ffn_conv_w=gfc[:, 0:3].transpose(1, 0, 2).reshape(3, 2 * FH), ffn_conv_b=gfc[:, 3].reshape(1, 2 * FH),
        w_down=g_wdown)
    dmod = jnp.concatenate([vec_in[0:1], vec_in[1:2], vec_up[3:4], vec_up[0:1], vec_up[1:2], vec_dn[0:1]], axis=1)
    dmodc = jnp.concatenate([vec_ctx[0:1], vec_ctx[1:2]], axis=1)
    return loss, gx, grads, dmod, dmodc


_SMALL = ["c_ctx", "b_mod", "norm_mix_w", "q_norm_w", "k_norm_w", "sink_logit", "conv_w", "conv_b", "conv_norm_w",
          "conv_norm_b", "norm_ffn_w", "ffn_conv_w", "ffn_conv_b"]
_GATHERED = ["w_in", "w_out", "w_up", "w_down"]
_BIG = ["w_mod"] + _GATHERED
_ORDER = ["c_ctx", "w_mod", "b_mod", "norm_mix_w", "w_in", "q_norm_w", "k_norm_w", "sink_logit", "conv_w", "conv_b",
          "conv_norm_w", "conv_norm_b", "w_out", "norm_ffn_w", "w_up", "ffn_conv_w", "ffn_conv_b", "w_down"]
_PACK = [("norm_mix_w", 1), ("norm_ffn_w", 1), ("q_norm_w", 1), ("k_norm_w", 1), ("sink_logit", 1), ("conv_b", 1),
         ("conv_norm_w", 1), ("conv_norm_b", 1), ("ffn_conv_b", 6), ("conv_w", 16), ("ffn_conv_w", 17)]
_PACK_ROWS = 56


def _pack_rows(a, nrows):
    flat = a.reshape(-1)
    return jnp.pad(flat, (0, nrows * D - flat.shape[0])).reshape(nrows, D)


def kernel(x, c, ctx, c_ctx, w_mod, b_mod, norm_mix_w, w_in, q_norm_w, k_norm_w, sink_logit, conv_w, conv_b, conv_norm_w, conv_norm_b, w_out, norm_ffn_w, w_up, ffn_conv_w, ffn_conv_b, w_down, loss_target, m_c_ctx, m_w_mod, m_b_mod, m_norm_mix_w, m_w_in, m_q_norm_w, m_k_norm_w, m_sink_logit, m_conv_w, m_conv_b, m_conv_norm_w, m_conv_norm_b, m_w_out, m_norm_ffn_w, m_w_up, m_ffn_conv_w, m_ffn_conv_b, m_w_down, v_c_ctx, v_w_mod, v_b_mod, v_norm_mix_w, v_w_in, v_q_norm_w, v_k_norm_w, v_sink_logit, v_conv_w, v_conv_b, v_conv_norm_w, v_conv_norm_b, v_w_out, v_norm_ffn_w, v_w_up, v_ffn_conv_w, v_ffn_conv_b, v_w_down):
    w = dict(c_ctx=c_ctx.reshape(1, D), w_mod=w_mod[0], b_mod=b_mod, norm_mix_w=norm_mix_w, w_in=w_in[0],
             q_norm_w=q_norm_w, k_norm_w=k_norm_w, sink_logit=sink_logit, conv_w=conv_w[0], conv_b=conv_b,
             conv_norm_w=conv_norm_w, conv_norm_b=conv_norm_b, w_out=w_out[0], norm_ffn_w=norm_ffn_w, w_up=w_up[0],
             ffn_conv_w=ffn_conv_w[0], ffn_conv_b=ffn_conv_b, w_down=w_down[0])
    m = dict(c_ctx=m_c_ctx.reshape(1, D), w_mod=m_w_mod[0], b_mod=m_b_mod, norm_mix_w=m_norm_mix_w, w_in=m_w_in[0],
             q_norm_w=m_q_norm_w, k_norm_w=m_k_norm_w, sink_logit=m_sink_logit, conv_w=m_conv_w[0], conv_b=m_conv_b,
             conv_norm_w=m_conv_norm_w, conv_norm_b=m_conv_norm_b, w_out=m_w_out[0], norm_ffn_w=m_norm_ffn_w,
             w_up=m_w_up[0], ffn_conv_w=m_ffn_conv_w[0], ffn_conv_b=m_ffn_conv_b, w_down=m_w_down[0])
    v = dict(c_ctx=v_c_ctx.reshape(1, D), w_mod=v_w_mod[0], b_mod=v_b_mod, norm_mix_w=v_norm_mix_w, w_in=v_w_in[0],
             q_norm_w=v_q_norm_w, k_norm_w=v_k_norm_w, sink_logit=v_sink_logit, conv_w=v_conv_w[0], conv_b=v_conv_b,
             conv_norm_w=v_conv_norm_w, conv_norm_b=v_conv_norm_b, w_out=v_w_out[0], norm_ffn_w=v_norm_ffn_w,
             w_up=v_w_up[0], ffn_conv_w=v_ffn_conv_w[0], ffn_conv_b=v_ffn_conv_b, w_down=v_w_down[0])
    xi, yi, ci = _position()
    chip = 2 * xi + yi
    dev = 2 * chip + ci
    s = x.shape[1]
    ncol = w["w_mod"].shape[1]

    win_buf, wout_buf, wup_buf, wdown_buf = _cast_shards([w[n] for n in _GATHERED], chip)

    blk0 = jnp.concatenate([_pack_rows(w["conv_w"], 4), _pack_rows(w["ffn_conv_w"], 5), c,
                            jnp.zeros((6, D), F32)], axis=0)
    g0, _, (win4,) = _small_allgather(blk0, "gather_c_convw", gather=[win_buf])
    win = win4.transpose(1, 0, 2).reshape(D, INW)
    g0 = g0.reshape(8, 16, D)
    c_all = g0[:, 9, :]
    cs = jnp.concatenate([c_all, w["c_ctx"], jnp.zeros((7, D), F32)], axis=0)
    cw_full = jnp.concatenate([g0[2 * j, 0:4].reshape(-1)[:CK * 128].reshape(CK, 128) for j in range(4)], axis=1)
    fcw_full = jnp.concatenate([g0[2 * j, 4:9].reshape(-1)[:3 * 1408].reshape(3, 1408) for j in range(4)], axis=1)

    b_loc = lax.dynamic_slice(w["b_mod"], (0, chip * ncol), (1, ncol))
    modp = _mod_fwd(cs, w["w_mod"], b_loc)
    gm = _small_allgather(modp, "gather_mod")[0].reshape(8, 16, ncol)
    mod_all = jnp.concatenate([gm[2 * j] for j in range(4)], axis=1)
    mod = lax.dynamic_slice(mod_all, (dev, 0), (1, 6 * D))
    modc = mod_all[8:9]

    loss_loc, gx, gl, dmod, dmodc = _local_step(
        x[0], ctx[0], loss_target[0], mod, modc, w["norm_mix_w"], win, w["q_norm_w"], w["k_norm_w"], w["sink_logit"],
        cw_full, w["conv_b"], w["conv_norm_w"], w["conv_norm_b"], wout_buf, w["norm_ffn_w"], wup_buf, fcw_full,
        w["ffn_conv_b"], wdown_buf, core=ci)

    last = [gl["w_in"].reshape(D, 4, INW // 4).transpose(1, 0, 2)]
    parts = _pair_sum(last, _grad_swap(last, "grad_swap_in"), ci, "grad_pair_sum_in")

    pack = jnp.concatenate([dmod.reshape(6, D), dmodc.reshape(2, D)]
                           + [_pack_rows(gl[name], nr) for name, nr in _PACK], axis=0)
    pack = jnp.pad(pack, ((0, _PACK_ROWS - pack.shape[0]), (0, 0)))
    last_row = lax.broadcasted_iota(jnp.int32, pack.shape, 0) == _PACK_ROWS - 1
    pack = jnp.where(last_row, loss_loc, pack)
    gp = _small_allgather(pack, "gather_small_grads")[0].reshape(8, _PACK_ROWS, D)
    tot = _sum_leading(gp, "sum_small_grads")
    loss = tot[_PACK_ROWS - 1, 0]
    g = {}
    r = 8
    for name, nr in _PACK:
        shape = gl[name].shape
        g[name] = tot[r:r + nr].reshape(-1)[:math.prod(shape)].reshape(shape)
        r += nr
    dmod_all = jnp.concatenate([gp[:, 0:6].reshape(8, 6 * D),
                                jnp.pad(tot[6:8].reshape(1, 2 * D), ((0, 0), (0, 4 * D))),
                                jnp.zeros((7, 6 * D), F32)], axis=0)
    g["b_mod"] = _sum_leading(dmod_all.reshape(16, 1, 6 * D), "grad_b_mod")
    g["w_mod"], part = _mod_bwd(cs, lax.dynamic_slice(dmod_all, (0, chip * ncol), (16, ncol)), w["w_mod"])
    delta, new_m, new_v = {}, {}, {}
    (delta["w_mod"], new_m["w_mod"], new_v["w_mod"]), arrived = _adam_big(
        w["w_mod"], g["w_mod"], m["w_mod"], v["w_mod"], "adam_w_mod", exchange=parts)
    late = [gl[n] for n in ("w_out", "w_up", "w_down")]
    shard_grads = _chip_sum([*parts, *[p for p, _ in late]], [*arrived, *[a for _, a in late]], chip, ci)
    cparts, _, (g["w_in"], g["w_out"], g["w_up"], g["w_down"]) = _small_allgather(part, "gather_cctx",
                                                                                 share=shard_grads)
    g["c_ctx"] = _cctx_grad(cparts.reshape(8, 16, D)[0::2], w["c_ctx"])
    g["conv_w"] = lax.dynamic_slice(g["conv_w"], (0, chip * 128), (CK, 128))
    g["ffn_conv_w"] = lax.dynamic_slice(g["ffn_conv_w"], (0, chip * 1408), (3, 1408))

    for name in _GATHERED:
        delta[name], new_m[name], new_v[name] = _adam_big(w[name], g[name], m[name], v[name], "adam_" + name)
    ds, ms, vs = _adam_small([w[n] for n in _SMALL], [g[n] for n in _SMALL], [m[n] for n in _SMALL],
                             [v[n] for n in _SMALL])
    for k, name in enumerate(_SMALL):
        delta[name], new_m[name], new_v[name] = ds[k], ms[k], vs[k]

    def shaped(d, name):
        a = d[name]
        if name == "c_ctx":
            return a.reshape(D)
        if name in ("w_mod", "w_in", "w_out", "w_up", "w_down", "conv_w", "ffn_conv_w"):
            return a[None]
        return a

    outs = [loss, gx[None]]
    for d in (g, delta, new_m, new_v):
        outs += [shaped(d, name) for name in _ORDER]
    return tuple(outs)
```

```python
import functools
import math

import jax
import jax.numpy as jnp
from jax import lax
from jax.experimental import pallas as pl
from jax.experimental.pallas import tpu as pltpu

F32 = jnp.float32
_MX = jnp.bfloat16
EPS = 1e-6
NEG = -1e30
D = 1024
AW = 512
CW = 512
INW = 1792
FH = 2816
LC = 256
CK = 31
GRID_W = 64
TM = 512
TQ = 128
NCH = FH // 128
FQ = 2 * FH // 4
VMEM_LIMIT = 56 * 1024 * 1024
MESH = pl.DeviceIdType.MESH

ADAM_LR, ADAM_B1, ADAM_B2, ADAM_EPS, ADAM_WD, ADAM_STEP = 0.001, 0.9, 0.999, 1e-08, 0.01, 10


def _cp(*sem):
    return pltpu.CompilerParams(dimension_semantics=sem, vmem_limit_bytes=VMEM_LIMIT)


def _vspec():
    return pl.BlockSpec(memory_space=pltpu.VMEM)


def _sig(z):
    return 1.0 / (1.0 + jnp.exp(-z))


def _dot(a, b):
    return jnp.dot(a, b, preferred_element_type=F32)


def _dot_nt(a, b):
    return lax.dot_general(a, b, (((1,), (1,)), ((), ())), preferred_element_type=F32)


def _dot_tn(a, b):
    return lax.dot_general(a, b, (((0,), (0,)), ((), ())), preferred_element_type=F32)


def _gsum(v, g):
    hi = v.astype(_MX)
    lo = (v - hi.astype(F32)).astype(_MX)
    return _dot(hi, g) + _dot(lo, g)


def _headnorm(ch, w, g):
    r = lax.rsqrt(_gsum(ch * ch, g) * (1.0 / 64.0) + EPS)
    return ch * r * w


def _headnorm_bwd(ch, dy, w, g):
    r = lax.rsqrt(_gsum(ch * ch, g) * (1.0 / 64.0) + EPS)
    hat = ch * r
    dhat = dy * w
    dch = r * (dhat - hat * (_gsum(dhat * hat, g) * (1.0 / 64.0)))
    return dch, dy * hat


def _rope(u, cos, sa, sb):
    return u * cos + pltpu.roll(u, 112, 1) * sa + pltpu.roll(u, 16, 1) * sb


def _rope_bwd(d, cos, sa, sb):
    return d * cos - pltpu.roll(d, 112, 1) * sa - pltpu.roll(d, 16, 1) * sb


def _rowsum(v):
    return jnp.sum(v, axis=0, keepdims=True)


def _call_hosting_gather(body, gather, step_of, nsteps, *, in_specs, out_specs, out_shape, scratch_shapes=(),
                         comm=None, **kw):
    comm = comm or _Gather
    ng, n_in, n_out = len(gather), len(in_specs), len(out_specs)
    if not ng:
        return pl.pallas_call(body, in_specs=in_specs, out_specs=out_specs, out_shape=out_shape,
                              scratch_shapes=list(scratch_shapes), **kw)

    def hosted(*refs):
        ins, outs = refs[:n_in], refs[n_in + ng:n_in + ng + n_out]
        rest = refs[n_in + 2 * ng + n_out:]
        copies = comm(refs[n_in:n_in + ng], refs[n_in + ng + n_out:n_in + 2 * ng + n_out], rest[0], rest[1],
                      [b.shape[1] for b in gather])
        step = step_of()
        pl.when(step == 0)(copies.start)
        body(*ins, *outs, *rest[2:])
        pl.when(step == max(nsteps - 3, 0))(copies.forward)
        pl.when(step == nsteps - 1)(copies.finish)

    return pl.pallas_call(
        hosted, in_specs=list(in_specs) + _any_specs(ng), out_specs=list(out_specs) + _any_specs(ng),
        out_shape=list(out_shape) + comm.results(gather),
        input_output_aliases={n_in + w: n_out + w for w in range(ng)} if comm.in_place else {},
        scratch_shapes=comm.semaphores(ng) + list(scratch_shapes), **kw)


def _mix_in(x, nw, sc, sh, win, qnw2, knw2, cos, sa, sb, gmat, gather=()):
    s = x.shape[0]

    def body(x_ref, nw_ref, sc_ref, sh_ref, win_ref, qnw_ref, knw_ref, cos_ref, sa_ref, sb_ref, g_ref,
             proj_ref, h_ref, q0_ref, q1_ref, k_ref, ksw_ref, v_ref, vsw_ref, u0_ref):
        xv = x_ref[...]
        r = lax.rsqrt(jnp.mean(xv * xv, axis=-1, keepdims=True) + EPS)
        h = xv * r * (nw_ref[...] * (1.0 + sc_ref[...])) + sh_ref[...]
        hb = h.astype(_MX)
        h_ref[...] = hb
        proj = _dot(hb, win_ref[...])
        proj_ref[...] = proj.astype(_MX)
        cs, sav, sbv, g = cos_ref[...], sa_ref[...], sb_ref[...], g_ref[...]
        lo = lax.broadcasted_iota(jnp.int32, (1, 128), 1) < 64
        for j in range(4):
            q = _rope(_headnorm(proj[:, 128 * j:128 * (j + 1)], qnw_ref[...], g), cs, sav, sbv) * 0.125
            q0_ref[:, 128 * j:128 * (j + 1)] = jnp.where(lo, q, 0.0).astype(_MX)
            q1_ref[:, 128 * j:128 * (j + 1)] = jnp.where(lo, 0.0, q).astype(_MX)
        k = _rope(_headnorm(proj[:, 512:640], knw_ref[...], g), cs, sav, sbv)
        k_ref[...] = k.astype(_MX)
        ksw_ref[...] = pltpu.roll(k, 64, 1).astype(_MX)
        v = proj[:, 640:768]
        v_ref[...] = v.astype(_MX)
        vsw_ref[...] = pltpu.roll(v, 64, 1).astype(_MX)
        u0_ref[...] = proj[:, 768:1280] * _sig(proj[:, 1280:1792])

    row = lambda n: pl.BlockSpec((1, n), lambda i: (0, 0))
    tile = lambda n: pl.BlockSpec((TM, n), lambda i: (i, 0))
    res = _call_hosting_gather(
        body, gather, lambda: pl.program_id(0), s // TM, name="mix_in", grid=(s // TM,),
        in_specs=[tile(D), row(D), row(D), row(D), pl.BlockSpec((D, INW), lambda i: (0, 0)), row(128), row(128),
                  tile(128), tile(128), tile(128), pl.BlockSpec((128, 128), lambda i: (0, 0))],
        out_specs=[tile(INW), tile(D), tile(AW), tile(AW), tile(128), tile(128), tile(128), tile(128), tile(CW)],
        out_shape=[jax.ShapeDtypeStruct((s, INW), _MX), jax.ShapeDtypeStruct((s, D), _MX),
                   jax.ShapeDtypeStruct((s, AW), _MX), jax.ShapeDtypeStruct((s, AW), _MX),
                   jax.ShapeDtypeStruct((s, 128), _MX), jax.ShapeDtypeStruct((s, 128), _MX),
                   jax.ShapeDtypeStruct((s, 128), _MX), jax.ShapeDtypeStruct((s, 128), _MX),
                   jax.ShapeDtypeStruct((s, CW), F32)],
        compiler_params=_cp("arbitrary"),
    )(x, nw, sc, sh, win, qnw2, knw2, cos, sa, sb, gmat, *gather)
    return res[:9], res[9:]


def _ctx_kv(ctx, nw, scc, shc, winkv, knw2, gmat):
    def body(ctx_ref, nw_ref, sc_ref, sh_ref, w_ref, knw_ref, g_ref,
             kvc_ref, hc_ref, kc_ref, kcsw_ref, vc_ref, vcsw_ref):
        cv = ctx_ref[...]
        r = lax.rsqrt(jnp.mean(cv * cv, axis=-1, keepdims=True) + EPS)
        hc = (cv * r * (nw_ref[...] * (1.0 + sc_ref[...])) + sh_ref[...]).astype(_MX)
        hc_ref[...] = hc
        kvc = _dot(hc, w_ref[...])
        kvc_ref[...] = kvc
        kc = _headnorm(kvc[:, :128], knw_ref[...], g_ref[...])
        kc_ref[...] = kc.astype(_MX)
        kcsw_ref[...] = pltpu.roll(kc, 64, 1).astype(_MX)
        vc = kvc[:, 128:]
        vc_ref[...] = vc.astype(_MX)
        vcsw_ref[...] = pltpu.roll(vc, 64, 1).astype(_MX)

    return pl.pallas_call(
        body, name="ctx_kv",
        in_specs=[_vspec()] * 7, out_specs=[_vspec()] * 6,
        out_shape=[jax.ShapeDtypeStruct((LC, 256), F32), jax.ShapeDtypeStruct((LC, D), _MX)]
        + [jax.ShapeDtypeStruct((LC, 128), _MX)] * 4,
        compiler_params=pltpu.CompilerParams(vmem_limit_bytes=VMEM_LIMIT),
    )(ctx, nw, scc, shc, winkv, knw2, gmat)


def _attn_mask(i, s):
    r = lax.broadcasted_iota(jnp.int32, (2 * TQ, LC + 3 * TQ), 0) % TQ
    cidx = lax.broadcasted_iota(jnp.int32, (2 * TQ, LC + 3 * TQ), 1)
    qpos = i * TQ + r
    kpos = (i - 1) * TQ + (cidx - LC)
    near = (jnp.abs(qpos - kpos) <= 128) & (kpos >= 0) & (kpos < s)
    return (cidx < LC) | near


def _attn_bias(i, s):
    return jnp.where(_attn_mask(i, s), 0.0, NEG)


def _attn_exp(sc, bias, sinkv):
    sc = sc + bias
    m = jnp.maximum(jnp.max(sc, axis=-1, keepdims=True), sinkv)
    ex = jnp.exp(sc - m)
    es = jnp.exp(sinkv - m)
    return ex, es, 1.0 / (jnp.sum(ex, axis=-1, keepdims=True) + es)


def _sink_rows(sink_ref, g, e):
    return jnp.concatenate([jnp.full((TQ, 1), sink_ref[4 * g + e], F32),
                            jnp.full((TQ, 1), sink_ref[4 * g + 2 + e], F32)], axis=0)


def _attn_fwd(q0, q1, kp, kswp, vp, vswp, kc, kcsw, vc, vcsw, sink, gather=()):
    s = q0.shape[0]
    nb = s // TQ

    def body(q0_ref, q1_ref, kp_ref, kswp_ref, vp_ref, vswp_ref, kc_ref, kcsw_ref, vc_ref, vcsw_ref, sink_ref,
             o_ref, ex_ref, stat_ref):
        i = pl.program_id(0)
        st = pl.multiple_of(i * TQ, TQ)
        kall = (jnp.concatenate([kc_ref[...], kp_ref[pl.ds(st, 3 * TQ), :]], axis=0),
                jnp.concatenate([kcsw_ref[...], kswp_ref[pl.ds(st, 3 * TQ), :]], axis=0))
        vall = (jnp.concatenate([vc_ref[...], vp_ref[pl.ds(st, 3 * TQ), :]], axis=0),
                jnp.concatenate([vcsw_ref[...], vswp_ref[pl.ds(st, 3 * TQ), :]], axis=0))
        bias = _attn_bias(i, s)
        lo = lax.broadcasted_iota(jnp.int32, (1, 128), 1) < 64
        qrefs = (q0_ref, q1_ref)
        out = [jnp.zeros((TQ, 128), F32) for _ in range(4)]
        combos = [(g, e) for g in range(2) for e in range(2)]
        scores = [_dot_nt(jnp.concatenate([qrefs[e][:, 256 * g:256 * g + 128],
                                           qrefs[e][:, 256 * g + 128:256 * g + 256]], axis=0),
                          kall[0 if e == g else 1]) for g, e in combos]
        parts = []
        for n, ((g, e), sc) in enumerate(zip(combos, scores)):
            ex, es, inv = _attn_exp(sc, bias, _sink_rows(sink_ref, g, e))
            parts.append((ex.astype(_MX), inv))
            ex_ref[n] = parts[-1][0]
            stat_ref[n] = jnp.where(lo, inv, es * inv)
        for (g, e), (ex, inv) in zip(combos, parts):
            me = lo if e == 0 else jnp.logical_not(lo)
            o2 = _dot(ex, vall[0 if e == g else 1]) * inv
            out[2 * g] = out[2 * g] + jnp.where(me, o2[:TQ], 0.0)
            out[2 * g + 1] = out[2 * g + 1] + jnp.where(me, o2[TQ:], 0.0)
        for j in range(4):
            o_ref[:, 128 * j:128 * (j + 1)] = out[j]

    full = lambda a: pl.BlockSpec(a.shape, lambda i: (0, 0))
    qs = pl.BlockSpec((TQ, AW), lambda i: (i, 0))
    res = _call_hosting_gather(
        body, gather, lambda: pl.program_id(0), nb, name="attn_fwd", grid=(nb,),
        in_specs=[qs, qs, full(kp), full(kp), full(kp), full(kp), full(kc), full(kc), full(kc), full(kc),
                  pl.BlockSpec(memory_space=pltpu.SMEM)],
        out_specs=[qs, pl.BlockSpec((None, 4, 2 * TQ, LC + 3 * TQ), lambda i: (i, 0, 0, 0)),
                   pl.BlockSpec((None, 4, 2 * TQ, 128), lambda i: (i, 0, 0, 0))],
        out_shape=[jax.ShapeDtypeStruct((s, AW), F32), jax.ShapeDtypeStruct((nb, 4, 2 * TQ, LC + 3 * TQ), _MX),
                   jax.ShapeDtypeStruct((nb, 4, 2 * TQ, 128), F32)],
        compiler_params=_cp("arbitrary"),
    )(q0, q1, kp, kswp, vp, vswp, kc, kcsw, vc, vcsw, sink, *gather)
    return res[:3], res[3:]


def _conv31(u0, cw32, cb):
    s = u0.shape[0]
    rch = 256

    def body(u_ref, w_ref, b_ref, o_ref, pad_ref):
        pad_ref[0:16, :] = jnp.zeros((16, 128), F32)
        pad_ref[s + 16:s + 32, :] = jnp.zeros((16, 128), F32)
        pad_ref[16:s + 16, :] = u_ref[...]
        for cidx in range(s // rch):
            base = cidx * rch
            acc = jnp.zeros((rch, 128), F32) + b_ref[...]
            for j in range(CK):
                acc = acc + w_ref[j:j + 1, :] * pad_ref[base + j + 1:base + j + 1 + rch, :]
            o_ref[base:base + rch, :] = acc

    return pl.pallas_call(
        body, name="conv31", grid=(CW // 128,),
        in_specs=[pl.BlockSpec((s, 128), lambda c: (0, c)), pl.BlockSpec((32, 128), lambda c: (0, c)),
                  pl.BlockSpec((1, 128), lambda c: (0, c))],
        out_specs=pl.BlockSpec((s, 128), lambda c: (0, c)),
        out_shape=jax.ShapeDtypeStruct((s, CW), F32),
        scratch_shapes=[pltpu.VMEM((s + 32, 128), F32)],
        compiler_params=_cp("arbitrary"),
    )(u0, cw32, cb)


def _ln_stats(u1):
    mu = jnp.mean(u1, axis=-1, keepdims=True)
    xc = u1 - mu
    rstd = lax.rsqrt(jnp.mean(xc * xc, axis=-1, keepdims=True) + EPS)
    return xc * rstd, rstd


def _mix_out(o, u1, cnw, cnb, wout, x, g1):
    s = x.shape[0]

    def body(o_ref, u1_ref, cnw_ref, cnb_ref, w_ref, x_ref, g1_ref, x1_ref, mix_ref, cat_ref):
        u2n, _ = _ln_stats(u1_ref[...])
        u2 = u2n * cnw_ref[...] + cnb_ref[...]
        u3 = u2 * _sig(u2)
        cat = jnp.concatenate([o_ref[...], u3], axis=1).astype(_MX)
        cat_ref[...] = cat
        mix = _dot(cat, w_ref[...])
        mix_ref[...] = mix.astype(_MX)
        x1_ref[...] = x_ref[...] + g1_ref[...] * mix

    row = lambda n: pl.BlockSpec((1, n), lambda i: (0, 0))
    tile = lambda n: pl.BlockSpec((TM, n), lambda i: (i, 0))
    return pl.pallas_call(
        body, name="mix_out", grid=(s // TM,),
        in_specs=[tile(AW), tile(CW), row(CW), row(CW), pl.BlockSpec((D, D), lambda i: (0, 0)), tile(D), row(D)],
        out_specs=[tile(D), tile(D), tile(D)],
        out_shape=[jax.ShapeDtypeStruct((s, D), F32), jax.ShapeDtypeStruct((s, D), _MX),
                   jax.ShapeDtypeStruct((s, D), _MX)],
        compiler_params=_cp("arbitrary"),
    )(o, u1, cnw, cnb, wout, x, g1)


def _ffn_up(x1, nw, sc, sh, wup4, gather=()):
    s = x1.shape[0]
    tm = min(1024, s)

    def body(x_ref, nw_ref, sc_ref, sh_ref, w_ref, h2_ref, up_ref, h2s):
        @pl.when(pl.program_id(1) == 0)
        def _():
            xv = x_ref[...]
            r = lax.rsqrt(jnp.mean(xv * xv, axis=-1, keepdims=True) + EPS)
            h2 = (xv * r * (nw_ref[...] * (1.0 + sc_ref[...])) + sh_ref[...]).astype(_MX)
            h2s[...] = h2
            h2_ref[...] = h2
        up_ref[...] = _dot(h2s[...], w_ref[...]).astype(_MX)

    row = pl.BlockSpec((1, D), lambda i, j: (0, 0))
    res = _call_hosting_gather(
        body, gather, lambda: pl.program_id(0) * 4 + pl.program_id(1), 4 * (s // tm), name="ffn_up",
        grid=(s // tm, 4),
        in_specs=[pl.BlockSpec((tm, D), lambda i, j: (i, 0)), row, row, row,
                  pl.BlockSpec((None, D, FQ), lambda i, j: (j, 0, 0))],
        out_specs=[pl.BlockSpec((tm, D), lambda i, j: (i, 0)),
                   pl.BlockSpec((None, tm, FQ), lambda i, j: (j // 2, i, j % 2))],
        out_shape=[jax.ShapeDtypeStruct((s, D), _MX), jax.ShapeDtypeStruct((2, s, FH), _MX)],
        scratch_shapes=[pltpu.VMEM((tm, D), _MX)],
        compiler_params=_cp("arbitrary", "arbitrary"),
    )(x1, nw, sc, sh, wup4, *gather)
    return res[:2], res[2:]


def _fill_pad8(pad_ref, val, s):
    pad_ref[0:8, :] = jnp.zeros((8, 128), F32)
    pad_ref[s + 8:s + 16, :] = jnp.zeros((8, 128), F32)
    pad_ref[8:s + 8, :] = val


def _conv3_at(pad_ref, w_ref, half, base, rch):
    return (w_ref[half, 0:1, :] * pad_ref[base + 7:base + 7 + rch, :]
            + w_ref[half, 1:2, :] * pad_ref[base + 8:base + 8 + rch, :]
            + w_ref[half, 2:3, :] * pad_ref[base + 9:base + 9 + rch, :])


def _ffn_conv_act(up0, fcw, fcb):
    s = up0.shape[1]
    rch = 256

    def body(up_ref, w_ref, b_ref, act_ref, gv_ref, padg, padv):
        _fill_pad8(padg, up_ref[0].astype(F32), s)
        _fill_pad8(padv, up_ref[1].astype(F32), s)
        for cidx in range(s // rch):
            base = cidx * rch
            gate = _conv3_at(padg, w_ref, 0, base, rch) + b_ref[0]
            val = _conv3_at(padv, w_ref, 1, base, rch) + b_ref[1]
            act_ref[base:base + rch, :] = (gate * _sig(gate) * val).astype(_MX)
            gv_ref[0, base:base + rch, :] = gate.astype(_MX)
            gv_ref[1, base:base + rch, :] = val.astype(_MX)

    return pl.pallas_call(
        body, name="ffn_conv_act", grid=(NCH,),
        in_specs=[pl.BlockSpec((2, s, 128), lambda c: (0, 0, c)), pl.BlockSpec((2, 8, 128), lambda c: (0, 0, c)),
                  pl.BlockSpec((2, 1, 128), lambda c: (0, 0, c))],
        out_specs=[pl.BlockSpec((s, 128), lambda c: (0, c)), pl.BlockSpec((2, s, 128), lambda c: (0, 0, c))],
        out_shape=[jax.ShapeDtypeStruct((s, FH), _MX), jax.ShapeDtypeStruct((2, s, FH), _MX)],
        scratch_shapes=[pltpu.VMEM((s + 16, 128), F32)] * 2,
        compiler_params=_cp("arbitrary"),
    )(up0, fcw, fcb)


def _ffn_down(act, wdown, x1, g2, tgt):
    s = x1.shape[0]

    def body(act_ref, w_ref, x1_ref, g2_ref, tgt_ref, dy_ref, dact_ref, ddn_ref, vec_ref):
        @pl.when(pl.program_id(0) == 0)
        def _():
            vec_ref[...] = jnp.zeros((8, D), F32)
        dn = _dot(act_ref[...], w_ref[...])
        diff = x1_ref[...] + g2_ref[...] * dn - tgt_ref[...]
        dy = diff * (1.0 / D)
        dy_ref[...] = dy
        ddn = (dy * g2_ref[...]).astype(_MX)
        ddn_ref[...] = ddn
        dact_ref[...] = _dot_nt(ddn, w_ref[...]).astype(_MX)
        vec_ref[0:1, :] += _rowsum(dy * dn)
        vec_ref[1:2, :] += _rowsum(diff * diff)

    tile = lambda n: pl.BlockSpec((TM, n), lambda i: (i, 0))
    return pl.pallas_call(
        body, name="ffn_down", grid=(s // TM,),
        in_specs=[tile(FH), pl.BlockSpec((FH, D), lambda i: (0, 0)), tile(D), pl.BlockSpec((1, D), lambda i: (0, 0)),
                  tile(D)],
        out_specs=[tile(D), tile(FH), tile(D), pl.BlockSpec((8, D), lambda i: (0, 0))],
        out_shape=[jax.ShapeDtypeStruct((s, D), F32), jax.ShapeDtypeStruct((s, FH), _MX),
                   jax.ShapeDtypeStruct((s, D), _MX), jax.ShapeDtypeStruct((8, D), F32)],
        compiler_params=_cp("arbitrary"),
    )(act, wdown, x1, g2, tgt)


def _ffn_conv_bwd(up0, gv, dact, fcw):
    s = up0.shape[1]
    rch = 256

    def body(up_ref, gv_ref, da_ref, w_ref, dup_ref, gw_ref, dpg, dpv):
        for p in (dpg, dpv):
            p[0:8, :] = jnp.zeros((8, 128), F32)
            p[s + 8:s + 16, :] = jnp.zeros((8, 128), F32)
        for cidx in range(s // rch):
            rows = slice(cidx * rch, (cidx + 1) * rch)
            gate = gv_ref[0, rows, :].astype(F32)
            da = da_ref[rows, :].astype(F32)
            sg = _sig(gate)
            silu = gate * sg
            dpg[8 + cidx * rch:8 + (cidx + 1) * rch, :] = da * gv_ref[1, rows, :].astype(F32) * (sg + silu * (1.0 - sg))
            dpv[8 + cidx * rch:8 + (cidx + 1) * rch, :] = da * silu
        acc = [[jnp.zeros((1, 128), F32) for _ in range(4)] for _ in range(2)]
        for cidx in range(s // rch):
            base = cidx * rch
            for half, dp in enumerate((dpg, dpv)):
                shifted = [dp[base + 9 - j:base + 9 - j + rch, :] for j in range(3)]
                dup_ref[half, base:base + rch, :] = (w_ref[half, 0:1, :] * shifted[0] + w_ref[half, 1:2, :] * shifted[1]
                                                     + w_ref[half, 2:3, :] * shifted[2]).astype(_MX)
                xv = up_ref[half, base:base + rch, :].astype(F32)
                for j in range(3):
                    acc[half][j] = acc[half][j] + _rowsum(xv * shifted[j])
                acc[half][3] = acc[half][3] + _rowsum(shifted[1])
        for half in range(2):
            gw_ref[half] = jnp.zeros((8, 128), F32)
            for j in range(4):
                gw_ref[half, j:j + 1, :] = acc[half][j]

    both = pl.BlockSpec((2, s, 128), lambda c: (0, 0, c))
    return pl.pallas_call(
        body, name="ffn_conv_bwd", grid=(NCH,),
        in_specs=[both, both, pl.BlockSpec((s, 128), lambda c: (0, c)), pl.BlockSpec((2, 8, 128), lambda c: (0, 0, c))],
        out_specs=[both, pl.BlockSpec((2, 8, 128), lambda c: (0, 0, c))],
        out_shape=[jax.ShapeDtypeStruct((2, s, FH), _MX), jax.ShapeDtypeStruct((2, 8, FH), F32)],
        scratch_shapes=[pltpu.VMEM((s + 16, 128), F32)] * 2,
        compiler_params=_cp("arbitrary"),
    )(up0, gv, dact, fcw)


def _tn_matmul(a, b, tm, tn, name, b_split=False, by_chip=False, init=None):
    s, m = a.shape
    n = 2 * b.shape[2] if b_split else b.shape[1]
    ts = min(1024, s)
    nsteps = s // ts
    npb = (n // 2) // tn if b_split else None

    def body(*refs):
        if init is None:
            a_ref, b_ref, o_ref, acc = refs
        else:
            a_ref, b_ref, i_ref, o_ref, acc = refs
        k = pl.program_id(2)

        @pl.when(k == 0)
        def _():
            acc[...] = jnp.zeros((tm, tn), F32)
            if init is not None:
                acc[:, 512:768] = i_ref[...]
        acc[...] += _dot_tn(a_ref[...], b_ref[...])

        @pl.when(k == nsteps - 1)
        def _():
            o_ref[...] = acc[...].astype(_MX)

    if b_split:
        bspec = pl.BlockSpec((None, ts, tn), lambda i, j, k: (j // npb, k, j % npb))
    else:
        bspec = pl.BlockSpec((ts, tn), lambda i, j, k: (k, j))
    in_specs = [pl.BlockSpec((ts, tm), lambda i, j, k: (k, i)), bspec]
    args = [a, b]
    if init is not None:
        in_specs.append(pl.BlockSpec((tm, 256), lambda i, j, k: (i, 0)))
        args.append(init)
    if by_chip:
        out_spec = pl.BlockSpec((None, tm, tn), lambda i, j, k: (j, i, 0))
        out_shape = jax.ShapeDtypeStruct((n // tn, m, tn), _MX)
    else:
        out_spec = pl.BlockSpec((tm, tn), lambda i, j, k: (i, j))
        out_shape = jax.ShapeDtypeStruct((m, n), _MX)
    return pl.pallas_call(
        body, name=name, grid=(m // tm, n // tn, nsteps),
        in_specs=in_specs, out_specs=out_spec, out_shape=out_shape,
        scratch_shapes=[pltpu.VMEM((tm, tn), F32)],
        compiler_params=_cp("arbitrary", "arbitrary", "arbitrary"),
    )(*args)


def _ffn_up_bwd(dup, wup4, x1, dy, mix, nw, sc, g1):
    s = x1.shape[0]
    nk = 2

    def body(dup_ref, w_ref, x1_ref, dy_ref, mix_ref, nw_ref, sc_ref, g1_ref, dx1_ref, dmix_ref, vec_ref, acc):
        i, k = pl.program_id(0), pl.program_id(1)

        @pl.when(k == 0)
        def _():
            acc[...] = jnp.zeros((TM, D), F32)

        @pl.when((k == 0) & (i == 0))
        def _():
            vec_ref[...] = jnp.zeros((8, D), F32)
        acc[...] += _dot_nt(dup_ref[:, :FQ], w_ref[0]) + _dot_nt(dup_ref[:, FQ:], w_ref[1])

        @pl.when(k == nk - 1)
        def _():
            dh = acc[...]
            xv = x1_ref[...]
            r = lax.rsqrt(jnp.mean(xv * xv, axis=-1, keepdims=True) + EPS)
            xn = xv * r
            nwv, scv = nw_ref[...], sc_ref[...]
            vec_ref[0:1, :] += _rowsum(dh)
            vec_ref[1:2, :] += _rowsum(dh * xn) * nwv
            vec_ref[2:3, :] += _rowsum(dh * xn) * (1.0 + scv)
            dxn = dh * (nwv * (1.0 + scv))
            dx1 = dy_ref[...] + r * (dxn - xn * jnp.mean(dxn * xn, axis=-1, keepdims=True))
            dx1_ref[...] = dx1
            vec_ref[3:4, :] += _rowsum(dx1 * mix_ref[...].astype(F32))
            dmix_ref[...] = (dx1 * g1_ref[...]).astype(_MX)

    tile = pl.BlockSpec((TM, D), lambda i, k: (i, 0))
    row = pl.BlockSpec((1, D), lambda i, k: (0, 0))
    return pl.pallas_call(
        body, name="ffn_up_bwd", grid=(s // TM, nk),
        in_specs=[pl.BlockSpec((None, TM, FH), lambda i, k: (k, i, 0)),
                  pl.BlockSpec((2, D, FQ), lambda i, k: (k, 0, 0)), tile, tile, tile, row, row, row],
        out_specs=[tile, tile, pl.BlockSpec((8, D), lambda i, k: (0, 0))],
        out_shape=[jax.ShapeDtypeStruct((s, D), F32), jax.ShapeDtypeStruct((s, D), _MX),
                   jax.ShapeDtypeStruct((8, D), F32)],
        scratch_shapes=[pltpu.VMEM((TM, D), F32)],
        compiler_params=_cp("arbitrary", "arbitrary"),
    )(dup, wup4, x1, dy, mix, nw, sc, g1)


def _mix_out_bwd(dmix, wout, u1, cnw, cnb, swap=()):
    s = u1.shape[0]

    def body(dm_ref, w_ref, u1_ref, cnw_ref, cnb_ref, do_ref, du1_ref, vec_ref):
        @pl.when(pl.program_id(0) == 0)
        def _():
            vec_ref[...] = jnp.zeros((8, CW), F32)
        dcat = _dot_nt(dm_ref[...], w_ref[...])
        do_ref[...] = dcat[:, :AW]
        du3 = dcat[:, AW:]
        u2n, rstd = _ln_stats(u1_ref[...])
        u2 = u2n * cnw_ref[...] + cnb_ref[...]
        sg = _sig(u2)
        du2 = du3 * sg * (1.0 + u2 * (1.0 - sg))
        vec_ref[0:1, :] += _rowsum(du2)
        vec_ref[1:2, :] += _rowsum(du2 * u2n)
        d2n = du2 * cnw_ref[...]
        du1_ref[...] = rstd * (d2n - jnp.mean(d2n, axis=-1, keepdims=True)
                               - u2n * jnp.mean(d2n * u2n, axis=-1, keepdims=True))

    row = lambda n: pl.BlockSpec((1, n), lambda i: (0, 0))
    tile = lambda n: pl.BlockSpec((TM, n), lambda i: (i, 0))
    res = _call_hosting_gather(
        body, swap, lambda: pl.program_id(0), s // TM, comm=_Swap, name="mix_out_bwd", grid=(s // TM,),
        in_specs=[tile(D), pl.BlockSpec((D, D), lambda i: (0, 0)), tile(CW), row(CW), row(CW)],
        out_specs=[tile(AW), tile(CW), pl.BlockSpec((8, CW), lambda i: (0, 0))],
        out_shape=[jax.ShapeDtypeStruct((s, AW), F32), jax.ShapeDtypeStruct((s, CW), F32),
                   jax.ShapeDtypeStruct((8, CW), F32)],
        compiler_params=_cp("arbitrary"),
    )(dmix, wout, u1, cnw, cnb, *swap)
    return res[:3], res[3:]


def _conv31_bwd(du1, u0, cw32, proj, exchange=()):
    s = u0.shape[0]
    rch = 128

    def body(d_ref, u_ref, w_ref, ga_ref, gb_ref, dga_ref, dgb_ref, gw_ref, padd):
        padd[0:16, :] = jnp.zeros((16, 128), F32)
        padd[s + 16:s + 32, :] = jnp.zeros((16, 128), F32)
        padd[16:s + 16, :] = d_ref[...]
        gw_ref[...] = jnp.zeros((32, 128), F32)
        gw_ref[CK:CK + 1, :] = _rowsum(d_ref[...])
        for cidx in range(s // rch):
            base = cidx * rch
            uv = u_ref[base:base + rch, :]
            du0 = jnp.zeros((rch, 128), F32)
            for j in range(CK):
                shifted = padd[base + 31 - j:base + 31 - j + rch, :]
                du0 = du0 + w_ref[j:j + 1, :] * shifted
                gw_ref[j:j + 1, :] += _rowsum(uv * shifted)
            sg = _sig(gb_ref[base:base + rch, :].astype(F32))
            ga = ga_ref[base:base + rch, :].astype(F32)
            dga_ref[base:base + rch, :] = (du0 * sg).astype(_MX)
            dgb_ref[base:base + rch, :] = (du0 * ga * sg * (1.0 - sg)).astype(_MX)

    blk = lambda off: pl.BlockSpec((s, 128), lambda c: (0, c + off))
    res = _call_hosting_gather(
        body, exchange, lambda: pl.program_id(0), CW // 128, comm=_Exchange, name="conv31_bwd", grid=(CW // 128,),
        in_specs=[blk(0), blk(0), pl.BlockSpec((32, 128), lambda c: (0, c)), blk(6), blk(10)],
        out_specs=[blk(0), blk(0), pl.BlockSpec((32, 128), lambda c: (0, c))],
        out_shape=[jax.ShapeDtypeStruct((s, CW), _MX), jax.ShapeDtypeStruct((s, CW), _MX),
                   jax.ShapeDtypeStruct((32, CW), F32)],
        scratch_shapes=[pltpu.VMEM((s + 32, 128), F32)],
        compiler_params=_cp("arbitrary"),
    )(du1, u0, cw32, proj, proj, *exchange)
    return res[:3], res[3:]


def _attn_bwd(q0, q1, kp, kswp, vp, vswp, kc, kcsw, vc, vcsw, ex, stat, o, do, exchange=()):
    s = q0.shape[0]
    nb = s // TQ
    ne = len(exchange)

    def body(*refs):
        (q0_ref, q1_ref, kp_ref, kswp_ref, vp_ref, vswp_ref, kc_ref, kcsw_ref, vc_ref, vcsw_ref, ex_ref, stat_ref,
         o_ref, do_ref) = refs[:14]
        dq_ref, dk_ref, dv_ref, dkc_ref, dvc_ref, dsink_ref = refs[14 + ne:20 + ne]
        i = pl.program_id(0)
        if ne:
            exch = _Exchange(refs[14:14 + ne], refs[20 + ne:20 + 2 * ne], refs[20 + 2 * ne], refs[21 + 2 * ne])
            pl.when(i == 0)(exch.start)

        @pl.when(i == 0)
        def _():
            dk_ref[...] = jnp.zeros((s + 2 * TQ, 128), F32)
            dv_ref[...] = jnp.zeros((s + 2 * TQ, 128), F32)
            dkc_ref[...] = jnp.zeros((LC, 128), F32)
            dvc_ref[...] = jnp.zeros((LC, 128), F32)
            dsink_ref[...] = jnp.zeros((8, 128), F32)
        st = pl.multiple_of(i * TQ, TQ)
        kall = (jnp.concatenate([kc_ref[...], kp_ref[pl.ds(st, 3 * TQ), :]], axis=0),
                jnp.concatenate([kcsw_ref[...], kswp_ref[pl.ds(st, 3 * TQ), :]], axis=0))
        vall = (jnp.concatenate([vc_ref[...], vp_ref[pl.ds(st, 3 * TQ), :]], axis=0),
                jnp.concatenate([vcsw_ref[...], vswp_ref[pl.ds(st, 3 * TQ), :]], axis=0))
        lo = lax.broadcasted_iota(jnp.int32, (1, 128), 1) < 64
        qrefs = (q0_ref, q1_ref)
        dq = [jnp.zeros((TQ, 128), F32) for _ in range(4)]
        dkt = jnp.zeros((128, LC + 3 * TQ), F32)
        dvt = jnp.zeros((128, LC + 3 * TQ), F32)
        combos = [(g, e) for g in range(2) for e in range(2)]

        def stacked(ref, g):
            return jnp.concatenate([ref[:, 256 * g:256 * g + 128], ref[:, 256 * g + 128:256 * g + 256]], axis=0)

        for n, (g, e) in enumerate(combos):
            me = lo if e == 0 else jnp.logical_not(lo)
            sw = 0 if e == g else 1
            qm = stacked(qrefs[e], g)
            p = ex_ref[n].astype(F32) * stat_ref[n, :, 0:1]
            dom = jnp.where(me, stacked(do_ref, g), 0.0)
            dd = jnp.sum(dom * stacked(o_ref, g), axis=-1, keepdims=True)
            domx = dom.astype(_MX)
            sd = stat_ref[n, :, 64:65] * dd
            dsink_ref[4 * g + e:4 * g + e + 1, :] -= jnp.sum(sd[:TQ], axis=0, keepdims=True)
            dsink_ref[4 * g + 2 + e:4 * g + 3 + e, :] -= jnp.sum(sd[TQ:], axis=0, keepdims=True)
            ds = (p * (_dot_nt(domx, vall[sw]) - dd)).astype(_MX)
            dq2 = _dot(ds, kall[sw])
            dq[2 * g] = dq[2 * g] + jnp.where(me, dq2[:TQ], 0.0)
            dq[2 * g + 1] = dq[2 * g + 1] + jnp.where(me, dq2[TQ:], 0.0)
            dk2 = _dot_tn(qm, ds)
            dv2 = _dot_tn(domx, p.astype(_MX))
            if sw:
                dk2 = pltpu.roll(dk2, 64, 0)
                dv2 = pltpu.roll(dv2, 64, 0)
            dkt = dkt + dk2
            dvt = dvt + dv2
        dkt = dkt.T
        dvt = dvt.T
        for j in range(4):
            dq_ref[:, 128 * j:128 * (j + 1)] = dq[j] * 0.125
        dkc_ref[...] += dkt[:LC]
        dvc_ref[...] += dvt[:LC]
        dk_ref[pl.ds(st, 3 * TQ), :] += dkt[LC:]
        dv_ref[pl.ds(st, 3 * TQ), :] += dvt[LC:]
        if ne:
            pl.when(i == nb - 1)(exch.finish)

    full = lambda a: pl.BlockSpec(a.shape, lambda i: (0, 0))
    fs = lambda r: pl.BlockSpec((r, 128), lambda i: (0, 0))
    qs = pl.BlockSpec((TQ, AW), lambda i: (i, 0))
    res = pl.pallas_call(
        body, name="attn_bwd", grid=(nb,),
        in_specs=[qs, qs, full(kp), full(kp), full(kp), full(kp), full(kc), full(kc), full(kc), full(kc),
                  pl.BlockSpec((None, 4, 2 * TQ, LC + 3 * TQ), lambda i: (i, 0, 0, 0)),
                  pl.BlockSpec((None, 4, 2 * TQ, 128), lambda i: (i, 0, 0, 0)), qs, qs] + _any_specs(ne),
        out_specs=[qs, fs(s + 2 * TQ), fs(s + 2 * TQ), fs(LC), fs(LC), fs(8)] + _any_specs(ne),
        out_shape=[jax.ShapeDtypeStruct((s, AW), F32), jax.ShapeDtypeStruct((s + 2 * TQ, 128), F32),
                   jax.ShapeDtypeStruct((s + 2 * TQ, 128), F32), jax.ShapeDtypeStruct((LC, 128), F32),
                   jax.ShapeDtypeStruct((LC, 128), F32), jax.ShapeDtypeStruct((8, 128), F32)]
        + _Exchange.out_shapes(exchange),
        scratch_shapes=_Exchange.semaphores(ne) if ne else [],
        compiler_params=_cp("arbitrary"),
    )(q0, q1, kp, kswp, vp, vswp, kc, kcsw, vc, vcsw, ex, stat, o, do, *exchange)
    return res[:6], res[6:]


def _mix_in_bwd(dq, dk, dv, dga, dgb, proj, x, dx1, win, nw, sc, qnw2, knw2, cos, sa, sb, gmat):
    s = x.shape[0]

    def body(dq_ref, dk_ref, dv_ref, dga_ref, dgb_ref, proj_ref, x_ref, dx1_ref, win_ref, nw_ref, sc_ref,
             qnw_ref, knw_ref, cos_ref, sa_ref, sb_ref, g_ref, gx_ref, dproj_ref, vec_ref):
        @pl.when(pl.program_id(0) == 0)
        def _():
            vec_ref[...] = jnp.zeros((8, D), F32)
        cs, sav, sbv, g = cos_ref[...], sa_ref[...], sb_ref[...], g_ref[...]
        gq = jnp.zeros((1, 128), F32)
        for j in range(4):
            dqn = _rope_bwd(dq_ref[:, 128 * j:128 * (j + 1)], cs, sav, sbv)
            dch, gw = _headnorm_bwd(proj_ref[:, 128 * j:128 * (j + 1)].astype(F32), dqn, qnw_ref[...], g)
            dproj_ref[:, 128 * j:128 * (j + 1)] = dch.astype(_MX)
            gq = gq + _rowsum(gw)
        dkn = _rope_bwd(dk_ref[...], cs, sav, sbv)
        dch, gw = _headnorm_bwd(proj_ref[:, 512:640].astype(F32), dkn, knw_ref[...], g)
        dproj_ref[:, 512:640] = dch.astype(_MX)
        dproj_ref[:, 640:768] = dv_ref[...].astype(_MX)
        dproj_ref[:, 768:1280] = dga_ref[...]
        dproj_ref[:, 1280:1792] = dgb_ref[...]
        vec_ref[3:4, 0:128] += gq
        vec_ref[3:4, 128:256] += _rowsum(gw)
        dh = _dot_nt(dproj_ref[...], win_ref[...])
        xv = x_ref[...]
        r = lax.rsqrt(jnp.mean(xv * xv, axis=-1, keepdims=True) + EPS)
        xn = xv * r
        nwv, scv = nw_ref[...], sc_ref[...]
        vec_ref[0:1, :] += _rowsum(dh)
        vec_ref[1:2, :] += _rowsum(dh * xn) * nwv
        vec_ref[2:3, :] += _rowsum(dh * xn) * (1.0 + scv)
        dxn = dh * (nwv * (1.0 + scv))
        gx_ref[...] = dx1_ref[...] + r * (dxn - xn * jnp.mean(dxn * xn, axis=-1, keepdims=True))

    row = lambda n: pl.BlockSpec((1, n), lambda i: (0, 0))
    tile = lambda n: pl.BlockSpec((TM, n), lambda i: (i, 0))
    return pl.pallas_call(
        body, name="mix_in_bwd", grid=(s // TM,),
        in_specs=[tile(AW), tile(128), tile(128), tile(CW), tile(CW), tile(INW), tile(D), tile(D),
                  pl.BlockSpec((D, INW), lambda i: (0, 0)), row(D), row(D), row(128), row(128),
                  tile(128), tile(128), tile(128), pl.BlockSpec((128, 128), lambda i: (0, 0))],
        out_specs=[tile(D), tile(INW), pl.BlockSpec((8, D), lambda i: (0, 0))],
        out_shape=[jax.ShapeDtypeStruct((s, D), F32), jax.ShapeDtypeStruct((s, INW), _MX),
                   jax.ShapeDtypeStruct((8, D), F32)],
        compiler_params=_cp("arbitrary"),
    )(dq, dk, dv, dga, dgb, proj, x, dx1, win, nw, sc, qnw2, knw2, cos, sa, sb, gmat)


def _ctx_bwd(ctx, nw, scc, winkv, kvc, hc, dkc, dvc, knw2, gmat):
    def body(ctx_ref, nw_ref, sc_ref, w_ref, kvc_ref, hc_ref, dkc_ref, dvc_ref, knw_ref, g_ref, gw_ref, vec_ref):
        dkr, gk = _headnorm_bwd(kvc_ref[:, 0:128], dkc_ref[...], knw_ref[...], g_ref[...])
        dkv = jnp.concatenate([dkr, dvc_ref[...]], axis=1).astype(_MX)
        gw_ref[...] = _dot_tn(hc_ref[...], dkv)
        dh = _dot_nt(dkv, w_ref[...])
        cv = ctx_ref[...]
        r = lax.rsqrt(jnp.mean(cv * cv, axis=-1, keepdims=True) + EPS)
        cn = cv * r
        vec_ref[...] = jnp.zeros((8, D), F32)
        vec_ref[0:1, :] = _rowsum(dh)
        vec_ref[1:2, :] = _rowsum(dh * cn) * nw_ref[...]
        vec_ref[2:3, :] = _rowsum(dh * cn) * (1.0 + sc_ref[...])
        vec_ref[3:4, 0:128] = _rowsum(gk)

    return pl.pallas_call(
        body, name="ctx_bwd", in_specs=[_vspec()] * 10, out_specs=[_vspec()] * 2,
        out_shape=[jax.ShapeDtypeStruct((D, 256), F32), jax.ShapeDtypeStruct((8, D), F32)],
        compiler_params=pltpu.CompilerParams(vmem_limit_bytes=VMEM_LIMIT),
    )(ctx, nw, scc, winkv, kvc, hc, dkc, dvc, knw2, gmat)


def _mod_fwd(cs, wmod, bloc):
    def body(c_ref, w_ref, b_ref, o_ref):
        cv = c_ref[...]
        o_ref[...] = _dot((cv * _sig(cv)).astype(_MX), w_ref[...].astype(_MX)) + b_ref[...]

    return pl.pallas_call(
        body, name="mod_fwd", in_specs=[_vspec()] * 3, out_specs=_vspec(),
        out_shape=jax.ShapeDtypeStruct((16, wmod.shape[1]), F32),
        compiler_params=pltpu.CompilerParams(vmem_limit_bytes=VMEM_LIMIT),
    )(cs, wmod, bloc)


def _mod_bwd(cs, dmod, wmod):
    def body(c_ref, d_ref, w_ref, gw_ref, part_ref):
        cv = c_ref[...]
        sl = (cv * _sig(cv)).astype(_MX)
        dm = d_ref[...].astype(_MX)
        gw_ref[...] = _dot_tn(sl, dm)
        part_ref[...] = _dot_nt(dm, w_ref[...].astype(_MX))

    return pl.pallas_call(
        body, name="mod_bwd", in_specs=[_vspec()] * 3, out_specs=[_vspec()] * 2,
        out_shape=[jax.ShapeDtypeStruct(wmod.shape, F32), jax.ShapeDtypeStruct((16, D), F32)],
        compiler_params=pltpu.CompilerParams(vmem_limit_bytes=VMEM_LIMIT),
    )(cs, dmod, wmod)


def _sum_leading(a, name):
    n = a.shape[0]

    def body(a_ref, o_ref):
        acc = a_ref[0]
        for k in range(1, n):
            acc = acc + a_ref[k]
        o_ref[...] = acc

    return pl.pallas_call(
        body, name=name, in_specs=[_vspec()], out_specs=_vspec(),
        out_shape=jax.ShapeDtypeStruct(a.shape[1:], F32),
        compiler_params=pltpu.CompilerParams(vmem_limit_bytes=VMEM_LIMIT),
    )(a)


def _cctx_grad(parts, cc):
    def body(p_ref, c_ref, o_ref):
        acc = p_ref[0, 8:9, :]
        for k in range(1, 4):
            acc = acc + p_ref[k, 8:9, :]
        cv = c_ref[...]
        sg = _sig(cv)
        o_ref[...] = acc * (sg * (1.0 + cv * (1.0 - sg)))

    return pl.pallas_call(
        body, name="cctx_grad", in_specs=[_vspec()] * 2, out_specs=_vspec(),
        out_shape=jax.ShapeDtypeStruct((1, D), F32),
    )(parts, cc)


def _adam_math(w, g, m, v):
    mn = ADAM_B1 * m + (1.0 - ADAM_B1) * g
    vn = ADAM_B2 * v + (1.0 - ADAM_B2) * (g * g)
    mh = mn / (1.0 - ADAM_B1 ** ADAM_STEP)
    vh = vn / (1.0 - ADAM_B2 ** ADAM_STEP)
    delta = -ADAM_LR * (mh / (jnp.sqrt(vh) + ADAM_EPS) + ADAM_WD * w)
    return delta, mn, vn


def _adam_big(w, g, m, v, name):
    r, n = w.shape
    tr = 256 if r % 256 == 0 else 64

    def body(w_ref, g_ref, m_ref, v_ref, d_ref, mo_ref, vo_ref):
        d, mn, vn = _adam_math(w_ref[...], g_ref[...], m_ref[...], v_ref[...])
        d_ref[...] = d
        mo_ref[...] = mn
        vo_ref[...] = vn

    spec = pl.BlockSpec((tr, n), lambda i: (i, 0))
    return pl.pallas_call(
        body, name=name, grid=(r // tr,), in_specs=[spec] * 4, out_specs=[spec] * 3,
        out_shape=[jax.ShapeDtypeStruct((r, n), F32)] * 3,
        compiler_params=_cp("arbitrary"),
    )(w, g, m, v)


def _adam_small(ws, gs, ms, vs):
    n = len(ws)

    def body(*refs):
        ins, outs = refs[:4 * n], refs[4 * n:]
        for k in range(n):
            d, mn, vn = _adam_math(ins[k][...], ins[n + k][...], ins[2 * n + k][...], ins[3 * n + k][...])
            outs[k][...] = d
            outs[n + k][...] = mn
            outs[2 * n + k][...] = vn

    shapes = [jax.ShapeDtypeStruct(w.shape, F32) for w in ws]
    res = pl.pallas_call(
        body, name="adam_small", in_specs=[_vspec()] * (4 * n), out_specs=[_vspec()] * (3 * n),
        out_shape=shapes * 3,
    )(*ws, *gs, *ms, *vs)
    return res[:n], res[n:2 * n], res[2 * n:]


def _pair_sum(grads, from_sib, core, name):
    n = len(grads)

    def body(c_ref, *refs):
        for w in range(n):
            a_ref, b_ref, o_ref = refs[w], refs[n + w], refs[2 * n + w]
            o_ref[...] = (a_ref[...].astype(F32) + b_ref[...].astype(F32)).astype(_MX)

    halves = [(None, g.shape[1] // 2, g.shape[2]) for g in grads]
    return pl.pallas_call(
        body, name=name,
        grid_spec=pltpu.PrefetchScalarGridSpec(
            num_scalar_prefetch=1, grid=(4,),
            in_specs=[pl.BlockSpec(h, lambda j, c: (j, c[0], 0)) for h in halves]
            + [pl.BlockSpec(h, lambda j, c: (j, 0, 0)) for h in halves],
            out_specs=[pl.BlockSpec(h, lambda j, c: (j, 0, 0)) for h in halves]),
        out_shape=[jax.ShapeDtypeStruct(p.shape, _MX) for p in from_sib],
        compiler_params=_cp("arbitrary"),
    )(core.reshape(1), *grads, *from_sib)


def _chip_sum(parts, arrived, chip, core):
    n = len(parts)

    def body(s_ref, *refs):
        for w in range(n):
            own_ref, p_ref, o_ref = refs[w], refs[n + w], refs[2 * n + w]
            acc = own_ref[...].astype(F32)
            for k in range(3):
                acc = acc + p_ref[k].astype(F32)
            o_ref[...] = acc

    blk = [(p.shape[1] // 2, p.shape[2]) for p in parts]
    return pl.pallas_call(
        body, name="grad_chip_sum",
        grid_spec=pltpu.PrefetchScalarGridSpec(
            num_scalar_prefetch=1, grid=(2,),
            in_specs=[pl.BlockSpec((None,) + b, lambda i, s: (s[0], i, 0)) for b in blk]
            + [pl.BlockSpec((3,) + b, lambda i, s: (0, i, 0)) for b in blk],
            out_specs=[pl.BlockSpec(b, lambda i, s: (2 * s[1] + i, 0)) for b in blk]),
        out_shape=[jax.ShapeDtypeStruct((2 * p.shape[1], p.shape[2]), F32) for p in parts],
        compiler_params=_cp("arbitrary"),
    )(jnp.stack([chip, core]), *parts, *arrived)


def _cast_shards(ws, chip):
    n = len(ws)

    def body(s_ref, *refs):
        for w in range(n):
            refs[n + w][...] = refs[w][...].astype(_MX)

    blk = [(a.shape[0] // 2, a.shape[1]) for a in ws]
    return pl.pallas_call(
        body, name="cast_shards",
        grid_spec=pltpu.PrefetchScalarGridSpec(
            num_scalar_prefetch=1, grid=(2,),
            in_specs=[pl.BlockSpec(b, lambda i, s: (i, 0)) for b in blk],
            out_specs=[pl.BlockSpec((None,) + b, lambda i, s: (s[0], i, 0)) for b in blk]),
        out_shape=[jax.ShapeDtypeStruct((4,) + a.shape, _MX) for a in ws],
        compiler_params=_cp("arbitrary"),
    )(chip.reshape(1), *ws)


def _position():
    x, y, c = lax.axis_index("x"), lax.axis_index("y"), lax.axis_index("c")
    return x, y, c


def _small_allgather(blk, name, exchange=(), share=(), gather=()):
    m_per, n = blk.shape
    assert not (share and gather)
    share = tuple(share) + tuple(gather)
    ne, ns = len(exchange), len(share)

    def body(*refs):
        x_ref = refs[0]
        out_ref = refs[1 + ne + ns]
        send_sems, recv_sems = refs[2 + 2 * ne + 2 * ns], refs[3 + 2 * ne + 2 * ns]
        extra = refs[4 + 2 * ne + 2 * ns:]
        x, y, c = _position()
        me, sibling = (x, y, c), (x, y, 1 - c)
        chips = [(1 - x, y), (x, 1 - y), (1 - x, 1 - y)]
        if ne:
            exch = _Exchange(refs[1:1 + ne], refs[2 + ne + ns:2 + 2 * ne + ns], extra[0], extra[1])
            exch.start()
        if gather:
            gat = _Gather(refs[1 + ne:1 + ne + ns], refs[2 + 2 * ne + ns:2 + 2 * ne + 2 * ns],
                          extra[2 * bool(ne)], extra[2 * bool(ne) + 1], [b.shape[1] for b in gather])
            gat.start()
        elif ns:
            halves = _share_copies(refs[1 + ne:1 + ne + ns], refs[2 + 2 * ne + ns:2 + 2 * ne + 2 * ns],
                                   extra[2 * bool(ne)], extra[2 * bool(ne) + 1], [b.shape[0] for b in share])
            for cp in halves:
                cp.start()

        def rows(px, py, pc):
            return out_ref.at[pl.ds(pl.multiple_of((4 * px + 2 * py + pc) * m_per, 8), m_per), :]

        def copy(k, block, to, src=None):
            return pltpu.make_async_remote_copy(
                src_ref=rows(*block) if src is None else src, dst_ref=rows(*block),
                send_sem=send_sems.at[k], recv_sem=recv_sems.at[k], device_id=to, device_id_type=MESH)

        out_ref[pl.ds(pl.multiple_of((4 * x + 2 * y + c) * m_per, 8), m_per), :] = x_ref[...]
        first = [copy(0, me, sibling, src=x_ref)]
        first += [copy(1 + j, me, (*chip, c), src=x_ref) for j, chip in enumerate(chips)]
        for cp in first:
            cp.start()
        passed = [copy(4 + j, (*chip, c), sibling) for j, chip in enumerate(chips)]
        for j, chip in enumerate(chips):
            copy(1 + j, (*chip, c), me).wait_recv()
            passed[j].start()
        copy(0, sibling, me).wait_recv()
        for j, chip in enumerate(chips):
            copy(4 + j, (*chip, 1 - c), me).wait_recv()
        for cp in first + passed:
            cp.wait_send()
        if ne:
            exch.finish()
        if gather:
            gat.forward()
            gat.finish()
        elif ns:
            for cp in halves:
                cp.wait()

    res = pl.pallas_call(
        body, name=name,
        out_shape=[jax.ShapeDtypeStruct((8 * m_per, n), blk.dtype)] + _Exchange.out_shapes(exchange)
        + [jax.ShapeDtypeStruct(b.shape, b.dtype) for b in share],
        in_specs=[_vspec()] + _any_specs(ne + ns), out_specs=[_vspec()] + _any_specs(ne + ns),
        input_output_aliases={1 + ne + w: 1 + ne + w for w in range(ns)},
        scratch_shapes=[pltpu.SemaphoreType.DMA((7,)), pltpu.SemaphoreType.DMA((7,))]
        + (_Exchange.semaphores(ne) if ne else [])
        + (_Gather.semaphores(ns) if gather else
           [pltpu.SemaphoreType.DMA((ns,)), pltpu.SemaphoreType.DMA((ns,))] if ns else []),
    )(blk, *exchange, *share)
    return res[0], res[1:1 + ne], res[1 + ne:]


def _share_copies(src_refs, out_refs, send_sems, recv_sems, nrows):
    x, y, c = _position()
    return [pltpu.make_async_remote_copy(
        src_ref=_rows(src_refs[w], c, r), dst_ref=_rows(out_refs[w], c, r), send_sem=send_sems.at[w],
        recv_sem=recv_sems.at[w], device_id=(x, y, 1 - c), device_id_type=MESH) for w, r in enumerate(nrows)]


def _any_specs(n):
    return [pl.BlockSpec(memory_space=pl.ANY)] * n


def _rows(ref, half, nrows):
    return ref.at[pl.ds(half * (nrows // 2), nrows // 2), :]


class _Gather:
    def __init__(self, src_refs, out_refs, send_sems, recv_sems, nrows):
        x, y, c = _position()
        chip = 2 * x + y
        sibling = (x, y, 1 - c)

        def copy(k, src, dst, to):
            return pltpu.make_async_remote_copy(src_ref=src, dst_ref=dst, send_sem=send_sems.at[k],
                                                recv_sem=recv_sems.at[k], device_id=to, device_id_type=MESH)

        self.first, self.first_landed, self.passed, self.passed_landed = [], [], [], []
        for w, r in enumerate(nrows):
            for j, ch in enumerate([(1 - x, y), (x, 1 - y), (1 - x, 1 - y)]):
                theirs = _rows(out_refs[w].at[2 * ch[0] + ch[1]], c, r)
                other = _rows(out_refs[w].at[2 * ch[0] + ch[1]], 1 - c, r)
                self.first.append(copy(6 * w + j, _rows(src_refs[w].at[chip], c, r),
                                       _rows(out_refs[w].at[chip], c, r), (*ch, c)))
                self.first_landed.append(copy(6 * w + j, theirs, theirs, sibling))
                self.passed.append(copy(6 * w + 3 + j, theirs, theirs, sibling))
                self.passed_landed.append(copy(6 * w + 3 + j, other, other, sibling))

    in_place = True

    @staticmethod
    def semaphores(n):
        return [pltpu.SemaphoreType.DMA((6 * n,)), pltpu.SemaphoreType.DMA((6 * n,))]

    @staticmethod
    def results(bufs):
        return [jax.ShapeDtypeStruct(b.shape, b.dtype) for b in bufs]

    def start(self):
        for cp in self.first:
            cp.start()

    def forward(self):
        for landed, cp in zip(self.first_landed, self.passed):
            landed.wait_recv()
            cp.start()

    def finish(self):
        for landed in self.passed_landed:
            landed.wait_recv()
        for cp in self.first + self.passed:
            cp.wait_send()


class _Swap:
    in_place = False

    def __init__(self, grad_refs, out_refs, send_sems, recv_sems, nrows):
        x, y, c = _position()
        self.copies = [pltpu.make_async_remote_copy(
            src_ref=grad_refs[w].at[:, pl.ds((1 - c) * (r // 2), r // 2), :], dst_ref=out_refs[w],
            send_sem=send_sems.at[w], recv_sem=recv_sems.at[w], device_id=(x, y, 1 - c), device_id_type=MESH)
            for w, r in enumerate(nrows)]

    @staticmethod
    def semaphores(n):
        return [pltpu.SemaphoreType.DMA((n,)), pltpu.SemaphoreType.DMA((n,))]

    @staticmethod
    def results(grads):
        return [jax.ShapeDtypeStruct((4, g.shape[1] // 2, g.shape[2]), g.dtype) for g in grads]

    def start(self):
        for cp in self.copies:
            cp.start()

    def forward(self):
        pass

    def finish(self):
        for cp in self.copies:
            cp.wait()


def _grad_swap(grads, name):
    n = len(grads)

    def body(*refs):
        swap = _Swap(refs[:n], refs[n:2 * n], refs[2 * n], refs[2 * n + 1], [g.shape[1] for g in grads])
        swap.start()
        swap.finish()

    return pl.pallas_call(
        body, name=name, out_shape=_Swap.results(grads), in_specs=_any_specs(n), out_specs=_any_specs(n),
        scratch_shapes=_Swap.semaphores(n),
    )(*grads)


class _Exchange:
    in_place = False

    def __init__(self, part_refs, out_refs, send_sems, recv_sems, nrows=None):
        x, y, c = _position()
        self.copies = [
            pltpu.make_async_remote_copy(
                src_ref=part_refs[w].at[2 * ch[0] + ch[1]], dst_ref=out_refs[w].at[k], send_sem=send_sems.at[3 * w + k],
                recv_sem=recv_sems.at[3 * w + k], device_id=(*ch, c), device_id_type=MESH)
            for w in range(len(part_refs)) for k, ch in enumerate([(1 - x, y), (x, 1 - y), (1 - x, 1 - y)])]

    @staticmethod
    def semaphores(n):
        return [pltpu.SemaphoreType.DMA((3 * n,)), pltpu.SemaphoreType.DMA((3 * n,))]

    @staticmethod
    def out_shapes(parts):
        return [jax.ShapeDtypeStruct((3,) + p.shape[1:], p.dtype) for p in parts]

    results = out_shapes

    def start(self):
        for cp in self.copies:
            cp.start()

    def forward(self):
        pass

    def finish(self):
        for cp in self.copies:
            cp.wait()


def _rope_tables(s):
    rows = s // GRID_W
    inv = 10000.0 ** (-jnp.arange(0, 32, 2, dtype=F32) / 32.0)
    ang_r = jnp.arange(rows, dtype=F32)[:, None] * inv
    ang_c = jnp.arange(GRID_W, dtype=F32)[:, None] * inv
    lane = jnp.arange(128)
    first = (lane % 32) < 16
    by_row = (lane % 64) < 32

    def table(fn):
        tr = jnp.tile(fn(ang_r), (1, 8))[:, None, :]
        tc = jnp.tile(fn(ang_c), (1, 8))[None, :, :]
        return jnp.where(by_row, tr, tc).reshape(s, 128)

    cos, sin = table(jnp.cos), table(jnp.sin)
    return cos, jnp.where(first, -sin, 0.0), jnp.where(first, 0.0, sin)


def _local_step(x, ctx, tgt, mod, modc, nw1, win, qnw, knw, sink, cw, cb, cnw, cnb, wout4, nw2, wup, fcw, fcb, wdown4,
                core=None):
    s = x.shape[0]
    sh1, sc1, g1, sh2, sc2, g2 = [mod[:, D * k:D * (k + 1)] for k in range(6)]
    shc, scc = modc[:, :D], modc[:, D:2 * D]
    cos, sa, sb = _rope_tables(s)
    gi = jnp.arange(128) // 64
    gmat = (gi[:, None] == gi[None, :]).astype(_MX)
    qnw2, knw2 = jnp.tile(qnw, (1, 2)), jnp.tile(knw, (1, 2))
    cw32 = jnp.pad(cw, ((0, 1), (0, 0)))
    fcw2 = jnp.pad(fcw, ((0, 5), (0, 0))).reshape(8, 2, FH).transpose(1, 0, 2)
    fcb2 = fcb.reshape(2, 1, FH)
    winkv = win[:, 512:768]
    sinkv = sink.reshape(8)

    host = (lambda *bufs: ()) if core is None else (lambda *bufs: bufs)
    (proj, h, q0, q1, k, ksw, v, vsw, u0), got = _mix_in(x, nw1, sc1, sh1, win, qnw2, knw2, cos, sa, sb, gmat,
                                                        gather=host(wout4))
    wout = (got[0] if got else wout4).reshape(D, D)
    kvc, hc, kc, kcsw, vc, vcsw = _ctx_kv(ctx, nw1, scc, shc, winkv, knw2, gmat)
    padr = lambda a: jnp.pad(a, ((TQ, TQ), (0, 0)))
    kp, kswp, vp, vswp = padr(k), padr(ksw), padr(v), padr(vsw)
    (o, pex, pstat), got = _attn_fwd(q0, q1, kp, kswp, vp, vswp, kc, kcsw, vc, vcsw, sinkv, gather=host(wup))
    wup = got[0] if got else wup
    u1 = _conv31(u0, cw32, cb)
    x1, mix, cat = _mix_out(o, u1, cnw, cnb, wout, x, g1)
    (h2, up0), got = _ffn_up(x1, nw2, sc2, sh2, wup, gather=host(wdown4))
    wdown = (got[0] if got else wdown4).reshape(FH, D)
    act, gv = _ffn_conv_act(up0, fcw2, fcb2)
    dy, dact, ddn, vec_dn = _ffn_down(act, wdown, x1, g2, tgt)
    loss = (0.5 / D) * jnp.sum(vec_dn[1])

    dup, gfc = _ffn_conv_bwd(up0, gv, dact, fcw2)
    g_wdown = _tn_matmul(act, ddn, FQ, D, "gw_down").reshape(4, FH // 4, D)
    g_wup = _tn_matmul(h2, dup, D, FQ, "gw_up", b_split=True, by_chip=True)
    dx1, dmix, vec_up = _ffn_up_bwd(dup, wup, x1, dy, mix, nw2, sc2, g1)
    g_wout = _tn_matmul(cat, dmix, D, D, "gw_out").reshape(4, D // 4, D)
    late = [g_wout, g_wup, g_wdown]
    (do, du1, vec_ln), from_sib = _mix_out_bwd(dmix, wout, u1, cnw, cnb, swap=() if core is None else late)
    parts = () if core is None else _pair_sum(late, from_sib, core, "grad_pair_sum_late")
    (dga, dgb, gcw), arrived_a = _conv31_bwd(du1, u0, cw32, proj, exchange=parts[0:1] + parts[2:3])
    (dq, dkp, dvp, dkc, dvc, dsink), arrived_b = _attn_bwd(q0, q1, kp, kswp, vp, vswp, kc, kcsw, vc, vcsw, pex, pstat,
                                                           o, do, exchange=parts[1:2])
    if core is not None:
        g_wout, g_wup, g_wdown = zip(parts, [arrived_a[0], arrived_b[0], arrived_a[1]])
    gwinkv, vec_ctx = _ctx_bwd(ctx, nw1, scc, winkv, kvc, hc, dkc, dvc, knw2, gmat)
    gx, dproj, vec_in = _mix_in_bwd(dq, dkp[TQ:TQ + s], dvp[TQ:TQ + s], dga, dgb, proj, x, dx1, win, nw1, sc1,
                                    qnw2, knw2, cos, sa, sb, gmat)
    g_win = _tn_matmul(h, dproj, D, INW, "gw_in", init=gwinkv)

    g_qn = vec_in[3:4, 0:128]
    g_kn = vec_in[3:4, 128:256] + vec_ctx[3:4, 0:128]
    grads = dict(
        norm_mix_w=vec_in[2:3] + vec_ctx[2:3], w_in=g_win,
        q_norm_w=g_qn[:, :64] + g_qn[:, 64:], k_norm_w=g_kn[:, :64] + g_kn[:, 64:],
        sink_logit=dsink[:, 0].reshape(1, 8), conv_w=gcw[:CK], conv_b=gcw[CK:CK + 1],
        conv_norm_w=vec_ln[1:2], conv_norm_b=vec_ln[0:1], w_out=g_wout, norm_ffn_w=vec_up[2:3], w_up=g_wup,
        ffn_conv_w=gfc[:, 0:3].transpose(1, 0, 2).reshape(3, 2 * FH), ffn_conv_b=gfc[:, 3].reshape(1, 2 * FH),
        w_down=g_wdown)
    dmod = jnp.concatenate([vec_in[0:1], vec_in[1:2], vec_up[3:4], vec_up[0:1], vec_up[1:2], vec_dn[0:1]], axis=1)
    dmodc = jnp.concatenate([vec_ctx[0:1], vec_ctx[1:2]], axis=1)
    return loss, gx, grads, dmod, dmodc


_SMALL = ["c_ctx", "b_mod", "norm_mix_w", "q_norm_w", "k_norm_w", "sink_logit", "conv_w", "conv_b", "conv_norm_w",
          "conv_norm_b", "norm_ffn_w", "ffn_conv_w", "ffn_conv_b"]
_GATHERED = ["w_in", "w_out", "w_up", "w_down"]
_BIG = ["w_mod"] + _GATHERED
_ORDER = ["c_ctx", "w_mod", "b_mod", "norm_mix_w", "w_in", "q_norm_w", "k_norm_w", "sink_logit", "conv_w", "conv_b",
          "conv_norm_w", "conv_norm_b", "w_out", "norm_ffn_w", "w_up", "ffn_conv_w", "ffn_conv_b", "w_down"]
_PACK = [("norm_mix_w", 1), ("norm_ffn_w", 1), ("q_norm_w", 1), ("k_norm_w", 1), ("sink_logit", 1), ("conv_b", 1),
         ("conv_norm_w", 1), ("conv_norm_b", 1), ("ffn_conv_b", 6), ("conv_w", 16), ("ffn_conv_w", 17)]
_PACK_ROWS = 56


def _pack_rows(a, nrows):
    flat = a.reshape(-1)
    return jnp.pad(flat, (0, nrows * D - flat.shape[0])).reshape(nrows, D)


def kernel(x, c, ctx, c_ctx, w_mod, b_mod, norm_mix_w, w_in, q_norm_w, k_norm_w, sink_logit, conv_w, conv_b, conv_norm_w, conv_norm_b, w_out, norm_ffn_w, w_up, ffn_conv_w, ffn_conv_b, w_down, loss_target, m_c_ctx, m_w_mod, m_b_mod, m_norm_mix_w, m_w_in, m_q_norm_w, m_k_norm_w, m_sink_logit, m_conv_w, m_conv_b, m_conv_norm_w, m_conv_norm_b, m_w_out, m_norm_ffn_w, m_w_up, m_ffn_conv_w, m_ffn_conv_b, m_w_down, v_c_ctx, v_w_mod, v_b_mod, v_norm_mix_w, v_w_in, v_q_norm_w, v_k_norm_w, v_sink_logit, v_conv_w, v_conv_b, v_conv_norm_w, v_conv_norm_b, v_w_out, v_norm_ffn_w, v_w_up, v_ffn_conv_w, v_ffn_conv_b, v_w_down):
    w = dict(c_ctx=c_ctx.reshape(1, D), w_mod=w_mod[0], b_mod=b_mod, norm_mix_w=norm_mix_w, w_in=w_in[0],
             q_norm_w=q_norm_w, k_norm_w=k_norm_w, sink_logit=sink_logit, conv_w=conv_w[0], conv_b=conv_b,
             conv_norm_w=conv_norm_w, conv_norm_b=conv_norm_b, w_out=w_out[0], norm_ffn_w=norm_ffn_w, w_up=w_up[0],
             ffn_conv_w=ffn_conv_w[0], ffn_conv_b=ffn_conv_b, w_down=w_down[0])
    m = dict(c_ctx=m_c_ctx.reshape(1, D), w_mod=m_w_mod[0], b_mod=m_b_mod, norm_mix_w=m_norm_mix_w, w_in=m_w_in[0],
             q_norm_w=m_q_norm_w, k_norm_w=m_k_norm_w, sink_logit=m_sink_logit, conv_w=m_conv_w[0], conv_b=m_conv_b,
             conv_norm_w=m_conv_norm_w, conv_norm_b=m_conv_norm_b, w_out=m_w_out[0], norm_ffn_w=m_norm_ffn_w,
             w_up=m_w_up[0], ffn_conv_w=m_ffn_conv_w[0], ffn_conv_b=m_ffn_conv_b, w_down=m_w_down[0])
    v = dict(c_ctx=v_c_ctx.reshape(1, D), w_mod=v_w_mod[0], b_mod=v_b_mod, norm_mix_w=v_norm_mix_w, w_in=v_w_in[0],
             q_norm_w=v_q_norm_w, k_norm_w=v_k_norm_w, sink_logit=v_sink_logit, conv_w=v_conv_w[0], conv_b=v_conv_b,
             conv_norm_w=v_conv_norm_w, conv_norm_b=v_conv_norm_b, w_out=v_w_out[0], norm_ffn_w=v_norm_ffn_w,
             w_up=v_w_up[0], ffn_conv_w=v_ffn_conv_w[0], ffn_conv_b=v_ffn_conv_b, w_down=v_w_down[0])
    xi, yi, ci = _position()
    chip = 2 * xi + yi
    dev = 2 * chip + ci
    s = x.shape[1]
    ncol = w["w_mod"].shape[1]

    win_buf, wout_buf, wup_buf, wdown_buf = _cast_shards([w[n] for n in _GATHERED], chip)

    blk0 = jnp.concatenate([_pack_rows(w["conv_w"], 4), _pack_rows(w["ffn_conv_w"], 5), c,
                            jnp.zeros((6, D), F32)], axis=0)
    g0, _, (win4,) = _small_allgather(blk0, "gather_c_convw", gather=[win_buf])
    win = win4.transpose(1, 0, 2).reshape(D, INW)
    g0 = g0.reshape(8, 16, D)
    c_all = g0[:, 9, :]
    cs = jnp.concatenate([c_all, w["c_ctx"], jnp.zeros((7, D), F32)], axis=0)
    cw_full = jnp.concatenate([g0[2 * j, 0:4].reshape(-1)[:CK * 128].reshape(CK, 128) for j in range(4)], axis=1)
    fcw_full = jnp.concatenate([g0[2 * j, 4:9].reshape(-1)[:3 * 1408].reshape(3, 1408) for j in range(4)], axis=1)

    b_loc = lax.dynamic_slice(w["b_mod"], (0, chip * ncol), (1, ncol))
    modp = _mod_fwd(cs, w["w_mod"], b_loc)
    gm = _small_allgather(modp, "gather_mod")[0].reshape(8, 16, ncol)
    mod_all = jnp.concatenate([gm[2 * j] for j in range(4)], axis=1)
    mod = lax.dynamic_slice(mod_all, (dev, 0), (1, 6 * D))
    modc = mod_all[8:9]

    loss_loc, gx, gl, dmod, dmodc = _local_step(
        x[0], ctx[0], loss_target[0], mod, modc, w["norm_mix_w"], win, w["q_norm_w"], w["k_norm_w"], w["sink_logit"],
        cw_full, w["conv_b"], w["conv_norm_w"], w["conv_norm_b"], wout_buf, w["norm_ffn_w"], wup_buf, fcw_full,
        w["ffn_conv_b"], wdown_buf, core=ci)

    last = [gl["w_in"].reshape(D, 4, INW // 4).transpose(1, 0, 2)]
    parts = _pair_sum(last, _grad_swap(last, "grad_swap_in"), ci, "grad_pair_sum_in")

    pack = jnp.concatenate([dmod.reshape(6, D), dmodc.reshape(2, D)]
                           + [_pack_rows(gl[name], nr) for name, nr in _PACK], axis=0)
    pack = jnp.pad(pack, ((0, _PACK_ROWS - pack.shape[0]), (0, 0)))
    last_row = lax.broadcasted_iota(jnp.int32, pack.shape, 0) == _PACK_ROWS - 1
    pack = jnp.where(last_row, loss_loc, pack)
    gp, arrived, _ = _small_allgather(pack, "gather_small_grads", exchange=parts)
    gp = gp.reshape(8, _PACK_ROWS, D)
    late = [gl[n] for n in ("w_out", "w_up", "w_down")]
    shard_grads = _chip_sum([*parts, *[p for p, _ in late]], [*arrived, *[a for _, a in late]], chip, ci)
    tot = _sum_leading(gp, "sum_small_grads")
    loss = tot[_PACK_ROWS - 1, 0]
    g = {}
    r = 8
    for name, nr in _PACK:
        shape = gl[name].shape
        g[name] = tot[r:r + nr].reshape(-1)[:math.prod(shape)].reshape(shape)
        r += nr
    dmod_all = jnp.concatenate([gp[:, 0:6].reshape(8, 6 * D),
                                jnp.pad(tot[6:8].reshape(1, 2 * D), ((0, 0), (0, 4 * D))),
                                jnp.zeros((7, 6 * D), F32)], axis=0)
    g["b_mod"] = _sum_leading(dmod_all.reshape(16, 1, 6 * D), "grad_b_mod")
    g["w_mod"], part = _mod_bwd(cs, lax.dynamic_slice(dmod_all, (0, chip * ncol), (16, ncol)), w["w_mod"])
    cparts, _, (g["w_in"], g["w_out"], g["w_up"], g["w_down"]) = _small_allgather(part, "gather_cctx",
                                                                                 share=shard_grads)
    g["c_ctx"] = _cctx_grad(cparts.reshape(8, 16, D)[0::2], w["c_ctx"])
    g["conv_w"] = lax.dynamic_slice(g["conv_w"], (0, chip * 128), (CK, 128))
    g["ffn_conv_w"] = lax.dynamic_slice(g["ffn_conv_w"], (0, chip * 1408), (3, 1408))

    delta, new_m, new_v = {}, {}, {}
    for name in _BIG:
        delta[name], new_m[name], new_v[name] = _adam_big(w[name], g[name], m[name], v[name], "adam_" + name)
    ds, ms, vs = _adam_small([w[n] for n in _SMALL], [g[n] for n in _SMALL], [m[n] for n in _SMALL],
                             [v[n] for n in _SMALL])
    for k, name in enumerate(_SMALL):
        delta[name], new_m[name], new_v[name] = ds[k], ms[k], vs[k]

    def shaped(d, name):
        a = d[name]
        if name == "c_ctx":
            return a.reshape(D)
        if name in ("w_mod", "w_in", "w_out", "w_up", "w_down", "conv_w", "ffn_conv_w"):
            return a[None]
        return a

    outs = [loss, gx[None]]
    for d in (g, delta, new_m, new_v):
        outs += [shaped(d, name) for name in _ORDER]
    return tuple(outs)
```

```python
import functools
import math

import jax
import jax.numpy as jnp
from jax import lax
from jax.experimental import pallas as pl
from jax.experimental.pallas import tpu as pltpu

F32 = jnp.float32
_MX = jnp.bfloat16
EPS = 1e-6
NEG = -1e30
D = 1024
AW = 512
CW = 512
INW = 1792
FH = 2816
LC = 256
CK = 31
GRID_W = 64
TM = 512
TQ = 128
NCH = FH // 128
FQ = 2 * FH // 4
VMEM_LIMIT = 56 * 1024 * 1024
MESH = pl.DeviceIdType.MESH

ADAM_LR, ADAM_B1, ADAM_B2, ADAM_EPS, ADAM_WD, ADAM_STEP = 0.001, 0.9, 0.999, 1e-08, 0.01, 10


def _cp(*sem):
    return pltpu.CompilerParams(dimension_semantics=sem, vmem_limit_bytes=VMEM_LIMIT)


def _vspec():
    return pl.BlockSpec(memory_space=pltpu.VMEM)


def _sig(z):
    return 1.0 / (1.0 + jnp.exp(-z))


def _dot(a, b):
    return jnp.dot(a, b, preferred_element_type=F32)


def _dot_nt(a, b):
    return lax.dot_general(a, b, (((1,), (1,)), ((), ())), preferred_element_type=F32)


def _dot_tn(a, b):
    return lax.dot_general(a, b, (((0,), (0,)), ((), ())), preferred_element_type=F32)


def _gsum(v, g):
    hi = v.astype(_MX)
    lo = (v - hi.astype(F32)).astype(_MX)
    return _dot(hi, g) + _dot(lo, g)


def _headnorm(ch, w, g):
    r = lax.rsqrt(_gsum(ch * ch, g) * (1.0 / 64.0) + EPS)
    return ch * r * w


def _headnorm_bwd(ch, dy, w, g):
    r = lax.rsqrt(_gsum(ch * ch, g) * (1.0 / 64.0) + EPS)
    hat = ch * r
    dhat = dy * w
    dch = r * (dhat - hat * (_gsum(dhat * hat, g) * (1.0 / 64.0)))
    return dch, dy * hat


def _rope(u, cos, sa, sb):
    return u * cos + pltpu.roll(u, 112, 1) * sa + pltpu.roll(u, 16, 1) * sb


def _rope_bwd(d, cos, sa, sb):
    return d * cos - pltpu.roll(d, 112, 1) * sa - pltpu.roll(d, 16, 1) * sb


def _rowsum(v):
    return jnp.sum(v, axis=0, keepdims=True)


def _call_hosting_gather(body, gather, step_of, nsteps, *, in_specs, out_specs, out_shape, scratch_shapes=(),
                         comm=None, **kw):
    comm = comm or _Gather
    ng, n_in, n_out = len(gather), len(in_specs), len(out_specs)
    if not ng:
        return pl.pallas_call(body, in_specs=in_specs, out_specs=out_specs, out_shape=out_shape,
                              scratch_shapes=list(scratch_shapes), **kw)

    def hosted(*refs):
        ins, outs = refs[:n_in], refs[n_in + ng:n_in + ng + n_out]
        rest = refs[n_in + 2 * ng + n_out:]
        copies = comm(refs[n_in:n_in + ng], refs[n_in + ng + n_out:n_in + 2 * ng + n_out], rest[0], rest[1],
                      [b.shape[1] for b in gather])
        step = step_of()
        pl.when(step == 0)(copies.start)
        body(*ins, *outs, *rest[2:])
        pl.when(step == max(nsteps - 3, 0))(copies.forward)
        pl.when(step == nsteps - 1)(copies.finish)

    return pl.pallas_call(
        hosted, in_specs=list(in_specs) + _any_specs(ng), out_specs=list(out_specs) + _any_specs(ng),
        out_shape=list(out_shape) + comm.results(gather),
        input_output_aliases={n_in + w: n_out + w for w in range(ng)} if comm.in_place else {},
        scratch_shapes=comm.semaphores(ng) + list(scratch_shapes), **kw)


def _mix_in(x, nw, sc, sh, win, qnw2, knw2, cos, sa, sb, gmat, gather=()):
    s = x.shape[0]

    def body(x_ref, nw_ref, sc_ref, sh_ref, win_ref, qnw_ref, knw_ref, cos_ref, sa_ref, sb_ref, g_ref,
             proj_ref, h_ref, q0_ref, q1_ref, k_ref, ksw_ref, v_ref, vsw_ref, u0_ref):
        xv = x_ref[...]
        r = lax.rsqrt(jnp.mean(xv * xv, axis=-1, keepdims=True) + EPS)
        h = xv * r * (nw_ref[...] * (1.0 + sc_ref[...])) + sh_ref[...]
        hb = h.astype(_MX)
        h_ref[...] = hb
        proj = _dot(hb, win_ref[...])
        proj_ref[...] = proj.astype(_MX)
        cs, sav, sbv, g = cos_ref[...], sa_ref[...], sb_ref[...], g_ref[...]
        lo = lax.broadcasted_iota(jnp.int32, (1, 128), 1) < 64
        for j in range(4):
            q = _rope(_headnorm(proj[:, 128 * j:128 * (j + 1)], qnw_ref[...], g), cs, sav, sbv) * 0.125
            q0_ref[:, 128 * j:128 * (j + 1)] = jnp.where(lo, q, 0.0).astype(_MX)
            q1_ref[:, 128 * j:128 * (j + 1)] = jnp.where(lo, 0.0, q).astype(_MX)
        k = _rope(_headnorm(proj[:, 512:640], knw_ref[...], g), cs, sav, sbv)
        k_ref[...] = k.astype(_MX)
        ksw_ref[...] = pltpu.roll(k, 64, 1).astype(_MX)
        v = proj[:, 640:768]
        v_ref[...] = v.astype(_MX)
        vsw_ref[...] = pltpu.roll(v, 64, 1).astype(_MX)
        u0_ref[...] = proj[:, 768:1280] * _sig(proj[:, 1280:1792])

    row = lambda n: pl.BlockSpec((1, n), lambda i: (0, 0))
    tile = lambda n: pl.BlockSpec((TM, n), lambda i: (i, 0))
    res = _call_hosting_gather(
        body, gather, lambda: pl.program_id(0), s // TM, name="mix_in", grid=(s // TM,),
        in_specs=[tile(D), row(D), row(D), row(D), pl.BlockSpec((D, INW), lambda i: (0, 0)), row(128), row(128),
                  tile(128), tile(128), tile(128), pl.BlockSpec((128, 128), lambda i: (0, 0))],
        out_specs=[tile(INW), tile(D), tile(AW), tile(AW), tile(128), tile(128), tile(128), tile(128), tile(CW)],
        out_shape=[jax.ShapeDtypeStruct((s, INW), _MX), jax.ShapeDtypeStruct((s, D), _MX),
                   jax.ShapeDtypeStruct((s, AW), _MX), jax.ShapeDtypeStruct((s, AW), _MX),
                   jax.ShapeDtypeStruct((s, 128), _MX), jax.ShapeDtypeStruct((s, 128), _MX),
                   jax.ShapeDtypeStruct((s, 128), _MX), jax.ShapeDtypeStruct((s, 128), _MX),
                   jax.ShapeDtypeStruct((s, CW), F32)],
        compiler_params=_cp("arbitrary"),
    )(x, nw, sc, sh, win, qnw2, knw2, cos, sa, sb, gmat, *gather)
    return res[:9], res[9:]


def _ctx_kv(ctx, nw, scc, shc, winkv, knw2, gmat):
    def body(ctx_ref, nw_ref, sc_ref, sh_ref, w_ref, knw_ref, g_ref,
             kvc_ref, hc_ref, kc_ref, kcsw_ref, vc_ref, vcsw_ref):
        cv = ctx_ref[...]
        r = lax.rsqrt(jnp.mean(cv * cv, axis=-1, keepdims=True) + EPS)
        hc = (cv * r * (nw_ref[...] * (1.0 + sc_ref[...])) + sh_ref[...]).astype(_MX)
        hc_ref[...] = hc
        kvc = _dot(hc, w_ref[...])
        kvc_ref[...] = kvc
        kc = _headnorm(kvc[:, :128], knw_ref[...], g_ref[...])
        kc_ref[...] = kc.astype(_MX)
        kcsw_ref[...] = pltpu.roll(kc, 64, 1).astype(_MX)
        vc = kvc[:, 128:]
        vc_ref[...] = vc.astype(_MX)
        vcsw_ref[...] = pltpu.roll(vc, 64, 1).astype(_MX)

    return pl.pallas_call(
        body, name="ctx_kv",
        in_specs=[_vspec()] * 7, out_specs=[_vspec()] * 6,
        out_shape=[jax.ShapeDtypeStruct((LC, 256), F32), jax.ShapeDtypeStruct((LC, D), _MX)]
        + [jax.ShapeDtypeStruct((LC, 128), _MX)] * 4,
        compiler_params=pltpu.CompilerParams(vmem_limit_bytes=VMEM_LIMIT),
    )(ctx, nw, scc, shc, winkv, knw2, gmat)


def _attn_mask(i, s):
    r = lax.broadcasted_iota(jnp.int32, (2 * TQ, LC + 3 * TQ), 0) % TQ
    cidx = lax.broadcasted_iota(jnp.int32, (2 * TQ, LC + 3 * TQ), 1)
    qpos = i * TQ + r
    kpos = (i - 1) * TQ + (cidx - LC)
    near = (jnp.abs(qpos - kpos) <= 128) & (kpos >= 0) & (kpos < s)
    return (cidx < LC) | near


def _attn_bias(i, s):
    return jnp.where(_attn_mask(i, s), 0.0, NEG)


def _attn_exp(sc, bias, sinkv):
    sc = sc + bias
    m = jnp.maximum(jnp.max(sc, axis=-1, keepdims=True), sinkv)
    ex = jnp.exp(sc - m)
    es = jnp.exp(sinkv - m)
    return ex, es, 1.0 / (jnp.sum(ex, axis=-1, keepdims=True) + es)


def _sink_rows(sink_ref, g, e):
    return jnp.concatenate([jnp.full((TQ, 1), sink_ref[4 * g + e], F32),
                            jnp.full((TQ, 1), sink_ref[4 * g + 2 + e], F32)], axis=0)


def _attn_fwd(q0, q1, kp, kswp, vp, vswp, kc, kcsw, vc, vcsw, sink, gather=()):
    s = q0.shape[0]
    nb = s // TQ

    def body(q0_ref, q1_ref, kp_ref, kswp_ref, vp_ref, vswp_ref, kc_ref, kcsw_ref, vc_ref, vcsw_ref, sink_ref,
             o_ref, ex_ref, stat_ref):
        i = pl.program_id(0)
        st = pl.multiple_of(i * TQ, TQ)
        kall = (jnp.concatenate([kc_ref[...], kp_ref[pl.ds(st, 3 * TQ), :]], axis=0),
                jnp.concatenate([kcsw_ref[...], kswp_ref[pl.ds(st, 3 * TQ), :]], axis=0))
        vall = (jnp.concatenate([vc_ref[...], vp_ref[pl.ds(st, 3 * TQ), :]], axis=0),
                jnp.concatenate([vcsw_ref[...], vswp_ref[pl.ds(st, 3 * TQ), :]], axis=0))
        bias = _attn_bias(i, s)
        lo = lax.broadcasted_iota(jnp.int32, (1, 128), 1) < 64
        qrefs = (q0_ref, q1_ref)
        out = [jnp.zeros((TQ, 128), F32) for _ in range(4)]
        combos = [(g, e) for g in range(2) for e in range(2)]
        scores = [_dot_nt(jnp.concatenate([qrefs[e][:, 256 * g:256 * g + 128],
                                           qrefs[e][:, 256 * g + 128:256 * g + 256]], axis=0),
                          kall[0 if e == g else 1]) for g, e in combos]
        parts = []
        for n, ((g, e), sc) in enumerate(zip(combos, scores)):
            ex, es, inv = _attn_exp(sc, bias, _sink_rows(sink_ref, g, e))
            parts.append((ex.astype(_MX), inv))
            ex_ref[n] = parts[-1][0]
            stat_ref[n] = jnp.where(lo, inv, es * inv)
        for (g, e), (ex, inv) in zip(combos, parts):
            me = lo if e == 0 else jnp.logical_not(lo)
            o2 = _dot(ex, vall[0 if e == g else 1]) * inv
            out[2 * g] = out[2 * g] + jnp.where(me, o2[:TQ], 0.0)
            out[2 * g + 1] = out[2 * g + 1] + jnp.where(me, o2[TQ:], 0.0)
        for j in range(4):
            o_ref[:, 128 * j:128 * (j + 1)] = out[j].astype(_MX)

    full = lambda a: pl.BlockSpec(a.shape, lambda i: (0, 0))
    qs = pl.BlockSpec((TQ, AW), lambda i: (i, 0))
    res = _call_hosting_gather(
        body, gather, lambda: pl.program_id(0), nb, name="attn_fwd", grid=(nb,),
        in_specs=[qs, qs, full(kp), full(kp), full(kp), full(kp), full(kc), full(kc), full(kc), full(kc),
                  pl.BlockSpec(memory_space=pltpu.SMEM)],
        out_specs=[qs, pl.BlockSpec((None, 4, 2 * TQ, LC + 3 * TQ), lambda i: (i, 0, 0, 0)),
                   pl.BlockSpec((None, 4, 2 * TQ, 128), lambda i: (i, 0, 0, 0))],
        out_shape=[jax.ShapeDtypeStruct((s, AW), _MX), jax.ShapeDtypeStruct((nb, 4, 2 * TQ, LC + 3 * TQ), _MX),
                   jax.ShapeDtypeStruct((nb, 4, 2 * TQ, 128), F32)],
        compiler_params=_cp("arbitrary"),
    )(q0, q1, kp, kswp, vp, vswp, kc, kcsw, vc, vcsw, sink, *gather)
    return res[:3], res[3:]


def _conv31(u0, cw32, cb):
    s = u0.shape[0]
    rch = 256

    def body(u_ref, w_ref, b_ref, o_ref, pad_ref):
        pad_ref[0:16, :] = jnp.zeros((16, 128), F32)
        pad_ref[s + 16:s + 32, :] = jnp.zeros((16, 128), F32)
        pad_ref[16:s + 16, :] = u_ref[...]
        for cidx in range(s // rch):
            base = cidx * rch
            acc = jnp.zeros((rch, 128), F32) + b_ref[...]
            for j in range(CK):
                acc = acc + w_ref[j:j + 1, :] * pad_ref[base + j + 1:base + j + 1 + rch, :]
            o_ref[base:base + rch, :] = acc

    return pl.pallas_call(
        body, name="conv31", grid=(CW // 128,),
        in_specs=[pl.BlockSpec((s, 128), lambda c: (0, c)), pl.BlockSpec((32, 128), lambda c: (0, c)),
                  pl.BlockSpec((1, 128), lambda c: (0, c))],
        out_specs=pl.BlockSpec((s, 128), lambda c: (0, c)),
        out_shape=jax.ShapeDtypeStruct((s, CW), F32),
        scratch_shapes=[pltpu.VMEM((s + 32, 128), F32)],
        compiler_params=_cp("arbitrary"),
    )(u0, cw32, cb)


def _ln_stats(u1):
    mu = jnp.mean(u1, axis=-1, keepdims=True)
    xc = u1 - mu
    rstd = lax.rsqrt(jnp.mean(xc * xc, axis=-1, keepdims=True) + EPS)
    return xc * rstd, rstd


def _mix_out(o, u1, cnw, cnb, wout, x, g1):
    s = x.shape[0]

    def body(o_ref, u1_ref, cnw_ref, cnb_ref, w_ref, x_ref, g1_ref, x1_ref, mix_ref, cat_ref):
        u2n, _ = _ln_stats(u1_ref[...])
        u2 = u2n * cnw_ref[...] + cnb_ref[...]
        u3 = u2 * _sig(u2)
        cat = jnp.concatenate([o_ref[...], u3.astype(_MX)], axis=1)
        cat_ref[...] = cat
        mix = _dot(cat, w_ref[...])
        mix_ref[...] = mix.astype(_MX)
        x1_ref[...] = x_ref[...] + g1_ref[...] * mix

    row = lambda n: pl.BlockSpec((1, n), lambda i: (0, 0))
    tile = lambda n: pl.BlockSpec((TM, n), lambda i: (i, 0))
    return pl.pallas_call(
        body, name="mix_out", grid=(s // TM,),
        in_specs=[tile(AW), tile(CW), row(CW), row(CW), pl.BlockSpec((D, D), lambda i: (0, 0)), tile(D), row(D)],
        out_specs=[tile(D), tile(D), tile(D)],
        out_shape=[jax.ShapeDtypeStruct((s, D), F32), jax.ShapeDtypeStruct((s, D), _MX),
                   jax.ShapeDtypeStruct((s, D), _MX)],
        compiler_params=_cp("arbitrary"),
    )(o, u1, cnw, cnb, wout, x, g1)


def _ffn_up(x1, nw, sc, sh, wup4, gather=()):
    s = x1.shape[0]
    tm = min(1024, s)

    def body(x_ref, nw_ref, sc_ref, sh_ref, w_ref, h2_ref, up_ref, h2s):
        @pl.when(pl.program_id(1) == 0)
        def _():
            xv = x_ref[...]
            r = lax.rsqrt(jnp.mean(xv * xv, axis=-1, keepdims=True) + EPS)
            h2 = (xv * r * (nw_ref[...] * (1.0 + sc_ref[...])) + sh_ref[...]).astype(_MX)
            h2s[...] = h2
            h2_ref[...] = h2
        up_ref[...] = _dot(h2s[...], w_ref[...]).astype(_MX)

    row = pl.BlockSpec((1, D), lambda i, j: (0, 0))
    res = _call_hosting_gather(
        body, gather, lambda: pl.program_id(0) * 4 + pl.program_id(1), 4 * (s // tm), name="ffn_up",
        grid=(s // tm, 4),
        in_specs=[pl.BlockSpec((tm, D), lambda i, j: (i, 0)), row, row, row,
                  pl.BlockSpec((None, D, FQ), lambda i, j: (j, 0, 0))],
        out_specs=[pl.BlockSpec((tm, D), lambda i, j: (i, 0)),
                   pl.BlockSpec((None, tm, FQ), lambda i, j: (j // 2, i, j % 2))],
        out_shape=[jax.ShapeDtypeStruct((s, D), _MX), jax.ShapeDtypeStruct((2, s, FH), _MX)],
        scratch_shapes=[pltpu.VMEM((tm, D), _MX)],
        compiler_params=_cp("arbitrary", "arbitrary"),
    )(x1, nw, sc, sh, wup4, *gather)
    return res[:2], res[2:]


def _fill_pad8(pad_ref, val, s):
    pad_ref[0:8, :] = jnp.zeros((8, 128), F32)
    pad_ref[s + 8:s + 16, :] = jnp.zeros((8, 128), F32)
    pad_ref[8:s + 8, :] = val


def _conv3_at(pad_ref, w_ref, half, base, rch):
    return (w_ref[half, 0:1, :] * pad_ref[base + 7:base + 7 + rch, :]
            + w_ref[half, 1:2, :] * pad_ref[base + 8:base + 8 + rch, :]
            + w_ref[half, 2:3, :] * pad_ref[base + 9:base + 9 + rch, :])


def _ffn_conv_act(up0, fcw, fcb):
    s = up0.shape[1]
    rch = 256

    def body(up_ref, w_ref, b_ref, act_ref, gv_ref, padg, padv):
        _fill_pad8(padg, up_ref[0].astype(F32), s)
        _fill_pad8(padv, up_ref[1].astype(F32), s)
        for cidx in range(s // rch):
            base = cidx * rch
            gate = _conv3_at(padg, w_ref, 0, base, rch) + b_ref[0]
            val = _conv3_at(padv, w_ref, 1, base, rch) + b_ref[1]
            act_ref[base:base + rch, :] = (gate * _sig(gate) * val).astype(_MX)
            gv_ref[0, base:base + rch, :] = gate.astype(_MX)
            gv_ref[1, base:base + rch, :] = val.astype(_MX)

    return pl.pallas_call(
        body, name="ffn_conv_act", grid=(NCH,),
        in_specs=[pl.BlockSpec((2, s, 128), lambda c: (0, 0, c)), pl.BlockSpec((2, 8, 128), lambda c: (0, 0, c)),
                  pl.BlockSpec((2, 1, 128), lambda c: (0, 0, c))],
        out_specs=[pl.BlockSpec((s, 128), lambda c: (0, c)), pl.BlockSpec((2, s, 128), lambda c: (0, 0, c))],
        out_shape=[jax.ShapeDtypeStruct((s, FH), _MX), jax.ShapeDtypeStruct((2, s, FH), _MX)],
        scratch_shapes=[pltpu.VMEM((s + 16, 128), F32)] * 2,
        compiler_params=_cp("arbitrary"),
    )(up0, fcw, fcb)


def _ffn_down(act, wdown, x1, g2, tgt):
    s = x1.shape[0]

    def body(act_ref, w_ref, x1_ref, g2_ref, tgt_ref, dy_ref, dact_ref, ddn_ref, vec_ref):
        @pl.when(pl.program_id(0) == 0)
        def _():
            vec_ref[...] = jnp.zeros((8, D), F32)
        dn = _dot(act_ref[...], w_ref[...])
        diff = x1_ref[...] + g2_ref[...] * dn - tgt_ref[...]
        dy = diff * (1.0 / D)
        dy_ref[...] = dy
        ddn = (dy * g2_ref[...]).astype(_MX)
        ddn_ref[...] = ddn
        dact_ref[...] = _dot_nt(ddn, w_ref[...]).astype(_MX)
        vec_ref[0:1, :] += _rowsum(dy * dn)
        vec_ref[1:2, :] += _rowsum(diff * diff)

    tile = lambda n: pl.BlockSpec((TM, n), lambda i: (i, 0))
    return pl.pallas_call(
        body, name="ffn_down", grid=(s // TM,),
        in_specs=[tile(FH), pl.BlockSpec((FH, D), lambda i: (0, 0)), tile(D), pl.BlockSpec((1, D), lambda i: (0, 0)),
                  tile(D)],
        out_specs=[tile(D), tile(FH), tile(D), pl.BlockSpec((8, D), lambda i: (0, 0))],
        out_shape=[jax.ShapeDtypeStruct((s, D), F32), jax.ShapeDtypeStruct((s, FH), _MX),
                   jax.ShapeDtypeStruct((s, D), _MX), jax.ShapeDtypeStruct((8, D), F32)],
        compiler_params=_cp("arbitrary"),
    )(act, wdown, x1, g2, tgt)


def _ffn_conv_bwd(up0, gv, dact, fcw):
    s = up0.shape[1]
    rch = 256

    def body(up_ref, gv_ref, da_ref, w_ref, dup_ref, gw_ref, dpg, dpv):
        for p in (dpg, dpv):
            p[0:8, :] = jnp.zeros((8, 128), F32)
            p[s + 8:s + 16, :] = jnp.zeros((8, 128), F32)
        for cidx in range(s // rch):
            rows = slice(cidx * rch, (cidx + 1) * rch)
            gate = gv_ref[0, rows, :].astype(F32)
            da = da_ref[rows, :].astype(F32)
            sg = _sig(gate)
            silu = gate * sg
            dpg[8 + cidx * rch:8 + (cidx + 1) * rch, :] = da * gv_ref[1, rows, :].astype(F32) * (sg + silu * (1.0 - sg))
            dpv[8 + cidx * rch:8 + (cidx + 1) * rch, :] = da * silu
        acc = [[jnp.zeros((1, 128), F32) for _ in range(4)] for _ in range(2)]
        for cidx in range(s // rch):
            base = cidx * rch
            for half, dp in enumerate((dpg, dpv)):
                shifted = [dp[base + 9 - j:base + 9 - j + rch, :] for j in range(3)]
                dup_ref[half, base:base + rch, :] = (w_ref[half, 0:1, :] * shifted[0] + w_ref[half, 1:2, :] * shifted[1]
                                                     + w_ref[half, 2:3, :] * shifted[2]).astype(_MX)
                xv = up_ref[half, base:base + rch, :].astype(F32)
                for j in range(3):
                    acc[half][j] = acc[half][j] + _rowsum(xv * shifted[j])
                acc[half][3] = acc[half][3] + _rowsum(shifted[1])
        for half in range(2):
            gw_ref[half] = jnp.zeros((8, 128), F32)
            for j in range(4):
                gw_ref[half, j:j + 1, :] = acc[half][j]

    both = pl.BlockSpec((2, s, 128), lambda c: (0, 0, c))
    return pl.pallas_call(
        body, name="ffn_conv_bwd", grid=(NCH,),
        in_specs=[both, both, pl.BlockSpec((s, 128), lambda c: (0, c)), pl.BlockSpec((2, 8, 128), lambda c: (0, 0, c))],
        out_specs=[both, pl.BlockSpec((2, 8, 128), lambda c: (0, 0, c))],
        out_shape=[jax.ShapeDtypeStruct((2, s, FH), _MX), jax.ShapeDtypeStruct((2, 8, FH), F32)],
        scratch_shapes=[pltpu.VMEM((s + 16, 128), F32)] * 2,
        compiler_params=_cp("arbitrary"),
    )(up0, gv, dact, fcw)


def _tn_matmul(a, b, tm, tn, name, b_split=False, by_chip=False, init=None):
    s, m = a.shape
    n = 2 * b.shape[2] if b_split else b.shape[1]
    ts = min(1024, s)
    nsteps = s // ts
    npb = (n // 2) // tn if b_split else None

    def body(*refs):
        if init is None:
            a_ref, b_ref, o_ref, acc = refs
        else:
            a_ref, b_ref, i_ref, o_ref, acc = refs
        k = pl.program_id(2)

        @pl.when(k == 0)
        def _():
            acc[...] = jnp.zeros((tm, tn), F32)
            if init is not None:
                acc[:, 512:768] = i_ref[...]
        acc[...] += _dot_tn(a_ref[...], b_ref[...])

        @pl.when(k == nsteps - 1)
        def _():
            o_ref[...] = acc[...].astype(_MX)

    if b_split:
        bspec = pl.BlockSpec((None, ts, tn), lambda i, j, k: (j // npb, k, j % npb))
    else:
        bspec = pl.BlockSpec((ts, tn), lambda i, j, k: (k, j))
    in_specs = [pl.BlockSpec((ts, tm), lambda i, j, k: (k, i)), bspec]
    args = [a, b]
    if init is not None:
        in_specs.append(pl.BlockSpec((tm, 256), lambda i, j, k: (i, 0)))
        args.append(init)
    if by_chip:
        out_spec = pl.BlockSpec((None, tm, tn), lambda i, j, k: (j, i, 0))
        out_shape = jax.ShapeDtypeStruct((n // tn, m, tn), _MX)
    else:
        out_spec = pl.BlockSpec((tm, tn), lambda i, j, k: (i, j))
        out_shape = jax.ShapeDtypeStruct((m, n), _MX)
    return pl.pallas_call(
        body, name=name, grid=(m // tm, n // tn, nsteps),
        in_specs=in_specs, out_specs=out_spec, out_shape=out_shape,
        scratch_shapes=[pltpu.VMEM((tm, tn), F32)],
        compiler_params=_cp("arbitrary", "arbitrary", "arbitrary"),
    )(*args)


def _ffn_up_bwd(dup, wup4, x1, dy, mix, nw, sc, g1):
    s = x1.shape[0]
    nk = 2

    def body(dup_ref, w_ref, x1_ref, dy_ref, mix_ref, nw_ref, sc_ref, g1_ref, dx1_ref, dmix_ref, vec_ref, acc):
        i, k = pl.program_id(0), pl.program_id(1)

        @pl.when(k == 0)
        def _():
            acc[...] = jnp.zeros((TM, D), F32)

        @pl.when((k == 0) & (i == 0))
        def _():
            vec_ref[...] = jnp.zeros((8, D), F32)
        acc[...] += _dot_nt(dup_ref[:, :FQ], w_ref[0]) + _dot_nt(dup_ref[:, FQ:], w_ref[1])

        @pl.when(k == nk - 1)
        def _():
            dh = acc[...]
            xv = x1_ref[...]
            r = lax.rsqrt(jnp.mean(xv * xv, axis=-1, keepdims=True) + EPS)
            xn = xv * r
            nwv, scv = nw_ref[...], sc_ref[...]
            vec_ref[0:1, :] += _rowsum(dh)
            vec_ref[1:2, :] += _rowsum(dh * xn) * nwv
            vec_ref[2:3, :] += _rowsum(dh * xn) * (1.0 + scv)
            dxn = dh * (nwv * (1.0 + scv))
            dx1 = dy_ref[...] + r * (dxn - xn * jnp.mean(dxn * xn, axis=-1, keepdims=True))
            dx1_ref[...] = dx1
            vec_ref[3:4, :] += _rowsum(dx1 * mix_ref[...].astype(F32))
            dmix_ref[...] = (dx1 * g1_ref[...]).astype(_MX)

    tile = pl.BlockSpec((TM, D), lambda i, k: (i, 0))
    row = pl.BlockSpec((1, D), lambda i, k: (0, 0))
    return pl.pallas_call(
        body, name="ffn_up_bwd", grid=(s // TM, nk),
        in_specs=[pl.BlockSpec((None, TM, FH), lambda i, k: (k, i, 0)),
                  pl.BlockSpec((2, D, FQ), lambda i, k: (k, 0, 0)), tile, tile, tile, row, row, row],
        out_specs=[tile, tile, pl.BlockSpec((8, D), lambda i, k: (0, 0))],
        out_shape=[jax.ShapeDtypeStruct((s, D), F32), jax.ShapeDtypeStruct((s, D), _MX),
                   jax.ShapeDtypeStruct((8, D), F32)],
        scratch_shapes=[pltpu.VMEM((TM, D), F32)],
        compiler_params=_cp("arbitrary", "arbitrary"),
    )(dup, wup4, x1, dy, mix, nw, sc, g1)


def _mix_out_bwd(dmix, wout, u1, cnw, cnb, swap=()):
    s = u1.shape[0]

    def body(dm_ref, w_ref, u1_ref, cnw_ref, cnb_ref, do_ref, du1_ref, vec_ref):
        @pl.when(pl.program_id(0) == 0)
        def _():
            vec_ref[...] = jnp.zeros((8, CW), F32)
        dcat = _dot_nt(dm_ref[...], w_ref[...])
        do_ref[...] = dcat[:, :AW]
        du3 = dcat[:, AW:]
        u2n, rstd = _ln_stats(u1_ref[...])
        u2 = u2n * cnw_ref[...] + cnb_ref[...]
        sg = _sig(u2)
        du2 = du3 * sg * (1.0 + u2 * (1.0 - sg))
        vec_ref[0:1, :] += _rowsum(du2)
        vec_ref[1:2, :] += _rowsum(du2 * u2n)
        d2n = du2 * cnw_ref[...]
        du1_ref[...] = rstd * (d2n - jnp.mean(d2n, axis=-1, keepdims=True)
                               - u2n * jnp.mean(d2n * u2n, axis=-1, keepdims=True))

    row = lambda n: pl.BlockSpec((1, n), lambda i: (0, 0))
    tile = lambda n: pl.BlockSpec((TM, n), lambda i: (i, 0))
    res = _call_hosting_gather(
        body, swap, lambda: pl.program_id(0), s // TM, comm=_Swap, name="mix_out_bwd", grid=(s // TM,),
        in_specs=[tile(D), pl.BlockSpec((D, D), lambda i: (0, 0)), tile(CW), row(CW), row(CW)],
        out_specs=[tile(AW), tile(CW), pl.BlockSpec((8, CW), lambda i: (0, 0))],
        out_shape=[jax.ShapeDtypeStruct((s, AW), F32), jax.ShapeDtypeStruct((s, CW), F32),
                   jax.ShapeDtypeStruct((8, CW), F32)],
        compiler_params=_cp("arbitrary"),
    )(dmix, wout, u1, cnw, cnb, *swap)
    return res[:3], res[3:]


def _conv31_bwd(du1, u0, cw32, proj, exchange=()):
    s = u0.shape[0]
    rch = 128

    def body(d_ref, u_ref, w_ref, ga_ref, gb_ref, dga_ref, dgb_ref, gw_ref, padd):
        padd[0:16, :] = jnp.zeros((16, 128), F32)
        padd[s + 16:s + 32, :] = jnp.zeros((16, 128), F32)
        padd[16:s + 16, :] = d_ref[...]
        gw_ref[...] = jnp.zeros((32, 128), F32)
        gw_ref[CK:CK + 1, :] = _rowsum(d_ref[...])
        for cidx in range(s // rch):
            base = cidx * rch
            uv = u_ref[base:base + rch, :]
            du0 = jnp.zeros((rch, 128), F32)
            for j in range(CK):
                shifted = padd[base + 31 - j:base + 31 - j + rch, :]
                du0 = du0 + w_ref[j:j + 1, :] * shifted
                gw_ref[j:j + 1, :] += _rowsum(uv * shifted)
            sg = _sig(gb_ref[base:base + rch, :].astype(F32))
            ga = ga_ref[base:base + rch, :].astype(F32)
            dga_ref[base:base + rch, :] = (du0 * sg).astype(_MX)
            dgb_ref[base:base + rch, :] = (du0 * ga * sg * (1.0 - sg)).astype(_MX)

    blk = lambda off: pl.BlockSpec((s, 128), lambda c: (0, c + off))
    res = _call_hosting_gather(
        body, exchange, lambda: pl.program_id(0), CW // 128, comm=_Exchange, name="conv31_bwd", grid=(CW // 128,),
        in_specs=[blk(0), blk(0), pl.BlockSpec((32, 128), lambda c: (0, c)), blk(6), blk(10)],
        out_specs=[blk(0), blk(0), pl.BlockSpec((32, 128), lambda c: (0, c))],
        out_shape=[jax.ShapeDtypeStruct((s, CW), _MX), jax.ShapeDtypeStruct((s, CW), _MX),
                   jax.ShapeDtypeStruct((32, CW), F32)],
        scratch_shapes=[pltpu.VMEM((s + 32, 128), F32)],
        compiler_params=_cp("arbitrary"),
    )(du1, u0, cw32, proj, proj, *exchange)
    return res[:3], res[3:]


def _attn_bwd(q0, q1, kp, kswp, vp, vswp, kc, kcsw, vc, vcsw, ex, stat, o, do, exchange=()):
    s = q0.shape[0]
    nb = s // TQ
    ne = len(exchange)

    def body(*refs):
        (q0_ref, q1_ref, kp_ref, kswp_ref, vp_ref, vswp_ref, kc_ref, kcsw_ref, vc_ref, vcsw_ref, ex_ref, stat_ref,
         o_ref, do_ref) = refs[:14]
        dq_ref, dk_ref, dv_ref, dkc_ref, dvc_ref, dsink_ref = refs[14 + ne:20 + ne]
        i = pl.program_id(0)
        if ne:
            exch = _Exchange(refs[14:14 + ne], refs[20 + ne:20 + 2 * ne], refs[20 + 2 * ne], refs[21 + 2 * ne])
            pl.when(i == 0)(exch.start)

        @pl.when(i == 0)
        def _():
            dk_ref[...] = jnp.zeros((s + 2 * TQ, 128), F32)
            dv_ref[...] = jnp.zeros((s + 2 * TQ, 128), F32)
            dkc_ref[...] = jnp.zeros((LC, 128), F32)
            dvc_ref[...] = jnp.zeros((LC, 128), F32)
            dsink_ref[...] = jnp.zeros((8, 128), F32)
        st = pl.multiple_of(i * TQ, TQ)
        kall = (jnp.concatenate([kc_ref[...], kp_ref[pl.ds(st, 3 * TQ), :]], axis=0),
                jnp.concatenate([kcsw_ref[...], kswp_ref[pl.ds(st, 3 * TQ), :]], axis=0))
        vall = (jnp.concatenate([vc_ref[...], vp_ref[pl.ds(st, 3 * TQ), :]], axis=0),
                jnp.concatenate([vcsw_ref[...], vswp_ref[pl.ds(st, 3 * TQ), :]], axis=0))
        lo = lax.broadcasted_iota(jnp.int32, (1, 128), 1) < 64
        qrefs = (q0_ref, q1_ref)
        dq = [jnp.zeros((TQ, 128), F32) for _ in range(4)]
        dkt = jnp.zeros((128, LC + 3 * TQ), F32)
        dvt = jnp.zeros((128, LC + 3 * TQ), F32)
        combos = [(g, e) for g in range(2) for e in range(2)]

        def stacked(ref, g):
            return jnp.concatenate([ref[:, 256 * g:256 * g + 128], ref[:, 256 * g + 128:256 * g + 256]], axis=0)

        for n, (g, e) in enumerate(combos):
            me = lo if e == 0 else jnp.logical_not(lo)
            sw = 0 if e == g else 1
            qm = stacked(qrefs[e], g)
            p = ex_ref[n].astype(F32) * stat_ref[n, :, 0:1]
            dom = jnp.where(me, stacked(do_ref, g), 0.0)
            dd = jnp.sum(dom * stacked(o_ref, g).astype(F32), axis=-1, keepdims=True)
            domx = dom.astype(_MX)
            sd = stat_ref[n, :, 64:65] * dd
            dsink_ref[4 * g + e:4 * g + e + 1, :] -= jnp.sum(sd[:TQ], axis=0, keepdims=True)
            dsink_ref[4 * g + 2 + e:4 * g + 3 + e, :] -= jnp.sum(sd[TQ:], axis=0, keepdims=True)
            ds = (p * (_dot_nt(domx, vall[sw]) - dd)).astype(_MX)
            dq2 = _dot(ds, kall[sw])
            dq[2 * g] = dq[2 * g] + jnp.where(me, dq2[:TQ], 0.0)
            dq[2 * g + 1] = dq[2 * g + 1] + jnp.where(me, dq2[TQ:], 0.0)
            dk2 = _dot_tn(qm, ds)
            dv2 = _dot_tn(domx, p.astype(_MX))
            if sw:
                dk2 = pltpu.roll(dk2, 64, 0)
                dv2 = pltpu.roll(dv2, 64, 0)
            dkt = dkt + dk2
            dvt = dvt + dv2
        dkt = dkt.T
        dvt = dvt.T
        for j in range(4):
            dq_ref[:, 128 * j:128 * (j + 1)] = dq[j] * 0.125
        dkc_ref[...] += dkt[:LC]
        dvc_ref[...] += dvt[:LC]
        dk_ref[pl.ds(st, 3 * TQ), :] += dkt[LC:]
        dv_ref[pl.ds(st, 3 * TQ), :] += dvt[LC:]
        if ne:
            pl.when(i == nb - 1)(exch.finish)

    full = lambda a: pl.BlockSpec(a.shape, lambda i: (0, 0))
    fs = lambda r: pl.BlockSpec((r, 128), lambda i: (0, 0))
    qs = pl.BlockSpec((TQ, AW), lambda i: (i, 0))
    res = pl.pallas_call(
        body, name="attn_bwd", grid=(nb,),
        in_specs=[qs, qs, full(kp), full(kp), full(kp), full(kp), full(kc), full(kc), full(kc), full(kc),
                  pl.BlockSpec((None, 4, 2 * TQ, LC + 3 * TQ), lambda i: (i, 0, 0, 0)),
                  pl.BlockSpec((None, 4, 2 * TQ, 128), lambda i: (i, 0, 0, 0)), qs, qs] + _any_specs(ne),
        out_specs=[qs, fs(s + 2 * TQ), fs(s + 2 * TQ), fs(LC), fs(LC), fs(8)] + _any_specs(ne),
        out_shape=[jax.ShapeDtypeStruct((s, AW), F32), jax.ShapeDtypeStruct((s + 2 * TQ, 128), F32),
                   jax.ShapeDtypeStruct((s + 2 * TQ, 128), F32), jax.ShapeDtypeStruct((LC, 128), F32),
                   jax.ShapeDtypeStruct((LC, 128), F32), jax.ShapeDtypeStruct((8, 128), F32)]
        + _Exchange.out_shapes(exchange),
        scratch_shapes=_Exchange.semaphores(ne) if ne else [],
        compiler_params=_cp("arbitrary"),
    )(q0, q1, kp, kswp, vp, vswp, kc, kcsw, vc, vcsw, ex, stat, o, do, *exchange)
    return res[:6], res[6:]


def _mix_in_bwd(dq, dk, dv, dga, dgb, proj, x, dx1, win, nw, sc, qnw2, knw2, cos, sa, sb, gmat):
    s = x.shape[0]

    def body(dq_ref, dk_ref, dv_ref, dga_ref, dgb_ref, proj_ref, x_ref, dx1_ref, win_ref, nw_ref, sc_ref,
             qnw_ref, knw_ref, cos_ref, sa_ref, sb_ref, g_ref, gx_ref, dproj_ref, vec_ref):
        @pl.when(pl.program_id(0) == 0)
        def _():
            vec_ref[...] = jnp.zeros((8, D), F32)
        cs, sav, sbv, g = cos_ref[...], sa_ref[...], sb_ref[...], g_ref[...]
        gq = jnp.zeros((1, 128), F32)
        for j in range(4):
            dqn = _rope_bwd(dq_ref[:, 128 * j:128 * (j + 1)], cs, sav, sbv)
            dch, gw = _headnorm_bwd(proj_ref[:, 128 * j:128 * (j + 1)].astype(F32), dqn, qnw_ref[...], g)
            dproj_ref[:, 128 * j:128 * (j + 1)] = dch.astype(_MX)
            gq = gq + _rowsum(gw)
        dkn = _rope_bwd(dk_ref[...], cs, sav, sbv)
        dch, gw = _headnorm_bwd(proj_ref[:, 512:640].astype(F32), dkn, knw_ref[...], g)
        dproj_ref[:, 512:640] = dch.astype(_MX)
        dproj_ref[:, 640:768] = dv_ref[...].astype(_MX)
        dproj_ref[:, 768:1280] = dga_ref[...]
        dproj_ref[:, 1280:1792] = dgb_ref[...]
        vec_ref[3:4, 0:128] += gq
        vec_ref[3:4, 128:256] += _rowsum(gw)
        dh = _dot_nt(dproj_ref[...], win_ref[...])
        xv = x_ref[...]
        r = lax.rsqrt(jnp.mean(xv * xv, axis=-1, keepdims=True) + EPS)
        xn = xv * r
        nwv, scv = nw_ref[...], sc_ref[...]
        vec_ref[0:1, :] += _rowsum(dh)
        vec_ref[1:2, :] += _rowsum(dh * xn) * nwv
        vec_ref[2:3, :] += _rowsum(dh * xn) * (1.0 + scv)
        dxn = dh * (nwv * (1.0 + scv))
        gx_ref[...] = dx1_ref[...] + r * (dxn - xn * jnp.mean(dxn * xn, axis=-1, keepdims=True))

    row = lambda n: pl.BlockSpec((1, n), lambda i: (0, 0))
    tile = lambda n: pl.BlockSpec((TM, n), lambda i: (i, 0))
    return pl.pallas_call(
        body, name="mix_in_bwd", grid=(s // TM,),
        in_specs=[tile(AW), tile(128), tile(128), tile(CW), tile(CW), tile(INW), tile(D), tile(D),
                  pl.BlockSpec((D, INW), lambda i: (0, 0)), row(D), row(D), row(128), row(128),
                  tile(128), tile(128), tile(128), pl.BlockSpec((128, 128), lambda i: (0, 0))],
        out_specs=[tile(D), tile(INW), pl.BlockSpec((8, D), lambda i: (0, 0))],
        out_shape=[jax.ShapeDtypeStruct((s, D), F32), jax.ShapeDtypeStruct((s, INW), _MX),
                   jax.ShapeDtypeStruct((8, D), F32)],
        compiler_params=_cp("arbitrary"),
    )(dq, dk, dv, dga, dgb, proj, x, dx1, win, nw, sc, qnw2, knw2, cos, sa, sb, gmat)


def _ctx_bwd(ctx, nw, scc, winkv, kvc, hc, dkc, dvc, knw2, gmat):
    def body(ctx_ref, nw_ref, sc_ref, w_ref, kvc_ref, hc_ref, dkc_ref, dvc_ref, knw_ref, g_ref, gw_ref, vec_ref):
        dkr, gk = _headnorm_bwd(kvc_ref[:, 0:128], dkc_ref[...], knw_ref[...], g_ref[...])
        dkv = jnp.concatenate([dkr, dvc_ref[...]], axis=1).astype(_MX)
        gw_ref[...] = _dot_tn(hc_ref[...], dkv)
        dh = _dot_nt(dkv, w_ref[...])
        cv = ctx_ref[...]
        r = lax.rsqrt(jnp.mean(cv * cv, axis=-1, keepdims=True) + EPS)
        cn = cv * r
        vec_ref[...] = jnp.zeros((8, D), F32)
        vec_ref[0:1, :] = _rowsum(dh)
        vec_ref[1:2, :] = _rowsum(dh * cn) * nw_ref[...]
        vec_ref[2:3, :] = _rowsum(dh * cn) * (1.0 + sc_ref[...])
        vec_ref[3:4, 0:128] = _rowsum(gk)

    return pl.pallas_call(
        body, name="ctx_bwd", in_specs=[_vspec()] * 10, out_specs=[_vspec()] * 2,
        out_shape=[jax.ShapeDtypeStruct((D, 256), F32), jax.ShapeDtypeStruct((8, D), F32)],
        compiler_params=pltpu.CompilerParams(vmem_limit_bytes=VMEM_LIMIT),
    )(ctx, nw, scc, winkv, kvc, hc, dkc, dvc, knw2, gmat)


def _mod_fwd(cs, wmod, bloc):
    def body(c_ref, w_ref, b_ref, o_ref):
        cv = c_ref[...]
        o_ref[...] = _dot((cv * _sig(cv)).astype(_MX), w_ref[...].astype(_MX)) + b_ref[...]

    return pl.pallas_call(
        body, name="mod_fwd", in_specs=[_vspec()] * 3, out_specs=_vspec(),
        out_shape=jax.ShapeDtypeStruct((16, wmod.shape[1]), F32),
        compiler_params=pltpu.CompilerParams(vmem_limit_bytes=VMEM_LIMIT),
    )(cs, wmod, bloc)


def _mod_bwd(cs, dmod, wmod):
    def body(c_ref, d_ref, w_ref, gw_ref, part_ref):
        cv = c_ref[...]
        sl = (cv * _sig(cv)).astype(_MX)
        dm = d_ref[...].astype(_MX)
        gw_ref[...] = _dot_tn(sl, dm)
        part_ref[...] = _dot_nt(dm, w_ref[...].astype(_MX))

    return pl.pallas_call(
        body, name="mod_bwd", in_specs=[_vspec()] * 3, out_specs=[_vspec()] * 2,
        out_shape=[jax.ShapeDtypeStruct(wmod.shape, F32), jax.ShapeDtypeStruct((16, D), F32)],
        compiler_params=pltpu.CompilerParams(vmem_limit_bytes=VMEM_LIMIT),
    )(cs, dmod, wmod)


def _sum_leading(a, name):
    n = a.shape[0]

    def body(a_ref, o_ref):
        acc = a_ref[0]
        for k in range(1, n):
            acc = acc + a_ref[k]
        o_ref[...] = acc

    return pl.pallas_call(
        body, name=name, in_specs=[_vspec()], out_specs=_vspec(),
        out_shape=jax.ShapeDtypeStruct(a.shape[1:], F32),
        compiler_params=pltpu.CompilerParams(vmem_limit_bytes=VMEM_LIMIT),
    )(a)


def _cctx_grad(parts, cc):
    def body(p_ref, c_ref, o_ref):
        acc = p_ref[0, 8:9, :]
        for k in range(1, 4):
            acc = acc + p_ref[k, 8:9, :]
        cv = c_ref[...]
        sg = _sig(cv)
        o_ref[...] = acc * (sg * (1.0 + cv * (1.0 - sg)))

    return pl.pallas_call(
        body, name="cctx_grad", in_specs=[_vspec()] * 2, out_specs=_vspec(),
        out_shape=jax.ShapeDtypeStruct((1, D), F32),
    )(parts, cc)


def _adam_math(w, g, m, v):
    mn = ADAM_B1 * m + (1.0 - ADAM_B1) * g
    vn = ADAM_B2 * v + (1.0 - ADAM_B2) * (g * g)
    mh = mn / (1.0 - ADAM_B1 ** ADAM_STEP)
    vh = vn / (1.0 - ADAM_B2 ** ADAM_STEP)
    delta = -ADAM_LR * (mh / (jnp.sqrt(vh) + ADAM_EPS) + ADAM_WD * w)
    return delta, mn, vn


def _adam_big(w, g, m, v, name):
    r, n = w.shape
    tr = 256 if r % 256 == 0 else 64

    def body(w_ref, g_ref, m_ref, v_ref, d_ref, mo_ref, vo_ref):
        d, mn, vn = _adam_math(w_ref[...], g_ref[...], m_ref[...], v_ref[...])
        d_ref[...] = d
        mo_ref[...] = mn
        vo_ref[...] = vn

    spec = pl.BlockSpec((tr, n), lambda i: (i, 0))
    return pl.pallas_call(
        body, name=name, grid=(r // tr,), in_specs=[spec] * 4, out_specs=[spec] * 3,
        out_shape=[jax.ShapeDtypeStruct((r, n), F32)] * 3,
        compiler_params=_cp("arbitrary"),
    )(w, g, m, v)


def _adam_small(ws, gs, ms, vs):
    n = len(ws)

    def body(*refs):
        ins, outs = refs[:4 * n], refs[4 * n:]
        for k in range(n):
            d, mn, vn = _adam_math(ins[k][...], ins[n + k][...], ins[2 * n + k][...], ins[3 * n + k][...])
            outs[k][...] = d
            outs[n + k][...] = mn
            outs[2 * n + k][...] = vn

    shapes = [jax.ShapeDtypeStruct(w.shape, F32) for w in ws]
    res = pl.pallas_call(
        body, name="adam_small", in_specs=[_vspec()] * (4 * n), out_specs=[_vspec()] * (3 * n),
        out_shape=shapes * 3,
    )(*ws, *gs, *ms, *vs)
    return res[:n], res[n:2 * n], res[2 * n:]


def _pair_sum(grads, from_sib, core, name):
    n = len(grads)

    def body(c_ref, *refs):
        for w in range(n):
            a_ref, b_ref, o_ref = refs[w], refs[n + w], refs[2 * n + w]
            o_ref[...] = (a_ref[...].astype(F32) + b_ref[...].astype(F32)).astype(_MX)

    halves = [(None, g.shape[1] // 2, g.shape[2]) for g in grads]
    return pl.pallas_call(
        body, name=name,
        grid_spec=pltpu.PrefetchScalarGridSpec(
            num_scalar_prefetch=1, grid=(4,),
            in_specs=[pl.BlockSpec(h, lambda j, c: (j, c[0], 0)) for h in halves]
            + [pl.BlockSpec(h, lambda j, c: (j, 0, 0)) for h in halves],
            out_specs=[pl.BlockSpec(h, lambda j, c: (j, 0, 0)) for h in halves]),
        out_shape=[jax.ShapeDtypeStruct(p.shape, _MX) for p in from_sib],
        compiler_params=_cp("arbitrary"),
    )(core.reshape(1), *grads, *from_sib)


def _chip_sum(parts, arrived, chip, core):
    n = len(parts)

    def body(s_ref, *refs):
        for w in range(n):
            own_ref, p_ref, o_ref = refs[w], refs[n + w], refs[2 * n + w]
            acc = own_ref[...].astype(F32)
            for k in range(3):
                acc = acc + p_ref[k].astype(F32)
            o_ref[...] = acc

    blk = [(p.shape[1] // 2, p.shape[2]) for p in parts]
    return pl.pallas_call(
        body, name="grad_chip_sum",
        grid_spec=pltpu.PrefetchScalarGridSpec(
            num_scalar_prefetch=1, grid=(2,),
            in_specs=[pl.BlockSpec((None,) + b, lambda i, s: (s[0], i, 0)) for b in blk]
            + [pl.BlockSpec((3,) + b, lambda i, s: (0, i, 0)) for b in blk],
            out_specs=[pl.BlockSpec(b, lambda i, s: (2 * s[1] + i, 0)) for b in blk]),
        out_shape=[jax.ShapeDtypeStruct((2 * p.shape[1], p.shape[2]), F32) for p in parts],
        compiler_params=_cp("arbitrary"),
    )(jnp.stack([chip, core]), *parts, *arrived)


def _cast_shards(ws, chip):
    n = len(ws)

    def body(s_ref, *refs):
        for w in range(n):
            refs[n + w][...] = refs[w][...].astype(_MX)

    blk = [(a.shape[0] // 2, a.shape[1]) for a in ws]
    return pl.pallas_call(
        body, name="cast_shards",
        grid_spec=pltpu.PrefetchScalarGridSpec(
            num_scalar_prefetch=1, grid=(2,),
            in_specs=[pl.BlockSpec(b, lambda i, s: (i, 0)) for b in blk],
            out_specs=[pl.BlockSpec((None,) + b, lambda i, s: (s[0], i, 0)) for b in blk]),
        out_shape=[jax.ShapeDtypeStruct((4,) + a.shape, _MX) for a in ws],
        compiler_params=_cp("arbitrary"),
    )(chip.reshape(1), *ws)


def _position():
    x, y, c = lax.axis_index("x"), lax.axis_index("y"), lax.axis_index("c")
    return x, y, c


def _small_allgather(blk, name, exchange=(), share=(), gather=()):
    m_per, n = blk.shape
    assert not (share and gather)
    share = tuple(share) + tuple(gather)
    ne, ns = len(exchange), len(share)

    def body(*refs):
        x_ref = refs[0]
        out_ref = refs[1 + ne + ns]
        send_sems, recv_sems = refs[2 + 2 * ne + 2 * ns], refs[3 + 2 * ne + 2 * ns]
        extra = refs[4 + 2 * ne + 2 * ns:]
        x, y, c = _position()
        me, sibling = (x, y, c), (x, y, 1 - c)
        chips = [(1 - x, y), (x, 1 - y), (1 - x, 1 - y)]
        if ne:
            exch = _Exchange(refs[1:1 + ne], refs[2 + ne + ns:2 + 2 * ne + ns], extra[0], extra[1])
            exch.start()
        if gather:
            gat = _Gather(refs[1 + ne:1 + ne + ns], refs[2 + 2 * ne + ns:2 + 2 * ne + 2 * ns],
                          extra[2 * bool(ne)], extra[2 * bool(ne) + 1], [b.shape[1] for b in gather])
            gat.start()
        elif ns:
            halves = _share_copies(refs[1 + ne:1 + ne + ns], refs[2 + 2 * ne + ns:2 + 2 * ne + 2 * ns],
                                   extra[2 * bool(ne)], extra[2 * bool(ne) + 1], [b.shape[0] for b in share])
            for cp in halves:
                cp.start()

        def rows(px, py, pc):
            return out_ref.at[pl.ds(pl.multiple_of((4 * px + 2 * py + pc) * m_per, 8), m_per), :]

        def copy(k, block, to, src=None):
            return pltpu.make_async_remote_copy(
                src_ref=rows(*block) if src is None else src, dst_ref=rows(*block),
                send_sem=send_sems.at[k], recv_sem=recv_sems.at[k], device_id=to, device_id_type=MESH)

        out_ref[pl.ds(pl.multiple_of((4 * x + 2 * y + c) * m_per, 8), m_per), :] = x_ref[...]
        first = [copy(0, me, sibling, src=x_ref)]
        first += [copy(1 + j, me, (*chip, c), src=x_ref) for j, chip in enumerate(chips)]
        for cp in first:
            cp.start()
        passed = [copy(4 + j, (*chip, c), sibling) for j, chip in enumerate(chips)]
        for j, chip in enumerate(chips):
            copy(1 + j, (*chip, c), me).wait_recv()
            passed[j].start()
        copy(0, sibling, me).wait_recv()
        for j, chip in enumerate(chips):
            copy(4 + j, (*chip, 1 - c), me).wait_recv()
        for cp in first + passed:
            cp.wait_send()
        if ne:
            exch.finish()
        if gather:
            gat.forward()
            gat.finish()
        elif ns:
            for cp in halves:
                cp.wait()

    res = pl.pallas_call(
        body, name=name,
        out_shape=[jax.ShapeDtypeStruct((8 * m_per, n), blk.dtype)] + _Exchange.out_shapes(exchange)
        + [jax.ShapeDtypeStruct(b.shape, b.dtype) for b in share],
        in_specs=[_vspec()] + _any_specs(ne + ns), out_specs=[_vspec()] + _any_specs(ne + ns),
        input_output_aliases={1 + ne + w: 1 + ne + w for w in range(ns)},
        scratch_shapes=[pltpu.SemaphoreType.DMA((7,)), pltpu.SemaphoreType.DMA((7,))]
        + (_Exchange.semaphores(ne) if ne else [])
        + (_Gather.semaphores(ns) if gather else
           [pltpu.SemaphoreType.DMA((ns,)), pltpu.SemaphoreType.DMA((ns,))] if ns else []),
    )(blk, *exchange, *share)
    return res[0], res[1:1 + ne], res[1 + ne:]


def _share_copies(src_refs, out_refs, send_sems, recv_sems, nrows):
    x, y, c = _position()
    return [pltpu.make_async_remote_copy(
        src_ref=_rows(src_refs[w], c, r), dst_ref=_rows(out_refs[w], c, r), send_sem=send_sems.at[w],
        recv_sem=recv_sems.at[w], device_id=(x, y, 1 - c), device_id_type=MESH) for w, r in enumerate(nrows)]


def _any_specs(n):
    return [pl.BlockSpec(memory_space=pl.ANY)] * n


def _rows(ref, half, nrows):
    return ref.at[pl.ds(half * (nrows // 2), nrows // 2), :]


class _Gather:
    def __init__(self, src_refs, out_refs, send_sems, recv_sems, nrows):
        x, y, c = _position()
        chip = 2 * x + y
        sibling = (x, y, 1 - c)

        def copy(k, src, dst, to):
            return pltpu.make_async_remote_copy(src_ref=src, dst_ref=dst, send_sem=send_sems.at[k],
                                                recv_sem=recv_sems.at[k], device_id=to, device_id_type=MESH)

        self.first, self.first_landed, self.passed, self.passed_landed = [], [], [], []
        for w, r in enumerate(nrows):
            for j, ch in enumerate([(1 - x, y), (x, 1 - y), (1 - x, 1 - y)]):
                theirs = _rows(out_refs[w].at[2 * ch[0] + ch[1]], c, r)
                other = _rows(out_refs[w].at[2 * ch[0] + ch[1]], 1 - c, r)
                self.first.append(copy(6 * w + j, _rows(src_refs[w].at[chip], c, r),
                                       _rows(out_refs[w].at[chip], c, r), (*ch, c)))
                self.first_landed.append(copy(6 * w + j, theirs, theirs, sibling))
                self.passed.append(copy(6 * w + 3 + j, theirs, theirs, sibling))
                self.passed_landed.append(copy(6 * w + 3 + j, other, other, sibling))

    in_place = True

    @staticmethod
    def semaphores(n):
        return [pltpu.SemaphoreType.DMA((6 * n,)), pltpu.SemaphoreType.DMA((6 * n,))]

    @staticmethod
    def results(bufs):
        return [jax.ShapeDtypeStruct(b.shape, b.dtype) for b in bufs]

    def start(self):
        for cp in self.first:
            cp.start()

    def forward(self):
        for landed, cp in zip(self.first_landed, self.passed):
            landed.wait_recv()
            cp.start()

    def finish(self):
        for landed in self.passed_landed:
            landed.wait_recv()
        for cp in self.first + self.passed:
            cp.wait_send()


class _Swap:
    in_place = False

    def __init__(self, grad_refs, out_refs, send_sems, recv_sems, nrows):
        x, y, c = _position()
        self.copies = [pltpu.make_async_remote_copy(
            src_ref=grad_refs[w].at[:, pl.ds((1 - c) * (r // 2), r // 2), :], dst_ref=out_refs[w],
            send_sem=send_sems.at[w], recv_sem=recv_sems.at[w], device_id=(x, y, 1 - c), device_id_type=MESH)
            for w, r in enumerate(nrows)]

    @staticmethod
    def semaphores(n):
        return [pltpu.SemaphoreType.DMA((n,)), pltpu.SemaphoreType.DMA((n,))]

    @staticmethod
    def results(grads):
        return [jax.ShapeDtypeStruct((4, g.shape[1] // 2, g.shape[2]), g.dtype) for g in grads]

    def start(self):
        for cp in self.copies:
            cp.start()

    def forward(self):
        pass

    def finish(self):
        for cp in self.copies:
            cp.wait()


def _grad_swap(grads, name):
    n = len(grads)

    def body(*refs):
        swap = _Swap(refs[:n], refs[n:2 * n], refs[2 * n], refs[2 * n + 1], [g.shape[1] for g in grads])
        swap.start()
        swap.finish()

    return pl.pallas_call(
        body, name=name, out_shape=_Swap.results(grads), in_specs=_any_specs(n), out_specs=_any_specs(n),
        scratch_shapes=_Swap.semaphores(n),
    )(*grads)


class _Exchange:
    in_place = False

    def __init__(self, part_refs, out_refs, send_sems, recv_sems, nrows=None):
        x, y, c = _position()
        self.copies = [
            pltpu.make_async_remote_copy(
                src_ref=part_refs[w].at[2 * ch[0] + ch[1]], dst_ref=out_refs[w].at[k], send_sem=send_sems.at[3 * w + k],
                recv_sem=recv_sems.at[3 * w + k], device_id=(*ch, c), device_id_type=MESH)
            for w in range(len(part_refs)) for k, ch in enumerate([(1 - x, y), (x, 1 - y), (1 - x, 1 - y)])]

    @staticmethod
    def semaphores(n):
        return [pltpu.SemaphoreType.DMA((3 * n,)), pltpu.SemaphoreType.DMA((3 * n,))]

    @staticmethod
    def out_shapes(parts):
        return [jax.ShapeDtypeStruct((3,) + p.shape[1:], p.dtype) for p in parts]

    results = out_shapes

    def start(self):
        for cp in self.copies:
            cp.start()

    def forward(self):
        pass

    def finish(self):
        for cp in self.copies:
            cp.wait()


def _rope_tables(s):
    rows = s // GRID_W
    inv = 10000.0 ** (-jnp.arange(0, 32, 2, dtype=F32) / 32.0)
    ang_r = jnp.arange(rows, dtype=F32)[:, None] * inv
    ang_c = jnp.arange(GRID_W, dtype=F32)[:, None] * inv
    lane = jnp.arange(128)
    first = (lane % 32) < 16
    by_row = (lane % 64) < 32

    def table(fn):
        tr = jnp.tile(fn(ang_r), (1, 8))[:, None, :]
        tc = jnp.tile(fn(ang_c), (1, 8))[None, :, :]
        return jnp.where(by_row, tr, tc).reshape(s, 128)

    cos, sin = table(jnp.cos), table(jnp.sin)
    return cos, jnp.where(first, -sin, 0.0), jnp.where(first, 0.0, sin)


def _local_step(x, ctx, tgt, mod, modc, nw1, win, qnw, knw, sink, cw, cb, cnw, cnb, wout4, nw2, wup, fcw, fcb, wdown4,
                core=None):
    s = x.shape[0]
    sh1, sc1, g1, sh2, sc2, g2 = [mod[:, D * k:D * (k + 1)] for k in range(6)]
    shc, scc = modc[:, :D], modc[:, D:2 * D]
    cos, sa, sb = _rope_tables(s)
    gi = jnp.arange(128) // 64
    gmat = (gi[:, None] == gi[None, :]).astype(_MX)
    qnw2, knw2 = jnp.tile(qnw, (1, 2)), jnp.tile(knw, (1, 2))
    cw32 = jnp.pad(cw, ((0, 1), (0, 0)))
    fcw2 = jnp.pad(fcw, ((0, 5), (0, 0))).reshape(8, 2, FH).transpose(1, 0, 2)
    fcb2 = fcb.reshape(2, 1, FH)
    winkv = win[:, 512:768]
    sinkv = sink.reshape(8)

    host = (lambda *bufs: ()) if core is None else (lambda *bufs: bufs)
    (proj, h, q0, q1, k, ksw, v, vsw, u0), got = _mix_in(x, nw1, sc1, sh1, win, qnw2, knw2, cos, sa, sb, gmat,
                                                        gather=host(wout4))
    wout = (got[0] if got else wout4).reshape(D, D)
    kvc, hc, kc, kcsw, vc, vcsw = _ctx_kv(ctx, nw1, scc, shc, winkv, knw2, gmat)
    padr = lambda a: jnp.pad(a, ((TQ, TQ), (0, 0)))
    kp, kswp, vp, vswp = padr(k), padr(ksw), padr(v), padr(vsw)
    (o, pex, pstat), got = _attn_fwd(q0, q1, kp, kswp, vp, vswp, kc, kcsw, vc, vcsw, sinkv, gather=host(wup))
    wup = got[0] if got else wup
    u1 = _conv31(u0, cw32, cb)
    x1, mix, cat = _mix_out(o, u1, cnw, cnb, wout, x, g1)
    (h2, up0), got = _ffn_up(x1, nw2, sc2, sh2, wup, gather=host(wdown4))
    wdown = (got[0] if got else wdown4).reshape(FH, D)
    act, gv = _ffn_conv_act(up0, fcw2, fcb2)
    dy, dact, ddn, vec_dn = _ffn_down(act, wdown, x1, g2, tgt)
    loss = (0.5 / D) * jnp.sum(vec_dn[1])

    dup, gfc = _ffn_conv_bwd(up0, gv, dact, fcw2)
    g_wdown = _tn_matmul(act, ddn, FQ, D, "gw_down").reshape(4, FH // 4, D)
    g_wup = _tn_matmul(h2, dup, D, FQ, "gw_up", b_split=True, by_chip=True)
    dx1, dmix, vec_up = _ffn_up_bwd(dup, wup, x1, dy, mix, nw2, sc2, g1)
    g_wout = _tn_matmul(cat, dmix, D, D, "gw_out").reshape(4, D // 4, D)
    late = [g_wout, g_wup, g_wdown]
    (do, du1, vec_ln), from_sib = _mix_out_bwd(dmix, wout, u1, cnw, cnb, swap=() if core is None else late)
    parts = () if core is None else _pair_sum(late, from_sib, core, "grad_pair_sum_late")
    (dga, dgb, gcw), arrived_a = _conv31_bwd(du1, u0, cw32, proj, exchange=parts[0:1] + parts[2:3])
    (dq, dkp, dvp, dkc, dvc, dsink), arrived_b = _attn_bwd(q0, q1, kp, kswp, vp, vswp, kc, kcsw, vc, vcsw, pex, pstat,
                                                           o, do, exchange=parts[1:2])
    if core is not None:
        g_wout, g_wup, g_wdown = zip(parts, [arrived_a[0], arrived_b[0], arrived_a[1]])
    gwinkv, vec_ctx = _ctx_bwd(ctx, nw1, scc, winkv, kvc, hc, dkc, dvc, knw2, gmat)
    gx, dproj, vec_in = _mix_in_bwd(dq, dkp[TQ:TQ + s], dvp[TQ:TQ + s], dga, dgb, proj, x, dx1, win, nw1, sc1,
                                    qnw2, knw2, cos, sa, sb, gmat)
    g_win = _tn_matmul(h, dproj, D, INW, "gw_in", init=gwinkv)

    g_qn = vec_in[3:4, 0:128]
    g_kn = vec_in[3:4, 128:256] + vec_ctx[3:4, 0:128]
    grads = dict(
        norm_mix_w=vec_in[2:3] + vec_ctx[2:3], w_in=g_win,
        q_norm_w=g_qn[:, :64] + g_qn[:, 64:], k_norm_w=g_kn[:, :64] + g_kn[:, 64:],
        sink_logit=dsink[:, 0].reshape(1, 8), conv_w=gcw[:CK], conv_b=gcw[CK:CK + 1],
        conv_norm_w=vec_ln[1:2], conv_norm_b=vec_ln[0:1], w_out=g_wout, norm_ffn_w=vec_up[2:3], w_up=g_wup,
        ffn_conv_w=gfc[:, 0:3].transpose(1, 0, 2).reshape(3, 2 * FH), ffn_conv_b=gfc[:, 3].reshape(1, 2 * FH),
        w_down=g_wdown)
    dmod = jnp.concatenate([vec_in[0:1], vec_in[1:2], vec_up[3:4], vec_up[0:1], vec_up[1:2], vec_dn[0:1]], axis=1)
    dmodc = jnp.concatenate([vec_ctx[0:1], vec_ctx[1:2]], axis=1)
    return loss, gx, grads, dmod, dmodc


_SMALL = ["c_ctx", "b_mod", "norm_mix_w", "q_norm_w", "k_norm_w", "sink_logit", "conv_w", "conv_b", "conv_norm_w",
          "conv_norm_b", "norm_ffn_w", "ffn_conv_w", "ffn_conv_b"]
_GATHERED = ["w_in", "w_out", "w_up", "w_down"]
_BIG = ["w_mod"] + _GATHERED
_ORDER = ["c_ctx", "w_mod", "b_mod", "norm_mix_w", "w_in", "q_norm_w", "k_norm_w", "sink_logit", "conv_w", "conv_b",
          "conv_norm_w", "conv_norm_b", "w_out", "norm_ffn_w", "w_up", "ffn_conv_w", "ffn_conv_b", "w_down"]
_PACK = [("norm_mix_w", 1), ("norm_ffn_w", 1), ("q_norm_w", 1), ("k_norm_w", 1), ("sink_logit", 1), ("conv_b", 1),
         ("conv_norm_w", 1), ("conv_norm_b", 1), ("ffn_conv_b", 6), ("conv_w", 16), ("ffn_conv_w", 17)]
_PACK_ROWS = 56


def _pack_rows(a, nrows):
    flat = a.reshape(-1)
    return jnp.pad(flat, (0, nrows * D - flat.shape[0])).reshape(nrows, D)


def kernel(x, c, ctx, c_ctx, w_mod, b_mod, norm_mix_w, w_in, q_norm_w, k_norm_w, sink_logit, conv_w, conv_b, conv_norm_w, conv_norm_b, w_out, norm_ffn_w, w_up, ffn_conv_w, ffn_conv_b, w_down, loss_target, m_c_ctx, m_w_mod, m_b_mod, m_norm_mix_w, m_w_in, m_q_norm_w, m_k_norm_w, m_sink_logit, m_conv_w, m_conv_b, m_conv_norm_w, m_conv_norm_b, m_w_out, m_norm_ffn_w, m_w_up, m_ffn_conv_w, m_ffn_conv_b, m_w_down, v_c_ctx, v_w_mod, v_b_mod, v_norm_mix_w, v_w_in, v_q_norm_w, v_k_norm_w, v_sink_logit, v_conv_w, v_conv_b, v_conv_norm_w, v_conv_norm_b, v_w_out, v_norm_ffn_w, v_w_up, v_ffn_conv_w, v_ffn_conv_b, v_w_down):
    w = dict(c_ctx=c_ctx.reshape(1, D), w_mod=w_mod[0], b_mod=b_mod, norm_mix_w=norm_mix_w, w_in=w_in[0],
             q_norm_w=q_norm_w, k_norm_w=k_norm_w, sink_logit=sink_logit, conv_w=conv_w[0], conv_b=conv_b,
             conv_norm_w=conv_norm_w, conv_norm_b=conv_norm_b, w_out=w_out[0], norm_ffn_w=norm_ffn_w, w_up=w_up[0],
             ffn_conv_w=ffn_conv_w[0], ffn_conv_b=ffn_conv_b, w_down=w_down[0])
    m = dict(c_ctx=m_c_ctx.reshape(1, D), w_mod=m_w_mod[0], b_mod=m_b_mod, norm_mix_w=m_norm_mix_w, w_in=m_w_in[0],
             q_norm_w=m_q_norm_w, k_norm_w=m_k_norm_w, sink_logit=m_sink_logit, conv_w=m_conv_w[0], conv_b=m_conv_b,
             conv_norm_w=m_conv_norm_w, conv_norm_b=m_conv_norm_b, w_out=m_w_out[0], norm_ffn_w=m_norm_ffn_w,
             w_up=m_w_up[0], ffn_conv_w=m_ffn_conv_w[0], ffn_conv_b=m_ffn_conv_b, w_down=m_w_down[0])
    v = dict(c_ctx=v_c_ctx.reshape(1, D), w_mod=v_w_mod[0], b_mod=v_b_mod, norm_mix_w=v_norm_mix_w, w_in=v_w_in[0],
             q_norm_w=v_q_norm_w, k_norm_w=v_k_norm_w, sink_logit=v_sink_logit, conv_w=v_conv_w[0], conv_b=v_conv_b,
             conv_norm_w=v_conv_norm_w, conv_norm_b=v_conv_norm_b, w_out=v_w_out[0], norm_ffn_w=v_norm_ffn_w,
             w_up=v_w_up[0], ffn_conv_w=v_ffn_conv_w[0], ffn_conv_b=v_ffn_conv_b, w_down=v_w_down[0])
    xi, yi, ci = _position()
    chip = 2 * xi + yi
    dev = 2 * chip + ci
    s = x.shape[1]
    ncol = w["w_mod"].shape[1]

    win_buf, wout_buf, wup_buf, wdown_buf = _cast_shards([w[n] for n in _GATHERED], chip)

    blk0 = jnp.concatenate([_pack_rows(w["conv_w"], 4), _pack_rows(w["ffn_conv_w"], 5), c,
                            jnp.zeros((6, D), F32)], axis=0)
    g0, _, (win4,) = _small_allgather(blk0, "gather_c_convw", gather=[win_buf])
    win = win4.transpose(1, 0, 2).reshape(D, INW)
    g0 = g0.reshape(8, 16, D)
    c_all = g0[:, 9, :]
    cs = jnp.concatenate([c_all, w["c_ctx"], jnp.zeros((7, D), F32)], axis=0)
    cw_full = jnp.concatenate([g0[2 * j, 0:4].reshape(-1)[:CK * 128].reshape(CK, 128) for j in range(4)], axis=1)
    fcw_full = jnp.concatenate([g0[2 * j, 4:9].reshape(-1)[:3 * 1408].reshape(3, 1408) for j in range(4)], axis=1)

    b_loc = lax.dynamic_slice(w["b_mod"], (0, chip * ncol), (1, ncol))
    modp = _mod_fwd(cs, w["w_mod"], b_loc)
    gm = _small_allgather(modp, "gather_mod")[0].reshape(8, 16, ncol)
    mod_all = jnp.concatenate([gm[2 * j] for j in range(4)], axis=1)
    mod = lax.dynamic_slice(mod_all, (dev, 0), (1, 6 * D))
    modc = mod_all[8:9]

    loss_loc, gx, gl, dmod, dmodc = _local_step(
        x[0], ctx[0], loss_target[0], mod, modc, w["norm_mix_w"], win, w["q_norm_w"], w["k_norm_w"], w["sink_logit"],
        cw_full, w["conv_b"], w["conv_norm_w"], w["conv_norm_b"], wout_buf, w["norm_ffn_w"], wup_buf, fcw_full,
        w["ffn_conv_b"], wdown_buf, core=ci)

    last = [gl["w_in"].reshape(D, 4, INW // 4).transpose(1, 0, 2)]
    parts = _pair_sum(last, _grad_swap(last, "grad_swap_in"), ci, "grad_pair_sum_in")

    pack = jnp.concatenate([dmod.reshape(6, D), dmodc.reshape(2, D)]
                           + [_pack_rows(gl[name], nr) for name, nr in _PACK], axis=0)
    pack = jnp.pad(pack, ((0, _PACK_ROWS - pack.shape[0]), (0, 0)))
    last_row = lax.broadcasted_iota(jnp.int32, pack.shape, 0) == _PACK_ROWS - 1
    pack = jnp.where(last_row, loss_loc, pack)
    gp, arrived, _ = _small_allgather(pack, "gather_small_grads", exchange=parts)
    gp = gp.reshape(8, _PACK_ROWS, D)
    late = [gl[n] for n in ("w_out", "w_up", "w_down")]
    shard_grads = _chip_sum([*parts, *[p for p, _ in late]], [*arrived, *[a for _, a in late]], chip, ci)
    tot = _sum_leading(gp, "sum_small_grads")
    loss = tot[_PACK_ROWS - 1, 0]
    g = {}
    r = 8
    for name, nr in _PACK:
        shape = gl[name].shape
        g[name] = tot[r:r + nr].reshape(-1)[:math.prod(shape)].reshape(shape)
        r += nr
    dmod_all = jnp.concatenate([gp[:, 0:6].reshape(8, 6 * D),
                                jnp.pad(tot[6:8].reshape(1, 2 * D), ((0, 0), (0, 4 * D))),
                                jnp.zeros((7, 6 * D), F32)], axis=0)
    g["b_mod"] = _sum_leading(dmod_all.reshape(16, 1, 6 * D), "grad_b_mod")
    g["w_mod"], part = _mod_bwd(cs, lax.dynamic_slice(dmod_all, (0, chip * ncol), (16, ncol)), w["w_mod"])
    cparts, _, (g["w_in"], g["w_out"], g["w_up"], g["w_down"]) = _small_allgather(part, "gather_cctx",
                                                                                 share=shard_grads)
    g["c_ctx"] = _cctx_grad(cparts.reshape(8, 16, D)[0::2], w["c_ctx"])
    g["conv_w"] = lax.dynamic_slice(g["conv_w"], (0, chip * 128), (CK, 128))
    g["ffn_conv_w"] = lax.dynamic_slice(g["ffn_conv_w"], (0, chip * 1408), (3, 1408))

    delta, new_m, new_v = {}, {}, {}
    for name in _BIG:
        delta[name], new_m[name], new_v[name] = _adam_big(w[name], g[name], m[name], v[name], "adam_" + name)
    ds, ms, vs = _adam_small([w[n] for n in _SMALL], [g[n] for n in _SMALL], [m[n] for n in _SMALL],
                             [v[n] for n in _SMALL])
    for k, name in enumerate(_SMALL):
        delta[name], new_m[name], new_v[name] = ds[k], ms[k], vs[k]

    def shaped(d, name):
        a = d[name]
        if name == "c_ctx":
            return a.reshape(D)
        if name in ("w_mod", "w_in", "w_out", "w_up", "w_down", "conv_w", "ffn_conv_w"):
            return a[None]
        return a

    outs = [loss, gx[None]]
    for d in (g, delta, new_m, new_v):
        outs += [shaped(d, name) for name in _ORDER]
    return tuple(outs)
```
